```python
import math
import jax, jax.numpy as jnp
from jax import lax
import numpy as np

D_MODEL = 1024
BATCH = 2
SEQ = 8192
DEPTH = 1

D_MIX = D_MODEL
D_A = D_MIX // 2
D_B = D_MIX // 2
BLK = 128
A_HEADS = 4
A_HD = D_A // A_HEADS
HEAD_DIM = 64
N_HEADS = D_B // HEAD_DIM
N_KV = 2
GQA = N_HEADS // N_KV
WINDOW = 128
NUM_BUCKETS = 32
MAX_DIST = 128
D_IN = 2 * D_A + D_B + 2 * N_KV * HEAD_DIM
N_GROUPS = 4
E_PER_GROUP = 8
N_EXPERTS = N_GROUPS * E_PER_GROUP
TOP_K = 2
D_FF_E = 256
PLE_DIM = 256
EPS = 1e-6

kernel_name = "hymba_gmlp_swa_hmoe_layer"


def rmsnorm(x, g):
    x32 = x.astype(jnp.float32)
    y = x32 * lax.rsqrt(jnp.mean(x32 * x32, axis=-1, keepdims=True) + EPS)
    return (y * g.astype(jnp.float32)).astype(x.dtype)


def layernorm(x, g, b):
    x32 = x.astype(jnp.float32)
    mu = jnp.mean(x32, axis=-1, keepdims=True)
    var = jnp.mean(jnp.square(x32 - mu), axis=-1, keepdims=True)
    y = (x32 - mu) * lax.rsqrt(var + EPS)
    return (y * g.astype(jnp.float32) + b.astype(jnp.float32)).astype(x.dtype)


def t5_bucket(rel):
    n = NUM_BUCKETS // 2
    max_exact = n // 2
    ret = jnp.where(rel > 0, n, 0)
    a = jnp.abs(rel)
    large = max_exact + (jnp.log(jnp.maximum(a, 1).astype(jnp.float32) / max_exact)
                         / math.log(MAX_DIST / max_exact) * (n - max_exact)).astype(jnp.int32)
    large = jnp.minimum(large, n - 1)
    return ret + jnp.where(a < max_exact, a, large)


def gmlp_group(u, v, ln_g, ln_b, w_s, b_s):
    B, S, _ = v.shape
    nc = S // BLK
    v = layernorm(v, ln_g, ln_b).reshape(B, nc, BLK, A_HEADS, A_HD)
    sv = jnp.einsum("hij,bcjhd->bcihd", w_s, v) + b_s.T[:, :, None]
    return u * sv.reshape(B, S, D_A)


def band(t):
    B, S = t.shape[:2]
    nb = S // BLK
    tp = jnp.pad(t, ((0, 0), (BLK, BLK), (0, 0), (0, 0)))
    tb = tp.reshape(B, nb + 2, BLK, N_KV, HEAD_DIM)
    return jnp.concatenate([tb[:, :-2], tb[:, 1:-1], tb[:, 2:]], axis=2)


def windowed_gqa(q, k, v, sink, bias_table):
    B, S, _ = q.shape
    nb = S // BLK
    q = q.reshape(B, nb, BLK, N_KV, GQA, HEAD_DIM)
    kb = band(k.reshape(B, S, N_KV, HEAD_DIM))
    vb = band(v.reshape(B, S, N_KV, HEAD_DIM))
    s = jnp.einsum("bnqkgd,bnjkd->bnkgqj", q, kb).astype(jnp.float32) * (HEAD_DIM ** -0.5)
    i = jnp.arange(BLK, dtype=jnp.int32)[:, None]
    j = jnp.arange(3 * BLK, dtype=jnp.int32)[None, :]
    rel = j - BLK - i
    bias = bias_table.astype(jnp.float32)[t5_bucket(rel)]
    bias = jnp.transpose(bias, (2, 0, 1)).reshape(N_KV, GQA, BLK, 3 * BLK)
    kpos = jnp.arange(nb, dtype=jnp.int32)[:, None, None] * BLK - BLK + j[None]
    valid = (jnp.abs(rel)[None] <= WINDOW) & (kpos >= 0) & (kpos < S)
    s = jnp.where(valid[None, :, None, None], s + bias, -1e30)
    sk = sink.astype(jnp.float32).reshape(N_KV, GQA, 1, 1)
    m = jnp.maximum(jnp.max(s, axis=-1, keepdims=True), sk)
    e = jnp.exp(s - m)
    denom = jnp.sum(e, axis=-1, keepdims=True) + jnp.exp(sk - m)
    pr = (e / denom).astype(vb.dtype)
    o = jnp.einsum("bnkgqj,bnjkd->bnqkgd", pr, vb)
    return o.reshape(B, S, N_HEADS * HEAD_DIM)


def hier_moe(xt, w_rg, b_rg, w_re, b_re, w_gate, w_up, w_down):
    T = xt.shape[0]
    rows = jnp.arange(T)
    pg = jax.nn.softmax((xt @ w_rg + b_rg).astype(jnp.float32), axis=-1)
    g_idx = jnp.argmax(pg, axis=-1)
    pg_top = pg[rows, g_idx]
    le = (xt @ w_re + b_re).astype(jnp.float32).reshape(T, N_GROUPS, E_PER_GROUP)
    pe = jax.nn.softmax(le[rows, g_idx], axis=-1)
    top_v, top_i = lax.top_k(pe, TOP_K)
    w = pg_top[:, None] * top_v / jnp.sum(top_v, axis=-1, keepdims=True)
    eidx = g_idx[:, None] * E_PER_GROUP + top_i
    comb = jnp.zeros((T, N_EXPERTS), jnp.float32).at[rows[:, None], eidx].add(w).astype(xt.dtype)
    y = jnp.zeros_like(xt)
    for e in range(N_EXPERTS):
        hdn = jax.nn.silu(xt @ w_gate[e]) * (xt @ w_up[e])
        y = y + comb[:, e:e + 1] * (hdn @ w_down[e])
    return y


def setup_inputs(seed: int = 0) -> dict:
    key = jax.random.key(seed)
    ks = jax.random.split(key, 32)

    def nrm(k, shape, scale):
        return jax.random.normal(k, shape, jnp.float32) * scale

    def gain(k, shape):
        return 1.0 + nrm(k, shape, 0.02)

    L = DEPTH
    return {
        "x": nrm(ks[0], (BATCH, SEQ, D_MODEL), 1.0),
        "p": nrm(ks[1], (DEPTH, BATCH, SEQ, PLE_DIM), 1.0),
        "rel_bias": nrm(ks[2], (NUM_BUCKETS, N_HEADS), 0.5),
        "g_mix": gain(ks[3], (L, D_MODEL)),
        "w_in": nrm(ks[4], (L, D_MODEL, D_IN), D_MODEL ** -0.5),
        "ln_v_g": gain(ks[5], (L, D_A)),
        "ln_v_b": nrm(ks[6], (L, D_A), 0.02),
        "w_spatial": nrm(ks[7], (L, A_HEADS, BLK, BLK), BLK ** -0.5),
        "b_spatial": 1.0 + nrm(ks[8], (L, A_HEADS, BLK), 0.1),
        "sink": nrm(ks[9], (L, N_HEADS), 0.5),
        "g_out_grp": gain(ks[10], (L, D_MIX)),
        "w_out": nrm(ks[11], (L, D_MIX, D_MODEL), D_MIX ** -0.5),
        "g_ffn": gain(ks[12], (L, D_MODEL)),
        "w_router_group": nrm(ks[13], (L, D_MODEL, N_GROUPS), D_MODEL ** -0.5),
        "b_router_group": nrm(ks[14], (L, N_GROUPS), 0.01),
        "w_router_expert": nrm(ks[15], (L, D_MODEL, N_EXPERTS), D_MODEL ** -0.5),
        "b_router_expert": nrm(ks[16], (L, N_EXPERTS), 0.01),
        "w_gate_e": nrm(ks[17], (L, N_EXPERTS, D_MODEL, D_FF_E), D_MODEL ** -0.5),
        "w_up_e": nrm(ks[18], (L, N_EXPERTS, D_MODEL, D_FF_E), D_MODEL ** -0.5),
        "w_down_e": nrm(ks[19], (L, N_EXPERTS, D_FF_E, D_MODEL), D_FF_E ** -0.5),
        "w_ple_proj": nrm(ks[20], (L, PLE_DIM, D_MODEL), PLE_DIM ** -0.5),
        "g_ple": gain(ks[21], (L, D_MODEL)),
        "w_ple_gate": nrm(ks[22], (L, D_MODEL, D_MODEL), D_MODEL ** -0.5),
        "b_ple_gate": nrm(ks[23], (L, D_MODEL), 0.02),
        "g_final": gain(ks[24], (D_MODEL,)),
    }


def reference(x, p, rel_bias, g_mix, w_in, ln_v_g, ln_v_b, w_spatial, b_spatial, sink,
              g_out_grp, w_out, g_ffn, w_router_group, b_router_group, w_router_expert,
              b_router_expert, w_gate_e, w_up_e, w_down_e, w_ple_proj, g_ple, w_ple_gate,
              b_ple_gate, g_final):
    B, S, D = x.shape
    h = x
    c0, c1, c2 = D_A, 2 * D_A, 2 * D_A + D_B
    c3 = c2 + N_KV * HEAD_DIM
    for i in range(DEPTH):
        a = rmsnorm(h, g_mix[i])
        z = a @ w_in[i]
        uv = jax.nn.gelu(z[..., :c1])
        y_a = gmlp_group(uv[..., :c0], uv[..., c0:c1], ln_v_g[i], ln_v_b[i],
                         w_spatial[i], b_spatial[i])
        y_b = windowed_gqa(z[..., c1:c2], z[..., c2:c3], z[..., c3:], sink[i], rel_bias)
        y = jnp.concatenate([rmsnorm(y_a, g_out_grp[i, :D_A]),
                             rmsnorm(y_b, g_out_grp[i, D_A:])], axis=-1)
        h = h + y @ w_out[i]
        m = rmsnorm(h, g_ffn[i]).reshape(B * S, D)
        h = h + hier_moe(m, w_router_group[i], b_router_group[i], w_router_expert[i],
                         b_router_expert[i], w_gate_e[i], w_up_e[i], w_down_e[i]).reshape(B, S, D)
        gate = jax.nn.sigmoid((h @ w_ple_gate[i] + b_ple_gate[i]).astype(jnp.float32)).astype(h.dtype)
        h = h + gate * rmsnorm(p[i] @ w_ple_proj[i], g_ple[i])
    return rmsnorm(h, g_final)
```

```python
import functools
import math

import jax
import jax.numpy as jnp
import numpy as np
from jax import lax
from jax.experimental import pallas as pl
from jax.experimental.pallas import tpu as pltpu

D_MODEL = 1024
D_A = 512
D_B = 512
BLK = 128
A_HEADS = 4
HEAD_DIM = 64
N_HEADS = 8
N_KV = 2
GQA = 4
WINDOW = 128
NUM_BUCKETS = 32
MAX_DIST = 128
D_IN = 2 * D_A + D_B + 2 * N_KV * HEAD_DIM
N_GROUPS = 4
E_PER_GROUP = 8
N_EXPERTS = 32
D_FF_E = 256
PLE_DIM = 256
EPS = 1e-6
NEG = -1e30

TM_MIX = 512
TQ_ATT = 512
TR_POS = 1024
TS_SCT = 512
R_EXP = 256
TK_OUT = 512
ROUTE_ROWS = 8 + N_EXPERTS
VMEM_LIMIT = 48 * 1024 * 1024

f32 = jnp.float32
bf16 = jnp.bfloat16
i32 = jnp.int32
u32 = jnp.uint32


def _rms(x, g):
    return x * lax.rsqrt(jnp.mean(x * x, axis=-1, keepdims=True) + EPS) * g


def _gelu_tanh(x):
    c = math.sqrt(2.0 / math.pi)
    return x * (0.5 * (1.0 + jnp.tanh(c * (x + 0.044715 * (x * x * x)))))


ROW_SUB = D_MODEL // BLK


def _store_rows(ref, val):
    n = val.shape[0]
    for j in range(ROW_SUB):
        ref[pl.ds(j, n, stride=ROW_SUB), :] = val[:, j * BLK:(j + 1) * BLK]


def _load_rows(ref, n):
    return jnp.concatenate([ref[pl.ds(j, n, stride=ROW_SUB), :] for j in range(ROW_SUB)], axis=1)


def _bucket_table():
    n = NUM_BUCKETS // 2
    max_exact = n // 2
    i = np.arange(BLK)[:, None]
    j = np.arange(3 * BLK)[None, :]
    rel = j - BLK - i
    ret = np.where(rel > 0, n, 0)
    a = np.abs(rel)
    large = max_exact + (np.log(np.maximum(a, 1).astype(np.float64) / max_exact)
                         / math.log(MAX_DIST / max_exact) * (n - max_exact)).astype(np.int32)
    large = np.minimum(large, n - 1)
    return (ret + np.where(a < max_exact, a, large)).astype(np.int32)


def _bias_body(rb_ref, bucket_ref, o_ref):
    bucket = bucket_ref[...]
    o_ref[...] = jnp.zeros(o_ref.shape, f32)

    def step(b, carry):
        hit = bucket == b
        for h in range(N_HEADS):
            o_ref[h] = jnp.where(hit, rb_ref[b, h], o_ref[h])
        return carry

    lax.fori_loop(0, NUM_BUCKETS, step, 0)


def _bias_table(rel_bias):
    bucket = jnp.asarray(_bucket_table())
    return pl.pallas_call(
        _bias_body,
        out_shape=jax.ShapeDtypeStruct((N_HEADS, BLK, 3 * BLK), f32),
        in_specs=[pl.BlockSpec(memory_space=pltpu.SMEM),
                  pl.BlockSpec((BLK, 3 * BLK), lambda: (0, 0))],
        out_specs=pl.BlockSpec((N_HEADS, BLK, 3 * BLK), lambda: (0, 0, 0)),
        name="bias_table",
    )(rel_bias.astype(f32), bucket)


def _mix_in_body(x_ref, gmix_ref, win_ref, lng_ref, lnb_ref, ws_ref, bs_ref, gout_ref,
                 ya_ref, q_ref, k_ref, v_ref):
    tm = x_ref.shape[0]
    nc = tm // BLK
    a = _rms(x_ref[...], gmix_ref[...])
    z = jnp.dot(a.astype(bf16), win_ref[...], preferred_element_type=f32)
    uv = _gelu_tanh(z[:, :2 * D_A])
    u = uv[:, :D_A]
    v = uv[:, D_A:]
    mu = jnp.mean(v, axis=-1, keepdims=True)
    vc = v - mu
    var = jnp.mean(vc * vc, axis=-1, keepdims=True)
    vn = (vc * lax.rsqrt(var + EPS) * lng_ref[...] + lnb_ref[...]).astype(bf16)
    cols = []
    for h in range(A_HEADS):
        rhs = jnp.concatenate(
            [vn[c * BLK:(c + 1) * BLK, h * BLK:(h + 1) * BLK] for c in range(nc)], axis=1)
        r = jnp.dot(ws_ref[h], rhs, preferred_element_type=f32)
        cols.append(jnp.concatenate(
            [r[:, c * BLK:(c + 1) * BLK] + bs_ref[h] for c in range(nc)], axis=0))
    sv = jnp.concatenate(cols, axis=1)
    ya_ref[...] = _rms(u * sv, gout_ref[...]).astype(bf16)
    q_ref[...] = z[:, 2 * D_A:2 * D_A + D_B].astype(bf16)
    k_ref[...] = z[:, 2 * D_A + D_B:2 * D_A + D_B + BLK].astype(bf16)
    v_ref[...] = z[:, 2 * D_A + D_B + BLK:].astype(bf16)


def _mix_in(x2, gmix, win, lng, lnb, ws, bs, gout_a):
    t = x2.shape[0]
    tm = TM_MIX
    const = lambda shape: pl.BlockSpec(shape, lambda i: (0,) * len(shape))
    return pl.pallas_call(
        _mix_in_body,
        grid=(t // tm,),
        out_shape=(jax.ShapeDtypeStruct((t, D_A), bf16),
                   jax.ShapeDtypeStruct((t, D_B), bf16),
                   jax.ShapeDtypeStruct((t, BLK), bf16),
                   jax.ShapeDtypeStruct((t, BLK), bf16)),
        in_specs=[pl.BlockSpec((tm, D_MODEL), lambda i: (i, 0)),
                  const((1, D_MODEL)), const((D_MODEL, D_IN)),
                  const((1, D_A)), const((1, D_A)),
                  const((A_HEADS, BLK, BLK)), const((A_HEADS, BLK, BLK)),
                  const((1, D_A))],
        out_specs=(pl.BlockSpec((tm, D_A), lambda i: (i, 0)),
                   pl.BlockSpec((tm, D_B), lambda i: (i, 0)),
                   pl.BlockSpec((tm, BLK), lambda i: (i, 0)),
                   pl.BlockSpec((tm, BLK), lambda i: (i, 0))),
        compiler_params=pltpu.CompilerParams(
            dimension_semantics=("parallel",), vmem_limit_bytes=VMEM_LIMIT),
        name="mix_in",
    )(x2, gmix, win, lng, lnb, ws, bs, gout_a)


def _attn_out_body(sink_ref, x_ref, ya_ref, q_ref, kp_ref, km_ref, kn_ref, vp_ref, vm_ref, vn_ref,
                   bias_ref, goutb_ref, wout_ref, gffn_ref, wrh_ref, wrl_ref, br_ref,
                   h_ref, m_ref, re_ref, rw_ref,
                   kf_ref, vf_ref, e_ref, yb_ref, *, seq_len):
    tq = x_ref.shape[1]
    nb = tq // BLK
    ti = pl.program_id(1)

    kf_ref[0:BLK] = kp_ref[0]
    kf_ref[BLK:BLK + tq] = km_ref[0]
    kf_ref[BLK + tq:] = kn_ref[0]
    vf_ref[0:BLK] = vp_ref[0]
    vf_ref[BLK:BLK + tq] = vm_ref[0]
    vf_ref[BLK + tq:] = vn_ref[0]

    row = lax.broadcasted_iota(i32, (BLK, 3 * BLK), 0)
    col = lax.broadcasted_iota(i32, (BLK, 3 * BLK), 1)
    band = jnp.abs(col - BLK - row) <= WINDOW
    lane = lax.broadcasted_iota(i32, (BLK, BLK), 1)
    low_half = lane < HEAD_DIM

    def block(n, carry):
        r0 = pl.multiple_of(n * BLK, BLK)
        qb = q_ref[0, pl.ds(r0, BLK), :]
        kb = kf_ref[pl.ds(r0, 3 * BLK), :]
        vb = vf_ref[pl.ds(r0, 3 * BLK), :]
        zero = jnp.zeros((BLK, BLK), bf16)
        lhs = []
        for kh in range(N_KV):
            for g in range(GQA):
                tile = qb[:, g * BLK:(g + 1) * BLK]
                lhs.append(jnp.where(low_half, tile, zero) if kh == 0
                           else jnp.where(low_half, zero, tile))
        lhs = jnp.concatenate(lhs, axis=0)
        s_all = lax.dot_general(lhs, kb, (((1,), (1,)), ((), ())),
                                preferred_element_type=f32)
        kpos = col + ((ti * nb + n - 1) * BLK)
        valid = band & (kpos >= 0) & (kpos < seq_len)
        rden = []
        for h in range(N_HEADS):
            s = jnp.where(valid, s_all[h * BLK:(h + 1) * BLK] + bias_ref[h], NEG)
            sk = sink_ref[h]
            mrow = jnp.maximum(jnp.max(s, axis=-1, keepdims=True), sk)
            e = jnp.exp(s - mrow)
            den = jnp.sum(e, axis=-1, keepdims=True) + jnp.exp(sk - mrow)
            rden.append(1.0 / den)
            e_ref[h * BLK:(h + 1) * BLK, :] = e.astype(bf16)
        pv = jnp.dot(e_ref[...], vb, preferred_element_type=f32)
        for g in range(GQA):
            o0 = pv[g * BLK:(g + 1) * BLK] * rden[g]
            o1 = pv[(GQA + g) * BLK:(GQA + g + 1) * BLK] * rden[GQA + g]
            yb_ref[pl.ds(r0, BLK), g * BLK:(g + 1) * BLK] = jnp.where(low_half, o0, o1)
        return carry

    lax.fori_loop(0, nb, block, 0)

    ybn = _rms(yb_ref[...], goutb_ref[...]).astype(bf16)
    y = jnp.concatenate([ya_ref[0], ybn], axis=1)
    h = x_ref[0] + jnp.dot(y, wout_ref[...], preferred_element_type=f32)
    h_ref[0] = h
    m = _rms(h, gffn_ref[...])
    _store_rows(m_ref, m)

    m_hi = m.astype(bf16)
    m_lo = (m - m_hi.astype(f32)).astype(bf16)
    nt = (((1,), (1,)), ((), ()))
    logit = (lax.dot_general(wrh_ref[...], m_hi, nt, preferred_element_type=f32)
             + lax.dot_general(wrh_ref[...], m_lo, nt, preferred_element_type=f32)
             + lax.dot_general(wrl_ref[...], m_hi, nt, preferred_element_type=f32)
             + br_ref[...])
    sub = lax.broadcasted_iota(i32, (8, tq), 0)
    lg = logit[0:8]
    mg = jnp.max(lg, axis=0, keepdims=True)
    pg_top = 1.0 / jnp.sum(jnp.exp(lg - mg), axis=0, keepdims=True)
    g_idx = jnp.min(jnp.where(lg == mg, sub, 8), axis=0, keepdims=True)
    sel = logit[8:16]
    for g in range(1, N_GROUPS):
        sel = jnp.where(g_idx == g, logit[8 + 8 * g:16 + 8 * g], sel)
    m1 = jnp.max(sel, axis=0, keepdims=True)
    i1 = jnp.min(jnp.where(sel == m1, sub, 8), axis=0, keepdims=True)
    sel2 = jnp.where(sub == i1, -jnp.inf, sel)
    m2 = jnp.max(sel2, axis=0, keepdims=True)
    i2 = jnp.min(jnp.where(sel2 == m2, sub, 8), axis=0, keepdims=True)
    r = jnp.exp(m2 - m1)
    w1 = pg_top / (1.0 + r)
    w2 = pg_top * r / (1.0 + r)
    e1 = g_idx * E_PER_GROUP + i1
    e2 = g_idx * E_PER_GROUP + i2
    re_ref[...] = jnp.where(sub == 0, e1, jnp.where(sub == 1, e2, 0))
    rw_ref[...] = jnp.where(sub == 0, w1, jnp.where(sub == 1, w2, 0.0))


def _attn_out(sink, x, ya, q, k, v, bias, goutb, wout, gffn, wrh, wrl, br):
    b, s, _ = x.shape
    tq = TQ_ATT
    nb = tq // BLK
    nblk = s // BLK
    t = b * s
    nq = s // tq
    const = lambda shape: pl.BlockSpec(shape, lambda bi, i, *_: (0,) * len(shape))
    tok = lambda w: pl.BlockSpec((1, tq, w), lambda bi, i, *_: (bi, i, 0))
    prev = pl.BlockSpec((1, BLK, BLK), lambda bi, i, *_: (bi, jnp.maximum(i * nb - 1, 0), 0))
    nxt = pl.BlockSpec((1, BLK, BLK), lambda bi, i, *_: (bi, jnp.minimum(i * nb + nb, nblk - 1), 0))
    rows = pl.BlockSpec((tq * ROW_SUB, BLK), lambda bi, i, *_: (bi * nq + i, 0))
    lanes = lambda rows: pl.BlockSpec((rows, tq), lambda bi, i, *_: (0, bi * nq + i))
    grid_spec = pltpu.PrefetchScalarGridSpec(
        num_scalar_prefetch=1,
        grid=(b, nq),
        in_specs=[tok(D_MODEL), tok(D_A), tok(D_B),
                  prev, tok(BLK), nxt, prev, tok(BLK), nxt,
                  const((N_HEADS, BLK, 3 * BLK)), const((1, D_B)), const((D_MODEL, D_MODEL)),
                  const((1, D_MODEL)), const((ROUTE_ROWS, D_MODEL)), const((ROUTE_ROWS, D_MODEL)),
                  const((ROUTE_ROWS, tq))],
        out_specs=(tok(D_MODEL), rows, lanes(8), lanes(8)),
        scratch_shapes=[pltpu.VMEM((tq + 2 * BLK, BLK), bf16),
                        pltpu.VMEM((tq + 2 * BLK, BLK), bf16),
                        pltpu.VMEM((N_HEADS * BLK, 3 * BLK), bf16),
                        pltpu.VMEM((tq, D_B), f32)])
    return pl.pallas_call(
        functools.partial(_attn_out_body, seq_len=s),
        grid_spec=grid_spec,
        out_shape=(jax.ShapeDtypeStruct((b, s, D_MODEL), f32),
                   jax.ShapeDtypeStruct((t * ROW_SUB, BLK), f32),
                   jax.ShapeDtypeStruct((8, t), i32),
                   jax.ShapeDtypeStruct((8, t), f32)),
        compiler_params=pltpu.CompilerParams(
            dimension_semantics=("parallel", "parallel"), vmem_limit_bytes=VMEM_LIMIT),
        name="attn_out",
    )(sink, x, ya, q, k, k, k, v, v, v, bias, goutb, wout, gffn, wrh, wrl, br)


def _route_pos_body(re_ref, pos_ref, offs_ref, cnt_ref, carry_ref, tri_ref, *, rows_per_tile):
    phase = pl.program_id(0)
    ti = pl.program_id(1)
    tr = re_ref.shape[1]
    log_r = rows_per_tile.bit_length() - 1
    eid = lax.broadcasted_iota(i32, (N_EXPERTS, tr), 0)
    e1 = re_ref[0:1, :]
    e2 = re_ref[1:2, :]
    hit1 = eid == e1
    hit2 = eid == e2
    onehot = jnp.where(hit1 | hit2, 1.0, 0.0)
    tile_cnt = jnp.sum(onehot, axis=1, keepdims=True)

    @pl.when((phase == 0) & (ti == 0))
    def _():
        cnt_ref[...] = jnp.zeros(cnt_ref.shape, f32)
        a = lax.broadcasted_iota(i32, (tr, tr), 0)
        c = lax.broadcasted_iota(i32, (tr, tr), 1)
        tri_ref[...] = jnp.where(a < c, 1.0, 0.0).astype(bf16)

    @pl.when(phase == 0)
    def _():
        cnt_ref[...] = cnt_ref[...] + tile_cnt

    @pl.when((phase == 1) & (ti == 0))
    def _():
        cnt = cnt_ref[...].astype(i32)
        padded = ((cnt + (rows_per_tile - 1)) >> log_r) << log_r
        p_hi = (padded >> 8).astype(f32).astype(bf16)
        p_lo = (padded & 255).astype(f32).astype(bf16)
        a = lax.broadcasted_iota(i32, (N_EXPERTS, N_EXPERTS), 0)
        c = lax.broadcasted_iota(i32, (N_EXPERTS, N_EXPERTS), 1)
        low = jnp.where(c < a, 1.0, 0.0).astype(bf16)
        off = (jnp.dot(low, p_hi, preferred_element_type=f32) * 256.0
               + jnp.dot(low, p_lo, preferred_element_type=f32))
        carry_ref[...] = off
        offs_ref[0] = off.astype(i32)
        offs_ref[1] = padded

    @pl.when(phase == 1)
    def _():
        before = jnp.dot(onehot.astype(bf16), tri_ref[...], preferred_element_type=f32)
        slot = before + carry_ref[:, 0:1]
        p1 = jnp.sum(jnp.where(hit1, slot, 0.0), axis=0, keepdims=True)
        p2 = jnp.sum(jnp.where(hit2, slot, 0.0), axis=0, keepdims=True)
        sub = lax.broadcasted_iota(i32, (8, tr), 0)
        pos_ref[...] = jnp.where(sub == 0, p1.astype(i32), jnp.where(sub == 1, p2.astype(i32), 0))
        carry_ref[...] = carry_ref[...] + tile_cnt


def _route_pos(route_e, rows_per_tile):
    t = route_e.shape[1]
    tr = TR_POS
    return pl.pallas_call(
        functools.partial(_route_pos_body, rows_per_tile=rows_per_tile),
        grid=(2, t // tr),
        out_shape=(jax.ShapeDtypeStruct((8, t), i32),
                   jax.ShapeDtypeStruct((2, N_EXPERTS, BLK), i32)),
        in_specs=[pl.BlockSpec((8, tr), lambda p, i: (0, i))],
        out_specs=(pl.BlockSpec((8, tr), lambda p, i: (0, i * p)),
                   pl.BlockSpec((2, N_EXPERTS, BLK), lambda p, i: (0, 0, 0))),
        scratch_shapes=[pltpu.VMEM((N_EXPERTS, BLK), f32),
                        pltpu.VMEM((N_EXPERTS, BLK), f32),
                        pltpu.VMEM((tr, tr), bf16)],
        compiler_params=pltpu.CompilerParams(
            dimension_semantics=("arbitrary", "arbitrary"), vmem_limit_bytes=VMEM_LIMIT),
        name="route_pos",
    )(route_e)


def _scatter_body(ends_ref, padded_ref, pos_ref, m_ref, xs_ref, zero_ref, sem_ref, zsem_ref,
                  *, rows_per_tile):
    ts = m_ref.shape[0] // ROW_SUB
    tile_rows = rows_per_tile * ROW_SUB

    n_tiles = xs_ref.shape[0] // tile_rows
    n_used = ends_ref[N_EXPERTS - 1] // rows_per_tile

    def zero_tile(tile):
        start = pl.multiple_of(tile * tile_rows, tile_rows)
        return pltpu.make_async_copy(zero_ref, xs_ref.at[pl.ds(start, tile_rows), :], zsem_ref)

    @pl.when(pl.program_id(0) == 0)
    def _():
        zero_ref[...] = jnp.zeros(zero_ref.shape, f32)
        for wait in (False, True):
            for e in range(N_EXPERTS):
                @pl.when(padded_ref[e] > 0)
                def _():
                    cp = zero_tile(ends_ref[e] // rows_per_tile - 1)
                    cp.wait() if wait else cp.start()

                @pl.when(n_used + e < n_tiles)
                def _():
                    cp = zero_tile(n_used + e)
                    cp.wait() if wait else cp.start()

    def row_copy(r, dst):
        src_rows = m_ref.at[pl.ds(pl.multiple_of(r * ROW_SUB, ROW_SUB), ROW_SUB), :]
        dst_rows = xs_ref.at[pl.ds(pl.multiple_of(dst * ROW_SUB, ROW_SUB), ROW_SUB), :]
        return pltpu.make_async_copy(src_rows, dst_rows, sem_ref)

    def issue(r, carry):
        row_copy(r, pos_ref[0, 0, r]).start()
        row_copy(r, pos_ref[0, 0, ts + r]).start()
        return carry

    lax.fori_loop(0, ts, issue, 0, unroll=8)

    def drain(r, carry):
        row_copy(0, 0).wait()
        row_copy(0, 0).wait()
        return carry

    lax.fori_loop(0, ts, drain, 0, unroll=8)


def _scatter(ends, padded, pos_tiles, m_rows, total_rows, rows_per_tile):
    t = m_rows.shape[0] // ROW_SUB
    ts = TS_SCT
    grid_spec = pltpu.PrefetchScalarGridSpec(
        num_scalar_prefetch=2,
        grid=(t // ts,),
        in_specs=[pl.BlockSpec((1, 1, 2 * ts), lambda i, *_: (i, 0, 0), memory_space=pltpu.SMEM),
                  pl.BlockSpec((ts * ROW_SUB, BLK), lambda i, *_: (i, 0))],
        out_specs=pl.BlockSpec(memory_space=pl.ANY),
        scratch_shapes=[pltpu.VMEM((rows_per_tile * ROW_SUB, BLK), f32),
                        pltpu.SemaphoreType.DMA(()),
                        pltpu.SemaphoreType.DMA(())])
    return pl.pallas_call(
        functools.partial(_scatter_body, rows_per_tile=rows_per_tile),
        grid_spec=grid_spec,
        out_shape=jax.ShapeDtypeStruct((total_rows * ROW_SUB, BLK), f32),
        compiler_params=pltpu.CompilerParams(
            dimension_semantics=("arbitrary",), vmem_limit_bytes=VMEM_LIMIT),
        name="scatter",
    )(ends, padded, pos_tiles, m_rows)


def _experts_body(tile_ref, texp_ref, nused_ref, xs_ref, wgu_ref, wd_ref, ys_ref):
    @pl.when(pl.program_id(0) < nused_ref[0])
    def _():
        r = xs_ref.shape[0] // ROW_SUB
        xt = _load_rows(xs_ref, r).astype(bf16)
        gu = jnp.dot(xt, wgu_ref[0], preferred_element_type=f32)
        gate = gu[:, :D_FF_E]
        up = gu[:, D_FF_E:]
        hdn = (gate * jax.nn.sigmoid(gate)) * up
        y = jnp.dot(hdn.astype(bf16), wd_ref[0], preferred_element_type=f32)
        _store_rows(ys_ref, y)

    @pl.when(pl.program_id(0) >= nused_ref[0])
    def _():
        ys_ref[...] = jnp.zeros(ys_ref.shape, f32)


def _experts(tile_idx, tile_exp, n_used, xs, wgu, wd, rows_per_tile):
    total_rows = xs.shape[0] // ROW_SUB
    r = rows_per_tile
    grid_spec = pltpu.PrefetchScalarGridSpec(
        num_scalar_prefetch=3,
        grid=(total_rows // r,),
        in_specs=[pl.BlockSpec((r * ROW_SUB, BLK), lambda i, ti, te, nu: (ti[i], 0)),
                  pl.BlockSpec((1, D_MODEL, 2 * D_FF_E), lambda i, ti, te, nu: (te[i], 0, 0)),
                  pl.BlockSpec((1, D_FF_E, D_MODEL), lambda i, ti, te, nu: (te[i], 0, 0))],
        out_specs=pl.BlockSpec((r * ROW_SUB, BLK), lambda i, ti, te, nu: (i, 0)))
    return pl.pallas_call(
        _experts_body,
        grid_spec=grid_spec,
        out_shape=jax.ShapeDtypeStruct((total_rows * ROW_SUB, BLK), f32),
        compiler_params=pltpu.CompilerParams(
            dimension_semantics=("arbitrary",), vmem_limit_bytes=VMEM_LIMIT),
        name="experts",
    )(tile_idx, tile_exp, n_used, xs, wgu, wd)


def _combine_out_body(pos_cur_ref, pos_nxt_ref, h_ref, rw_ref, p_ref, ys_ref,
                      wpg_ref, bpg_ref, wpp_ref, gple_ref, gfin_ref,
                      o_ref, ybuf_ref, sem_ref):
    tk = h_ref.shape[0]
    i = pl.program_id(0)
    n = pl.num_programs(0)
    slot = i % 2

    def row_copy(pos_ref, s, k, r, src):
        src_rows = ys_ref.at[pl.ds(pl.multiple_of(src * ROW_SUB, ROW_SUB), ROW_SUB), :]
        dst_rows = ybuf_ref.at[s, k, pl.ds(pl.multiple_of(r * ROW_SUB, ROW_SUB), ROW_SUB), :]
        return pltpu.make_async_copy(src_rows, dst_rows, sem_ref.at[s])

    def gather(pos_ref, s):
        def issue(r, carry):
            row_copy(pos_ref, s, 0, r, pos_ref[0, 0, r]).start()
            row_copy(pos_ref, s, 1, r, pos_ref[0, 0, tk + r]).start()
            return carry
        lax.fori_loop(0, tk, issue, 0, unroll=8)

    @pl.when(i == 0)
    def _():
        gather(pos_cur_ref, 0)

    @pl.when(i + 1 < n)
    def _():
        gather(pos_nxt_ref, 1 - slot)

    def drain(r, carry):
        row_copy(pos_cur_ref, slot, 0, 0, 0).wait()
        row_copy(pos_cur_ref, slot, 1, 0, 0).wait()
        return carry

    lax.fori_loop(0, tk, drain, 0, unroll=8)

    y1 = _load_rows(ybuf_ref.at[slot, 0], tk)
    y2 = _load_rows(ybuf_ref.at[slot, 1], tk)
    h2 = h_ref[...] + (rw_ref[:, 0:1] * y1 + rw_ref[:, 1:2] * y2)
    gate = jax.nn.sigmoid(jnp.dot(h2.astype(bf16), wpg_ref[...], preferred_element_type=f32)
                          + bpg_ref[...])
    pp = jnp.dot(p_ref[...].astype(bf16), wpp_ref[...], preferred_element_type=f32)
    h3 = h2 + gate * _rms(pp, gple_ref[...])
    o_ref[...] = _rms(h3, gfin_ref[...])


def _combine_out(pos_tiles, h2d, rw_t, p2d, ys, wpg, bpg, wpp, gple, gfin):
    t = h2d.shape[0]
    tk = TK_OUT
    nt = t // tk
    const = lambda shape: pl.BlockSpec(shape, lambda i: (0,) * len(shape))
    return pl.pallas_call(
        _combine_out_body,
        grid=(nt,),
        out_shape=jax.ShapeDtypeStruct((t, D_MODEL), f32),
        in_specs=[pl.BlockSpec((1, 1, 2 * tk), lambda i: (i, 0, 0), memory_space=pltpu.SMEM),
                  pl.BlockSpec((1, 1, 2 * tk), lambda i: (jnp.minimum(i + 1, nt - 1), 0, 0),
                               memory_space=pltpu.SMEM),
                  pl.BlockSpec((tk, D_MODEL), lambda i: (i, 0)),
                  pl.BlockSpec((tk, 2), lambda i: (i, 0)),
                  pl.BlockSpec((tk, PLE_DIM), lambda i: (i, 0)),
                  pl.BlockSpec(memory_space=pl.ANY),
                  const((D_MODEL, D_MODEL)), const((1, D_MODEL)),
                  const((PLE_DIM, D_MODEL)), const((1, D_MODEL)), const((1, D_MODEL))],
        out_specs=pl.BlockSpec((tk, D_MODEL), lambda i: (i, 0)),
        scratch_shapes=[pltpu.VMEM((2, 2, tk * ROW_SUB, BLK), f32),
                        pltpu.SemaphoreType.DMA((2,))],
        compiler_params=pltpu.CompilerParams(
            dimension_semantics=("arbitrary",), vmem_limit_bytes=VMEM_LIMIT),
        name="combine_out",
    )(pos_tiles, pos_tiles, h2d, rw_t, p2d, ys, wpg, bpg, wpp, gple, gfin)


def _q_perm():
    perm = np.empty((D_B,), np.int32)
    for g in range(GQA):
        for kh in range(N_KV):
            for d in range(HEAD_DIM):
                perm[g * BLK + kh * HEAD_DIM + d] = (kh * GQA + g) * HEAD_DIM + d
    return perm


def _pos_tiles(pos, tile):
    t = pos.shape[1]
    nt = t // tile
    return pos[:2].reshape(2, nt, tile).transpose(1, 0, 2).reshape(nt, 1, 2 * tile)


def kernel(x, p, rel_bias, g_mix, w_in, ln_v_g, ln_v_b, w_spatial, b_spatial, sink, g_out_grp, w_out,
           g_ffn, w_router_group, b_router_group, w_router_expert, b_router_expert, w_gate_e, w_up_e,
           w_down_e, w_ple_proj, g_ple, w_ple_gate, b_ple_gate, g_final):
    b, s, d = x.shape
    t = b * s
    depth = g_mix.shape[0]
    assert depth == 1 and d == D_MODEL
    perm = _q_perm()
    c1, c2 = 2 * D_A, 2 * D_A + D_B
    bias = _bias_table(rel_bias)
    hcur = x.astype(f32)
    for li in range(depth):
        wi = w_in[li]
        win = jnp.concatenate([wi[:, :c1], wi[:, c1:c2][:, perm] * (HEAD_DIM ** -0.5), wi[:, c2:]],
                              axis=1).astype(bf16)
        bs = jnp.broadcast_to(b_spatial[li][:, :, None], (A_HEADS, BLK, BLK)).astype(f32)
        gout = g_out_grp[li]
        wo = w_out[li]
        wout = jnp.concatenate([wo[:D_A], wo[D_A:][perm]], axis=0).astype(bf16)
        wr = jnp.zeros((ROUTE_ROWS, D_MODEL), f32)
        wr = wr.at[0:N_GROUPS].set(w_router_group[li].T).at[8:].set(w_router_expert[li].T)
        wrh = wr.astype(bf16)
        wrl = (wr - wrh.astype(f32)).astype(bf16)
        br = jnp.full((ROUTE_ROWS,), NEG, f32)
        br = br.at[0:N_GROUPS].set(b_router_group[li]).at[8:].set(b_router_expert[li])
        br = jnp.broadcast_to(br[:, None], (ROUTE_ROWS, TQ_ATT))
        wgu = jnp.concatenate([w_gate_e[li], w_up_e[li]], axis=2).astype(bf16)
        wd = w_down_e[li].astype(bf16)

        ya, q, k, v = _mix_in(hcur.reshape(t, d), g_mix[li][None], win, ln_v_g[li][None],
                              ln_v_b[li][None], w_spatial[li].astype(bf16), bs, gout[None, :D_A])
        h, m_rows, route_e, route_w = _attn_out(
            sink[li].astype(f32), hcur, ya.reshape(b, s, D_A), q.reshape(b, s, D_B),
            k.reshape(b, s, BLK), v.reshape(b, s, BLK), bias, gout[None, D_A:][:, perm], wout,
            g_ffn[li][None], wrh, wrl, br)

        r = R_EXP
        n_tiles = (2 * t) // r + N_EXPERTS
        pos, offs = _route_pos(route_e, r)
        starts = offs[0, :, 0]
        padded = offs[1, :, 0]
        ends = starts + padded
        n_used = ends[-1] // r
        tile_idx = jnp.minimum(jnp.arange(n_tiles, dtype=i32), n_used - 1)
        tile_exp = jnp.minimum(
            jnp.sum((ends[None, :] <= (tile_idx * r)[:, None]).astype(i32), axis=1), N_EXPERTS - 1)
        xs = _scatter(ends, padded, _pos_tiles(pos, TS_SCT), m_rows, n_tiles * r, r)
        ys = _experts(tile_idx, tile_exp.astype(i32), n_used.reshape(1), xs, wgu, wd, r)

        out = _combine_out(_pos_tiles(pos, TK_OUT), h.reshape(t, d), route_w[:2].T,
                           p[li].reshape(t, PLE_DIM), ys, w_ple_gate[li].astype(bf16),
                           b_ple_gate[li][None], w_ple_proj[li].astype(bf16), g_ple[li][None],
                           g_final[None])
        hcur = out.reshape(b, s, d)
    return hcur
```

```python
import functools
import math

import jax
import jax.numpy as jnp
import numpy as np
from jax import lax
from jax.experimental import pallas as pl
from jax.experimental.pallas import tpu as pltpu

D_MODEL = 1024
D_A = 512
D_B = 512
BLK = 128
A_HEADS = 4
HEAD_DIM = 64
N_HEADS = 8
N_KV = 2
GQA = 4
WINDOW = 128
NUM_BUCKETS = 32
MAX_DIST = 128
D_IN = 2 * D_A + D_B + 2 * N_KV * HEAD_DIM
N_GROUPS = 4
E_PER_GROUP = 8
N_EXPERTS = 32
D_FF_E = 256
PLE_DIM = 256
EPS = 1e-6
NEG = -1e30

TM_MIX = 512
TQ_ATT = 512
TR_POS = 1024
R_EXP = 256
TK_OUT = 512
ROUTE_ROWS = 8 + N_EXPERTS
VMEM_LIMIT = 48 * 1024 * 1024

f32 = jnp.float32
bf16 = jnp.bfloat16
i32 = jnp.int32
u32 = jnp.uint32


def _rms(x, g):
    return x * lax.rsqrt(jnp.mean(x * x, axis=-1, keepdims=True) + EPS) * g


def _gelu_tanh(x):
    c = math.sqrt(2.0 / math.pi)
    return x * (0.5 * (1.0 + jnp.tanh(c * (x + 0.044715 * (x * x * x)))))


ROW_SUB = D_MODEL // BLK


def _store_rows(ref, val):
    n = val.shape[0]
    for j in range(ROW_SUB):
        ref[pl.ds(j, n, stride=ROW_SUB), :] = val[:, j * BLK:(j + 1) * BLK]


def _load_rows(ref, n):
    return jnp.concatenate([ref[pl.ds(j, n, stride=ROW_SUB), :] for j in range(ROW_SUB)], axis=1)


def _bucket_table():
    n = NUM_BUCKETS // 2
    max_exact = n // 2
    i = np.arange(BLK)[:, None]
    j = np.arange(3 * BLK)[None, :]
    rel = j - BLK - i
    ret = np.where(rel > 0, n, 0)
    a = np.abs(rel)
    large = max_exact + (np.log(np.maximum(a, 1).astype(np.float64) / max_exact)
                         / math.log(MAX_DIST / max_exact) * (n - max_exact)).astype(np.int32)
    large = np.minimum(large, n - 1)
    return (ret + np.where(a < max_exact, a, large)).astype(np.int32)


def _bias_body(rb_ref, bucket_ref, o_ref):
    bucket = bucket_ref[...]
    o_ref[...] = jnp.zeros(o_ref.shape, f32)

    def step(b, carry):
        hit = bucket == b
        for h in range(N_HEADS):
            o_ref[h] = jnp.where(hit, rb_ref[b, h], o_ref[h])
        return carry

    lax.fori_loop(0, NUM_BUCKETS, step, 0)


def _bias_table(rel_bias):
    bucket = jnp.asarray(_bucket_table())
    return pl.pallas_call(
        _bias_body,
        out_shape=jax.ShapeDtypeStruct((N_HEADS, BLK, 3 * BLK), f32),
        in_specs=[pl.BlockSpec(memory_space=pltpu.SMEM),
                  pl.BlockSpec((BLK, 3 * BLK), lambda: (0, 0))],
        out_specs=pl.BlockSpec((N_HEADS, BLK, 3 * BLK), lambda: (0, 0, 0)),
        name="bias_table",
    )(rel_bias.astype(f32), bucket)


def _mix_in_body(x_ref, gmix_ref, win_ref, lng_ref, lnb_ref, ws_ref, bs_ref, gout_ref,
                 ya_ref, q_ref, k_ref, v_ref):
    tm = x_ref.shape[0]
    nc = tm // BLK
    a = _rms(x_ref[...], gmix_ref[...])
    z = jnp.dot(a.astype(bf16), win_ref[...], preferred_element_type=f32)
    uv = _gelu_tanh(z[:, :2 * D_A])
    u = uv[:, :D_A]
    v = uv[:, D_A:]
    mu = jnp.mean(v, axis=-1, keepdims=True)
    vc = v - mu
    var = jnp.mean(vc * vc, axis=-1, keepdims=True)
    vn = (vc * lax.rsqrt(var + EPS) * lng_ref[...] + lnb_ref[...]).astype(bf16)
    cols = []
    for h in range(A_HEADS):
        rhs = jnp.concatenate(
            [vn[c * BLK:(c + 1) * BLK, h * BLK:(h + 1) * BLK] for c in range(nc)], axis=1)
        r = jnp.dot(ws_ref[h], rhs, preferred_element_type=f32)
        cols.append(jnp.concatenate(
            [r[:, c * BLK:(c + 1) * BLK] + bs_ref[h] for c in range(nc)], axis=0))
    sv = jnp.concatenate(cols, axis=1)
    ya_ref[...] = _rms(u * sv, gout_ref[...]).astype(bf16)
    q_ref[...] = z[:, 2 * D_A:2 * D_A + D_B].astype(bf16)
    k_ref[...] = z[:, 2 * D_A + D_B:2 * D_A + D_B + BLK].astype(bf16)
    v_ref[...] = z[:, 2 * D_A + D_B + BLK:].astype(bf16)


def _mix_in(x2, gmix, win, lng, lnb, ws, bs, gout_a):
    t = x2.shape[0]
    tm = TM_MIX
    const = lambda shape: pl.BlockSpec(shape, lambda i: (0,) * len(shape))
    return pl.pallas_call(
        _mix_in_body,
        grid=(t // tm,),
        out_shape=(jax.ShapeDtypeStruct((t, D_A), bf16),
                   jax.ShapeDtypeStruct((t, D_B), bf16),
                   jax.ShapeDtypeStruct((t, BLK), bf16),
                   jax.ShapeDtypeStruct((t, BLK), bf16)),
        in_specs=[pl.BlockSpec((tm, D_MODEL), lambda i: (i, 0)),
                  const((1, D_MODEL)), const((D_MODEL, D_IN)),
                  const((1, D_A)), const((1, D_A)),
                  const((A_HEADS, BLK, BLK)), const((A_HEADS, BLK, BLK)),
                  const((1, D_A))],
        out_specs=(pl.BlockSpec((tm, D_A), lambda i: (i, 0)),
                   pl.BlockSpec((tm, D_B), lambda i: (i, 0)),
                   pl.BlockSpec((tm, BLK), lambda i: (i, 0)),
                   pl.BlockSpec((tm, BLK), lambda i: (i, 0))),
        compiler_params=pltpu.CompilerParams(
            dimension_semantics=("parallel",), vmem_limit_bytes=VMEM_LIMIT),
        name="mix_in",
    )(x2, gmix, win, lng, lnb, ws, bs, gout_a)


def _attn_out_body(sink_ref, x_ref, ya_ref, q_ref, kp_ref, km_ref, kn_ref, vp_ref, vm_ref, vn_ref,
                   bias_ref, goutb_ref, wout_ref, gffn_ref, wrh_ref, wrl_ref, br_ref,
                   h_ref, m_ref, re_ref, rw_ref,
                   kf_ref, vf_ref, e_ref, yb_ref, *, seq_len):
    tq = x_ref.shape[1]
    nb = tq // BLK
    ti = pl.program_id(1)

    kf_ref[0:BLK] = kp_ref[0]
    kf_ref[BLK:BLK + tq] = km_ref[0]
    kf_ref[BLK + tq:] = kn_ref[0]
    vf_ref[0:BLK] = vp_ref[0]
    vf_ref[BLK:BLK + tq] = vm_ref[0]
    vf_ref[BLK + tq:] = vn_ref[0]

    row = lax.broadcasted_iota(i32, (BLK, 3 * BLK), 0)
    col = lax.broadcasted_iota(i32, (BLK, 3 * BLK), 1)
    band = jnp.abs(col - BLK - row) <= WINDOW
    lane = lax.broadcasted_iota(i32, (BLK, BLK), 1)
    low_half = lane < HEAD_DIM

    def block(n, carry):
        r0 = pl.multiple_of(n * BLK, BLK)
        qb = q_ref[0, pl.ds(r0, BLK), :]
        kb = kf_ref[pl.ds(r0, 3 * BLK), :]
        vb = vf_ref[pl.ds(r0, 3 * BLK), :]
        zero = jnp.zeros((BLK, BLK), bf16)
        lhs = []
        for kh in range(N_KV):
            for g in range(GQA):
                tile = qb[:, g * BLK:(g + 1) * BLK]
                lhs.append(jnp.where(low_half, tile, zero) if kh == 0
                           else jnp.where(low_half, zero, tile))
        lhs = jnp.concatenate(lhs, axis=0)
        s_all = lax.dot_general(lhs, kb, (((1,), (1,)), ((), ())),
                                preferred_element_type=f32)
        kpos = col + ((ti * nb + n - 1) * BLK)
        valid = band & (kpos >= 0) & (kpos < seq_len)
        rden = []
        for h in range(N_HEADS):
            s = jnp.where(valid, s_all[h * BLK:(h + 1) * BLK] + bias_ref[h], NEG)
            sk = sink_ref[h]
            mrow = jnp.maximum(jnp.max(s, axis=-1, keepdims=True), sk)
            e = jnp.exp(s - mrow)
            den = jnp.sum(e, axis=-1, keepdims=True) + jnp.exp(sk - mrow)
            rden.append(1.0 / den)
            e_ref[h * BLK:(h + 1) * BLK, :] = e.astype(bf16)
        pv = jnp.dot(e_ref[...], vb, preferred_element_type=f32)
        for g in range(GQA):
            o0 = pv[g * BLK:(g + 1) * BLK] * rden[g]
            o1 = pv[(GQA + g) * BLK:(GQA + g + 1) * BLK] * rden[GQA + g]
            yb_ref[pl.ds(r0, BLK), g * BLK:(g + 1) * BLK] = jnp.where(low_half, o0, o1)
        return carry

    lax.fori_loop(0, nb, block, 0)

    ybn = _rms(yb_ref[...], goutb_ref[...]).astype(bf16)
    y = jnp.concatenate([ya_ref[0], ybn], axis=1)
    h = x_ref[0] + jnp.dot(y, wout_ref[...], preferred_element_type=f32)
    h_ref[0] = h
    m = _rms(h, gffn_ref[...])
    _store_rows(m_ref, m)

    m_hi = m.astype(bf16)
    m_lo = (m - m_hi.astype(f32)).astype(bf16)
    nt = (((1,), (1,)), ((), ()))
    logit = (lax.dot_general(wrh_ref[...], m_hi, nt, preferred_element_type=f32)
             + lax.dot_general(wrh_ref[...], m_lo, nt, preferred_element_type=f32)
             + lax.dot_general(wrl_ref[...], m_hi, nt, preferred_element_type=f32)
             + br_ref[...])
    sub = lax.broadcasted_iota(i32, (8, tq), 0)
    lg = logit[0:8]
    mg = jnp.max(lg, axis=0, keepdims=True)
    pg_top = 1.0 / jnp.sum(jnp.exp(lg - mg), axis=0, keepdims=True)
    g_idx = jnp.min(jnp.where(lg == mg, sub, 8), axis=0, keepdims=True)
    sel = logit[8:16]
    for g in range(1, N_GROUPS):
        sel = jnp.where(g_idx == g, logit[8 + 8 * g:16 + 8 * g], sel)
    m1 = jnp.max(sel, axis=0, keepdims=True)
    i1 = jnp.min(jnp.where(sel == m1, sub, 8), axis=0, keepdims=True)
    sel2 = jnp.where(sub == i1, -jnp.inf, sel)
    m2 = jnp.max(sel2, axis=0, keepdims=True)
    i2 = jnp.min(jnp.where(sel2 == m2, sub, 8), axis=0, keepdims=True)
    r = jnp.exp(m2 - m1)
    w1 = pg_top / (1.0 + r)
    w2 = pg_top * r / (1.0 + r)
    e1 = g_idx * E_PER_GROUP + i1
    e2 = g_idx * E_PER_GROUP + i2
    re_ref[...] = jnp.where(sub == 0, e1, jnp.where(sub == 1, e2, 0))
    rw_ref[...] = jnp.where(sub == 0, w1, jnp.where(sub == 1, w2, 0.0))


def _attn_out(sink, x, ya, q, k, v, bias, goutb, wout, gffn, wrh, wrl, br):
    b, s, _ = x.shape
    tq = TQ_ATT
    nb = tq // BLK
    nblk = s // BLK
    t = b * s
    nq = s // tq
    const = lambda shape: pl.BlockSpec(shape, lambda bi, i, *_: (0,) * len(shape))
    tok = lambda w: pl.BlockSpec((1, tq, w), lambda bi, i, *_: (bi, i, 0))
    prev = pl.BlockSpec((1, BLK, BLK), lambda bi, i, *_: (bi, jnp.maximum(i * nb - 1, 0), 0))
    nxt = pl.BlockSpec((1, BLK, BLK), lambda bi, i, *_: (bi, jnp.minimum(i * nb + nb, nblk - 1), 0))
    rows = pl.BlockSpec((tq * ROW_SUB, BLK), lambda bi, i, *_: (bi * nq + i, 0))
    lanes = lambda rows: pl.BlockSpec((rows, tq), lambda bi, i, *_: (0, bi * nq + i))
    grid_spec = pltpu.PrefetchScalarGridSpec(
        num_scalar_prefetch=1,
        grid=(b, nq),
        in_specs=[tok(D_MODEL), tok(D_A), tok(D_B),
                  prev, tok(BLK), nxt, prev, tok(BLK), nxt,
                  const((N_HEADS, BLK, 3 * BLK)), const((1, D_B)), const((D_MODEL, D_MODEL)),
                  const((1, D_MODEL)), const((ROUTE_ROWS, D_MODEL)), const((ROUTE_ROWS, D_MODEL)),
                  const((ROUTE_ROWS, tq))],
        out_specs=(tok(D_MODEL), rows, lanes(8), lanes(8)),
        scratch_shapes=[pltpu.VMEM((tq + 2 * BLK, BLK), bf16),
                        pltpu.VMEM((tq + 2 * BLK, BLK), bf16),
                        pltpu.VMEM((N_HEADS * BLK, 3 * BLK), bf16),
                        pltpu.VMEM((tq, D_B), f32)])
    return pl.pallas_call(
        functools.partial(_attn_out_body, seq_len=s),
        grid_spec=grid_spec,
        out_shape=(jax.ShapeDtypeStruct((b, s, D_MODEL), f32),
                   jax.ShapeDtypeStruct((t * ROW_SUB, BLK), f32),
                   jax.ShapeDtypeStruct((8, t), i32),
                   jax.ShapeDtypeStruct((8, t), f32)),
        compiler_params=pltpu.CompilerParams(
            dimension_semantics=("parallel", "parallel"), vmem_limit_bytes=VMEM_LIMIT),
        name="attn_out",
    )(sink, x, ya, q, k, k, k, v, v, v, bias, goutb, wout, gffn, wrh, wrl, br)


def _route_pos_body(re_ref, pos_ref, offs_ref, cnt_ref, carry_ref, tri_ref, *, rows_per_tile):
    phase = pl.program_id(0)
    ti = pl.program_id(1)
    tr = re_ref.shape[1]
    log_r = rows_per_tile.bit_length() - 1
    eid = lax.broadcasted_iota(i32, (N_EXPERTS, tr), 0)
    e1 = re_ref[0:1, :]
    e2 = re_ref[1:2, :]
    hit1 = eid == e1
    hit2 = eid == e2
    onehot = jnp.where(hit1 | hit2, 1.0, 0.0)
    tile_cnt = jnp.sum(onehot, axis=1, keepdims=True)

    @pl.when((phase == 0) & (ti == 0))
    def _():
        cnt_ref[...] = jnp.zeros(cnt_ref.shape, f32)
        a = lax.broadcasted_iota(i32, (tr, tr), 0)
        c = lax.broadcasted_iota(i32, (tr, tr), 1)
        tri_ref[...] = jnp.where(a < c, 1.0, 0.0).astype(bf16)

    @pl.when(phase == 0)
    def _():
        cnt_ref[...] = cnt_ref[...] + tile_cnt

    @pl.when((phase == 1) & (ti == 0))
    def _():
        cnt = cnt_ref[...].astype(i32)
        padded = ((cnt + (rows_per_tile - 1)) >> log_r) << log_r
        p_hi = (padded >> 8).astype(f32).astype(bf16)
        p_lo = (padded & 255).astype(f32).astype(bf16)
        a = lax.broadcasted_iota(i32, (N_EXPERTS, N_EXPERTS), 0)
        c = lax.broadcasted_iota(i32, (N_EXPERTS, N_EXPERTS), 1)
        low = jnp.where(c < a, 1.0, 0.0).astype(bf16)
        off = (jnp.dot(low, p_hi, preferred_element_type=f32) * 256.0
               + jnp.dot(low, p_lo, preferred_element_type=f32))
        carry_ref[...] = off
        offs_ref[0] = off.astype(i32)
        offs_ref[1] = padded
        offs_ref[2] = cnt

    @pl.when(phase == 1)
    def _():
        before = jnp.dot(onehot.astype(bf16), tri_ref[...], preferred_element_type=f32)
        slot = before + carry_ref[:, 0:1]
        p1 = jnp.sum(jnp.where(hit1, slot, 0.0), axis=0, keepdims=True)
        p2 = jnp.sum(jnp.where(hit2, slot, 0.0), axis=0, keepdims=True)
        sub = lax.broadcasted_iota(i32, (8, tr), 0)
        pos_ref[...] = jnp.where(sub == 0, p1.astype(i32), jnp.where(sub == 1, p2.astype(i32), 0))
        carry_ref[...] = carry_ref[...] + tile_cnt


def _route_pos(route_e, rows_per_tile):
    t = route_e.shape[1]
    tr = TR_POS
    return pl.pallas_call(
        functools.partial(_route_pos_body, rows_per_tile=rows_per_tile),
        grid=(2, t // tr),
        out_shape=(jax.ShapeDtypeStruct((8, t), i32),
                   jax.ShapeDtypeStruct((3, N_EXPERTS, BLK), i32)),
        in_specs=[pl.BlockSpec((8, tr), lambda p, i: (0, i))],
        out_specs=(pl.BlockSpec((8, tr), lambda p, i: (0, i * p)),
                   pl.BlockSpec((3, N_EXPERTS, BLK), lambda p, i: (0, 0, 0))),
        scratch_shapes=[pltpu.VMEM((N_EXPERTS, BLK), f32),
                        pltpu.VMEM((N_EXPERTS, BLK), f32),
                        pltpu.VMEM((tr, tr), bf16)],
        compiler_params=pltpu.CompilerParams(
            dimension_semantics=("arbitrary", "arbitrary"), vmem_limit_bytes=VMEM_LIMIT),
        name="route_pos",
    )(route_e)


def _slot_tokens_body(pos_ref, src_ref, zero_ref, sem_ref):
    t = pos_ref.shape[0] // 2
    zero_ref[...] = jnp.zeros(zero_ref.shape, i32)
    fill = pltpu.make_async_copy(zero_ref, src_ref, sem_ref)
    fill.start()
    fill.wait()

    def body(a8, carry):
        for j in range(8):
            a = a8 * 8 + j
            src_ref[pos_ref[a]] = a
            src_ref[pos_ref[t + a]] = a
        return carry

    lax.fori_loop(0, t // 8, body, 0)


def _slot_tokens(pos_flat, total_rows):
    return pl.pallas_call(
        _slot_tokens_body,
        out_shape=jax.ShapeDtypeStruct((total_rows,), i32),
        in_specs=[pl.BlockSpec(memory_space=pltpu.SMEM)],
        out_specs=pl.BlockSpec(memory_space=pltpu.SMEM),
        scratch_shapes=[pltpu.VMEM((total_rows,), i32), pltpu.SemaphoreType.DMA(())],
        name="slot_tokens",
    )(pos_flat)


def _experts_body(tile_ref, texp_ref, nused_ref, src_cur_ref, src_nxt_ref, m_ref, wgu_ref, wd_ref,
                  ys_ref, xbuf_ref, sem_ref):
    i = pl.program_id(0)
    n_used = nused_ref[0]
    r_tile = src_cur_ref.shape[2]
    slot = i % 2

    def row_copy(s, r, tok):
        src_rows = m_ref.at[pl.ds(pl.multiple_of(tok * ROW_SUB, ROW_SUB), ROW_SUB), :]
        dst_rows = xbuf_ref.at[s, pl.ds(pl.multiple_of(r * ROW_SUB, ROW_SUB), ROW_SUB), :]
        return pltpu.make_async_copy(src_rows, dst_rows, sem_ref.at[s])

    def gather(src_ref, s):
        def issue(r8, carry):
            for j in range(8):
                r = r8 * 8 + j
                row_copy(s, r, src_ref[0, 0, r]).start(priority=j % 2)
            return carry
        lax.fori_loop(0, r_tile // 8, issue, 0)

    @pl.when(i == 0)
    def _():
        gather(src_cur_ref, 0)

    @pl.when(i + 1 < n_used)
    def _():
        gather(src_nxt_ref, 1 - slot)

    @pl.when(i < n_used)
    def _():
        def drain(r8, carry):
            for j in range(8):
                row_copy(slot, 0, 0).wait()
            return carry
        lax.fori_loop(0, r_tile // 8, drain, 0)
        xt = _load_rows(xbuf_ref.at[slot], r_tile).astype(bf16)
        gu = jnp.dot(xt, wgu_ref[0], preferred_element_type=f32)
        gate = gu[:, :D_FF_E]
        up = gu[:, D_FF_E:]
        hdn = (gate * jax.nn.sigmoid(gate)) * up
        y = jnp.dot(hdn.astype(bf16), wd_ref[0], preferred_element_type=f32)
        _store_rows(ys_ref, y)

    @pl.when(i >= n_used)
    def _():
        ys_ref[...] = jnp.zeros(ys_ref.shape, f32)


def _experts(tile_idx, tile_exp, n_used, src_tiles, m_rows, wgu, wd):
    n_tiles, _, r = src_tiles.shape
    nxt = lambda i, ti, te, nu: (ti[jnp.minimum(i + 1, n_tiles - 1)], 0, 0)
    grid_spec = pltpu.PrefetchScalarGridSpec(
        num_scalar_prefetch=3,
        grid=(n_tiles,),
        in_specs=[pl.BlockSpec((1, 1, r), lambda i, ti, te, nu: (ti[i], 0, 0),
                               memory_space=pltpu.SMEM),
                  pl.BlockSpec((1, 1, r), nxt, memory_space=pltpu.SMEM),
                  pl.BlockSpec(memory_space=pl.ANY),
                  pl.BlockSpec((1, D_MODEL, 2 * D_FF_E), lambda i, ti, te, nu: (te[i], 0, 0)),
                  pl.BlockSpec((1, D_FF_E, D_MODEL), lambda i, ti, te, nu: (te[i], 0, 0))],
        out_specs=pl.BlockSpec((r * ROW_SUB, BLK), lambda i, ti, te, nu: (i, 0)),
        scratch_shapes=[pltpu.VMEM((2, r * ROW_SUB, BLK), f32),
                        pltpu.SemaphoreType.DMA((2,))])
    return pl.pallas_call(
        _experts_body,
        grid_spec=grid_spec,
        out_shape=jax.ShapeDtypeStruct((n_tiles * r * ROW_SUB, BLK), f32),
        compiler_params=pltpu.CompilerParams(
            dimension_semantics=("arbitrary",), vmem_limit_bytes=VMEM_LIMIT),
        name="experts",
    )(tile_idx, tile_exp, n_used, src_tiles, src_tiles, m_rows, wgu, wd)


def _combine_out_body(pos_cur_ref, pos_nxt_ref, h_ref, rw_ref, p_ref, ys_ref,
                      wpg_ref, bpg_ref, wpp_ref, gple_ref, gfin_ref,
                      o_ref, ybuf_ref, sem_ref):
    tk = h_ref.shape[0]
    i = pl.program_id(0)
    n = pl.num_programs(0)
    slot = i % 2

    def row_copy(pos_ref, s, k, r, src):
        src_rows = ys_ref.at[pl.ds(pl.multiple_of(src * ROW_SUB, ROW_SUB), ROW_SUB), :]
        dst_rows = ybuf_ref.at[s, k, pl.ds(pl.multiple_of(r * ROW_SUB, ROW_SUB), ROW_SUB), :]
        return pltpu.make_async_copy(src_rows, dst_rows, sem_ref.at[s])

    def gather(pos_ref, s):
        def issue(r8, carry):
            for j in range(8):
                r = r8 * 8 + j
                row_copy(pos_ref, s, 0, r, pos_ref[0, 0, r]).start(priority=0)
                row_copy(pos_ref, s, 1, r, pos_ref[0, 0, tk + r]).start(priority=1)
            return carry
        lax.fori_loop(0, tk // 8, issue, 0)

    @pl.when(i == 0)
    def _():
        gather(pos_cur_ref, 0)

    @pl.when(i + 1 < n)
    def _():
        gather(pos_nxt_ref, 1 - slot)

    def drain(r, carry):
        row_copy(pos_cur_ref, slot, 0, 0, 0).wait()
        row_copy(pos_cur_ref, slot, 1, 0, 0).wait()
        return carry

    lax.fori_loop(0, tk, drain, 0, unroll=8)

    y1 = _load_rows(ybuf_ref.at[slot, 0], tk)
    y2 = _load_rows(ybuf_ref.at[slot, 1], tk)
    h2 = h_ref[...] + (rw_ref[:, 0:1] * y1 + rw_ref[:, 1:2] * y2)
    gate = jax.nn.sigmoid(jnp.dot(h2.astype(bf16), wpg_ref[...], preferred_element_type=f32)
                          + bpg_ref[...])
    pp = jnp.dot(p_ref[...].astype(bf16), wpp_ref[...], preferred_element_type=f32)
    h3 = h2 + gate * _rms(pp, gple_ref[...])
    o_ref[...] = _rms(h3, gfin_ref[...])


def _combine_out(pos_tiles, h2d, rw_t, p2d, ys, wpg, bpg, wpp, gple, gfin):
    t = h2d.shape[0]
    tk = TK_OUT
    nt = t // tk
    const = lambda shape: pl.BlockSpec(shape, lambda i: (0,) * len(shape))
    return pl.pallas_call(
        _combine_out_body,
        grid=(nt,),
        out_shape=jax.ShapeDtypeStruct((t, D_MODEL), f32),
        in_specs=[pl.BlockSpec((1, 1, 2 * tk), lambda i: (i, 0, 0), memory_space=pltpu.SMEM),
                  pl.BlockSpec((1, 1, 2 * tk), lambda i: (jnp.minimum(i + 1, nt - 1), 0, 0),
                               memory_space=pltpu.SMEM),
                  pl.BlockSpec((tk, D_MODEL), lambda i: (i, 0)),
                  pl.BlockSpec((tk, 2), lambda i: (i, 0)),
                  pl.BlockSpec((tk, PLE_DIM), lambda i: (i, 0)),
                  pl.BlockSpec(memory_space=pl.ANY),
                  const((D_MODEL, D_MODEL)), const((1, D_MODEL)),
                  const((PLE_DIM, D_MODEL)), const((1, D_MODEL)), const((1, D_MODEL))],
        out_specs=pl.BlockSpec((tk, D_MODEL), lambda i: (i, 0)),
        scratch_shapes=[pltpu.VMEM((2, 2, tk * ROW_SUB, BLK), f32),
                        pltpu.SemaphoreType.DMA((2,))],
        compiler_params=pltpu.CompilerParams(
            dimension_semantics=("arbitrary",), vmem_limit_bytes=VMEM_LIMIT),
        name="combine_out",
    )(pos_tiles, pos_tiles, h2d, rw_t, p2d, ys, wpg, bpg, wpp, gple, gfin)


def _q_perm():
    perm = np.empty((D_B,), np.int32)
    for g in range(GQA):
        for kh in range(N_KV):
            for d in range(HEAD_DIM):
                perm[g * BLK + kh * HEAD_DIM + d] = (kh * GQA + g) * HEAD_DIM + d
    return perm


def _pos_tiles(pos, tile):
    t = pos.shape[1]
    nt = t // tile
    return pos[:2].reshape(2, nt, tile).transpose(1, 0, 2).reshape(nt, 1, 2 * tile)


def kernel(x, p, rel_bias, g_mix, w_in, ln_v_g, ln_v_b, w_spatial, b_spatial, sink, g_out_grp, w_out,
           g_ffn, w_router_group, b_router_group, w_router_expert, b_router_expert, w_gate_e, w_up_e,
           w_down_e, w_ple_proj, g_ple, w_ple_gate, b_ple_gate, g_final):
    b, s, d = x.shape
    t = b * s
    depth = g_mix.shape[0]
    assert depth == 1 and d == D_MODEL
    perm = _q_perm()
    c1, c2 = 2 * D_A, 2 * D_A + D_B
    bias = _bias_table(rel_bias)
    hcur = x.astype(f32)
    for li in range(depth):
        wi = w_in[li]
        win = jnp.concatenate([wi[:, :c1], wi[:, c1:c2][:, perm] * (HEAD_DIM ** -0.5), wi[:, c2:]],
                              axis=1).astype(bf16)
        bs = jnp.broadcast_to(b_spatial[li][:, :, None], (A_HEADS, BLK, BLK)).astype(f32)
        gout = g_out_grp[li]
        wo = w_out[li]
        wout = jnp.concatenate([wo[:D_A], wo[D_A:][perm]], axis=0).astype(bf16)
        wr = jnp.zeros((ROUTE_ROWS, D_MODEL), f32)
        wr = wr.at[0:N_GROUPS].set(w_router_group[li].T).at[8:].set(w_router_expert[li].T)
        wrh = wr.astype(bf16)
        wrl = (wr - wrh.astype(f32)).astype(bf16)
        br = jnp.full((ROUTE_ROWS,), NEG, f32)
        br = br.at[0:N_GROUPS].set(b_router_group[li]).at[8:].set(b_router_expert[li])
        br = jnp.broadcast_to(br[:, None], (ROUTE_ROWS, TQ_ATT))
        wgu = jnp.concatenate([w_gate_e[li], w_up_e[li]], axis=2).astype(bf16)
        wd = w_down_e[li].astype(bf16)

        ya, q, k, v = _mix_in(hcur.reshape(t, d), g_mix[li][None], win, ln_v_g[li][None],
                              ln_v_b[li][None], w_spatial[li].astype(bf16), bs, gout[None, :D_A])
        h, m_rows, route_e, route_w = _attn_out(
            sink[li].astype(f32), hcur, ya.reshape(b, s, D_A), q.reshape(b, s, D_B),
            k.reshape(b, s, BLK), v.reshape(b, s, BLK), bias, gout[None, D_A:][:, perm], wout,
            g_ffn[li][None], wrh, wrl, br)

        r = R_EXP
        n_tiles = (2 * t) // r + N_EXPERTS
        pos, offs = _route_pos(route_e, r)
        starts = offs[0, :, 0]
        padded = offs[1, :, 0]
        ends = starts + padded
        n_used = ends[-1] // r
        tile_idx = jnp.minimum(jnp.arange(n_tiles, dtype=i32), n_used - 1)
        tile_exp = jnp.minimum(
            jnp.sum((ends[None, :] <= (tile_idx * r)[:, None]).astype(i32), axis=1), N_EXPERTS - 1)
        src = _slot_tokens(pos[:2].reshape(2 * t), n_tiles * r)
        ys = _experts(tile_idx, tile_exp.astype(i32), n_used.reshape(1),
                      src.reshape(n_tiles, 1, r), m_rows, wgu, wd)

        out = _combine_out(_pos_tiles(pos, TK_OUT), h.reshape(t, d), route_w[:2].T,
                           p[li].reshape(t, PLE_DIM), ys, w_ple_gate[li].astype(bf16),
                           b_ple_gate[li][None], w_ple_proj[li].astype(bf16), g_ple[li][None],
                           g_final[None])
        hcur = out.reshape(b, s, d)
    return hcur
```

```python
import functools
import math

import jax
import jax.numpy as jnp
import numpy as np
from jax import lax
from jax.experimental import pallas as pl
from jax.experimental.pallas import tpu as pltpu

D_MODEL = 1024
D_A = 512
D_B = 512
BLK = 128
A_HEADS = 4
HEAD_DIM = 64
N_HEADS = 8
N_KV = 2
GQA = 4
WINDOW = 128
NUM_BUCKETS = 32
MAX_DIST = 128
D_IN = 2 * D_A + D_B + 2 * N_KV * HEAD_DIM
N_GROUPS = 4
E_PER_GROUP = 8
N_EXPERTS = 32
D_FF_E = 256
PLE_DIM = 256
EPS = 1e-6
NEG = -1e30

TM_MIX = 512
TQ_ATT = 512
TL_SORT = 512
RUN_ALIGN = 16
RUN_BITS = (TL_SORT // RUN_ALIGN).bit_length()
LB_SORT = 2 * TL_SORT + N_EXPERTS * RUN_ALIGN
R_EXP = 256
XS_COLS = D_MODEL + BLK
ROUTE_ROWS = 8 + N_EXPERTS
VMEM_LIMIT = 48 * 1024 * 1024

f32 = jnp.float32
bf16 = jnp.bfloat16
i32 = jnp.int32


def _rms(x, g):
    return x * lax.rsqrt(jnp.mean(x * x, axis=-1, keepdims=True) + EPS) * g


def _gelu_tanh(x):
    c = math.sqrt(2.0 / math.pi)
    return x * (0.5 * (1.0 + jnp.tanh(c * (x + 0.044715 * (x * x * x)))))


def _bucket_table():
    n = NUM_BUCKETS // 2
    max_exact = n // 2
    i = np.arange(BLK)[:, None]
    j = np.arange(3 * BLK)[None, :]
    rel = j - BLK - i
    ret = np.where(rel > 0, n, 0)
    a = np.abs(rel)
    large = max_exact + (np.log(np.maximum(a, 1).astype(np.float64) / max_exact)
                         / math.log(MAX_DIST / max_exact) * (n - max_exact)).astype(np.int32)
    large = np.minimum(large, n - 1)
    return (ret + np.where(a < max_exact, a, large)).astype(np.int32)


def _bias_body(rb_ref, bucket_ref, o_ref):
    bucket = bucket_ref[...]
    o_ref[...] = jnp.zeros(o_ref.shape, f32)

    def step(b, carry):
        hit = bucket == b
        for h in range(N_HEADS):
            o_ref[h] = jnp.where(hit, rb_ref[b, h], o_ref[h])
        return carry

    lax.fori_loop(0, NUM_BUCKETS, step, 0)


def _bias_table(rel_bias):
    bucket = jnp.asarray(_bucket_table())
    return pl.pallas_call(
        _bias_body,
        out_shape=jax.ShapeDtypeStruct((N_HEADS, BLK, 3 * BLK), f32),
        in_specs=[pl.BlockSpec(memory_space=pltpu.SMEM),
                  pl.BlockSpec((BLK, 3 * BLK), lambda: (0, 0))],
        out_specs=pl.BlockSpec((N_HEADS, BLK, 3 * BLK), lambda: (0, 0, 0)),
        name="bias_table",
    )(rel_bias.astype(f32), bucket)


def _mix_in_body(x_ref, gmix_ref, win_ref, lng_ref, lnb_ref, ws_ref, bs_ref, gout_ref,
                 ya_ref, q_ref, k_ref, v_ref):
    tm = x_ref.shape[0]
    nc = tm // BLK
    a = _rms(x_ref[...], gmix_ref[...])
    z = jnp.dot(a.astype(bf16), win_ref[...], preferred_element_type=f32)
    uv = _gelu_tanh(z[:, :2 * D_A])
    u = uv[:, :D_A]
    v = uv[:, D_A:]
    mu = jnp.mean(v, axis=-1, keepdims=True)
    vc = v - mu
    var = jnp.mean(vc * vc, axis=-1, keepdims=True)
    vn = (vc * lax.rsqrt(var + EPS) * lng_ref[...] + lnb_ref[...]).astype(bf16)
    cols = []
    for h in range(A_HEADS):
        rhs = jnp.concatenate(
            [vn[c * BLK:(c + 1) * BLK, h * BLK:(h + 1) * BLK] for c in range(nc)], axis=1)
        r = jnp.dot(ws_ref[h], rhs, preferred_element_type=f32)
        cols.append(jnp.concatenate(
            [r[:, c * BLK:(c + 1) * BLK] + bs_ref[h] for c in range(nc)], axis=0))
    sv = jnp.concatenate(cols, axis=1)
    ya_ref[...] = _rms(u * sv, gout_ref[...]).astype(bf16)
    q_ref[...] = z[:, 2 * D_A:2 * D_A + D_B].astype(bf16)
    k_ref[...] = z[:, 2 * D_A + D_B:2 * D_A + D_B + BLK].astype(bf16)
    v_ref[...] = z[:, 2 * D_A + D_B + BLK:].astype(bf16)


def _mix_in(x2, gmix, win, lng, lnb, ws, bs, gout_a):
    t = x2.shape[0]
    tm = TM_MIX
    const = lambda shape: pl.BlockSpec(shape, lambda i: (0,) * len(shape))
    return pl.pallas_call(
        _mix_in_body,
        grid=(t // tm,),
        out_shape=(jax.ShapeDtypeStruct((t, D_A), bf16),
                   jax.ShapeDtypeStruct((t, D_B), bf16),
                   jax.ShapeDtypeStruct((t, BLK), bf16),
                   jax.ShapeDtypeStruct((t, BLK), bf16)),
        in_specs=[pl.BlockSpec((tm, D_MODEL), lambda i: (i, 0)),
                  const((1, D_MODEL)), const((D_MODEL, D_IN)),
                  const((1, D_A)), const((1, D_A)),
                  const((A_HEADS, BLK, BLK)), const((A_HEADS, BLK, BLK)),
                  const((1, D_A))],
        out_specs=(pl.BlockSpec((tm, D_A), lambda i: (i, 0)),
                   pl.BlockSpec((tm, D_B), lambda i: (i, 0)),
                   pl.BlockSpec((tm, BLK), lambda i: (i, 0)),
                   pl.BlockSpec((tm, BLK), lambda i: (i, 0))),
        compiler_params=pltpu.CompilerParams(
            dimension_semantics=("parallel",), vmem_limit_bytes=VMEM_LIMIT),
        name="mix_in",
    )(x2, gmix, win, lng, lnb, ws, bs, gout_a)


def _attn_out_body(sink_ref, x_ref, ya_ref, q_ref, kp_ref, km_ref, kn_ref, vp_ref, vm_ref, vn_ref,
                   bias_ref, goutb_ref, wout_ref, gffn_ref, wrh_ref, wrl_ref, br_ref,
                   h_ref, m_ref, re_ref, rw_ref,
                   kf_ref, vf_ref, e_ref, yb_ref, *, seq_len):
    tq = x_ref.shape[1]
    nb = tq // BLK
    ti = pl.program_id(1)

    kf_ref[0:BLK] = kp_ref[0]
    kf_ref[BLK:BLK + tq] = km_ref[0]
    kf_ref[BLK + tq:] = kn_ref[0]
    vf_ref[0:BLK] = vp_ref[0]
    vf_ref[BLK:BLK + tq] = vm_ref[0]
    vf_ref[BLK + tq:] = vn_ref[0]

    row = lax.broadcasted_iota(i32, (BLK, 3 * BLK), 0)
    col = lax.broadcasted_iota(i32, (BLK, 3 * BLK), 1)
    band = jnp.abs(col - BLK - row) <= WINDOW
    lane = lax.broadcasted_iota(i32, (BLK, BLK), 1)
    low_half = lane < HEAD_DIM

    def block(n, carry):
        r0 = pl.multiple_of(n * BLK, BLK)
        qb = q_ref[0, pl.ds(r0, BLK), :]
        kb = kf_ref[pl.ds(r0, 3 * BLK), :]
        vb = vf_ref[pl.ds(r0, 3 * BLK), :]
        zero = jnp.zeros((BLK, BLK), bf16)
        lhs = []
        for kh in range(N_KV):
            for g in range(GQA):
                tile = qb[:, g * BLK:(g + 1) * BLK]
                lhs.append(jnp.where(low_half, tile, zero) if kh == 0
                           else jnp.where(low_half, zero, tile))
        lhs = jnp.concatenate(lhs, axis=0)
        s_all = lax.dot_general(lhs, kb, (((1,), (1,)), ((), ())),
                                preferred_element_type=f32)
        kpos = col + ((ti * nb + n - 1) * BLK)
        valid = band & (kpos >= 0) & (kpos < seq_len)
        rden = []
        for h in range(N_HEADS):
            s = jnp.where(valid, s_all[h * BLK:(h + 1) * BLK] + bias_ref[h], NEG)
            sk = sink_ref[h]
            mrow = jnp.maximum(jnp.max(s, axis=-1, keepdims=True), sk)
            e = jnp.exp(s - mrow)
            den = jnp.sum(e, axis=-1, keepdims=True) + jnp.exp(sk - mrow)
            rden.append(1.0 / den)
            e_ref[h * BLK:(h + 1) * BLK, :] = e.astype(bf16)
        pv = jnp.dot(e_ref[...], vb, preferred_element_type=f32)
        for g in range(GQA):
            o0 = pv[g * BLK:(g + 1) * BLK] * rden[g]
            o1 = pv[(GQA + g) * BLK:(GQA + g + 1) * BLK] * rden[GQA + g]
            yb_ref[pl.ds(r0, BLK), g * BLK:(g + 1) * BLK] = jnp.where(low_half, o0, o1)
        return carry

    lax.fori_loop(0, nb, block, 0)

    ybn = _rms(yb_ref[...], goutb_ref[...]).astype(bf16)
    y = jnp.concatenate([ya_ref[0], ybn], axis=1)
    h = x_ref[0] + jnp.dot(y, wout_ref[...], preferred_element_type=f32)
    h_ref[0] = h
    m = _rms(h, gffn_ref[...])
    m_ref[...] = m.astype(bf16)

    m_hi = m.astype(bf16)
    m_lo = (m - m_hi.astype(f32)).astype(bf16)
    nt = (((1,), (1,)), ((), ()))
    logit = (lax.dot_general(wrh_ref[...], m_hi, nt, preferred_element_type=f32)
             + lax.dot_general(wrh_ref[...], m_lo, nt, preferred_element_type=f32)
             + lax.dot_general(wrl_ref[...], m_hi, nt, preferred_element_type=f32)
             + br_ref[...])
    sub = lax.broadcasted_iota(i32, (8, tq), 0)
    lg = logit[0:8]
    mg = jnp.max(lg, axis=0, keepdims=True)
    pg_top = 1.0 / jnp.sum(jnp.exp(lg - mg), axis=0, keepdims=True)
    g_idx = jnp.min(jnp.where(lg == mg, sub, 8), axis=0, keepdims=True)
    sel = logit[8:16]
    for g in range(1, N_GROUPS):
        sel = jnp.where(g_idx == g, logit[8 + 8 * g:16 + 8 * g], sel)
    m1 = jnp.max(sel, axis=0, keepdims=True)
    i1 = jnp.min(jnp.where(sel == m1, sub, 8), axis=0, keepdims=True)
    sel2 = jnp.where(sub == i1, -jnp.inf, sel)
    m2 = jnp.max(sel2, axis=0, keepdims=True)
    i2 = jnp.min(jnp.where(sel2 == m2, sub, 8), axis=0, keepdims=True)
    r = jnp.exp(m2 - m1)
    w1 = pg_top / (1.0 + r)
    w2 = pg_top * r / (1.0 + r)
    e1 = g_idx * E_PER_GROUP + i1
    e2 = g_idx * E_PER_GROUP + i2
    re_ref[...] = jnp.where(sub == 0, e1, jnp.where(sub == 1, e2, 0))
    rw_ref[...] = jnp.where(sub == 0, w1, jnp.where(sub == 1, w2, 0.0))


def _attn_out(sink, x, ya, q, k, v, bias, goutb, wout, gffn, wrh, wrl, br):
    b, s, _ = x.shape
    tq = TQ_ATT
    nb = tq // BLK
    nblk = s // BLK
    t = b * s
    nq = s // tq
    const = lambda shape: pl.BlockSpec(shape, lambda bi, i, *_: (0,) * len(shape))
    tok = lambda w: pl.BlockSpec((1, tq, w), lambda bi, i, *_: (bi, i, 0))
    prev = pl.BlockSpec((1, BLK, BLK), lambda bi, i, *_: (bi, jnp.maximum(i * nb - 1, 0), 0))
    nxt = pl.BlockSpec((1, BLK, BLK), lambda bi, i, *_: (bi, jnp.minimum(i * nb + nb, nblk - 1), 0))
    rows = pl.BlockSpec((tq, D_MODEL), lambda bi, i, *_: (bi * nq + i, 0))
    lanes = lambda rows: pl.BlockSpec((rows, tq), lambda bi, i, *_: (0, bi * nq + i))
    grid_spec = pltpu.PrefetchScalarGridSpec(
        num_scalar_prefetch=1,
        grid=(b, nq),
        in_specs=[tok(D_MODEL), tok(D_A), tok(D_B),
                  prev, tok(BLK), nxt, prev, tok(BLK), nxt,
                  const((N_HEADS, BLK, 3 * BLK)), const((1, D_B)), const((D_MODEL, D_MODEL)),
                  const((1, D_MODEL)), const((ROUTE_ROWS, D_MODEL)), const((ROUTE_ROWS, D_MODEL)),
                  const((ROUTE_ROWS, tq))],
        out_specs=(tok(D_MODEL), rows, lanes(8), lanes(8)),
        scratch_shapes=[pltpu.VMEM((tq + 2 * BLK, BLK), bf16),
                        pltpu.VMEM((tq + 2 * BLK, BLK), bf16),
                        pltpu.VMEM((N_HEADS * BLK, 3 * BLK), bf16),
                        pltpu.VMEM((tq, D_B), f32)])
    return pl.pallas_call(
        functools.partial(_attn_out_body, seq_len=s),
        grid_spec=grid_spec,
        out_shape=(jax.ShapeDtypeStruct((b, s, D_MODEL), f32),
                   jax.ShapeDtypeStruct((t, D_MODEL), bf16),
                   jax.ShapeDtypeStruct((8, t), i32),
                   jax.ShapeDtypeStruct((8, t), f32)),
        compiler_params=pltpu.CompilerParams(
            dimension_semantics=("parallel", "parallel"), vmem_limit_bytes=VMEM_LIMIT),
        name="attn_out",
    )(sink, x, ya, q, k, k, k, v, v, v, bias, goutb, wout, gffn, wrh, wrl, br)


def _exclusive_prefix(vals):
    a = lax.broadcasted_iota(i32, (N_EXPERTS, N_EXPERTS), 0)
    c = lax.broadcasted_iota(i32, (N_EXPERTS, N_EXPERTS), 1)
    low = jnp.where(c < a, 1.0, 0.0).astype(bf16)
    hi = (vals >> 8).astype(f32).astype(bf16)
    lo = (vals & 255).astype(f32).astype(bf16)
    return (jnp.dot(low, hi, preferred_element_type=f32) * 256.0
            + jnp.dot(low, lo, preferred_element_type=f32)).astype(i32)


def _round_up_pow2(vals, mult):
    log_m = mult.bit_length() - 1
    return ((vals + (mult - 1)) >> log_m) << log_m


def _route_pos_body(re_ref, slot_ref, runs_ref, offs_ref, tot_ref, carry_ref, tri_ref,
                    *, rows_per_tile):
    phase = pl.program_id(0)
    ti = pl.program_id(1)
    tl = re_ref.shape[1]
    eid = lax.broadcasted_iota(i32, (N_EXPERTS, tl), 0)
    hit1 = eid == re_ref[0:1, :]
    hit2 = eid == re_ref[1:2, :]
    onehot = jnp.where(hit1 | hit2, 1.0, 0.0)
    cnt = jnp.sum(onehot, axis=1, keepdims=True).astype(i32)
    run_len = jnp.broadcast_to(_round_up_pow2(cnt, RUN_ALIGN), (N_EXPERTS, BLK))

    @pl.when((phase == 0) & (ti == 0))
    def _():
        tot_ref[...] = jnp.zeros(tot_ref.shape, i32)
        a = lax.broadcasted_iota(i32, (tl, tl), 0)
        c = lax.broadcasted_iota(i32, (tl, tl), 1)
        tri_ref[...] = jnp.where(a < c, 1.0, 0.0).astype(bf16)

    @pl.when(phase == 0)
    def _():
        tot_ref[...] = tot_ref[...] + run_len

    @pl.when((phase == 1) & (ti == 0))
    def _():
        padded = _round_up_pow2(tot_ref[...], rows_per_tile)
        starts = _exclusive_prefix(padded)
        carry_ref[...] = starts
        offs_ref[0] = starts
        offs_ref[1] = padded

    @pl.when(phase == 1)
    def _():
        local_off = _exclusive_prefix(run_len)
        before = jnp.dot(onehot.astype(bf16), tri_ref[...], preferred_element_type=f32)
        slot = before + local_off[:, 0:1].astype(f32)
        s1 = jnp.sum(jnp.where(hit1, slot, 0.0), axis=0, keepdims=True)
        s2 = jnp.sum(jnp.where(hit2, slot, 0.0), axis=0, keepdims=True)
        sub = lax.broadcasted_iota(i32, (8, tl), 0)
        slot_ref[...] = jnp.where(sub == 0, s1.astype(i32), jnp.where(sub == 1, s2.astype(i32), 0))
        runs_ref[0, 0] = local_off
        runs_ref[0, 1] = run_len
        runs_ref[0, 2] = carry_ref[...]
        carry_ref[...] = carry_ref[...] + run_len


def _route_pos(route_e, rows_per_tile):
    t = route_e.shape[1]
    tl = TL_SORT
    return pl.pallas_call(
        functools.partial(_route_pos_body, rows_per_tile=rows_per_tile),
        grid=(2, t // tl),
        out_shape=(jax.ShapeDtypeStruct((8, t), i32),
                   jax.ShapeDtypeStruct((t // tl, 3, N_EXPERTS, BLK), i32),
                   jax.ShapeDtypeStruct((2, N_EXPERTS, BLK), i32)),
        in_specs=[pl.BlockSpec((8, tl), lambda p, i: (0, i))],
        out_specs=(pl.BlockSpec((8, tl), lambda p, i: (0, i * p)),
                   pl.BlockSpec((1, 3, N_EXPERTS, BLK), lambda p, i: (i * p, 0, 0, 0)),
                   pl.BlockSpec((2, N_EXPERTS, BLK), lambda p, i: (0, 0, 0))),
        scratch_shapes=[pltpu.VMEM((N_EXPERTS, BLK), i32),
                        pltpu.VMEM((N_EXPERTS, BLK), i32),
                        pltpu.VMEM((tl, tl), bf16)],
        compiler_params=pltpu.CompilerParams(
            dimension_semantics=("arbitrary", "arbitrary"), vmem_limit_bytes=VMEM_LIMIT),
        name="route_pos",
    )(route_e)


def _for_each_run_piece(runs_ref, fn):
    def per_expert(e, carry):
        local_off = runs_ref[0, 0, e]
        length = runs_ref[0, 0, N_EXPERTS + e]
        global_off = runs_ref[0, 0, 2 * N_EXPERTS + e]
        units = length >> (RUN_ALIGN.bit_length() - 1)
        for b in range(RUN_BITS):
            @pl.when(((units >> b) & 1) == 1)
            def _():
                done = ((units >> (b + 1)) << (b + 1)) * RUN_ALIGN
                fn(pl.multiple_of(local_off + done, RUN_ALIGN),
                   pl.multiple_of(global_off + done, RUN_ALIGN), RUN_ALIGN << b)
        return carry

    lax.fori_loop(0, N_EXPERTS, per_expert, 0)


def _sort_rows_body(ends_ref, padded_ref, runs_ref, m_ref, slot_ref, rw_ref, xs_ref,
                    local_ref, zero_ref, sem_ref, zsem_ref, *, rows_per_tile):
    tl = m_ref.shape[0]
    lb = local_ref.shape[0]
    n_tiles = xs_ref.shape[0] // rows_per_tile
    n_used = ends_ref[N_EXPERTS - 1] // rows_per_tile

    def zero_tile(tile):
        start = pl.multiple_of(tile * rows_per_tile, rows_per_tile)
        return pltpu.make_async_copy(zero_ref, xs_ref.at[pl.ds(start, rows_per_tile), :], zsem_ref)

    @pl.when(pl.program_id(0) == 0)
    def _():
        zero_ref[...] = jnp.zeros(zero_ref.shape, bf16)
        for wait in (False, True):
            for e in range(N_EXPERTS):
                @pl.when(padded_ref[e] > 0)
                def _():
                    cp = zero_tile(ends_ref[e] // rows_per_tile - 1)
                    cp.wait() if wait else cp.start()

            def spare(j, carry):
                cp = zero_tile(n_used + j)
                cp.wait() if wait else cp.start()
                return carry

            lax.fori_loop(0, n_tiles - n_used, spare, 0)

    s1 = slot_ref[0:1, :]
    s2 = slot_ref[1:2, :]
    w1 = rw_ref[0:1, :]
    w2 = rw_ref[1:2, :]
    chunk = 256
    lane = lax.broadcasted_iota(i32, (chunk, BLK), 1)
    for c0 in range(0, lb, chunk):
        srow = lax.broadcasted_iota(i32, (chunk, tl), 0) + c0
        p1 = srow == s1
        p2 = srow == s2
        perm = jnp.where(p1 | p2, 1.0, 0.0).astype(bf16)
        rows = jnp.dot(perm, m_ref[...], preferred_element_type=f32)
        w = jnp.sum(jnp.where(p1, w1, 0.0) + jnp.where(p2, w2, 0.0), axis=1, keepdims=True)
        w_hi = w.astype(bf16).astype(f32)
        w_lo = w - w_hi
        local_ref[c0:c0 + chunk, :D_MODEL] = rows.astype(bf16)
        local_ref[c0:c0 + chunk, D_MODEL:] = jnp.where(
            lane == 0, w_hi, jnp.where(lane == 1, w_lo, 0.0)).astype(bf16)

    def piece(local_row, global_row, rows):
        return pltpu.make_async_copy(local_ref.at[pl.ds(local_row, rows), :],
                                     xs_ref.at[pl.ds(global_row, rows), :], sem_ref)

    _for_each_run_piece(runs_ref, lambda l, g, n: piece(l, g, n).start())
    _for_each_run_piece(runs_ref, lambda l, g, n: piece(l, g, n).wait())


def _sort_rows(ends, padded, run_tiles, m, slot, route_w, total_rows, rows_per_tile):
    t = m.shape[0]
    tl = TL_SORT
    grid_spec = pltpu.PrefetchScalarGridSpec(
        num_scalar_prefetch=2,
        grid=(t // tl,),
        in_specs=[pl.BlockSpec((1, 1, 3 * N_EXPERTS), lambda i, *_: (i, 0, 0),
                               memory_space=pltpu.SMEM),
                  pl.BlockSpec((tl, D_MODEL), lambda i, *_: (i, 0)),
                  pl.BlockSpec((8, tl), lambda i, *_: (0, i)),
                  pl.BlockSpec((8, tl), lambda i, *_: (0, i))],
        out_specs=pl.BlockSpec(memory_space=pl.ANY),
        scratch_shapes=[pltpu.VMEM((LB_SORT, XS_COLS), bf16),
                        pltpu.VMEM((rows_per_tile, XS_COLS), bf16),
                        pltpu.SemaphoreType.DMA(()),
                        pltpu.SemaphoreType.DMA(())])
    return pl.pallas_call(
        functools.partial(_sort_rows_body, rows_per_tile=rows_per_tile),
        grid_spec=grid_spec,
        out_shape=jax.ShapeDtypeStruct((total_rows, XS_COLS), bf16),
        compiler_params=pltpu.CompilerParams(
            dimension_semantics=("arbitrary",), vmem_limit_bytes=VMEM_LIMIT),
        name="sort_rows",
    )(ends, padded, run_tiles, m, slot, route_w)


def _experts_body(tile_ref, texp_ref, nused_ref, xs_ref, wgu_ref, wd_ref, ys_ref):
    @pl.when(pl.program_id(0) < nused_ref[0])
    def _():
        w_row = (xs_ref[:, D_MODEL:D_MODEL + 1].astype(f32)
                 + xs_ref[:, D_MODEL + 1:D_MODEL + 2].astype(f32))
        gu = jnp.dot(xs_ref[:, :D_MODEL], wgu_ref[0], preferred_element_type=f32)
        gate = gu[:, :D_FF_E]
        up = gu[:, D_FF_E:]
        hdn = (gate * jax.nn.sigmoid(gate)) * up
        y = jnp.dot(hdn.astype(bf16), wd_ref[0], preferred_element_type=f32)
        ys_ref[...] = (y * w_row).astype(bf16)

    @pl.when(pl.program_id(0) >= nused_ref[0])
    def _():
        ys_ref[...] = jnp.zeros(ys_ref.shape, bf16)


def _experts(tile_idx, tile_exp, n_used, xs, wgu, wd, rows_per_tile):
    r = rows_per_tile
    n_tiles = xs.shape[0] // r
    grid_spec = pltpu.PrefetchScalarGridSpec(
        num_scalar_prefetch=3,
        grid=(n_tiles,),
        in_specs=[pl.BlockSpec((r, XS_COLS), lambda i, ti, te, nu: (ti[i], 0)),
                  pl.BlockSpec((1, D_MODEL, 2 * D_FF_E), lambda i, ti, te, nu: (te[i], 0, 0)),
                  pl.BlockSpec((1, D_FF_E, D_MODEL), lambda i, ti, te, nu: (te[i], 0, 0))],
        out_specs=pl.BlockSpec((r, D_MODEL), lambda i, ti, te, nu: (i, 0)))
    return pl.pallas_call(
        _experts_body,
        grid_spec=grid_spec,
        out_shape=jax.ShapeDtypeStruct((n_tiles * r, D_MODEL), bf16),
        compiler_params=pltpu.CompilerParams(
            dimension_semantics=("arbitrary",), vmem_limit_bytes=VMEM_LIMIT),
        name="experts",
    )(tile_idx, tile_exp, n_used, xs, wgu, wd)


def _combine_out_body(runs_cur_ref, runs_nxt_ref, h_ref, slot_ref, p_ref, ys_ref,
                      wpg_ref, bpg_ref, wpp_ref, gple_ref, gfin_ref,
                      o_ref, ybuf_ref, sem_ref):
    tk = h_ref.shape[0]
    lb = ybuf_ref.shape[1]
    i = pl.program_id(0)
    n = pl.num_programs(0)
    cur = i % 2

    def piece(s, local_row, global_row, rows):
        return pltpu.make_async_copy(ys_ref.at[pl.ds(global_row, rows), :],
                                     ybuf_ref.at[s, pl.ds(local_row, rows), :], sem_ref.at[s])

    @pl.when(i == 0)
    def _():
        ybuf_ref[...] = jnp.zeros(ybuf_ref.shape, bf16)
        _for_each_run_piece(runs_cur_ref, lambda l, g, r: piece(0, l, g, r).start())

    @pl.when(i + 1 < n)
    def _():
        _for_each_run_piece(runs_nxt_ref, lambda l, g, r: piece(1 - cur, l, g, r).start())

    _for_each_run_piece(runs_cur_ref, lambda l, g, r: piece(cur, l, g, r).wait())

    col = lax.broadcasted_iota(i32, (tk, lb), 1)
    unperm = jnp.where((col == slot_ref[:, 0:1]) | (col == slot_ref[:, 1:2]), 1.0, 0.0).astype(bf16)
    h2 = h_ref[...] + jnp.dot(unperm, ybuf_ref[cur], preferred_element_type=f32)
    gate = jax.nn.sigmoid(jnp.dot(h2.astype(bf16), wpg_ref[...], preferred_element_type=f32)
                          + bpg_ref[...])
    pp = jnp.dot(p_ref[...].astype(bf16), wpp_ref[...], preferred_element_type=f32)
    h3 = h2 + gate * _rms(pp, gple_ref[...])
    o_ref[...] = _rms(h3, gfin_ref[...])


def _combine_out(run_tiles, h2d, slot_t, p2d, ys, wpg, bpg, wpp, gple, gfin):
    t = h2d.shape[0]
    tk = TL_SORT
    nt = t // tk
    const = lambda shape: pl.BlockSpec(shape, lambda i: (0,) * len(shape))
    runs = lambda imap: pl.BlockSpec((1, 1, 3 * N_EXPERTS), imap, memory_space=pltpu.SMEM)
    return pl.pallas_call(
        _combine_out_body,
        grid=(nt,),
        out_shape=jax.ShapeDtypeStruct((t, D_MODEL), f32),
        in_specs=[runs(lambda i: (i, 0, 0)),
                  runs(lambda i: (jnp.minimum(i + 1, nt - 1), 0, 0)),
                  pl.BlockSpec((tk, D_MODEL), lambda i: (i, 0)),
                  pl.BlockSpec((tk, 2), lambda i: (i, 0)),
                  pl.BlockSpec((tk, PLE_DIM), lambda i: (i, 0)),
                  pl.BlockSpec(memory_space=pl.ANY),
                  const((D_MODEL, D_MODEL)), const((1, D_MODEL)),
                  const((PLE_DIM, D_MODEL)), const((1, D_MODEL)), const((1, D_MODEL))],
        out_specs=pl.BlockSpec((tk, D_MODEL), lambda i: (i, 0)),
        scratch_shapes=[pltpu.VMEM((2, LB_SORT, D_MODEL), bf16),
                        pltpu.SemaphoreType.DMA((2,))],
        compiler_params=pltpu.CompilerParams(
            dimension_semantics=("arbitrary",), vmem_limit_bytes=VMEM_LIMIT),
        name="combine_out",
    )(run_tiles, run_tiles, h2d, slot_t, p2d, ys, wpg, bpg, wpp, gple, gfin)


def _q_perm():
    perm = np.empty((D_B,), np.int32)
    for g in range(GQA):
        for kh in range(N_KV):
            for d in range(HEAD_DIM):
                perm[g * BLK + kh * HEAD_DIM + d] = (kh * GQA + g) * HEAD_DIM + d
    return perm


def kernel(x, p, rel_bias, g_mix, w_in, ln_v_g, ln_v_b, w_spatial, b_spatial, sink, g_out_grp, w_out,
           g_ffn, w_router_group, b_router_group, w_router_expert, b_router_expert, w_gate_e, w_up_e,
           w_down_e, w_ple_proj, g_ple, w_ple_gate, b_ple_gate, g_final):
    b, s, d = x.shape
    t = b * s
    depth = g_mix.shape[0]
    assert depth == 1 and d == D_MODEL
    perm = _q_perm()
    c1, c2 = 2 * D_A, 2 * D_A + D_B
    bias = _bias_table(rel_bias)
    hcur = x.astype(f32)
    for li in range(depth):
        wi = w_in[li]
        win = jnp.concatenate([wi[:, :c1], wi[:, c1:c2][:, perm] * (HEAD_DIM ** -0.5), wi[:, c2:]],
                              axis=1).astype(bf16)
        bs = jnp.broadcast_to(b_spatial[li][:, :, None], (A_HEADS, BLK, BLK)).astype(f32)
        gout = g_out_grp[li]
        wo = w_out[li]
        wout = jnp.concatenate([wo[:D_A], wo[D_A:][perm]], axis=0).astype(bf16)
        wr = jnp.zeros((ROUTE_ROWS, D_MODEL), f32)
        wr = wr.at[0:N_GROUPS].set(w_router_group[li].T).at[8:].set(w_router_expert[li].T)
        wrh = wr.astype(bf16)
        wrl = (wr - wrh.astype(f32)).astype(bf16)
        br = jnp.full((ROUTE_ROWS,), NEG, f32)
        br = br.at[0:N_GROUPS].set(b_router_group[li]).at[8:].set(b_router_expert[li])
        br = jnp.broadcast_to(br[:, None], (ROUTE_ROWS, TQ_ATT))
        wgu = jnp.concatenate([w_gate_e[li], w_up_e[li]], axis=2).astype(bf16)
        wd = w_down_e[li].astype(bf16)

        ya, q, k, v = _mix_in(hcur.reshape(t, d), g_mix[li][None], win, ln_v_g[li][None],
                              ln_v_b[li][None], w_spatial[li].astype(bf16), bs, gout[None, :D_A])
        h, m_rows, route_e, route_w = _attn_out(
            sink[li].astype(f32), hcur, ya.reshape(b, s, D_A), q.reshape(b, s, D_B),
            k.reshape(b, s, BLK), v.reshape(b, s, BLK), bias, gout[None, D_A:][:, perm], wout,
            g_ffn[li][None], wrh, wrl, br)

        r = R_EXP
        n_sort = t // TL_SORT
        max_rows = 2 * t + n_sort * N_EXPERTS * (RUN_ALIGN - 1)
        n_tiles = -(-max_rows // r) + N_EXPERTS
        slot, runs, offs = _route_pos(route_e, r)
        starts = offs[0, :, 0]
        padded = offs[1, :, 0]
        ends = starts + padded
        n_used = ends[-1] // r
        tile_idx = jnp.minimum(jnp.arange(n_tiles, dtype=i32), n_used - 1)
        tile_exp = jnp.minimum(
            jnp.sum((ends[None, :] <= (tile_idx * r)[:, None]).astype(i32), axis=1), N_EXPERTS - 1)
        run_tiles = runs[:, :, :, 0].reshape(n_sort, 1, 3 * N_EXPERTS)
        xs = _sort_rows(ends, padded, run_tiles, m_rows, slot, route_w, n_tiles * r, r)
        ys = _experts(tile_idx, tile_exp.astype(i32), n_used.reshape(1), xs, wgu, wd, r)

        out = _combine_out(run_tiles, h.reshape(t, d), slot[:2].T,
                           p[li].reshape(t, PLE_DIM), ys, w_ple_gate[li].astype(bf16),
                           b_ple_gate[li][None], w_ple_proj[li].astype(bf16), g_ple[li][None],
                           g_final[None])
        hcur = out.reshape(b, s, d)
    return hcur
```

```python
import functools
import math

import jax
import jax.numpy as jnp
import numpy as np
from jax import lax
from jax.experimental import pallas as pl
from jax.experimental.pallas import tpu as pltpu

D_MODEL = 1024
D_A = 512
D_B = 512
BLK = 128
A_HEADS = 4
HEAD_DIM = 64
N_HEADS = 8
N_KV = 2
GQA = 4
WINDOW = 128
NUM_BUCKETS = 32
MAX_DIST = 128
D_IN = 2 * D_A + D_B + 2 * N_KV * HEAD_DIM
N_GROUPS = 4
E_PER_GROUP = 8
N_EXPERTS = 32
D_FF_E = 256
PLE_DIM = 256
EPS = 1e-6
NEG = -1e30

TM_MIX = 512
TQ_ATT = 512
TL_SORT = 512
RUN_ALIGN = 16
RUN_BITS = (TL_SORT // RUN_ALIGN).bit_length()
LB_SORT = 2 * TL_SORT + N_EXPERTS * RUN_ALIGN
R_EXP = 512
XS_COLS = D_MODEL + BLK
ROUTE_ROWS = 8 + N_EXPERTS
VMEM_LIMIT = 48 * 1024 * 1024

f32 = jnp.float32
bf16 = jnp.bfloat16
i32 = jnp.int32


def _rms(x, g):
    return x * lax.rsqrt(jnp.mean(x * x, axis=-1, keepdims=True) + EPS) * g


def _gelu_tanh(x):
    c = math.sqrt(2.0 / math.pi)
    return x * (0.5 * (1.0 + jnp.tanh(c * (x + 0.044715 * (x * x * x)))))


def _bucket_table():
    n = NUM_BUCKETS // 2
    max_exact = n // 2
    i = np.arange(BLK)[:, None]
    j = np.arange(3 * BLK)[None, :]
    rel = j - BLK - i
    ret = np.where(rel > 0, n, 0)
    a = np.abs(rel)
    large = max_exact + (np.log(np.maximum(a, 1).astype(np.float64) / max_exact)
                         / math.log(MAX_DIST / max_exact) * (n - max_exact)).astype(np.int32)
    large = np.minimum(large, n - 1)
    return (ret + np.where(a < max_exact, a, large)).astype(np.int32)


def _bias_body(rb_ref, bucket_ref, o_ref):
    bucket = bucket_ref[...]
    o_ref[...] = jnp.zeros(o_ref.shape, f32)

    def step(b, carry):
        hit = bucket == b
        for h in range(N_HEADS):
            o_ref[h] = jnp.where(hit, rb_ref[b, h], o_ref[h])
        return carry

    lax.fori_loop(0, NUM_BUCKETS, step, 0)


def _bias_table(rel_bias):
    bucket = jnp.asarray(_bucket_table())
    return pl.pallas_call(
        _bias_body,
        out_shape=jax.ShapeDtypeStruct((N_HEADS, BLK, 3 * BLK), f32),
        in_specs=[pl.BlockSpec(memory_space=pltpu.SMEM),
                  pl.BlockSpec((BLK, 3 * BLK), lambda: (0, 0))],
        out_specs=pl.BlockSpec((N_HEADS, BLK, 3 * BLK), lambda: (0, 0, 0)),
        name="bias_table",
    )(rel_bias.astype(f32), bucket)


def _mix_in_body(x_ref, gmix_ref, win_ref, lng_ref, lnb_ref, ws_ref, bs_ref, gout_ref,
                 ya_ref, q_ref, k_ref, v_ref):
    tm = x_ref.shape[0]
    nc = tm // BLK
    a = _rms(x_ref[...], gmix_ref[...])
    z = jnp.dot(a.astype(bf16), win_ref[...], preferred_element_type=f32)
    uv = _gelu_tanh(z[:, :2 * D_A])
    u = uv[:, :D_A]
    v = uv[:, D_A:]
    mu = jnp.mean(v, axis=-1, keepdims=True)
    vc = v - mu
    var = jnp.mean(vc * vc, axis=-1, keepdims=True)
    vn = (vc * lax.rsqrt(var + EPS) * lng_ref[...] + lnb_ref[...]).astype(bf16)
    cols = []
    for h in range(A_HEADS):
        rhs = jnp.concatenate(
            [vn[c * BLK:(c + 1) * BLK, h * BLK:(h + 1) * BLK] for c in range(nc)], axis=1)
        r = jnp.dot(ws_ref[h], rhs, preferred_element_type=f32)
        cols.append(jnp.concatenate(
            [r[:, c * BLK:(c + 1) * BLK] + bs_ref[h] for c in range(nc)], axis=0))
    sv = jnp.concatenate(cols, axis=1)
    ya_ref[...] = _rms(u * sv, gout_ref[...]).astype(bf16)
    q_ref[...] = z[:, 2 * D_A:2 * D_A + D_B].astype(bf16)
    k_ref[...] = z[:, 2 * D_A + D_B:2 * D_A + D_B + BLK].astype(bf16)
    v_ref[...] = z[:, 2 * D_A + D_B + BLK:].astype(bf16)


def _mix_in(x2, gmix, win, lng, lnb, ws, bs, gout_a):
    t = x2.shape[0]
    tm = TM_MIX
    const = lambda shape: pl.BlockSpec(shape, lambda i: (0,) * len(shape))
    return pl.pallas_call(
        _mix_in_body,
        grid=(t // tm,),
        out_shape=(jax.ShapeDtypeStruct((t, D_A), bf16),
                   jax.ShapeDtypeStruct((t, D_B), bf16),
                   jax.ShapeDtypeStruct((t, BLK), bf16),
                   jax.ShapeDtypeStruct((t, BLK), bf16)),
        in_specs=[pl.BlockSpec((tm, D_MODEL), lambda i: (i, 0)),
                  const((1, D_MODEL)), const((D_MODEL, D_IN)),
                  const((1, D_A)), const((1, D_A)),
                  const((A_HEADS, BLK, BLK)), const((A_HEADS, BLK, BLK)),
                  const((1, D_A))],
        out_specs=(pl.BlockSpec((tm, D_A), lambda i: (i, 0)),
                   pl.BlockSpec((tm, D_B), lambda i: (i, 0)),
                   pl.BlockSpec((tm, BLK), lambda i: (i, 0)),
                   pl.BlockSpec((tm, BLK), lambda i: (i, 0))),
        compiler_params=pltpu.CompilerParams(
            dimension_semantics=("parallel",), vmem_limit_bytes=VMEM_LIMIT),
        name="mix_in",
    )(x2, gmix, win, lng, lnb, ws, bs, gout_a)


def _attn_out_body(sink_ref, x_ref, ya_ref, q_ref, kp_ref, km_ref, kn_ref, vp_ref, vm_ref, vn_ref,
                   bias_ref, goutb_ref, wout_ref, gffn_ref, wrh_ref, wrl_ref, br_ref,
                   h_ref, m_ref, re_ref, rw_ref,
                   kf_ref, vf_ref, e_ref, yb_ref, *, seq_len):
    tq = x_ref.shape[1]
    nb = tq // BLK
    ti = pl.program_id(1)

    kf_ref[0:BLK] = kp_ref[0]
    kf_ref[BLK:BLK + tq] = km_ref[0]
    kf_ref[BLK + tq:] = kn_ref[0]
    vf_ref[0:BLK] = vp_ref[0]
    vf_ref[BLK:BLK + tq] = vm_ref[0]
    vf_ref[BLK + tq:] = vn_ref[0]

    row = lax.broadcasted_iota(i32, (BLK, 3 * BLK), 0)
    col = lax.broadcasted_iota(i32, (BLK, 3 * BLK), 1)
    band = jnp.abs(col - BLK - row) <= WINDOW
    lane = lax.broadcasted_iota(i32, (BLK, BLK), 1)
    low_half = lane < HEAD_DIM

    def block(n, carry):
        r0 = pl.multiple_of(n * BLK, BLK)
        qb = q_ref[0, pl.ds(r0, BLK), :]
        kb = kf_ref[pl.ds(r0, 3 * BLK), :]
        vb = vf_ref[pl.ds(r0, 3 * BLK), :]
        zero = jnp.zeros((BLK, BLK), bf16)
        lhs = []
        for kh in range(N_KV):
            for g in range(GQA):
                tile = qb[:, g * BLK:(g + 1) * BLK]
                lhs.append(jnp.where(low_half, tile, zero) if kh == 0
                           else jnp.where(low_half, zero, tile))
        lhs = jnp.concatenate(lhs, axis=0)
        s_all = lax.dot_general(lhs, kb, (((1,), (1,)), ((), ())),
                                preferred_element_type=f32)
        kpos = col + ((ti * nb + n - 1) * BLK)
        valid = band & (kpos >= 0) & (kpos < seq_len)
        rden = []
        for h in range(N_HEADS):
            s = jnp.where(valid, s_all[h * BLK:(h + 1) * BLK] + bias_ref[h], NEG)
            sk = sink_ref[h]
            mrow = jnp.maximum(jnp.max(s, axis=-1, keepdims=True), sk)
            e = jnp.exp(s - mrow)
            den = jnp.sum(e, axis=-1, keepdims=True) + jnp.exp(sk - mrow)
            rden.append(1.0 / den)
            e_ref[h * BLK:(h + 1) * BLK, :] = e.astype(bf16)
        pv = jnp.dot(e_ref[...], vb, preferred_element_type=f32)
        for g in range(GQA):
            o0 = pv[g * BLK:(g + 1) * BLK] * rden[g]
            o1 = pv[(GQA + g) * BLK:(GQA + g + 1) * BLK] * rden[GQA + g]
            yb_ref[pl.ds(r0, BLK), g * BLK:(g + 1) * BLK] = jnp.where(low_half, o0, o1)
        return carry

    lax.fori_loop(0, nb, block, 0)

    ybn = _rms(yb_ref[...], goutb_ref[...]).astype(bf16)
    y = jnp.concatenate([ya_ref[0], ybn], axis=1)
    h = x_ref[0] + jnp.dot(y, wout_ref[...], preferred_element_type=f32)
    h_ref[0] = h
    m = _rms(h, gffn_ref[...])
    m_ref[...] = m.astype(bf16)

    m_hi = m.astype(bf16)
    m_lo = (m - m_hi.astype(f32)).astype(bf16)
    nt = (((1,), (1,)), ((), ()))
    logit = (lax.dot_general(wrh_ref[...], m_hi, nt, preferred_element_type=f32)
             + lax.dot_general(wrh_ref[...], m_lo, nt, preferred_element_type=f32)
             + lax.dot_general(wrl_ref[...], m_hi, nt, preferred_element_type=f32)
             + br_ref[...])
    sub = lax.broadcasted_iota(i32, (8, tq), 0)
    lg = logit[0:8]
    mg = jnp.max(lg, axis=0, keepdims=True)
    pg_top = 1.0 / jnp.sum(jnp.exp(lg - mg), axis=0, keepdims=True)
    g_idx = jnp.min(jnp.where(lg == mg, sub, 8), axis=0, keepdims=True)
    sel = logit[8:16]
    for g in range(1, N_GROUPS):
        sel = jnp.where(g_idx == g, logit[8 + 8 * g:16 + 8 * g], sel)
    m1 = jnp.max(sel, axis=0, keepdims=True)
    i1 = jnp.min(jnp.where(sel == m1, sub, 8), axis=0, keepdims=True)
    sel2 = jnp.where(sub == i1, -jnp.inf, sel)
    m2 = jnp.max(sel2, axis=0, keepdims=True)
    i2 = jnp.min(jnp.where(sel2 == m2, sub, 8), axis=0, keepdims=True)
    r = jnp.exp(m2 - m1)
    w1 = pg_top / (1.0 + r)
    w2 = pg_top * r / (1.0 + r)
    e1 = g_idx * E_PER_GROUP + i1
    e2 = g_idx * E_PER_GROUP + i2
    re_ref[...] = jnp.where(sub == 0, e1, jnp.where(sub == 1, e2, 0))
    rw_ref[...] = jnp.where(sub == 0, w1, jnp.where(sub == 1, w2, 0.0))


def _attn_out(sink, x, ya, q, k, v, bias, goutb, wout, gffn, wrh, wrl, br):
    b, s, _ = x.shape
    tq = TQ_ATT
    nb = tq // BLK
    nblk = s // BLK
    t = b * s
    nq = s // tq
    const = lambda shape: pl.BlockSpec(shape, lambda bi, i, *_: (0,) * len(shape))
    tok = lambda w: pl.BlockSpec((1, tq, w), lambda bi, i, *_: (bi, i, 0))
    prev = pl.BlockSpec((1, BLK, BLK), lambda bi, i, *_: (bi, jnp.maximum(i * nb - 1, 0), 0))
    nxt = pl.BlockSpec((1, BLK, BLK), lambda bi, i, *_: (bi, jnp.minimum(i * nb + nb, nblk - 1), 0))
    rows = pl.BlockSpec((tq, D_MODEL), lambda bi, i, *_: (bi * nq + i, 0))
    lanes = lambda rows: pl.BlockSpec((rows, tq), lambda bi, i, *_: (0, bi * nq + i))
    grid_spec = pltpu.PrefetchScalarGridSpec(
        num_scalar_prefetch=1,
        grid=(b, nq),
        in_specs=[tok(D_MODEL), tok(D_A), tok(D_B),
                  prev, tok(BLK), nxt, prev, tok(BLK), nxt,
                  const((N_HEADS, BLK, 3 * BLK)), const((1, D_B)), const((D_MODEL, D_MODEL)),
                  const((1, D_MODEL)), const((ROUTE_ROWS, D_MODEL)), const((ROUTE_ROWS, D_MODEL)),
                  const((ROUTE_ROWS, tq))],
        out_specs=(tok(D_MODEL), rows, lanes(8), lanes(8)),
        scratch_shapes=[pltpu.VMEM((tq + 2 * BLK, BLK), bf16),
                        pltpu.VMEM((tq + 2 * BLK, BLK), bf16),
                        pltpu.VMEM((N_HEADS * BLK, 3 * BLK), bf16),
                        pltpu.VMEM((tq, D_B), f32)])
    return pl.pallas_call(
        functools.partial(_attn_out_body, seq_len=s),
        grid_spec=grid_spec,
        out_shape=(jax.ShapeDtypeStruct((b, s, D_MODEL), f32),
                   jax.ShapeDtypeStruct((t, D_MODEL), bf16),
                   jax.ShapeDtypeStruct((8, t), i32),
                   jax.ShapeDtypeStruct((8, t), f32)),
        compiler_params=pltpu.CompilerParams(
            dimension_semantics=("parallel", "parallel"), vmem_limit_bytes=VMEM_LIMIT),
        name="attn_out",
    )(sink, x, ya, q, k, k, k, v, v, v, bias, goutb, wout, gffn, wrh, wrl, br)


def _exclusive_prefix(vals):
    a = lax.broadcasted_iota(i32, (N_EXPERTS, N_EXPERTS), 0)
    c = lax.broadcasted_iota(i32, (N_EXPERTS, N_EXPERTS), 1)
    low = jnp.where(c < a, 1.0, 0.0).astype(bf16)
    hi = (vals >> 8).astype(f32).astype(bf16)
    lo = (vals & 255).astype(f32).astype(bf16)
    return (jnp.dot(low, hi, preferred_element_type=f32) * 256.0
            + jnp.dot(low, lo, preferred_element_type=f32)).astype(i32)


def _round_up_pow2(vals, mult):
    log_m = mult.bit_length() - 1
    return ((vals + (mult - 1)) >> log_m) << log_m


def _route_pos_body(re_ref, slot_ref, runs_ref, offs_ref, tot_ref, carry_ref, tri_ref,
                    *, rows_per_tile):
    phase = pl.program_id(0)
    ti = pl.program_id(1)
    tl = re_ref.shape[1]
    eid = lax.broadcasted_iota(i32, (N_EXPERTS, tl), 0)
    hit1 = eid == re_ref[0:1, :]
    hit2 = eid == re_ref[1:2, :]
    onehot = jnp.where(hit1 | hit2, 1.0, 0.0)
    cnt = jnp.sum(onehot, axis=1, keepdims=True).astype(i32)
    run_len = jnp.broadcast_to(_round_up_pow2(cnt, RUN_ALIGN), (N_EXPERTS, BLK))

    @pl.when((phase == 0) & (ti == 0))
    def _():
        tot_ref[...] = jnp.zeros(tot_ref.shape, i32)
        a = lax.broadcasted_iota(i32, (tl, tl), 0)
        c = lax.broadcasted_iota(i32, (tl, tl), 1)
        tri_ref[...] = jnp.where(a < c, 1.0, 0.0).astype(bf16)

    @pl.when(phase == 0)
    def _():
        tot_ref[...] = tot_ref[...] + run_len

    @pl.when((phase == 1) & (ti == 0))
    def _():
        padded = _round_up_pow2(tot_ref[...], rows_per_tile)
        starts = _exclusive_prefix(padded)
        carry_ref[...] = starts
        offs_ref[0] = starts
        offs_ref[1] = padded

    @pl.when(phase == 1)
    def _():
        local_off = _exclusive_prefix(run_len)
        before = jnp.dot(onehot.astype(bf16), tri_ref[...], preferred_element_type=f32)
        slot = before + local_off[:, 0:1].astype(f32)
        s1 = jnp.sum(jnp.where(hit1, slot, 0.0), axis=0, keepdims=True)
        s2 = jnp.sum(jnp.where(hit2, slot, 0.0), axis=0, keepdims=True)
        sub = lax.broadcasted_iota(i32, (8, tl), 0)
        slot_ref[...] = jnp.where(sub == 0, s1.astype(i32), jnp.where(sub == 1, s2.astype(i32), 0))
        runs_ref[0, 0] = local_off
        runs_ref[0, 1] = run_len
        runs_ref[0, 2] = carry_ref[...]
        carry_ref[...] = carry_ref[...] + run_len


def _route_pos(route_e, rows_per_tile):
    t = route_e.shape[1]
    tl = TL_SORT
    return pl.pallas_call(
        functools.partial(_route_pos_body, rows_per_tile=rows_per_tile),
        grid=(2, t // tl),
        out_shape=(jax.ShapeDtypeStruct((8, t), i32),
                   jax.ShapeDtypeStruct((t // tl, 3, N_EXPERTS, BLK), i32),
                   jax.ShapeDtypeStruct((2, N_EXPERTS, BLK), i32)),
        in_specs=[pl.BlockSpec((8, tl), lambda p, i: (0, i))],
        out_specs=(pl.BlockSpec((8, tl), lambda p, i: (0, i * p)),
                   pl.BlockSpec((1, 3, N_EXPERTS, BLK), lambda p, i: (i * p, 0, 0, 0)),
                   pl.BlockSpec((2, N_EXPERTS, BLK), lambda p, i: (0, 0, 0))),
        scratch_shapes=[pltpu.VMEM((N_EXPERTS, BLK), i32),
                        pltpu.VMEM((N_EXPERTS, BLK), i32),
                        pltpu.VMEM((tl, tl), bf16)],
        compiler_params=pltpu.CompilerParams(
            dimension_semantics=("arbitrary", "arbitrary"), vmem_limit_bytes=VMEM_LIMIT),
        name="route_pos",
    )(route_e)


def _for_each_run_piece(runs_ref, fn):
    def per_expert(e, carry):
        local_off = runs_ref[0, 0, e]
        length = runs_ref[0, 0, N_EXPERTS + e]
        global_off = runs_ref[0, 0, 2 * N_EXPERTS + e]
        units = length >> (RUN_ALIGN.bit_length() - 1)
        for b in range(RUN_BITS):
            @pl.when(((units >> b) & 1) == 1)
            def _():
                done = ((units >> (b + 1)) << (b + 1)) * RUN_ALIGN
                fn(pl.multiple_of(local_off + done, RUN_ALIGN),
                   pl.multiple_of(global_off + done, RUN_ALIGN), RUN_ALIGN << b)
        return carry

    lax.fori_loop(0, N_EXPERTS, per_expert, 0)


def _tile_run_rows(runs_ref):
    return lax.fori_loop(0, N_EXPERTS, lambda e, acc: acc + runs_ref[0, 0, N_EXPERTS + e], 0)


def _wait_rows(total_rows, make_copy):
    units = total_rows >> (RUN_ALIGN.bit_length() - 1)
    for b in range((LB_SORT // RUN_ALIGN).bit_length()):
        @pl.when(((units >> b) & 1) == 1)
        def _():
            make_copy(RUN_ALIGN << b).wait()


def _sort_rows_body(ends_ref, padded_ref, runs_ref, m_ref, slot_ref, rw_ref, xs_ref,
                    local_ref, zero_ref, pending_ref, sem_ref, zsem_ref, *, rows_per_tile):
    tl = m_ref.shape[0]
    lb = local_ref.shape[1]
    step = pl.program_id(0)
    cur = step % 2
    n_tiles = xs_ref.shape[0] // rows_per_tile
    n_used = ends_ref[N_EXPERTS - 1] // rows_per_tile

    def zero_tile(tile):
        start = pl.multiple_of(tile * rows_per_tile, rows_per_tile)
        return pltpu.make_async_copy(zero_ref, xs_ref.at[pl.ds(start, rows_per_tile), :], zsem_ref)

    @pl.when(pl.program_id(0) == 0)
    def _():
        zero_ref[...] = jnp.zeros(zero_ref.shape, bf16)
        for wait in (False, True):
            for e in range(N_EXPERTS):
                @pl.when(padded_ref[e] > 0)
                def _():
                    cp = zero_tile(ends_ref[e] // rows_per_tile - 1)
                    cp.wait() if wait else cp.start()

            def spare(j, carry):
                cp = zero_tile(n_used + j)
                cp.wait() if wait else cp.start()
                return carry

            lax.fori_loop(0, n_tiles - n_used, spare, 0)

    s1 = slot_ref[0:1, :]
    s2 = slot_ref[1:2, :]
    w1 = rw_ref[0:1, :]
    w2 = rw_ref[1:2, :]
    chunk = 256
    lane = lax.broadcasted_iota(i32, (chunk, BLK), 1)
    for c0 in range(0, lb, chunk):
        srow = lax.broadcasted_iota(i32, (chunk, tl), 0) + c0
        p1 = srow == s1
        p2 = srow == s2
        perm = jnp.where(p1 | p2, 1.0, 0.0).astype(bf16)
        rows = jnp.dot(perm, m_ref[...], preferred_element_type=f32)
        w = jnp.sum(jnp.where(p1, w1, 0.0) + jnp.where(p2, w2, 0.0), axis=1, keepdims=True)
        w_hi = w.astype(bf16).astype(f32)
        w_lo = w - w_hi
        local_ref[cur, c0:c0 + chunk, :D_MODEL] = rows.astype(bf16)
        local_ref[cur, c0:c0 + chunk, D_MODEL:] = jnp.where(
            lane == 0, w_hi, jnp.where(lane == 1, w_lo, 0.0)).astype(bf16)

    def piece(s, local_row, global_row, rows):
        return pltpu.make_async_copy(local_ref.at[s, pl.ds(local_row, rows), :],
                                     xs_ref.at[pl.ds(global_row, rows), :], sem_ref.at[s])

    _for_each_run_piece(runs_ref, lambda l, g, n: piece(cur, l, g, n).start())

    @pl.when(step > 0)
    def _():
        _wait_rows(pending_ref[0], lambda n: piece(1 - cur, 0, 0, n))

    pending_ref[0] = _tile_run_rows(runs_ref)

    @pl.when(step == pl.num_programs(0) - 1)
    def _():
        _wait_rows(pending_ref[0], lambda n: piece(cur, 0, 0, n))


def _sort_rows(ends, padded, run_tiles, m, slot, route_w, total_rows, rows_per_tile):
    t = m.shape[0]
    tl = TL_SORT
    grid_spec = pltpu.PrefetchScalarGridSpec(
        num_scalar_prefetch=2,
        grid=(t // tl,),
        in_specs=[pl.BlockSpec((1, 1, 3 * N_EXPERTS), lambda i, *_: (i, 0, 0),
                               memory_space=pltpu.SMEM),
                  pl.BlockSpec((tl, D_MODEL), lambda i, *_: (i, 0)),
                  pl.BlockSpec((8, tl), lambda i, *_: (0, i)),
                  pl.BlockSpec((8, tl), lambda i, *_: (0, i))],
        out_specs=pl.BlockSpec(memory_space=pl.ANY),
        scratch_shapes=[pltpu.VMEM((2, LB_SORT, XS_COLS), bf16),
                        pltpu.VMEM((rows_per_tile, XS_COLS), bf16),
                        pltpu.SMEM((1,), i32),
                        pltpu.SemaphoreType.DMA((2,)),
                        pltpu.SemaphoreType.DMA(())])
    return pl.pallas_call(
        functools.partial(_sort_rows_body, rows_per_tile=rows_per_tile),
        grid_spec=grid_spec,
        out_shape=jax.ShapeDtypeStruct((total_rows, XS_COLS), bf16),
        compiler_params=pltpu.CompilerParams(
            dimension_semantics=("arbitrary",), vmem_limit_bytes=VMEM_LIMIT),
        name="sort_rows",
    )(ends, padded, run_tiles, m, slot, route_w)


def _experts_body(tile_ref, texp_ref, nused_ref, xs_ref, wgu_ref, wd_ref, ys_ref):
    @pl.when(pl.program_id(0) < nused_ref[0])
    def _():
        w_row = (xs_ref[:, D_MODEL:D_MODEL + 1].astype(f32)
                 + xs_ref[:, D_MODEL + 1:D_MODEL + 2].astype(f32))
        gu = jnp.dot(xs_ref[:, :D_MODEL], wgu_ref[0], preferred_element_type=f32)
        gate = gu[:, :D_FF_E]
        up = gu[:, D_FF_E:]
        hdn = (gate * jax.nn.sigmoid(gate)) * up
        y = jnp.dot(hdn.astype(bf16), wd_ref[0], preferred_element_type=f32)
        ys_ref[...] = (y * w_row).astype(bf16)

    @pl.when(pl.program_id(0) >= nused_ref[0])
    def _():
        ys_ref[...] = jnp.zeros(ys_ref.shape, bf16)


def _experts(tile_idx, tile_exp, n_used, xs, wgu, wd, rows_per_tile):
    r = rows_per_tile
    n_tiles = xs.shape[0] // r
    grid_spec = pltpu.PrefetchScalarGridSpec(
        num_scalar_prefetch=3,
        grid=(n_tiles,),
        in_specs=[pl.BlockSpec((r, XS_COLS), lambda i, ti, te, nu: (ti[i], 0)),
                  pl.BlockSpec((1, D_MODEL, 2 * D_FF_E), lambda i, ti, te, nu: (te[i], 0, 0)),
                  pl.BlockSpec((1, D_FF_E, D_MODEL), lambda i, ti, te, nu: (te[i], 0, 0))],
        out_specs=pl.BlockSpec((r, D_MODEL), lambda i, ti, te, nu: (i, 0)))
    return pl.pallas_call(
        _experts_body,
        grid_spec=grid_spec,
        out_shape=jax.ShapeDtypeStruct((n_tiles * r, D_MODEL), bf16),
        compiler_params=pltpu.CompilerParams(
            dimension_semantics=("arbitrary",), vmem_limit_bytes=VMEM_LIMIT),
        name="experts",
    )(tile_idx, tile_exp, n_used, xs, wgu, wd)


def _combine_out_body(runs_cur_ref, runs_nxt_ref, h_ref, slot_ref, p_ref, ys_ref,
                      wpg_ref, bpg_ref, wpp_ref, gple_ref, gfin_ref,
                      o_ref, ybuf_ref, sem_ref):
    tk = h_ref.shape[0]
    lb = ybuf_ref.shape[1]
    i = pl.program_id(0)
    n = pl.num_programs(0)
    cur = i % 2

    def piece(s, local_row, global_row, rows):
        return pltpu.make_async_copy(ys_ref.at[pl.ds(global_row, rows), :],
                                     ybuf_ref.at[s, pl.ds(local_row, rows), :], sem_ref.at[s])

    @pl.when(i == 0)
    def _():
        ybuf_ref[...] = jnp.zeros(ybuf_ref.shape, bf16)
        _for_each_run_piece(runs_cur_ref, lambda l, g, r: piece(0, l, g, r).start())

    @pl.when(i + 1 < n)
    def _():
        _for_each_run_piece(runs_nxt_ref, lambda l, g, r: piece(1 - cur, l, g, r).start())

    _wait_rows(_tile_run_rows(runs_cur_ref), lambda r: piece(cur, 0, 0, r))

    col = lax.broadcasted_iota(i32, (tk, lb), 1)
    unperm = jnp.where((col == slot_ref[:, 0:1]) | (col == slot_ref[:, 1:2]), 1.0, 0.0).astype(bf16)
    h2 = h_ref[...] + jnp.dot(unperm, ybuf_ref[cur], preferred_element_type=f32)
    gate = jax.nn.sigmoid(jnp.dot(h2.astype(bf16), wpg_ref[...], preferred_element_type=f32)
                          + bpg_ref[...])
    pp = jnp.dot(p_ref[...].astype(bf16), wpp_ref[...], preferred_element_type=f32)
    h3 = h2 + gate * _rms(pp, gple_ref[...])
    o_ref[...] = _rms(h3, gfin_ref[...])


def _combine_out(run_tiles, h2d, slot_t, p2d, ys, wpg, bpg, wpp, gple, gfin):
    t = h2d.shape[0]
    tk = TL_SORT
    nt = t // tk
    const = lambda shape: pl.BlockSpec(shape, lambda i: (0,) * len(shape))
    runs = lambda imap: pl.BlockSpec((1, 1, 3 * N_EXPERTS), imap, memory_space=pltpu.SMEM)
    return pl.pallas_call(
        _combine_out_body,
        grid=(nt,),
        out_shape=jax.ShapeDtypeStruct((t, D_MODEL), f32),
        in_specs=[runs(lambda i: (i, 0, 0)),
                  runs(lambda i: (jnp.minimum(i + 1, nt - 1), 0, 0)),
                  pl.BlockSpec((tk, D_MODEL), lambda i: (i, 0)),
                  pl.BlockSpec((tk, 2), lambda i: (i, 0)),
                  pl.BlockSpec((tk, PLE_DIM), lambda i: (i, 0)),
                  pl.BlockSpec(memory_space=pl.ANY),
                  const((D_MODEL, D_MODEL)), const((1, D_MODEL)),
                  const((PLE_DIM, D_MODEL)), const((1, D_MODEL)), const((1, D_MODEL))],
        out_specs=pl.BlockSpec((tk, D_MODEL), lambda i: (i, 0)),
        scratch_shapes=[pltpu.VMEM((2, LB_SORT, D_MODEL), bf16),
                        pltpu.SemaphoreType.DMA((2,))],
        compiler_params=pltpu.CompilerParams(
            dimension_semantics=("arbitrary",), vmem_limit_bytes=VMEM_LIMIT),
        name="combine_out",
    )(run_tiles, run_tiles, h2d, slot_t, p2d, ys, wpg, bpg, wpp, gple, gfin)


def _q_perm():
    perm = np.empty((D_B,), np.int32)
    for g in range(GQA):
        for kh in range(N_KV):
            for d in range(HEAD_DIM):
                perm[g * BLK + kh * HEAD_DIM + d] = (kh * GQA + g) * HEAD_DIM + d
    return perm


def kernel(x, p, rel_bias, g_mix, w_in, ln_v_g, ln_v_b, w_spatial, b_spatial, sink, g_out_grp, w_out,
           g_ffn, w_router_group, b_router_group, w_router_expert, b_router_expert, w_gate_e, w_up_e,
           w_down_e, w_ple_proj, g_ple, w_ple_gate, b_ple_gate, g_final):
    b, s, d = x.shape
    t = b * s
    depth = g_mix.shape[0]
    assert depth == 1 and d == D_MODEL
    perm = _q_perm()
    c1, c2 = 2 * D_A, 2 * D_A + D_B
    bias = _bias_table(rel_bias)
    hcur = x.astype(f32)
    for li in range(depth):
        wi = w_in[li]
        win = jnp.concatenate([wi[:, :c1], wi[:, c1:c2][:, perm] * (HEAD_DIM ** -0.5), wi[:, c2:]],
                              axis=1).astype(bf16)
        bs = jnp.broadcast_to(b_spatial[li][:, :, None], (A_HEADS, BLK, BLK)).astype(f32)
        gout = g_out_grp[li]
        wo = w_out[li]
        wout = jnp.concatenate([wo[:D_A], wo[D_A:][perm]], axis=0).astype(bf16)
        wr = jnp.zeros((ROUTE_ROWS, D_MODEL), f32)
        wr = wr.at[0:N_GROUPS].set(w_router_group[li].T).at[8:].set(w_router_expert[li].T)
        wrh = wr.astype(bf16)
        wrl = (wr - wrh.astype(f32)).astype(bf16)
        br = jnp.full((ROUTE_ROWS,), NEG, f32)
        br = br.at[0:N_GROUPS].set(b_router_group[li]).at[8:].set(b_router_expert[li])
        br = jnp.broadcast_to(br[:, None], (ROUTE_ROWS, TQ_ATT))
        wgu = jnp.concatenate([w_gate_e[li], w_up_e[li]], axis=2).astype(bf16)
        wd = w_down_e[li].astype(bf16)

        ya, q, k, v = _mix_in(hcur.reshape(t, d), g_mix[li][None], win, ln_v_g[li][None],
                              ln_v_b[li][None], w_spatial[li].astype(bf16), bs, gout[None, :D_A])
        h, m_rows, route_e, route_w = _attn_out(
            sink[li].astype(f32), hcur, ya.reshape(b, s, D_A), q.reshape(b, s, D_B),
            k.reshape(b, s, BLK), v.reshape(b, s, BLK), bias, gout[None, D_A:][:, perm], wout,
            g_ffn[li][None], wrh, wrl, br)

        r = R_EXP
        n_sort = t // TL_SORT
        max_rows = 2 * t + n_sort * N_EXPERTS * (RUN_ALIGN - 1)
        n_tiles = -(-max_rows // r) + N_EXPERTS
        slot, runs, offs = _route_pos(route_e, r)
        starts = offs[0, :, 0]
        padded = offs[1, :, 0]
        ends = starts + padded
        n_used = ends[-1] // r
        tile_idx = jnp.minimum(jnp.arange(n_tiles, dtype=i32), n_used - 1)
        tile_exp = jnp.minimum(
            jnp.sum((ends[None, :] <= (tile_idx * r)[:, None]).astype(i32), axis=1), N_EXPERTS - 1)
        run_tiles = runs[:, :, :, 0].reshape(n_sort, 1, 3 * N_EXPERTS)
        xs = _sort_rows(ends, padded, run_tiles, m_rows, slot, route_w, n_tiles * r, r)
        ys = _experts(tile_idx, tile_exp.astype(i32), n_used.reshape(1), xs, wgu, wd, r)

        out = _combine_out(run_tiles, h.reshape(t, d), slot[:2].T,
                           p[li].reshape(t, PLE_DIM), ys, w_ple_gate[li].astype(bf16),
                           b_ple_gate[li][None], w_ple_proj[li].astype(bf16), g_ple[li][None],
                           g_final[None])
        hcur = out.reshape(b, s, d)
    return hcur
```

```python
import functools
import math

import jax
import jax.numpy as jnp
import numpy as np
from jax import lax
from jax.experimental import pallas as pl
from jax.experimental.pallas import tpu as pltpu

D_MODEL = 1024
D_A = 512
D_B = 512
BLK = 128
A_HEADS = 4
HEAD_DIM = 64
N_HEADS = 8
N_KV = 2
GQA = 4
WINDOW = 128
NUM_BUCKETS = 32
MAX_DIST = 128
D_IN = 2 * D_A + D_B + 2 * N_KV * HEAD_DIM
N_GROUPS = 4
E_PER_GROUP = 8
N_EXPERTS = 32
D_FF_E = 256
PLE_DIM = 256
EPS = 1e-6
NEG = -1e30
LOG2E = math.log2(math.e)

TM_MIX = 512
TQ_ATT = 512
TL_SORT = 512
RUN_ALIGN = 16
RUN_BITS = (TL_SORT // RUN_ALIGN).bit_length()
LB_SORT = 2 * TL_SORT + N_EXPERTS * RUN_ALIGN
R_EXP = 512
XS_COLS = D_MODEL + BLK
ROUTE_ROWS = 8 + N_EXPERTS
VMEM_LIMIT = 48 * 1024 * 1024

f32 = jnp.float32
bf16 = jnp.bfloat16
i32 = jnp.int32


def _rms(x, g):
    return x * lax.rsqrt(jnp.mean(x * x, axis=-1, keepdims=True) + EPS) * g


def _gelu_tanh(x):
    c = math.sqrt(2.0 / math.pi)
    return x * (0.5 * (1.0 + jnp.tanh(c * (x + 0.044715 * (x * x * x)))))


def _bucket_table():
    n = NUM_BUCKETS // 2
    max_exact = n // 2
    i = np.arange(BLK)[:, None]
    j = np.arange(3 * BLK)[None, :]
    rel = j - BLK - i
    ret = np.where(rel > 0, n, 0)
    a = np.abs(rel)
    large = max_exact + (np.log(np.maximum(a, 1).astype(np.float64) / max_exact)
                         / math.log(MAX_DIST / max_exact) * (n - max_exact)).astype(np.int32)
    large = np.minimum(large, n - 1)
    return (ret + np.where(a < max_exact, a, large)).astype(np.int32)


def _bias_body(rb_ref, bucket_ref, o_ref):
    bucket = bucket_ref[...]
    o_ref[...] = jnp.zeros(o_ref.shape, f32)

    def step(b, carry):
        hit = bucket == b
        for h in range(N_HEADS):
            o_ref[h] = jnp.where(hit, rb_ref[b, h], o_ref[h])
        return carry

    lax.fori_loop(0, NUM_BUCKETS, step, 0)


def _bias_table(rel_bias):
    bucket = jnp.asarray(_bucket_table())
    return pl.pallas_call(
        _bias_body,
        out_shape=jax.ShapeDtypeStruct((N_HEADS, BLK, 3 * BLK), f32),
        in_specs=[pl.BlockSpec(memory_space=pltpu.SMEM),
                  pl.BlockSpec((BLK, 3 * BLK), lambda: (0, 0))],
        out_specs=pl.BlockSpec((N_HEADS, BLK, 3 * BLK), lambda: (0, 0, 0)),
        name="bias_table",
    )(rel_bias.astype(f32), bucket)


def _mix_in_body(x_ref, gmix_ref, win_ref, lng_ref, lnb_ref, ws_ref, bs_ref, gout_ref,
                 ya_ref, q_ref, k_ref, v_ref):
    tm = x_ref.shape[0]
    nc = tm // BLK
    a = _rms(x_ref[...], gmix_ref[...])
    z = jnp.dot(a.astype(bf16), win_ref[...], preferred_element_type=f32)
    uv = _gelu_tanh(z[:, :2 * D_A])
    u = uv[:, :D_A]
    v = uv[:, D_A:]
    mu = jnp.mean(v, axis=-1, keepdims=True)
    vc = v - mu
    var = jnp.mean(vc * vc, axis=-1, keepdims=True)
    vn = (vc * lax.rsqrt(var + EPS) * lng_ref[...] + lnb_ref[...]).astype(bf16)
    cols = []
    for h in range(A_HEADS):
        rhs = jnp.concatenate(
            [vn[c * BLK:(c + 1) * BLK, h * BLK:(h + 1) * BLK] for c in range(nc)], axis=1)
        r = jnp.dot(ws_ref[h], rhs, preferred_element_type=f32)
        cols.append(jnp.concatenate(
            [r[:, c * BLK:(c + 1) * BLK] + bs_ref[h] for c in range(nc)], axis=0))
    sv = jnp.concatenate(cols, axis=1)
    ya_ref[...] = _rms(u * sv, gout_ref[...]).astype(bf16)
    q_ref[...] = z[:, 2 * D_A:2 * D_A + D_B].astype(bf16)
    k_ref[...] = z[:, 2 * D_A + D_B:2 * D_A + D_B + BLK].astype(bf16)
    v_ref[...] = z[:, 2 * D_A + D_B + BLK:].astype(bf16)


def _mix_in(x2, gmix, win, lng, lnb, ws, bs, gout_a):
    t = x2.shape[0]
    tm = TM_MIX
    const = lambda shape: pl.BlockSpec(shape, lambda i: (0,) * len(shape))
    return pl.pallas_call(
        _mix_in_body,
        grid=(t // tm,),
        out_shape=(jax.ShapeDtypeStruct((t, D_A), bf16),
                   jax.ShapeDtypeStruct((t, D_B), bf16),
                   jax.ShapeDtypeStruct((t, BLK), bf16),
                   jax.ShapeDtypeStruct((t, BLK), bf16)),
        in_specs=[pl.BlockSpec((tm, D_MODEL), lambda i: (i, 0)),
                  const((1, D_MODEL)), const((D_MODEL, D_IN)),
                  const((1, D_A)), const((1, D_A)),
                  const((A_HEADS, BLK, BLK)), const((A_HEADS, BLK, BLK)),
                  const((1, D_A))],
        out_specs=(pl.BlockSpec((tm, D_A), lambda i: (i, 0)),
                   pl.BlockSpec((tm, D_B), lambda i: (i, 0)),
                   pl.BlockSpec((tm, BLK), lambda i: (i, 0)),
                   pl.BlockSpec((tm, BLK), lambda i: (i, 0))),
        compiler_params=pltpu.CompilerParams(
            dimension_semantics=("parallel",), vmem_limit_bytes=VMEM_LIMIT),
        name="mix_in",
    )(x2, gmix, win, lng, lnb, ws, bs, gout_a)


def _attn_out_body(sink_ref, x_ref, ya_ref, q_ref, kp_ref, km_ref, kn_ref, vp_ref, vm_ref, vn_ref,
                   bias_ref, goutb_ref, wout_ref, gffn_ref, wr_ref, br_ref,
                   h_ref, m_ref, re_ref, rw_ref,
                   kf_ref, vf_ref, e_ref, yb_ref, *, seq_len):
    tq = x_ref.shape[1]
    nb = tq // BLK
    ti = pl.program_id(1)

    kf_ref[0:BLK] = kp_ref[0]
    kf_ref[BLK:BLK + tq] = km_ref[0]
    kf_ref[BLK + tq:] = kn_ref[0]
    vlane = lax.broadcasted_iota(i32, (BLK, BLK), 1)
    for src, r0, r1 in ((vp_ref, 0, BLK), (vn_ref, BLK + tq, tq + 2 * BLK)) + tuple(
            (vm_ref.at[:, pl.ds(c * BLK, BLK), :], BLK + c * BLK, BLK + (c + 1) * BLK)
            for c in range(nb)):
        vals = src[0].astype(f32)
        vf_ref[0, r0:r1] = jnp.where(vlane < HEAD_DIM, vals,
                                     jnp.where(vlane == HEAD_DIM, 1.0, 0.0)).astype(bf16)
        vf_ref[1, r0:r1] = jnp.where(vlane >= HEAD_DIM, vals,
                                     jnp.where(vlane == 0, 1.0, 0.0)).astype(bf16)

    row = lax.broadcasted_iota(i32, (BLK, 3 * BLK), 0)
    col = lax.broadcasted_iota(i32, (BLK, 3 * BLK), 1)
    band = jnp.abs(col - BLK - row) <= WINDOW
    lane = lax.broadcasted_iota(i32, (BLK, BLK), 1)
    low_half = lane < HEAD_DIM

    for n in range(nb):
        r0 = n * BLK
        qb = q_ref[0, pl.ds(r0, BLK), :]
        kb = kf_ref[pl.ds(r0, 3 * BLK), :]
        zero = jnp.zeros((BLK, BLK), bf16)
        lhs = []
        for kh in range(N_KV):
            for g in range(GQA):
                tile = qb[:, g * BLK:(g + 1) * BLK]
                lhs.append(jnp.where(low_half, tile, zero) if kh == 0
                           else jnp.where(low_half, zero, tile))
        lhs = jnp.concatenate(lhs, axis=0)
        s_all = lax.dot_general(lhs, kb, (((1,), (1,)), ((), ())),
                                preferred_element_type=f32)
        kpos = col + ((ti * nb + n - 1) * BLK)
        valid = band & (kpos >= 0) & (kpos < seq_len)
        sink_e = []
        for h in range(N_HEADS):
            s = jnp.where(valid, s_all[h * BLK:(h + 1) * BLK] + bias_ref[h], NEG)
            sk = sink_ref[h]
            mrow = jnp.maximum(jnp.max(s, axis=-1, keepdims=True), sk)
            e_ref[n, h * BLK:(h + 1) * BLK, :] = jnp.exp2(s - mrow).astype(bf16)
            sink_e.append(jnp.exp2(sk - mrow))
        half = GQA * BLK
        pv = [jnp.dot(e_ref[n, kh * half:(kh + 1) * half, :], vf_ref[kh, pl.ds(r0, 3 * BLK), :],
                      preferred_element_type=f32) for kh in range(N_KV)]

        def head_out(h):
            kh, g = divmod(h, GQA)
            rows = pv[kh][g * BLK:(g + 1) * BLK]
            ones_col = HEAD_DIM if kh == 0 else 0
            return rows * (1.0 / (rows[:, ones_col:ones_col + 1] + sink_e[h]))

        for g in range(GQA):
            yb_ref[pl.ds(r0, BLK), g * BLK:(g + 1) * BLK] = jnp.where(
                low_half, head_out(g), head_out(GQA + g))

    ybn = _rms(yb_ref[...], goutb_ref[...]).astype(bf16)
    y = jnp.concatenate([ya_ref[0], ybn], axis=1)
    h = x_ref[0] + jnp.dot(y, wout_ref[...], preferred_element_type=f32)
    h_ref[0] = h
    m = _rms(h, gffn_ref[...])
    m_ref[...] = m.astype(bf16)

    logit_t = jnp.dot(m.astype(bf16), wr_ref[...], preferred_element_type=f32)
    logit = jnp.transpose(logit_t)[:ROUTE_ROWS] + br_ref[...]
    sub = lax.broadcasted_iota(i32, (8, tq), 0)
    lg = logit[0:8]
    mg = jnp.max(lg, axis=0, keepdims=True)
    pg_top = 1.0 / jnp.sum(jnp.exp(lg - mg), axis=0, keepdims=True)
    g_idx = jnp.min(jnp.where(lg == mg, sub, 8), axis=0, keepdims=True)
    sel = logit[8:16]
    for g in range(1, N_GROUPS):
        sel = jnp.where(g_idx == g, logit[8 + 8 * g:16 + 8 * g], sel)
    m1 = jnp.max(sel, axis=0, keepdims=True)
    i1 = jnp.min(jnp.where(sel == m1, sub, 8), axis=0, keepdims=True)
    sel2 = jnp.where(sub == i1, -jnp.inf, sel)
    m2 = jnp.max(sel2, axis=0, keepdims=True)
    i2 = jnp.min(jnp.where(sel2 == m2, sub, 8), axis=0, keepdims=True)
    r = jnp.exp(m2 - m1)
    w1 = pg_top / (1.0 + r)
    w2 = pg_top * r / (1.0 + r)
    e1 = g_idx * E_PER_GROUP + i1
    e2 = g_idx * E_PER_GROUP + i2
    re_ref[...] = jnp.where(sub == 0, e1, jnp.where(sub == 1, e2, 0))
    rw_ref[...] = jnp.where(sub == 0, w1, jnp.where(sub == 1, w2, 0.0))


def _attn_out(sink, x, ya, q, k, v, bias, goutb, wout, gffn, wr, br):
    b, s, _ = x.shape
    tq = TQ_ATT
    nb = tq // BLK
    nblk = s // BLK
    t = b * s
    nq = s // tq
    const = lambda shape: pl.BlockSpec(shape, lambda bi, i, *_: (0,) * len(shape))
    tok = lambda w: pl.BlockSpec((1, tq, w), lambda bi, i, *_: (bi, i, 0))
    prev = pl.BlockSpec((1, BLK, BLK), lambda bi, i, *_: (bi, jnp.maximum(i * nb - 1, 0), 0))
    nxt = pl.BlockSpec((1, BLK, BLK), lambda bi, i, *_: (bi, jnp.minimum(i * nb + nb, nblk - 1), 0))
    rows = pl.BlockSpec((tq, D_MODEL), lambda bi, i, *_: (bi * nq + i, 0))
    lanes = lambda rows: pl.BlockSpec((rows, tq), lambda bi, i, *_: (0, bi * nq + i))
    grid_spec = pltpu.PrefetchScalarGridSpec(
        num_scalar_prefetch=1,
        grid=(b, nq),
        in_specs=[tok(D_MODEL), tok(D_A), tok(D_B),
                  prev, tok(BLK), nxt, prev, tok(BLK), nxt,
                  const((N_HEADS, BLK, 3 * BLK)), const((1, D_B)), const((D_MODEL, D_MODEL)),
                  const((1, D_MODEL)), const((D_MODEL, BLK)), const((ROUTE_ROWS, tq))],
        out_specs=(tok(D_MODEL), rows, lanes(8), lanes(8)),
        scratch_shapes=[pltpu.VMEM((tq + 2 * BLK, BLK), bf16),
                        pltpu.VMEM((N_KV, tq + 2 * BLK, BLK), bf16),
                        pltpu.VMEM((nb, N_HEADS * BLK, 3 * BLK), bf16),
                        pltpu.VMEM((tq, D_B), f32)])
    return pl.pallas_call(
        functools.partial(_attn_out_body, seq_len=s),
        grid_spec=grid_spec,
        out_shape=(jax.ShapeDtypeStruct((b, s, D_MODEL), f32),
                   jax.ShapeDtypeStruct((t, D_MODEL), bf16),
                   jax.ShapeDtypeStruct((8, t), i32),
                   jax.ShapeDtypeStruct((8, t), f32)),
        compiler_params=pltpu.CompilerParams(
            dimension_semantics=("parallel", "parallel"), vmem_limit_bytes=VMEM_LIMIT),
        name="attn_out",
    )(sink, x, ya, q, k, k, k, v, v, v, bias, goutb, wout, gffn, wr, br)


def _exclusive_prefix(vals):
    a = lax.broadcasted_iota(i32, (N_EXPERTS, N_EXPERTS), 0)
    c = lax.broadcasted_iota(i32, (N_EXPERTS, N_EXPERTS), 1)
    low = jnp.where(c < a, 1.0, 0.0).astype(bf16)
    hi = (vals >> 8).astype(f32).astype(bf16)
    lo = (vals & 255).astype(f32).astype(bf16)
    return (jnp.dot(low, hi, preferred_element_type=f32) * 256.0
            + jnp.dot(low, lo, preferred_element_type=f32)).astype(i32)


def _round_up_pow2(vals, mult):
    log_m = mult.bit_length() - 1
    return ((vals + (mult - 1)) >> log_m) << log_m


def _route_pos_body(re_ref, slot_ref, runs_ref, offs_ref, tot_ref, carry_ref, tri_ref,
                    *, rows_per_tile):
    phase = pl.program_id(0)
    ti = pl.program_id(1)
    tl = re_ref.shape[1]
    eid = lax.broadcasted_iota(i32, (N_EXPERTS, tl), 0)
    hit1 = eid == re_ref[0:1, :]
    hit2 = eid == re_ref[1:2, :]
    onehot = jnp.where(hit1 | hit2, 1.0, 0.0)
    cnt = jnp.sum(onehot, axis=1, keepdims=True).astype(i32)
    run_len = jnp.broadcast_to(_round_up_pow2(cnt, RUN_ALIGN), (N_EXPERTS, BLK))

    @pl.when((phase == 0) & (ti == 0))
    def _():
        tot_ref[...] = jnp.zeros(tot_ref.shape, i32)
        a = lax.broadcasted_iota(i32, (tl, tl), 0)
        c = lax.broadcasted_iota(i32, (tl, tl), 1)
        tri_ref[...] = jnp.where(a < c, 1.0, 0.0).astype(bf16)

    @pl.when(phase == 0)
    def _():
        tot_ref[...] = tot_ref[...] + run_len

    @pl.when((phase == 1) & (ti == 0))
    def _():
        padded = _round_up_pow2(tot_ref[...], rows_per_tile)
        starts = _exclusive_prefix(padded)
        carry_ref[...] = starts
        offs_ref[0] = starts
        offs_ref[1] = padded

    @pl.when(phase == 1)
    def _():
        local_off = _exclusive_prefix(run_len)
        before = jnp.dot(onehot.astype(bf16), tri_ref[...], preferred_element_type=f32)
        slot = before + local_off[:, 0:1].astype(f32)
        s1 = jnp.sum(jnp.where(hit1, slot, 0.0), axis=0, keepdims=True)
        s2 = jnp.sum(jnp.where(hit2, slot, 0.0), axis=0, keepdims=True)
        sub = lax.broadcasted_iota(i32, (8, tl), 0)
        slot_ref[...] = jnp.where(sub == 0, s1.astype(i32), jnp.where(sub == 1, s2.astype(i32), 0))
        runs_ref[0, 0] = local_off
        runs_ref[0, 1] = run_len
        runs_ref[0, 2] = carry_ref[...]
        carry_ref[...] = carry_ref[...] + run_len


def _route_pos(route_e, rows_per_tile):
    t = route_e.shape[1]
    tl = TL_SORT
    return pl.pallas_call(
        functools.partial(_route_pos_body, rows_per_tile=rows_per_tile),
        grid=(2, t // tl),
        out_shape=(jax.ShapeDtypeStruct((8, t), i32),
                   jax.ShapeDtypeStruct((t // tl, 3, N_EXPERTS, BLK), i32),
                   jax.ShapeDtypeStruct((2, N_EXPERTS, BLK), i32)),
        in_specs=[pl.BlockSpec((8, tl), lambda p, i: (0, i))],
        out_specs=(pl.BlockSpec((8, tl), lambda p, i: (0, i * p)),
                   pl.BlockSpec((1, 3, N_EXPERTS, BLK), lambda p, i: (i * p, 0, 0, 0)),
                   pl.BlockSpec((2, N_EXPERTS, BLK), lambda p, i: (0, 0, 0))),
        scratch_shapes=[pltpu.VMEM((N_EXPERTS, BLK), i32),
                        pltpu.VMEM((N_EXPERTS, BLK), i32),
                        pltpu.VMEM((tl, tl), bf16)],
        compiler_params=pltpu.CompilerParams(
            dimension_semantics=("arbitrary", "arbitrary"), vmem_limit_bytes=VMEM_LIMIT),
        name="route_pos",
    )(route_e)


def _for_each_run_piece(runs_ref, fn):
    def per_expert(e, carry):
        local_off = runs_ref[0, 0, e]
        length = runs_ref[0, 0, N_EXPERTS + e]
        global_off = runs_ref[0, 0, 2 * N_EXPERTS + e]
        units = length >> (RUN_ALIGN.bit_length() - 1)
        for b in range(RUN_BITS):
            @pl.when(((units >> b) & 1) == 1)
            def _():
                done = ((units >> (b + 1)) << (b + 1)) * RUN_ALIGN
                fn(pl.multiple_of(local_off + done, RUN_ALIGN),
                   pl.multiple_of(global_off + done, RUN_ALIGN), RUN_ALIGN << b)
        return carry

    lax.fori_loop(0, N_EXPERTS, per_expert, 0)


def _tile_run_rows(runs_ref):
    return lax.fori_loop(0, N_EXPERTS, lambda e, acc: acc + runs_ref[0, 0, N_EXPERTS + e], 0)


def _wait_rows(total_rows, make_copy):
    units = total_rows >> (RUN_ALIGN.bit_length() - 1)
    for b in range((LB_SORT // RUN_ALIGN).bit_length()):
        @pl.when(((units >> b) & 1) == 1)
        def _():
            make_copy(RUN_ALIGN << b).wait()


def _sort_rows_body(ends_ref, padded_ref, runs_ref, m_ref, slot_ref, rw_ref, xs_ref,
                    local_ref, zero_ref, pending_ref, sem_ref, zsem_ref, *, rows_per_tile):
    tl = m_ref.shape[0]
    lb = local_ref.shape[1]
    step = pl.program_id(0)
    cur = step % 2
    n_tiles = xs_ref.shape[0] // rows_per_tile
    n_used = ends_ref[N_EXPERTS - 1] // rows_per_tile

    def zero_tile(tile):
        start = pl.multiple_of(tile * rows_per_tile, rows_per_tile)
        return pltpu.make_async_copy(zero_ref, xs_ref.at[pl.ds(start, rows_per_tile), :], zsem_ref)

    @pl.when(pl.program_id(0) == 0)
    def _():
        zero_ref[...] = jnp.zeros(zero_ref.shape, bf16)
        for wait in (False, True):
            for e in range(N_EXPERTS):
                @pl.when(padded_ref[e] > 0)
                def _():
                    cp = zero_tile(ends_ref[e] // rows_per_tile - 1)
                    cp.wait() if wait else cp.start()

            def spare(j, carry):
                cp = zero_tile(n_used + j)
                cp.wait() if wait else cp.start()
                return carry

            lax.fori_loop(0, n_tiles - n_used, spare, 0)

    s1 = slot_ref[0:1, :]
    s2 = slot_ref[1:2, :]
    w1 = rw_ref[0:1, :]
    w2 = rw_ref[1:2, :]
    chunk = 256
    lane = lax.broadcasted_iota(i32, (chunk, BLK), 1)
    run_rows = _tile_run_rows(runs_ref)

    def sort_chunk(c0):
        srow = lax.broadcasted_iota(i32, (chunk, tl), 0) + c0
        p1 = srow == s1
        p2 = srow == s2
        perm = jnp.where(p1 | p2, 1.0, 0.0).astype(bf16)
        rows = jnp.dot(perm, m_ref[...], preferred_element_type=f32)
        w = jnp.sum(jnp.where(p1, w1, 0.0) + jnp.where(p2, w2, 0.0), axis=1, keepdims=True)
        w_hi = w.astype(bf16).astype(f32)
        w_lo = w - w_hi
        local_ref[cur, c0:c0 + chunk, :D_MODEL] = rows.astype(bf16)
        local_ref[cur, c0:c0 + chunk, D_MODEL:] = jnp.where(
            lane == 0, w_hi, jnp.where(lane == 1, w_lo, 0.0)).astype(bf16)

    for c0 in range(0, lb, chunk):
        pl.when(c0 < run_rows)(functools.partial(sort_chunk, c0))

    def piece(s, local_row, global_row, rows):
        return pltpu.make_async_copy(local_ref.at[s, pl.ds(local_row, rows), :],
                                     xs_ref.at[pl.ds(global_row, rows), :], sem_ref.at[s])

    _for_each_run_piece(runs_ref, lambda l, g, n: piece(cur, l, g, n).start())

    @pl.when(step > 0)
    def _():
        _wait_rows(pending_ref[0], lambda n: piece(1 - cur, 0, 0, n))

    pending_ref[0] = run_rows

    @pl.when(step == pl.num_programs(0) - 1)
    def _():
        _wait_rows(pending_ref[0], lambda n: piece(cur, 0, 0, n))


def _sort_rows(ends, padded, run_tiles, m, slot, route_w, total_rows, rows_per_tile):
    t = m.shape[0]
    tl = TL_SORT
    grid_spec = pltpu.PrefetchScalarGridSpec(
        num_scalar_prefetch=2,
        grid=(t // tl,),
        in_specs=[pl.BlockSpec((1, 1, 3 * N_EXPERTS), lambda i, *_: (i, 0, 0),
                               memory_space=pltpu.SMEM),
                  pl.BlockSpec((tl, D_MODEL), lambda i, *_: (i, 0)),
                  pl.BlockSpec((8, tl), lambda i, *_: (0, i)),
                  pl.BlockSpec((8, tl), lambda i, *_: (0, i))],
        out_specs=pl.BlockSpec(memory_space=pl.ANY),
        scratch_shapes=[pltpu.VMEM((2, LB_SORT, XS_COLS), bf16),
                        pltpu.VMEM((rows_per_tile, XS_COLS), bf16),
                        pltpu.SMEM((1,), i32),
                        pltpu.SemaphoreType.DMA((2,)),
                        pltpu.SemaphoreType.DMA(())])
    return pl.pallas_call(
        functools.partial(_sort_rows_body, rows_per_tile=rows_per_tile),
        grid_spec=grid_spec,
        out_shape=jax.ShapeDtypeStruct((total_rows, XS_COLS), bf16),
        compiler_params=pltpu.CompilerParams(
            dimension_semantics=("arbitrary",), vmem_limit_bytes=VMEM_LIMIT),
        name="sort_rows",
    )(ends, padded, run_tiles, m, slot, route_w)


def _experts_body(tile_ref, texp_ref, nused_ref, xs_ref, wg_ref, wu_ref, wd_ref, ys_ref):
    @pl.when(pl.program_id(0) < nused_ref[0])
    def _():
        w_row = (xs_ref[:, D_MODEL:D_MODEL + 1].astype(f32)
                 + xs_ref[:, D_MODEL + 1:D_MODEL + 2].astype(f32))
        gate = jnp.dot(xs_ref[:, :D_MODEL], wg_ref[0], preferred_element_type=f32)
        up = jnp.dot(xs_ref[:, :D_MODEL], wu_ref[0], preferred_element_type=f32)
        hdn = (gate * jax.nn.sigmoid(gate)) * up
        y = jnp.dot(hdn.astype(bf16), wd_ref[0], preferred_element_type=f32)
        ys_ref[...] = (y * w_row).astype(bf16)

    @pl.when(pl.program_id(0) >= nused_ref[0])
    def _():
        ys_ref[...] = jnp.zeros(ys_ref.shape, bf16)


def _experts(tile_idx, tile_exp, n_used, xs, wg, wu, wd, rows_per_tile):
    r = rows_per_tile
    n_tiles = xs.shape[0] // r
    grid_spec = pltpu.PrefetchScalarGridSpec(
        num_scalar_prefetch=3,
        grid=(n_tiles,),
        in_specs=[pl.BlockSpec((r, XS_COLS), lambda i, ti, te, nu: (ti[i], 0)),
                  pl.BlockSpec((1, D_MODEL, D_FF_E), lambda i, ti, te, nu: (te[i], 0, 0)),
                  pl.BlockSpec((1, D_MODEL, D_FF_E), lambda i, ti, te, nu: (te[i], 0, 0)),
                  pl.BlockSpec((1, D_FF_E, D_MODEL), lambda i, ti, te, nu: (te[i], 0, 0))],
        out_specs=pl.BlockSpec((r, D_MODEL), lambda i, ti, te, nu: (i, 0)))
    return pl.pallas_call(
        _experts_body,
        grid_spec=grid_spec,
        out_shape=jax.ShapeDtypeStruct((n_tiles * r, D_MODEL), bf16),
        compiler_params=pltpu.CompilerParams(
            dimension_semantics=("arbitrary",), vmem_limit_bytes=VMEM_LIMIT),
        name="experts",
    )(tile_idx, tile_exp, n_used, xs, wg, wu, wd)


def _combine_out_body(runs_cur_ref, runs_nxt_ref, h_ref, slot_ref, p_ref, ys_ref,
                      wpg_ref, bpg_ref, wpp_ref, gple_ref, gfin_ref,
                      o_ref, ybuf_ref, sem_ref):
    tk = h_ref.shape[0]
    lb = ybuf_ref.shape[1]
    i = pl.program_id(0)
    n = pl.num_programs(0)
    cur = i % 2

    def piece(s, local_row, global_row, rows):
        return pltpu.make_async_copy(ys_ref.at[pl.ds(global_row, rows), :],
                                     ybuf_ref.at[s, pl.ds(local_row, rows), :], sem_ref.at[s])

    @pl.when(i == 0)
    def _():
        ybuf_ref[...] = jnp.zeros(ybuf_ref.shape, bf16)
        _for_each_run_piece(runs_cur_ref, lambda l, g, r: piece(0, l, g, r).start())

    @pl.when(i + 1 < n)
    def _():
        _for_each_run_piece(runs_nxt_ref, lambda l, g, r: piece(1 - cur, l, g, r).start())

    _wait_rows(_tile_run_rows(runs_cur_ref), lambda r: piece(cur, 0, 0, r))

    col = lax.broadcasted_iota(i32, (tk, lb), 1)
    unperm = jnp.where((col == slot_ref[:, 0:1]) | (col == slot_ref[:, 1:2]), 1.0, 0.0).astype(bf16)
    h2 = h_ref[...] + jnp.dot(unperm, ybuf_ref[cur], preferred_element_type=f32)
    gate = jax.nn.sigmoid(jnp.dot(h2.astype(bf16), wpg_ref[...], preferred_element_type=f32)
                          + bpg_ref[...])
    pp = jnp.dot(p_ref[...].astype(bf16), wpp_ref[...], preferred_element_type=f32)
    h3 = h2 + gate * _rms(pp, gple_ref[...])
    o_ref[...] = _rms(h3, gfin_ref[...])


def _combine_out(run_tiles, h2d, slot_t, p2d, ys, wpg, bpg, wpp, gple, gfin):
    t = h2d.shape[0]
    tk = TL_SORT
    nt = t // tk
    const = lambda shape: pl.BlockSpec(shape, lambda i: (0,) * len(shape))
    runs = lambda imap: pl.BlockSpec((1, 1, 3 * N_EXPERTS), imap, memory_space=pltpu.SMEM)
    return pl.pallas_call(
        _combine_out_body,
        grid=(nt,),
        out_shape=jax.ShapeDtypeStruct((t, D_MODEL), f32),
        in_specs=[runs(lambda i: (i, 0, 0)),
                  runs(lambda i: (jnp.minimum(i + 1, nt - 1), 0, 0)),
                  pl.BlockSpec((tk, D_MODEL), lambda i: (i, 0)),
                  pl.BlockSpec((tk, 2), lambda i: (i, 0)),
                  pl.BlockSpec((tk, PLE_DIM), lambda i: (i, 0)),
                  pl.BlockSpec(memory_space=pl.ANY),
                  const((D_MODEL, D_MODEL)), const((1, D_MODEL)),
                  const((PLE_DIM, D_MODEL)), const((1, D_MODEL)), const((1, D_MODEL))],
        out_specs=pl.BlockSpec((tk, D_MODEL), lambda i: (i, 0)),
        scratch_shapes=[pltpu.VMEM((2, LB_SORT, D_MODEL), bf16),
                        pltpu.SemaphoreType.DMA((2,))],
        compiler_params=pltpu.CompilerParams(
            dimension_semantics=("arbitrary",), vmem_limit_bytes=VMEM_LIMIT),
        name="combine_out",
    )(run_tiles, run_tiles, h2d, slot_t, p2d, ys, wpg, bpg, wpp, gple, gfin)


def _q_perm():
    perm = np.empty((D_B,), np.int32)
    for g in range(GQA):
        for kh in range(N_KV):
            for d in range(HEAD_DIM):
                perm[g * BLK + kh * HEAD_DIM + d] = (kh * GQA + g) * HEAD_DIM + d
    return perm


def kernel(x, p, rel_bias, g_mix, w_in, ln_v_g, ln_v_b, w_spatial, b_spatial, sink, g_out_grp, w_out,
           g_ffn, w_router_group, b_router_group, w_router_expert, b_router_expert, w_gate_e, w_up_e,
           w_down_e, w_ple_proj, g_ple, w_ple_gate, b_ple_gate, g_final):
    b, s, d = x.shape
    t = b * s
    depth = g_mix.shape[0]
    assert depth == 1 and d == D_MODEL
    perm = _q_perm()
    c1, c2 = 2 * D_A, 2 * D_A + D_B
    bias = _bias_table(rel_bias * LOG2E)
    hcur = x.astype(f32)
    for li in range(depth):
        wi = w_in[li]
        win = jnp.concatenate([wi[:, :c1], wi[:, c1:c2][:, perm] * (LOG2E * HEAD_DIM ** -0.5), wi[:, c2:]],
                              axis=1).astype(bf16)
        bs = jnp.broadcast_to(b_spatial[li][:, :, None], (A_HEADS, BLK, BLK)).astype(f32)
        gout = g_out_grp[li]
        wo = w_out[li]
        wout = jnp.concatenate([wo[:D_A], wo[D_A:][perm]], axis=0).astype(bf16)
        wr = jnp.zeros((D_MODEL, BLK), f32)
        wr = wr.at[:, 0:N_GROUPS].set(w_router_group[li]).at[:, 8:ROUTE_ROWS].set(w_router_expert[li])
        wr = wr.astype(bf16)
        br = jnp.full((ROUTE_ROWS,), NEG, f32)
        br = br.at[0:N_GROUPS].set(b_router_group[li]).at[8:].set(b_router_expert[li])
        br = jnp.broadcast_to(br[:, None], (ROUTE_ROWS, TQ_ATT))
        wg = w_gate_e[li].astype(bf16)
        wu = w_up_e[li].astype(bf16)
        wd = w_down_e[li].astype(bf16)

        ya, q, k, v = _mix_in(hcur.reshape(t, d), g_mix[li][None], win, ln_v_g[li][None],
                              ln_v_b[li][None], w_spatial[li].astype(bf16), bs, gout[None, :D_A])
        h, m_rows, route_e, route_w = _attn_out(
            sink[li].astype(f32) * LOG2E, hcur, ya.reshape(b, s, D_A), q.reshape(b, s, D_B),
            k.reshape(b, s, BLK), v.reshape(b, s, BLK), bias, gout[None, D_A:][:, perm], wout,
            g_ffn[li][None], wr, br)

        r = R_EXP
        n_sort = t // TL_SORT
        max_rows = 2 * t + n_sort * N_EXPERTS * (RUN_ALIGN - 1)
        n_tiles = -(-max_rows // r) + N_EXPERTS
        slot, runs, offs = _route_pos(route_e, r)
        starts = offs[0, :, 0]
        padded = offs[1, :, 0]
        ends = starts + padded
        n_used = ends[-1] // r
        tile_idx = jnp.minimum(jnp.arange(n_tiles, dtype=i32), n_used - 1)
        tile_exp = jnp.minimum(
            jnp.sum((ends[None, :] <= (tile_idx * r)[:, None]).astype(i32), axis=1), N_EXPERTS - 1)
        run_tiles = runs[:, :, :, 0].reshape(n_sort, 1, 3 * N_EXPERTS)
        xs = _sort_rows(ends, padded, run_tiles, m_rows, slot, route_w, n_tiles * r, r)
        ys = _experts(tile_idx, tile_exp.astype(i32), n_used.reshape(1), xs, wg, wu, wd, r)

        out = _combine_out(run_tiles, h.reshape(t, d), slot[:2].T,
                           p[li].reshape(t, PLE_DIM), ys, w_ple_gate[li].astype(bf16),
                           b_ple_gate[li][None], w_ple_proj[li].astype(bf16), g_ple[li][None],
                           g_final[None])
        hcur = out.reshape(b, s, d)
    return hcur
```

```python
import functools
import math

import jax
import jax.numpy as jnp
import numpy as np
from jax import lax
from jax.experimental import pallas as pl
from jax.experimental.pallas import tpu as pltpu

D_MODEL = 1024
D_A = 512
D_B = 512
BLK = 128
A_HEADS = 4
HEAD_DIM = 64
N_HEADS = 8
N_KV = 2
GQA = 4
WINDOW = 128
NUM_BUCKETS = 32
MAX_DIST = 128
D_IN = 2 * D_A + D_B + 2 * N_KV * HEAD_DIM
N_GROUPS = 4
E_PER_GROUP = 8
N_EXPERTS = 32
D_FF_E = 256
PLE_DIM = 256
EPS = 1e-6
NEG = -1e30
LOG2E = math.log2(math.e)

TM_MIX = 512
TQ_ATT = 512
TL_SORT = 512
RUN_ALIGN = 16
RUN_BITS = (TL_SORT // RUN_ALIGN).bit_length()
LB_SORT = 2 * TL_SORT + N_EXPERTS * RUN_ALIGN
R_EXP = 512
XS_COLS = D_MODEL + BLK
ROUTE_ROWS = 8 + N_EXPERTS
VMEM_LIMIT = 48 * 1024 * 1024

f32 = jnp.float32
bf16 = jnp.bfloat16
i32 = jnp.int32


def _rms(x, g):
    return x * lax.rsqrt(jnp.mean(x * x, axis=-1, keepdims=True) + EPS) * g


def _gelu_tanh(x):
    c = math.sqrt(2.0 / math.pi)
    return x * (0.5 * (1.0 + jnp.tanh(c * (x + 0.044715 * (x * x * x)))))


def _bucket_table():
    n = NUM_BUCKETS // 2
    max_exact = n // 2
    i = np.arange(BLK)[:, None]
    j = np.arange(3 * BLK)[None, :]
    rel = j - BLK - i
    ret = np.where(rel > 0, n, 0)
    a = np.abs(rel)
    large = max_exact + (np.log(np.maximum(a, 1).astype(np.float64) / max_exact)
                         / math.log(MAX_DIST / max_exact) * (n - max_exact)).astype(np.int32)
    large = np.minimum(large, n - 1)
    return (ret + np.where(a < max_exact, a, large)).astype(np.int32)


def _bias_body(rb_ref, bucket_ref, o_ref):
    bucket = bucket_ref[...]
    o_ref[...] = jnp.zeros(o_ref.shape, f32)

    def step(b, carry):
        hit = bucket == b
        for h in range(N_HEADS):
            o_ref[h] = jnp.where(hit, rb_ref[b, h], o_ref[h])
        return carry

    lax.fori_loop(0, NUM_BUCKETS, step, 0)


def _bias_table(rel_bias):
    bucket = jnp.asarray(_bucket_table())
    return pl.pallas_call(
        _bias_body,
        out_shape=jax.ShapeDtypeStruct((N_HEADS, BLK, 3 * BLK), f32),
        in_specs=[pl.BlockSpec(memory_space=pltpu.SMEM),
                  pl.BlockSpec((BLK, 3 * BLK), lambda: (0, 0))],
        out_specs=pl.BlockSpec((N_HEADS, BLK, 3 * BLK), lambda: (0, 0, 0)),
        name="bias_table",
    )(rel_bias.astype(f32), bucket)


def _mix_in_body(x_ref, gmix_ref, win_ref, lng_ref, lnb_ref, ws_ref, bs_ref, gout_ref,
                 ya_ref, q_ref, k_ref, v_ref):
    tm = x_ref.shape[0]
    nc = tm // BLK
    a = _rms(x_ref[...], gmix_ref[...])
    z = jnp.dot(a.astype(bf16), win_ref[...], preferred_element_type=f32)
    uv = _gelu_tanh(z[:, :2 * D_A])
    u = uv[:, :D_A]
    v = uv[:, D_A:]
    mu = jnp.mean(v, axis=-1, keepdims=True)
    vc = v - mu
    var = jnp.mean(vc * vc, axis=-1, keepdims=True)
    vn = (vc * lax.rsqrt(var + EPS) * lng_ref[...] + lnb_ref[...]).astype(bf16)
    cols = []
    for h in range(A_HEADS):
        rhs = jnp.concatenate(
            [vn[c * BLK:(c + 1) * BLK, h * BLK:(h + 1) * BLK] for c in range(nc)], axis=1)
        r = jnp.dot(ws_ref[h], rhs, preferred_element_type=f32)
        cols.append(jnp.concatenate(
            [r[:, c * BLK:(c + 1) * BLK] + bs_ref[h] for c in range(nc)], axis=0))
    sv = jnp.concatenate(cols, axis=1)
    ya_ref[...] = _rms(u * sv, gout_ref[...]).astype(bf16)
    q_ref[...] = z[:, 2 * D_A:2 * D_A + D_B].astype(bf16)
    k_ref[...] = z[:, 2 * D_A + D_B:2 * D_A + D_B + BLK].astype(bf16)
    v_ref[...] = z[:, 2 * D_A + D_B + BLK:].astype(bf16)


def _mix_in(x2, gmix, win, lng, lnb, ws, bs, gout_a):
    t = x2.shape[0]
    tm = TM_MIX
    const = lambda shape: pl.BlockSpec(shape, lambda i: (0,) * len(shape))
    return pl.pallas_call(
        _mix_in_body,
        grid=(t // tm,),
        out_shape=(jax.ShapeDtypeStruct((t, D_A), bf16),
                   jax.ShapeDtypeStruct((t, D_B), bf16),
                   jax.ShapeDtypeStruct((t, BLK), bf16),
                   jax.ShapeDtypeStruct((t, BLK), bf16)),
        in_specs=[pl.BlockSpec((tm, D_MODEL), lambda i: (i, 0)),
                  const((1, D_MODEL)), const((D_MODEL, D_IN)),
                  const((1, D_A)), const((1, D_A)),
                  const((A_HEADS, BLK, BLK)), const((A_HEADS, BLK, BLK)),
                  const((1, D_A))],
        out_specs=(pl.BlockSpec((tm, D_A), lambda i: (i, 0)),
                   pl.BlockSpec((tm, D_B), lambda i: (i, 0)),
                   pl.BlockSpec((tm, BLK), lambda i: (i, 0)),
                   pl.BlockSpec((tm, BLK), lambda i: (i, 0))),
        compiler_params=pltpu.CompilerParams(
            dimension_semantics=("parallel",), vmem_limit_bytes=VMEM_LIMIT),
        name="mix_in",
    )(x2, gmix, win, lng, lnb, ws, bs, gout_a)


def _attn_out_body(sink_ref, x_ref, ya_ref, q_ref, kp_ref, km_ref, kn_ref, vp_ref, vm_ref, vn_ref,
                   bias_ref, goutb_ref, wout_ref, gffn_ref, wr_ref, br_ref,
                   h_ref, m_ref, re_ref, rw_ref,
                   kf_ref, vf_ref, e_ref, yb_ref, *, seq_len):
    tq = x_ref.shape[1]
    nb = tq // BLK
    ti = pl.program_id(1)

    kf_ref[0:BLK] = kp_ref[0]
    kf_ref[BLK:BLK + tq] = km_ref[0]
    kf_ref[BLK + tq:] = kn_ref[0]
    vlane = lax.broadcasted_iota(i32, (BLK, BLK), 1)
    for src, r0, r1 in ((vp_ref, 0, BLK), (vn_ref, BLK + tq, tq + 2 * BLK)) + tuple(
            (vm_ref.at[:, pl.ds(c * BLK, BLK), :], BLK + c * BLK, BLK + (c + 1) * BLK)
            for c in range(nb)):
        vals = src[0].astype(f32)
        vf_ref[0, r0:r1] = jnp.where(vlane < HEAD_DIM, vals,
                                     jnp.where(vlane == HEAD_DIM, 1.0, 0.0)).astype(bf16)
        vf_ref[1, r0:r1] = jnp.where(vlane >= HEAD_DIM, vals,
                                     jnp.where(vlane == 0, 1.0, 0.0)).astype(bf16)

    row = lax.broadcasted_iota(i32, (BLK, 3 * BLK), 0)
    col = lax.broadcasted_iota(i32, (BLK, 3 * BLK), 1)
    band = jnp.abs(col - BLK - row) <= WINDOW
    lane = lax.broadcasted_iota(i32, (BLK, BLK), 1)
    low_half = lane < HEAD_DIM

    for n in range(nb):
        r0 = n * BLK
        qb = q_ref[0, pl.ds(r0, BLK), :]
        kb = kf_ref[pl.ds(r0, 3 * BLK), :]
        zero = jnp.zeros((BLK, BLK), bf16)
        lhs = []
        for kh in range(N_KV):
            for g in range(GQA):
                tile = qb[:, g * BLK:(g + 1) * BLK]
                lhs.append(jnp.where(low_half, tile, zero) if kh == 0
                           else jnp.where(low_half, zero, tile))
        lhs = jnp.concatenate(lhs, axis=0)
        s_all = lax.dot_general(lhs, kb, (((1,), (1,)), ((), ())),
                                preferred_element_type=f32)
        kpos = col + ((ti * nb + n - 1) * BLK)
        valid = band & (kpos >= 0) & (kpos < seq_len)
        sink_e = []
        for h in range(N_HEADS):
            s = jnp.where(valid, s_all[h * BLK:(h + 1) * BLK] + bias_ref[h], NEG)
            sk = sink_ref[h]
            mrow = jnp.maximum(jnp.max(s, axis=-1, keepdims=True), sk)
            e_ref[n, h * BLK:(h + 1) * BLK, :] = jnp.exp2(s - mrow).astype(bf16)
            sink_e.append(jnp.exp2(sk - mrow))
        half = GQA * BLK
        pv = [jnp.dot(e_ref[n, kh * half:(kh + 1) * half, :], vf_ref[kh, pl.ds(r0, 3 * BLK), :],
                      preferred_element_type=f32) for kh in range(N_KV)]

        def head_out(h):
            kh, g = divmod(h, GQA)
            rows = pv[kh][g * BLK:(g + 1) * BLK]
            ones_col = HEAD_DIM if kh == 0 else 0
            return rows * (1.0 / (rows[:, ones_col:ones_col + 1] + sink_e[h]))

        for g in range(GQA):
            yb_ref[pl.ds(r0, BLK), g * BLK:(g + 1) * BLK] = jnp.where(
                low_half, head_out(g), head_out(GQA + g))

    ybn = _rms(yb_ref[...], goutb_ref[...]).astype(bf16)
    y = jnp.concatenate([ya_ref[0], ybn], axis=1)
    h = x_ref[0] + jnp.dot(y, wout_ref[...], preferred_element_type=f32)
    h_ref[0] = h
    m = _rms(h, gffn_ref[...])
    m_ref[...] = m.astype(bf16)

    logit_t = jnp.dot(m.astype(bf16), wr_ref[...], preferred_element_type=f32)
    logit = jnp.transpose(logit_t)[:ROUTE_ROWS] + br_ref[...]
    sub = lax.broadcasted_iota(i32, (8, tq), 0)
    lg = logit[0:8]
    mg = jnp.max(lg, axis=0, keepdims=True)
    pg_top = 1.0 / jnp.sum(jnp.exp(lg - mg), axis=0, keepdims=True)
    g_idx = jnp.min(jnp.where(lg == mg, sub, 8), axis=0, keepdims=True)
    sel = logit[8:16]
    for g in range(1, N_GROUPS):
        sel = jnp.where(g_idx == g, logit[8 + 8 * g:16 + 8 * g], sel)
    m1 = jnp.max(sel, axis=0, keepdims=True)
    i1 = jnp.min(jnp.where(sel == m1, sub, 8), axis=0, keepdims=True)
    sel2 = jnp.where(sub == i1, -jnp.inf, sel)
    m2 = jnp.max(sel2, axis=0, keepdims=True)
    i2 = jnp.min(jnp.where(sel2 == m2, sub, 8), axis=0, keepdims=True)
    r = jnp.exp(m2 - m1)
    w1 = pg_top / (1.0 + r)
    w2 = pg_top * r / (1.0 + r)
    e1 = g_idx * E_PER_GROUP + i1
    e2 = g_idx * E_PER_GROUP + i2
    re_ref[...] = jnp.where(sub == 0, e1, jnp.where(sub == 1, e2, 0))
    rw_ref[...] = jnp.where(sub == 0, w1, jnp.where(sub == 1, w2, 0.0))


def _attn_out(sink, x, ya, q, k, v, bias, goutb, wout, gffn, wr, br):
    b, s, _ = x.shape
    tq = TQ_ATT
    nb = tq // BLK
    nblk = s // BLK
    t = b * s
    nq = s // tq
    const = lambda shape: pl.BlockSpec(shape, lambda bi, i, *_: (0,) * len(shape))
    tok = lambda w: pl.BlockSpec((1, tq, w), lambda bi, i, *_: (bi, i, 0))
    prev = pl.BlockSpec((1, BLK, BLK), lambda bi, i, *_: (bi, jnp.maximum(i * nb - 1, 0), 0))
    nxt = pl.BlockSpec((1, BLK, BLK), lambda bi, i, *_: (bi, jnp.minimum(i * nb + nb, nblk - 1), 0))
    rows = pl.BlockSpec((tq, D_MODEL), lambda bi, i, *_: (bi * nq + i, 0))
    lanes = lambda rows: pl.BlockSpec((rows, tq), lambda bi, i, *_: (0, bi * nq + i))
    grid_spec = pltpu.PrefetchScalarGridSpec(
        num_scalar_prefetch=1,
        grid=(b, nq),
        in_specs=[tok(D_MODEL), tok(D_A), tok(D_B),
                  prev, tok(BLK), nxt, prev, tok(BLK), nxt,
                  const((N_HEADS, BLK, 3 * BLK)), const((1, D_B)), const((D_MODEL, D_MODEL)),
                  const((1, D_MODEL)), const((D_MODEL, BLK)), const((ROUTE_ROWS, tq))],
        out_specs=(tok(D_MODEL), rows, lanes(8), lanes(8)),
        scratch_shapes=[pltpu.VMEM((tq + 2 * BLK, BLK), bf16),
                        pltpu.VMEM((N_KV, tq + 2 * BLK, BLK), bf16),
                        pltpu.VMEM((nb, N_HEADS * BLK, 3 * BLK), bf16),
                        pltpu.VMEM((tq, D_B), f32)])
    return pl.pallas_call(
        functools.partial(_attn_out_body, seq_len=s),
        grid_spec=grid_spec,
        out_shape=(jax.ShapeDtypeStruct((b, s, D_MODEL), f32),
                   jax.ShapeDtypeStruct((t, D_MODEL), bf16),
                   jax.ShapeDtypeStruct((8, t), i32),
                   jax.ShapeDtypeStruct((8, t), f32)),
        compiler_params=pltpu.CompilerParams(
            dimension_semantics=("parallel", "parallel"), vmem_limit_bytes=VMEM_LIMIT),
        name="attn_out",
    )(sink, x, ya, q, k, k, k, v, v, v, bias, goutb, wout, gffn, wr, br)


def _exclusive_prefix(vals):
    a = lax.broadcasted_iota(i32, (N_EXPERTS, N_EXPERTS), 0)
    c = lax.broadcasted_iota(i32, (N_EXPERTS, N_EXPERTS), 1)
    low = jnp.where(c < a, 1.0, 0.0).astype(bf16)
    hi = (vals >> 8).astype(f32).astype(bf16)
    lo = (vals & 255).astype(f32).astype(bf16)
    return (jnp.dot(low, hi, preferred_element_type=f32) * 256.0
            + jnp.dot(low, lo, preferred_element_type=f32)).astype(i32)


def _round_up_pow2(vals, mult):
    log_m = mult.bit_length() - 1
    return ((vals + (mult - 1)) >> log_m) << log_m


def _route_pos_body(re_ref, slot_ref, runs_ref, offs_ref, tot_ref, carry_ref, tri_ref,
                    *, rows_per_tile):
    phase = pl.program_id(0)
    ti = pl.program_id(1)
    tl = re_ref.shape[1]
    eid = lax.broadcasted_iota(i32, (N_EXPERTS, tl), 0)
    hit1 = eid == re_ref[0:1, :]
    hit2 = eid == re_ref[1:2, :]
    onehot = jnp.where(hit1 | hit2, 1.0, 0.0)
    cnt = jnp.sum(onehot, axis=1, keepdims=True).astype(i32)
    run_len = jnp.broadcast_to(_round_up_pow2(cnt, RUN_ALIGN), (N_EXPERTS, BLK))

    @pl.when((phase == 0) & (ti == 0))
    def _():
        tot_ref[...] = jnp.zeros(tot_ref.shape, i32)
        a = lax.broadcasted_iota(i32, (tl, tl), 0)
        c = lax.broadcasted_iota(i32, (tl, tl), 1)
        tri_ref[...] = jnp.where(a < c, 1.0, 0.0).astype(bf16)

    @pl.when(phase == 0)
    def _():
        tot_ref[...] = tot_ref[...] + run_len

    @pl.when((phase == 1) & (ti == 0))
    def _():
        padded = _round_up_pow2(tot_ref[...], rows_per_tile)
        starts = _exclusive_prefix(padded)
        carry_ref[...] = starts
        offs_ref[0] = starts
        offs_ref[1] = padded

    @pl.when(phase == 1)
    def _():
        local_off = _exclusive_prefix(run_len)
        before = jnp.dot(onehot.astype(bf16), tri_ref[...], preferred_element_type=f32)
        slot = before + local_off[:, 0:1].astype(f32)
        s1 = jnp.sum(jnp.where(hit1, slot, 0.0), axis=0, keepdims=True)
        s2 = jnp.sum(jnp.where(hit2, slot, 0.0), axis=0, keepdims=True)
        sub = lax.broadcasted_iota(i32, (8, tl), 0)
        slot_ref[...] = jnp.where(sub == 0, s1.astype(i32), jnp.where(sub == 1, s2.astype(i32), 0))
        runs_ref[0, 0] = local_off
        runs_ref[0, 1] = run_len
        runs_ref[0, 2] = carry_ref[...]
        carry_ref[...] = carry_ref[...] + run_len


def _route_pos(route_e, rows_per_tile):
    t = route_e.shape[1]
    tl = TL_SORT
    return pl.pallas_call(
        functools.partial(_route_pos_body, rows_per_tile=rows_per_tile),
        grid=(2, t // tl),
        out_shape=(jax.ShapeDtypeStruct((8, t), i32),
                   jax.ShapeDtypeStruct((t // tl, 3, N_EXPERTS, BLK), i32),
                   jax.ShapeDtypeStruct((2, N_EXPERTS, BLK), i32)),
        in_specs=[pl.BlockSpec((8, tl), lambda p, i: (0, i))],
        out_specs=(pl.BlockSpec((8, tl), lambda p, i: (0, i * p)),
                   pl.BlockSpec((1, 3, N_EXPERTS, BLK), lambda p, i: (i * p, 0, 0, 0)),
                   pl.BlockSpec((2, N_EXPERTS, BLK), lambda p, i: (0, 0, 0))),
        scratch_shapes=[pltpu.VMEM((N_EXPERTS, BLK), i32),
                        pltpu.VMEM((N_EXPERTS, BLK), i32),
                        pltpu.VMEM((tl, tl), bf16)],
        compiler_params=pltpu.CompilerParams(
            dimension_semantics=("arbitrary", "arbitrary"), vmem_limit_bytes=VMEM_LIMIT),
        name="route_pos",
    )(route_e)


def _for_each_run_piece(runs_ref, fn):
    def per_expert(e, carry):
        local_off = runs_ref[0, 0, e]
        length = runs_ref[0, 0, N_EXPERTS + e]
        global_off = runs_ref[0, 0, 2 * N_EXPERTS + e]
        units = length >> (RUN_ALIGN.bit_length() - 1)
        for b in range(RUN_BITS):
            @pl.when(((units >> b) & 1) == 1)
            def _():
                done = ((units >> (b + 1)) << (b + 1)) * RUN_ALIGN
                fn(pl.multiple_of(local_off + done, RUN_ALIGN),
                   pl.multiple_of(global_off + done, RUN_ALIGN), RUN_ALIGN << b)
        return carry

    lax.fori_loop(0, N_EXPERTS, per_expert, 0)


def _tile_run_rows(runs_ref):
    return lax.fori_loop(0, N_EXPERTS, lambda e, acc: acc + runs_ref[0, 0, N_EXPERTS + e], 0)


def _wait_rows(total_rows, make_copy):
    units = total_rows >> (RUN_ALIGN.bit_length() - 1)
    for b in range((LB_SORT // RUN_ALIGN).bit_length()):
        @pl.when(((units >> b) & 1) == 1)
        def _():
            make_copy(RUN_ALIGN << b).wait()


def _sort_rows_body(ends_ref, padded_ref, runs_ref, m_ref, slot_ref, rw_ref, xs_ref,
                    local_ref, zero_ref, pending_ref, sem_ref, zsem_ref, *, rows_per_tile):
    tl = m_ref.shape[0]
    lb = local_ref.shape[1]
    step = pl.program_id(0)
    cur = step % 2
    n_tiles = xs_ref.shape[0] // rows_per_tile
    n_used = ends_ref[N_EXPERTS - 1] // rows_per_tile

    def zero_tile(tile):
        start = pl.multiple_of(tile * rows_per_tile, rows_per_tile)
        return pltpu.make_async_copy(zero_ref, xs_ref.at[pl.ds(start, rows_per_tile), :], zsem_ref)

    @pl.when(pl.program_id(0) == 0)
    def _():
        zero_ref[...] = jnp.zeros(zero_ref.shape, bf16)
        for wait in (False, True):
            for e in range(N_EXPERTS):
                @pl.when(padded_ref[e] > 0)
                def _():
                    cp = zero_tile(ends_ref[e] // rows_per_tile - 1)
                    cp.wait() if wait else cp.start()

            def spare(j, carry):
                cp = zero_tile(n_used + j)
                cp.wait() if wait else cp.start()
                return carry

            lax.fori_loop(0, n_tiles - n_used, spare, 0)

    s1 = slot_ref[0:1, :]
    s2 = slot_ref[1:2, :]
    w1 = rw_ref[0:1, :]
    w2 = rw_ref[1:2, :]
    chunk = 256
    lane = lax.broadcasted_iota(i32, (chunk, BLK), 1)
    run_rows = _tile_run_rows(runs_ref)

    def sort_chunk(c0):
        srow = lax.broadcasted_iota(i32, (chunk, tl), 0) + c0
        p1 = srow == s1
        p2 = srow == s2
        perm = jnp.where(p1 | p2, 1.0, 0.0).astype(bf16)
        rows = jnp.dot(perm, m_ref[...], preferred_element_type=f32)
        w = jnp.sum(jnp.where(p1, w1, 0.0) + jnp.where(p2, w2, 0.0), axis=1, keepdims=True)
        w_hi = w.astype(bf16).astype(f32)
        w_lo = w - w_hi
        local_ref[cur, c0:c0 + chunk, :D_MODEL] = rows.astype(bf16)
        local_ref[cur, c0:c0 + chunk, D_MODEL:] = jnp.where(
            lane == 0, w_hi, jnp.where(lane == 1, w_lo, 0.0)).astype(bf16)

    for c0 in range(0, lb, chunk):
        sort_chunk(c0)

    def piece(s, local_row, global_row, rows):
        return pltpu.make_async_copy(local_ref.at[s, pl.ds(local_row, rows), :],
                                     xs_ref.at[pl.ds(global_row, rows), :], sem_ref.at[s])

    _for_each_run_piece(runs_ref, lambda l, g, n: piece(cur, l, g, n).start())

    @pl.when(step > 0)
    def _():
        _wait_rows(pending_ref[0], lambda n: piece(1 - cur, 0, 0, n))

    pending_ref[0] = run_rows

    @pl.when(step == pl.num_programs(0) - 1)
    def _():
        _wait_rows(pending_ref[0], lambda n: piece(cur, 0, 0, n))


def _sort_rows(ends, padded, run_tiles, m, slot, route_w, total_rows, rows_per_tile):
    t = m.shape[0]
    tl = TL_SORT
    grid_spec = pltpu.PrefetchScalarGridSpec(
        num_scalar_prefetch=2,
        grid=(t // tl,),
        in_specs=[pl.BlockSpec((1, 1, 3 * N_EXPERTS), lambda i, *_: (i, 0, 0),
                               memory_space=pltpu.SMEM),
                  pl.BlockSpec((tl, D_MODEL), lambda i, *_: (i, 0)),
                  pl.BlockSpec((8, tl), lambda i, *_: (0, i)),
                  pl.BlockSpec((8, tl), lambda i, *_: (0, i))],
        out_specs=pl.BlockSpec(memory_space=pl.ANY),
        scratch_shapes=[pltpu.VMEM((2, LB_SORT, XS_COLS), bf16),
                        pltpu.VMEM((rows_per_tile, XS_COLS), bf16),
                        pltpu.SMEM((1,), i32),
                        pltpu.SemaphoreType.DMA((2,)),
                        pltpu.SemaphoreType.DMA(())])
    return pl.pallas_call(
        functools.partial(_sort_rows_body, rows_per_tile=rows_per_tile),
        grid_spec=grid_spec,
        out_shape=jax.ShapeDtypeStruct((total_rows, XS_COLS), bf16),
        compiler_params=pltpu.CompilerParams(
            dimension_semantics=("arbitrary",), vmem_limit_bytes=VMEM_LIMIT),
        name="sort_rows",
    )(ends, padded, run_tiles, m, slot, route_w)


def _experts_body(tile_ref, texp_ref, first_ref, next_ref, slot_ref, nused_ref,
                  xs_ref, wg_hbm, wu_hbm, wd_hbm, ys_ref,
                  sg_ref, su_ref, sd_ref, wg_ref, wu_ref, wd_ref, sem_ref):
    i = pl.program_id(0)

    def fetches(expert, s):
        return (pltpu.make_async_copy(wg_hbm.at[expert], sg_ref.at[s], sem_ref.at[s]),
                pltpu.make_async_copy(wu_hbm.at[expert], su_ref.at[s], sem_ref.at[s]),
                pltpu.make_async_copy(wd_hbm.at[expert], sd_ref.at[s], sem_ref.at[s]))

    @pl.when(i < nused_ref[0])
    def _():
        s = slot_ref[i]

        @pl.when(first_ref[i] == 1)
        def _():
            @pl.when(i == 0)
            def _():
                for cp in fetches(texp_ref[0], 0):
                    cp.start()

            for cp in fetches(texp_ref[i], s):
                cp.wait()

            @pl.when(next_ref[i] >= 0)
            def _():
                for cp in fetches(next_ref[i], 1 - s):
                    cp.start()

            wg_ref[...] = sg_ref[s].astype(bf16)
            wu_ref[...] = su_ref[s].astype(bf16)
            wd_ref[...] = sd_ref[s].astype(bf16)

        w_row = (xs_ref[:, D_MODEL:D_MODEL + 1].astype(f32)
                 + xs_ref[:, D_MODEL + 1:D_MODEL + 2].astype(f32))
        gate = jnp.dot(xs_ref[:, :D_MODEL], wg_ref[...], preferred_element_type=f32)
        up = jnp.dot(xs_ref[:, :D_MODEL], wu_ref[...], preferred_element_type=f32)
        hdn = (gate * jax.nn.sigmoid(gate)) * up
        y = jnp.dot(hdn.astype(bf16), wd_ref[...], preferred_element_type=f32)
        ys_ref[...] = (y * w_row).astype(bf16)

    @pl.when(i >= nused_ref[0])
    def _():
        ys_ref[...] = jnp.zeros(ys_ref.shape, bf16)


def _experts(tile_idx, tile_exp, first, next_exp, slot, n_used, xs, wg, wu, wd, rows_per_tile):
    r = rows_per_tile
    n_tiles = xs.shape[0] // r
    grid_spec = pltpu.PrefetchScalarGridSpec(
        num_scalar_prefetch=6,
        grid=(n_tiles,),
        in_specs=[pl.BlockSpec((r, XS_COLS), lambda i, ti, *_: (ti[i], 0)),
                  pl.BlockSpec(memory_space=pl.ANY),
                  pl.BlockSpec(memory_space=pl.ANY),
                  pl.BlockSpec(memory_space=pl.ANY)],
        out_specs=pl.BlockSpec((r, D_MODEL), lambda i, *_: (i, 0)),
        scratch_shapes=[pltpu.VMEM((2, D_MODEL, D_FF_E), f32),
                        pltpu.VMEM((2, D_MODEL, D_FF_E), f32),
                        pltpu.VMEM((2, D_FF_E, D_MODEL), f32),
                        pltpu.VMEM((D_MODEL, D_FF_E), bf16),
                        pltpu.VMEM((D_MODEL, D_FF_E), bf16),
                        pltpu.VMEM((D_FF_E, D_MODEL), bf16),
                        pltpu.SemaphoreType.DMA((2,))])
    return pl.pallas_call(
        _experts_body,
        grid_spec=grid_spec,
        out_shape=jax.ShapeDtypeStruct((n_tiles * r, D_MODEL), bf16),
        compiler_params=pltpu.CompilerParams(
            dimension_semantics=("arbitrary",), vmem_limit_bytes=VMEM_LIMIT),
        name="experts",
    )(tile_idx, tile_exp, first, next_exp, slot, n_used, xs, wg, wu, wd)


def _combine_out_body(runs_cur_ref, runs_nxt_ref, h_ref, slot_ref, p_ref, ys_ref,
                      wpg_ref, bpg_ref, wpp_ref, gple_ref, gfin_ref,
                      o_ref, ybuf_ref, sem_ref):
    tk = h_ref.shape[0]
    lb = ybuf_ref.shape[1]
    i = pl.program_id(0)
    n = pl.num_programs(0)
    cur = i % 2

    def piece(s, local_row, global_row, rows):
        return pltpu.make_async_copy(ys_ref.at[pl.ds(global_row, rows), :],
                                     ybuf_ref.at[s, pl.ds(local_row, rows), :], sem_ref.at[s])

    @pl.when(i == 0)
    def _():
        ybuf_ref[...] = jnp.zeros(ybuf_ref.shape, bf16)
        _for_each_run_piece(runs_cur_ref, lambda l, g, r: piece(0, l, g, r).start())

    @pl.when(i + 1 < n)
    def _():
        _for_each_run_piece(runs_nxt_ref, lambda l, g, r: piece(1 - cur, l, g, r).start())

    _wait_rows(_tile_run_rows(runs_cur_ref), lambda r: piece(cur, 0, 0, r))

    col = lax.broadcasted_iota(i32, (tk, lb), 1)
    unperm = jnp.where((col == slot_ref[:, 0:1]) | (col == slot_ref[:, 1:2]), 1.0, 0.0).astype(bf16)
    h2 = h_ref[...] + jnp.dot(unperm, ybuf_ref[cur], preferred_element_type=f32)
    gate = jax.nn.sigmoid(jnp.dot(h2.astype(bf16), wpg_ref[...], preferred_element_type=f32)
                          + bpg_ref[...])
    pp = jnp.dot(p_ref[...].astype(bf16), wpp_ref[...], preferred_element_type=f32)
    h3 = h2 + gate * _rms(pp, gple_ref[...])
    o_ref[...] = _rms(h3, gfin_ref[...])


def _combine_out(run_tiles, h2d, slot_t, p2d, ys, wpg, bpg, wpp, gple, gfin):
    t = h2d.shape[0]
    tk = TL_SORT
    nt = t // tk
    const = lambda shape: pl.BlockSpec(shape, lambda i: (0,) * len(shape))
    runs = lambda imap: pl.BlockSpec((1, 1, 3 * N_EXPERTS), imap, memory_space=pltpu.SMEM)
    return pl.pallas_call(
        _combine_out_body,
        grid=(nt,),
        out_shape=jax.ShapeDtypeStruct((t, D_MODEL), f32),
        in_specs=[runs(lambda i: (i, 0, 0)),
                  runs(lambda i: (jnp.minimum(i + 1, nt - 1), 0, 0)),
                  pl.BlockSpec((tk, D_MODEL), lambda i: (i, 0)),
                  pl.BlockSpec((tk, 2), lambda i: (i, 0)),
                  pl.BlockSpec((tk, PLE_DIM), lambda i: (i, 0)),
                  pl.BlockSpec(memory_space=pl.ANY),
                  const((D_MODEL, D_MODEL)), const((1, D_MODEL)),
                  const((PLE_DIM, D_MODEL)), const((1, D_MODEL)), const((1, D_MODEL))],
        out_specs=pl.BlockSpec((tk, D_MODEL), lambda i: (i, 0)),
        scratch_shapes=[pltpu.VMEM((2, LB_SORT, D_MODEL), bf16),
                        pltpu.SemaphoreType.DMA((2,))],
        compiler_params=pltpu.CompilerParams(
            dimension_semantics=("arbitrary",), vmem_limit_bytes=VMEM_LIMIT),
        name="combine_out",
    )(run_tiles, run_tiles, h2d, slot_t, p2d, ys, wpg, bpg, wpp, gple, gfin)


def _q_perm():
    perm = np.empty((D_B,), np.int32)
    for g in range(GQA):
        for kh in range(N_KV):
            for d in range(HEAD_DIM):
                perm[g * BLK + kh * HEAD_DIM + d] = (kh * GQA + g) * HEAD_DIM + d
    return perm


def kernel(x, p, rel_bias, g_mix, w_in, ln_v_g, ln_v_b, w_spatial, b_spatial, sink, g_out_grp, w_out,
           g_ffn, w_router_group, b_router_group, w_router_expert, b_router_expert, w_gate_e, w_up_e,
           w_down_e, w_ple_proj, g_ple, w_ple_gate, b_ple_gate, g_final):
    b, s, d = x.shape
    t = b * s
    depth = g_mix.shape[0]
    assert depth == 1 and d == D_MODEL
    perm = _q_perm()
    c1, c2 = 2 * D_A, 2 * D_A + D_B
    bias = _bias_table(rel_bias * LOG2E)
    hcur = x.astype(f32)
    for li in range(depth):
        wi = w_in[li]
        win = jnp.concatenate([wi[:, :c1], wi[:, c1:c2][:, perm] * (LOG2E * HEAD_DIM ** -0.5), wi[:, c2:]],
                              axis=1).astype(bf16)
        bs = jnp.broadcast_to(b_spatial[li][:, :, None], (A_HEADS, BLK, BLK)).astype(f32)
        gout = g_out_grp[li]
        wo = w_out[li]
        wout = jnp.concatenate([wo[:D_A], wo[D_A:][perm]], axis=0).astype(bf16)
        wr = jnp.zeros((D_MODEL, BLK), f32)
        wr = wr.at[:, 0:N_GROUPS].set(w_router_group[li]).at[:, 8:ROUTE_ROWS].set(w_router_expert[li])
        wr = wr.astype(bf16)
        br = jnp.full((ROUTE_ROWS,), NEG, f32)
        br = br.at[0:N_GROUPS].set(b_router_group[li]).at[8:].set(b_router_expert[li])
        br = jnp.broadcast_to(br[:, None], (ROUTE_ROWS, TQ_ATT))

        ya, q, k, v = _mix_in(hcur.reshape(t, d), g_mix[li][None], win, ln_v_g[li][None],
                              ln_v_b[li][None], w_spatial[li].astype(bf16), bs, gout[None, :D_A])
        h, m_rows, route_e, route_w = _attn_out(
            sink[li].astype(f32) * LOG2E, hcur, ya.reshape(b, s, D_A), q.reshape(b, s, D_B),
            k.reshape(b, s, BLK), v.reshape(b, s, BLK), bias, gout[None, D_A:][:, perm], wout,
            g_ffn[li][None], wr, br)

        r = R_EXP
        n_sort = t // TL_SORT
        max_rows = 2 * t + n_sort * N_EXPERTS * (RUN_ALIGN - 1)
        n_tiles = -(-max_rows // r) + N_EXPERTS
        slot, runs, offs = _route_pos(route_e, r)
        starts = offs[0, :, 0]
        padded = offs[1, :, 0]
        ends = starts + padded
        n_used = ends[-1] // r
        tile_idx = jnp.minimum(jnp.arange(n_tiles, dtype=i32), n_used - 1)
        tile_exp = jnp.minimum(
            jnp.sum((ends[None, :] <= (tile_idx * r)[:, None]).astype(i32), axis=1), N_EXPERTS - 1)
        tile_exp = tile_exp.astype(i32)
        first = jnp.concatenate([jnp.ones((1,), i32),
                                 (tile_exp[1:] != tile_exp[:-1]).astype(i32)])
        next_tile = ends[tile_exp] // r
        next_exp = jnp.where(next_tile < n_used, tile_exp[jnp.minimum(next_tile, n_tiles - 1)], -1)
        fetch_slot = (jnp.cumsum(first) - 1) % 2
        run_tiles = runs[:, :, :, 0].reshape(n_sort, 1, 3 * N_EXPERTS)
        xs = _sort_rows(ends, padded, run_tiles, m_rows, slot, route_w, n_tiles * r, r)
        ys = _experts(tile_idx, tile_exp, first, next_exp.astype(i32), fetch_slot.astype(i32),
                      n_used.reshape(1), xs, w_gate_e[li], w_up_e[li], w_down_e[li], r)

        out = _combine_out(run_tiles, h.reshape(t, d), slot[:2].T,
                           p[li].reshape(t, PLE_DIM), ys, w_ple_gate[li].astype(bf16),
                           b_ple_gate[li][None], w_ple_proj[li].astype(bf16), g_ple[li][None],
                           g_final[None])
        hcur = out.reshape(b, s, d)
    return hcur
```

```python
import functools
import math

import jax
import jax.numpy as jnp
import numpy as np
from jax import lax
from jax.experimental import pallas as pl
from jax.experimental.pallas import tpu as pltpu

D_MODEL = 1024
D_A = 512
D_B = 512
BLK = 128
A_HEADS = 4
HEAD_DIM = 64
N_HEADS = 8
N_KV = 2
GQA = 4
WINDOW = 128
NUM_BUCKETS = 32
MAX_DIST = 128
D_IN = 2 * D_A + D_B + 2 * N_KV * HEAD_DIM
N_GROUPS = 4
E_PER_GROUP = 8
N_EXPERTS = 32
D_FF_E = 256
PLE_DIM = 256
EPS = 1e-6
NEG = -1e30
LOG2E = math.log2(math.e)

TM_MIX = 512
TQ_ATT = 512
TL_SORT = 512
RUN_ALIGN = 16
SMALL_BITS = 2
BIG_PIECE = RUN_ALIGN << SMALL_BITS
LB_SORT = 2 * TL_SORT + N_EXPERTS * RUN_ALIGN
R_EXP = 512
XS_COLS = D_MODEL + BLK
ROUTE_ROWS = 8 + N_EXPERTS
VMEM_LIMIT = 48 * 1024 * 1024

f32 = jnp.float32
bf16 = jnp.bfloat16
i32 = jnp.int32


def _rms(x, g):
    return x * lax.rsqrt(jnp.mean(x * x, axis=-1, keepdims=True) + EPS) * g


def _gelu_tanh(x):
    c = math.sqrt(2.0 / math.pi)
    return x * (0.5 * (1.0 + jnp.tanh(c * (x + 0.044715 * (x * x * x)))))


def _bucket_table():
    n = NUM_BUCKETS // 2
    max_exact = n // 2
    i = np.arange(BLK)[:, None]
    j = np.arange(3 * BLK)[None, :]
    rel = j - BLK - i
    ret = np.where(rel > 0, n, 0)
    a = np.abs(rel)
    large = max_exact + (np.log(np.maximum(a, 1).astype(np.float64) / max_exact)
                         / math.log(MAX_DIST / max_exact) * (n - max_exact)).astype(np.int32)
    large = np.minimum(large, n - 1)
    return (ret + np.where(a < max_exact, a, large)).astype(np.int32)


def _bias_body(rb_ref, bucket_ref, o_ref):
    bucket = bucket_ref[...]
    o_ref[...] = jnp.zeros(o_ref.shape, f32)

    def step(b, carry):
        hit = bucket == b
        for h in range(N_HEADS):
            o_ref[h] = jnp.where(hit, rb_ref[b, h], o_ref[h])
        return carry

    lax.fori_loop(0, NUM_BUCKETS, step, 0)


def _bias_table(rel_bias):
    bucket = jnp.asarray(_bucket_table())
    return pl.pallas_call(
        _bias_body,
        out_shape=jax.ShapeDtypeStruct((N_HEADS, BLK, 3 * BLK), f32),
        in_specs=[pl.BlockSpec(memory_space=pltpu.SMEM),
                  pl.BlockSpec((BLK, 3 * BLK), lambda: (0, 0))],
        out_specs=pl.BlockSpec((N_HEADS, BLK, 3 * BLK), lambda: (0, 0, 0)),
        name="bias_table",
    )(rel_bias.astype(f32), bucket)


def _mix_in_body(x_ref, gmix_ref, win_ref, lng_ref, lnb_ref, ws_ref, bs_ref, gout_ref,
                 ya_ref, q_ref, k_ref, v_ref):
    tm = x_ref.shape[0]
    nc = tm // BLK
    a = _rms(x_ref[...], gmix_ref[...])
    z = jnp.dot(a.astype(bf16), win_ref[...], preferred_element_type=f32)
    uv = _gelu_tanh(z[:, :2 * D_A])
    u = uv[:, :D_A]
    v = uv[:, D_A:]
    mu = jnp.mean(v, axis=-1, keepdims=True)
    vc = v - mu
    var = jnp.mean(vc * vc, axis=-1, keepdims=True)
    vn = (vc * lax.rsqrt(var + EPS) * lng_ref[...] + lnb_ref[...]).astype(bf16)
    cols = []
    for h in range(A_HEADS):
        rhs = jnp.concatenate(
            [vn[c * BLK:(c + 1) * BLK, h * BLK:(h + 1) * BLK] for c in range(nc)], axis=1)
        r = jnp.dot(ws_ref[h], rhs, preferred_element_type=f32)
        cols.append(jnp.concatenate(
            [r[:, c * BLK:(c + 1) * BLK] + bs_ref[h] for c in range(nc)], axis=0))
    sv = jnp.concatenate(cols, axis=1)
    ya_ref[...] = _rms(u * sv, gout_ref[...]).astype(bf16)
    q_ref[...] = z[:, 2 * D_A:2 * D_A + D_B].astype(bf16)
    k_ref[...] = z[:, 2 * D_A + D_B:2 * D_A + D_B + BLK].astype(bf16)
    v_ref[...] = z[:, 2 * D_A + D_B + BLK:].astype(bf16)


def _mix_in(x2, gmix, win, lng, lnb, ws, bs, gout_a):
    t = x2.shape[0]
    tm = TM_MIX
    const = lambda shape: pl.BlockSpec(shape, lambda i: (0,) * len(shape))
    return pl.pallas_call(
        _mix_in_body,
        grid=(t // tm,),
        out_shape=(jax.ShapeDtypeStruct((t, D_A), bf16),
                   jax.ShapeDtypeStruct((t, D_B), bf16),
                   jax.ShapeDtypeStruct((t, BLK), bf16),
                   jax.ShapeDtypeStruct((t, BLK), bf16)),
        in_specs=[pl.BlockSpec((tm, D_MODEL), lambda i: (i, 0)),
                  const((1, D_MODEL)), const((D_MODEL, D_IN)),
                  const((1, D_A)), const((1, D_A)),
                  const((A_HEADS, BLK, BLK)), const((A_HEADS, BLK, BLK)),
                  const((1, D_A))],
        out_specs=(pl.BlockSpec((tm, D_A), lambda i: (i, 0)),
                   pl.BlockSpec((tm, D_B), lambda i: (i, 0)),
                   pl.BlockSpec((tm, BLK), lambda i: (i, 0)),
                   pl.BlockSpec((tm, BLK), lambda i: (i, 0))),
        compiler_params=pltpu.CompilerParams(
            dimension_semantics=("parallel",), vmem_limit_bytes=VMEM_LIMIT),
        name="mix_in",
    )(x2, gmix, win, lng, lnb, ws, bs, gout_a)


def _attn_out_body(sink_ref, x_ref, ya_ref, q_ref, kp_ref, km_ref, kn_ref, vp_ref, vm_ref, vn_ref,
                   bias_ref, goutb_ref, wout_ref, gffn_ref, wr_ref, br_ref,
                   h_ref, m_ref, re_ref, rw_ref,
                   kf_ref, vf_ref, e_ref, yb_ref, *, seq_len):
    tq = x_ref.shape[1]
    nb = tq // BLK
    ti = pl.program_id(1)

    kf_ref[0:BLK] = kp_ref[0]
    kf_ref[BLK:BLK + tq] = km_ref[0]
    kf_ref[BLK + tq:] = kn_ref[0]
    vlane = lax.broadcasted_iota(i32, (BLK, BLK), 1)
    for src, r0, r1 in ((vp_ref, 0, BLK), (vn_ref, BLK + tq, tq + 2 * BLK)) + tuple(
            (vm_ref.at[:, pl.ds(c * BLK, BLK), :], BLK + c * BLK, BLK + (c + 1) * BLK)
            for c in range(nb)):
        vals = src[0].astype(f32)
        vf_ref[0, r0:r1] = jnp.where(vlane < HEAD_DIM, vals,
                                     jnp.where(vlane == HEAD_DIM, 1.0, 0.0)).astype(bf16)
        vf_ref[1, r0:r1] = jnp.where(vlane >= HEAD_DIM, vals,
                                     jnp.where(vlane == 0, 1.0, 0.0)).astype(bf16)

    row = lax.broadcasted_iota(i32, (BLK, 3 * BLK), 0)
    col = lax.broadcasted_iota(i32, (BLK, 3 * BLK), 1)
    band = jnp.abs(col - BLK - row) <= WINDOW
    lane = lax.broadcasted_iota(i32, (BLK, BLK), 1)
    low_half = lane < HEAD_DIM

    for n in range(nb):
        r0 = n * BLK
        qb = q_ref[0, pl.ds(r0, BLK), :]
        kb = kf_ref[pl.ds(r0, 3 * BLK), :]
        zero = jnp.zeros((BLK, BLK), bf16)
        lhs = []
        for kh in range(N_KV):
            for g in range(GQA):
                tile = qb[:, g * BLK:(g + 1) * BLK]
                lhs.append(jnp.where(low_half, tile, zero) if kh == 0
                           else jnp.where(low_half, zero, tile))
        lhs = jnp.concatenate(lhs, axis=0)
        s_all = lax.dot_general(lhs, kb, (((1,), (1,)), ((), ())),
                                preferred_element_type=f32)
        kpos = col + ((ti * nb + n - 1) * BLK)
        valid = band & (kpos >= 0) & (kpos < seq_len)
        sink_e = []
        for h in range(N_HEADS):
            s = jnp.where(valid, s_all[h * BLK:(h + 1) * BLK] + bias_ref[h], NEG)
            sk = sink_ref[h]
            mrow = jnp.maximum(jnp.max(s, axis=-1, keepdims=True), sk)
            e_ref[n, h * BLK:(h + 1) * BLK, :] = jnp.exp2(s - mrow).astype(bf16)
            sink_e.append(jnp.exp2(sk - mrow))
        half = GQA * BLK
        pv = [jnp.dot(e_ref[n, kh * half:(kh + 1) * half, :], vf_ref[kh, pl.ds(r0, 3 * BLK), :],
                      preferred_element_type=f32) for kh in range(N_KV)]

        def head_out(h):
            kh, g = divmod(h, GQA)
            rows = pv[kh][g * BLK:(g + 1) * BLK]
            ones_col = HEAD_DIM if kh == 0 else 0
            return rows * (1.0 / (rows[:, ones_col:ones_col + 1] + sink_e[h]))

        for g in range(GQA):
            yb_ref[pl.ds(r0, BLK), g * BLK:(g + 1) * BLK] = jnp.where(
                low_half, head_out(g), head_out(GQA + g))

    ybn = _rms(yb_ref[...], goutb_ref[...]).astype(bf16)
    y = jnp.concatenate([ya_ref[0], ybn], axis=1)
    h = x_ref[0] + jnp.dot(y, wout_ref[...], preferred_element_type=f32)
    h_ref[0] = h
    m = _rms(h, gffn_ref[...])
    m_ref[...] = m.astype(bf16)

    logit_t = jnp.dot(m.astype(bf16), wr_ref[...], preferred_element_type=f32)
    logit = jnp.transpose(logit_t)[:ROUTE_ROWS] + br_ref[...]
    sub = lax.broadcasted_iota(i32, (8, tq), 0)
    lg = logit[0:8]
    mg = jnp.max(lg, axis=0, keepdims=True)
    pg_top = 1.0 / jnp.sum(jnp.exp(lg - mg), axis=0, keepdims=True)
    g_idx = jnp.min(jnp.where(lg == mg, sub, 8), axis=0, keepdims=True)
    sel = logit[8:16]
    for g in range(1, N_GROUPS):
        sel = jnp.where(g_idx == g, logit[8 + 8 * g:16 + 8 * g], sel)
    m1 = jnp.max(sel, axis=0, keepdims=True)
    i1 = jnp.min(jnp.where(sel == m1, sub, 8), axis=0, keepdims=True)
    sel2 = jnp.where(sub == i1, -jnp.inf, sel)
    m2 = jnp.max(sel2, axis=0, keepdims=True)
    i2 = jnp.min(jnp.where(sel2 == m2, sub, 8), axis=0, keepdims=True)
    r = jnp.exp(m2 - m1)
    w1 = pg_top / (1.0 + r)
    w2 = pg_top * r / (1.0 + r)
    e1 = g_idx * E_PER_GROUP + i1
    e2 = g_idx * E_PER_GROUP + i2
    re_ref[...] = jnp.where(sub == 0, e1, jnp.where(sub == 1, e2, 0))
    rw_ref[...] = jnp.where(sub == 0, w1, jnp.where(sub == 1, w2, 0.0))


def _attn_out(sink, x, ya, q, k, v, bias, goutb, wout, gffn, wr, br):
    b, s, _ = x.shape
    tq = TQ_ATT
    nb = tq // BLK
    nblk = s // BLK
    t = b * s
    nq = s // tq
    const = lambda shape: pl.BlockSpec(shape, lambda bi, i, *_: (0,) * len(shape))
    tok = lambda w: pl.BlockSpec((1, tq, w), lambda bi, i, *_: (bi, i, 0))
    prev = pl.BlockSpec((1, BLK, BLK), lambda bi, i, *_: (bi, jnp.maximum(i * nb - 1, 0), 0))
    nxt = pl.BlockSpec((1, BLK, BLK), lambda bi, i, *_: (bi, jnp.minimum(i * nb + nb, nblk - 1), 0))
    rows = pl.BlockSpec((tq, D_MODEL), lambda bi, i, *_: (bi * nq + i, 0))
    lanes = lambda rows: pl.BlockSpec((rows, tq), lambda bi, i, *_: (0, bi * nq + i))
    grid_spec = pltpu.PrefetchScalarGridSpec(
        num_scalar_prefetch=1,
        grid=(b, nq),
        in_specs=[tok(D_MODEL), tok(D_A), tok(D_B),
                  prev, tok(BLK), nxt, prev, tok(BLK), nxt,
                  const((N_HEADS, BLK, 3 * BLK)), const((1, D_B)), const((D_MODEL, D_MODEL)),
                  const((1, D_MODEL)), const((D_MODEL, BLK)), const((ROUTE_ROWS, tq))],
        out_specs=(tok(D_MODEL), rows, lanes(8), lanes(8)),
        scratch_shapes=[pltpu.VMEM((tq + 2 * BLK, BLK), bf16),
                        pltpu.VMEM((N_KV, tq + 2 * BLK, BLK), bf16),
                        pltpu.VMEM((nb, N_HEADS * BLK, 3 * BLK), bf16),
                        pltpu.VMEM((tq, D_B), f32)])
    return pl.pallas_call(
        functools.partial(_attn_out_body, seq_len=s),
        grid_spec=grid_spec,
        out_shape=(jax.ShapeDtypeStruct((b, s, D_MODEL), f32),
                   jax.ShapeDtypeStruct((t, D_MODEL), bf16),
                   jax.ShapeDtypeStruct((8, t), i32),
                   jax.ShapeDtypeStruct((8, t), f32)),
        compiler_params=pltpu.CompilerParams(
            dimension_semantics=("parallel", "parallel"), vmem_limit_bytes=VMEM_LIMIT),
        name="attn_out",
    )(sink, x, ya, q, k, k, k, v, v, v, bias, goutb, wout, gffn, wr, br)


def _exclusive_prefix(vals):
    a = lax.broadcasted_iota(i32, (N_EXPERTS, N_EXPERTS), 0)
    c = lax.broadcasted_iota(i32, (N_EXPERTS, N_EXPERTS), 1)
    low = jnp.where(c < a, 1.0, 0.0).astype(bf16)
    hi = (vals >> 8).astype(f32).astype(bf16)
    lo = (vals & 255).astype(f32).astype(bf16)
    return (jnp.dot(low, hi, preferred_element_type=f32) * 256.0
            + jnp.dot(low, lo, preferred_element_type=f32)).astype(i32)


def _round_up_pow2(vals, mult):
    log_m = mult.bit_length() - 1
    return ((vals + (mult - 1)) >> log_m) << log_m


def _route_pos_body(re_ref, slot_ref, runs_ref, offs_ref, tot_ref, carry_ref, tri_ref,
                    *, rows_per_tile):
    phase = pl.program_id(0)
    ti = pl.program_id(1)
    tl = re_ref.shape[1]
    eid = lax.broadcasted_iota(i32, (N_EXPERTS, tl), 0)
    hit1 = eid == re_ref[0:1, :]
    hit2 = eid == re_ref[1:2, :]
    onehot = jnp.where(hit1 | hit2, 1.0, 0.0)
    cnt = jnp.sum(onehot, axis=1, keepdims=True).astype(i32)
    run_len = jnp.broadcast_to(_round_up_pow2(cnt, RUN_ALIGN), (N_EXPERTS, BLK))

    @pl.when((phase == 0) & (ti == 0))
    def _():
        tot_ref[...] = jnp.zeros(tot_ref.shape, i32)
        a = lax.broadcasted_iota(i32, (tl, tl), 0)
        c = lax.broadcasted_iota(i32, (tl, tl), 1)
        tri_ref[...] = jnp.where(a < c, 1.0, 0.0).astype(bf16)

    @pl.when(phase == 0)
    def _():
        tot_ref[...] = tot_ref[...] + run_len

    @pl.when((phase == 1) & (ti == 0))
    def _():
        padded = _round_up_pow2(tot_ref[...], rows_per_tile)
        starts = _exclusive_prefix(padded)
        carry_ref[...] = starts
        offs_ref[0] = starts
        offs_ref[1] = padded

    @pl.when(phase == 1)
    def _():
        local_off = _exclusive_prefix(run_len)
        before = jnp.dot(onehot.astype(bf16), tri_ref[...], preferred_element_type=f32)
        slot = before + local_off[:, 0:1].astype(f32)
        s1 = jnp.sum(jnp.where(hit1, slot, 0.0), axis=0, keepdims=True)
        s2 = jnp.sum(jnp.where(hit2, slot, 0.0), axis=0, keepdims=True)
        sub = lax.broadcasted_iota(i32, (8, tl), 0)
        slot_ref[...] = jnp.where(sub == 0, s1.astype(i32), jnp.where(sub == 1, s2.astype(i32), 0))
        runs_ref[0, 0] = local_off
        runs_ref[0, 1] = run_len
        runs_ref[0, 2] = carry_ref[...]
        carry_ref[...] = carry_ref[...] + run_len


def _route_pos(route_e, rows_per_tile):
    t = route_e.shape[1]
    tl = TL_SORT
    return pl.pallas_call(
        functools.partial(_route_pos_body, rows_per_tile=rows_per_tile),
        grid=(2, t // tl),
        out_shape=(jax.ShapeDtypeStruct((8, t), i32),
                   jax.ShapeDtypeStruct((t // tl, 3, N_EXPERTS, BLK), i32),
                   jax.ShapeDtypeStruct((2, N_EXPERTS, BLK), i32)),
        in_specs=[pl.BlockSpec((8, tl), lambda p, i: (0, i))],
        out_specs=(pl.BlockSpec((8, tl), lambda p, i: (0, i * p)),
                   pl.BlockSpec((1, 3, N_EXPERTS, BLK), lambda p, i: (i * p, 0, 0, 0)),
                   pl.BlockSpec((2, N_EXPERTS, BLK), lambda p, i: (0, 0, 0))),
        scratch_shapes=[pltpu.VMEM((N_EXPERTS, BLK), i32),
                        pltpu.VMEM((N_EXPERTS, BLK), i32),
                        pltpu.VMEM((tl, tl), bf16)],
        compiler_params=pltpu.CompilerParams(
            dimension_semantics=("arbitrary", "arbitrary"), vmem_limit_bytes=VMEM_LIMIT),
        name="route_pos",
    )(route_e)


def _for_each_run_piece(runs_ref, fn):
    def per_expert(e, carry):
        local_off = runs_ref[0, 0, e]
        length = runs_ref[0, 0, N_EXPERTS + e]
        global_off = runs_ref[0, 0, 2 * N_EXPERTS + e]
        units = length >> (RUN_ALIGN.bit_length() - 1)
        n_big = units >> SMALL_BITS

        def big_piece(k, inner):
            done = k * BIG_PIECE
            fn(pl.multiple_of(local_off + done, RUN_ALIGN),
               pl.multiple_of(global_off + done, RUN_ALIGN), BIG_PIECE)
            return inner

        lax.fori_loop(0, n_big, big_piece, 0)
        for b in range(SMALL_BITS):
            @pl.when(((units >> b) & 1) == 1)
            def _():
                done = ((units >> (b + 1)) << (b + 1)) * RUN_ALIGN
                fn(pl.multiple_of(local_off + done, RUN_ALIGN),
                   pl.multiple_of(global_off + done, RUN_ALIGN), RUN_ALIGN << b)
        return carry

    lax.fori_loop(0, N_EXPERTS, per_expert, 0)


def _tile_run_rows(runs_ref):
    return lax.fori_loop(0, N_EXPERTS, lambda e, acc: acc + runs_ref[0, 0, N_EXPERTS + e], 0)


def _wait_rows(total_rows, make_copy):
    units = total_rows >> (RUN_ALIGN.bit_length() - 1)
    for b in range((LB_SORT // RUN_ALIGN).bit_length()):
        @pl.when(((units >> b) & 1) == 1)
        def _():
            make_copy(RUN_ALIGN << b).wait()


def _sort_rows_body(ends_ref, padded_ref, runs_ref, m_ref, slot_ref, rw_ref, xs_ref,
                    local_ref, zero_ref, pending_ref, sem_ref, zsem_ref, *, rows_per_tile):
    tl = m_ref.shape[0]
    lb = local_ref.shape[1]
    step = pl.program_id(0)
    cur = step % 2
    n_tiles = xs_ref.shape[0] // rows_per_tile
    n_used = ends_ref[N_EXPERTS - 1] // rows_per_tile

    def zero_tile(tile):
        start = pl.multiple_of(tile * rows_per_tile, rows_per_tile)
        return pltpu.make_async_copy(zero_ref, xs_ref.at[pl.ds(start, rows_per_tile), :], zsem_ref)

    @pl.when(pl.program_id(0) == 0)
    def _():
        zero_ref[...] = jnp.zeros(zero_ref.shape, bf16)
        for wait in (False, True):
            for e in range(N_EXPERTS):
                @pl.when(padded_ref[e] > 0)
                def _():
                    cp = zero_tile(ends_ref[e] // rows_per_tile - 1)
                    cp.wait() if wait else cp.start()

            def spare(j, carry):
                cp = zero_tile(n_used + j)
                cp.wait() if wait else cp.start()
                return carry

            lax.fori_loop(0, n_tiles - n_used, spare, 0)

    s1 = slot_ref[0:1, :]
    s2 = slot_ref[1:2, :]
    w1 = rw_ref[0:1, :]
    w2 = rw_ref[1:2, :]
    chunk = 256
    lane = lax.broadcasted_iota(i32, (chunk, BLK), 1)
    run_rows = _tile_run_rows(runs_ref)

    def sort_chunk(c0):
        srow = lax.broadcasted_iota(i32, (chunk, tl), 0) + c0
        p1 = srow == s1
        p2 = srow == s2
        perm = jnp.where(p1 | p2, 1.0, 0.0).astype(bf16)
        rows = jnp.dot(perm, m_ref[...], preferred_element_type=f32)
        w = jnp.sum(jnp.where(p1, w1, 0.0) + jnp.where(p2, w2, 0.0), axis=1, keepdims=True)
        w_hi = w.astype(bf16).astype(f32)
        w_lo = w - w_hi
        local_ref[cur, c0:c0 + chunk, :D_MODEL] = rows.astype(bf16)
        local_ref[cur, c0:c0 + chunk, D_MODEL:] = jnp.where(
            lane == 0, w_hi, jnp.where(lane == 1, w_lo, 0.0)).astype(bf16)

    short = lb - chunk

    @pl.when(run_rows <= short)
    def _():
        for c0 in range(0, short, chunk):
            sort_chunk(c0)

    @pl.when(run_rows > short)
    def _():
        for c0 in range(0, lb, chunk):
            sort_chunk(c0)

    def piece(s, local_row, global_row, rows):
        return pltpu.make_async_copy(local_ref.at[s, pl.ds(local_row, rows), :],
                                     xs_ref.at[pl.ds(global_row, rows), :], sem_ref.at[s])

    _for_each_run_piece(runs_ref, lambda l, g, n: piece(cur, l, g, n).start())

    @pl.when(step > 0)
    def _():
        _wait_rows(pending_ref[0], lambda n: piece(1 - cur, 0, 0, n))

    pending_ref[0] = run_rows

    @pl.when(step == pl.num_programs(0) - 1)
    def _():
        _wait_rows(pending_ref[0], lambda n: piece(cur, 0, 0, n))


def _sort_rows(ends, padded, run_tiles, m, slot, route_w, total_rows, rows_per_tile):
    t = m.shape[0]
    tl = TL_SORT
    grid_spec = pltpu.PrefetchScalarGridSpec(
        num_scalar_prefetch=2,
        grid=(t // tl,),
        in_specs=[pl.BlockSpec((1, 1, 3 * N_EXPERTS), lambda i, *_: (i, 0, 0),
                               memory_space=pltpu.SMEM),
                  pl.BlockSpec((tl, D_MODEL), lambda i, *_: (i, 0)),
                  pl.BlockSpec((8, tl), lambda i, *_: (0, i)),
                  pl.BlockSpec((8, tl), lambda i, *_: (0, i))],
        out_specs=pl.BlockSpec(memory_space=pl.ANY),
        scratch_shapes=[pltpu.VMEM((2, LB_SORT, XS_COLS), bf16),
                        pltpu.VMEM((rows_per_tile, XS_COLS), bf16),
                        pltpu.SMEM((1,), i32),
                        pltpu.SemaphoreType.DMA((2,)),
                        pltpu.SemaphoreType.DMA(())])
    return pl.pallas_call(
        functools.partial(_sort_rows_body, rows_per_tile=rows_per_tile),
        grid_spec=grid_spec,
        out_shape=jax.ShapeDtypeStruct((total_rows, XS_COLS), bf16),
        compiler_params=pltpu.CompilerParams(
            dimension_semantics=("arbitrary",), vmem_limit_bytes=VMEM_LIMIT),
        name="sort_rows",
    )(ends, padded, run_tiles, m, slot, route_w)


def _experts_body(tile_ref, texp_ref, first_ref, next_ref, slot_ref, nused_ref,
                  xs_ref, wg_hbm, wu_hbm, wd_hbm, ys_ref,
                  sg_ref, su_ref, sd_ref, wg_ref, wu_ref, wd_ref, sem_ref):
    i = pl.program_id(0)

    def fetches(expert, s):
        return (pltpu.make_async_copy(wg_hbm.at[expert], sg_ref.at[s], sem_ref.at[s]),
                pltpu.make_async_copy(wu_hbm.at[expert], su_ref.at[s], sem_ref.at[s]),
                pltpu.make_async_copy(wd_hbm.at[expert], sd_ref.at[s], sem_ref.at[s]))

    @pl.when(i < nused_ref[0])
    def _():
        s = slot_ref[i]

        @pl.when(first_ref[i] == 1)
        def _():
            @pl.when(i == 0)
            def _():
                for cp in fetches(texp_ref[0], 0):
                    cp.start()

            for cp in fetches(texp_ref[i], s):
                cp.wait()

            @pl.when(next_ref[i] >= 0)
            def _():
                for cp in fetches(next_ref[i], 1 - s):
                    cp.start()

            wg_ref[...] = sg_ref[s].astype(bf16)
            wu_ref[...] = su_ref[s].astype(bf16)
            wd_ref[...] = sd_ref[s].astype(bf16)

        w_row = (xs_ref[:, D_MODEL:D_MODEL + 1].astype(f32)
                 + xs_ref[:, D_MODEL + 1:D_MODEL + 2].astype(f32))
        gate = jnp.dot(xs_ref[:, :D_MODEL], wg_ref[...], preferred_element_type=f32)
        up = jnp.dot(xs_ref[:, :D_MODEL], wu_ref[...], preferred_element_type=f32)
        hdn = (gate * jax.nn.sigmoid(gate)) * up
        y = jnp.dot(hdn.astype(bf16), wd_ref[...], preferred_element_type=f32)
        ys_ref[...] = (y * w_row).astype(bf16)

    @pl.when(i >= nused_ref[0])
    def _():
        ys_ref[...] = jnp.zeros(ys_ref.shape, bf16)


def _experts(tile_idx, tile_exp, first, next_exp, slot, n_used, xs, wg, wu, wd, rows_per_tile):
    r = rows_per_tile
    n_tiles = xs.shape[0] // r
    grid_spec = pltpu.PrefetchScalarGridSpec(
        num_scalar_prefetch=6,
        grid=(n_tiles,),
        in_specs=[pl.BlockSpec((r, XS_COLS), lambda i, ti, *_: (ti[i], 0)),
                  pl.BlockSpec(memory_space=pl.ANY),
                  pl.BlockSpec(memory_space=pl.ANY),
                  pl.BlockSpec(memory_space=pl.ANY)],
        out_specs=pl.BlockSpec((r, D_MODEL), lambda i, *_: (i, 0)),
        scratch_shapes=[pltpu.VMEM((2, D_MODEL, D_FF_E), f32),
                        pltpu.VMEM((2, D_MODEL, D_FF_E), f32),
                        pltpu.VMEM((2, D_FF_E, D_MODEL), f32),
                        pltpu.VMEM((D_MODEL, D_FF_E), bf16),
                        pltpu.VMEM((D_MODEL, D_FF_E), bf16),
                        pltpu.VMEM((D_FF_E, D_MODEL), bf16),
                        pltpu.SemaphoreType.DMA((2,))])
    return pl.pallas_call(
        _experts_body,
        grid_spec=grid_spec,
        out_shape=jax.ShapeDtypeStruct((n_tiles * r, D_MODEL), bf16),
        compiler_params=pltpu.CompilerParams(
            dimension_semantics=("arbitrary",), vmem_limit_bytes=VMEM_LIMIT),
        name="experts",
    )(tile_idx, tile_exp, first, next_exp, slot, n_used, xs, wg, wu, wd)


def _combine_out_body(runs_cur_ref, runs_nxt_ref, h_ref, slot_ref, p_ref, ys_ref,
                      wpg_ref, bpg_ref, wpp_ref, gple_ref, gfin_ref,
                      o_ref, ybuf_ref, moe_ref, sem_ref):
    tk = h_ref.shape[0]
    lb = ybuf_ref.shape[1]
    i = pl.program_id(0)
    n = pl.num_programs(0)
    cur = i % 2

    def piece(s, local_row, global_row, rows):
        return pltpu.make_async_copy(ys_ref.at[pl.ds(global_row, rows), :],
                                     ybuf_ref.at[s, pl.ds(local_row, rows), :], sem_ref.at[s])

    @pl.when(i == 0)
    def _():
        ybuf_ref[...] = jnp.zeros(ybuf_ref.shape, bf16)
        _for_each_run_piece(runs_cur_ref, lambda l, g, r: piece(0, l, g, r).start())

    @pl.when(i + 1 < n)
    def _():
        _for_each_run_piece(runs_nxt_ref, lambda l, g, r: piece(1 - cur, l, g, r).start())

    run_rows = _tile_run_rows(runs_cur_ref)
    _wait_rows(run_rows, lambda r: piece(cur, 0, 0, r))

    def unsort(rows):
        col = lax.broadcasted_iota(i32, (tk, rows), 1)
        unperm = jnp.where((col == slot_ref[:, 0:1]) | (col == slot_ref[:, 1:2]), 1.0, 0.0)
        moe_ref[...] = jnp.dot(unperm.astype(bf16), ybuf_ref[cur, :rows, :],
                               preferred_element_type=f32)

    short = lb - 256
    pl.when(run_rows <= short)(functools.partial(unsort, short))
    pl.when(run_rows > short)(functools.partial(unsort, lb))
    h2 = h_ref[...] + moe_ref[...]
    gate = jax.nn.sigmoid(jnp.dot(h2.astype(bf16), wpg_ref[...], preferred_element_type=f32)
                          + bpg_ref[...])
    pp = jnp.dot(p_ref[...].astype(bf16), wpp_ref[...], preferred_element_type=f32)
    h3 = h2 + gate * _rms(pp, gple_ref[...])
    o_ref[...] = _rms(h3, gfin_ref[...])


def _combine_out(run_tiles, h2d, slot_t, p2d, ys, wpg, bpg, wpp, gple, gfin):
    t = h2d.shape[0]
    tk = TL_SORT
    nt = t // tk
    const = lambda shape: pl.BlockSpec(shape, lambda i: (0,) * len(shape))
    runs = lambda imap: pl.BlockSpec((1, 1, 3 * N_EXPERTS), imap, memory_space=pltpu.SMEM)
    return pl.pallas_call(
        _combine_out_body,
        grid=(nt,),
        out_shape=jax.ShapeDtypeStruct((t, D_MODEL), f32),
        in_specs=[runs(lambda i: (i, 0, 0)),
                  runs(lambda i: (jnp.minimum(i + 1, nt - 1), 0, 0)),
                  pl.BlockSpec((tk, D_MODEL), lambda i: (i, 0)),
                  pl.BlockSpec((tk, 2), lambda i: (i, 0)),
                  pl.BlockSpec((tk, PLE_DIM), lambda i: (i, 0)),
                  pl.BlockSpec(memory_space=pl.ANY),
                  const((D_MODEL, D_MODEL)), const((1, D_MODEL)),
                  const((PLE_DIM, D_MODEL)), const((1, D_MODEL)), const((1, D_MODEL))],
        out_specs=pl.BlockSpec((tk, D_MODEL), lambda i: (i, 0)),
        scratch_shapes=[pltpu.VMEM((2, LB_SORT, D_MODEL), bf16),
                        pltpu.VMEM((tk, D_MODEL), f32),
                        pltpu.SemaphoreType.DMA((2,))],
        compiler_params=pltpu.CompilerParams(
            dimension_semantics=("arbitrary",), vmem_limit_bytes=VMEM_LIMIT),
        name="combine_out",
    )(run_tiles, run_tiles, h2d, slot_t, p2d, ys, wpg, bpg, wpp, gple, gfin)


def _q_perm():
    perm = np.empty((D_B,), np.int32)
    for g in range(GQA):
        for kh in range(N_KV):
            for d in range(HEAD_DIM):
                perm[g * BLK + kh * HEAD_DIM + d] = (kh * GQA + g) * HEAD_DIM + d
    return perm


def kernel(x, p, rel_bias, g_mix, w_in, ln_v_g, ln_v_b, w_spatial, b_spatial, sink, g_out_grp, w_out,
           g_ffn, w_router_group, b_router_group, w_router_expert, b_router_expert, w_gate_e, w_up_e,
           w_down_e, w_ple_proj, g_ple, w_ple_gate, b_ple_gate, g_final):
    b, s, d = x.shape
    t = b * s
    depth = g_mix.shape[0]
    assert depth == 1 and d == D_MODEL
    perm = _q_perm()
    c1, c2 = 2 * D_A, 2 * D_A + D_B
    bias = _bias_table(rel_bias * LOG2E)
    hcur = x.astype(f32)
    for li in range(depth):
        wi = w_in[li]
        win = jnp.concatenate([wi[:, :c1], wi[:, c1:c2][:, perm] * (LOG2E * HEAD_DIM ** -0.5), wi[:, c2:]],
                              axis=1).astype(bf16)
        bs = jnp.broadcast_to(b_spatial[li][:, :, None], (A_HEADS, BLK, BLK)).astype(f32)
        gout = g_out_grp[li]
        wo = w_out[li]
        wout = jnp.concatenate([wo[:D_A], wo[D_A:][perm]], axis=0).astype(bf16)
        wr = jnp.zeros((D_MODEL, BLK), f32)
        wr = wr.at[:, 0:N_GROUPS].set(w_router_group[li]).at[:, 8:ROUTE_ROWS].set(w_router_expert[li])
        wr = wr.astype(bf16)
        br = jnp.full((ROUTE_ROWS,), NEG, f32)
        br = br.at[0:N_GROUPS].set(b_router_group[li]).at[8:].set(b_router_expert[li])
        br = jnp.broadcast_to(br[:, None], (ROUTE_ROWS, TQ_ATT))

        ya, q, k, v = _mix_in(hcur.reshape(t, d), g_mix[li][None], win, ln_v_g[li][None],
                              ln_v_b[li][None], w_spatial[li].astype(bf16), bs, gout[None, :D_A])
        h, m_rows, route_e, route_w = _attn_out(
            sink[li].astype(f32) * LOG2E, hcur, ya.reshape(b, s, D_A), q.reshape(b, s, D_B),
            k.reshape(b, s, BLK), v.reshape(b, s, BLK), bias, gout[None, D_A:][:, perm], wout,
            g_ffn[li][None], wr, br)

        r = R_EXP
        n_sort = t // TL_SORT
        max_rows = 2 * t + n_sort * N_EXPERTS * (RUN_ALIGN - 1)
        n_tiles = -(-max_rows // r) + N_EXPERTS
        slot, runs, offs = _route_pos(route_e, r)
        starts = offs[0, :, 0]
        padded = offs[1, :, 0]
        ends = starts + padded
        n_used = ends[-1] // r
        tile_idx = jnp.minimum(jnp.arange(n_tiles, dtype=i32), n_used - 1)
        tile_exp = jnp.minimum(
            jnp.sum((ends[None, :] <= (tile_idx * r)[:, None]).astype(i32), axis=1), N_EXPERTS - 1)
        tile_exp = tile_exp.astype(i32)
        first = jnp.concatenate([jnp.ones((1,), i32),
                                 (tile_exp[1:] != tile_exp[:-1]).astype(i32)])
        next_tile = ends[tile_exp] // r
        next_exp = jnp.where(next_tile < n_used, tile_exp[jnp.minimum(next_tile, n_tiles - 1)], -1)
        fetch_slot = (jnp.cumsum(first) - 1) % 2
        run_tiles = runs[:, :, :, 0].reshape(n_sort, 1, 3 * N_EXPERTS)
        xs = _sort_rows(ends, padded, run_tiles, m_rows, slot, route_w, n_tiles * r, r)
        ys = _experts(tile_idx, tile_exp, first, next_exp.astype(i32), fetch_slot.astype(i32),
                      n_used.reshape(1), xs, w_gate_e[li], w_up_e[li], w_down_e[li], r)

        out = _combine_out(run_tiles, h.reshape(t, d), slot[:2].T,
                           p[li].reshape(t, PLE_DIM), ys, w_ple_gate[li].astype(bf16),
                           b_ple_gate[li][None], w_ple_proj[li].astype(bf16), g_ple[li][None],
                           g_final[None])
        hcur = out.reshape(b, s, d)
    return hcur
```

```python
import functools
import math

import jax
import jax.numpy as jnp
import numpy as np
from jax import lax
from jax.experimental import pallas as pl
from jax.experimental.pallas import tpu as pltpu

D_MODEL = 1024
D_A = 512
D_B = 512
BLK = 128
A_HEADS = 4
HEAD_DIM = 64
N_HEADS = 8
N_KV = 2
GQA = 4
WINDOW = 128
NUM_BUCKETS = 32
MAX_DIST = 128
D_IN = 2 * D_A + D_B + 2 * N_KV * HEAD_DIM
N_GROUPS = 4
E_PER_GROUP = 8
N_EXPERTS = 32
D_FF_E = 256
PLE_DIM = 256
EPS = 1e-6
NEG = -1e30
LOG2E = math.log2(math.e)

TM_MIX = 512
TQ_ATT = 512
TL_SORT = 512
RUN_ALIGN = 16
SMALL_BITS = 2
BIG_PIECE = RUN_ALIGN << SMALL_BITS
LB_SORT = 2 * TL_SORT + N_EXPERTS * RUN_ALIGN
R_EXP = 512
XS_COLS = D_MODEL + BLK
XS_RING = 3
ROUTE_ROWS = 8 + N_EXPERTS
VMEM_LIMIT = 48 * 1024 * 1024

f32 = jnp.float32
bf16 = jnp.bfloat16
i32 = jnp.int32


def _rms(x, g):
    return x * lax.rsqrt(jnp.mean(x * x, axis=-1, keepdims=True) + EPS) * g


def _gelu_tanh(x):
    c = math.sqrt(2.0 / math.pi)
    return x * (0.5 * (1.0 + jnp.tanh(c * (x + 0.044715 * (x * x * x)))))


def _bucket_table():
    n = NUM_BUCKETS // 2
    max_exact = n // 2
    i = np.arange(BLK)[:, None]
    j = np.arange(3 * BLK)[None, :]
    rel = j - BLK - i
    ret = np.where(rel > 0, n, 0)
    a = np.abs(rel)
    large = max_exact + (np.log(np.maximum(a, 1).astype(np.float64) / max_exact)
                         / math.log(MAX_DIST / max_exact) * (n - max_exact)).astype(np.int32)
    large = np.minimum(large, n - 1)
    return (ret + np.where(a < max_exact, a, large)).astype(np.int32)


def _bias_body(rb_ref, bucket_ref, o_ref):
    bucket = bucket_ref[...]
    o_ref[...] = jnp.zeros(o_ref.shape, f32)

    def step(b, carry):
        hit = bucket == b
        for h in range(N_HEADS):
            o_ref[h] = jnp.where(hit, rb_ref[b, h], o_ref[h])
        return carry

    lax.fori_loop(0, NUM_BUCKETS, step, 0)


def _bias_table(rel_bias):
    bucket = jnp.asarray(_bucket_table())
    return pl.pallas_call(
        _bias_body,
        out_shape=jax.ShapeDtypeStruct((N_HEADS, BLK, 3 * BLK), f32),
        in_specs=[pl.BlockSpec(memory_space=pltpu.SMEM),
                  pl.BlockSpec((BLK, 3 * BLK), lambda: (0, 0))],
        out_specs=pl.BlockSpec((N_HEADS, BLK, 3 * BLK), lambda: (0, 0, 0)),
        name="bias_table",
    )(rel_bias.astype(f32), bucket)


def _mix_in_body(x_ref, gmix_ref, win_ref, lng_ref, lnb_ref, ws_ref, bs_ref, gout_ref,
                 ya_ref, q_ref, k_ref, v_ref):
    tm = x_ref.shape[0]
    nc = tm // BLK
    a = _rms(x_ref[...], gmix_ref[...])
    z = jnp.dot(a.astype(bf16), win_ref[...], preferred_element_type=f32)
    uv = _gelu_tanh(z[:, :2 * D_A])
    u = uv[:, :D_A]
    v = uv[:, D_A:]
    mu = jnp.mean(v, axis=-1, keepdims=True)
    vc = v - mu
    var = jnp.mean(vc * vc, axis=-1, keepdims=True)
    vn = (vc * lax.rsqrt(var + EPS) * lng_ref[...] + lnb_ref[...]).astype(bf16)
    cols = []
    for h in range(A_HEADS):
        rhs = jnp.concatenate(
            [vn[c * BLK:(c + 1) * BLK, h * BLK:(h + 1) * BLK] for c in range(nc)], axis=1)
        r = jnp.dot(ws_ref[h], rhs, preferred_element_type=f32)
        cols.append(jnp.concatenate(
            [r[:, c * BLK:(c + 1) * BLK] + bs_ref[h] for c in range(nc)], axis=0))
    sv = jnp.concatenate(cols, axis=1)
    ya_ref[...] = _rms(u * sv, gout_ref[...]).astype(bf16)
    q_ref[...] = z[:, 2 * D_A:2 * D_A + D_B].astype(bf16)
    k_ref[...] = z[:, 2 * D_A + D_B:2 * D_A + D_B + BLK].astype(bf16)
    v_ref[...] = z[:, 2 * D_A + D_B + BLK:].astype(bf16)


def _mix_in(x2, gmix, win, lng, lnb, ws, bs, gout_a):
    t = x2.shape[0]
    tm = TM_MIX
    const = lambda shape: pl.BlockSpec(shape, lambda i: (0,) * len(shape))
    return pl.pallas_call(
        _mix_in_body,
        grid=(t // tm,),
        out_shape=(jax.ShapeDtypeStruct((t, D_A), bf16),
                   jax.ShapeDtypeStruct((t, D_B), bf16),
                   jax.ShapeDtypeStruct((t, BLK), bf16),
                   jax.ShapeDtypeStruct((t, BLK), bf16)),
        in_specs=[pl.BlockSpec((tm, D_MODEL), lambda i: (i, 0)),
                  const((1, D_MODEL)), const((D_MODEL, D_IN)),
                  const((1, D_A)), const((1, D_A)),
                  const((A_HEADS, BLK, BLK)), const((A_HEADS, BLK, BLK)),
                  const((1, D_A))],
        out_specs=(pl.BlockSpec((tm, D_A), lambda i: (i, 0)),
                   pl.BlockSpec((tm, D_B), lambda i: (i, 0)),
                   pl.BlockSpec((tm, BLK), lambda i: (i, 0)),
                   pl.BlockSpec((tm, BLK), lambda i: (i, 0))),
        compiler_params=pltpu.CompilerParams(
            dimension_semantics=("parallel",), vmem_limit_bytes=VMEM_LIMIT),
        name="mix_in",
    )(x2, gmix, win, lng, lnb, ws, bs, gout_a)


def _attn_out_body(sink_ref, x_ref, ya_ref, q_ref, kp_ref, km_ref, kn_ref, vp_ref, vm_ref, vn_ref,
                   bias_ref, goutb_ref, wout_ref, gffn_ref, wr_ref, br_ref,
                   h_ref, m_ref, re_ref, rw_ref,
                   kf_ref, vf_ref, e_ref, yb_ref, *, seq_len):
    tq = x_ref.shape[1]
    nb = tq // BLK
    ti = pl.program_id(1)

    kf_ref[0:BLK] = kp_ref[0]
    kf_ref[BLK:BLK + tq] = km_ref[0]
    kf_ref[BLK + tq:] = kn_ref[0]
    vlane = lax.broadcasted_iota(i32, (BLK, BLK), 1)
    for src, r0, r1 in ((vp_ref, 0, BLK), (vn_ref, BLK + tq, tq + 2 * BLK)) + tuple(
            (vm_ref.at[:, pl.ds(c * BLK, BLK), :], BLK + c * BLK, BLK + (c + 1) * BLK)
            for c in range(nb)):
        vals = src[0].astype(f32)
        vf_ref[0, r0:r1] = jnp.where(vlane < HEAD_DIM, vals,
                                     jnp.where(vlane == HEAD_DIM, 1.0, 0.0)).astype(bf16)
        vf_ref[1, r0:r1] = jnp.where(vlane >= HEAD_DIM, vals,
                                     jnp.where(vlane == 0, 1.0, 0.0)).astype(bf16)

    row = lax.broadcasted_iota(i32, (BLK, 3 * BLK), 0)
    col = lax.broadcasted_iota(i32, (BLK, 3 * BLK), 1)
    band = jnp.abs(col - BLK - row) <= WINDOW
    lane = lax.broadcasted_iota(i32, (BLK, BLK), 1)
    low_half = lane < HEAD_DIM

    for n in range(nb):
        r0 = n * BLK
        qb = q_ref[0, pl.ds(r0, BLK), :]
        kb = kf_ref[pl.ds(r0, 3 * BLK), :]
        zero = jnp.zeros((BLK, BLK), bf16)
        lhs = []
        for kh in range(N_KV):
            for g in range(GQA):
                tile = qb[:, g * BLK:(g + 1) * BLK]
                lhs.append(jnp.where(low_half, tile, zero) if kh == 0
                           else jnp.where(low_half, zero, tile))
        lhs = jnp.concatenate(lhs, axis=0)
        s_all = lax.dot_general(lhs, kb, (((1,), (1,)), ((), ())),
                                preferred_element_type=f32)
        kpos = col + ((ti * nb + n - 1) * BLK)
        valid = band & (kpos >= 0) & (kpos < seq_len)
        sink_e = []
        for h in range(N_HEADS):
            s = jnp.where(valid, s_all[h * BLK:(h + 1) * BLK] + bias_ref[h], NEG)
            sk = sink_ref[h]
            mrow = jnp.maximum(jnp.max(s, axis=-1, keepdims=True), sk)
            e_ref[n, h * BLK:(h + 1) * BLK, :] = jnp.exp2(s - mrow).astype(bf16)
            sink_e.append(jnp.exp2(sk - mrow))
        half = GQA * BLK
        pv = [jnp.dot(e_ref[n, kh * half:(kh + 1) * half, :], vf_ref[kh, pl.ds(r0, 3 * BLK), :],
                      preferred_element_type=f32) for kh in range(N_KV)]

        def head_out(h):
            kh, g = divmod(h, GQA)
            rows = pv[kh][g * BLK:(g + 1) * BLK]
            ones_col = HEAD_DIM if kh == 0 else 0
            return rows * (1.0 / (rows[:, ones_col:ones_col + 1] + sink_e[h]))

        for g in range(GQA):
            yb_ref[pl.ds(r0, BLK), g * BLK:(g + 1) * BLK] = jnp.where(
                low_half, head_out(g), head_out(GQA + g))

    ybn = _rms(yb_ref[...], goutb_ref[...]).astype(bf16)
    y = jnp.concatenate([ya_ref[0], ybn], axis=1)
    h = x_ref[0] + jnp.dot(y, wout_ref[...], preferred_element_type=f32)
    h_ref[0] = h
    m = _rms(h, gffn_ref[...])
    m_ref[...] = m.astype(bf16)

    logit_t = jnp.dot(m.astype(bf16), wr_ref[...], preferred_element_type=f32)
    logit = jnp.transpose(logit_t)[:ROUTE_ROWS] + br_ref[...]
    sub = lax.broadcasted_iota(i32, (8, tq), 0)
    lg = logit[0:8]
    mg = jnp.max(lg, axis=0, keepdims=True)
    pg_top = 1.0 / jnp.sum(jnp.exp(lg - mg), axis=0, keepdims=True)
    g_idx = jnp.min(jnp.where(lg == mg, sub, 8), axis=0, keepdims=True)
    sel = logit[8:16]
    for g in range(1, N_GROUPS):
        sel = jnp.where(g_idx == g, logit[8 + 8 * g:16 + 8 * g], sel)
    m1 = jnp.max(sel, axis=0, keepdims=True)
    i1 = jnp.min(jnp.where(sel == m1, sub, 8), axis=0, keepdims=True)
    sel2 = jnp.where(sub == i1, -jnp.inf, sel)
    m2 = jnp.max(sel2, axis=0, keepdims=True)
    i2 = jnp.min(jnp.where(sel2 == m2, sub, 8), axis=0, keepdims=True)
    r = jnp.exp(m2 - m1)
    w1 = pg_top / (1.0 + r)
    w2 = pg_top * r / (1.0 + r)
    e1 = g_idx * E_PER_GROUP + i1
    e2 = g_idx * E_PER_GROUP + i2
    re_ref[...] = jnp.where(sub == 0, e1, jnp.where(sub == 1, e2, 0))
    rw_ref[...] = jnp.where(sub == 0, w1, jnp.where(sub == 1, w2, 0.0))


def _attn_out(sink, x, ya, q, k, v, bias, goutb, wout, gffn, wr, br):
    b, s, _ = x.shape
    tq = TQ_ATT
    nb = tq // BLK
    nblk = s // BLK
    t = b * s
    nq = s // tq
    const = lambda shape: pl.BlockSpec(shape, lambda bi, i, *_: (0,) * len(shape))
    tok = lambda w: pl.BlockSpec((1, tq, w), lambda bi, i, *_: (bi, i, 0))
    prev = pl.BlockSpec((1, BLK, BLK), lambda bi, i, *_: (bi, jnp.maximum(i * nb - 1, 0), 0))
    nxt = pl.BlockSpec((1, BLK, BLK), lambda bi, i, *_: (bi, jnp.minimum(i * nb + nb, nblk - 1), 0))
    rows = pl.BlockSpec((tq, D_MODEL), lambda bi, i, *_: (bi * nq + i, 0))
    lanes = lambda rows: pl.BlockSpec((rows, tq), lambda bi, i, *_: (0, bi * nq + i))
    grid_spec = pltpu.PrefetchScalarGridSpec(
        num_scalar_prefetch=1,
        grid=(b, nq),
        in_specs=[tok(D_MODEL), tok(D_A), tok(D_B),
                  prev, tok(BLK), nxt, prev, tok(BLK), nxt,
                  const((N_HEADS, BLK, 3 * BLK)), const((1, D_B)), const((D_MODEL, D_MODEL)),
                  const((1, D_MODEL)), const((D_MODEL, BLK)), const((ROUTE_ROWS, tq))],
        out_specs=(tok(D_MODEL), rows, lanes(8), lanes(8)),
        scratch_shapes=[pltpu.VMEM((tq + 2 * BLK, BLK), bf16),
                        pltpu.VMEM((N_KV, tq + 2 * BLK, BLK), bf16),
                        pltpu.VMEM((nb, N_HEADS * BLK, 3 * BLK), bf16),
                        pltpu.VMEM((tq, D_B), f32)])
    return pl.pallas_call(
        functools.partial(_attn_out_body, seq_len=s),
        grid_spec=grid_spec,
        out_shape=(jax.ShapeDtypeStruct((b, s, D_MODEL), f32),
                   jax.ShapeDtypeStruct((t, D_MODEL), bf16),
                   jax.ShapeDtypeStruct((8, t), i32),
                   jax.ShapeDtypeStruct((8, t), f32)),
        compiler_params=pltpu.CompilerParams(
            dimension_semantics=("parallel", "parallel"), vmem_limit_bytes=VMEM_LIMIT),
        name="attn_out",
    )(sink, x, ya, q, k, k, k, v, v, v, bias, goutb, wout, gffn, wr, br)


def _exclusive_prefix(vals):
    a = lax.broadcasted_iota(i32, (N_EXPERTS, N_EXPERTS), 0)
    c = lax.broadcasted_iota(i32, (N_EXPERTS, N_EXPERTS), 1)
    low = jnp.where(c < a, 1.0, 0.0).astype(bf16)
    hi = (vals >> 8).astype(f32).astype(bf16)
    lo = (vals & 255).astype(f32).astype(bf16)
    return (jnp.dot(low, hi, preferred_element_type=f32) * 256.0
            + jnp.dot(low, lo, preferred_element_type=f32)).astype(i32)


def _round_up_pow2(vals, mult):
    log_m = mult.bit_length() - 1
    return ((vals + (mult - 1)) >> log_m) << log_m


def _route_pos_body(re_ref, slot_ref, runs_ref, offs_ref, tot_ref, carry_ref, tri_ref,
                    *, rows_per_tile):
    phase = pl.program_id(0)
    ti = pl.program_id(1)
    tl = re_ref.shape[1]
    eid = lax.broadcasted_iota(i32, (N_EXPERTS, tl), 0)
    hit1 = eid == re_ref[0:1, :]
    hit2 = eid == re_ref[1:2, :]
    onehot = jnp.where(hit1 | hit2, 1.0, 0.0)
    cnt = jnp.sum(onehot, axis=1, keepdims=True).astype(i32)
    run_len = jnp.broadcast_to(_round_up_pow2(cnt, RUN_ALIGN), (N_EXPERTS, BLK))

    @pl.when((phase == 0) & (ti == 0))
    def _():
        tot_ref[...] = jnp.zeros(tot_ref.shape, i32)
        a = lax.broadcasted_iota(i32, (tl, tl), 0)
        c = lax.broadcasted_iota(i32, (tl, tl), 1)
        tri_ref[...] = jnp.where(a < c, 1.0, 0.0).astype(bf16)

    @pl.when(phase == 0)
    def _():
        tot_ref[...] = tot_ref[...] + run_len

    @pl.when((phase == 1) & (ti == 0))
    def _():
        padded = _round_up_pow2(tot_ref[...], rows_per_tile)
        starts = _exclusive_prefix(padded)
        carry_ref[...] = starts
        offs_ref[0] = starts
        offs_ref[1] = padded

    @pl.when(phase == 1)
    def _():
        local_off = _exclusive_prefix(run_len)
        before = jnp.dot(onehot.astype(bf16), tri_ref[...], preferred_element_type=f32)
        slot = before + local_off[:, 0:1].astype(f32)
        s1 = jnp.sum(jnp.where(hit1, slot, 0.0), axis=0, keepdims=True)
        s2 = jnp.sum(jnp.where(hit2, slot, 0.0), axis=0, keepdims=True)
        sub = lax.broadcasted_iota(i32, (8, tl), 0)
        slot_ref[...] = jnp.where(sub == 0, s1.astype(i32), jnp.where(sub == 1, s2.astype(i32), 0))
        runs_ref[0, 0] = local_off
        runs_ref[0, 1] = run_len
        runs_ref[0, 2] = carry_ref[...]
        carry_ref[...] = carry_ref[...] + run_len


def _route_pos(route_e, rows_per_tile):
    t = route_e.shape[1]
    tl = TL_SORT
    return pl.pallas_call(
        functools.partial(_route_pos_body, rows_per_tile=rows_per_tile),
        grid=(2, t // tl),
        out_shape=(jax.ShapeDtypeStruct((8, t), i32),
                   jax.ShapeDtypeStruct((t // tl, 3, N_EXPERTS, BLK), i32),
                   jax.ShapeDtypeStruct((2, N_EXPERTS, BLK), i32)),
        in_specs=[pl.BlockSpec((8, tl), lambda p, i: (0, i))],
        out_specs=(pl.BlockSpec((8, tl), lambda p, i: (0, i * p)),
                   pl.BlockSpec((1, 3, N_EXPERTS, BLK), lambda p, i: (i * p, 0, 0, 0)),
                   pl.BlockSpec((2, N_EXPERTS, BLK), lambda p, i: (0, 0, 0))),
        scratch_shapes=[pltpu.VMEM((N_EXPERTS, BLK), i32),
                        pltpu.VMEM((N_EXPERTS, BLK), i32),
                        pltpu.VMEM((tl, tl), bf16)],
        compiler_params=pltpu.CompilerParams(
            dimension_semantics=("arbitrary", "arbitrary"), vmem_limit_bytes=VMEM_LIMIT),
        name="route_pos",
    )(route_e)


def _for_each_run_piece(runs_ref, fn):
    def per_expert(e, carry):
        local_off = runs_ref[0, 0, e]
        length = runs_ref[0, 0, N_EXPERTS + e]
        global_off = runs_ref[0, 0, 2 * N_EXPERTS + e]
        units = length >> (RUN_ALIGN.bit_length() - 1)
        n_big = units >> SMALL_BITS

        def big_piece(k, inner):
            done = k * BIG_PIECE
            fn(pl.multiple_of(local_off + done, RUN_ALIGN),
               pl.multiple_of(global_off + done, RUN_ALIGN), BIG_PIECE)
            return inner

        lax.fori_loop(0, n_big, big_piece, 0)
        for b in range(SMALL_BITS):
            @pl.when(((units >> b) & 1) == 1)
            def _():
                done = ((units >> (b + 1)) << (b + 1)) * RUN_ALIGN
                fn(pl.multiple_of(local_off + done, RUN_ALIGN),
                   pl.multiple_of(global_off + done, RUN_ALIGN), RUN_ALIGN << b)
        return carry

    lax.fori_loop(0, N_EXPERTS, per_expert, 0)


def _tile_run_rows(runs_ref):
    return lax.fori_loop(0, N_EXPERTS, lambda e, acc: acc + runs_ref[0, 0, N_EXPERTS + e], 0)


def _wait_rows(total_rows, make_copy):
    units = total_rows >> (RUN_ALIGN.bit_length() - 1)
    for b in range((LB_SORT // RUN_ALIGN).bit_length()):
        @pl.when(((units >> b) & 1) == 1)
        def _():
            make_copy(RUN_ALIGN << b).wait()


def _sort_rows_body(ends_ref, padded_ref, runs_ref, m_ref, slot_ref, rw_ref, xs_ref,
                    local_ref, zero_ref, pending_ref, sem_ref, zsem_ref, *, rows_per_tile):
    tl = m_ref.shape[0]
    lb = local_ref.shape[1]
    step = pl.program_id(0)
    cur = step % 2
    n_tiles = xs_ref.shape[0] // rows_per_tile
    n_used = ends_ref[N_EXPERTS - 1] // rows_per_tile

    def zero_tile(tile):
        start = pl.multiple_of(tile * rows_per_tile, rows_per_tile)
        return pltpu.make_async_copy(zero_ref, xs_ref.at[pl.ds(start, rows_per_tile), :], zsem_ref)

    @pl.when(pl.program_id(0) == 0)
    def _():
        zero_ref[...] = jnp.zeros(zero_ref.shape, bf16)
        for wait in (False, True):
            for e in range(N_EXPERTS):
                @pl.when(padded_ref[e] > 0)
                def _():
                    cp = zero_tile(ends_ref[e] // rows_per_tile - 1)
                    cp.wait() if wait else cp.start()

            def spare(j, carry):
                cp = zero_tile(n_used + j)
                cp.wait() if wait else cp.start()
                return carry

            lax.fori_loop(0, n_tiles - n_used, spare, 0)

    s1 = slot_ref[0:1, :]
    s2 = slot_ref[1:2, :]
    w1 = rw_ref[0:1, :]
    w2 = rw_ref[1:2, :]
    chunk = 256
    lane = lax.broadcasted_iota(i32, (chunk, BLK), 1)
    run_rows = _tile_run_rows(runs_ref)

    def sort_chunk(c0):
        srow = lax.broadcasted_iota(i32, (chunk, tl), 0) + c0
        p1 = srow == s1
        p2 = srow == s2
        perm = jnp.where(p1 | p2, 1.0, 0.0).astype(bf16)
        rows = jnp.dot(perm, m_ref[...], preferred_element_type=f32)
        w = jnp.sum(jnp.where(p1, w1, 0.0) + jnp.where(p2, w2, 0.0), axis=1, keepdims=True)
        w_hi = w.astype(bf16).astype(f32)
        w_lo = w - w_hi
        local_ref[cur, c0:c0 + chunk, :D_MODEL] = rows.astype(bf16)
        local_ref[cur, c0:c0 + chunk, D_MODEL:] = jnp.where(
            lane == 0, w_hi, jnp.where(lane == 1, w_lo, 0.0)).astype(bf16)

    short = lb - chunk

    @pl.when(run_rows <= short)
    def _():
        for c0 in range(0, short, chunk):
            sort_chunk(c0)

    @pl.when(run_rows > short)
    def _():
        for c0 in range(0, lb, chunk):
            sort_chunk(c0)

    def piece(s, local_row, global_row, rows):
        return pltpu.make_async_copy(local_ref.at[s, pl.ds(local_row, rows), :],
                                     xs_ref.at[pl.ds(global_row, rows), :], sem_ref.at[s])

    _for_each_run_piece(runs_ref, lambda l, g, n: piece(cur, l, g, n).start())

    @pl.when(step > 0)
    def _():
        _wait_rows(pending_ref[0], lambda n: piece(1 - cur, 0, 0, n))

    pending_ref[0] = run_rows

    @pl.when(step == pl.num_programs(0) - 1)
    def _():
        _wait_rows(pending_ref[0], lambda n: piece(cur, 0, 0, n))


def _sort_rows(ends, padded, run_tiles, m, slot, route_w, total_rows, rows_per_tile):
    t = m.shape[0]
    tl = TL_SORT
    grid_spec = pltpu.PrefetchScalarGridSpec(
        num_scalar_prefetch=2,
        grid=(t // tl,),
        in_specs=[pl.BlockSpec((1, 1, 3 * N_EXPERTS), lambda i, *_: (i, 0, 0),
                               memory_space=pltpu.SMEM),
                  pl.BlockSpec((tl, D_MODEL), lambda i, *_: (i, 0)),
                  pl.BlockSpec((8, tl), lambda i, *_: (0, i)),
                  pl.BlockSpec((8, tl), lambda i, *_: (0, i))],
        out_specs=pl.BlockSpec(memory_space=pl.ANY),
        scratch_shapes=[pltpu.VMEM((2, LB_SORT, XS_COLS), bf16),
                        pltpu.VMEM((rows_per_tile, XS_COLS), bf16),
                        pltpu.SMEM((1,), i32),
                        pltpu.SemaphoreType.DMA((2,)),
                        pltpu.SemaphoreType.DMA(())])
    return pl.pallas_call(
        functools.partial(_sort_rows_body, rows_per_tile=rows_per_tile),
        grid_spec=grid_spec,
        out_shape=jax.ShapeDtypeStruct((total_rows, XS_COLS), bf16),
        compiler_params=pltpu.CompilerParams(
            dimension_semantics=("arbitrary",), vmem_limit_bytes=VMEM_LIMIT),
        name="sort_rows",
    )(ends, padded, run_tiles, m, slot, route_w)


def _experts_body(tile_ref, texp_ref, first_ref, next_ref, slot_ref, nused_ref,
                  xs_hbm, wg_hbm, wu_hbm, wd_hbm, ys_ref,
                  xbuf_ref, sg_ref, su_ref, sd_ref, wg_ref, wu_ref, wd_ref, xsem_ref, sem_ref):
    i = pl.program_id(0)
    n_used = nused_ref[0]
    r = xbuf_ref.shape[1]

    def row_tile(tile):
        s = tile % XS_RING
        start = pl.multiple_of(tile * r, r)
        return pltpu.make_async_copy(xs_hbm.at[pl.ds(start, r), :], xbuf_ref.at[s], xsem_ref.at[s])

    @pl.when(i == 0)
    def _():
        for ahead in range(XS_RING - 1):
            @pl.when(ahead < n_used)
            def _():
                row_tile(ahead).start()

    @pl.when(i + (XS_RING - 1) < n_used)
    def _():
        row_tile(i + (XS_RING - 1)).start()

    def fetches(expert, s):
        return (pltpu.make_async_copy(wg_hbm.at[expert], sg_ref.at[s], sem_ref.at[s]),
                pltpu.make_async_copy(wu_hbm.at[expert], su_ref.at[s], sem_ref.at[s]),
                pltpu.make_async_copy(wd_hbm.at[expert], sd_ref.at[s], sem_ref.at[s]))

    @pl.when(i < n_used)
    def _():
        s = slot_ref[i]
        row_tile(i).wait()
        xs_ref = xbuf_ref.at[i % XS_RING]

        @pl.when(first_ref[i] == 1)
        def _():
            @pl.when(i == 0)
            def _():
                for cp in fetches(texp_ref[0], 0):
                    cp.start()

            for cp in fetches(texp_ref[i], s):
                cp.wait()

            @pl.when(next_ref[i] >= 0)
            def _():
                for cp in fetches(next_ref[i], 1 - s):
                    cp.start()

            wg_ref[...] = sg_ref[s].astype(bf16)
            wu_ref[...] = su_ref[s].astype(bf16)
            wd_ref[...] = sd_ref[s].astype(bf16)

        w_row = (xs_ref[:, D_MODEL:D_MODEL + 1].astype(f32)
                 + xs_ref[:, D_MODEL + 1:D_MODEL + 2].astype(f32))
        gate = jnp.dot(xs_ref[:, :D_MODEL], wg_ref[...], preferred_element_type=f32)
        up = jnp.dot(xs_ref[:, :D_MODEL], wu_ref[...], preferred_element_type=f32)
        hdn = (gate * jax.nn.sigmoid(gate)) * up
        y = jnp.dot(hdn.astype(bf16), wd_ref[...], preferred_element_type=f32)
        ys_ref[...] = (y * w_row).astype(bf16)

    @pl.when(i >= n_used)
    def _():
        ys_ref[...] = jnp.zeros(ys_ref.shape, bf16)


def _experts(tile_idx, tile_exp, first, next_exp, slot, n_used, xs, wg, wu, wd, rows_per_tile):
    r = rows_per_tile
    n_tiles = xs.shape[0] // r
    grid_spec = pltpu.PrefetchScalarGridSpec(
        num_scalar_prefetch=6,
        grid=(n_tiles,),
        in_specs=[pl.BlockSpec(memory_space=pl.ANY),
                  pl.BlockSpec(memory_space=pl.ANY),
                  pl.BlockSpec(memory_space=pl.ANY),
                  pl.BlockSpec(memory_space=pl.ANY)],
        out_specs=pl.BlockSpec((r, D_MODEL), lambda i, *_: (i, 0)),
        scratch_shapes=[pltpu.VMEM((XS_RING, r, XS_COLS), bf16),
                        pltpu.VMEM((2, D_MODEL, D_FF_E), f32),
                        pltpu.VMEM((2, D_MODEL, D_FF_E), f32),
                        pltpu.VMEM((2, D_FF_E, D_MODEL), f32),
                        pltpu.VMEM((D_MODEL, D_FF_E), bf16),
                        pltpu.VMEM((D_MODEL, D_FF_E), bf16),
                        pltpu.VMEM((D_FF_E, D_MODEL), bf16),
                        pltpu.SemaphoreType.DMA((XS_RING,)),
                        pltpu.SemaphoreType.DMA((2,))])
    return pl.pallas_call(
        _experts_body,
        grid_spec=grid_spec,
        out_shape=jax.ShapeDtypeStruct((n_tiles * r, D_MODEL), bf16),
        compiler_params=pltpu.CompilerParams(
            dimension_semantics=("arbitrary",), vmem_limit_bytes=VMEM_LIMIT),
        name="experts",
    )(tile_idx, tile_exp, first, next_exp, slot, n_used, xs, wg, wu, wd)


def _combine_out_body(runs_cur_ref, runs_nxt_ref, h_ref, slot_ref, p_ref, ys_ref,
                      wpg_ref, bpg_ref, wpp_ref, gple_ref, gfin_ref,
                      o_ref, ybuf_ref, moe_ref, sem_ref):
    tk = h_ref.shape[0]
    lb = ybuf_ref.shape[1]
    i = pl.program_id(0)
    n = pl.num_programs(0)
    cur = i % 2

    def piece(s, local_row, global_row, rows):
        return pltpu.make_async_copy(ys_ref.at[pl.ds(global_row, rows), :],
                                     ybuf_ref.at[s, pl.ds(local_row, rows), :], sem_ref.at[s])

    @pl.when(i == 0)
    def _():
        ybuf_ref[...] = jnp.zeros(ybuf_ref.shape, bf16)
        _for_each_run_piece(runs_cur_ref, lambda l, g, r: piece(0, l, g, r).start())

    @pl.when(i + 1 < n)
    def _():
        _for_each_run_piece(runs_nxt_ref, lambda l, g, r: piece(1 - cur, l, g, r).start())

    run_rows = _tile_run_rows(runs_cur_ref)
    _wait_rows(run_rows, lambda r: piece(cur, 0, 0, r))

    def unsort(rows):
        col = lax.broadcasted_iota(i32, (tk, rows), 1)
        unperm = jnp.where((col == slot_ref[:, 0:1]) | (col == slot_ref[:, 1:2]), 1.0, 0.0)
        moe_ref[...] = jnp.dot(unperm.astype(bf16), ybuf_ref[cur, :rows, :],
                               preferred_element_type=f32)

    short = lb - 256
    pl.when(run_rows <= short)(functools.partial(unsort, short))
    pl.when(run_rows > short)(functools.partial(unsort, lb))
    h2 = h_ref[...] + moe_ref[...]
    gate = jax.nn.sigmoid(jnp.dot(h2.astype(bf16), wpg_ref[...], preferred_element_type=f32)
                          + bpg_ref[...])
    pp = jnp.dot(p_ref[...].astype(bf16), wpp_ref[...], preferred_element_type=f32)
    h3 = h2 + gate * _rms(pp, gple_ref[...])
    o_ref[...] = _rms(h3, gfin_ref[...])


def _combine_out(run_tiles, h2d, slot_t, p2d, ys, wpg, bpg, wpp, gple, gfin):
    t = h2d.shape[0]
    tk = TL_SORT
    nt = t // tk
    const = lambda shape: pl.BlockSpec(shape, lambda i: (0,) * len(shape))
    runs = lambda imap: pl.BlockSpec((1, 1, 3 * N_EXPERTS), imap, memory_space=pltpu.SMEM)
    return pl.pallas_call(
        _combine_out_body,
        grid=(nt,),
        out_shape=jax.ShapeDtypeStruct((t, D_MODEL), f32),
        in_specs=[runs(lambda i: (i, 0, 0)),
                  runs(lambda i: (jnp.minimum(i + 1, nt - 1), 0, 0)),
                  pl.BlockSpec((tk, D_MODEL), lambda i: (i, 0)),
                  pl.BlockSpec((tk, 2), lambda i: (i, 0)),
                  pl.BlockSpec((tk, PLE_DIM), lambda i: (i, 0)),
                  pl.BlockSpec(memory_space=pl.ANY),
                  const((D_MODEL, D_MODEL)), const((1, D_MODEL)),
                  const((PLE_DIM, D_MODEL)), const((1, D_MODEL)), const((1, D_MODEL))],
        out_specs=pl.BlockSpec((tk, D_MODEL), lambda i: (i, 0)),
        scratch_shapes=[pltpu.VMEM((2, LB_SORT, D_MODEL), bf16),
                        pltpu.VMEM((tk, D_MODEL), f32),
                        pltpu.SemaphoreType.DMA((2,))],
        compiler_params=pltpu.CompilerParams(
            dimension_semantics=("arbitrary",), vmem_limit_bytes=VMEM_LIMIT),
        name="combine_out",
    )(run_tiles, run_tiles, h2d, slot_t, p2d, ys, wpg, bpg, wpp, gple, gfin)


def _q_perm():
    perm = np.empty((D_B,), np.int32)
    for g in range(GQA):
        for kh in range(N_KV):
            for d in range(HEAD_DIM):
                perm[g * BLK + kh * HEAD_DIM + d] = (kh * GQA + g) * HEAD_DIM + d
    return perm


def kernel(x, p, rel_bias, g_mix, w_in, ln_v_g, ln_v_b, w_spatial, b_spatial, sink, g_out_grp, w_out,
           g_ffn, w_router_group, b_router_group, w_router_expert, b_router_expert, w_gate_e, w_up_e,
           w_down_e, w_ple_proj, g_ple, w_ple_gate, b_ple_gate, g_final):
    b, s, d = x.shape
    t = b * s
    depth = g_mix.shape[0]
    assert depth == 1 and d == D_MODEL
    perm = _q_perm()
    c1, c2 = 2 * D_A, 2 * D_A + D_B
    bias = _bias_table(rel_bias * LOG2E)
    hcur = x.astype(f32)
    for li in range(depth):
        wi = w_in[li]
        win = jnp.concatenate([wi[:, :c1], wi[:, c1:c2][:, perm] * (LOG2E * HEAD_DIM ** -0.5), wi[:, c2:]],
                              axis=1).astype(bf16)
        bs = jnp.broadcast_to(b_spatial[li][:, :, None], (A_HEADS, BLK, BLK)).astype(f32)
        gout = g_out_grp[li]
        wo = w_out[li]
        wout = jnp.concatenate([wo[:D_A], wo[D_A:][perm]], axis=0).astype(bf16)
        wr = jnp.zeros((D_MODEL, BLK), f32)
        wr = wr.at[:, 0:N_GROUPS].set(w_router_group[li]).at[:, 8:ROUTE_ROWS].set(w_router_expert[li])
        wr = wr.astype(bf16)
        br = jnp.full((ROUTE_ROWS,), NEG, f32)
        br = br.at[0:N_GROUPS].set(b_router_group[li]).at[8:].set(b_router_expert[li])
        br = jnp.broadcast_to(br[:, None], (ROUTE_ROWS, TQ_ATT))

        ya, q, k, v = _mix_in(hcur.reshape(t, d), g_mix[li][None], win, ln_v_g[li][None],
                              ln_v_b[li][None], w_spatial[li].astype(bf16), bs, gout[None, :D_A])
        h, m_rows, route_e, route_w = _attn_out(
            sink[li].astype(f32) * LOG2E, hcur, ya.reshape(b, s, D_A), q.reshape(b, s, D_B),
            k.reshape(b, s, BLK), v.reshape(b, s, BLK), bias, gout[None, D_A:][:, perm], wout,
            g_ffn[li][None], wr, br)

        r = R_EXP
        n_sort = t // TL_SORT
        max_rows = 2 * t + n_sort * N_EXPERTS * (RUN_ALIGN - 1)
        n_tiles = -(-max_rows // r) + N_EXPERTS
        slot, runs, offs = _route_pos(route_e, r)
        starts = offs[0, :, 0]
        padded = offs[1, :, 0]
        ends = starts + padded
        n_used = ends[-1] // r
        tile_idx = jnp.minimum(jnp.arange(n_tiles, dtype=i32), n_used - 1)
        tile_exp = jnp.minimum(
            jnp.sum((ends[None, :] <= (tile_idx * r)[:, None]).astype(i32), axis=1), N_EXPERTS - 1)
        tile_exp = tile_exp.astype(i32)
        first = jnp.concatenate([jnp.ones((1,), i32),
                                 (tile_exp[1:] != tile_exp[:-1]).astype(i32)])
        next_tile = ends[tile_exp] // r
        next_exp = jnp.where(next_tile < n_used, tile_exp[jnp.minimum(next_tile, n_tiles - 1)], -1)
        fetch_slot = (jnp.cumsum(first) - 1) % 2
        run_tiles = runs[:, :, :, 0].reshape(n_sort, 1, 3 * N_EXPERTS)
        xs = _sort_rows(ends, padded, run_tiles, m_rows, slot, route_w, n_tiles * r, r)
        ys = _experts(tile_idx, tile_exp, first, next_exp.astype(i32), fetch_slot.astype(i32),
                      n_used.reshape(1), xs, w_gate_e[li], w_up_e[li], w_down_e[li], r)

        out = _combine_out(run_tiles, h.reshape(t, d), slot[:2].T,
                           p[li].reshape(t, PLE_DIM), ys, w_ple_gate[li].astype(bf16),
                           b_ple_gate[li][None], w_ple_proj[li].astype(bf16), g_ple[li][None],
                           g_final[None])
        hcur = out.reshape(b, s, d)
    return hcur
```

```python
import functools
import math

import jax
import jax.numpy as jnp
import numpy as np
from jax import lax
from jax.experimental import pallas as pl
from jax.experimental.pallas import tpu as pltpu

D_MODEL = 1024
D_A = 512
D_B = 512
BLK = 128
A_HEADS = 4
HEAD_DIM = 64
N_HEADS = 8
N_KV = 2
GQA = 4
WINDOW = 128
NUM_BUCKETS = 32
MAX_DIST = 128
D_IN = 2 * D_A + D_B + 2 * N_KV * HEAD_DIM
N_GROUPS = 4
E_PER_GROUP = 8
N_EXPERTS = 32
D_FF_E = 256
PLE_DIM = 256
EPS = 1e-6
NEG = -1e30
LOG2E = math.log2(math.e)

TM_MIX = 512
TQ_ATT = 512
TL_SORT = 512
RUN_ALIGN = 16
SMALL_BITS = 2
BIG_PIECE = RUN_ALIGN << SMALL_BITS
LB_SORT = 2 * TL_SORT + N_EXPERTS * RUN_ALIGN
R_EXP = 512
XS_COLS = D_MODEL + BLK
XS_RING = 3
ROUTE_ROWS = 8 + N_EXPERTS
VMEM_LIMIT = 48 * 1024 * 1024

f32 = jnp.float32
bf16 = jnp.bfloat16
i32 = jnp.int32


def _rms(x, g):
    return x * lax.rsqrt(jnp.mean(x * x, axis=-1, keepdims=True) + EPS) * g


def _gelu_tanh(x):
    c = math.sqrt(2.0 / math.pi)
    return x * (0.5 * (1.0 + jnp.tanh(c * (x + 0.044715 * (x * x * x)))))


def _bucket_table():
    n = NUM_BUCKETS // 2
    max_exact = n // 2
    i = np.arange(BLK)[:, None]
    j = np.arange(3 * BLK)[None, :]
    rel = j - BLK - i
    ret = np.where(rel > 0, n, 0)
    a = np.abs(rel)
    large = max_exact + (np.log(np.maximum(a, 1).astype(np.float64) / max_exact)
                         / math.log(MAX_DIST / max_exact) * (n - max_exact)).astype(np.int32)
    large = np.minimum(large, n - 1)
    return (ret + np.where(a < max_exact, a, large)).astype(np.int32)


def _bias_body(rb_ref, bucket_ref, o_ref):
    bucket = bucket_ref[...]
    o_ref[...] = jnp.zeros(o_ref.shape, f32)

    def step(b, carry):
        hit = bucket == b
        for h in range(N_HEADS):
            o_ref[h] = jnp.where(hit, rb_ref[b, h], o_ref[h])
        return carry

    lax.fori_loop(0, NUM_BUCKETS, step, 0)


def _bias_table(rel_bias):
    bucket = jnp.asarray(_bucket_table())
    return pl.pallas_call(
        _bias_body,
        out_shape=jax.ShapeDtypeStruct((N_HEADS, BLK, 3 * BLK), f32),
        in_specs=[pl.BlockSpec(memory_space=pltpu.SMEM),
                  pl.BlockSpec((BLK, 3 * BLK), lambda: (0, 0))],
        out_specs=pl.BlockSpec((N_HEADS, BLK, 3 * BLK), lambda: (0, 0, 0)),
        name="bias_table",
    )(rel_bias.astype(f32), bucket)


def _mix_in_body(x_ref, gmix_ref, win_ref, lng_ref, lnb_ref, ws_ref, bs_ref, gout_ref,
                 ya_ref, q_ref, k_ref, v_ref):
    tm = x_ref.shape[0]
    nc = tm // BLK
    a = _rms(x_ref[...], gmix_ref[...])
    z = jnp.dot(a.astype(bf16), win_ref[...], preferred_element_type=f32)
    uv = _gelu_tanh(z[:, :2 * D_A])
    u = uv[:, :D_A]
    v = uv[:, D_A:]
    mu = jnp.mean(v, axis=-1, keepdims=True)
    vc = v - mu
    var = jnp.mean(vc * vc, axis=-1, keepdims=True)
    vn = (vc * lax.rsqrt(var + EPS) * lng_ref[...] + lnb_ref[...]).astype(bf16)
    cols = []
    for h in range(A_HEADS):
        rhs = jnp.concatenate(
            [vn[c * BLK:(c + 1) * BLK, h * BLK:(h + 1) * BLK] for c in range(nc)], axis=1)
        r = jnp.dot(ws_ref[h], rhs, preferred_element_type=f32)
        cols.append(jnp.concatenate(
            [r[:, c * BLK:(c + 1) * BLK] + bs_ref[h] for c in range(nc)], axis=0))
    sv = jnp.concatenate(cols, axis=1)
    ya_ref[...] = _rms(u * sv, gout_ref[...]).astype(bf16)
    q_ref[...] = z[:, 2 * D_A:2 * D_A + D_B].astype(bf16)
    k_ref[...] = z[:, 2 * D_A + D_B:2 * D_A + D_B + BLK].astype(bf16)
    v_ref[...] = z[:, 2 * D_A + D_B + BLK:].astype(bf16)


def _mix_in(x2, gmix, win, lng, lnb, ws, bs, gout_a):
    t = x2.shape[0]
    tm = TM_MIX
    const = lambda shape: pl.BlockSpec(shape, lambda i: (0,) * len(shape))
    return pl.pallas_call(
        _mix_in_body,
        grid=(t // tm,),
        out_shape=(jax.ShapeDtypeStruct((t, D_A), bf16),
                   jax.ShapeDtypeStruct((t, D_B), bf16),
                   jax.ShapeDtypeStruct((t, BLK), bf16),
                   jax.ShapeDtypeStruct((t, BLK), bf16)),
        in_specs=[pl.BlockSpec((tm, D_MODEL), lambda i: (i, 0)),
                  const((1, D_MODEL)), const((D_MODEL, D_IN)),
                  const((1, D_A)), const((1, D_A)),
                  const((A_HEADS, BLK, BLK)), const((A_HEADS, BLK, BLK)),
                  const((1, D_A))],
        out_specs=(pl.BlockSpec((tm, D_A), lambda i: (i, 0)),
                   pl.BlockSpec((tm, D_B), lambda i: (i, 0)),
                   pl.BlockSpec((tm, BLK), lambda i: (i, 0)),
                   pl.BlockSpec((tm, BLK), lambda i: (i, 0))),
        compiler_params=pltpu.CompilerParams(
            dimension_semantics=("parallel",), vmem_limit_bytes=VMEM_LIMIT),
        name="mix_in",
    )(x2, gmix, win, lng, lnb, ws, bs, gout_a)


def _attn_out_body(sink_ref, x_ref, ya_ref, q_ref, kp_ref, km_ref, kn_ref, vp_ref, vm_ref, vn_ref,
                   bias_ref, goutb_ref, wout_ref, gffn_ref, wr_ref, br_ref,
                   h_ref, m_ref, re_ref, rw_ref,
                   kf_ref, vf_ref, e_ref, yb_ref, *, seq_len):
    tq = x_ref.shape[1]
    nb = tq // BLK
    step = pl.program_id(0)
    n_tok_tiles = pl.num_programs(0) - 1
    ti = jnp.minimum(step, n_tok_tiles - 1) % (seq_len // tq)

    @pl.when(step == 0)
    def _():
        yb_ref[1] = jnp.zeros(yb_ref.shape[1:], f32)

    ybn = _rms(yb_ref[1], goutb_ref[...]).astype(bf16)
    y = jnp.concatenate([ya_ref[0], ybn], axis=1)
    h = x_ref[0] + jnp.dot(y, wout_ref[...], preferred_element_type=f32)
    h_ref[0] = h
    m = _rms(h, gffn_ref[...])
    m_ref[...] = m.astype(bf16)

    logit_t = jnp.dot(m.astype(bf16), wr_ref[...], preferred_element_type=f32)
    logit = jnp.transpose(logit_t)[:ROUTE_ROWS] + br_ref[...]
    sub = lax.broadcasted_iota(i32, (8, tq), 0)
    lg = logit[0:8]
    mg = jnp.max(lg, axis=0, keepdims=True)
    pg_top = 1.0 / jnp.sum(jnp.exp(lg - mg), axis=0, keepdims=True)
    g_idx = jnp.min(jnp.where(lg == mg, sub, 8), axis=0, keepdims=True)
    sel = logit[8:16]
    for g in range(1, N_GROUPS):
        sel = jnp.where(g_idx == g, logit[8 + 8 * g:16 + 8 * g], sel)
    m1 = jnp.max(sel, axis=0, keepdims=True)
    i1 = jnp.min(jnp.where(sel == m1, sub, 8), axis=0, keepdims=True)
    sel2 = jnp.where(sub == i1, -jnp.inf, sel)
    m2 = jnp.max(sel2, axis=0, keepdims=True)
    i2 = jnp.min(jnp.where(sel2 == m2, sub, 8), axis=0, keepdims=True)
    r = jnp.exp(m2 - m1)
    w1 = pg_top / (1.0 + r)
    w2 = pg_top * r / (1.0 + r)
    e1 = g_idx * E_PER_GROUP + i1
    e2 = g_idx * E_PER_GROUP + i2
    re_ref[...] = jnp.where(sub == 0, e1, jnp.where(sub == 1, e2, 0))
    rw_ref[...] = jnp.where(sub == 0, w1, jnp.where(sub == 1, w2, 0.0))

    kf_ref[0:BLK] = kp_ref[0]
    kf_ref[BLK:BLK + tq] = km_ref[0]
    kf_ref[BLK + tq:] = kn_ref[0]
    vlane = lax.broadcasted_iota(i32, (BLK, BLK), 1)
    for src, r0, r1 in ((vp_ref, 0, BLK), (vn_ref, BLK + tq, tq + 2 * BLK)) + tuple(
            (vm_ref.at[:, pl.ds(c * BLK, BLK), :], BLK + c * BLK, BLK + (c + 1) * BLK)
            for c in range(nb)):
        vals = src[0].astype(f32)
        vf_ref[0, r0:r1] = jnp.where(vlane < HEAD_DIM, vals,
                                     jnp.where(vlane == HEAD_DIM, 1.0, 0.0)).astype(bf16)
        vf_ref[1, r0:r1] = jnp.where(vlane >= HEAD_DIM, vals,
                                     jnp.where(vlane == 0, 1.0, 0.0)).astype(bf16)

    row = lax.broadcasted_iota(i32, (BLK, 3 * BLK), 0)
    col = lax.broadcasted_iota(i32, (BLK, 3 * BLK), 1)
    band = jnp.abs(col - BLK - row) <= WINDOW
    lane = lax.broadcasted_iota(i32, (BLK, BLK), 1)
    low_half = lane < HEAD_DIM

    for n in range(nb):
        r0 = n * BLK
        qb = q_ref[0, pl.ds(r0, BLK), :]
        kb = kf_ref[pl.ds(r0, 3 * BLK), :]
        zero = jnp.zeros((BLK, BLK), bf16)
        lhs = []
        for kh in range(N_KV):
            for g in range(GQA):
                tile = qb[:, g * BLK:(g + 1) * BLK]
                lhs.append(jnp.where(low_half, tile, zero) if kh == 0
                           else jnp.where(low_half, zero, tile))
        lhs = jnp.concatenate(lhs, axis=0)
        s_all = lax.dot_general(lhs, kb, (((1,), (1,)), ((), ())),
                                preferred_element_type=f32)
        kpos = col + ((ti * nb + n - 1) * BLK)
        valid = band & (kpos >= 0) & (kpos < seq_len)
        sink_e = []
        for h in range(N_HEADS):
            s = jnp.where(valid, s_all[h * BLK:(h + 1) * BLK] + bias_ref[h], NEG)
            sk = sink_ref[h]
            mrow = jnp.maximum(jnp.max(s, axis=-1, keepdims=True), sk)
            e_ref[n, h * BLK:(h + 1) * BLK, :] = jnp.exp2(s - mrow).astype(bf16)
            sink_e.append(jnp.exp2(sk - mrow))
        half = GQA * BLK
        pv = [jnp.dot(e_ref[n, kh * half:(kh + 1) * half, :], vf_ref[kh, pl.ds(r0, 3 * BLK), :],
                      preferred_element_type=f32) for kh in range(N_KV)]

        def head_out(h):
            kh, g = divmod(h, GQA)
            rows = pv[kh][g * BLK:(g + 1) * BLK]
            ones_col = HEAD_DIM if kh == 0 else 0
            return rows * (1.0 / (rows[:, ones_col:ones_col + 1] + sink_e[h]))

        for g in range(GQA):
            yb_ref[0, pl.ds(r0, BLK), g * BLK:(g + 1) * BLK] = jnp.where(
                low_half, head_out(g), head_out(GQA + g))

    yb_ref[1] = yb_ref[0]


def _attn_out(sink, x, ya, q, k, v, bias, goutb, wout, gffn, wr, br):
    b, s, _ = x.shape
    tq = TQ_ATT
    nb = tq // BLK
    nblk = s // BLK
    t = b * s
    nq = s // tq
    n_tok_tiles = b * nq
    att = lambda s: jnp.minimum(s, n_tok_tiles - 1)
    epi = lambda s: jnp.maximum(s - 1, 0)
    const = lambda shape: pl.BlockSpec(shape, lambda s, *_: (0,) * len(shape))
    tok_att = lambda w: pl.BlockSpec((1, tq, w), lambda s, *_: (att(s) // nq, att(s) % nq, 0))
    tok_epi = lambda w: pl.BlockSpec((1, tq, w), lambda s, *_: (epi(s) // nq, epi(s) % nq, 0))
    prev = pl.BlockSpec((1, BLK, BLK), lambda s, *_: (
        att(s) // nq, jnp.maximum((att(s) % nq) * nb - 1, 0), 0))
    nxt = pl.BlockSpec((1, BLK, BLK), lambda s, *_: (
        att(s) // nq, jnp.minimum((att(s) % nq) * nb + nb, nblk - 1), 0))
    rows = pl.BlockSpec((tq, D_MODEL), lambda s, *_: (epi(s), 0))
    lanes = lambda rows: pl.BlockSpec((rows, tq), lambda s, *_: (0, epi(s)))
    grid_spec = pltpu.PrefetchScalarGridSpec(
        num_scalar_prefetch=1,
        grid=(n_tok_tiles + 1,),
        in_specs=[tok_epi(D_MODEL), tok_epi(D_A), tok_att(D_B),
                  prev, tok_att(BLK), nxt, prev, tok_att(BLK), nxt,
                  const((N_HEADS, BLK, 3 * BLK)), const((1, D_B)), const((D_MODEL, D_MODEL)),
                  const((1, D_MODEL)), const((D_MODEL, BLK)), const((ROUTE_ROWS, tq))],
        out_specs=(tok_epi(D_MODEL), rows, lanes(8), lanes(8)),
        scratch_shapes=[pltpu.VMEM((tq + 2 * BLK, BLK), bf16),
                        pltpu.VMEM((N_KV, tq + 2 * BLK, BLK), bf16),
                        pltpu.VMEM((nb, N_HEADS * BLK, 3 * BLK), bf16),
                        pltpu.VMEM((2, tq, D_B), f32)])
    return pl.pallas_call(
        functools.partial(_attn_out_body, seq_len=s),
        grid_spec=grid_spec,
        out_shape=(jax.ShapeDtypeStruct((b, s, D_MODEL), f32),
                   jax.ShapeDtypeStruct((t, D_MODEL), bf16),
                   jax.ShapeDtypeStruct((8, t), i32),
                   jax.ShapeDtypeStruct((8, t), f32)),
        compiler_params=pltpu.CompilerParams(
            dimension_semantics=("arbitrary",), vmem_limit_bytes=VMEM_LIMIT),
        name="attn_out",
    )(sink, x, ya, q, k, k, k, v, v, v, bias, goutb, wout, gffn, wr, br)


def _exclusive_prefix(vals):
    a = lax.broadcasted_iota(i32, (N_EXPERTS, N_EXPERTS), 0)
    c = lax.broadcasted_iota(i32, (N_EXPERTS, N_EXPERTS), 1)
    low = jnp.where(c < a, 1.0, 0.0).astype(bf16)
    hi = (vals >> 8).astype(f32).astype(bf16)
    lo = (vals & 255).astype(f32).astype(bf16)
    return (jnp.dot(low, hi, preferred_element_type=f32) * 256.0
            + jnp.dot(low, lo, preferred_element_type=f32)).astype(i32)


def _round_up_pow2(vals, mult):
    log_m = mult.bit_length() - 1
    return ((vals + (mult - 1)) >> log_m) << log_m


def _route_pos_body(re_ref, slot_ref, runs_ref, offs_ref, tot_ref, carry_ref, tri_ref,
                    *, rows_per_tile):
    phase = pl.program_id(0)
    ti = pl.program_id(1)
    tl = re_ref.shape[1]
    eid = lax.broadcasted_iota(i32, (N_EXPERTS, tl), 0)
    hit1 = eid == re_ref[0:1, :]
    hit2 = eid == re_ref[1:2, :]
    onehot = jnp.where(hit1 | hit2, 1.0, 0.0)
    cnt = jnp.sum(onehot, axis=1, keepdims=True).astype(i32)
    run_len = jnp.broadcast_to(_round_up_pow2(cnt, RUN_ALIGN), (N_EXPERTS, BLK))

    @pl.when((phase == 0) & (ti == 0))
    def _():
        tot_ref[...] = jnp.zeros(tot_ref.shape, i32)
        a = lax.broadcasted_iota(i32, (tl, tl), 0)
        c = lax.broadcasted_iota(i32, (tl, tl), 1)
        tri_ref[...] = jnp.where(a < c, 1.0, 0.0).astype(bf16)

    @pl.when(phase == 0)
    def _():
        tot_ref[...] = tot_ref[...] + run_len

    @pl.when((phase == 1) & (ti == 0))
    def _():
        padded = _round_up_pow2(tot_ref[...], rows_per_tile)
        starts = _exclusive_prefix(padded)
        carry_ref[...] = starts
        offs_ref[0] = starts
        offs_ref[1] = padded

    @pl.when(phase == 1)
    def _():
        local_off = _exclusive_prefix(run_len)
        before = jnp.dot(onehot.astype(bf16), tri_ref[...], preferred_element_type=f32)
        slot = before + local_off[:, 0:1].astype(f32)
        s1 = jnp.sum(jnp.where(hit1, slot, 0.0), axis=0, keepdims=True)
        s2 = jnp.sum(jnp.where(hit2, slot, 0.0), axis=0, keepdims=True)
        sub = lax.broadcasted_iota(i32, (8, tl), 0)
        slot_ref[...] = jnp.where(sub == 0, s1.astype(i32), jnp.where(sub == 1, s2.astype(i32), 0))
        runs_ref[0, 0] = local_off
        runs_ref[0, 1] = run_len
        runs_ref[0, 2] = carry_ref[...]
        carry_ref[...] = carry_ref[...] + run_len


def _route_pos(route_e, rows_per_tile):
    t = route_e.shape[1]
    tl = TL_SORT
    return pl.pallas_call(
        functools.partial(_route_pos_body, rows_per_tile=rows_per_tile),
        grid=(2, t // tl),
        out_shape=(jax.ShapeDtypeStruct((8, t), i32),
                   jax.ShapeDtypeStruct((t // tl, 3, N_EXPERTS, BLK), i32),
                   jax.ShapeDtypeStruct((2, N_EXPERTS, BLK), i32)),
        in_specs=[pl.BlockSpec((8, tl), lambda p, i: (0, i))],
        out_specs=(pl.BlockSpec((8, tl), lambda p, i: (0, i * p)),
                   pl.BlockSpec((1, 3, N_EXPERTS, BLK), lambda p, i: (i * p, 0, 0, 0)),
                   pl.BlockSpec((2, N_EXPERTS, BLK), lambda p, i: (0, 0, 0))),
        scratch_shapes=[pltpu.VMEM((N_EXPERTS, BLK), i32),
                        pltpu.VMEM((N_EXPERTS, BLK), i32),
                        pltpu.VMEM((tl, tl), bf16)],
        compiler_params=pltpu.CompilerParams(
            dimension_semantics=("arbitrary", "arbitrary"), vmem_limit_bytes=VMEM_LIMIT),
        name="route_pos",
    )(route_e)


def _for_each_run_piece(runs_ref, fn):
    def per_expert(e, carry):
        local_off = runs_ref[0, 0, e]
        length = runs_ref[0, 0, N_EXPERTS + e]
        global_off = runs_ref[0, 0, 2 * N_EXPERTS + e]
        units = length >> (RUN_ALIGN.bit_length() - 1)

        def big_piece(k, inner):
            done = k * BIG_PIECE
            fn(pl.multiple_of(local_off + done, RUN_ALIGN),
               pl.multiple_of(global_off + done, RUN_ALIGN), BIG_PIECE)
            return inner

        lax.fori_loop(0, units >> SMALL_BITS, big_piece, 0)
        for b in range(SMALL_BITS):
            @pl.when(((units >> b) & 1) == 1)
            def _():
                done = ((units >> (b + 1)) << (b + 1)) * RUN_ALIGN
                fn(pl.multiple_of(local_off + done, RUN_ALIGN),
                   pl.multiple_of(global_off + done, RUN_ALIGN), RUN_ALIGN << b)
        return carry

    lax.fori_loop(0, N_EXPERTS, per_expert, 0)


def _tile_run_rows(runs_ref):
    return lax.fori_loop(0, N_EXPERTS, lambda e, acc: acc + runs_ref[0, 0, N_EXPERTS + e], 0)


def _wait_rows(total_rows, make_copy):
    units = total_rows >> (RUN_ALIGN.bit_length() - 1)
    for b in range((LB_SORT // RUN_ALIGN).bit_length()):
        @pl.when(((units >> b) & 1) == 1)
        def _():
            make_copy(RUN_ALIGN << b).wait()


def _sort_rows_body(ends_ref, padded_ref, runs_ref, m_ref, slot_ref, rw_ref, xs_ref,
                    local_ref, zero_ref, pending_ref, sem_ref, zsem_ref, *, rows_per_tile):
    tl = m_ref.shape[0]
    lb = local_ref.shape[1]
    step = pl.program_id(0)
    cur = step % 2
    n_tiles = xs_ref.shape[0] // rows_per_tile
    n_used = ends_ref[N_EXPERTS - 1] // rows_per_tile

    def zero_tile(tile):
        start = pl.multiple_of(tile * rows_per_tile, rows_per_tile)
        return pltpu.make_async_copy(zero_ref, xs_ref.at[pl.ds(start, rows_per_tile), :], zsem_ref)

    @pl.when(pl.program_id(0) == 0)
    def _():
        zero_ref[...] = jnp.zeros(zero_ref.shape, bf16)
        for wait in (False, True):
            for e in range(N_EXPERTS):
                @pl.when(padded_ref[e] > 0)
                def _():
                    cp = zero_tile(ends_ref[e] // rows_per_tile - 1)
                    cp.wait() if wait else cp.start()

            def spare(j, carry):
                cp = zero_tile(n_used + j)
                cp.wait() if wait else cp.start()
                return carry

            lax.fori_loop(0, n_tiles - n_used, spare, 0)

    s1 = slot_ref[0:1, :]
    s2 = slot_ref[1:2, :]
    w1 = rw_ref[0:1, :]
    w2 = rw_ref[1:2, :]
    chunk = 256
    lane = lax.broadcasted_iota(i32, (chunk, BLK), 1)
    run_rows = _tile_run_rows(runs_ref)

    def sort_chunk(c0):
        srow = lax.broadcasted_iota(i32, (chunk, tl), 0) + c0
        p1 = srow == s1
        p2 = srow == s2
        perm = jnp.where(p1 | p2, 1.0, 0.0).astype(bf16)
        rows = jnp.dot(perm, m_ref[...], preferred_element_type=f32)
        w = jnp.sum(jnp.where(p1, w1, 0.0) + jnp.where(p2, w2, 0.0), axis=1, keepdims=True)
        w_hi = w.astype(bf16).astype(f32)
        w_lo = w - w_hi
        local_ref[cur, c0:c0 + chunk, :D_MODEL] = rows.astype(bf16)
        local_ref[cur, c0:c0 + chunk, D_MODEL:] = jnp.where(
            lane == 0, w_hi, jnp.where(lane == 1, w_lo, 0.0)).astype(bf16)

    def piece(s, local_row, global_row, rows):
        return pltpu.make_async_copy(local_ref.at[s, pl.ds(local_row, rows), :],
                                     xs_ref.at[pl.ds(global_row, rows), :], sem_ref.at[s])

    def sort_rows_upto(rows):
        for c0 in range(0, rows, chunk):
            sort_chunk(c0)

    short = lb - chunk
    pl.when(run_rows <= short)(functools.partial(sort_rows_upto, short))
    pl.when(run_rows > short)(functools.partial(sort_rows_upto, lb))

    _for_each_run_piece(runs_ref, lambda l, g, n: piece(cur, l, g, n).start())

    @pl.when(step > 0)
    def _():
        _wait_rows(pending_ref[0], lambda n: piece(1 - cur, 0, 0, n))

    pending_ref[0] = run_rows

    @pl.when(step == pl.num_programs(0) - 1)
    def _():
        _wait_rows(pending_ref[0], lambda n: piece(cur, 0, 0, n))


def _sort_rows(ends, padded, run_tiles, m, slot, route_w, total_rows, rows_per_tile):
    t = m.shape[0]
    tl = TL_SORT
    grid_spec = pltpu.PrefetchScalarGridSpec(
        num_scalar_prefetch=2,
        grid=(t // tl,),
        in_specs=[pl.BlockSpec((1, 1, 3 * N_EXPERTS), lambda i, *_: (i, 0, 0),
                               memory_space=pltpu.SMEM),
                  pl.BlockSpec((tl, D_MODEL), lambda i, *_: (i, 0)),
                  pl.BlockSpec((8, tl), lambda i, *_: (0, i)),
                  pl.BlockSpec((8, tl), lambda i, *_: (0, i))],
        out_specs=pl.BlockSpec(memory_space=pl.ANY),
        scratch_shapes=[pltpu.VMEM((2, LB_SORT, XS_COLS), bf16),
                        pltpu.VMEM((rows_per_tile, XS_COLS), bf16),
                        pltpu.SMEM((1,), i32),
                        pltpu.SemaphoreType.DMA((2,)),
                        pltpu.SemaphoreType.DMA(())])
    return pl.pallas_call(
        functools.partial(_sort_rows_body, rows_per_tile=rows_per_tile),
        grid_spec=grid_spec,
        out_shape=jax.ShapeDtypeStruct((total_rows, XS_COLS), bf16),
        compiler_params=pltpu.CompilerParams(
            dimension_semantics=("arbitrary",), vmem_limit_bytes=VMEM_LIMIT),
        name="sort_rows",
    )(ends, padded, run_tiles, m, slot, route_w)


def _experts_body(tile_ref, texp_ref, first_ref, next_ref, slot_ref, nused_ref,
                  xs_hbm, wg_hbm, wu_hbm, wd_hbm, ys_ref,
                  xbuf_ref, sg_ref, su_ref, sd_ref, wg_ref, wu_ref, wd_ref, xsem_ref, sem_ref):
    i = pl.program_id(0)
    n_used = nused_ref[0]
    r = xbuf_ref.shape[1]

    def row_tile(tile):
        s = tile % XS_RING
        start = pl.multiple_of(tile * r, r)
        return pltpu.make_async_copy(xs_hbm.at[pl.ds(start, r), :], xbuf_ref.at[s], xsem_ref.at[s])

    @pl.when(i == 0)
    def _():
        for ahead in range(XS_RING - 1):
            @pl.when(ahead < n_used)
            def _():
                row_tile(ahead).start()

    @pl.when(i + (XS_RING - 1) < n_used)
    def _():
        row_tile(i + (XS_RING - 1)).start()

    def fetches(expert, s):
        return (pltpu.make_async_copy(wg_hbm.at[expert], sg_ref.at[s], sem_ref.at[s]),
                pltpu.make_async_copy(wu_hbm.at[expert], su_ref.at[s], sem_ref.at[s]),
                pltpu.make_async_copy(wd_hbm.at[expert], sd_ref.at[s], sem_ref.at[s]))

    @pl.when(i < n_used)
    def _():
        s = slot_ref[i]
        row_tile(i).wait()
        xs_ref = xbuf_ref.at[i % XS_RING]

        @pl.when(first_ref[i] == 1)
        def _():
            @pl.when(i == 0)
            def _():
                for cp in fetches(texp_ref[0], 0):
                    cp.start()

            for cp in fetches(texp_ref[i], s):
                cp.wait()

            @pl.when(next_ref[i] >= 0)
            def _():
                for cp in fetches(next_ref[i], 1 - s):
                    cp.start()

            wg_ref[...] = sg_ref[s].astype(bf16)
            wu_ref[...] = su_ref[s].astype(bf16)
            wd_ref[...] = sd_ref[s].astype(bf16)

        w_row = (xs_ref[:, D_MODEL:D_MODEL + 1].astype(f32)
                 + xs_ref[:, D_MODEL + 1:D_MODEL + 2].astype(f32))
        gate = jnp.dot(xs_ref[:, :D_MODEL], wg_ref[...], preferred_element_type=f32)
        up = jnp.dot(xs_ref[:, :D_MODEL], wu_ref[...], preferred_element_type=f32)
        hdn = (gate * jax.nn.sigmoid(gate)) * up
        y = jnp.dot(hdn.astype(bf16), wd_ref[...], preferred_element_type=f32)
        ys_ref[...] = (y * w_row).astype(bf16)

    @pl.when(i >= n_used)
    def _():
        ys_ref[...] = jnp.zeros(ys_ref.shape, bf16)


def _experts(tile_idx, tile_exp, first, next_exp, slot, n_used, xs, wg, wu, wd, rows_per_tile):
    r = rows_per_tile
    n_tiles = xs.shape[0] // r
    grid_spec = pltpu.PrefetchScalarGridSpec(
        num_scalar_prefetch=6,
        grid=(n_tiles,),
        in_specs=[pl.BlockSpec(memory_space=pl.ANY),
                  pl.BlockSpec(memory_space=pl.ANY),
                  pl.BlockSpec(memory_space=pl.ANY),
                  pl.BlockSpec(memory_space=pl.ANY)],
        out_specs=pl.BlockSpec((r, D_MODEL), lambda i, *_: (i, 0)),
        scratch_shapes=[pltpu.VMEM((XS_RING, r, XS_COLS), bf16),
                        pltpu.VMEM((2, D_MODEL, D_FF_E), f32),
                        pltpu.VMEM((2, D_MODEL, D_FF_E), f32),
                        pltpu.VMEM((2, D_FF_E, D_MODEL), f32),
                        pltpu.VMEM((D_MODEL, D_FF_E), bf16),
                        pltpu.VMEM((D_MODEL, D_FF_E), bf16),
                        pltpu.VMEM((D_FF_E, D_MODEL), bf16),
                        pltpu.SemaphoreType.DMA((XS_RING,)),
                        pltpu.SemaphoreType.DMA((2,))])
    return pl.pallas_call(
        _experts_body,
        grid_spec=grid_spec,
        out_shape=jax.ShapeDtypeStruct((n_tiles * r, D_MODEL), bf16),
        compiler_params=pltpu.CompilerParams(
            dimension_semantics=("arbitrary",), vmem_limit_bytes=VMEM_LIMIT),
        name="experts",
    )(tile_idx, tile_exp, first, next_exp, slot, n_used, xs, wg, wu, wd)


def _combine_out_body(runs_cur_ref, runs_nxt_ref, h_ref, slot_ref, p_ref, ys_ref,
                      wpg_ref, bpg_ref, wpp_ref, gple_ref, gfin_ref,
                      o_ref, ybuf_ref, moe_ref, sem_ref):
    tk = h_ref.shape[0]
    lb = ybuf_ref.shape[1]
    i = pl.program_id(0)
    n = pl.num_programs(0)
    cur = i % 2

    def piece(s, local_row, global_row, rows):
        return pltpu.make_async_copy(ys_ref.at[pl.ds(global_row, rows), :],
                                     ybuf_ref.at[s, pl.ds(local_row, rows), :], sem_ref.at[s])

    @pl.when(i == 0)
    def _():
        ybuf_ref[...] = jnp.zeros(ybuf_ref.shape, bf16)
        _for_each_run_piece(runs_cur_ref, lambda l, g, r: piece(0, l, g, r).start())

    @pl.when(i + 1 < n)
    def _():
        _for_each_run_piece(runs_nxt_ref, lambda l, g, r: piece(1 - cur, l, g, r).start())

    run_rows = _tile_run_rows(runs_cur_ref)
    _wait_rows(run_rows, lambda r: piece(cur, 0, 0, r))

    def unsort(rows):
        col = lax.broadcasted_iota(i32, (tk, rows), 1)
        unperm = jnp.where((col == slot_ref[:, 0:1]) | (col == slot_ref[:, 1:2]), 1.0, 0.0)
        moe_ref[...] = jnp.dot(unperm.astype(bf16), ybuf_ref[cur, :rows, :],
                               preferred_element_type=f32)

    short = lb - 256
    pl.when(run_rows <= short)(functools.partial(unsort, short))
    pl.when(run_rows > short)(functools.partial(unsort, lb))
    h2 = h_ref[...] + moe_ref[...]
    gate = jax.nn.sigmoid(jnp.dot(h2.astype(bf16), wpg_ref[...], preferred_element_type=f32)
                          + bpg_ref[...])
    pp = jnp.dot(p_ref[...].astype(bf16), wpp_ref[...], preferred_element_type=f32)
    h3 = h2 + gate * _rms(pp, gple_ref[...])
    o_ref[...] = _rms(h3, gfin_ref[...])


def _combine_out(run_tiles, h2d, slot_t, p2d, ys, wpg, bpg, wpp, gple, gfin):
    t = h2d.shape[0]
    tk = TL_SORT
    nt = t // tk
    const = lambda shape: pl.BlockSpec(shape, lambda i: (0,) * len(shape))
    runs = lambda imap: pl.BlockSpec((1, 1, 3 * N_EXPERTS), imap, memory_space=pltpu.SMEM)
    return pl.pallas_call(
        _combine_out_body,
        grid=(nt,),
        out_shape=jax.ShapeDtypeStruct((t, D_MODEL), f32),
        in_specs=[runs(lambda i: (i, 0, 0)),
                  runs(lambda i: (jnp.minimum(i + 1, nt - 1), 0, 0)),
                  pl.BlockSpec((tk, D_MODEL), lambda i: (i, 0)),
                  pl.BlockSpec((tk, 2), lambda i: (i, 0)),
                  pl.BlockSpec((tk, PLE_DIM), lambda i: (i, 0)),
                  pl.BlockSpec(memory_space=pl.ANY),
                  const((D_MODEL, D_MODEL)), const((1, D_MODEL)),
                  const((PLE_DIM, D_MODEL)), const((1, D_MODEL)), const((1, D_MODEL))],
        out_specs=pl.BlockSpec((tk, D_MODEL), lambda i: (i, 0)),
        scratch_shapes=[pltpu.VMEM((2, LB_SORT, D_MODEL), bf16),
                        pltpu.VMEM((tk, D_MODEL), f32),
                        pltpu.SemaphoreType.DMA((2,))],
        compiler_params=pltpu.CompilerParams(
            dimension_semantics=("arbitrary",), vmem_limit_bytes=VMEM_LIMIT),
        name="combine_out",
    )(run_tiles, run_tiles, h2d, slot_t, p2d, ys, wpg, bpg, wpp, gple, gfin)


def _q_perm():
    perm = np.empty((D_B,), np.int32)
    for g in range(GQA):
        for kh in range(N_KV):
            for d in range(HEAD_DIM):
                perm[g * BLK + kh * HEAD_DIM + d] = (kh * GQA + g) * HEAD_DIM + d
    return perm


def kernel(x, p, rel_bias, g_mix, w_in, ln_v_g, ln_v_b, w_spatial, b_spatial, sink, g_out_grp, w_out,
           g_ffn, w_router_group, b_router_group, w_router_expert, b_router_expert, w_gate_e, w_up_e,
           w_down_e, w_ple_proj, g_ple, w_ple_gate, b_ple_gate, g_final):
    b, s, d = x.shape
    t = b * s
    depth = g_mix.shape[0]
    assert depth == 1 and d == D_MODEL
    perm = _q_perm()
    c1, c2 = 2 * D_A, 2 * D_A + D_B
    bias = _bias_table(rel_bias * LOG2E)
    hcur = x.astype(f32)
    for li in range(depth):
        wi = w_in[li]
        win = jnp.concatenate([wi[:, :c1], wi[:, c1:c2][:, perm] * (LOG2E * HEAD_DIM ** -0.5), wi[:, c2:]],
                              axis=1).astype(bf16)
        bs = jnp.broadcast_to(b_spatial[li][:, :, None], (A_HEADS, BLK, BLK)).astype(f32)
        gout = g_out_grp[li]
        wo = w_out[li]
        wout = jnp.concatenate([wo[:D_A], wo[D_A:][perm]], axis=0).astype(bf16)
        wr = jnp.zeros((D_MODEL, BLK), f32)
        wr = wr.at[:, 0:N_GROUPS].set(w_router_group[li]).at[:, 8:ROUTE_ROWS].set(w_router_expert[li])
        wr = wr.astype(bf16)
        br = jnp.full((ROUTE_ROWS,), NEG, f32)
        br = br.at[0:N_GROUPS].set(b_router_group[li]).at[8:].set(b_router_expert[li])
        br = jnp.broadcast_to(br[:, None], (ROUTE_ROWS, TQ_ATT))

        ya, q, k, v = _mix_in(hcur.reshape(t, d), g_mix[li][None], win, ln_v_g[li][None],
                              ln_v_b[li][None], w_spatial[li].astype(bf16), bs, gout[None, :D_A])
        h, m_rows, route_e, route_w = _attn_out(
            sink[li].astype(f32) * LOG2E, hcur, ya.reshape(b, s, D_A), q.reshape(b, s, D_B),
            k.reshape(b, s, BLK), v.reshape(b, s, BLK), bias, gout[None, D_A:][:, perm], wout,
            g_ffn[li][None], wr, br)

        r = R_EXP
        n_sort = t // TL_SORT
        max_rows = 2 * t + n_sort * N_EXPERTS * (RUN_ALIGN - 1)
        n_tiles = -(-max_rows // r) + N_EXPERTS
        slot, runs, offs = _route_pos(route_e, r)
        starts = offs[0, :, 0]
        padded = offs[1, :, 0]
        ends = starts + padded
        n_used = ends[-1] // r
        tile_idx = jnp.minimum(jnp.arange(n_tiles, dtype=i32), n_used - 1)
        tile_exp = jnp.minimum(
            jnp.sum((ends[None, :] <= (tile_idx * r)[:, None]).astype(i32), axis=1), N_EXPERTS - 1)
        tile_exp = tile_exp.astype(i32)
        first = jnp.concatenate([jnp.ones((1,), i32),
                                 (tile_exp[1:] != tile_exp[:-1]).astype(i32)])
        next_tile = ends[tile_exp] // r
        next_exp = jnp.where(next_tile < n_used, tile_exp[jnp.minimum(next_tile, n_tiles - 1)], -1)
        fetch_slot = (jnp.cumsum(first) - 1) % 2
        run_tiles = runs[:, :, :, 0].reshape(n_sort, 1, 3 * N_EXPERTS)
        xs = _sort_rows(ends, padded, run_tiles, m_rows, slot, route_w, n_tiles * r, r)
        ys = _experts(tile_idx, tile_exp, first, next_exp.astype(i32), fetch_slot.astype(i32),
                      n_used.reshape(1), xs, w_gate_e[li], w_up_e[li], w_down_e[li], r)

        out = _combine_out(run_tiles, h.reshape(t, d), slot[:2].T,
                           p[li].reshape(t, PLE_DIM), ys, w_ple_gate[li].astype(bf16),
                           b_ple_gate[li][None], w_ple_proj[li].astype(bf16), g_ple[li][None],
                           g_final[None])
        hcur = out.reshape(b, s, d)
    return hcur
```

```python
import functools
import math

import jax
import jax.numpy as jnp
import numpy as np
from jax import lax
from jax.experimental import pallas as pl
from jax.experimental.pallas import tpu as pltpu

D_MODEL = 1024
D_A = 512
D_B = 512
BLK = 128
A_HEADS = 4
HEAD_DIM = 64
N_HEADS = 8
N_KV = 2
GQA = 4
WINDOW = 128
NUM_BUCKETS = 32
MAX_DIST = 128
D_IN = 2 * D_A + D_B + 2 * N_KV * HEAD_DIM
N_GROUPS = 4
E_PER_GROUP = 8
N_EXPERTS = 32
D_FF_E = 256
PLE_DIM = 256
EPS = 1e-6
NEG = -1e30
LOG2E = math.log2(math.e)

TM_MIX = 512
TQ_ATT = 512
TL_SORT = 512
RUN_ALIGN = 16
SMALL_BITS = 2
BIG_PIECE = RUN_ALIGN << SMALL_BITS
LB_SORT = 2 * TL_SORT + N_EXPERTS * RUN_ALIGN
R_EXP = 512
XS_COLS = D_MODEL + BLK
XS_RING = 3
ROUTE_ROWS = 8 + N_EXPERTS
VMEM_LIMIT = 48 * 1024 * 1024

f32 = jnp.float32
bf16 = jnp.bfloat16
i32 = jnp.int32


def _rms(x, g):
    return x * lax.rsqrt(jnp.mean(x * x, axis=-1, keepdims=True) + EPS) * g


def _gelu_tanh(x):
    c = math.sqrt(2.0 / math.pi)
    return x * (0.5 * (1.0 + jnp.tanh(c * (x + 0.044715 * (x * x * x)))))


def _bucket_table():
    n = NUM_BUCKETS // 2
    max_exact = n // 2
    i = np.arange(BLK)[:, None]
    j = np.arange(3 * BLK)[None, :]
    rel = j - BLK - i
    ret = np.where(rel > 0, n, 0)
    a = np.abs(rel)
    large = max_exact + (np.log(np.maximum(a, 1).astype(np.float64) / max_exact)
                         / math.log(MAX_DIST / max_exact) * (n - max_exact)).astype(np.int32)
    large = np.minimum(large, n - 1)
    return (ret + np.where(a < max_exact, a, large)).astype(np.int32)


def _bias_body(rb_ref, bucket_ref, o_ref):
    bucket = bucket_ref[...]
    o_ref[...] = jnp.zeros(o_ref.shape, f32)

    def step(b, carry):
        hit = bucket == b
        for h in range(N_HEADS):
            o_ref[h] = jnp.where(hit, rb_ref[b, h], o_ref[h])
        return carry

    lax.fori_loop(0, NUM_BUCKETS, step, 0)


def _bias_table(rel_bias):
    bucket = jnp.asarray(_bucket_table())
    return pl.pallas_call(
        _bias_body,
        out_shape=jax.ShapeDtypeStruct((N_HEADS, BLK, 3 * BLK), f32),
        in_specs=[pl.BlockSpec(memory_space=pltpu.SMEM),
                  pl.BlockSpec((BLK, 3 * BLK), lambda: (0, 0))],
        out_specs=pl.BlockSpec((N_HEADS, BLK, 3 * BLK), lambda: (0, 0, 0)),
        name="bias_table",
    )(rel_bias.astype(f32), bucket)


def _mix_in_body(x_ref, gmix_ref, win_ref, lng_ref, lnb_ref, ws_ref, bs_ref, gout_ref,
                 ya_ref, q_ref, k_ref, v_ref):
    tm = x_ref.shape[0]
    nc = tm // BLK
    a = _rms(x_ref[...], gmix_ref[...])
    z = jnp.dot(a.astype(bf16), win_ref[...], preferred_element_type=f32)
    uv = _gelu_tanh(z[:, :2 * D_A])
    u = uv[:, :D_A]
    v = uv[:, D_A:]
    mu = jnp.mean(v, axis=-1, keepdims=True)
    vc = v - mu
    var = jnp.mean(vc * vc, axis=-1, keepdims=True)
    vn = (vc * lax.rsqrt(var + EPS) * lng_ref[...] + lnb_ref[...]).astype(bf16)
    cols = []
    for h in range(A_HEADS):
        rhs = jnp.concatenate(
            [vn[c * BLK:(c + 1) * BLK, h * BLK:(h + 1) * BLK] for c in range(nc)], axis=1)
        r = jnp.dot(ws_ref[h], rhs, preferred_element_type=f32)
        cols.append(jnp.concatenate(
            [r[:, c * BLK:(c + 1) * BLK] + bs_ref[h] for c in range(nc)], axis=0))
    sv = jnp.concatenate(cols, axis=1)
    ya_ref[...] = _rms(u * sv, gout_ref[...]).astype(bf16)
    q_ref[...] = z[:, 2 * D_A:2 * D_A + D_B].astype(bf16)
    k_ref[...] = z[:, 2 * D_A + D_B:2 * D_A + D_B + BLK].astype(bf16)
    v_ref[...] = z[:, 2 * D_A + D_B + BLK:].astype(bf16)


def _mix_in(x2, gmix, win, lng, lnb, ws, bs, gout_a):
    t = x2.shape[0]
    tm = TM_MIX
    const = lambda shape: pl.BlockSpec(shape, lambda i: (0,) * len(shape))
    return pl.pallas_call(
        _mix_in_body,
        grid=(t // tm,),
        out_shape=(jax.ShapeDtypeStruct((t, D_A), bf16),
                   jax.ShapeDtypeStruct((t, D_B), bf16),
                   jax.ShapeDtypeStruct((t, BLK), bf16),
                   jax.ShapeDtypeStruct((t, BLK), bf16)),
        in_specs=[pl.BlockSpec((tm, D_MODEL), lambda i: (i, 0)),
                  const((1, D_MODEL)), const((D_MODEL, D_IN)),
                  const((1, D_A)), const((1, D_A)),
                  const((A_HEADS, BLK, BLK)), const((A_HEADS, BLK, BLK)),
                  const((1, D_A))],
        out_specs=(pl.BlockSpec((tm, D_A), lambda i: (i, 0)),
                   pl.BlockSpec((tm, D_B), lambda i: (i, 0)),
                   pl.BlockSpec((tm, BLK), lambda i: (i, 0)),
                   pl.BlockSpec((tm, BLK), lambda i: (i, 0))),
        compiler_params=pltpu.CompilerParams(
            dimension_semantics=("parallel",), vmem_limit_bytes=VMEM_LIMIT),
        name="mix_in",
    )(x2, gmix, win, lng, lnb, ws, bs, gout_a)


def _attn_out_body(sink_ref, x_ref, ya_ref, q_ref, kp_ref, km_ref, kn_ref, vp_ref, vm_ref, vn_ref,
                   bias_ref, goutb_ref, wout_ref, gffn_ref, wr_ref, br_ref,
                   h_ref, m_ref, re_ref, rw_ref,
                   kf_ref, vf_ref, e_ref, yb_ref, *, seq_len):
    tq = x_ref.shape[1]
    nb = tq // BLK
    step = pl.program_id(0)
    n_tok_tiles = pl.num_programs(0) - 1
    ti = jnp.minimum(step, n_tok_tiles - 1) % (seq_len // tq)

    @pl.when(step == 0)
    def _():
        yb_ref[1] = jnp.zeros(yb_ref.shape[1:], f32)

    ybn = _rms(yb_ref[1], goutb_ref[...]).astype(bf16)
    y = jnp.concatenate([ya_ref[0], ybn], axis=1)
    h = x_ref[0] + jnp.dot(y, wout_ref[...], preferred_element_type=f32)
    h_ref[0] = h
    m = _rms(h, gffn_ref[...])
    m_ref[...] = m.astype(bf16)

    logit_t = jnp.dot(m.astype(bf16), wr_ref[...], preferred_element_type=f32)
    logit = jnp.transpose(logit_t)[:ROUTE_ROWS] + br_ref[...]
    sub = lax.broadcasted_iota(i32, (8, tq), 0)
    lg = logit[0:8]
    mg = jnp.max(lg, axis=0, keepdims=True)
    pg_top = 1.0 / jnp.sum(jnp.exp(lg - mg), axis=0, keepdims=True)
    g_idx = jnp.min(jnp.where(lg == mg, sub, 8), axis=0, keepdims=True)
    sel = logit[8:16]
    for g in range(1, N_GROUPS):
        sel = jnp.where(g_idx == g, logit[8 + 8 * g:16 + 8 * g], sel)
    m1 = jnp.max(sel, axis=0, keepdims=True)
    i1 = jnp.min(jnp.where(sel == m1, sub, 8), axis=0, keepdims=True)
    sel2 = jnp.where(sub == i1, -jnp.inf, sel)
    m2 = jnp.max(sel2, axis=0, keepdims=True)
    i2 = jnp.min(jnp.where(sel2 == m2, sub, 8), axis=0, keepdims=True)
    r = jnp.exp(m2 - m1)
    w1 = pg_top / (1.0 + r)
    w2 = pg_top * r / (1.0 + r)
    e1 = g_idx * E_PER_GROUP + i1
    e2 = g_idx * E_PER_GROUP + i2
    re_ref[...] = jnp.where(sub == 0, e1, jnp.where(sub == 1, e2, 0))
    rw_ref[...] = jnp.where(sub == 0, w1, jnp.where(sub == 1, w2, 0.0))

    kf_ref[0:BLK] = kp_ref[0]
    kf_ref[BLK:BLK + tq] = km_ref[0]
    kf_ref[BLK + tq:] = kn_ref[0]
    vlane = lax.broadcasted_iota(i32, (BLK, BLK), 1)
    for src, r0, r1 in ((vp_ref, 0, BLK), (vn_ref, BLK + tq, tq + 2 * BLK)) + tuple(
            (vm_ref.at[:, pl.ds(c * BLK, BLK), :], BLK + c * BLK, BLK + (c + 1) * BLK)
            for c in range(nb)):
        vals = src[0].astype(f32)
        vf_ref[0, r0:r1] = jnp.where(vlane < HEAD_DIM, vals,
                                     jnp.where(vlane == HEAD_DIM, 1.0, 0.0)).astype(bf16)
        vf_ref[1, r0:r1] = jnp.where(vlane >= HEAD_DIM, vals,
                                     jnp.where(vlane == 0, 1.0, 0.0)).astype(bf16)

    row = lax.broadcasted_iota(i32, (BLK, 3 * BLK), 0)
    col = lax.broadcasted_iota(i32, (BLK, 3 * BLK), 1)
    band = jnp.abs(col - BLK - row) <= WINDOW
    lane = lax.broadcasted_iota(i32, (BLK, BLK), 1)
    low_half = lane < HEAD_DIM

    for n in range(nb):
        r0 = n * BLK
        qb = q_ref[0, pl.ds(r0, BLK), :]
        kb = kf_ref[pl.ds(r0, 3 * BLK), :]
        zero = jnp.zeros((BLK, BLK), bf16)
        lhs = []
        for kh in range(N_KV):
            for g in range(GQA):
                tile = qb[:, g * BLK:(g + 1) * BLK]
                lhs.append(jnp.where(low_half, tile, zero) if kh == 0
                           else jnp.where(low_half, zero, tile))
        lhs = jnp.concatenate(lhs, axis=0)
        s_all = lax.dot_general(lhs, kb, (((1,), (1,)), ((), ())),
                                preferred_element_type=f32)
        kpos = col + ((ti * nb + n - 1) * BLK)
        valid = band & (kpos >= 0) & (kpos < seq_len)
        sink_e = []
        for h in range(N_HEADS):
            s = jnp.where(valid, s_all[h * BLK:(h + 1) * BLK] + bias_ref[h], NEG)
            sk = sink_ref[h]
            mrow = jnp.maximum(jnp.max(s, axis=-1, keepdims=True), sk)
            e_ref[n, h * BLK:(h + 1) * BLK, :] = jnp.exp2(s - mrow).astype(bf16)
            sink_e.append(jnp.exp2(sk - mrow))
        half = GQA * BLK
        pv = [jnp.dot(e_ref[n, kh * half:(kh + 1) * half, :], vf_ref[kh, pl.ds(r0, 3 * BLK), :],
                      preferred_element_type=f32) for kh in range(N_KV)]

        def head_out(h):
            kh, g = divmod(h, GQA)
            rows = pv[kh][g * BLK:(g + 1) * BLK]
            ones_col = HEAD_DIM if kh == 0 else 0
            return rows * (1.0 / (rows[:, ones_col:ones_col + 1] + sink_e[h]))

        for g in range(GQA):
            yb_ref[0, pl.ds(r0, BLK), g * BLK:(g + 1) * BLK] = jnp.where(
                low_half, head_out(g), head_out(GQA + g))

    yb_ref[1] = yb_ref[0]


def _attn_out(sink, x, ya, q, k, v, bias, goutb, wout, gffn, wr, br):
    b, s, _ = x.shape
    tq = TQ_ATT
    nb = tq // BLK
    nblk = s // BLK
    t = b * s
    nq = s // tq
    n_tok_tiles = b * nq
    att = lambda s: jnp.minimum(s, n_tok_tiles - 1)
    epi = lambda s: jnp.maximum(s - 1, 0)
    const = lambda shape: pl.BlockSpec(shape, lambda s, *_: (0,) * len(shape))
    tok_att = lambda w: pl.BlockSpec((1, tq, w), lambda s, *_: (att(s) // nq, att(s) % nq, 0))
    tok_epi = lambda w: pl.BlockSpec((1, tq, w), lambda s, *_: (epi(s) // nq, epi(s) % nq, 0))
    prev = pl.BlockSpec((1, BLK, BLK), lambda s, *_: (
        att(s) // nq, jnp.maximum((att(s) % nq) * nb - 1, 0), 0))
    nxt = pl.BlockSpec((1, BLK, BLK), lambda s, *_: (
        att(s) // nq, jnp.minimum((att(s) % nq) * nb + nb, nblk - 1), 0))
    rows = pl.BlockSpec((tq, D_MODEL), lambda s, *_: (epi(s), 0))
    lanes = lambda rows: pl.BlockSpec((rows, tq), lambda s, *_: (0, epi(s)))
    grid_spec = pltpu.PrefetchScalarGridSpec(
        num_scalar_prefetch=1,
        grid=(n_tok_tiles + 1,),
        in_specs=[tok_epi(D_MODEL), tok_epi(D_A), tok_att(D_B),
                  prev, tok_att(BLK), nxt, prev, tok_att(BLK), nxt,
                  const((N_HEADS, BLK, 3 * BLK)), const((1, D_B)), const((D_MODEL, D_MODEL)),
                  const((1, D_MODEL)), const((D_MODEL, BLK)), const((ROUTE_ROWS, tq))],
        out_specs=(tok_epi(D_MODEL), rows, lanes(8), lanes(8)),
        scratch_shapes=[pltpu.VMEM((tq + 2 * BLK, BLK), bf16),
                        pltpu.VMEM((N_KV, tq + 2 * BLK, BLK), bf16),
                        pltpu.VMEM((nb, N_HEADS * BLK, 3 * BLK), bf16),
                        pltpu.VMEM((2, tq, D_B), f32)])
    return pl.pallas_call(
        functools.partial(_attn_out_body, seq_len=s),
        grid_spec=grid_spec,
        out_shape=(jax.ShapeDtypeStruct((b, s, D_MODEL), f32),
                   jax.ShapeDtypeStruct((t, D_MODEL), bf16),
                   jax.ShapeDtypeStruct((8, t), i32),
                   jax.ShapeDtypeStruct((8, t), f32)),
        compiler_params=pltpu.CompilerParams(
            dimension_semantics=("arbitrary",), vmem_limit_bytes=VMEM_LIMIT),
        name="attn_out",
    )(sink, x, ya, q, k, k, k, v, v, v, bias, goutb, wout, gffn, wr, br)


def _exclusive_prefix(vals):
    a = lax.broadcasted_iota(i32, (N_EXPERTS, N_EXPERTS), 0)
    c = lax.broadcasted_iota(i32, (N_EXPERTS, N_EXPERTS), 1)
    low = jnp.where(c < a, 1.0, 0.0).astype(bf16)
    hi = (vals >> 8).astype(f32).astype(bf16)
    lo = (vals & 255).astype(f32).astype(bf16)
    return (jnp.dot(low, hi, preferred_element_type=f32) * 256.0
            + jnp.dot(low, lo, preferred_element_type=f32)).astype(i32)


def _round_up_pow2(vals, mult):
    log_m = mult.bit_length() - 1
    return ((vals + (mult - 1)) >> log_m) << log_m


def _route_pos_body(re_ref, slot_ref, runs_ref, offs_ref, tot_ref, carry_ref, tri_ref,
                    *, rows_per_tile):
    phase = pl.program_id(0)
    ti = pl.program_id(1)
    tl = re_ref.shape[1]
    eid = lax.broadcasted_iota(i32, (N_EXPERTS, tl), 0)
    hit1 = eid == re_ref[0:1, :]
    hit2 = eid == re_ref[1:2, :]
    onehot = jnp.where(hit1 | hit2, 1.0, 0.0)
    cnt = jnp.sum(onehot, axis=1, keepdims=True).astype(i32)
    run_len = jnp.broadcast_to(_round_up_pow2(cnt, RUN_ALIGN), (N_EXPERTS, BLK))

    @pl.when((phase == 0) & (ti == 0))
    def _():
        tot_ref[...] = jnp.zeros(tot_ref.shape, i32)
        a = lax.broadcasted_iota(i32, (tl, tl), 0)
        c = lax.broadcasted_iota(i32, (tl, tl), 1)
        tri_ref[...] = jnp.where(a < c, 1.0, 0.0).astype(bf16)

    @pl.when(phase == 0)
    def _():
        tot_ref[...] = tot_ref[...] + run_len

    @pl.when((phase == 1) & (ti == 0))
    def _():
        padded = _round_up_pow2(tot_ref[...], rows_per_tile)
        starts = _exclusive_prefix(padded)
        carry_ref[...] = starts
        offs_ref[0] = starts
        offs_ref[1] = padded

    @pl.when(phase == 1)
    def _():
        local_off = _exclusive_prefix(run_len)
        before = jnp.dot(onehot.astype(bf16), tri_ref[...], preferred_element_type=f32)
        slot = before + local_off[:, 0:1].astype(f32)
        s1 = jnp.sum(jnp.where(hit1, slot, 0.0), axis=0, keepdims=True)
        s2 = jnp.sum(jnp.where(hit2, slot, 0.0), axis=0, keepdims=True)
        sub = lax.broadcasted_iota(i32, (8, tl), 0)
        slot_ref[...] = jnp.where(sub == 0, s1.astype(i32), jnp.where(sub == 1, s2.astype(i32), 0))
        runs_ref[0, 0] = local_off
        runs_ref[0, 1] = run_len
        runs_ref[0, 2] = carry_ref[...]
        carry_ref[...] = carry_ref[...] + run_len


def _route_pos(route_e, rows_per_tile):
    t = route_e.shape[1]
    tl = TL_SORT
    return pl.pallas_call(
        functools.partial(_route_pos_body, rows_per_tile=rows_per_tile),
        grid=(2, t // tl),
        out_shape=(jax.ShapeDtypeStruct((8, t), i32),
                   jax.ShapeDtypeStruct((t // tl, 3, N_EXPERTS, BLK), i32),
                   jax.ShapeDtypeStruct((2, N_EXPERTS, BLK), i32)),
        in_specs=[pl.BlockSpec((8, tl), lambda p, i: (0, i))],
        out_specs=(pl.BlockSpec((8, tl), lambda p, i: (0, i * p)),
                   pl.BlockSpec((1, 3, N_EXPERTS, BLK), lambda p, i: (i * p, 0, 0, 0)),
                   pl.BlockSpec((2, N_EXPERTS, BLK), lambda p, i: (0, 0, 0))),
        scratch_shapes=[pltpu.VMEM((N_EXPERTS, BLK), i32),
                        pltpu.VMEM((N_EXPERTS, BLK), i32),
                        pltpu.VMEM((tl, tl), bf16)],
        compiler_params=pltpu.CompilerParams(
            dimension_semantics=("arbitrary", "arbitrary"), vmem_limit_bytes=VMEM_LIMIT),
        name="route_pos",
    )(route_e)


def _for_each_run_piece(runs_ref, fn):
    def per_expert(e, carry):
        local_off = runs_ref[0, 0, e]
        length = runs_ref[0, 0, N_EXPERTS + e]
        global_off = runs_ref[0, 0, 2 * N_EXPERTS + e]
        units = length >> (RUN_ALIGN.bit_length() - 1)

        def big_piece(k, inner):
            done = k * BIG_PIECE
            fn(pl.multiple_of(local_off + done, RUN_ALIGN),
               pl.multiple_of(global_off + done, RUN_ALIGN), BIG_PIECE)
            return inner

        lax.fori_loop(0, units >> SMALL_BITS, big_piece, 0)
        for b in range(SMALL_BITS):
            @pl.when(((units >> b) & 1) == 1)
            def _():
                done = ((units >> (b + 1)) << (b + 1)) * RUN_ALIGN
                fn(pl.multiple_of(local_off + done, RUN_ALIGN),
                   pl.multiple_of(global_off + done, RUN_ALIGN), RUN_ALIGN << b)
        return carry

    lax.fori_loop(0, N_EXPERTS, per_expert, 0)


def _tile_run_rows(runs_ref):
    return lax.fori_loop(0, N_EXPERTS, lambda e, acc: acc + runs_ref[0, 0, N_EXPERTS + e], 0)


def _wait_rows(total_rows, make_copy):
    units = total_rows >> (RUN_ALIGN.bit_length() - 1)
    for b in range((LB_SORT // RUN_ALIGN).bit_length()):
        @pl.when(((units >> b) & 1) == 1)
        def _():
            make_copy(RUN_ALIGN << b).wait()


def _sort_rows_body(ends_ref, padded_ref, runs_ref, m_ref, slot_ref, rw_ref, xs_ref,
                    local_ref, zero_ref, pending_ref, sem_ref, zsem_ref, *, rows_per_tile):
    tl = m_ref.shape[0]
    lb = local_ref.shape[1]
    step = pl.program_id(0)
    cur = step % 2
    n_tiles = xs_ref.shape[0] // rows_per_tile
    n_used = ends_ref[N_EXPERTS - 1] // rows_per_tile

    def zero_tile(tile):
        start = pl.multiple_of(tile * rows_per_tile, rows_per_tile)
        return pltpu.make_async_copy(zero_ref, xs_ref.at[pl.ds(start, rows_per_tile), :], zsem_ref)

    @pl.when(pl.program_id(0) == 0)
    def _():
        zero_ref[...] = jnp.zeros(zero_ref.shape, bf16)
        for wait in (False, True):
            for e in range(N_EXPERTS):
                @pl.when(padded_ref[e] > 0)
                def _():
                    cp = zero_tile(ends_ref[e] // rows_per_tile - 1)
                    cp.wait() if wait else cp.start()

            def spare(j, carry):
                cp = zero_tile(n_used + j)
                cp.wait() if wait else cp.start()
                return carry

            lax.fori_loop(0, n_tiles - n_used, spare, 0)

    s1 = slot_ref[0:1, :]
    s2 = slot_ref[1:2, :]
    w1 = rw_ref[0:1, :]
    w2 = rw_ref[1:2, :]
    chunk = 256
    lane = lax.broadcasted_iota(i32, (chunk, BLK), 1)
    run_rows = _tile_run_rows(runs_ref)

    def sort_chunk(c0):
        srow = lax.broadcasted_iota(i32, (chunk, tl), 0) + c0
        p1 = srow == s1
        p2 = srow == s2
        perm = jnp.where(p1 | p2, 1.0, 0.0).astype(bf16)
        rows = jnp.dot(perm, m_ref[...], preferred_element_type=f32)
        w = jnp.sum(jnp.where(p1, w1, 0.0) + jnp.where(p2, w2, 0.0), axis=1, keepdims=True)
        w_hi = w.astype(bf16).astype(f32)
        w_lo = w - w_hi
        local_ref[cur, c0:c0 + chunk, :D_MODEL] = rows.astype(bf16)
        local_ref[cur, c0:c0 + chunk, D_MODEL:] = jnp.where(
            lane == 0, w_hi, jnp.where(lane == 1, w_lo, 0.0)).astype(bf16)

    def piece(s, local_row, global_row, rows):
        return pltpu.make_async_copy(local_ref.at[s, pl.ds(local_row, rows), :],
                                     xs_ref.at[pl.ds(global_row, rows), :], sem_ref.at[s])

    def sort_rows_upto(rows):
        for c0 in range(0, rows, chunk):
            sort_chunk(c0)

    short = lb - chunk
    pl.when(run_rows <= short)(functools.partial(sort_rows_upto, short))
    pl.when(run_rows > short)(functools.partial(sort_rows_upto, lb))

    _for_each_run_piece(runs_ref, lambda l, g, n: piece(cur, l, g, n).start())

    @pl.when(step > 0)
    def _():
        _wait_rows(pending_ref[0], lambda n: piece(1 - cur, 0, 0, n))

    pending_ref[0] = run_rows

    @pl.when(step == pl.num_programs(0) - 1)
    def _():
        _wait_rows(pending_ref[0], lambda n: piece(cur, 0, 0, n))


def _sort_rows(ends, padded, run_tiles, m, slot, route_w, total_rows, rows_per_tile):
    t = m.shape[0]
    tl = TL_SORT
    grid_spec = pltpu.PrefetchScalarGridSpec(
        num_scalar_prefetch=2,
        grid=(t // tl,),
        in_specs=[pl.BlockSpec((1, 1, 3 * N_EXPERTS), lambda i, *_: (i, 0, 0),
                               memory_space=pltpu.SMEM),
                  pl.BlockSpec((tl, D_MODEL), lambda i, *_: (i, 0)),
                  pl.BlockSpec((8, tl), lambda i, *_: (0, i)),
                  pl.BlockSpec((8, tl), lambda i, *_: (0, i))],
        out_specs=pl.BlockSpec(memory_space=pl.ANY),
        scratch_shapes=[pltpu.VMEM((2, LB_SORT, XS_COLS), bf16),
                        pltpu.VMEM((rows_per_tile, XS_COLS), bf16),
                        pltpu.SMEM((1,), i32),
                        pltpu.SemaphoreType.DMA((2,)),
                        pltpu.SemaphoreType.DMA(())])
    return pl.pallas_call(
        functools.partial(_sort_rows_body, rows_per_tile=rows_per_tile),
        grid_spec=grid_spec,
        out_shape=jax.ShapeDtypeStruct((total_rows, XS_COLS), bf16),
        compiler_params=pltpu.CompilerParams(
            dimension_semantics=("arbitrary",), vmem_limit_bytes=VMEM_LIMIT),
        name="sort_rows",
    )(ends, padded, run_tiles, m, slot, route_w)


def _experts_body(tile_ref, texp_ref, first_ref, next_ref, slot_ref, nused_ref,
                  xs_hbm, wg_hbm, wu_hbm, wd_hbm, ys_ref,
                  xbuf_ref, sg_ref, su_ref, sd_ref, wg_ref, wu_ref, wd_ref, xsem_ref, sem_ref):
    i = pl.program_id(0)
    n_used = nused_ref[0]
    r = xbuf_ref.shape[1]

    def row_tile(tile):
        s = tile % XS_RING
        start = pl.multiple_of(tile * r, r)
        return pltpu.make_async_copy(xs_hbm.at[pl.ds(start, r), :], xbuf_ref.at[s], xsem_ref.at[s])

    @pl.when(i == 0)
    def _():
        for ahead in range(XS_RING - 1):
            @pl.when(ahead < n_used)
            def _():
                row_tile(ahead).start()

    @pl.when(i + (XS_RING - 1) < n_used)
    def _():
        row_tile(i + (XS_RING - 1)).start()

    def fetches(expert, s):
        return (pltpu.make_async_copy(wg_hbm.at[expert], sg_ref.at[s], sem_ref.at[s]),
                pltpu.make_async_copy(wu_hbm.at[expert], su_ref.at[s], sem_ref.at[s]),
                pltpu.make_async_copy(wd_hbm.at[expert], sd_ref.at[s], sem_ref.at[s]))

    @pl.when(i < n_used)
    def _():
        s = slot_ref[i]
        row_tile(i).wait()
        xs_ref = xbuf_ref.at[i % XS_RING]

        @pl.when(first_ref[i] == 1)
        def _():
            @pl.when(i == 0)
            def _():
                for cp in fetches(texp_ref[0], 0):
                    cp.start()

            for cp in fetches(texp_ref[i], s):
                cp.wait()

            @pl.when(next_ref[i] >= 0)
            def _():
                for cp in fetches(next_ref[i], 1 - s):
                    cp.start()

            wg_ref[...] = sg_ref[s].astype(bf16)
            wu_ref[...] = su_ref[s].astype(bf16)
            wd_ref[...] = sd_ref[s].astype(bf16)

        w_row = (xs_ref[:, D_MODEL:D_MODEL + 1].astype(f32)
                 + xs_ref[:, D_MODEL + 1:D_MODEL + 2].astype(f32))
        gate = jnp.dot(xs_ref[:, :D_MODEL], wg_ref[...], preferred_element_type=f32)
        up = jnp.dot(xs_ref[:, :D_MODEL], wu_ref[...], preferred_element_type=f32)
        hdn = (gate * jax.nn.sigmoid(gate)) * up
        y = jnp.dot(hdn.astype(bf16), wd_ref[...], preferred_element_type=f32)
        ys_ref[...] = (y * w_row).astype(bf16)

    @pl.when(i >= n_used)
    def _():
        ys_ref[...] = jnp.zeros(ys_ref.shape, bf16)


def _experts(tile_idx, tile_exp, first, next_exp, slot, n_used, xs, wg, wu, wd, rows_per_tile):
    r = rows_per_tile
    n_tiles = xs.shape[0] // r
    grid_spec = pltpu.PrefetchScalarGridSpec(
        num_scalar_prefetch=6,
        grid=(n_tiles,),
        in_specs=[pl.BlockSpec(memory_space=pl.ANY),
                  pl.BlockSpec(memory_space=pl.ANY),
                  pl.BlockSpec(memory_space=pl.ANY),
                  pl.BlockSpec(memory_space=pl.ANY)],
        out_specs=pl.BlockSpec((r, D_MODEL), lambda i, *_: (i, 0)),
        scratch_shapes=[pltpu.VMEM((XS_RING, r, XS_COLS), bf16),
                        pltpu.VMEM((2, D_MODEL, D_FF_E), f32),
                        pltpu.VMEM((2, D_MODEL, D_FF_E), f32),
                        pltpu.VMEM((2, D_FF_E, D_MODEL), f32),
                        pltpu.VMEM((D_MODEL, D_FF_E), bf16),
                        pltpu.VMEM((D_MODEL, D_FF_E), bf16),
                        pltpu.VMEM((D_FF_E, D_MODEL), bf16),
                        pltpu.SemaphoreType.DMA((XS_RING,)),
                        pltpu.SemaphoreType.DMA((2,))])
    return pl.pallas_call(
        _experts_body,
        grid_spec=grid_spec,
        out_shape=jax.ShapeDtypeStruct((n_tiles * r, D_MODEL), bf16),
        compiler_params=pltpu.CompilerParams(
            dimension_semantics=("arbitrary",), vmem_limit_bytes=VMEM_LIMIT),
        name="experts",
    )(tile_idx, tile_exp, first, next_exp, slot, n_used, xs, wg, wu, wd)


def _combine_out_body(runs_cur_ref, runs_nxt_ref, h_ref, slot_ref, p_ref, ys_ref,
                      wpg_ref, bpg_ref, wpp_ref, gple_ref, gfin_ref,
                      o_ref, ybuf_ref, moe_ref, sem_ref):
    tk = h_ref.shape[0]
    lb = ybuf_ref.shape[1]
    step = pl.program_id(0)
    n = pl.num_programs(0) - 1
    cur = jnp.minimum(step, n - 1) % 2

    def piece(s, local_row, global_row, rows):
        return pltpu.make_async_copy(ys_ref.at[pl.ds(global_row, rows), :],
                                     ybuf_ref.at[s, pl.ds(local_row, rows), :], sem_ref.at[s])

    @pl.when(step == 0)
    def _():
        ybuf_ref[...] = jnp.zeros(ybuf_ref.shape, bf16)
        moe_ref[1] = jnp.zeros(moe_ref.shape[1:], f32)
        _for_each_run_piece(runs_cur_ref, lambda l, g, r: piece(0, l, g, r).start())

    @pl.when(step + 1 < n)
    def _():
        _for_each_run_piece(runs_nxt_ref, lambda l, g, r: piece(1 - cur, l, g, r).start())

    run_rows = _tile_run_rows(runs_cur_ref)

    @pl.when(step < n)
    def _():
        _wait_rows(run_rows, lambda r: piece(cur, 0, 0, r))

    def work(rows):
        h2 = h_ref[...] + moe_ref[1]
        gate = jax.nn.sigmoid(jnp.dot(h2.astype(bf16), wpg_ref[...], preferred_element_type=f32)
                              + bpg_ref[...])
        pp = jnp.dot(p_ref[...].astype(bf16), wpp_ref[...], preferred_element_type=f32)
        h3 = h2 + gate * _rms(pp, gple_ref[...])
        o_ref[...] = _rms(h3, gfin_ref[...])
        col = lax.broadcasted_iota(i32, (tk, rows), 1)
        unperm = jnp.where((col == slot_ref[:, 0:1]) | (col == slot_ref[:, 1:2]), 1.0, 0.0)
        moe_ref[0] = jnp.dot(unperm.astype(bf16), ybuf_ref[cur, :rows, :],
                             preferred_element_type=f32)
        moe_ref[1] = moe_ref[0]

    short = lb - 256
    pl.when(run_rows <= short)(functools.partial(work, short))
    pl.when(run_rows > short)(functools.partial(work, lb))


def _combine_out(run_tiles, h2d, slot_t, p2d, ys, wpg, bpg, wpp, gple, gfin):
    t = h2d.shape[0]
    tk = TL_SORT
    nt = t // tk
    const = lambda shape: pl.BlockSpec(shape, lambda i: (0,) * len(shape))
    runs = lambda imap: pl.BlockSpec((1, 1, 3 * N_EXPERTS), imap, memory_space=pltpu.SMEM)
    uns = lambda s: jnp.minimum(s, nt - 1)
    fin = lambda s: jnp.maximum(s - 1, 0)
    return pl.pallas_call(
        _combine_out_body,
        grid=(nt + 1,),
        out_shape=jax.ShapeDtypeStruct((t, D_MODEL), f32),
        in_specs=[runs(lambda s: (uns(s), 0, 0)),
                  runs(lambda s: (uns(s + 1), 0, 0)),
                  pl.BlockSpec((tk, D_MODEL), lambda s: (fin(s), 0)),
                  pl.BlockSpec((tk, 2), lambda s: (uns(s), 0)),
                  pl.BlockSpec((tk, PLE_DIM), lambda s: (fin(s), 0)),
                  pl.BlockSpec(memory_space=pl.ANY),
                  const((D_MODEL, D_MODEL)), const((1, D_MODEL)),
                  const((PLE_DIM, D_MODEL)), const((1, D_MODEL)), const((1, D_MODEL))],
        out_specs=pl.BlockSpec((tk, D_MODEL), lambda s: (fin(s), 0)),
        scratch_shapes=[pltpu.VMEM((2, LB_SORT, D_MODEL), bf16),
                        pltpu.VMEM((2, tk, D_MODEL), f32),
                        pltpu.SemaphoreType.DMA((2,))],
        compiler_params=pltpu.CompilerParams(
            dimension_semantics=("arbitrary",), vmem_limit_bytes=VMEM_LIMIT),
        name="combine_out",
    )(run_tiles, run_tiles, h2d, slot_t, p2d, ys, wpg, bpg, wpp, gple, gfin)


def _q_perm():
    perm = np.empty((D_B,), np.int32)
    for g in range(GQA):
        for kh in range(N_KV):
            for d in range(HEAD_DIM):
                perm[g * BLK + kh * HEAD_DIM + d] = (kh * GQA + g) * HEAD_DIM + d
    return perm


def kernel(x, p, rel_bias, g_mix, w_in, ln_v_g, ln_v_b, w_spatial, b_spatial, sink, g_out_grp, w_out,
           g_ffn, w_router_group, b_router_group, w_router_expert, b_router_expert, w_gate_e, w_up_e,
           w_down_e, w_ple_proj, g_ple, w_ple_gate, b_ple_gate, g_final):
    b, s, d = x.shape
    t = b * s
    depth = g_mix.shape[0]
    assert depth == 1 and d == D_MODEL
    perm = _q_perm()
    c1, c2 = 2 * D_A, 2 * D_A + D_B
    bias = _bias_table(rel_bias * LOG2E)
    hcur = x.astype(f32)
    for li in range(depth):
        wi = w_in[li]
        win = jnp.concatenate([wi[:, :c1], wi[:, c1:c2][:, perm] * (LOG2E * HEAD_DIM ** -0.5), wi[:, c2:]],
                              axis=1).astype(bf16)
        bs = jnp.broadcast_to(b_spatial[li][:, :, None], (A_HEADS, BLK, BLK)).astype(f32)
        gout = g_out_grp[li]
        wo = w_out[li]
        wout = jnp.concatenate([wo[:D_A], wo[D_A:][perm]], axis=0).astype(bf16)
        wr = jnp.zeros((D_MODEL, BLK), f32)
        wr = wr.at[:, 0:N_GROUPS].set(w_router_group[li]).at[:, 8:ROUTE_ROWS].set(w_router_expert[li])
        wr = wr.astype(bf16)
        br = jnp.full((ROUTE_ROWS,), NEG, f32)
        br = br.at[0:N_GROUPS].set(b_router_group[li]).at[8:].set(b_router_expert[li])
        br = jnp.broadcast_to(br[:, None], (ROUTE_ROWS, TQ_ATT))

        ya, q, k, v = _mix_in(hcur.reshape(t, d), g_mix[li][None], win, ln_v_g[li][None],
                              ln_v_b[li][None], w_spatial[li].astype(bf16), bs, gout[None, :D_A])
        h, m_rows, route_e, route_w = _attn_out(
            sink[li].astype(f32) * LOG2E, hcur, ya.reshape(b, s, D_A), q.reshape(b, s, D_B),
            k.reshape(b, s, BLK), v.reshape(b, s, BLK), bias, gout[None, D_A:][:, perm], wout,
            g_ffn[li][None], wr, br)

        r = R_EXP
        n_sort = t // TL_SORT
        max_rows = 2 * t + n_sort * N_EXPERTS * (RUN_ALIGN - 1)
        n_tiles = -(-max_rows // r) + N_EXPERTS
        slot, runs, offs = _route_pos(route_e, r)
        starts = offs[0, :, 0]
        padded = offs[1, :, 0]
        ends = starts + padded
        n_used = ends[-1] // r
        tile_idx = jnp.minimum(jnp.arange(n_tiles, dtype=i32), n_used - 1)
        tile_exp = jnp.minimum(
            jnp.sum((ends[None, :] <= (tile_idx * r)[:, None]).astype(i32), axis=1), N_EXPERTS - 1)
        tile_exp = tile_exp.astype(i32)
        first = jnp.concatenate([jnp.ones((1,), i32),
                                 (tile_exp[1:] != tile_exp[:-1]).astype(i32)])
        next_tile = ends[tile_exp] // r
        next_exp = jnp.where(next_tile < n_used, tile_exp[jnp.minimum(next_tile, n_tiles - 1)], -1)
        fetch_slot = (jnp.cumsum(first) - 1) % 2
        run_tiles = runs[:, :, :, 0].reshape(n_sort, 1, 3 * N_EXPERTS)
        xs = _sort_rows(ends, padded, run_tiles, m_rows, slot, route_w, n_tiles * r, r)
        ys = _experts(tile_idx, tile_exp, first, next_exp.astype(i32), fetch_slot.astype(i32),
                      n_used.reshape(1), xs, w_gate_e[li], w_up_e[li], w_down_e[li], r)

        out = _combine_out(run_tiles, h.reshape(t, d), slot[:2].T,
                           p[li].reshape(t, PLE_DIM), ys, w_ple_gate[li].astype(bf16),
                           b_ple_gate[li][None], w_ple_proj[li].astype(bf16), g_ple[li][None],
                           g_final[None])
        hcur = out.reshape(b, s, d)
    return hcur
```

```python
import functools
import math

import jax
import jax.numpy as jnp
import numpy as np
from jax import lax
from jax.experimental import pallas as pl
from jax.experimental.pallas import tpu as pltpu

D_MODEL = 1024
D_A = 512
D_B = 512
BLK = 128
A_HEADS = 4
HEAD_DIM = 64
N_HEADS = 8
N_KV = 2
GQA = 4
WINDOW = 128
NUM_BUCKETS = 32
MAX_DIST = 128
D_IN = 2 * D_A + D_B + 2 * N_KV * HEAD_DIM
N_GROUPS = 4
E_PER_GROUP = 8
N_EXPERTS = 32
D_FF_E = 256
PLE_DIM = 256
EPS = 1e-6
NEG = -1e30
LOG2E = math.log2(math.e)

TM_MIX = 512
TQ_ATT = 512
TL_SORT = 512
RUN_ALIGN = 16
SMALL_BITS = 2
BIG_PIECE = RUN_ALIGN << SMALL_BITS
LB_SORT = 2 * TL_SORT + N_EXPERTS * RUN_ALIGN
R_EXP = 512
XS_COLS = D_MODEL + BLK
XS_RING = 3
ROUTE_ROWS = 8 + N_EXPERTS
VMEM_LIMIT = 48 * 1024 * 1024

f32 = jnp.float32
bf16 = jnp.bfloat16
i32 = jnp.int32


def _rms(x, g):
    return x * lax.rsqrt(jnp.mean(x * x, axis=-1, keepdims=True) + EPS) * g


def _gelu_tanh(x):
    c = math.sqrt(2.0 / math.pi)
    return x * (0.5 * (1.0 + jnp.tanh(c * (x + 0.044715 * (x * x * x)))))


def _bucket_table():
    n = NUM_BUCKETS // 2
    max_exact = n // 2
    i = np.arange(BLK)[:, None]
    j = np.arange(3 * BLK)[None, :]
    rel = j - BLK - i
    ret = np.where(rel > 0, n, 0)
    a = np.abs(rel)
    large = max_exact + (np.log(np.maximum(a, 1).astype(np.float64) / max_exact)
                         / math.log(MAX_DIST / max_exact) * (n - max_exact)).astype(np.int32)
    large = np.minimum(large, n - 1)
    return (ret + np.where(a < max_exact, a, large)).astype(np.int32)


def _bias_body(rb_ref, bucket_ref, o_ref):
    rows = 16
    for r0 in range(0, BLK, rows):
        bucket = bucket_ref[r0:r0 + rows, :]
        acc = [jnp.zeros(bucket.shape, f32) for _ in range(N_HEADS)]
        for b in range(NUM_BUCKETS):
            hit = bucket == b
            acc = [jnp.where(hit, rb_ref[b, h], acc[h]) for h in range(N_HEADS)]
        for h in range(N_HEADS):
            o_ref[h, r0:r0 + rows, :] = acc[h]


def _bias_table(rel_bias):
    bucket = jnp.asarray(_bucket_table())
    return pl.pallas_call(
        _bias_body,
        out_shape=jax.ShapeDtypeStruct((N_HEADS, BLK, 3 * BLK), f32),
        in_specs=[pl.BlockSpec(memory_space=pltpu.SMEM),
                  pl.BlockSpec((BLK, 3 * BLK), lambda: (0, 0))],
        out_specs=pl.BlockSpec((N_HEADS, BLK, 3 * BLK), lambda: (0, 0, 0)),
        name="bias_table",
    )(rel_bias.astype(f32), bucket)


def _mix_in_body(x_ref, gmix_ref, win_ref, lng_ref, lnb_ref, ws_ref, bs_ref, gout_ref,
                 ya_ref, q_ref, k_ref, v_ref):
    tm = x_ref.shape[0]
    nc = tm // BLK
    a = _rms(x_ref[...], gmix_ref[...])
    z = jnp.dot(a.astype(bf16), win_ref[...], preferred_element_type=f32)
    uv = _gelu_tanh(z[:, :2 * D_A])
    u = uv[:, :D_A]
    v = uv[:, D_A:]
    mu = jnp.mean(v, axis=-1, keepdims=True)
    vc = v - mu
    var = jnp.mean(vc * vc, axis=-1, keepdims=True)
    vn = (vc * lax.rsqrt(var + EPS) * lng_ref[...] + lnb_ref[...]).astype(bf16)
    cols = []
    for h in range(A_HEADS):
        rhs = jnp.concatenate(
            [vn[c * BLK:(c + 1) * BLK, h * BLK:(h + 1) * BLK] for c in range(nc)], axis=1)
        r = jnp.dot(ws_ref[h], rhs, preferred_element_type=f32)
        cols.append(jnp.concatenate(
            [r[:, c * BLK:(c + 1) * BLK] + bs_ref[h] for c in range(nc)], axis=0))
    sv = jnp.concatenate(cols, axis=1)
    ya_ref[...] = _rms(u * sv, gout_ref[...]).astype(bf16)
    q_ref[...] = z[:, 2 * D_A:2 * D_A + D_B].astype(bf16)
    k_ref[...] = z[:, 2 * D_A + D_B:2 * D_A + D_B + BLK].astype(bf16)
    v_ref[...] = z[:, 2 * D_A + D_B + BLK:].astype(bf16)


def _mix_in(x2, gmix, win, lng, lnb, ws, bs, gout_a):
    t = x2.shape[0]
    tm = TM_MIX
    const = lambda shape: pl.BlockSpec(shape, lambda i: (0,) * len(shape))
    return pl.pallas_call(
        _mix_in_body,
        grid=(t // tm,),
        out_shape=(jax.ShapeDtypeStruct((t, D_A), bf16),
                   jax.ShapeDtypeStruct((t, D_B), bf16),
                   jax.ShapeDtypeStruct((t, BLK), bf16),
                   jax.ShapeDtypeStruct((t, BLK), bf16)),
        in_specs=[pl.BlockSpec((tm, D_MODEL), lambda i: (i, 0)),
                  const((1, D_MODEL)), const((D_MODEL, D_IN)),
                  const((1, D_A)), const((1, D_A)),
                  const((A_HEADS, BLK, BLK)), const((A_HEADS, BLK, BLK)),
                  const((1, D_A))],
        out_specs=(pl.BlockSpec((tm, D_A), lambda i: (i, 0)),
                   pl.BlockSpec((tm, D_B), lambda i: (i, 0)),
                   pl.BlockSpec((tm, BLK), lambda i: (i, 0)),
                   pl.BlockSpec((tm, BLK), lambda i: (i, 0))),
        compiler_params=pltpu.CompilerParams(
            dimension_semantics=("parallel",), vmem_limit_bytes=VMEM_LIMIT),
        name="mix_in",
    )(x2, gmix, win, lng, lnb, ws, bs, gout_a)


def _attn_out_body(sink_ref, x_ref, ya_ref, q_ref, kp_ref, km_ref, kn_ref, vp_ref, vm_ref, vn_ref,
                   bias_ref, goutb_ref, wout_ref, gffn_ref, wr_ref, br_ref,
                   h_ref, m_ref, re_ref, rw_ref,
                   kf_ref, vf_ref, e_ref, yb_ref, *, seq_len):
    tq = x_ref.shape[1]
    nb = tq // BLK
    step = pl.program_id(0)
    n_tok_tiles = pl.num_programs(0) - 1
    ti = jnp.minimum(step, n_tok_tiles - 1) % (seq_len // tq)

    @pl.when(step == 0)
    def _():
        yb_ref[1] = jnp.zeros(yb_ref.shape[1:], f32)

    ybn = _rms(yb_ref[1], goutb_ref[...]).astype(bf16)
    y = jnp.concatenate([ya_ref[0], ybn], axis=1)
    h = x_ref[0] + jnp.dot(y, wout_ref[...], preferred_element_type=f32)
    h_ref[0] = h
    m = _rms(h, gffn_ref[...])
    m_ref[...] = m.astype(bf16)

    logit_t = jnp.dot(m.astype(bf16), wr_ref[...], preferred_element_type=f32)
    logit = jnp.transpose(logit_t)[:ROUTE_ROWS] + br_ref[...]
    sub = lax.broadcasted_iota(i32, (8, tq), 0)
    lg = logit[0:8]
    mg = jnp.max(lg, axis=0, keepdims=True)
    pg_top = 1.0 / jnp.sum(jnp.exp(lg - mg), axis=0, keepdims=True)
    g_idx = jnp.min(jnp.where(lg == mg, sub, 8), axis=0, keepdims=True)
    sel = logit[8:16]
    for g in range(1, N_GROUPS):
        sel = jnp.where(g_idx == g, logit[8 + 8 * g:16 + 8 * g], sel)
    m1 = jnp.max(sel, axis=0, keepdims=True)
    i1 = jnp.min(jnp.where(sel == m1, sub, 8), axis=0, keepdims=True)
    sel2 = jnp.where(sub == i1, -jnp.inf, sel)
    m2 = jnp.max(sel2, axis=0, keepdims=True)
    i2 = jnp.min(jnp.where(sel2 == m2, sub, 8), axis=0, keepdims=True)
    r = jnp.exp(m2 - m1)
    w1 = pg_top / (1.0 + r)
    w2 = pg_top * r / (1.0 + r)
    e1 = g_idx * E_PER_GROUP + i1
    e2 = g_idx * E_PER_GROUP + i2
    re_ref[...] = jnp.where(sub == 0, e1, jnp.where(sub == 1, e2, 0))
    rw_ref[...] = jnp.where(sub == 0, w1, jnp.where(sub == 1, w2, 0.0))

    kf_ref[0:BLK] = kp_ref[0]
    kf_ref[BLK:BLK + tq] = km_ref[0]
    kf_ref[BLK + tq:] = kn_ref[0]
    vlane = lax.broadcasted_iota(i32, (BLK, BLK), 1)
    for src, r0, r1 in ((vp_ref, 0, BLK), (vn_ref, BLK + tq, tq + 2 * BLK)) + tuple(
            (vm_ref.at[:, pl.ds(c * BLK, BLK), :], BLK + c * BLK, BLK + (c + 1) * BLK)
            for c in range(nb)):
        vals = src[0].astype(f32)
        vf_ref[0, r0:r1] = jnp.where(vlane < HEAD_DIM, vals,
                                     jnp.where(vlane == HEAD_DIM, 1.0, 0.0)).astype(bf16)
        vf_ref[1, r0:r1] = jnp.where(vlane >= HEAD_DIM, vals,
                                     jnp.where(vlane == 0, 1.0, 0.0)).astype(bf16)

    row = lax.broadcasted_iota(i32, (BLK, 3 * BLK), 0)
    col = lax.broadcasted_iota(i32, (BLK, 3 * BLK), 1)
    band = jnp.abs(col - BLK - row) <= WINDOW
    lane = lax.broadcasted_iota(i32, (BLK, BLK), 1)
    low_half = lane < HEAD_DIM

    for n in range(nb):
        r0 = n * BLK
        qb = q_ref[0, pl.ds(r0, BLK), :]
        kb = kf_ref[pl.ds(r0, 3 * BLK), :]
        zero = jnp.zeros((BLK, BLK), bf16)
        lhs = []
        for kh in range(N_KV):
            for g in range(GQA):
                tile = qb[:, g * BLK:(g + 1) * BLK]
                lhs.append(jnp.where(low_half, tile, zero) if kh == 0
                           else jnp.where(low_half, zero, tile))
        lhs = jnp.concatenate(lhs, axis=0)
        s_all = lax.dot_general(lhs, kb, (((1,), (1,)), ((), ())),
                                preferred_element_type=f32)
        kpos = col + ((ti * nb + n - 1) * BLK)
        valid = band & (kpos >= 0) & (kpos < seq_len)
        sink_e = []
        for h in range(N_HEADS):
            s = jnp.where(valid, s_all[h * BLK:(h + 1) * BLK] + bias_ref[h], NEG)
            sk = sink_ref[h]
            mrow = jnp.maximum(jnp.max(s, axis=-1, keepdims=True), sk)
            e_ref[n, h * BLK:(h + 1) * BLK, :] = jnp.exp2(s - mrow).astype(bf16)
            sink_e.append(jnp.exp2(sk - mrow))
        half = GQA * BLK
        pv = [jnp.dot(e_ref[n, kh * half:(kh + 1) * half, :], vf_ref[kh, pl.ds(r0, 3 * BLK), :],
                      preferred_element_type=f32) for kh in range(N_KV)]

        def head_out(h):
            kh, g = divmod(h, GQA)
            rows = pv[kh][g * BLK:(g + 1) * BLK]
            ones_col = HEAD_DIM if kh == 0 else 0
            return rows * (1.0 / (rows[:, ones_col:ones_col + 1] + sink_e[h]))

        for g in range(GQA):
            yb_ref[0, pl.ds(r0, BLK), g * BLK:(g + 1) * BLK] = jnp.where(
                low_half, head_out(g), head_out(GQA + g))

    yb_ref[1] = yb_ref[0]


def _attn_out(sink, x, ya, q, k, v, bias, goutb, wout, gffn, wr, br):
    b, s, _ = x.shape
    tq = TQ_ATT
    nb = tq // BLK
    nblk = s // BLK
    t = b * s
    nq = s // tq
    n_tok_tiles = b * nq
    att = lambda s: jnp.minimum(s, n_tok_tiles - 1)
    epi = lambda s: jnp.maximum(s - 1, 0)
    const = lambda shape: pl.BlockSpec(shape, lambda s, *_: (0,) * len(shape))
    tok_att = lambda w: pl.BlockSpec((1, tq, w), lambda s, *_: (att(s) // nq, att(s) % nq, 0))
    tok_epi = lambda w: pl.BlockSpec((1, tq, w), lambda s, *_: (epi(s) // nq, epi(s) % nq, 0))
    prev = pl.BlockSpec((1, BLK, BLK), lambda s, *_: (
        att(s) // nq, jnp.maximum((att(s) % nq) * nb - 1, 0), 0))
    nxt = pl.BlockSpec((1, BLK, BLK), lambda s, *_: (
        att(s) // nq, jnp.minimum((att(s) % nq) * nb + nb, nblk - 1), 0))
    rows = pl.BlockSpec((tq, D_MODEL), lambda s, *_: (epi(s), 0))
    lanes = lambda rows: pl.BlockSpec((rows, tq), lambda s, *_: (0, epi(s)))
    grid_spec = pltpu.PrefetchScalarGridSpec(
        num_scalar_prefetch=1,
        grid=(n_tok_tiles + 1,),
        in_specs=[tok_epi(D_MODEL), tok_epi(D_A), tok_att(D_B),
                  prev, tok_att(BLK), nxt, prev, tok_att(BLK), nxt,
                  const((N_HEADS, BLK, 3 * BLK)), const((1, D_B)), const((D_MODEL, D_MODEL)),
                  const((1, D_MODEL)), const((D_MODEL, BLK)), const((ROUTE_ROWS, tq))],
        out_specs=(tok_epi(D_MODEL), rows, lanes(8), lanes(8)),
        scratch_shapes=[pltpu.VMEM((tq + 2 * BLK, BLK), bf16),
                        pltpu.VMEM((N_KV, tq + 2 * BLK, BLK), bf16),
                        pltpu.VMEM((nb, N_HEADS * BLK, 3 * BLK), bf16),
                        pltpu.VMEM((2, tq, D_B), f32)])
    return pl.pallas_call(
        functools.partial(_attn_out_body, seq_len=s),
        grid_spec=grid_spec,
        out_shape=(jax.ShapeDtypeStruct((b, s, D_MODEL), f32),
                   jax.ShapeDtypeStruct((t, D_MODEL), bf16),
                   jax.ShapeDtypeStruct((8, t), i32),
                   jax.ShapeDtypeStruct((8, t), f32)),
        compiler_params=pltpu.CompilerParams(
            dimension_semantics=("arbitrary",), vmem_limit_bytes=VMEM_LIMIT),
        name="attn_out",
    )(sink, x, ya, q, k, k, k, v, v, v, bias, goutb, wout, gffn, wr, br)


def _exclusive_prefix(vals):
    a = lax.broadcasted_iota(i32, (N_EXPERTS, N_EXPERTS), 0)
    c = lax.broadcasted_iota(i32, (N_EXPERTS, N_EXPERTS), 1)
    low = jnp.where(c < a, 1.0, 0.0).astype(bf16)
    hi = (vals >> 8).astype(f32).astype(bf16)
    lo = (vals & 255).astype(f32).astype(bf16)
    return (jnp.dot(low, hi, preferred_element_type=f32) * 256.0
            + jnp.dot(low, lo, preferred_element_type=f32)).astype(i32)


def _round_up_pow2(vals, mult):
    log_m = mult.bit_length() - 1
    return ((vals + (mult - 1)) >> log_m) << log_m


def _route_pos_body(re_ref, slot_ref, runs_ref, offs_ref, tri_ref, *, rows_per_tile, tl):
    n_tok_tiles = re_ref.shape[1] // tl
    a = lax.broadcasted_iota(i32, (tl, tl), 0)
    c = lax.broadcasted_iota(i32, (tl, tl), 1)
    tri_ref[...] = jnp.where(a < c, 1.0, 0.0).astype(bf16)
    eid = lax.broadcasted_iota(i32, (N_EXPERTS, tl), 0)

    def tile_hits(j):
        cols = pl.ds(pl.multiple_of(j * tl, tl), tl)
        hit1 = eid == re_ref[0:1, cols]
        hit2 = eid == re_ref[1:2, cols]
        onehot = jnp.where(hit1 | hit2, 1.0, 0.0)
        cnt = jnp.sum(onehot, axis=1, keepdims=True).astype(i32)
        run_len = jnp.broadcast_to(_round_up_pow2(cnt, RUN_ALIGN), (N_EXPERTS, BLK))
        return cols, hit1, hit2, onehot, run_len

    total = lax.fori_loop(0, n_tok_tiles, lambda j, acc: acc + tile_hits(j)[4],
                          jnp.zeros((N_EXPERTS, BLK), i32))
    padded = _round_up_pow2(total, rows_per_tile)
    starts = _exclusive_prefix(padded)
    offs_ref[0] = starts
    offs_ref[1] = padded

    def place(j, global_off):
        cols, hit1, hit2, onehot, run_len = tile_hits(j)
        local_off = _exclusive_prefix(run_len)
        before = jnp.dot(onehot.astype(bf16), tri_ref[...], preferred_element_type=f32)
        slot = before + local_off[:, 0:1].astype(f32)
        s1 = jnp.sum(jnp.where(hit1, slot, 0.0), axis=0, keepdims=True)
        s2 = jnp.sum(jnp.where(hit2, slot, 0.0), axis=0, keepdims=True)
        sub = lax.broadcasted_iota(i32, (8, tl), 0)
        slot_ref[:, cols] = jnp.where(sub == 0, s1.astype(i32),
                                      jnp.where(sub == 1, s2.astype(i32), 0))
        runs_ref[j, 0] = local_off
        runs_ref[j, 1] = run_len
        runs_ref[j, 2] = global_off
        return global_off + run_len

    lax.fori_loop(0, n_tok_tiles, place, starts)


def _route_pos(route_e, rows_per_tile):
    t = route_e.shape[1]
    tl = TL_SORT
    return pl.pallas_call(
        functools.partial(_route_pos_body, rows_per_tile=rows_per_tile, tl=tl),
        out_shape=(jax.ShapeDtypeStruct((8, t), i32),
                   jax.ShapeDtypeStruct((t // tl, 3, N_EXPERTS, BLK), i32),
                   jax.ShapeDtypeStruct((2, N_EXPERTS, BLK), i32)),
        scratch_shapes=[pltpu.VMEM((tl, tl), bf16)],
        compiler_params=pltpu.CompilerParams(vmem_limit_bytes=VMEM_LIMIT),
        name="route_pos",
    )(route_e)


def _for_each_run_piece(runs_ref, fn):
    def per_expert(e, carry):
        local_off = runs_ref[0, 0, e]
        length = runs_ref[0, 0, N_EXPERTS + e]
        global_off = runs_ref[0, 0, 2 * N_EXPERTS + e]
        units = length >> (RUN_ALIGN.bit_length() - 1)

        def big_piece(k, inner):
            done = k * BIG_PIECE
            fn(pl.multiple_of(local_off + done, RUN_ALIGN),
               pl.multiple_of(global_off + done, RUN_ALIGN), BIG_PIECE)
            return inner

        lax.fori_loop(0, units >> SMALL_BITS, big_piece, 0)
        for b in range(SMALL_BITS):
            @pl.when(((units >> b) & 1) == 1)
            def _():
                done = ((units >> (b + 1)) << (b + 1)) * RUN_ALIGN
                fn(pl.multiple_of(local_off + done, RUN_ALIGN),
                   pl.multiple_of(global_off + done, RUN_ALIGN), RUN_ALIGN << b)
        return carry

    lax.fori_loop(0, N_EXPERTS, per_expert, 0)


def _tile_run_rows(runs_ref):
    return lax.fori_loop(0, N_EXPERTS, lambda e, acc: acc + runs_ref[0, 0, N_EXPERTS + e], 0)


def _wait_rows(total_rows, make_copy):
    units = total_rows >> (RUN_ALIGN.bit_length() - 1)
    for b in range((LB_SORT // RUN_ALIGN).bit_length()):
        @pl.when(((units >> b) & 1) == 1)
        def _():
            make_copy(RUN_ALIGN << b).wait()


def _sort_rows_body(ends_ref, padded_ref, runs_ref, m_ref, slot_ref, rw_ref, xs_ref,
                    local_ref, zero_ref, pending_ref, sem_ref, zsem_ref, *, rows_per_tile):
    tl = m_ref.shape[0]
    lb = local_ref.shape[1]
    step = pl.program_id(0)
    cur = step % 2
    n_tiles = xs_ref.shape[0] // rows_per_tile
    n_used = ends_ref[N_EXPERTS - 1] // rows_per_tile

    def zero_tile(tile):
        start = pl.multiple_of(tile * rows_per_tile, rows_per_tile)
        return pltpu.make_async_copy(zero_ref, xs_ref.at[pl.ds(start, rows_per_tile), :], zsem_ref)

    @pl.when(pl.program_id(0) == 0)
    def _():
        zero_ref[...] = jnp.zeros(zero_ref.shape, bf16)
        for wait in (False, True):
            for e in range(N_EXPERTS):
                @pl.when(padded_ref[e] > 0)
                def _():
                    cp = zero_tile(ends_ref[e] // rows_per_tile - 1)
                    cp.wait() if wait else cp.start()

            def spare(j, carry):
                cp = zero_tile(n_used + j)
                cp.wait() if wait else cp.start()
                return carry

            lax.fori_loop(0, n_tiles - n_used, spare, 0)

    s1 = slot_ref[0:1, :]
    s2 = slot_ref[1:2, :]
    w1 = rw_ref[0:1, :]
    w2 = rw_ref[1:2, :]
    run_rows = _tile_run_rows(runs_ref)

    def sort_rows_upto(n):
        srow = lax.broadcasted_iota(i32, (n, tl), 0)
        p1 = srow == s1
        p2 = srow == s2
        perm = jnp.where(p1 | p2, 1.0, 0.0).astype(bf16)
        rows = jnp.dot(perm, m_ref[...], preferred_element_type=f32)
        w = jnp.sum(jnp.where(p1, w1, 0.0) + jnp.where(p2, w2, 0.0), axis=1, keepdims=True)
        w_hi = w.astype(bf16).astype(f32)
        w_lo = w - w_hi
        lane = lax.broadcasted_iota(i32, (n, BLK), 1)
        local_ref[cur, :n, :D_MODEL] = rows.astype(bf16)
        local_ref[cur, :n, D_MODEL:] = jnp.where(
            lane == 0, w_hi, jnp.where(lane == 1, w_lo, 0.0)).astype(bf16)

    def piece(s, local_row, global_row, rows):
        return pltpu.make_async_copy(local_ref.at[s, pl.ds(local_row, rows), :],
                                     xs_ref.at[pl.ds(global_row, rows), :], sem_ref.at[s])

    short = lb - 256
    pl.when(run_rows <= short)(functools.partial(sort_rows_upto, short))
    pl.when(run_rows > short)(functools.partial(sort_rows_upto, lb))

    _for_each_run_piece(runs_ref, lambda l, g, n: piece(cur, l, g, n).start())

    @pl.when(step > 0)
    def _():
        _wait_rows(pending_ref[0], lambda n: piece(1 - cur, 0, 0, n))

    pending_ref[0] = run_rows

    @pl.when(step == pl.num_programs(0) - 1)
    def _():
        _wait_rows(pending_ref[0], lambda n: piece(cur, 0, 0, n))


def _sort_rows(ends, padded, run_tiles, m, slot, route_w, total_rows, rows_per_tile):
    t = m.shape[0]
    tl = TL_SORT
    grid_spec = pltpu.PrefetchScalarGridSpec(
        num_scalar_prefetch=2,
        grid=(t // tl,),
        in_specs=[pl.BlockSpec((1, 1, 3 * N_EXPERTS), lambda i, *_: (i, 0, 0),
                               memory_space=pltpu.SMEM),
                  pl.BlockSpec((tl, D_MODEL), lambda i, *_: (i, 0)),
                  pl.BlockSpec((8, tl), lambda i, *_: (0, i)),
                  pl.BlockSpec((8, tl), lambda i, *_: (0, i))],
        out_specs=pl.BlockSpec(memory_space=pl.ANY),
        scratch_shapes=[pltpu.VMEM((2, LB_SORT, XS_COLS), bf16),
                        pltpu.VMEM((rows_per_tile, XS_COLS), bf16),
                        pltpu.SMEM((1,), i32),
                        pltpu.SemaphoreType.DMA((2,)),
                        pltpu.SemaphoreType.DMA(())])
    return pl.pallas_call(
        functools.partial(_sort_rows_body, rows_per_tile=rows_per_tile),
        grid_spec=grid_spec,
        out_shape=jax.ShapeDtypeStruct((total_rows, XS_COLS), bf16),
        compiler_params=pltpu.CompilerParams(
            dimension_semantics=("arbitrary",), vmem_limit_bytes=VMEM_LIMIT),
        name="sort_rows",
    )(ends, padded, run_tiles, m, slot, route_w)


def _experts_body(tile_ref, texp_ref, first_ref, next_ref, slot_ref, nused_ref,
                  xs_hbm, wg_hbm, wu_hbm, wd_hbm, ys_ref,
                  xbuf_ref, sg_ref, su_ref, sd_ref, wg_ref, wu_ref, wd_ref, xsem_ref, sem_ref):
    i = pl.program_id(0)
    n_used = nused_ref[0]
    r = xbuf_ref.shape[1]

    def row_tile(tile):
        s = tile % XS_RING
        start = pl.multiple_of(tile * r, r)
        return pltpu.make_async_copy(xs_hbm.at[pl.ds(start, r), :], xbuf_ref.at[s], xsem_ref.at[s])

    @pl.when(i == 0)
    def _():
        for ahead in range(XS_RING - 1):
            @pl.when(ahead < n_used)
            def _():
                row_tile(ahead).start()

    @pl.when(i + (XS_RING - 1) < n_used)
    def _():
        row_tile(i + (XS_RING - 1)).start()

    def fetches(expert, s):
        return (pltpu.make_async_copy(wg_hbm.at[expert], sg_ref.at[s], sem_ref.at[s]),
                pltpu.make_async_copy(wu_hbm.at[expert], su_ref.at[s], sem_ref.at[s]),
                pltpu.make_async_copy(wd_hbm.at[expert], sd_ref.at[s], sem_ref.at[s]))

    @pl.when(i < n_used)
    def _():
        s = slot_ref[i]
        row_tile(i).wait()
        xs_ref = xbuf_ref.at[i % XS_RING]

        @pl.when(first_ref[i] == 1)
        def _():
            @pl.when(i == 0)
            def _():
                for cp in fetches(texp_ref[0], 0):
                    cp.start()

            for cp in fetches(texp_ref[i], s):
                cp.wait()

            @pl.when(next_ref[i] >= 0)
            def _():
                for cp in fetches(next_ref[i], 1 - s):
                    cp.start()

            wg_ref[...] = sg_ref[s].astype(bf16)
            wu_ref[...] = su_ref[s].astype(bf16)
            wd_ref[...] = sd_ref[s].astype(bf16)

        w_row = (xs_ref[:, D_MODEL:D_MODEL + 1].astype(f32)
                 + xs_ref[:, D_MODEL + 1:D_MODEL + 2].astype(f32))
        gate = jnp.dot(xs_ref[:, :D_MODEL], wg_ref[...], preferred_element_type=f32)
        up = jnp.dot(xs_ref[:, :D_MODEL], wu_ref[...], preferred_element_type=f32)
        hdn = (gate * jax.nn.sigmoid(gate)) * up
        y = jnp.dot(hdn.astype(bf16), wd_ref[...], preferred_element_type=f32)
        ys_ref[...] = (y * w_row).astype(bf16)

    @pl.when(i >= n_used)
    def _():
        ys_ref[...] = jnp.zeros(ys_ref.shape, bf16)


def _experts(tile_idx, tile_exp, first, next_exp, slot, n_used, xs, wg, wu, wd, rows_per_tile):
    r = rows_per_tile
    n_tiles = xs.shape[0] // r
    grid_spec = pltpu.PrefetchScalarGridSpec(
        num_scalar_prefetch=6,
        grid=(n_tiles,),
        in_specs=[pl.BlockSpec(memory_space=pl.ANY),
                  pl.BlockSpec(memory_space=pl.ANY),
                  pl.BlockSpec(memory_space=pl.ANY),
                  pl.BlockSpec(memory_space=pl.ANY)],
        out_specs=pl.BlockSpec((r, D_MODEL), lambda i, *_: (i, 0)),
        scratch_shapes=[pltpu.VMEM((XS_RING, r, XS_COLS), bf16),
                        pltpu.VMEM((2, D_MODEL, D_FF_E), f32),
                        pltpu.VMEM((2, D_MODEL, D_FF_E), f32),
                        pltpu.VMEM((2, D_FF_E, D_MODEL), f32),
                        pltpu.VMEM((D_MODEL, D_FF_E), bf16),
                        pltpu.VMEM((D_MODEL, D_FF_E), bf16),
                        pltpu.VMEM((D_FF_E, D_MODEL), bf16),
                        pltpu.SemaphoreType.DMA((XS_RING,)),
                        pltpu.SemaphoreType.DMA((2,))])
    return pl.pallas_call(
        _experts_body,
        grid_spec=grid_spec,
        out_shape=jax.ShapeDtypeStruct((n_tiles * r, D_MODEL), bf16),
        compiler_params=pltpu.CompilerParams(
            dimension_semantics=("arbitrary",), vmem_limit_bytes=VMEM_LIMIT),
        name="experts",
    )(tile_idx, tile_exp, first, next_exp, slot, n_used, xs, wg, wu, wd)


def _combine_out_body(runs_cur_ref, runs_nxt_ref, h_ref, slot_ref, p_ref, ys_ref,
                      wpg_ref, bpg_ref, wpp_ref, gple_ref, gfin_ref,
                      o_ref, ybuf_ref, moe_ref, sem_ref):
    tk = h_ref.shape[0]
    lb = ybuf_ref.shape[1]
    step = pl.program_id(0)
    n = pl.num_programs(0) - 1
    cur = jnp.minimum(step, n - 1) % 2

    def piece(s, local_row, global_row, rows):
        return pltpu.make_async_copy(ys_ref.at[pl.ds(global_row, rows), :],
                                     ybuf_ref.at[s, pl.ds(local_row, rows), :], sem_ref.at[s])

    @pl.when(step == 0)
    def _():
        ybuf_ref[...] = jnp.zeros(ybuf_ref.shape, bf16)
        moe_ref[1] = jnp.zeros(moe_ref.shape[1:], f32)
        _for_each_run_piece(runs_cur_ref, lambda l, g, r: piece(0, l, g, r).start())

    @pl.when(step + 1 < n)
    def _():
        _for_each_run_piece(runs_nxt_ref, lambda l, g, r: piece(1 - cur, l, g, r).start())

    run_rows = _tile_run_rows(runs_cur_ref)

    @pl.when(step < n)
    def _():
        _wait_rows(run_rows, lambda r: piece(cur, 0, 0, r))

    def work(rows):
        h2 = h_ref[...] + moe_ref[1]
        gate = jax.nn.sigmoid(jnp.dot(h2.astype(bf16), wpg_ref[...], preferred_element_type=f32)
                              + bpg_ref[...])
        pp = jnp.dot(p_ref[...].astype(bf16), wpp_ref[...], preferred_element_type=f32)
        h3 = h2 + gate * _rms(pp, gple_ref[...])
        o_ref[...] = _rms(h3, gfin_ref[...])
        col = lax.broadcasted_iota(i32, (tk, rows), 1)
        unperm = jnp.where((col == slot_ref[:, 0:1]) | (col == slot_ref[:, 1:2]), 1.0, 0.0)
        moe_ref[0] = jnp.dot(unperm.astype(bf16), ybuf_ref[cur, :rows, :],
                             preferred_element_type=f32)
        moe_ref[1] = moe_ref[0]

    short = lb - 256
    pl.when(run_rows <= short)(functools.partial(work, short))
    pl.when(run_rows > short)(functools.partial(work, lb))


def _combine_out(run_tiles, h2d, slot_t, p2d, ys, wpg, bpg, wpp, gple, gfin):
    t = h2d.shape[0]
    tk = TL_SORT
    nt = t // tk
    const = lambda shape: pl.BlockSpec(shape, lambda i: (0,) * len(shape))
    runs = lambda imap: pl.BlockSpec((1, 1, 3 * N_EXPERTS), imap, memory_space=pltpu.SMEM)
    uns = lambda s: jnp.minimum(s, nt - 1)
    fin = lambda s: jnp.maximum(s - 1, 0)
    return pl.pallas_call(
        _combine_out_body,
        grid=(nt + 1,),
        out_shape=jax.ShapeDtypeStruct((t, D_MODEL), f32),
        in_specs=[runs(lambda s: (uns(s), 0, 0)),
                  runs(lambda s: (uns(s + 1), 0, 0)),
                  pl.BlockSpec((tk, D_MODEL), lambda s: (fin(s), 0)),
                  pl.BlockSpec((tk, 2), lambda s: (uns(s), 0)),
                  pl.BlockSpec((tk, PLE_DIM), lambda s: (fin(s), 0)),
                  pl.BlockSpec(memory_space=pl.ANY),
                  const((D_MODEL, D_MODEL)), const((1, D_MODEL)),
                  const((PLE_DIM, D_MODEL)), const((1, D_MODEL)), const((1, D_MODEL))],
        out_specs=pl.BlockSpec((tk, D_MODEL), lambda s: (fin(s), 0)),
        scratch_shapes=[pltpu.VMEM((2, LB_SORT, D_MODEL), bf16),
                        pltpu.VMEM((2, tk, D_MODEL), f32),
                        pltpu.SemaphoreType.DMA((2,))],
        compiler_params=pltpu.CompilerParams(
            dimension_semantics=("arbitrary",), vmem_limit_bytes=VMEM_LIMIT),
        name="combine_out",
    )(run_tiles, run_tiles, h2d, slot_t, p2d, ys, wpg, bpg, wpp, gple, gfin)


def _q_perm():
    perm = np.empty((D_B,), np.int32)
    for g in range(GQA):
        for kh in range(N_KV):
            for d in range(HEAD_DIM):
                perm[g * BLK + kh * HEAD_DIM + d] = (kh * GQA + g) * HEAD_DIM + d
    return perm


def kernel(x, p, rel_bias, g_mix, w_in, ln_v_g, ln_v_b, w_spatial, b_spatial, sink, g_out_grp, w_out,
           g_ffn, w_router_group, b_router_group, w_router_expert, b_router_expert, w_gate_e, w_up_e,
           w_down_e, w_ple_proj, g_ple, w_ple_gate, b_ple_gate, g_final):
    b, s, d = x.shape
    t = b * s
    depth = g_mix.shape[0]
    assert depth == 1 and d == D_MODEL
    perm = _q_perm()
    c1, c2 = 2 * D_A, 2 * D_A + D_B
    bias = _bias_table(rel_bias * LOG2E)
    hcur = x.astype(f32)
    for li in range(depth):
        wi = w_in[li]
        win = jnp.concatenate([wi[:, :c1], wi[:, c1:c2][:, perm] * (LOG2E * HEAD_DIM ** -0.5), wi[:, c2:]],
                              axis=1).astype(bf16)
        bs = jnp.broadcast_to(b_spatial[li][:, :, None], (A_HEADS, BLK, BLK)).astype(f32)
        gout = g_out_grp[li]
        wo = w_out[li]
        wout = jnp.concatenate([wo[:D_A], wo[D_A:][perm]], axis=0).astype(bf16)
        wr = jnp.zeros((D_MODEL, BLK), f32)
        wr = wr.at[:, 0:N_GROUPS].set(w_router_group[li]).at[:, 8:ROUTE_ROWS].set(w_router_expert[li])
        wr = wr.astype(bf16)
        br = jnp.full((ROUTE_ROWS,), NEG, f32)
        br = br.at[0:N_GROUPS].set(b_router_group[li]).at[8:].set(b_router_expert[li])
        br = jnp.broadcast_to(br[:, None], (ROUTE_ROWS, TQ_ATT))

        ya, q, k, v = _mix_in(hcur.reshape(t, d), g_mix[li][None], win, ln_v_g[li][None],
                              ln_v_b[li][None], w_spatial[li].astype(bf16), bs, gout[None, :D_A])
        h, m_rows, route_e, route_w = _attn_out(
            sink[li].astype(f32) * LOG2E, hcur, ya.reshape(b, s, D_A), q.reshape(b, s, D_B),
            k.reshape(b, s, BLK), v.reshape(b, s, BLK), bias, gout[None, D_A:][:, perm], wout,
            g_ffn[li][None], wr, br)

        r = R_EXP
        n_sort = t // TL_SORT
        max_rows = 2 * t + n_sort * N_EXPERTS * (RUN_ALIGN - 1)
        n_tiles = -(-max_rows // r) + N_EXPERTS
        slot, runs, offs = _route_pos(route_e, r)
        starts = offs[0, :, 0]
        padded = offs[1, :, 0]
        ends = starts + padded
        n_used = ends[-1] // r
        tile_idx = jnp.minimum(jnp.arange(n_tiles, dtype=i32), n_used - 1)
        tile_exp = jnp.minimum(
            jnp.sum((ends[None, :] <= (tile_idx * r)[:, None]).astype(i32), axis=1), N_EXPERTS - 1)
        tile_exp = tile_exp.astype(i32)
        first = jnp.concatenate([jnp.ones((1,), i32),
                                 (tile_exp[1:] != tile_exp[:-1]).astype(i32)])
        next_tile = ends[tile_exp] // r
        next_exp = jnp.where(next_tile < n_used, tile_exp[jnp.minimum(next_tile, n_tiles - 1)], -1)
        fetch_slot = (jnp.cumsum(first) - 1) % 2
        run_tiles = runs[:, :, :, 0].reshape(n_sort, 1, 3 * N_EXPERTS)
        xs = _sort_rows(ends, padded, run_tiles, m_rows, slot, route_w, n_tiles * r, r)
        ys = _experts(tile_idx, tile_exp, first, next_exp.astype(i32), fetch_slot.astype(i32),
                      n_used.reshape(1), xs, w_gate_e[li], w_up_e[li], w_down_e[li], r)

        out = _combine_out(run_tiles, h.reshape(t, d), slot[:2].T,
                           p[li].reshape(t, PLE_DIM), ys, w_ple_gate[li].astype(bf16),
                           b_ple_gate[li][None], w_ple_proj[li].astype(bf16), g_ple[li][None],
                           g_final[None])
        hcur = out.reshape(b, s, d)
    return hcur
```

```python
import functools
import math

import jax
import jax.numpy as jnp
import numpy as np
from jax import lax
from jax.experimental import pallas as pl
from jax.experimental.pallas import tpu as pltpu

D_MODEL = 1024
D_A = 512
D_B = 512
BLK = 128
A_HEADS = 4
HEAD_DIM = 64
N_HEADS = 8
N_KV = 2
GQA = 4
WINDOW = 128
NUM_BUCKETS = 32
MAX_DIST = 128
D_IN = 2 * D_A + D_B + 2 * N_KV * HEAD_DIM
N_GROUPS = 4
E_PER_GROUP = 8
N_EXPERTS = 32
D_FF_E = 256
PLE_DIM = 256
EPS = 1e-6
NEG = -1e30
LOG2E = math.log2(math.e)

TM_MIX = 512
TQ_ATT = 512
TL_SORT = 512
RUN_ALIGN = 16
SMALL_BITS = 2
BIG_PIECE = RUN_ALIGN << SMALL_BITS
LB_SORT = 2 * TL_SORT + N_EXPERTS * RUN_ALIGN
R_EXP = 512
XS_COLS = D_MODEL + BLK
XS_RING = 3
ROUTE_ROWS = 8 + N_EXPERTS
VMEM_LIMIT = 48 * 1024 * 1024

f32 = jnp.float32
bf16 = jnp.bfloat16
i32 = jnp.int32


def _rms(x, g):
    return x * lax.rsqrt(jnp.mean(x * x, axis=-1, keepdims=True) + EPS) * g


def _gelu_tanh(x):
    c = math.sqrt(2.0 / math.pi)
    return x * (0.5 * (1.0 + jnp.tanh(c * (x + 0.044715 * (x * x * x)))))


def _bucket_table():
    n = NUM_BUCKETS // 2
    max_exact = n // 2
    i = np.arange(BLK)[:, None]
    j = np.arange(3 * BLK)[None, :]
    rel = j - BLK - i
    ret = np.where(rel > 0, n, 0)
    a = np.abs(rel)
    large = max_exact + (np.log(np.maximum(a, 1).astype(np.float64) / max_exact)
                         / math.log(MAX_DIST / max_exact) * (n - max_exact)).astype(np.int32)
    large = np.minimum(large, n - 1)
    return (ret + np.where(a < max_exact, a, large)).astype(np.int32)


def _bias_body(rb_ref, bucket_ref, o_ref):
    rows = 16
    for r0 in range(0, BLK, rows):
        bucket = bucket_ref[r0:r0 + rows, :]
        acc = [jnp.zeros(bucket.shape, f32) for _ in range(N_HEADS)]
        for b in range(NUM_BUCKETS):
            hit = bucket == b
            acc = [jnp.where(hit, rb_ref[b, h], acc[h]) for h in range(N_HEADS)]
        for h in range(N_HEADS):
            o_ref[h, r0:r0 + rows, :] = acc[h]


def _bias_table(rel_bias):
    bucket = jnp.asarray(_bucket_table())
    return pl.pallas_call(
        _bias_body,
        out_shape=jax.ShapeDtypeStruct((N_HEADS, BLK, 3 * BLK), f32),
        in_specs=[pl.BlockSpec(memory_space=pltpu.SMEM),
                  pl.BlockSpec((BLK, 3 * BLK), lambda: (0, 0))],
        out_specs=pl.BlockSpec((N_HEADS, BLK, 3 * BLK), lambda: (0, 0, 0)),
        name="bias_table",
    )(rel_bias.astype(f32), bucket)


def _mix_in_body(x_ref, gmix_ref, win_ref, lng_ref, lnb_ref, ws_ref, bs_ref, gout_ref,
                 ya_ref, q_ref, k_ref, v_ref):
    tm = x_ref.shape[0]
    nc = tm // BLK
    a = _rms(x_ref[...], gmix_ref[...])
    z = jnp.dot(a.astype(bf16), win_ref[...], preferred_element_type=f32)
    uv = _gelu_tanh(z[:, :2 * D_A])
    u = uv[:, :D_A]
    v = uv[:, D_A:]
    mu = jnp.mean(v, axis=-1, keepdims=True)
    vc = v - mu
    var = jnp.mean(vc * vc, axis=-1, keepdims=True)
    vn = (vc * lax.rsqrt(var + EPS) * lng_ref[...] + lnb_ref[...]).astype(bf16)
    cols = []
    for h in range(A_HEADS):
        rhs = jnp.concatenate(
            [vn[c * BLK:(c + 1) * BLK, h * BLK:(h + 1) * BLK] for c in range(nc)], axis=1)
        r = jnp.dot(ws_ref[h], rhs, preferred_element_type=f32)
        cols.append(jnp.concatenate(
            [r[:, c * BLK:(c + 1) * BLK] + bs_ref[h] for c in range(nc)], axis=0))
    sv = jnp.concatenate(cols, axis=1)
    ya_ref[...] = _rms(u * sv, gout_ref[...]).astype(bf16)
    q_ref[...] = z[:, 2 * D_A:2 * D_A + D_B].astype(bf16)
    k_ref[...] = z[:, 2 * D_A + D_B:2 * D_A + D_B + BLK].astype(bf16)
    v_ref[...] = z[:, 2 * D_A + D_B + BLK:].astype(bf16)


def _mix_in(x2, gmix, win, lng, lnb, ws, bs, gout_a):
    t = x2.shape[0]
    tm = TM_MIX
    const = lambda shape: pl.BlockSpec(shape, lambda i: (0,) * len(shape))
    return pl.pallas_call(
        _mix_in_body,
        grid=(t // tm,),
        out_shape=(jax.ShapeDtypeStruct((t, D_A), bf16),
                   jax.ShapeDtypeStruct((t, D_B), bf16),
                   jax.ShapeDtypeStruct((t, BLK), bf16),
                   jax.ShapeDtypeStruct((t, BLK), bf16)),
        in_specs=[pl.BlockSpec((tm, D_MODEL), lambda i: (i, 0)),
                  const((1, D_MODEL)), const((D_MODEL, D_IN)),
                  const((1, D_A)), const((1, D_A)),
                  const((A_HEADS, BLK, BLK)), const((A_HEADS, BLK, BLK)),
                  const((1, D_A))],
        out_specs=(pl.BlockSpec((tm, D_A), lambda i: (i, 0)),
                   pl.BlockSpec((tm, D_B), lambda i: (i, 0)),
                   pl.BlockSpec((tm, BLK), lambda i: (i, 0)),
                   pl.BlockSpec((tm, BLK), lambda i: (i, 0))),
        compiler_params=pltpu.CompilerParams(
            dimension_semantics=("parallel",), vmem_limit_bytes=VMEM_LIMIT),
        name="mix_in",
    )(x2, gmix, win, lng, lnb, ws, bs, gout_a)


def _attn_out_body(sink_ref, x_ref, ya_ref, q_ref, kp_ref, km_ref, kn_ref, vp_ref, vm_ref, vn_ref,
                   bias_ref, goutb_ref, wout_ref, gffn_ref, wr_ref, br_ref,
                   h_ref, m_ref, re_ref, rw_ref,
                   kf_ref, vf_ref, e_ref, yb_ref, *, seq_len):
    tq = x_ref.shape[1]
    nb = tq // BLK
    step = pl.program_id(0)
    n_tok_tiles = pl.num_programs(0) - 1
    ti = jnp.minimum(step, n_tok_tiles - 1) % (seq_len // tq)

    @pl.when(step == 0)
    def _():
        yb_ref[1] = jnp.zeros(yb_ref.shape[1:], f32)

    ybn = _rms(yb_ref[1], goutb_ref[...]).astype(bf16)
    y = jnp.concatenate([ya_ref[0], ybn], axis=1)
    h = x_ref[0] + jnp.dot(y, wout_ref[...], preferred_element_type=f32)
    h_ref[0] = h
    m = _rms(h, gffn_ref[...])
    m_ref[...] = m.astype(bf16)

    logit_t = jnp.dot(m.astype(bf16), wr_ref[...], preferred_element_type=f32)
    logit = jnp.transpose(logit_t)[:ROUTE_ROWS] + br_ref[...]
    sub = lax.broadcasted_iota(i32, (8, tq), 0)
    lg = logit[0:8]
    mg = jnp.max(lg, axis=0, keepdims=True)
    pg_top = 1.0 / jnp.sum(jnp.exp(lg - mg), axis=0, keepdims=True)
    g_idx = jnp.min(jnp.where(lg == mg, sub, 8), axis=0, keepdims=True)
    sel = logit[8:16]
    for g in range(1, N_GROUPS):
        sel = jnp.where(g_idx == g, logit[8 + 8 * g:16 + 8 * g], sel)
    m1 = jnp.max(sel, axis=0, keepdims=True)
    i1 = jnp.min(jnp.where(sel == m1, sub, 8), axis=0, keepdims=True)
    sel2 = jnp.where(sub == i1, -jnp.inf, sel)
    m2 = jnp.max(sel2, axis=0, keepdims=True)
    i2 = jnp.min(jnp.where(sel2 == m2, sub, 8), axis=0, keepdims=True)
    r = jnp.exp(m2 - m1)
    w1 = pg_top / (1.0 + r)
    w2 = pg_top * r / (1.0 + r)
    e1 = g_idx * E_PER_GROUP + i1
    e2 = g_idx * E_PER_GROUP + i2
    re_ref[...] = jnp.where(sub == 0, e1, jnp.where(sub == 1, e2, 0))
    rw_ref[...] = jnp.where(sub == 0, w1, jnp.where(sub == 1, w2, 0.0))

    kf_ref[0:BLK] = kp_ref[0]
    kf_ref[BLK:BLK + tq] = km_ref[0]
    kf_ref[BLK + tq:] = kn_ref[0]
    vlane = lax.broadcasted_iota(i32, (BLK, BLK), 1)
    for src, r0, r1 in ((vp_ref, 0, BLK), (vn_ref, BLK + tq, tq + 2 * BLK)) + tuple(
            (vm_ref.at[:, pl.ds(c * BLK, BLK), :], BLK + c * BLK, BLK + (c + 1) * BLK)
            for c in range(nb)):
        vals = src[0].astype(f32)
        vf_ref[0, r0:r1] = jnp.where(vlane < HEAD_DIM, vals,
                                     jnp.where(vlane == HEAD_DIM, 1.0, 0.0)).astype(bf16)
        vf_ref[1, r0:r1] = jnp.where(vlane >= HEAD_DIM, vals,
                                     jnp.where(vlane == 0, 1.0, 0.0)).astype(bf16)

    row = lax.broadcasted_iota(i32, (BLK, 3 * BLK), 0)
    col = lax.broadcasted_iota(i32, (BLK, 3 * BLK), 1)
    band = jnp.abs(col - BLK - row) <= WINDOW
    lane = lax.broadcasted_iota(i32, (BLK, BLK), 1)
    low_half = lane < HEAD_DIM

    for n in range(nb):
        r0 = n * BLK
        qb = q_ref[0, pl.ds(r0, BLK), :]
        kb = kf_ref[pl.ds(r0, 3 * BLK), :]
        zero = jnp.zeros((BLK, BLK), bf16)
        lhs = []
        for kh in range(N_KV):
            for g in range(GQA):
                tile = qb[:, g * BLK:(g + 1) * BLK]
                lhs.append(jnp.where(low_half, tile, zero) if kh == 0
                           else jnp.where(low_half, zero, tile))
        lhs = jnp.concatenate(lhs, axis=0)
        s_all = lax.dot_general(lhs, kb, (((1,), (1,)), ((), ())),
                                preferred_element_type=f32)
        kpos = col + ((ti * nb + n - 1) * BLK)
        valid = band & (kpos >= 0) & (kpos < seq_len)
        sink_e = []
        for h in range(N_HEADS):
            s = jnp.where(valid, s_all[h * BLK:(h + 1) * BLK] + bias_ref[h], NEG)
            sk = sink_ref[h]
            mrow = jnp.maximum(jnp.max(s, axis=-1, keepdims=True), sk)
            e_ref[n, h * BLK:(h + 1) * BLK, :] = jnp.exp2(s - mrow).astype(bf16)
            sink_e.append(jnp.exp2(sk - mrow))
        half = GQA * BLK
        pv = [jnp.dot(e_ref[n, kh * half:(kh + 1) * half, :], vf_ref[kh, pl.ds(r0, 3 * BLK), :],
                      preferred_element_type=f32) for kh in range(N_KV)]

        def head_out(h):
            kh, g = divmod(h, GQA)
            rows = pv[kh][g * BLK:(g + 1) * BLK]
            ones_col = HEAD_DIM if kh == 0 else 0
            return rows * (1.0 / (rows[:, ones_col:ones_col + 1] + sink_e[h]))

        for g in range(GQA):
            yb_ref[0, pl.ds(r0, BLK), g * BLK:(g + 1) * BLK] = jnp.where(
                low_half, head_out(g), head_out(GQA + g))

    yb_ref[1] = yb_ref[0]


def _attn_out(sink, x, ya, q, k, v, bias, goutb, wout, gffn, wr, br):
    b, s, _ = x.shape
    tq = TQ_ATT
    nb = tq // BLK
    nblk = s // BLK
    t = b * s
    nq = s // tq
    n_tok_tiles = b * nq
    att = lambda s: jnp.minimum(s, n_tok_tiles - 1)
    epi = lambda s: jnp.maximum(s - 1, 0)
    const = lambda shape: pl.BlockSpec(shape, lambda s, *_: (0,) * len(shape))
    tok_att = lambda w: pl.BlockSpec((1, tq, w), lambda s, *_: (att(s) // nq, att(s) % nq, 0))
    tok_epi = lambda w: pl.BlockSpec((1, tq, w), lambda s, *_: (epi(s) // nq, epi(s) % nq, 0))
    prev = pl.BlockSpec((1, BLK, BLK), lambda s, *_: (
        att(s) // nq, jnp.maximum((att(s) % nq) * nb - 1, 0), 0))
    nxt = pl.BlockSpec((1, BLK, BLK), lambda s, *_: (
        att(s) // nq, jnp.minimum((att(s) % nq) * nb + nb, nblk - 1), 0))
    rows = pl.BlockSpec((tq, D_MODEL), lambda s, *_: (epi(s), 0))
    lanes = lambda rows: pl.BlockSpec((rows, tq), lambda s, *_: (0, epi(s)))
    grid_spec = pltpu.PrefetchScalarGridSpec(
        num_scalar_prefetch=1,
        grid=(n_tok_tiles + 1,),
        in_specs=[tok_epi(D_MODEL), tok_epi(D_A), tok_att(D_B),
                  prev, tok_att(BLK), nxt, prev, tok_att(BLK), nxt,
                  const((N_HEADS, BLK, 3 * BLK)), const((1, D_B)), const((D_MODEL, D_MODEL)),
                  const((1, D_MODEL)), const((D_MODEL, BLK)), const((ROUTE_ROWS, tq))],
        out_specs=(tok_epi(D_MODEL), rows, lanes(8), lanes(8)),
        scratch_shapes=[pltpu.VMEM((tq + 2 * BLK, BLK), bf16),
                        pltpu.VMEM((N_KV, tq + 2 * BLK, BLK), bf16),
                        pltpu.VMEM((nb, N_HEADS * BLK, 3 * BLK), bf16),
                        pltpu.VMEM((2, tq, D_B), f32)])
    return pl.pallas_call(
        functools.partial(_attn_out_body, seq_len=s),
        grid_spec=grid_spec,
        out_shape=(jax.ShapeDtypeStruct((b, s, D_MODEL), f32),
                   jax.ShapeDtypeStruct((t, D_MODEL), bf16),
                   jax.ShapeDtypeStruct((8, t), i32),
                   jax.ShapeDtypeStruct((8, t), f32)),
        compiler_params=pltpu.CompilerParams(
            dimension_semantics=("arbitrary",), vmem_limit_bytes=VMEM_LIMIT),
        name="attn_out",
    )(sink, x, ya, q, k, k, k, v, v, v, bias, goutb, wout, gffn, wr, br)


def _exclusive_prefix(vals):
    a = lax.broadcasted_iota(i32, (N_EXPERTS, N_EXPERTS), 0)
    c = lax.broadcasted_iota(i32, (N_EXPERTS, N_EXPERTS), 1)
    low = jnp.where(c < a, 1.0, 0.0).astype(bf16)
    hi = (vals >> 8).astype(f32).astype(bf16)
    lo = (vals & 255).astype(f32).astype(bf16)
    return (jnp.dot(low, hi, preferred_element_type=f32) * 256.0
            + jnp.dot(low, lo, preferred_element_type=f32)).astype(i32)


def _round_up_pow2(vals, mult):
    log_m = mult.bit_length() - 1
    return ((vals + (mult - 1)) >> log_m) << log_m


def _route_pos_body(re_ref, slot_ref, slot_t_ref, runs_ref, offs_ref, tri_ref, *, rows_per_tile, tl):
    n_tok_tiles = re_ref.shape[1] // tl
    a = lax.broadcasted_iota(i32, (tl, tl), 0)
    c = lax.broadcasted_iota(i32, (tl, tl), 1)
    tri_ref[...] = jnp.where(a < c, 1.0, 0.0).astype(bf16)
    eid = lax.broadcasted_iota(i32, (N_EXPERTS, tl), 0)

    def tile_hits(j):
        cols = pl.ds(pl.multiple_of(j * tl, tl), tl)
        hit1 = eid == re_ref[0:1, cols]
        hit2 = eid == re_ref[1:2, cols]
        onehot = jnp.where(hit1 | hit2, 1.0, 0.0)
        cnt = jnp.sum(onehot, axis=1, keepdims=True).astype(i32)
        run_len = jnp.broadcast_to(_round_up_pow2(cnt, RUN_ALIGN), (N_EXPERTS, BLK))
        return cols, hit1, hit2, onehot, run_len

    total = lax.fori_loop(0, n_tok_tiles, lambda j, acc: acc + tile_hits(j)[4],
                          jnp.zeros((N_EXPERTS, BLK), i32))
    padded = _round_up_pow2(total, rows_per_tile)
    starts = _exclusive_prefix(padded)
    offs_ref[0] = starts
    offs_ref[1] = padded

    def place(j, global_off):
        cols, hit1, hit2, onehot, run_len = tile_hits(j)
        local_off = _exclusive_prefix(run_len)
        before = jnp.dot(onehot.astype(bf16), tri_ref[...], preferred_element_type=f32)
        slot = before + local_off[:, 0:1].astype(f32)
        s1 = jnp.sum(jnp.where(hit1, slot, 0.0), axis=0, keepdims=True)
        s2 = jnp.sum(jnp.where(hit2, slot, 0.0), axis=0, keepdims=True)
        sub = lax.broadcasted_iota(i32, (8, tl), 0)
        slot_ref[:, cols] = jnp.where(sub == 0, s1.astype(i32),
                                      jnp.where(sub == 1, s2.astype(i32), 0))
        sub_t = lax.broadcasted_iota(i32, (BLK, tl), 0)
        slots_t = jnp.transpose(jnp.where(sub_t == 0, s1, jnp.where(sub_t == 1, s2, 0.0)))
        slot_t_ref[cols, :] = slots_t[:, :8].astype(i32)
        runs_ref[j, 0] = local_off
        runs_ref[j, 1] = run_len
        runs_ref[j, 2] = global_off
        return global_off + run_len

    lax.fori_loop(0, n_tok_tiles, place, starts)


def _route_pos(route_e, rows_per_tile):
    t = route_e.shape[1]
    tl = TL_SORT
    return pl.pallas_call(
        functools.partial(_route_pos_body, rows_per_tile=rows_per_tile, tl=tl),
        out_shape=(jax.ShapeDtypeStruct((8, t), i32),
                   jax.ShapeDtypeStruct((t, 8), i32),
                   jax.ShapeDtypeStruct((t // tl, 3, N_EXPERTS, BLK), i32),
                   jax.ShapeDtypeStruct((2, N_EXPERTS, BLK), i32)),
        scratch_shapes=[pltpu.VMEM((tl, tl), bf16)],
        compiler_params=pltpu.CompilerParams(vmem_limit_bytes=VMEM_LIMIT),
        name="route_pos",
    )(route_e)


def _for_each_run_piece(runs_ref, fn):
    def per_expert(e, carry):
        local_off = runs_ref[0, 0, e]
        length = runs_ref[0, 0, N_EXPERTS + e]
        global_off = runs_ref[0, 0, 2 * N_EXPERTS + e]
        units = length >> (RUN_ALIGN.bit_length() - 1)

        def big_piece(k, inner):
            done = k * BIG_PIECE
            fn(pl.multiple_of(local_off + done, RUN_ALIGN),
               pl.multiple_of(global_off + done, RUN_ALIGN), BIG_PIECE)
            return inner

        lax.fori_loop(0, units >> SMALL_BITS, big_piece, 0)
        for b in range(SMALL_BITS):
            @pl.when(((units >> b) & 1) == 1)
            def _():
                done = ((units >> (b + 1)) << (b + 1)) * RUN_ALIGN
                fn(pl.multiple_of(local_off + done, RUN_ALIGN),
                   pl.multiple_of(global_off + done, RUN_ALIGN), RUN_ALIGN << b)
        return carry

    lax.fori_loop(0, N_EXPERTS, per_expert, 0)


def _tile_run_rows(runs_ref):
    return lax.fori_loop(0, N_EXPERTS, lambda e, acc: acc + runs_ref[0, 0, N_EXPERTS + e], 0)


def _wait_rows(total_rows, make_copy):
    units = total_rows >> (RUN_ALIGN.bit_length() - 1)
    for b in range((LB_SORT // RUN_ALIGN).bit_length()):
        @pl.when(((units >> b) & 1) == 1)
        def _():
            make_copy(RUN_ALIGN << b).wait()


def _sort_rows_body(ends_ref, padded_ref, runs_ref, m_ref, slot_ref, rw_ref, xs_ref,
                    local_ref, zero_ref, pending_ref, sem_ref, zsem_ref, *, rows_per_tile):
    tl = m_ref.shape[0]
    lb = local_ref.shape[1]
    step = pl.program_id(0)
    cur = step % 2
    n_tiles = xs_ref.shape[0] // rows_per_tile
    n_used = ends_ref[N_EXPERTS - 1] // rows_per_tile

    def zero_tile(tile):
        start = pl.multiple_of(tile * rows_per_tile, rows_per_tile)
        return pltpu.make_async_copy(zero_ref, xs_ref.at[pl.ds(start, rows_per_tile), :], zsem_ref)

    @pl.when(pl.program_id(0) == 0)
    def _():
        zero_ref[...] = jnp.zeros(zero_ref.shape, bf16)
        for wait in (False, True):
            for e in range(N_EXPERTS):
                @pl.when(padded_ref[e] > 0)
                def _():
                    cp = zero_tile(ends_ref[e] // rows_per_tile - 1)
                    cp.wait() if wait else cp.start()

            def spare(j, carry):
                cp = zero_tile(n_used + j)
                cp.wait() if wait else cp.start()
                return carry

            lax.fori_loop(0, n_tiles - n_used, spare, 0)

    s1 = slot_ref[0:1, :]
    s2 = slot_ref[1:2, :]
    w1 = rw_ref[0:1, :]
    w2 = rw_ref[1:2, :]
    run_rows = _tile_run_rows(runs_ref)

    def sort_rows_upto(n):
        srow = lax.broadcasted_iota(i32, (n, tl), 0)
        p1 = srow == s1
        p2 = srow == s2
        perm = jnp.where(p1 | p2, 1.0, 0.0).astype(bf16)
        rows = jnp.dot(perm, m_ref[...], preferred_element_type=f32)
        w = jnp.sum(jnp.where(p1, w1, 0.0) + jnp.where(p2, w2, 0.0), axis=1, keepdims=True)
        w_hi = w.astype(bf16).astype(f32)
        w_lo = w - w_hi
        lane = lax.broadcasted_iota(i32, (n, BLK), 1)
        local_ref[cur, :n, :D_MODEL] = rows.astype(bf16)
        local_ref[cur, :n, D_MODEL:] = jnp.where(
            lane == 0, w_hi, jnp.where(lane == 1, w_lo, 0.0)).astype(bf16)

    def piece(s, local_row, global_row, rows):
        return pltpu.make_async_copy(local_ref.at[s, pl.ds(local_row, rows), :],
                                     xs_ref.at[pl.ds(global_row, rows), :], sem_ref.at[s])

    short = lb - 256
    pl.when(run_rows <= short)(functools.partial(sort_rows_upto, short))
    pl.when(run_rows > short)(functools.partial(sort_rows_upto, lb))

    _for_each_run_piece(runs_ref, lambda l, g, n: piece(cur, l, g, n).start())

    @pl.when(step > 0)
    def _():
        _wait_rows(pending_ref[0], lambda n: piece(1 - cur, 0, 0, n))

    pending_ref[0] = run_rows

    @pl.when(step == pl.num_programs(0) - 1)
    def _():
        _wait_rows(pending_ref[0], lambda n: piece(cur, 0, 0, n))


def _sort_rows(ends, padded, run_tiles, m, slot, route_w, total_rows, rows_per_tile):
    t = m.shape[0]
    tl = TL_SORT
    grid_spec = pltpu.PrefetchScalarGridSpec(
        num_scalar_prefetch=2,
        grid=(t // tl,),
        in_specs=[pl.BlockSpec((1, 1, 3 * N_EXPERTS), lambda i, *_: (i, 0, 0),
                               memory_space=pltpu.SMEM),
                  pl.BlockSpec((tl, D_MODEL), lambda i, *_: (i, 0)),
                  pl.BlockSpec((8, tl), lambda i, *_: (0, i)),
                  pl.BlockSpec((8, tl), lambda i, *_: (0, i))],
        out_specs=pl.BlockSpec(memory_space=pl.ANY),
        scratch_shapes=[pltpu.VMEM((2, LB_SORT, XS_COLS), bf16),
                        pltpu.VMEM((rows_per_tile, XS_COLS), bf16),
                        pltpu.SMEM((1,), i32),
                        pltpu.SemaphoreType.DMA((2,)),
                        pltpu.SemaphoreType.DMA(())])
    return pl.pallas_call(
        functools.partial(_sort_rows_body, rows_per_tile=rows_per_tile),
        grid_spec=grid_spec,
        out_shape=jax.ShapeDtypeStruct((total_rows, XS_COLS), bf16),
        compiler_params=pltpu.CompilerParams(
            dimension_semantics=("arbitrary",), vmem_limit_bytes=VMEM_LIMIT),
        name="sort_rows",
    )(ends, padded, run_tiles, m, slot, route_w)


def _experts_body(tile_ref, texp_ref, first_ref, next_ref, slot_ref, nused_ref,
                  xs_hbm, wg_hbm, wu_hbm, wd_hbm, ys_ref,
                  xbuf_ref, sg_ref, su_ref, sd_ref, wg_ref, wu_ref, wd_ref, xsem_ref, sem_ref):
    i = pl.program_id(0)
    n_used = nused_ref[0]
    r = xbuf_ref.shape[1]

    def row_tile(tile):
        s = tile % XS_RING
        start = pl.multiple_of(tile * r, r)
        return pltpu.make_async_copy(xs_hbm.at[pl.ds(start, r), :], xbuf_ref.at[s], xsem_ref.at[s])

    @pl.when(i == 0)
    def _():
        for ahead in range(XS_RING - 1):
            @pl.when(ahead < n_used)
            def _():
                row_tile(ahead).start()

    @pl.when(i + (XS_RING - 1) < n_used)
    def _():
        row_tile(i + (XS_RING - 1)).start()

    def fetches(expert, s):
        return (pltpu.make_async_copy(wg_hbm.at[expert], sg_ref.at[s], sem_ref.at[s]),
                pltpu.make_async_copy(wu_hbm.at[expert], su_ref.at[s], sem_ref.at[s]),
                pltpu.make_async_copy(wd_hbm.at[expert], sd_ref.at[s], sem_ref.at[s]))

    @pl.when(i < n_used)
    def _():
        s = slot_ref[i]
        row_tile(i).wait()
        xs_ref = xbuf_ref.at[i % XS_RING]

        @pl.when(first_ref[i] == 1)
        def _():
            @pl.when(i == 0)
            def _():
                for cp in fetches(texp_ref[0], 0):
                    cp.start()

            for cp in fetches(texp_ref[i], s):
                cp.wait()

            @pl.when(next_ref[i] >= 0)
            def _():
                for cp in fetches(next_ref[i], 1 - s):
                    cp.start()

            wg_ref[...] = sg_ref[s].astype(bf16)
            wu_ref[...] = su_ref[s].astype(bf16)
            wd_ref[...] = sd_ref[s].astype(bf16)

        w_row = (xs_ref[:, D_MODEL:D_MODEL + 1].astype(f32)
                 + xs_ref[:, D_MODEL + 1:D_MODEL + 2].astype(f32))
        gate = jnp.dot(xs_ref[:, :D_MODEL], wg_ref[...], preferred_element_type=f32)
        up = jnp.dot(xs_ref[:, :D_MODEL], wu_ref[...], preferred_element_type=f32)
        hdn = (gate * jax.nn.sigmoid(gate)) * up
        y = jnp.dot(hdn.astype(bf16), wd_ref[...], preferred_element_type=f32)
        ys_ref[...] = (y * w_row).astype(bf16)

    @pl.when(i >= n_used)
    def _():
        ys_ref[...] = jnp.zeros(ys_ref.shape, bf16)


def _experts(tile_idx, tile_exp, first, next_exp, slot, n_used, xs, wg, wu, wd, rows_per_tile):
    r = rows_per_tile
    n_tiles = xs.shape[0] // r
    grid_spec = pltpu.PrefetchScalarGridSpec(
        num_scalar_prefetch=6,
        grid=(n_tiles,),
        in_specs=[pl.BlockSpec(memory_space=pl.ANY),
                  pl.BlockSpec(memory_space=pl.ANY),
                  pl.BlockSpec(memory_space=pl.ANY),
                  pl.BlockSpec(memory_space=pl.ANY)],
        out_specs=pl.BlockSpec((r, D_MODEL), lambda i, *_: (i, 0)),
        scratch_shapes=[pltpu.VMEM((XS_RING, r, XS_COLS), bf16),
                        pltpu.VMEM((2, D_MODEL, D_FF_E), f32),
                        pltpu.VMEM((2, D_MODEL, D_FF_E), f32),
                        pltpu.VMEM((2, D_FF_E, D_MODEL), f32),
                        pltpu.VMEM((D_MODEL, D_FF_E), bf16),
                        pltpu.VMEM((D_MODEL, D_FF_E), bf16),
                        pltpu.VMEM((D_FF_E, D_MODEL), bf16),
                        pltpu.SemaphoreType.DMA((XS_RING,)),
                        pltpu.SemaphoreType.DMA((2,))])
    return pl.pallas_call(
        _experts_body,
        grid_spec=grid_spec,
        out_shape=jax.ShapeDtypeStruct((n_tiles * r, D_MODEL), bf16),
        compiler_params=pltpu.CompilerParams(
            dimension_semantics=("arbitrary",), vmem_limit_bytes=VMEM_LIMIT),
        name="experts",
    )(tile_idx, tile_exp, first, next_exp, slot, n_used, xs, wg, wu, wd)


def _combine_out_body(runs_cur_ref, runs_nxt_ref, h_ref, slot_ref, p_ref, ys_ref,
                      wpg_ref, bpg_ref, wpp_ref, gple_ref, gfin_ref,
                      o_ref, ybuf_ref, moe_ref, sem_ref):
    tk = h_ref.shape[0]
    lb = ybuf_ref.shape[1]
    step = pl.program_id(0)
    n = pl.num_programs(0) - 1
    cur = jnp.minimum(step, n - 1) % 2

    def piece(s, local_row, global_row, rows):
        return pltpu.make_async_copy(ys_ref.at[pl.ds(global_row, rows), :],
                                     ybuf_ref.at[s, pl.ds(local_row, rows), :], sem_ref.at[s])

    @pl.when(step == 0)
    def _():
        ybuf_ref[...] = jnp.zeros(ybuf_ref.shape, bf16)
        moe_ref[1] = jnp.zeros(moe_ref.shape[1:], f32)
        _for_each_run_piece(runs_cur_ref, lambda l, g, r: piece(0, l, g, r).start())

    @pl.when(step + 1 < n)
    def _():
        _for_each_run_piece(runs_nxt_ref, lambda l, g, r: piece(1 - cur, l, g, r).start())

    run_rows = _tile_run_rows(runs_cur_ref)

    @pl.when(step < n)
    def _():
        _wait_rows(run_rows, lambda r: piece(cur, 0, 0, r))

    def work(rows):
        h2 = h_ref[...] + moe_ref[1]
        gate = jax.nn.sigmoid(jnp.dot(h2.astype(bf16), wpg_ref[...], preferred_element_type=f32)
                              + bpg_ref[...])
        pp = jnp.dot(p_ref[...].astype(bf16), wpp_ref[...], preferred_element_type=f32)
        h3 = h2 + gate * _rms(pp, gple_ref[...])
        o_ref[...] = _rms(h3, gfin_ref[...])
        col = lax.broadcasted_iota(i32, (tk, rows), 1)
        unperm = jnp.where((col == slot_ref[:, 0:1]) | (col == slot_ref[:, 1:2]), 1.0, 0.0)
        moe_ref[0] = jnp.dot(unperm.astype(bf16), ybuf_ref[cur, :rows, :],
                             preferred_element_type=f32)
        moe_ref[1] = moe_ref[0]

    short = lb - 256
    pl.when(run_rows <= short)(functools.partial(work, short))
    pl.when(run_rows > short)(functools.partial(work, lb))


def _combine_out(run_tiles, h2d, slot_t, p2d, ys, wpg, bpg, wpp, gple, gfin):
    t = h2d.shape[0]
    tk = TL_SORT
    nt = t // tk
    const = lambda shape: pl.BlockSpec(shape, lambda i: (0,) * len(shape))
    runs = lambda imap: pl.BlockSpec((1, 1, 3 * N_EXPERTS), imap, memory_space=pltpu.SMEM)
    uns = lambda s: jnp.minimum(s, nt - 1)
    fin = lambda s: jnp.maximum(s - 1, 0)
    return pl.pallas_call(
        _combine_out_body,
        grid=(nt + 1,),
        out_shape=jax.ShapeDtypeStruct((t, D_MODEL), f32),
        in_specs=[runs(lambda s: (uns(s), 0, 0)),
                  runs(lambda s: (uns(s + 1), 0, 0)),
                  pl.BlockSpec((tk, D_MODEL), lambda s: (fin(s), 0)),
                  pl.BlockSpec((tk, 8), lambda s: (uns(s), 0)),
                  pl.BlockSpec((tk, PLE_DIM), lambda s: (fin(s), 0)),
                  pl.BlockSpec(memory_space=pl.ANY),
                  const((D_MODEL, D_MODEL)), const((1, D_MODEL)),
                  const((PLE_DIM, D_MODEL)), const((1, D_MODEL)), const((1, D_MODEL))],
        out_specs=pl.BlockSpec((tk, D_MODEL), lambda s: (fin(s), 0)),
        scratch_shapes=[pltpu.VMEM((2, LB_SORT, D_MODEL), bf16),
                        pltpu.VMEM((2, tk, D_MODEL), f32),
                        pltpu.SemaphoreType.DMA((2,))],
        compiler_params=pltpu.CompilerParams(
            dimension_semantics=("arbitrary",), vmem_limit_bytes=VMEM_LIMIT),
        name="combine_out",
    )(run_tiles, run_tiles, h2d, slot_t, p2d, ys, wpg, bpg, wpp, gple, gfin)


def _head_major_to_tile_major(a, axis):
    axis = axis % a.ndim
    shape = a.shape
    a = a.reshape(shape[:axis] + (N_KV, GQA, HEAD_DIM) + shape[axis + 1:])
    return jnp.swapaxes(a, axis, axis + 1).reshape(shape)


def kernel(x, p, rel_bias, g_mix, w_in, ln_v_g, ln_v_b, w_spatial, b_spatial, sink, g_out_grp, w_out,
           g_ffn, w_router_group, b_router_group, w_router_expert, b_router_expert, w_gate_e, w_up_e,
           w_down_e, w_ple_proj, g_ple, w_ple_gate, b_ple_gate, g_final):
    b, s, d = x.shape
    t = b * s
    depth = g_mix.shape[0]
    assert depth == 1 and d == D_MODEL
    c1, c2 = 2 * D_A, 2 * D_A + D_B
    bias = _bias_table(rel_bias * LOG2E)
    hcur = x.astype(f32)
    for li in range(depth):
        wi = w_in[li]
        wq = _head_major_to_tile_major(wi[:, c1:c2], 1) * (LOG2E * HEAD_DIM ** -0.5)
        win = jnp.concatenate([wi[:, :c1], wq, wi[:, c2:]], axis=1).astype(bf16)
        bs = jnp.broadcast_to(b_spatial[li][:, :, None], (A_HEADS, BLK, BLK)).astype(f32)
        gout = g_out_grp[li]
        gout_b = _head_major_to_tile_major(gout[D_A:], 0)[None]
        wo = w_out[li]
        wout = jnp.concatenate([wo[:D_A], _head_major_to_tile_major(wo[D_A:], 0)],
                               axis=0).astype(bf16)
        wr = jnp.concatenate([
            w_router_group[li], jnp.zeros((D_MODEL, 8 - N_GROUPS), f32), w_router_expert[li],
            jnp.zeros((D_MODEL, BLK - ROUTE_ROWS), f32)], axis=1).astype(bf16)
        br = jnp.concatenate([b_router_group[li], jnp.full((8 - N_GROUPS,), NEG, f32),
                              b_router_expert[li]])
        br = jnp.broadcast_to(br[:, None], (ROUTE_ROWS, TQ_ATT))

        ya, q, k, v = _mix_in(hcur.reshape(t, d), g_mix[li][None], win, ln_v_g[li][None],
                              ln_v_b[li][None], w_spatial[li].astype(bf16), bs, gout[None, :D_A])
        h, m_rows, route_e, route_w = _attn_out(
            sink[li].astype(f32) * LOG2E, hcur, ya.reshape(b, s, D_A), q.reshape(b, s, D_B),
            k.reshape(b, s, BLK), v.reshape(b, s, BLK), bias, gout_b, wout,
            g_ffn[li][None], wr, br)

        r = R_EXP
        n_sort = t // TL_SORT
        max_rows = 2 * t + n_sort * N_EXPERTS * (RUN_ALIGN - 1)
        n_tiles = -(-max_rows // r) + N_EXPERTS
        slot, slot_t, runs, offs = _route_pos(route_e, r)
        starts = offs[0, :, 0]
        padded = offs[1, :, 0]
        ends = starts + padded
        n_used = ends[-1] // r
        tile_idx = jnp.minimum(jnp.arange(n_tiles, dtype=i32), n_used - 1)
        tile_exp = jnp.minimum(
            jnp.sum((ends[None, :] <= (tile_idx * r)[:, None]).astype(i32), axis=1), N_EXPERTS - 1)
        tile_exp = tile_exp.astype(i32)
        first = jnp.concatenate([jnp.ones((1,), i32),
                                 (tile_exp[1:] != tile_exp[:-1]).astype(i32)])
        next_tile = ends[tile_exp] // r
        next_exp = jnp.where(next_tile < n_used, tile_exp[jnp.minimum(next_tile, n_tiles - 1)], -1)
        fetch_slot = (jnp.cumsum(first) - 1) % 2
        run_tiles = runs[:, :, :, 0].reshape(n_sort, 1, 3 * N_EXPERTS)
        xs = _sort_rows(ends, padded, run_tiles, m_rows, slot, route_w, n_tiles * r, r)
        ys = _experts(tile_idx, tile_exp, first, next_exp.astype(i32), fetch_slot.astype(i32),
                      n_used.reshape(1), xs, w_gate_e[li], w_up_e[li], w_down_e[li], r)

        out = _combine_out(run_tiles, h.reshape(t, d), slot_t,
                           p[li].reshape(t, PLE_DIM), ys, w_ple_gate[li].astype(bf16),
                           b_ple_gate[li][None], w_ple_proj[li].astype(bf16), g_ple[li][None],
                           g_final[None])
        hcur = out.reshape(b, s, d)
    return hcur
```

```python
import functools
import math

import jax
import jax.numpy as jnp
import numpy as np
from jax import lax
from jax.experimental import pallas as pl
from jax.experimental.pallas import tpu as pltpu

D_MODEL = 1024
D_A = 512
D_B = 512
BLK = 128
A_HEADS = 4
HEAD_DIM = 64
N_HEADS = 8
N_KV = 2
GQA = 4
WINDOW = 128
NUM_BUCKETS = 32
MAX_DIST = 128
D_IN = 2 * D_A + D_B + 2 * N_KV * HEAD_DIM
N_GROUPS = 4
E_PER_GROUP = 8
N_EXPERTS = 32
D_FF_E = 256
PLE_DIM = 256
EPS = 1e-6
NEG = -1e30
LOG2E = math.log2(math.e)

TM_MIX = 512
TQ_ATT = 512
TL_SORT = 512
RUN_ALIGN = 16
SMALL_BITS = 2
BIG_PIECE = RUN_ALIGN << SMALL_BITS
LB_SORT = 2 * TL_SORT + N_EXPERTS * RUN_ALIGN
R_EXP = 512
XS_COLS = D_MODEL + BLK
XS_RING = 3
ROUTE_ROWS = 8 + N_EXPERTS
VMEM_LIMIT = 48 * 1024 * 1024

f32 = jnp.float32
bf16 = jnp.bfloat16
i32 = jnp.int32


def _rms(x, g):
    return x * lax.rsqrt(jnp.mean(x * x, axis=-1, keepdims=True) + EPS) * g


def _gelu_tanh(x):
    c = math.sqrt(2.0 / math.pi)
    return x * (0.5 * (1.0 + jnp.tanh(c * (x + 0.044715 * (x * x * x)))))


def _bucket_table():
    n = NUM_BUCKETS // 2
    max_exact = n // 2
    i = np.arange(BLK)[:, None]
    j = np.arange(3 * BLK)[None, :]
    rel = j - BLK - i
    ret = np.where(rel > 0, n, 0)
    a = np.abs(rel)
    large = max_exact + (np.log(np.maximum(a, 1).astype(np.float64) / max_exact)
                         / math.log(MAX_DIST / max_exact) * (n - max_exact)).astype(np.int32)
    large = np.minimum(large, n - 1)
    return (ret + np.where(a < max_exact, a, large)).astype(np.int32)


def _bias_body(rb_ref, bucket_ref, o_ref):
    rows = 16
    for r0 in range(0, BLK, rows):
        bucket = bucket_ref[r0:r0 + rows, :]
        acc = [jnp.zeros(bucket.shape, f32) for _ in range(N_HEADS)]
        for b in range(NUM_BUCKETS):
            hit = bucket == b
            acc = [jnp.where(hit, rb_ref[b, h], acc[h]) for h in range(N_HEADS)]
        for h in range(N_HEADS):
            o_ref[h, r0:r0 + rows, :] = acc[h]


def _bias_table(rel_bias):
    bucket = jnp.asarray(_bucket_table())
    return pl.pallas_call(
        _bias_body,
        out_shape=jax.ShapeDtypeStruct((N_HEADS, BLK, 3 * BLK), f32),
        in_specs=[pl.BlockSpec(memory_space=pltpu.SMEM),
                  pl.BlockSpec((BLK, 3 * BLK), lambda: (0, 0))],
        out_specs=pl.BlockSpec((N_HEADS, BLK, 3 * BLK), lambda: (0, 0, 0)),
        name="bias_table",
    )(rel_bias.astype(f32), bucket)


def _mix_in_body(x_ref, gmix_ref, win_ref, lng_ref, lnb_ref, ws_ref, bs_ref, gout_ref,
                 ya_ref, q_ref, k_ref, v_ref):
    tm = x_ref.shape[0]
    nc = tm // BLK
    a = _rms(x_ref[...], gmix_ref[...])
    z = jnp.dot(a.astype(bf16), win_ref[...], preferred_element_type=f32)
    uv = _gelu_tanh(z[:, :2 * D_A])
    u = uv[:, :D_A]
    v = uv[:, D_A:]
    mu = jnp.mean(v, axis=-1, keepdims=True)
    vc = v - mu
    var = jnp.mean(vc * vc, axis=-1, keepdims=True)
    vn = (vc * lax.rsqrt(var + EPS) * lng_ref[...] + lnb_ref[...]).astype(bf16)
    cols = []
    for h in range(A_HEADS):
        rhs = jnp.concatenate(
            [vn[c * BLK:(c + 1) * BLK, h * BLK:(h + 1) * BLK] for c in range(nc)], axis=1)
        r = jnp.dot(ws_ref[h], rhs, preferred_element_type=f32)
        cols.append(jnp.concatenate(
            [r[:, c * BLK:(c + 1) * BLK] + bs_ref[h] for c in range(nc)], axis=0))
    sv = jnp.concatenate(cols, axis=1)
    ya_ref[...] = _rms(u * sv, gout_ref[...]).astype(bf16)
    q_ref[...] = z[:, 2 * D_A:2 * D_A + D_B].astype(bf16)
    k_ref[...] = z[:, 2 * D_A + D_B:2 * D_A + D_B + BLK].astype(bf16)
    v_ref[...] = z[:, 2 * D_A + D_B + BLK:].astype(bf16)


def _mix_in(x2, gmix, win, lng, lnb, ws, bs, gout_a):
    t = x2.shape[0]
    tm = TM_MIX
    const = lambda shape: pl.BlockSpec(shape, lambda i: (0,) * len(shape))
    return pl.pallas_call(
        _mix_in_body,
        grid=(t // tm,),
        out_shape=(jax.ShapeDtypeStruct((t, D_A), bf16),
                   jax.ShapeDtypeStruct((t, D_B), bf16),
                   jax.ShapeDtypeStruct((t, BLK), bf16),
                   jax.ShapeDtypeStruct((t, BLK), bf16)),
        in_specs=[pl.BlockSpec((tm, D_MODEL), lambda i: (i, 0)),
                  const((1, D_MODEL)), const((D_MODEL, D_IN)),
                  const((1, D_A)), const((1, D_A)),
                  const((A_HEADS, BLK, BLK)), const((A_HEADS, BLK, BLK)),
                  const((1, D_A))],
        out_specs=(pl.BlockSpec((tm, D_A), lambda i: (i, 0)),
                   pl.BlockSpec((tm, D_B), lambda i: (i, 0)),
                   pl.BlockSpec((tm, BLK), lambda i: (i, 0)),
                   pl.BlockSpec((tm, BLK), lambda i: (i, 0))),
        compiler_params=pltpu.CompilerParams(
            dimension_semantics=("parallel",), vmem_limit_bytes=VMEM_LIMIT),
        name="mix_in",
    )(x2, gmix, win, lng, lnb, ws, bs, gout_a)


def _attn_out_body(sink_ref, x_ref, ya_ref, q_ref, kp_ref, km_ref, kn_ref, vp_ref, vm_ref, vn_ref,
                   bias_ref, goutb_ref, wout_ref, gffn_ref, wr_ref, br_ref,
                   h_ref, m_ref, re_ref, rw_ref,
                   kf_ref, vf_ref, e_ref, yb_ref, *, seq_len):
    tq = x_ref.shape[1]
    nb = tq // BLK
    step = pl.program_id(0)
    n_tok_tiles = pl.num_programs(0) - 1
    ti = jnp.minimum(step, n_tok_tiles - 1) % (seq_len // tq)

    @pl.when(step == 0)
    def _():
        yb_ref[1] = jnp.zeros(yb_ref.shape[1:], f32)

    ybn = _rms(yb_ref[1], goutb_ref[...]).astype(bf16)
    y = jnp.concatenate([ya_ref[0], ybn], axis=1)
    h = x_ref[0] + jnp.dot(y, wout_ref[...], preferred_element_type=f32)
    h_ref[0] = h
    m = _rms(h, gffn_ref[...])
    m_ref[...] = m.astype(bf16)

    logit_t = jnp.dot(m.astype(bf16), wr_ref[...], preferred_element_type=f32)
    logit = jnp.transpose(logit_t)[:ROUTE_ROWS] + br_ref[...]
    sub = lax.broadcasted_iota(i32, (8, tq), 0)
    lg = logit[0:8]
    mg = jnp.max(lg, axis=0, keepdims=True)
    pg_top = 1.0 / jnp.sum(jnp.exp(lg - mg), axis=0, keepdims=True)
    g_idx = jnp.min(jnp.where(lg == mg, sub, 8), axis=0, keepdims=True)
    sel = logit[8:16]
    for g in range(1, N_GROUPS):
        sel = jnp.where(g_idx == g, logit[8 + 8 * g:16 + 8 * g], sel)
    m1 = jnp.max(sel, axis=0, keepdims=True)
    i1 = jnp.min(jnp.where(sel == m1, sub, 8), axis=0, keepdims=True)
    sel2 = jnp.where(sub == i1, -jnp.inf, sel)
    m2 = jnp.max(sel2, axis=0, keepdims=True)
    i2 = jnp.min(jnp.where(sel2 == m2, sub, 8), axis=0, keepdims=True)
    r = jnp.exp(m2 - m1)
    w1 = pg_top / (1.0 + r)
    w2 = pg_top * r / (1.0 + r)
    e1 = g_idx * E_PER_GROUP + i1
    e2 = g_idx * E_PER_GROUP + i2
    re_ref[...] = jnp.where(sub == 0, e1, jnp.where(sub == 1, e2, 0))
    rw_ref[...] = jnp.where(sub == 0, w1, jnp.where(sub == 1, w2, 0.0))

    kf_ref[0:BLK] = kp_ref[0]
    kf_ref[BLK:BLK + tq] = km_ref[0]
    kf_ref[BLK + tq:] = kn_ref[0]
    vlane = lax.broadcasted_iota(i32, (BLK, BLK), 1)
    for src, r0, r1 in ((vp_ref, 0, BLK), (vn_ref, BLK + tq, tq + 2 * BLK)) + tuple(
            (vm_ref.at[:, pl.ds(c * BLK, BLK), :], BLK + c * BLK, BLK + (c + 1) * BLK)
            for c in range(nb)):
        vals = src[0].astype(f32)
        vf_ref[0, r0:r1] = jnp.where(vlane < HEAD_DIM, vals,
                                     jnp.where(vlane == HEAD_DIM, 1.0, 0.0)).astype(bf16)
        vf_ref[1, r0:r1] = jnp.where(vlane >= HEAD_DIM, vals,
                                     jnp.where(vlane == 0, 1.0, 0.0)).astype(bf16)

    row = lax.broadcasted_iota(i32, (BLK, 3 * BLK), 0)
    col = lax.broadcasted_iota(i32, (BLK, 3 * BLK), 1)
    band = jnp.abs(col - BLK - row) <= WINDOW
    lane = lax.broadcasted_iota(i32, (BLK, BLK), 1)
    low_half = lane < HEAD_DIM

    for n in range(nb):
        r0 = n * BLK
        qb = q_ref[0, pl.ds(r0, BLK), :]
        kb = kf_ref[pl.ds(r0, 3 * BLK), :]
        zero = jnp.zeros((BLK, BLK), bf16)
        lhs = []
        for kh in range(N_KV):
            for g in range(GQA):
                tile = qb[:, g * BLK:(g + 1) * BLK]
                lhs.append(jnp.where(low_half, tile, zero) if kh == 0
                           else jnp.where(low_half, zero, tile))
        lhs = jnp.concatenate(lhs, axis=0)
        s_all = lax.dot_general(lhs, kb, (((1,), (1,)), ((), ())),
                                preferred_element_type=f32)
        kpos = col + ((ti * nb + n - 1) * BLK)
        valid = band & (kpos >= 0) & (kpos < seq_len)
        sink_e = []
        for h in range(N_HEADS):
            s = jnp.where(valid, s_all[h * BLK:(h + 1) * BLK] + bias_ref[h], NEG)
            sk = sink_ref[h]
            mrow = jnp.maximum(jnp.max(s, axis=-1, keepdims=True), sk)
            e_ref[n, h * BLK:(h + 1) * BLK, :] = jnp.exp2(s - mrow).astype(bf16)
            sink_e.append(jnp.exp2(sk - mrow))
        half = GQA * BLK
        pv = [jnp.dot(e_ref[n, kh * half:(kh + 1) * half, :], vf_ref[kh, pl.ds(r0, 3 * BLK), :],
                      preferred_element_type=f32) for kh in range(N_KV)]

        def head_out(h):
            kh, g = divmod(h, GQA)
            rows = pv[kh][g * BLK:(g + 1) * BLK]
            ones_col = HEAD_DIM if kh == 0 else 0
            return rows * (1.0 / (rows[:, ones_col:ones_col + 1] + sink_e[h]))

        for g in range(GQA):
            yb_ref[0, pl.ds(r0, BLK), g * BLK:(g + 1) * BLK] = jnp.where(
                low_half, head_out(g), head_out(GQA + g))

    yb_ref[1] = yb_ref[0]


def _attn_out(sink, x, ya, q, k, v, bias, goutb, wout, gffn, wr, br):
    b, s, _ = x.shape
    tq = TQ_ATT
    nb = tq // BLK
    nblk = s // BLK
    t = b * s
    nq = s // tq
    n_tok_tiles = b * nq
    att = lambda s: jnp.minimum(s, n_tok_tiles - 1)
    epi = lambda s: jnp.maximum(s - 1, 0)
    const = lambda shape: pl.BlockSpec(shape, lambda s, *_: (0,) * len(shape))
    tok_att = lambda w: pl.BlockSpec((1, tq, w), lambda s, *_: (att(s) // nq, att(s) % nq, 0))
    tok_epi = lambda w: pl.BlockSpec((1, tq, w), lambda s, *_: (epi(s) // nq, epi(s) % nq, 0))
    prev = pl.BlockSpec((1, BLK, BLK), lambda s, *_: (
        att(s) // nq, jnp.maximum((att(s) % nq) * nb - 1, 0), 0))
    nxt = pl.BlockSpec((1, BLK, BLK), lambda s, *_: (
        att(s) // nq, jnp.minimum((att(s) % nq) * nb + nb, nblk - 1), 0))
    rows = pl.BlockSpec((tq, D_MODEL), lambda s, *_: (epi(s), 0))
    lanes = lambda rows: pl.BlockSpec((rows, tq), lambda s, *_: (0, epi(s)))
    grid_spec = pltpu.PrefetchScalarGridSpec(
        num_scalar_prefetch=1,
        grid=(n_tok_tiles + 1,),
        in_specs=[tok_epi(D_MODEL), tok_epi(D_A), tok_att(D_B),
                  prev, tok_att(BLK), nxt, prev, tok_att(BLK), nxt,
                  const((N_HEADS, BLK, 3 * BLK)), const((1, D_B)), const((D_MODEL, D_MODEL)),
                  const((1, D_MODEL)), const((D_MODEL, BLK)), const((ROUTE_ROWS, tq))],
        out_specs=(tok_epi(D_MODEL), rows, lanes(8), lanes(8)),
        scratch_shapes=[pltpu.VMEM((tq + 2 * BLK, BLK), bf16),
                        pltpu.VMEM((N_KV, tq + 2 * BLK, BLK), bf16),
                        pltpu.VMEM((nb, N_HEADS * BLK, 3 * BLK), bf16),
                        pltpu.VMEM((2, tq, D_B), f32)])
    return pl.pallas_call(
        functools.partial(_attn_out_body, seq_len=s),
        grid_spec=grid_spec,
        out_shape=(jax.ShapeDtypeStruct((b, s, D_MODEL), f32),
                   jax.ShapeDtypeStruct((t, D_MODEL), bf16),
                   jax.ShapeDtypeStruct((8, t), i32),
                   jax.ShapeDtypeStruct((8, t), f32)),
        compiler_params=pltpu.CompilerParams(
            dimension_semantics=("arbitrary",), vmem_limit_bytes=VMEM_LIMIT),
        name="attn_out",
    )(sink, x, ya, q, k, k, k, v, v, v, bias, goutb, wout, gffn, wr, br)


def _exclusive_prefix(vals):
    a = lax.broadcasted_iota(i32, (N_EXPERTS, N_EXPERTS), 0)
    c = lax.broadcasted_iota(i32, (N_EXPERTS, N_EXPERTS), 1)
    low = jnp.where(c < a, 1.0, 0.0).astype(bf16)
    hi = (vals >> 8).astype(f32).astype(bf16)
    lo = (vals & 255).astype(f32).astype(bf16)
    return (jnp.dot(low, hi, preferred_element_type=f32) * 256.0
            + jnp.dot(low, lo, preferred_element_type=f32)).astype(i32)


def _round_up_pow2(vals, mult):
    log_m = mult.bit_length() - 1
    return ((vals + (mult - 1)) >> log_m) << log_m


def _route_pos_body(re_ref, slot_ref, runs_ref, offs_ref, tri_ref, *, rows_per_tile, tl):
    n_tok_tiles = re_ref.shape[1] // tl
    a = lax.broadcasted_iota(i32, (tl, tl), 0)
    c = lax.broadcasted_iota(i32, (tl, tl), 1)
    tri_ref[...] = jnp.where(a < c, 1.0, 0.0).astype(bf16)
    eid = lax.broadcasted_iota(i32, (N_EXPERTS, tl), 0)

    def tile_hits(j):
        cols = pl.ds(pl.multiple_of(j * tl, tl), tl)
        hit1 = eid == re_ref[0:1, cols]
        hit2 = eid == re_ref[1:2, cols]
        onehot = jnp.where(hit1 | hit2, 1.0, 0.0)
        cnt = jnp.sum(onehot, axis=1, keepdims=True).astype(i32)
        run_len = jnp.broadcast_to(_round_up_pow2(cnt, RUN_ALIGN), (N_EXPERTS, BLK))
        return cols, hit1, hit2, onehot, run_len

    total = lax.fori_loop(0, n_tok_tiles, lambda j, acc: acc + tile_hits(j)[4],
                          jnp.zeros((N_EXPERTS, BLK), i32))
    padded = _round_up_pow2(total, rows_per_tile)
    starts = _exclusive_prefix(padded)
    offs_ref[0] = starts
    offs_ref[1] = padded
    offs_ref[2] = total

    def place(j, global_off):
        cols, hit1, hit2, onehot, run_len = tile_hits(j)
        local_off = _exclusive_prefix(run_len)
        before = jnp.dot(onehot.astype(bf16), tri_ref[...], preferred_element_type=f32)
        slot = before + local_off[:, 0:1].astype(f32)
        s1 = jnp.sum(jnp.where(hit1, slot, 0.0), axis=0, keepdims=True)
        s2 = jnp.sum(jnp.where(hit2, slot, 0.0), axis=0, keepdims=True)
        sub = lax.broadcasted_iota(i32, (8, tl), 0)
        slot_ref[:, cols] = jnp.where(sub == 0, s1.astype(i32),
                                      jnp.where(sub == 1, s2.astype(i32), 0))
        runs_ref[j, 0] = local_off
        runs_ref[j, 1] = run_len
        runs_ref[j, 2] = global_off
        return global_off + run_len

    lax.fori_loop(0, n_tok_tiles, place, starts)


def _route_pos(route_e, rows_per_tile):
    t = route_e.shape[1]
    tl = TL_SORT
    return pl.pallas_call(
        functools.partial(_route_pos_body, rows_per_tile=rows_per_tile, tl=tl),
        out_shape=(jax.ShapeDtypeStruct((8, t), i32),
                   jax.ShapeDtypeStruct((t // tl, 3, N_EXPERTS, BLK), i32),
                   jax.ShapeDtypeStruct((3, N_EXPERTS, BLK), i32)),
        scratch_shapes=[pltpu.VMEM((tl, tl), bf16)],
        compiler_params=pltpu.CompilerParams(vmem_limit_bytes=VMEM_LIMIT),
        name="route_pos",
    )(route_e)


def _for_each_run_piece(runs_ref, fn):
    def per_expert(e, carry):
        local_off = runs_ref[0, 0, e]
        length = runs_ref[0, 0, N_EXPERTS + e]
        global_off = runs_ref[0, 0, 2 * N_EXPERTS + e]
        units = length >> (RUN_ALIGN.bit_length() - 1)

        def big_piece(k, inner):
            done = k * BIG_PIECE
            fn(pl.multiple_of(local_off + done, RUN_ALIGN),
               pl.multiple_of(global_off + done, RUN_ALIGN), BIG_PIECE)
            return inner

        lax.fori_loop(0, units >> SMALL_BITS, big_piece, 0)
        for b in range(SMALL_BITS):
            @pl.when(((units >> b) & 1) == 1)
            def _():
                done = ((units >> (b + 1)) << (b + 1)) * RUN_ALIGN
                fn(pl.multiple_of(local_off + done, RUN_ALIGN),
                   pl.multiple_of(global_off + done, RUN_ALIGN), RUN_ALIGN << b)
        return carry

    lax.fori_loop(0, N_EXPERTS, per_expert, 0)


def _tile_run_rows(runs_ref):
    return lax.fori_loop(0, N_EXPERTS, lambda e, acc: acc + runs_ref[0, 0, N_EXPERTS + e], 0)


def _wait_rows(total_rows, make_copy):
    units = total_rows >> (RUN_ALIGN.bit_length() - 1)
    for b in range((LB_SORT // RUN_ALIGN).bit_length()):
        @pl.when(((units >> b) & 1) == 1)
        def _():
            make_copy(RUN_ALIGN << b).wait()


def _sort_rows_body(ends_ref, padded_ref, total_ref, runs_ref, m_ref, slot_ref, rw_ref,
                    xs_ref, slot_t_ref,
                    local_ref, zero_ref, pending_ref, sem_ref, zsem_ref, *, rows_per_tile):
    tl = m_ref.shape[0]
    lb = local_ref.shape[1]
    step = pl.program_id(0)
    cur = step % 2
    n_tiles = xs_ref.shape[0] // rows_per_tile
    n_used = ends_ref[N_EXPERTS - 1] // rows_per_tile

    def zero_rows(start, rows, wait):
        cp = pltpu.make_async_copy(zero_ref.at[pl.ds(0, rows), :],
                                   xs_ref.at[pl.ds(pl.multiple_of(start, RUN_ALIGN), rows), :],
                                   zsem_ref)
        cp.wait() if wait else cp.start()

    def zero_fill(wait):
        for e in range(N_EXPERTS):
            pad_units = (padded_ref[e] - total_ref[e]) >> (RUN_ALIGN.bit_length() - 1)
            for bit in range((rows_per_tile // RUN_ALIGN).bit_length() - 1):
                @pl.when(((pad_units >> bit) & 1) == 1)
                def _():
                    done = ((pad_units >> (bit + 1)) << (bit + 1)) * RUN_ALIGN
                    zero_rows(ends_ref[e] - (padded_ref[e] - total_ref[e]) + done,
                              RUN_ALIGN << bit, wait)

        def spare(j, carry):
            zero_rows((n_used + j) * rows_per_tile, rows_per_tile, wait)
            return carry

        lax.fori_loop(0, n_tiles - n_used, spare, 0)

    @pl.when(step == 0)
    def _():
        zero_ref[...] = jnp.zeros(zero_ref.shape, bf16)
        zero_fill(wait=False)

    s1 = slot_ref[0:1, :]
    s2 = slot_ref[1:2, :]
    w1 = rw_ref[0:1, :]
    w2 = rw_ref[1:2, :]
    run_rows = _tile_run_rows(runs_ref)

    def sort_rows_upto(n):
        srow = lax.broadcasted_iota(i32, (n, tl), 0)
        p1 = srow == s1
        p2 = srow == s2
        perm = jnp.where(p1 | p2, 1.0, 0.0).astype(bf16)
        rows = jnp.dot(perm, m_ref[...], preferred_element_type=f32)
        w = jnp.sum(jnp.where(p1, w1, 0.0) + jnp.where(p2, w2, 0.0), axis=1, keepdims=True)
        w_hi = w.astype(bf16).astype(f32)
        w_lo = w - w_hi
        lane = lax.broadcasted_iota(i32, (n, BLK), 1)
        local_ref[cur, :n, :D_MODEL] = rows.astype(bf16)
        local_ref[cur, :n, D_MODEL:] = jnp.where(
            lane == 0, w_hi, jnp.where(lane == 1, w_lo, 0.0)).astype(bf16)
        sub_t = lax.broadcasted_iota(i32, (BLK, tl), 0)
        slots = jnp.where(sub_t == 0, s1.astype(f32), jnp.where(sub_t == 1, s2.astype(f32), 0.0))
        slot_t_ref[...] = jnp.transpose(slots)[:, :8].astype(i32)

    def piece(s, local_row, global_row, rows):
        return pltpu.make_async_copy(local_ref.at[s, pl.ds(local_row, rows), :],
                                     xs_ref.at[pl.ds(global_row, rows), :], sem_ref.at[s])

    short = lb - 256
    pl.when(run_rows <= short)(functools.partial(sort_rows_upto, short))
    pl.when(run_rows > short)(functools.partial(sort_rows_upto, lb))

    _for_each_run_piece(runs_ref, lambda l, g, n: piece(cur, l, g, n).start())

    @pl.when(step > 0)
    def _():
        _wait_rows(pending_ref[0], lambda n: piece(1 - cur, 0, 0, n))

    pending_ref[0] = run_rows

    @pl.when(step == pl.num_programs(0) - 1)
    def _():
        _wait_rows(pending_ref[0], lambda n: piece(cur, 0, 0, n))
        zero_fill(wait=True)


def _sort_rows(ends, padded, totals, run_tiles, m, slot, route_w, total_rows, rows_per_tile):
    t = m.shape[0]
    tl = TL_SORT
    grid_spec = pltpu.PrefetchScalarGridSpec(
        num_scalar_prefetch=3,
        grid=(t // tl,),
        in_specs=[pl.BlockSpec((1, 1, 3 * N_EXPERTS), lambda i, *_: (i, 0, 0),
                               memory_space=pltpu.SMEM),
                  pl.BlockSpec((tl, D_MODEL), lambda i, *_: (i, 0)),
                  pl.BlockSpec((8, tl), lambda i, *_: (0, i)),
                  pl.BlockSpec((8, tl), lambda i, *_: (0, i))],
        out_specs=(pl.BlockSpec(memory_space=pl.ANY),
                   pl.BlockSpec((tl, 8), lambda i, *_: (i, 0))),
        scratch_shapes=[pltpu.VMEM((2, LB_SORT, XS_COLS), bf16),
                        pltpu.VMEM((rows_per_tile, XS_COLS), bf16),
                        pltpu.SMEM((1,), i32),
                        pltpu.SemaphoreType.DMA((2,)),
                        pltpu.SemaphoreType.DMA(())])
    return pl.pallas_call(
        functools.partial(_sort_rows_body, rows_per_tile=rows_per_tile),
        grid_spec=grid_spec,
        out_shape=(jax.ShapeDtypeStruct((total_rows, XS_COLS), bf16),
                   jax.ShapeDtypeStruct((t, 8), i32)),
        compiler_params=pltpu.CompilerParams(
            dimension_semantics=("arbitrary",), vmem_limit_bytes=VMEM_LIMIT),
        name="sort_rows",
    )(ends, padded, totals, run_tiles, m, slot, route_w)


def _experts_body(tile_ref, texp_ref, first_ref, next_ref, slot_ref, nused_ref,
                  xs_hbm, wg_hbm, wu_hbm, wd_hbm, ys_ref,
                  xbuf_ref, sg_ref, su_ref, sd_ref, wg_ref, wu_ref, wd_ref, xsem_ref, sem_ref):
    i = pl.program_id(0)
    n_used = nused_ref[0]
    r = xbuf_ref.shape[1]

    def row_tile(tile):
        s = tile % XS_RING
        start = pl.multiple_of(tile * r, r)
        return pltpu.make_async_copy(xs_hbm.at[pl.ds(start, r), :], xbuf_ref.at[s], xsem_ref.at[s])

    @pl.when(i == 0)
    def _():
        for ahead in range(XS_RING - 1):
            @pl.when(ahead < n_used)
            def _():
                row_tile(ahead).start()

    @pl.when(i + (XS_RING - 1) < n_used)
    def _():
        row_tile(i + (XS_RING - 1)).start()

    def fetches(expert, s):
        return (pltpu.make_async_copy(wg_hbm.at[expert], sg_ref.at[s], sem_ref.at[s]),
                pltpu.make_async_copy(wu_hbm.at[expert], su_ref.at[s], sem_ref.at[s]),
                pltpu.make_async_copy(wd_hbm.at[expert], sd_ref.at[s], sem_ref.at[s]))

    @pl.when(i < n_used)
    def _():
        s = slot_ref[i]
        row_tile(i).wait()
        xs_ref = xbuf_ref.at[i % XS_RING]

        @pl.when(first_ref[i] == 1)
        def _():
            @pl.when(i == 0)
            def _():
                for cp in fetches(texp_ref[0], 0):
                    cp.start()

            for cp in fetches(texp_ref[i], s):
                cp.wait()

            @pl.when(next_ref[i] >= 0)
            def _():
                for cp in fetches(next_ref[i], 1 - s):
                    cp.start()

            wg_ref[...] = sg_ref[s].astype(bf16)
            wu_ref[...] = su_ref[s].astype(bf16)
            wd_ref[...] = sd_ref[s].astype(bf16)

        w_row = (xs_ref[:, D_MODEL:D_MODEL + 1].astype(f32)
                 + xs_ref[:, D_MODEL + 1:D_MODEL + 2].astype(f32))
        gate = jnp.dot(xs_ref[:, :D_MODEL], wg_ref[...], preferred_element_type=f32)
        up = jnp.dot(xs_ref[:, :D_MODEL], wu_ref[...], preferred_element_type=f32)
        hdn = (gate * jax.nn.sigmoid(gate)) * up
        y = jnp.dot(hdn.astype(bf16), wd_ref[...], preferred_element_type=f32)
        ys_ref[...] = (y * w_row).astype(bf16)

    @pl.when(i >= n_used)
    def _():
        ys_ref[...] = jnp.zeros(ys_ref.shape, bf16)


def _experts(tile_idx, tile_exp, first, next_exp, slot, n_used, xs, wg, wu, wd, rows_per_tile):
    r = rows_per_tile
    n_tiles = xs.shape[0] // r
    grid_spec = pltpu.PrefetchScalarGridSpec(
        num_scalar_prefetch=6,
        grid=(n_tiles,),
        in_specs=[pl.BlockSpec(memory_space=pl.ANY),
                  pl.BlockSpec(memory_space=pl.ANY),
                  pl.BlockSpec(memory_space=pl.ANY),
                  pl.BlockSpec(memory_space=pl.ANY)],
        out_specs=pl.BlockSpec((r, D_MODEL), lambda i, *_: (i, 0)),
        scratch_shapes=[pltpu.VMEM((XS_RING, r, XS_COLS), bf16),
                        pltpu.VMEM((2, D_MODEL, D_FF_E), f32),
                        pltpu.VMEM((2, D_MODEL, D_FF_E), f32),
                        pltpu.VMEM((2, D_FF_E, D_MODEL), f32),
                        pltpu.VMEM((D_MODEL, D_FF_E), bf16),
                        pltpu.VMEM((D_MODEL, D_FF_E), bf16),
                        pltpu.VMEM((D_FF_E, D_MODEL), bf16),
                        pltpu.SemaphoreType.DMA((XS_RING,)),
                        pltpu.SemaphoreType.DMA((2,))])
    return pl.pallas_call(
        _experts_body,
        grid_spec=grid_spec,
        out_shape=jax.ShapeDtypeStruct((n_tiles * r, D_MODEL), bf16),
        compiler_params=pltpu.CompilerParams(
            dimension_semantics=("arbitrary",), vmem_limit_bytes=VMEM_LIMIT),
        name="experts",
    )(tile_idx, tile_exp, first, next_exp, slot, n_used, xs, wg, wu, wd)


def _combine_out_body(runs_cur_ref, runs_nxt_ref, h_ref, slot_ref, p_ref, ys_ref,
                      wpg_ref, bpg_ref, wpp_ref, gple_ref, gfin_ref,
                      o_ref, ybuf_ref, moe_ref, sem_ref):
    tk = h_ref.shape[0]
    lb = ybuf_ref.shape[1]
    step = pl.program_id(0)
    n = pl.num_programs(0) - 1
    cur = jnp.minimum(step, n - 1) % 2

    def piece(s, local_row, global_row, rows):
        return pltpu.make_async_copy(ys_ref.at[pl.ds(global_row, rows), :],
                                     ybuf_ref.at[s, pl.ds(local_row, rows), :], sem_ref.at[s])

    @pl.when(step == 0)
    def _():
        ybuf_ref[...] = jnp.zeros(ybuf_ref.shape, bf16)
        moe_ref[1] = jnp.zeros(moe_ref.shape[1:], f32)
        _for_each_run_piece(runs_cur_ref, lambda l, g, r: piece(0, l, g, r).start())

    @pl.when(step + 1 < n)
    def _():
        _for_each_run_piece(runs_nxt_ref, lambda l, g, r: piece(1 - cur, l, g, r).start())

    run_rows = _tile_run_rows(runs_cur_ref)

    @pl.when(step < n)
    def _():
        _wait_rows(run_rows, lambda r: piece(cur, 0, 0, r))

    def work(rows):
        h2 = h_ref[...] + moe_ref[1]
        gate = jax.nn.sigmoid(jnp.dot(h2.astype(bf16), wpg_ref[...], preferred_element_type=f32)
                              + bpg_ref[...])
        pp = jnp.dot(p_ref[...].astype(bf16), wpp_ref[...], preferred_element_type=f32)
        h3 = h2 + gate * _rms(pp, gple_ref[...])
        o_ref[...] = _rms(h3, gfin_ref[...])
        col = lax.broadcasted_iota(i32, (tk, rows), 1)
        unperm = jnp.where((col == slot_ref[:, 0:1]) | (col == slot_ref[:, 1:2]), 1.0, 0.0)
        moe_ref[0] = jnp.dot(unperm.astype(bf16), ybuf_ref[cur, :rows, :],
                             preferred_element_type=f32)
        moe_ref[1] = moe_ref[0]

    short = lb - 256
    pl.when(run_rows <= short)(functools.partial(work, short))
    pl.when(run_rows > short)(functools.partial(work, lb))


def _combine_out(run_tiles, h2d, slot_t, p2d, ys, wpg, bpg, wpp, gple, gfin):
    t = h2d.shape[0]
    tk = TL_SORT
    nt = t // tk
    const = lambda shape: pl.BlockSpec(shape, lambda i: (0,) * len(shape))
    runs = lambda imap: pl.BlockSpec((1, 1, 3 * N_EXPERTS), imap, memory_space=pltpu.SMEM)
    uns = lambda s: jnp.minimum(s, nt - 1)
    fin = lambda s: jnp.maximum(s - 1, 0)
    return pl.pallas_call(
        _combine_out_body,
        grid=(nt + 1,),
        out_shape=jax.ShapeDtypeStruct((t, D_MODEL), f32),
        in_specs=[runs(lambda s: (uns(s), 0, 0)),
                  runs(lambda s: (uns(s + 1), 0, 0)),
                  pl.BlockSpec((tk, D_MODEL), lambda s: (fin(s), 0)),
                  pl.BlockSpec((tk, 8), lambda s: (uns(s), 0)),
                  pl.BlockSpec((tk, PLE_DIM), lambda s: (fin(s), 0)),
                  pl.BlockSpec(memory_space=pl.ANY),
                  const((D_MODEL, D_MODEL)), const((1, D_MODEL)),
                  const((PLE_DIM, D_MODEL)), const((1, D_MODEL)), const((1, D_MODEL))],
        out_specs=pl.BlockSpec((tk, D_MODEL), lambda s: (fin(s), 0)),
        scratch_shapes=[pltpu.VMEM((2, LB_SORT, D_MODEL), bf16),
                        pltpu.VMEM((2, tk, D_MODEL), f32),
                        pltpu.SemaphoreType.DMA((2,))],
        compiler_params=pltpu.CompilerParams(
            dimension_semantics=("arbitrary",), vmem_limit_bytes=VMEM_LIMIT),
        name="combine_out",
    )(run_tiles, run_tiles, h2d, slot_t, p2d, ys, wpg, bpg, wpp, gple, gfin)


def _head_major_to_tile_major(a, axis):
    axis = axis % a.ndim
    shape = a.shape
    a = a.reshape(shape[:axis] + (N_KV, GQA, HEAD_DIM) + shape[axis + 1:])
    return jnp.swapaxes(a, axis, axis + 1).reshape(shape)


def kernel(x, p, rel_bias, g_mix, w_in, ln_v_g, ln_v_b, w_spatial, b_spatial, sink, g_out_grp, w_out,
           g_ffn, w_router_group, b_router_group, w_router_expert, b_router_expert, w_gate_e, w_up_e,
           w_down_e, w_ple_proj, g_ple, w_ple_gate, b_ple_gate, g_final):
    b, s, d = x.shape
    t = b * s
    depth = g_mix.shape[0]
    assert depth == 1 and d == D_MODEL
    c1, c2 = 2 * D_A, 2 * D_A + D_B
    bias = _bias_table(rel_bias * LOG2E)
    hcur = x.astype(f32)
    for li in range(depth):
        wi = w_in[li]
        wq = _head_major_to_tile_major(wi[:, c1:c2], 1) * (LOG2E * HEAD_DIM ** -0.5)
        win = jnp.concatenate([wi[:, :c1], wq, wi[:, c2:]], axis=1).astype(bf16)
        bs = jnp.broadcast_to(b_spatial[li][:, :, None], (A_HEADS, BLK, BLK)).astype(f32)
        gout = g_out_grp[li]
        gout_b = _head_major_to_tile_major(gout[D_A:], 0)[None]
        wo = w_out[li]
        wout = jnp.concatenate([wo[:D_A], _head_major_to_tile_major(wo[D_A:], 0)],
                               axis=0).astype(bf16)
        wr = jnp.concatenate([
            w_router_group[li], jnp.zeros((D_MODEL, 8 - N_GROUPS), f32), w_router_expert[li],
            jnp.zeros((D_MODEL, BLK - ROUTE_ROWS), f32)], axis=1).astype(bf16)
        br = jnp.concatenate([b_router_group[li], jnp.full((8 - N_GROUPS,), NEG, f32),
                              b_router_expert[li]])
        br = jnp.broadcast_to(br[:, None], (ROUTE_ROWS, TQ_ATT))

        ya, q, k, v = _mix_in(hcur.reshape(t, d), g_mix[li][None], win, ln_v_g[li][None],
                              ln_v_b[li][None], w_spatial[li].astype(bf16), bs, gout[None, :D_A])
        h, m_rows, route_e, route_w = _attn_out(
            sink[li].astype(f32) * LOG2E, hcur, ya.reshape(b, s, D_A), q.reshape(b, s, D_B),
            k.reshape(b, s, BLK), v.reshape(b, s, BLK), bias, gout_b, wout,
            g_ffn[li][None], wr, br)

        r = R_EXP
        n_sort = t // TL_SORT
        max_rows = 2 * t + n_sort * N_EXPERTS * (RUN_ALIGN - 1)
        n_tiles = -(-max_rows // r) + N_EXPERTS
        slot, runs, offs = _route_pos(route_e, r)
        starts = offs[0, :, 0]
        padded = offs[1, :, 0]
        ends = starts + padded
        n_used = ends[-1] // r
        tile_idx = jnp.minimum(jnp.arange(n_tiles, dtype=i32), n_used - 1)
        tile_exp = jnp.minimum(
            jnp.sum((ends[None, :] <= (tile_idx * r)[:, None]).astype(i32), axis=1), N_EXPERTS - 1)
        tile_exp = tile_exp.astype(i32)
        first = jnp.concatenate([jnp.ones((1,), i32),
                                 (tile_exp[1:] != tile_exp[:-1]).astype(i32)])
        next_tile = ends[tile_exp] // r
        next_exp = jnp.where(next_tile < n_used, tile_exp[jnp.minimum(next_tile, n_tiles - 1)], -1)
        fetch_slot = (jnp.cumsum(first) - 1) % 2
        run_tiles = runs[:, :, :, 0].reshape(n_sort, 1, 3 * N_EXPERTS)
        xs, slot_t = _sort_rows(ends, padded, offs[2, :, 0], run_tiles, m_rows, slot, route_w,
                                n_tiles * r, r)
        ys = _experts(tile_idx, tile_exp, first, next_exp.astype(i32), fetch_slot.astype(i32),
                      n_used.reshape(1), xs, w_gate_e[li], w_up_e[li], w_down_e[li], r)

        out = _combine_out(run_tiles, h.reshape(t, d), slot_t,
                           p[li].reshape(t, PLE_DIM), ys, w_ple_gate[li].astype(bf16),
                           b_ple_gate[li][None], w_ple_proj[li].astype(bf16), g_ple[li][None],
                           g_final[None])
        hcur = out.reshape(b, s, d)
    return hcur
```

```python
import functools
import math

import jax
import jax.numpy as jnp
import numpy as np
from jax import lax
from jax.experimental import pallas as pl
from jax.experimental.pallas import tpu as pltpu

D_MODEL = 1024
D_A = 512
D_B = 512
BLK = 128
A_HEADS = 4
HEAD_DIM = 64
N_HEADS = 8
N_KV = 2
GQA = 4
WINDOW = 128
NUM_BUCKETS = 32
MAX_DIST = 128
D_IN = 2 * D_A + D_B + 2 * N_KV * HEAD_DIM
N_GROUPS = 4
E_PER_GROUP = 8
N_EXPERTS = 32
D_FF_E = 256
PLE_DIM = 256
EPS = 1e-6
NEG = -1e30
LOG2E = math.log2(math.e)

TM_MIX = 512
TQ_ATT = 512
TL_SORT = 512
RUN_ALIGN = 16
SMALL_BITS = 2
BIG_PIECE = RUN_ALIGN << SMALL_BITS
LB_SORT = 2 * TL_SORT + N_EXPERTS * RUN_ALIGN
R_EXP = 512
XS_COLS = D_MODEL + BLK
XS_RING = 4
XS_AHEAD = 2
ROUTE_ROWS = 8 + N_EXPERTS
VMEM_LIMIT = 48 * 1024 * 1024

f32 = jnp.float32
bf16 = jnp.bfloat16
i32 = jnp.int32


def _rms(x, g):
    return x * lax.rsqrt(jnp.mean(x * x, axis=-1, keepdims=True) + EPS) * g


def _gelu_tanh(x):
    c = math.sqrt(2.0 / math.pi)
    return x * (0.5 * (1.0 + jnp.tanh(c * (x + 0.044715 * (x * x * x)))))


def _bucket_table():
    n = NUM_BUCKETS // 2
    max_exact = n // 2
    i = np.arange(BLK)[:, None]
    j = np.arange(3 * BLK)[None, :]
    rel = j - BLK - i
    ret = np.where(rel > 0, n, 0)
    a = np.abs(rel)
    large = max_exact + (np.log(np.maximum(a, 1).astype(np.float64) / max_exact)
                         / math.log(MAX_DIST / max_exact) * (n - max_exact)).astype(np.int32)
    large = np.minimum(large, n - 1)
    return (ret + np.where(a < max_exact, a, large)).astype(np.int32)


def _bias_body(rb_ref, bucket_ref, o_ref):
    rows = 16
    for r0 in range(0, BLK, rows):
        bucket = bucket_ref[r0:r0 + rows, :]
        acc = [jnp.zeros(bucket.shape, f32) for _ in range(N_HEADS)]
        for b in range(NUM_BUCKETS):
            hit = bucket == b
            acc = [jnp.where(hit, rb_ref[b, h], acc[h]) for h in range(N_HEADS)]
        for h in range(N_HEADS):
            o_ref[h, r0:r0 + rows, :] = acc[h]


def _bias_table(rel_bias):
    bucket = jnp.asarray(_bucket_table())
    return pl.pallas_call(
        _bias_body,
        out_shape=jax.ShapeDtypeStruct((N_HEADS, BLK, 3 * BLK), f32),
        in_specs=[pl.BlockSpec(memory_space=pltpu.SMEM),
                  pl.BlockSpec((BLK, 3 * BLK), lambda: (0, 0))],
        out_specs=pl.BlockSpec((N_HEADS, BLK, 3 * BLK), lambda: (0, 0, 0)),
        name="bias_table",
    )(rel_bias.astype(f32), bucket)


def _mix_in_body(x_ref, gmix_ref, win_ref, lng_ref, lnb_ref, ws_ref, bs_ref, gout_ref,
                 ya_ref, q_ref, k_ref, v_ref):
    tm = x_ref.shape[0]
    nc = tm // BLK
    a = _rms(x_ref[...], gmix_ref[...])
    z = jnp.dot(a.astype(bf16), win_ref[...], preferred_element_type=f32)
    uv = _gelu_tanh(z[:, :2 * D_A])
    u = uv[:, :D_A]
    v = uv[:, D_A:]
    mu = jnp.mean(v, axis=-1, keepdims=True)
    vc = v - mu
    var = jnp.mean(vc * vc, axis=-1, keepdims=True)
    vn = (vc * lax.rsqrt(var + EPS) * lng_ref[...] + lnb_ref[...]).astype(bf16)
    cols = []
    for h in range(A_HEADS):
        rhs = jnp.concatenate(
            [vn[c * BLK:(c + 1) * BLK, h * BLK:(h + 1) * BLK] for c in range(nc)], axis=1)
        r = jnp.dot(ws_ref[h], rhs, preferred_element_type=f32)
        cols.append(jnp.concatenate(
            [r[:, c * BLK:(c + 1) * BLK] + bs_ref[h] for c in range(nc)], axis=0))
    sv = jnp.concatenate(cols, axis=1)
    ya_ref[...] = _rms(u * sv, gout_ref[...]).astype(bf16)
    q_ref[...] = z[:, 2 * D_A:2 * D_A + D_B].astype(bf16)
    k_ref[...] = z[:, 2 * D_A + D_B:2 * D_A + D_B + BLK].astype(bf16)
    v_ref[...] = z[:, 2 * D_A + D_B + BLK:].astype(bf16)


def _mix_in(x2, gmix, win, lng, lnb, ws, bs, gout_a):
    t = x2.shape[0]
    tm = TM_MIX
    const = lambda shape: pl.BlockSpec(shape, lambda i: (0,) * len(shape))
    return pl.pallas_call(
        _mix_in_body,
        grid=(t // tm,),
        out_shape=(jax.ShapeDtypeStruct((t, D_A), bf16),
                   jax.ShapeDtypeStruct((t, D_B), bf16),
                   jax.ShapeDtypeStruct((t, BLK), bf16),
                   jax.ShapeDtypeStruct((t, BLK), bf16)),
        in_specs=[pl.BlockSpec((tm, D_MODEL), lambda i: (i, 0)),
                  const((1, D_MODEL)), const((D_MODEL, D_IN)),
                  const((1, D_A)), const((1, D_A)),
                  const((A_HEADS, BLK, BLK)), const((A_HEADS, BLK, BLK)),
                  const((1, D_A))],
        out_specs=(pl.BlockSpec((tm, D_A), lambda i: (i, 0)),
                   pl.BlockSpec((tm, D_B), lambda i: (i, 0)),
                   pl.BlockSpec((tm, BLK), lambda i: (i, 0)),
                   pl.BlockSpec((tm, BLK), lambda i: (i, 0))),
        compiler_params=pltpu.CompilerParams(
            dimension_semantics=("parallel",), vmem_limit_bytes=VMEM_LIMIT),
        name="mix_in",
    )(x2, gmix, win, lng, lnb, ws, bs, gout_a)


def _attn_out_body(sink_ref, x_ref, ya_ref, q_ref, kp_ref, km_ref, kn_ref, vp_ref, vm_ref, vn_ref,
                   bias_ref, goutb_ref, wout_ref, gffn_ref, wr_ref, br_ref,
                   h_ref, m_ref, re_ref, rw_ref,
                   kf_ref, vf_ref, e_ref, yb_ref, *, seq_len):
    tq = x_ref.shape[1]
    nb = tq // BLK
    step = pl.program_id(0)
    n_tok_tiles = pl.num_programs(0) - 1
    ti = jnp.minimum(step, n_tok_tiles - 1) % (seq_len // tq)

    @pl.when(step == 0)
    def _():
        yb_ref[...] = jnp.zeros(yb_ref.shape, f32)

    ybn = _rms(yb_ref[...], goutb_ref[...]).astype(bf16)
    y = jnp.concatenate([ya_ref[0], ybn], axis=1)
    h = x_ref[0] + jnp.dot(y, wout_ref[...], preferred_element_type=f32)
    h_ref[0] = h
    m = _rms(h, gffn_ref[...])
    m_ref[...] = m.astype(bf16)

    logit_t = jnp.dot(m.astype(bf16), wr_ref[...], preferred_element_type=f32)
    logit = jnp.transpose(logit_t)[:ROUTE_ROWS] + br_ref[...]
    sub = lax.broadcasted_iota(i32, (8, tq), 0)
    lg = logit[0:8]
    mg = jnp.max(lg, axis=0, keepdims=True)
    pg_top = 1.0 / jnp.sum(jnp.exp(lg - mg), axis=0, keepdims=True)
    g_idx = jnp.min(jnp.where(lg == mg, sub, 8), axis=0, keepdims=True)
    sel = logit[8:16]
    for g in range(1, N_GROUPS):
        sel = jnp.where(g_idx == g, logit[8 + 8 * g:16 + 8 * g], sel)
    m1 = jnp.max(sel, axis=0, keepdims=True)
    i1 = jnp.min(jnp.where(sel == m1, sub, 8), axis=0, keepdims=True)
    sel2 = jnp.where(sub == i1, -jnp.inf, sel)
    m2 = jnp.max(sel2, axis=0, keepdims=True)
    i2 = jnp.min(jnp.where(sel2 == m2, sub, 8), axis=0, keepdims=True)
    r = jnp.exp(m2 - m1)
    w1 = pg_top / (1.0 + r)
    w2 = pg_top * r / (1.0 + r)
    e1 = g_idx * E_PER_GROUP + i1
    e2 = g_idx * E_PER_GROUP + i2
    re_ref[...] = jnp.where(sub == 0, e1, jnp.where(sub == 1, e2, 0))
    rw_ref[...] = jnp.where(sub == 0, w1, jnp.where(sub == 1, w2, 0.0))

    kf_ref[0:BLK] = kp_ref[0]
    kf_ref[BLK:BLK + tq] = km_ref[0]
    kf_ref[BLK + tq:] = kn_ref[0]
    vlane = lax.broadcasted_iota(i32, (BLK, BLK), 1)
    for src, r0, r1 in ((vp_ref, 0, BLK), (vn_ref, BLK + tq, tq + 2 * BLK)) + tuple(
            (vm_ref.at[:, pl.ds(c * BLK, BLK), :], BLK + c * BLK, BLK + (c + 1) * BLK)
            for c in range(nb)):
        vals = src[0].astype(f32)
        vf_ref[0, r0:r1] = jnp.where(vlane < HEAD_DIM, vals,
                                     jnp.where(vlane == HEAD_DIM, 1.0, 0.0)).astype(bf16)
        vf_ref[1, r0:r1] = jnp.where(vlane >= HEAD_DIM, vals,
                                     jnp.where(vlane == 0, 1.0, 0.0)).astype(bf16)

    row = lax.broadcasted_iota(i32, (BLK, 3 * BLK), 0)
    col = lax.broadcasted_iota(i32, (BLK, 3 * BLK), 1)
    band = jnp.abs(col - BLK - row) <= WINDOW
    lane = lax.broadcasted_iota(i32, (BLK, BLK), 1)
    low_half = lane < HEAD_DIM

    for n in range(nb):
        r0 = n * BLK
        qb = q_ref[0, pl.ds(r0, BLK), :]
        kb = kf_ref[pl.ds(r0, 3 * BLK), :]
        zero = jnp.zeros((BLK, BLK), bf16)
        lhs = []
        for kh in range(N_KV):
            for g in range(GQA):
                tile = qb[:, g * BLK:(g + 1) * BLK]
                lhs.append(jnp.where(low_half, tile, zero) if kh == 0
                           else jnp.where(low_half, zero, tile))
        lhs = jnp.concatenate(lhs, axis=0)
        s_all = lax.dot_general(lhs, kb, (((1,), (1,)), ((), ())),
                                preferred_element_type=f32)
        kpos = col + ((ti * nb + n - 1) * BLK)
        valid = band & (kpos >= 0) & (kpos < seq_len)
        sink_e = []
        for h in range(N_HEADS):
            s = jnp.where(valid, s_all[h * BLK:(h + 1) * BLK] + bias_ref[h], NEG)
            sk = sink_ref[h]
            mrow = jnp.maximum(jnp.max(s, axis=-1, keepdims=True), sk)
            e_ref[n, h * BLK:(h + 1) * BLK, :] = jnp.exp2(s - mrow).astype(bf16)
            sink_e.append(jnp.exp2(sk - mrow))
        half = GQA * BLK
        pv = [jnp.dot(e_ref[n, kh * half:(kh + 1) * half, :], vf_ref[kh, pl.ds(r0, 3 * BLK), :],
                      preferred_element_type=f32) for kh in range(N_KV)]

        def head_out(h):
            kh, g = divmod(h, GQA)
            rows = pv[kh][g * BLK:(g + 1) * BLK]
            ones_col = HEAD_DIM if kh == 0 else 0
            return rows * (1.0 / (rows[:, ones_col:ones_col + 1] + sink_e[h]))

        for g in range(GQA):
            yb_ref[pl.ds(r0, BLK), g * BLK:(g + 1) * BLK] = jnp.where(
                low_half, head_out(g), head_out(GQA + g))


def _attn_out(sink, x, ya, q, k, v, bias, goutb, wout, gffn, wr, br):
    b, s, _ = x.shape
    tq = TQ_ATT
    nb = tq // BLK
    nblk = s // BLK
    t = b * s
    nq = s // tq
    n_tok_tiles = b * nq
    att = lambda s: jnp.minimum(s, n_tok_tiles - 1)
    epi = lambda s: jnp.maximum(s - 1, 0)
    const = lambda shape: pl.BlockSpec(shape, lambda s, *_: (0,) * len(shape))
    tok_att = lambda w: pl.BlockSpec((1, tq, w), lambda s, *_: (att(s) // nq, att(s) % nq, 0))
    tok_epi = lambda w: pl.BlockSpec((1, tq, w), lambda s, *_: (epi(s) // nq, epi(s) % nq, 0))
    prev = pl.BlockSpec((1, BLK, BLK), lambda s, *_: (
        att(s) // nq, jnp.maximum((att(s) % nq) * nb - 1, 0), 0))
    nxt = pl.BlockSpec((1, BLK, BLK), lambda s, *_: (
        att(s) // nq, jnp.minimum((att(s) % nq) * nb + nb, nblk - 1), 0))
    rows = pl.BlockSpec((tq, D_MODEL), lambda s, *_: (epi(s), 0))
    lanes = lambda rows: pl.BlockSpec((rows, tq), lambda s, *_: (0, epi(s)))
    grid_spec = pltpu.PrefetchScalarGridSpec(
        num_scalar_prefetch=1,
        grid=(n_tok_tiles + 1,),
        in_specs=[tok_epi(D_MODEL), tok_epi(D_A), tok_att(D_B),
                  prev, tok_att(BLK), nxt, prev, tok_att(BLK), nxt,
                  const((N_HEADS, BLK, 3 * BLK)), const((1, D_B)), const((D_MODEL, D_MODEL)),
                  const((1, D_MODEL)), const((D_MODEL, BLK)), const((ROUTE_ROWS, tq))],
        out_specs=(tok_epi(D_MODEL), rows, lanes(8), lanes(8)),
        scratch_shapes=[pltpu.VMEM((tq + 2 * BLK, BLK), bf16),
                        pltpu.VMEM((N_KV, tq + 2 * BLK, BLK), bf16),
                        pltpu.VMEM((nb, N_HEADS * BLK, 3 * BLK), bf16),
                        pltpu.VMEM((tq, D_B), f32)])
    return pl.pallas_call(
        functools.partial(_attn_out_body, seq_len=s),
        grid_spec=grid_spec,
        out_shape=(jax.ShapeDtypeStruct((b, s, D_MODEL), f32),
                   jax.ShapeDtypeStruct((t, D_MODEL), bf16),
                   jax.ShapeDtypeStruct((8, t), i32),
                   jax.ShapeDtypeStruct((8, t), f32)),
        compiler_params=pltpu.CompilerParams(
            dimension_semantics=("arbitrary",), vmem_limit_bytes=VMEM_LIMIT),
        name="attn_out",
    )(sink, x, ya, q, k, k, k, v, v, v, bias, goutb, wout, gffn, wr, br)


def _exclusive_prefix(vals):
    a = lax.broadcasted_iota(i32, (N_EXPERTS, N_EXPERTS), 0)
    c = lax.broadcasted_iota(i32, (N_EXPERTS, N_EXPERTS), 1)
    low = jnp.where(c < a, 1.0, 0.0).astype(bf16)
    hi = (vals >> 8).astype(f32).astype(bf16)
    lo = (vals & 255).astype(f32).astype(bf16)
    return (jnp.dot(low, hi, preferred_element_type=f32) * 256.0
            + jnp.dot(low, lo, preferred_element_type=f32)).astype(i32)


def _round_up_pow2(vals, mult):
    log_m = mult.bit_length() - 1
    return ((vals + (mult - 1)) >> log_m) << log_m


def _route_pos_body(re_ref, slot_ref, runs_ref, offs_ref, tri_ref, *, rows_per_tile, tl):
    n_tok_tiles = re_ref.shape[1] // tl
    a = lax.broadcasted_iota(i32, (tl, tl), 0)
    c = lax.broadcasted_iota(i32, (tl, tl), 1)
    tri_ref[...] = jnp.where(a < c, 1.0, 0.0).astype(bf16)
    eid = lax.broadcasted_iota(i32, (N_EXPERTS, tl), 0)

    def tile_hits(j):
        cols = pl.ds(pl.multiple_of(j * tl, tl), tl)
        hit1 = eid == re_ref[0:1, cols]
        hit2 = eid == re_ref[1:2, cols]
        onehot = jnp.where(hit1 | hit2, 1.0, 0.0)
        cnt = jnp.sum(onehot, axis=1, keepdims=True).astype(i32)
        run_len = jnp.broadcast_to(_round_up_pow2(cnt, RUN_ALIGN), (N_EXPERTS, BLK))
        return cols, hit1, hit2, onehot, run_len

    total = lax.fori_loop(0, n_tok_tiles, lambda j, acc: acc + tile_hits(j)[4],
                          jnp.zeros((N_EXPERTS, BLK), i32))
    padded = _round_up_pow2(total, rows_per_tile)
    starts = _exclusive_prefix(padded)
    offs_ref[0] = starts
    offs_ref[1] = padded
    offs_ref[2] = total

    def place(j, global_off):
        cols, hit1, hit2, onehot, run_len = tile_hits(j)
        local_off = _exclusive_prefix(run_len)
        before = jnp.dot(onehot.astype(bf16), tri_ref[...], preferred_element_type=f32)
        slot = before + local_off[:, 0:1].astype(f32)
        s1 = jnp.sum(jnp.where(hit1, slot, 0.0), axis=0, keepdims=True)
        s2 = jnp.sum(jnp.where(hit2, slot, 0.0), axis=0, keepdims=True)
        sub = lax.broadcasted_iota(i32, (8, tl), 0)
        slot_ref[:, cols] = jnp.where(sub == 0, s1.astype(i32),
                                      jnp.where(sub == 1, s2.astype(i32), 0))
        runs_ref[j, 0] = local_off
        runs_ref[j, 1] = run_len
        runs_ref[j, 2] = global_off
        return global_off + run_len

    lax.fori_loop(0, n_tok_tiles, place, starts)


def _route_pos(route_e, rows_per_tile):
    t = route_e.shape[1]
    tl = TL_SORT
    return pl.pallas_call(
        functools.partial(_route_pos_body, rows_per_tile=rows_per_tile, tl=tl),
        out_shape=(jax.ShapeDtypeStruct((8, t), i32),
                   jax.ShapeDtypeStruct((t // tl, 3, N_EXPERTS, BLK), i32),
                   jax.ShapeDtypeStruct((3, N_EXPERTS, BLK), i32)),
        scratch_shapes=[pltpu.VMEM((tl, tl), bf16)],
        compiler_params=pltpu.CompilerParams(vmem_limit_bytes=VMEM_LIMIT),
        name="route_pos",
    )(route_e)


def _for_each_run_piece(runs_ref, fn):
    def per_expert(e, carry):
        local_off = runs_ref[0, 0, e]
        length = runs_ref[0, 0, N_EXPERTS + e]
        global_off = runs_ref[0, 0, 2 * N_EXPERTS + e]
        units = length >> (RUN_ALIGN.bit_length() - 1)

        def big_piece(k, inner):
            done = k * BIG_PIECE
            fn(pl.multiple_of(local_off + done, RUN_ALIGN),
               pl.multiple_of(global_off + done, RUN_ALIGN), BIG_PIECE)
            return inner

        lax.fori_loop(0, units >> SMALL_BITS, big_piece, 0)
        for b in range(SMALL_BITS):
            @pl.when(((units >> b) & 1) == 1)
            def _():
                done = ((units >> (b + 1)) << (b + 1)) * RUN_ALIGN
                fn(pl.multiple_of(local_off + done, RUN_ALIGN),
                   pl.multiple_of(global_off + done, RUN_ALIGN), RUN_ALIGN << b)
        return carry

    lax.fori_loop(0, N_EXPERTS, per_expert, 0)


def _tile_run_rows(runs_ref):
    return lax.fori_loop(0, N_EXPERTS, lambda e, acc: acc + runs_ref[0, 0, N_EXPERTS + e], 0)


def _wait_rows(total_rows, make_copy):
    units = total_rows >> (RUN_ALIGN.bit_length() - 1)
    for b in range((LB_SORT // RUN_ALIGN).bit_length()):
        @pl.when(((units >> b) & 1) == 1)
        def _():
            make_copy(RUN_ALIGN << b).wait()


def _sort_rows_body(ends_ref, padded_ref, total_ref, runs_ref, m_ref, slot_ref, rw_ref,
                    xs_ref, slot_t_ref,
                    local_ref, zero_ref, pending_ref, sem_ref, zsem_ref, *, rows_per_tile):
    tl = m_ref.shape[0]
    lb = local_ref.shape[1]
    step = pl.program_id(0)
    cur = step % 2
    n_tiles = xs_ref.shape[0] // rows_per_tile
    n_used = ends_ref[N_EXPERTS - 1] // rows_per_tile

    def zero_rows(start, rows, wait):
        cp = pltpu.make_async_copy(zero_ref.at[pl.ds(0, rows), :],
                                   xs_ref.at[pl.ds(pl.multiple_of(start, RUN_ALIGN), rows), :],
                                   zsem_ref)
        cp.wait() if wait else cp.start()

    def zero_fill(wait):
        for e in range(N_EXPERTS):
            pad_units = (padded_ref[e] - total_ref[e]) >> (RUN_ALIGN.bit_length() - 1)
            for bit in range((rows_per_tile // RUN_ALIGN).bit_length() - 1):
                @pl.when(((pad_units >> bit) & 1) == 1)
                def _():
                    done = ((pad_units >> (bit + 1)) << (bit + 1)) * RUN_ALIGN
                    zero_rows(ends_ref[e] - (padded_ref[e] - total_ref[e]) + done,
                              RUN_ALIGN << bit, wait)

        def spare(j, carry):
            zero_rows((n_used + j) * rows_per_tile, rows_per_tile, wait)
            return carry

        lax.fori_loop(0, n_tiles - n_used, spare, 0)

    @pl.when(step == 0)
    def _():
        zero_ref[...] = jnp.zeros(zero_ref.shape, bf16)
        zero_fill(wait=False)

    s1 = slot_ref[0:1, :]
    s2 = slot_ref[1:2, :]
    w1 = rw_ref[0:1, :]
    w2 = rw_ref[1:2, :]
    run_rows = _tile_run_rows(runs_ref)

    def sort_rows_upto(n):
        srow = lax.broadcasted_iota(i32, (n, tl), 0)
        p1 = srow == s1
        p2 = srow == s2
        perm = jnp.where(p1 | p2, 1.0, 0.0).astype(bf16)
        rows = jnp.dot(perm, m_ref[...], preferred_element_type=f32)
        w = jnp.sum(jnp.where(p1, w1, 0.0) + jnp.where(p2, w2, 0.0), axis=1, keepdims=True)
        w_hi = w.astype(bf16).astype(f32)
        w_lo = w - w_hi
        lane = lax.broadcasted_iota(i32, (n, BLK), 1)
        local_ref[cur, :n, :D_MODEL] = rows.astype(bf16)
        local_ref[cur, :n, D_MODEL:] = jnp.where(
            lane == 0, w_hi, jnp.where(lane == 1, w_lo, 0.0)).astype(bf16)
        sub_t = lax.broadcasted_iota(i32, (BLK, tl), 0)
        slots = jnp.where(sub_t == 0, s1.astype(f32), jnp.where(sub_t == 1, s2.astype(f32), 0.0))
        slot_t_ref[...] = jnp.transpose(slots)[:, :8].astype(i32)

    def piece(s, local_row, global_row, rows):
        return pltpu.make_async_copy(local_ref.at[s, pl.ds(local_row, rows), :],
                                     xs_ref.at[pl.ds(global_row, rows), :], sem_ref.at[s])

    short = lb - 256
    pl.when(run_rows <= short)(functools.partial(sort_rows_upto, short))
    pl.when(run_rows > short)(functools.partial(sort_rows_upto, lb))

    _for_each_run_piece(runs_ref, lambda l, g, n: piece(cur, l, g, n).start())

    @pl.when(step > 0)
    def _():
        _wait_rows(pending_ref[0], lambda n: piece(1 - cur, 0, 0, n))

    pending_ref[0] = run_rows

    @pl.when(step == pl.num_programs(0) - 1)
    def _():
        _wait_rows(pending_ref[0], lambda n: piece(cur, 0, 0, n))
        zero_fill(wait=True)


def _sort_rows(ends, padded, totals, run_tiles, m, slot, route_w, total_rows, rows_per_tile):
    t = m.shape[0]
    tl = TL_SORT
    grid_spec = pltpu.PrefetchScalarGridSpec(
        num_scalar_prefetch=3,
        grid=(t // tl,),
        in_specs=[pl.BlockSpec((1, 1, 3 * N_EXPERTS), lambda i, *_: (i, 0, 0),
                               memory_space=pltpu.SMEM),
                  pl.BlockSpec((tl, D_MODEL), lambda i, *_: (i, 0)),
                  pl.BlockSpec((8, tl), lambda i, *_: (0, i)),
                  pl.BlockSpec((8, tl), lambda i, *_: (0, i))],
        out_specs=(pl.BlockSpec(memory_space=pl.ANY),
                   pl.BlockSpec((tl, 8), lambda i, *_: (i, 0))),
        scratch_shapes=[pltpu.VMEM((2, LB_SORT, XS_COLS), bf16),
                        pltpu.VMEM((rows_per_tile, XS_COLS), bf16),
                        pltpu.SMEM((1,), i32),
                        pltpu.SemaphoreType.DMA((2,)),
                        pltpu.SemaphoreType.DMA(())])
    return pl.pallas_call(
        functools.partial(_sort_rows_body, rows_per_tile=rows_per_tile),
        grid_spec=grid_spec,
        out_shape=(jax.ShapeDtypeStruct((total_rows, XS_COLS), bf16),
                   jax.ShapeDtypeStruct((t, 8), i32)),
        compiler_params=pltpu.CompilerParams(
            dimension_semantics=("arbitrary",), vmem_limit_bytes=VMEM_LIMIT),
        name="sort_rows",
    )(ends, padded, totals, run_tiles, m, slot, route_w)


def _experts_body(tile_ref, texp_ref, first_ref, next_ref, slot_ref, nused_ref,
                  xs_hbm, wg_hbm, wu_hbm, wd_hbm, ys_hbm,
                  xbuf_ref, sg_ref, su_ref, sd_ref, wg_ref, wu_ref, wd_ref,
                  xsem_ref, osem_ref, sem_ref):
    i = pl.program_id(0)
    n_used = nused_ref[0]
    r = xbuf_ref.shape[1]

    def tile_rows(ref, tile):
        return ref.at[pl.ds(pl.multiple_of(tile * r, r), r), :]

    def fetch_tile(tile):
        s = tile % XS_RING
        return pltpu.make_async_copy(tile_rows(xs_hbm, tile), xbuf_ref.at[s], xsem_ref.at[s])

    def store_tile(tile):
        s = tile % XS_RING
        return pltpu.make_async_copy(xbuf_ref.at[s], tile_rows(ys_hbm, tile), osem_ref.at[s])

    @pl.when(i == 0)
    def _():
        for ahead in range(XS_AHEAD):
            @pl.when(ahead < n_used)
            def _():
                fetch_tile(ahead).start()

    @pl.when(i + XS_AHEAD < n_used)
    def _():
        @pl.when(i + XS_AHEAD >= XS_RING)
        def _():
            store_tile(i + XS_AHEAD - XS_RING).wait()

        fetch_tile(i + XS_AHEAD).start()

    def fetches(expert, s):
        return (pltpu.make_async_copy(wg_hbm.at[expert], sg_ref.at[s], sem_ref.at[s]),
                pltpu.make_async_copy(wu_hbm.at[expert], su_ref.at[s], sem_ref.at[s]),
                pltpu.make_async_copy(wd_hbm.at[expert], sd_ref.at[s], sem_ref.at[s]))

    @pl.when(i < n_used)
    def _():
        s = slot_ref[i]
        fetch_tile(i).wait()
        xs_ref = xbuf_ref.at[i % XS_RING]

        @pl.when(first_ref[i] == 1)
        def _():
            @pl.when(i == 0)
            def _():
                for cp in fetches(texp_ref[0], 0):
                    cp.start()

            for cp in fetches(texp_ref[i], s):
                cp.wait()

            @pl.when(next_ref[i] >= 0)
            def _():
                for cp in fetches(next_ref[i], 1 - s):
                    cp.start()

            wg_ref[...] = sg_ref[s].astype(bf16)
            wu_ref[...] = su_ref[s].astype(bf16)
            wd_ref[...] = sd_ref[s].astype(bf16)

        w_row = (xs_ref[:, D_MODEL:D_MODEL + 1].astype(f32)
                 + xs_ref[:, D_MODEL + 1:D_MODEL + 2].astype(f32))
        gate = jnp.dot(xs_ref[:, :D_MODEL], wg_ref[...], preferred_element_type=f32)
        up = jnp.dot(xs_ref[:, :D_MODEL], wu_ref[...], preferred_element_type=f32)
        hdn = (gate * jax.nn.sigmoid(gate)) * up
        y = jnp.dot(hdn.astype(bf16), wd_ref[...], preferred_element_type=f32)
        xs_ref[:, :D_MODEL] = (y * w_row).astype(bf16)
        store_tile(i).start()

    @pl.when(i == pl.num_programs(0) - 1)
    def _():
        for back in range(1, XS_RING + 1):
            @pl.when(n_used - back >= 0)
            def _():
                store_tile(n_used - back).wait()


def _experts(tile_idx, tile_exp, first, next_exp, slot, n_used, xs, wg, wu, wd, rows_per_tile):
    r = rows_per_tile
    n_tiles = xs.shape[0] // r
    grid_spec = pltpu.PrefetchScalarGridSpec(
        num_scalar_prefetch=6,
        grid=(n_tiles,),
        in_specs=[pl.BlockSpec(memory_space=pl.ANY),
                  pl.BlockSpec(memory_space=pl.ANY),
                  pl.BlockSpec(memory_space=pl.ANY),
                  pl.BlockSpec(memory_space=pl.ANY)],
        out_specs=pl.BlockSpec(memory_space=pl.ANY),
        scratch_shapes=[pltpu.VMEM((XS_RING, r, XS_COLS), bf16),
                        pltpu.VMEM((2, D_MODEL, D_FF_E), f32),
                        pltpu.VMEM((2, D_MODEL, D_FF_E), f32),
                        pltpu.VMEM((2, D_FF_E, D_MODEL), f32),
                        pltpu.VMEM((D_MODEL, D_FF_E), bf16),
                        pltpu.VMEM((D_MODEL, D_FF_E), bf16),
                        pltpu.VMEM((D_FF_E, D_MODEL), bf16),
                        pltpu.SemaphoreType.DMA((XS_RING,)),
                        pltpu.SemaphoreType.DMA((XS_RING,)),
                        pltpu.SemaphoreType.DMA((2,))])
    return pl.pallas_call(
        _experts_body,
        grid_spec=grid_spec,
        out_shape=jax.ShapeDtypeStruct(xs.shape, xs.dtype),
        input_output_aliases={6: 0},
        compiler_params=pltpu.CompilerParams(
            dimension_semantics=("arbitrary",), vmem_limit_bytes=VMEM_LIMIT),
        name="experts",
    )(tile_idx, tile_exp, first, next_exp, slot, n_used, xs, wg, wu, wd)


def _combine_out_body(runs_cur_ref, runs_nxt_ref, h_ref, slot_ref, p_ref, ys_ref,
                      wpg_ref, bpg_ref, wpp_ref, gple_ref, gfin_ref,
                      o_ref, ybuf_ref, moe_ref, sem_ref):
    tk = h_ref.shape[0]
    lb = ybuf_ref.shape[1]
    step = pl.program_id(0)
    n = pl.num_programs(0) - 1
    cur = jnp.minimum(step, n - 1) % 2

    def piece(s, local_row, global_row, rows):
        return pltpu.make_async_copy(ys_ref.at[pl.ds(global_row, rows), pl.ds(0, D_MODEL)],
                                     ybuf_ref.at[s, pl.ds(local_row, rows), :], sem_ref.at[s])

    @pl.when(step == 0)
    def _():
        ybuf_ref[...] = jnp.zeros(ybuf_ref.shape, bf16)
        moe_ref[...] = jnp.zeros(moe_ref.shape, f32)
        _for_each_run_piece(runs_cur_ref, lambda l, g, r: piece(0, l, g, r).start())

    @pl.when(step + 1 < n)
    def _():
        _for_each_run_piece(runs_nxt_ref, lambda l, g, r: piece(1 - cur, l, g, r).start())

    run_rows = _tile_run_rows(runs_cur_ref)

    @pl.when(step < n)
    def _():
        _wait_rows(run_rows, lambda r: piece(cur, 0, 0, r))

    def work(rows):
        h2 = h_ref[...] + moe_ref[...]
        gate = jax.nn.sigmoid(jnp.dot(h2.astype(bf16), wpg_ref[...], preferred_element_type=f32)
                              + bpg_ref[...])
        pp = jnp.dot(p_ref[...].astype(bf16), wpp_ref[...], preferred_element_type=f32)
        h3 = h2 + gate * _rms(pp, gple_ref[...])
        o_ref[...] = _rms(h3, gfin_ref[...])
        col = lax.broadcasted_iota(i32, (tk, rows), 1)
        unperm = jnp.where((col == slot_ref[:, 0:1]) | (col == slot_ref[:, 1:2]), 1.0, 0.0)
        moe_ref[...] = jnp.dot(unperm.astype(bf16), ybuf_ref[cur, :rows, :],
                               preferred_element_type=f32)

    short = lb - 256
    pl.when(run_rows <= short)(functools.partial(work, short))
    pl.when(run_rows > short)(functools.partial(work, lb))


def _combine_out(run_tiles, h2d, slot_t, p2d, ys, wpg, bpg, wpp, gple, gfin):
    t = h2d.shape[0]
    tk = TL_SORT
    nt = t // tk
    const = lambda shape: pl.BlockSpec(shape, lambda i: (0,) * len(shape))
    runs = lambda imap: pl.BlockSpec((1, 1, 3 * N_EXPERTS), imap, memory_space=pltpu.SMEM)
    uns = lambda s: jnp.minimum(s, nt - 1)
    fin = lambda s: jnp.maximum(s - 1, 0)
    return pl.pallas_call(
        _combine_out_body,
        grid=(nt + 1,),
        out_shape=jax.ShapeDtypeStruct((t, D_MODEL), f32),
        in_specs=[runs(lambda s: (uns(s), 0, 0)),
                  runs(lambda s: (uns(s + 1), 0, 0)),
                  pl.BlockSpec((tk, D_MODEL), lambda s: (fin(s), 0)),
                  pl.BlockSpec((tk, 8), lambda s: (uns(s), 0)),
                  pl.BlockSpec((tk, PLE_DIM), lambda s: (fin(s), 0)),
                  pl.BlockSpec(memory_space=pl.ANY),
                  const((D_MODEL, D_MODEL)), const((1, D_MODEL)),
                  const((PLE_DIM, D_MODEL)), const((1, D_MODEL)), const((1, D_MODEL))],
        out_specs=pl.BlockSpec((tk, D_MODEL), lambda s: (fin(s), 0)),
        scratch_shapes=[pltpu.VMEM((2, LB_SORT, D_MODEL), bf16),
                        pltpu.VMEM((tk, D_MODEL), f32),
                        pltpu.SemaphoreType.DMA((2,))],
        compiler_params=pltpu.CompilerParams(
            dimension_semantics=("arbitrary",), vmem_limit_bytes=VMEM_LIMIT),
        name="combine_out",
    )(run_tiles, run_tiles, h2d, slot_t, p2d, ys, wpg, bpg, wpp, gple, gfin)


def _head_major_to_tile_major(a, axis):
    axis = axis % a.ndim
    shape = a.shape
    a = a.reshape(shape[:axis] + (N_KV, GQA, HEAD_DIM) + shape[axis + 1:])
    return jnp.swapaxes(a, axis, axis + 1).reshape(shape)


def kernel(x, p, rel_bias, g_mix, w_in, ln_v_g, ln_v_b, w_spatial, b_spatial, sink, g_out_grp, w_out,
           g_ffn, w_router_group, b_router_group, w_router_expert, b_router_expert, w_gate_e, w_up_e,
           w_down_e, w_ple_proj, g_ple, w_ple_gate, b_ple_gate, g_final):
    b, s, d = x.shape
    t = b * s
    depth = g_mix.shape[0]
    assert depth == 1 and d == D_MODEL
    c1, c2 = 2 * D_A, 2 * D_A + D_B
    bias = _bias_table(rel_bias * LOG2E)
    hcur = x.astype(f32)
    for li in range(depth):
        wi = w_in[li]
        wq = _head_major_to_tile_major(wi[:, c1:c2], 1) * (LOG2E * HEAD_DIM ** -0.5)
        win = jnp.concatenate([wi[:, :c1], wq, wi[:, c2:]], axis=1).astype(bf16)
        bs = jnp.broadcast_to(b_spatial[li][:, :, None], (A_HEADS, BLK, BLK)).astype(f32)
        gout = g_out_grp[li]
        gout_b = _head_major_to_tile_major(gout[D_A:], 0)[None]
        wo = w_out[li]
        wout = jnp.concatenate([wo[:D_A], _head_major_to_tile_major(wo[D_A:], 0)],
                               axis=0).astype(bf16)
        wr = jnp.concatenate([
            w_router_group[li], jnp.zeros((D_MODEL, 8 - N_GROUPS), f32), w_router_expert[li],
            jnp.zeros((D_MODEL, BLK - ROUTE_ROWS), f32)], axis=1).astype(bf16)
        br = jnp.concatenate([b_router_group[li], jnp.full((8 - N_GROUPS,), NEG, f32),
                              b_router_expert[li]])
        br = jnp.broadcast_to(br[:, None], (ROUTE_ROWS, TQ_ATT))

        ya, q, k, v = _mix_in(hcur.reshape(t, d), g_mix[li][None], win, ln_v_g[li][None],
                              ln_v_b[li][None], w_spatial[li].astype(bf16), bs, gout[None, :D_A])
        h, m_rows, route_e, route_w = _attn_out(
            sink[li].astype(f32) * LOG2E, hcur, ya.reshape(b, s, D_A), q.reshape(b, s, D_B),
            k.reshape(b, s, BLK), v.reshape(b, s, BLK), bias, gout_b, wout,
            g_ffn[li][None], wr, br)

        r = R_EXP
        n_sort = t // TL_SORT
        max_rows = 2 * t + n_sort * N_EXPERTS * (RUN_ALIGN - 1)
        n_tiles = -(-max_rows // r) + N_EXPERTS
        slot, runs, offs = _route_pos(route_e, r)
        starts = offs[0, :, 0]
        padded = offs[1, :, 0]
        ends = starts + padded
        n_used = ends[-1] // r
        tile_idx = jnp.minimum(jnp.arange(n_tiles, dtype=i32), n_used - 1)
        tile_exp = jnp.minimum(
            jnp.sum((ends[None, :] <= (tile_idx * r)[:, None]).astype(i32), axis=1), N_EXPERTS - 1)
        tile_exp = tile_exp.astype(i32)
        first = jnp.concatenate([jnp.ones((1,), i32),
                                 (tile_exp[1:] != tile_exp[:-1]).astype(i32)])
        next_tile = ends[tile_exp] // r
        next_exp = jnp.where(next_tile < n_used, tile_exp[jnp.minimum(next_tile, n_tiles - 1)], -1)
        fetch_slot = (jnp.cumsum(first) - 1) % 2
        run_tiles = runs[:, :, :, 0].reshape(n_sort, 1, 3 * N_EXPERTS)
        xs, slot_t = _sort_rows(ends, padded, offs[2, :, 0], run_tiles, m_rows, slot, route_w,
                                n_tiles * r, r)
        ys = _experts(tile_idx, tile_exp, first, next_exp.astype(i32), fetch_slot.astype(i32),
                      n_used.reshape(1), xs, w_gate_e[li], w_up_e[li], w_down_e[li], r)

        out = _combine_out(run_tiles, h.reshape(t, d), slot_t,
                           p[li].reshape(t, PLE_DIM), ys, w_ple_gate[li].astype(bf16),
                           b_ple_gate[li][None], w_ple_proj[li].astype(bf16), g_ple[li][None],
                           g_final[None])
        hcur = out.reshape(b, s, d)
    return hcur
```

```python
import functools
import math

import jax
import jax.numpy as jnp
import numpy as np
from jax import lax
from jax.experimental import pallas as pl
from jax.experimental.pallas import tpu as pltpu

D_MODEL = 1024
D_A = 512
D_B = 512
BLK = 128
A_HEADS = 4
HEAD_DIM = 64
N_HEADS = 8
N_KV = 2
GQA = 4
WINDOW = 128
NUM_BUCKETS = 32
MAX_DIST = 128
D_IN = 2 * D_A + D_B + 2 * N_KV * HEAD_DIM
N_GROUPS = 4
E_PER_GROUP = 8
N_EXPERTS = 32
D_FF_E = 256
PLE_DIM = 256
EPS = 1e-6
NEG = -1e30
LOG2E = math.log2(math.e)

TM_MIX = 512
TQ_ATT = 512
TL_SORT = 512
RUN_ALIGN = 16
SMALL_BITS = 2
BIG_PIECE = RUN_ALIGN << SMALL_BITS
LB_SORT = 2 * TL_SORT + N_EXPERTS * RUN_ALIGN
R_EXP = 512
XS_COLS = D_MODEL + BLK
XS_RING = 4
XS_AHEAD = 2
ROUTE_ROWS = 8 + N_EXPERTS
VMEM_LIMIT = 48 * 1024 * 1024

f32 = jnp.float32
bf16 = jnp.bfloat16
i32 = jnp.int32


def _rms(x, g):
    return x * lax.rsqrt(jnp.mean(x * x, axis=-1, keepdims=True) + EPS) * g


def _gelu_tanh(x):
    c = math.sqrt(2.0 / math.pi)
    return x * (0.5 * (1.0 + jnp.tanh(c * (x + 0.044715 * (x * x * x)))))


def _bucket_table():
    n = NUM_BUCKETS // 2
    max_exact = n // 2
    i = np.arange(BLK)[:, None]
    j = np.arange(3 * BLK)[None, :]
    rel = j - BLK - i
    ret = np.where(rel > 0, n, 0)
    a = np.abs(rel)
    large = max_exact + (np.log(np.maximum(a, 1).astype(np.float64) / max_exact)
                         / math.log(MAX_DIST / max_exact) * (n - max_exact)).astype(np.int32)
    large = np.minimum(large, n - 1)
    return (ret + np.where(a < max_exact, a, large)).astype(np.int32)


def _bias_body(rb_ref, bucket_ref, o_ref):
    rows = 16
    for r0 in range(0, BLK, rows):
        bucket = bucket_ref[r0:r0 + rows, :]
        acc = [jnp.zeros(bucket.shape, f32) for _ in range(N_HEADS)]
        for b in range(NUM_BUCKETS):
            hit = bucket == b
            acc = [jnp.where(hit, rb_ref[b, h], acc[h]) for h in range(N_HEADS)]
        for h in range(N_HEADS):
            o_ref[h, r0:r0 + rows, :] = acc[h]


def _bias_table(rel_bias):
    bucket = jnp.asarray(_bucket_table())
    return pl.pallas_call(
        _bias_body,
        out_shape=jax.ShapeDtypeStruct((N_HEADS, BLK, 3 * BLK), f32),
        in_specs=[pl.BlockSpec(memory_space=pltpu.SMEM),
                  pl.BlockSpec((BLK, 3 * BLK), lambda: (0, 0))],
        out_specs=pl.BlockSpec((N_HEADS, BLK, 3 * BLK), lambda: (0, 0, 0)),
        name="bias_table",
    )(rel_bias.astype(f32), bucket)


def _mix_in_body(x_ref, gmix_ref, win_ref, lng_ref, lnb_ref, ws_ref, bs_ref, gout_ref,
                 ya_ref, q_ref, k_ref, v_ref):
    tm = x_ref.shape[0]
    nc = tm // BLK
    a = _rms(x_ref[...], gmix_ref[...])
    z = jnp.dot(a.astype(bf16), win_ref[...], preferred_element_type=f32)
    uv = _gelu_tanh(z[:, :2 * D_A])
    u = uv[:, :D_A]
    v = uv[:, D_A:]
    mu = jnp.mean(v, axis=-1, keepdims=True)
    vc = v - mu
    var = jnp.mean(vc * vc, axis=-1, keepdims=True)
    vn = (vc * lax.rsqrt(var + EPS) * lng_ref[...] + lnb_ref[...]).astype(bf16)
    cols = []
    for h in range(A_HEADS):
        rhs = jnp.concatenate(
            [vn[c * BLK:(c + 1) * BLK, h * BLK:(h + 1) * BLK] for c in range(nc)], axis=1)
        r = jnp.dot(ws_ref[h], rhs, preferred_element_type=f32)
        cols.append(jnp.concatenate(
            [r[:, c * BLK:(c + 1) * BLK] + bs_ref[h] for c in range(nc)], axis=0))
    sv = jnp.concatenate(cols, axis=1)
    ya_ref[...] = _rms(u * sv, gout_ref[...]).astype(bf16)
    qt = [z[:, 2 * D_A + j * BLK:2 * D_A + (j + 1) * BLK] for j in range(D_B // BLK)]
    low_half = lax.broadcasted_iota(i32, (tm, BLK), 1) < HEAD_DIM
    for g in range(GQA):
        a0, a1 = qt[g // 2], qt[GQA // 2 + g // 2]
        tile = (jnp.where(low_half, a0, pltpu.roll(a1, HEAD_DIM, 1)) if g % 2 == 0
                else jnp.where(low_half, pltpu.roll(a0, HEAD_DIM, 1), a1))
        q_ref[:, g * BLK:(g + 1) * BLK] = tile.astype(bf16)
    k_ref[...] = z[:, 2 * D_A + D_B:2 * D_A + D_B + BLK].astype(bf16)
    v_ref[...] = z[:, 2 * D_A + D_B + BLK:].astype(bf16)


def _mix_in(x2, gmix, win, lng, lnb, ws, bs, gout_a):
    t = x2.shape[0]
    tm = TM_MIX
    const = lambda shape: pl.BlockSpec(shape, lambda i: (0,) * len(shape))
    return pl.pallas_call(
        _mix_in_body,
        grid=(t // tm,),
        out_shape=(jax.ShapeDtypeStruct((t, D_A), bf16),
                   jax.ShapeDtypeStruct((t, D_B), bf16),
                   jax.ShapeDtypeStruct((t, BLK), bf16),
                   jax.ShapeDtypeStruct((t, BLK), bf16)),
        in_specs=[pl.BlockSpec((tm, D_MODEL), lambda i: (i, 0)),
                  const((1, D_MODEL)), const((D_MODEL, D_IN)),
                  const((1, D_A)), const((1, D_A)),
                  const((A_HEADS, BLK, BLK)), const((A_HEADS, BLK, BLK)),
                  const((1, D_A))],
        out_specs=(pl.BlockSpec((tm, D_A), lambda i: (i, 0)),
                   pl.BlockSpec((tm, D_B), lambda i: (i, 0)),
                   pl.BlockSpec((tm, BLK), lambda i: (i, 0)),
                   pl.BlockSpec((tm, BLK), lambda i: (i, 0))),
        compiler_params=pltpu.CompilerParams(
            dimension_semantics=("parallel",), vmem_limit_bytes=VMEM_LIMIT),
        name="mix_in",
    )(x2, gmix, win, lng, lnb, ws, bs, gout_a)


def _attn_out_body(sink_ref, x_ref, ya_ref, q_ref, kp_ref, km_ref, kn_ref, vp_ref, vm_ref, vn_ref,
                   bias_ref, goutb_ref, wout_ref, gffn_ref, wr_ref, br_ref,
                   h_ref, m_ref, re_ref, rw_ref,
                   kf_ref, vf_ref, e_ref, yb_ref, *, seq_len):
    tq = x_ref.shape[1]
    nb = tq // BLK
    step = pl.program_id(0)
    n_tok_tiles = pl.num_programs(0) - 1
    ti = jnp.minimum(step, n_tok_tiles - 1) % (seq_len // tq)

    @pl.when(step == 0)
    def _():
        yb_ref[1] = jnp.zeros(yb_ref.shape[1:], f32)

    ybn = _rms(yb_ref[1], goutb_ref[...]).astype(bf16)
    y = jnp.concatenate([ya_ref[0], ybn], axis=1)
    h = x_ref[0] + jnp.dot(y, wout_ref[...], preferred_element_type=f32)
    h_ref[0] = h
    m = _rms(h, gffn_ref[...])
    m_ref[...] = m.astype(bf16)

    logit_t = jnp.dot(m.astype(bf16), wr_ref[...], preferred_element_type=f32)
    logit = jnp.transpose(logit_t)[:ROUTE_ROWS] + br_ref[...]
    sub = lax.broadcasted_iota(i32, (8, tq), 0)
    lg = logit[0:8]
    mg = jnp.max(lg, axis=0, keepdims=True)
    pg_top = 1.0 / jnp.sum(jnp.exp(lg - mg), axis=0, keepdims=True)
    g_idx = jnp.min(jnp.where(lg == mg, sub, 8), axis=0, keepdims=True)
    sel = logit[8:16]
    for g in range(1, N_GROUPS):
        sel = jnp.where(g_idx == g, logit[8 + 8 * g:16 + 8 * g], sel)
    m1 = jnp.max(sel, axis=0, keepdims=True)
    i1 = jnp.min(jnp.where(sel == m1, sub, 8), axis=0, keepdims=True)
    sel2 = jnp.where(sub == i1, -jnp.inf, sel)
    m2 = jnp.max(sel2, axis=0, keepdims=True)
    i2 = jnp.min(jnp.where(sel2 == m2, sub, 8), axis=0, keepdims=True)
    r = jnp.exp(m2 - m1)
    w1 = pg_top / (1.0 + r)
    w2 = pg_top * r / (1.0 + r)
    e1 = g_idx * E_PER_GROUP + i1
    e2 = g_idx * E_PER_GROUP + i2
    re_ref[...] = jnp.where(sub == 0, e1, jnp.where(sub == 1, e2, 0))
    rw_ref[...] = jnp.where(sub == 0, w1, jnp.where(sub == 1, w2, 0.0))

    kf_ref[0:BLK] = kp_ref[0]
    kf_ref[BLK:BLK + tq] = km_ref[0]
    kf_ref[BLK + tq:] = kn_ref[0]
    vlane = lax.broadcasted_iota(i32, (BLK, BLK), 1)
    for src, r0, r1 in ((vp_ref, 0, BLK), (vn_ref, BLK + tq, tq + 2 * BLK)) + tuple(
            (vm_ref.at[:, pl.ds(c * BLK, BLK), :], BLK + c * BLK, BLK + (c + 1) * BLK)
            for c in range(nb)):
        vals = src[0].astype(f32)
        vf_ref[0, r0:r1] = jnp.where(vlane < HEAD_DIM, vals,
                                     jnp.where(vlane == HEAD_DIM, 1.0, 0.0)).astype(bf16)
        vf_ref[1, r0:r1] = jnp.where(vlane >= HEAD_DIM, vals,
                                     jnp.where(vlane == 0, 1.0, 0.0)).astype(bf16)

    row = lax.broadcasted_iota(i32, (BLK, 3 * BLK), 0)
    col = lax.broadcasted_iota(i32, (BLK, 3 * BLK), 1)
    band = jnp.abs(col - BLK - row) <= WINDOW
    lane = lax.broadcasted_iota(i32, (BLK, BLK), 1)
    low_half = lane < HEAD_DIM

    for n in range(nb):
        r0 = n * BLK
        qb = q_ref[0, pl.ds(r0, BLK), :]
        kb = kf_ref[pl.ds(r0, 3 * BLK), :]
        zero = jnp.zeros((BLK, BLK), bf16)
        lhs = []
        for kh in range(N_KV):
            for g in range(GQA):
                tile = qb[:, g * BLK:(g + 1) * BLK]
                lhs.append(jnp.where(low_half, tile, zero) if kh == 0
                           else jnp.where(low_half, zero, tile))
        lhs = jnp.concatenate(lhs, axis=0)
        s_all = lax.dot_general(lhs, kb, (((1,), (1,)), ((), ())),
                                preferred_element_type=f32)
        kpos = col + ((ti * nb + n - 1) * BLK)
        valid = band & (kpos >= 0) & (kpos < seq_len)
        sink_e = []
        for h in range(N_HEADS):
            s = jnp.where(valid, s_all[h * BLK:(h + 1) * BLK] + bias_ref[h], NEG)
            sk = sink_ref[h]
            mrow = jnp.maximum(jnp.max(s, axis=-1, keepdims=True), sk)
            e_ref[n, h * BLK:(h + 1) * BLK, :] = jnp.exp2(s - mrow).astype(bf16)
            sink_e.append(jnp.exp2(sk - mrow))
        half = GQA * BLK
        pv = [jnp.dot(e_ref[n, kh * half:(kh + 1) * half, :], vf_ref[kh, pl.ds(r0, 3 * BLK), :],
                      preferred_element_type=f32) for kh in range(N_KV)]

        def head_out(h):
            kh, g = divmod(h, GQA)
            rows = pv[kh][g * BLK:(g + 1) * BLK]
            ones_col = HEAD_DIM if kh == 0 else 0
            return rows * (1.0 / (rows[:, ones_col:ones_col + 1] + sink_e[h]))

        for g in range(GQA):
            yb_ref[0, pl.ds(r0, BLK), g * BLK:(g + 1) * BLK] = jnp.where(
                low_half, head_out(g), head_out(GQA + g))

    yb_ref[1] = yb_ref[0]


def _attn_out(sink, x, ya, q, k, v, bias, goutb, wout, gffn, wr, br):
    b, s, _ = x.shape
    tq = TQ_ATT
    nb = tq // BLK
    nblk = s // BLK
    t = b * s
    nq = s // tq
    n_tok_tiles = b * nq
    att = lambda s: jnp.minimum(s, n_tok_tiles - 1)
    epi = lambda s: jnp.maximum(s - 1, 0)
    const = lambda shape: pl.BlockSpec(shape, lambda s, *_: (0,) * len(shape))
    tok_att = lambda w: pl.BlockSpec((1, tq, w), lambda s, *_: (att(s) // nq, att(s) % nq, 0))
    tok_epi = lambda w: pl.BlockSpec((1, tq, w), lambda s, *_: (epi(s) // nq, epi(s) % nq, 0))
    prev = pl.BlockSpec((1, BLK, BLK), lambda s, *_: (
        att(s) // nq, jnp.maximum((att(s) % nq) * nb - 1, 0), 0))
    nxt = pl.BlockSpec((1, BLK, BLK), lambda s, *_: (
        att(s) // nq, jnp.minimum((att(s) % nq) * nb + nb, nblk - 1), 0))
    rows = pl.BlockSpec((tq, D_MODEL), lambda s, *_: (epi(s), 0))
    lanes = lambda rows: pl.BlockSpec((rows, tq), lambda s, *_: (0, epi(s)))
    grid_spec = pltpu.PrefetchScalarGridSpec(
        num_scalar_prefetch=1,
        grid=(n_tok_tiles + 1,),
        in_specs=[tok_epi(D_MODEL), tok_epi(D_A), tok_att(D_B),
                  prev, tok_att(BLK), nxt, prev, tok_att(BLK), nxt,
                  const((N_HEADS, BLK, 3 * BLK)), const((1, D_B)), const((D_MODEL, D_MODEL)),
                  const((1, D_MODEL)), const((D_MODEL, BLK)), const((ROUTE_ROWS, tq))],
        out_specs=(tok_epi(D_MODEL), rows, lanes(8), lanes(8)),
        scratch_shapes=[pltpu.VMEM((tq + 2 * BLK, BLK), bf16),
                        pltpu.VMEM((N_KV, tq + 2 * BLK, BLK), bf16),
                        pltpu.VMEM((nb, N_HEADS * BLK, 3 * BLK), bf16),
                        pltpu.VMEM((2, tq, D_B), f32)])
    return pl.pallas_call(
        functools.partial(_attn_out_body, seq_len=s),
        grid_spec=grid_spec,
        out_shape=(jax.ShapeDtypeStruct((b, s, D_MODEL), f32),
                   jax.ShapeDtypeStruct((t, D_MODEL), bf16),
                   jax.ShapeDtypeStruct((8, t), i32),
                   jax.ShapeDtypeStruct((8, t), f32)),
        compiler_params=pltpu.CompilerParams(
            dimension_semantics=("arbitrary",), vmem_limit_bytes=VMEM_LIMIT),
        name="attn_out",
    )(sink, x, ya, q, k, k, k, v, v, v, bias, goutb, wout, gffn, wr, br)


def _exclusive_prefix(vals):
    a = lax.broadcasted_iota(i32, (N_EXPERTS, N_EXPERTS), 0)
    c = lax.broadcasted_iota(i32, (N_EXPERTS, N_EXPERTS), 1)
    low = jnp.where(c < a, 1.0, 0.0).astype(bf16)
    hi = (vals >> 8).astype(f32).astype(bf16)
    lo = (vals & 255).astype(f32).astype(bf16)
    return (jnp.dot(low, hi, preferred_element_type=f32) * 256.0
            + jnp.dot(low, lo, preferred_element_type=f32)).astype(i32)


def _round_up_pow2(vals, mult):
    log_m = mult.bit_length() - 1
    return ((vals + (mult - 1)) >> log_m) << log_m


def _route_pos_body(re_ref, slot_ref, runs_ref, offs_ref, tri_ref, *, rows_per_tile, tl):
    n_tok_tiles = re_ref.shape[1] // tl
    a = lax.broadcasted_iota(i32, (tl, tl), 0)
    c = lax.broadcasted_iota(i32, (tl, tl), 1)
    tri_ref[...] = jnp.where(a < c, 1.0, 0.0).astype(bf16)
    eid = lax.broadcasted_iota(i32, (N_EXPERTS, tl), 0)

    def tile_hits(j):
        cols = pl.ds(pl.multiple_of(j * tl, tl), tl)
        hit1 = eid == re_ref[0:1, cols]
        hit2 = eid == re_ref[1:2, cols]
        onehot = jnp.where(hit1 | hit2, 1.0, 0.0)
        cnt = jnp.sum(onehot, axis=1, keepdims=True).astype(i32)
        run_len = jnp.broadcast_to(_round_up_pow2(cnt, RUN_ALIGN), (N_EXPERTS, BLK))
        return cols, hit1, hit2, onehot, run_len

    total = lax.fori_loop(0, n_tok_tiles, lambda j, acc: acc + tile_hits(j)[4],
                          jnp.zeros((N_EXPERTS, BLK), i32))
    padded = _round_up_pow2(total, rows_per_tile)
    starts = _exclusive_prefix(padded)
    offs_ref[0] = starts
    offs_ref[1] = padded
    offs_ref[2] = total

    def place(j, global_off):
        cols, hit1, hit2, onehot, run_len = tile_hits(j)
        local_off = _exclusive_prefix(run_len)
        before = jnp.dot(onehot.astype(bf16), tri_ref[...], preferred_element_type=f32)
        slot = before + local_off[:, 0:1].astype(f32)
        s1 = jnp.sum(jnp.where(hit1, slot, 0.0), axis=0, keepdims=True)
        s2 = jnp.sum(jnp.where(hit2, slot, 0.0), axis=0, keepdims=True)
        sub = lax.broadcasted_iota(i32, (8, tl), 0)
        slot_ref[:, cols] = jnp.where(sub == 0, s1.astype(i32),
                                      jnp.where(sub == 1, s2.astype(i32), 0))
        runs_ref[j, 0] = local_off
        runs_ref[j, 1] = run_len
        runs_ref[j, 2] = global_off
        return global_off + run_len

    lax.fori_loop(0, n_tok_tiles, place, starts)


def _route_pos(route_e, rows_per_tile):
    t = route_e.shape[1]
    tl = TL_SORT
    return pl.pallas_call(
        functools.partial(_route_pos_body, rows_per_tile=rows_per_tile, tl=tl),
        out_shape=(jax.ShapeDtypeStruct((8, t), i32),
                   jax.ShapeDtypeStruct((t // tl, 3, N_EXPERTS, BLK), i32),
                   jax.ShapeDtypeStruct((3, N_EXPERTS, BLK), i32)),
        scratch_shapes=[pltpu.VMEM((tl, tl), bf16)],
        compiler_params=pltpu.CompilerParams(vmem_limit_bytes=VMEM_LIMIT),
        name="route_pos",
    )(route_e)


def _for_each_run_piece(runs_ref, fn):
    def per_expert(e, carry):
        local_off = runs_ref[0, 0, e]
        length = runs_ref[0, 0, N_EXPERTS + e]
        global_off = runs_ref[0, 0, 2 * N_EXPERTS + e]
        units = length >> (RUN_ALIGN.bit_length() - 1)

        def big_piece(k, inner):
            done = k * BIG_PIECE
            fn(pl.multiple_of(local_off + done, RUN_ALIGN),
               pl.multiple_of(global_off + done, RUN_ALIGN), BIG_PIECE)
            return inner

        lax.fori_loop(0, units >> SMALL_BITS, big_piece, 0)
        for b in range(SMALL_BITS):
            @pl.when(((units >> b) & 1) == 1)
            def _():
                done = ((units >> (b + 1)) << (b + 1)) * RUN_ALIGN
                fn(pl.multiple_of(local_off + done, RUN_ALIGN),
                   pl.multiple_of(global_off + done, RUN_ALIGN), RUN_ALIGN << b)
        return carry

    lax.fori_loop(0, N_EXPERTS, per_expert, 0)


def _tile_run_rows(runs_ref):
    return lax.fori_loop(0, N_EXPERTS, lambda e, acc: acc + runs_ref[0, 0, N_EXPERTS + e], 0)


def _wait_rows(total_rows, make_copy):
    units = total_rows >> (RUN_ALIGN.bit_length() - 1)
    for b in range((LB_SORT // RUN_ALIGN).bit_length()):
        @pl.when(((units >> b) & 1) == 1)
        def _():
            make_copy(RUN_ALIGN << b).wait()


def _sort_rows_body(ends_ref, padded_ref, total_ref, runs_ref, m_ref, slot_ref, rw_ref,
                    xs_ref, slot_t_ref,
                    local_ref, zero_ref, pending_ref, sem_ref, zsem_ref, *, rows_per_tile):
    tl = m_ref.shape[0]
    lb = local_ref.shape[1]
    step = pl.program_id(0)
    cur = step % 2
    n_tiles = xs_ref.shape[0] // rows_per_tile
    n_used = ends_ref[N_EXPERTS - 1] // rows_per_tile

    def zero_rows(start, rows, wait):
        cp = pltpu.make_async_copy(zero_ref.at[pl.ds(0, rows), :],
                                   xs_ref.at[pl.ds(pl.multiple_of(start, RUN_ALIGN), rows), :],
                                   zsem_ref)
        cp.wait() if wait else cp.start()

    def zero_fill(wait):
        for e in range(N_EXPERTS):
            pad_units = (padded_ref[e] - total_ref[e]) >> (RUN_ALIGN.bit_length() - 1)
            for bit in range((rows_per_tile // RUN_ALIGN).bit_length() - 1):
                @pl.when(((pad_units >> bit) & 1) == 1)
                def _():
                    done = ((pad_units >> (bit + 1)) << (bit + 1)) * RUN_ALIGN
                    zero_rows(ends_ref[e] - (padded_ref[e] - total_ref[e]) + done,
                              RUN_ALIGN << bit, wait)

        def spare(j, carry):
            zero_rows((n_used + j) * rows_per_tile, rows_per_tile, wait)
            return carry

        lax.fori_loop(0, n_tiles - n_used, spare, 0)

    @pl.when(step == 0)
    def _():
        zero_ref[...] = jnp.zeros(zero_ref.shape, bf16)
        zero_fill(wait=False)

    s1 = slot_ref[0:1, :]
    s2 = slot_ref[1:2, :]
    w1 = rw_ref[0:1, :]
    w2 = rw_ref[1:2, :]
    run_rows = _tile_run_rows(runs_ref)

    def sort_rows_upto(n):
        srow = lax.broadcasted_iota(i32, (n, tl), 0)
        p1 = srow == s1
        p2 = srow == s2
        perm = jnp.where(p1 | p2, 1.0, 0.0).astype(bf16)
        rows = jnp.dot(perm, m_ref[...], preferred_element_type=f32)
        w = jnp.sum(jnp.where(p1, w1, 0.0) + jnp.where(p2, w2, 0.0), axis=1, keepdims=True)
        w_hi = w.astype(bf16).astype(f32)
        w_lo = w - w_hi
        lane = lax.broadcasted_iota(i32, (n, BLK), 1)
        local_ref[cur, :n, :D_MODEL] = rows.astype(bf16)
        local_ref[cur, :n, D_MODEL:] = jnp.where(
            lane == 0, w_hi, jnp.where(lane == 1, w_lo, 0.0)).astype(bf16)
        sub_t = lax.broadcasted_iota(i32, (BLK, tl), 0)
        slots = jnp.where(sub_t == 0, s1.astype(f32), jnp.where(sub_t == 1, s2.astype(f32), 0.0))
        slot_t_ref[...] = jnp.transpose(slots)[:, :8].astype(i32)

    def piece(s, local_row, global_row, rows):
        return pltpu.make_async_copy(local_ref.at[s, pl.ds(local_row, rows), :],
                                     xs_ref.at[pl.ds(global_row, rows), :], sem_ref.at[s])

    short = lb - 256
    pl.when(run_rows <= short)(functools.partial(sort_rows_upto, short))
    pl.when(run_rows > short)(functools.partial(sort_rows_upto, lb))

    _for_each_run_piece(runs_ref, lambda l, g, n: piece(cur, l, g, n).start())

    @pl.when(step > 0)
    def _():
        _wait_rows(pending_ref[0], lambda n: piece(1 - cur, 0, 0, n))

    pending_ref[0] = run_rows

    @pl.when(step == pl.num_programs(0) - 1)
    def _():
        _wait_rows(pending_ref[0], lambda n: piece(cur, 0, 0, n))
        zero_fill(wait=True)


def _sort_rows(ends, padded, totals, run_tiles, m, slot, route_w, total_rows, rows_per_tile):
    t = m.shape[0]
    tl = TL_SORT
    grid_spec = pltpu.PrefetchScalarGridSpec(
        num_scalar_prefetch=3,
        grid=(t // tl,),
        in_specs=[pl.BlockSpec((1, 1, 3 * N_EXPERTS), lambda i, *_: (i, 0, 0),
                               memory_space=pltpu.SMEM),
                  pl.BlockSpec((tl, D_MODEL), lambda i, *_: (i, 0)),
                  pl.BlockSpec((8, tl), lambda i, *_: (0, i)),
                  pl.BlockSpec((8, tl), lambda i, *_: (0, i))],
        out_specs=(pl.BlockSpec(memory_space=pl.ANY),
                   pl.BlockSpec((tl, 8), lambda i, *_: (i, 0))),
        scratch_shapes=[pltpu.VMEM((2, LB_SORT, XS_COLS), bf16),
                        pltpu.VMEM((rows_per_tile, XS_COLS), bf16),
                        pltpu.SMEM((1,), i32),
                        pltpu.SemaphoreType.DMA((2,)),
                        pltpu.SemaphoreType.DMA(())])
    return pl.pallas_call(
        functools.partial(_sort_rows_body, rows_per_tile=rows_per_tile),
        grid_spec=grid_spec,
        out_shape=(jax.ShapeDtypeStruct((total_rows, XS_COLS), bf16),
                   jax.ShapeDtypeStruct((t, 8), i32)),
        compiler_params=pltpu.CompilerParams(
            dimension_semantics=("arbitrary",), vmem_limit_bytes=VMEM_LIMIT),
        name="sort_rows",
    )(ends, padded, totals, run_tiles, m, slot, route_w)


def _experts_body(texp_ref, first_ref, next_ref, slot_ref, nused_ref,
                  xs_hbm, wg_hbm, wu_hbm, wd_hbm, ys_hbm,
                  xbuf_ref, sg_ref, su_ref, sd_ref, wg_ref, wu_ref, wd_ref,
                  xsem_ref, osem_ref, sem_ref):
    i = pl.program_id(0)
    n_used = nused_ref[0]
    r = xbuf_ref.shape[1]

    def tile_rows(ref, tile):
        return ref.at[pl.ds(pl.multiple_of(tile * r, r), r), :]

    def fetch_tile(tile):
        s = tile % XS_RING
        return pltpu.make_async_copy(tile_rows(xs_hbm, tile), xbuf_ref.at[s], xsem_ref.at[s])

    def store_tile(tile):
        s = tile % XS_RING
        return pltpu.make_async_copy(xbuf_ref.at[s], tile_rows(ys_hbm, tile), osem_ref.at[s])

    @pl.when(i == 0)
    def _():
        for ahead in range(XS_AHEAD):
            @pl.when(ahead < n_used)
            def _():
                fetch_tile(ahead).start()

    @pl.when(i + XS_AHEAD < n_used)
    def _():
        @pl.when(i + XS_AHEAD >= XS_RING)
        def _():
            store_tile(i + XS_AHEAD - XS_RING).wait()

        fetch_tile(i + XS_AHEAD).start()

    def fetches(expert, s):
        return (pltpu.make_async_copy(wg_hbm.at[expert], sg_ref.at[s], sem_ref.at[s]),
                pltpu.make_async_copy(wu_hbm.at[expert], su_ref.at[s], sem_ref.at[s]),
                pltpu.make_async_copy(wd_hbm.at[expert], sd_ref.at[s], sem_ref.at[s]))

    @pl.when(i < n_used)
    def _():
        s = slot_ref[i]
        fetch_tile(i).wait()
        xs_ref = xbuf_ref.at[i % XS_RING]

        @pl.when(first_ref[i] == 1)
        def _():
            @pl.when(i == 0)
            def _():
                for cp in fetches(texp_ref[0], 0):
                    cp.start()

            for cp in fetches(texp_ref[i], s):
                cp.wait()

            @pl.when(next_ref[i] >= 0)
            def _():
                for cp in fetches(next_ref[i], 1 - s):
                    cp.start()

            wg_ref[...] = sg_ref[s].astype(bf16)
            wu_ref[...] = su_ref[s].astype(bf16)
            wd_ref[...] = sd_ref[s].astype(bf16)

        w_row = (xs_ref[:, D_MODEL:D_MODEL + 1].astype(f32)
                 + xs_ref[:, D_MODEL + 1:D_MODEL + 2].astype(f32))
        gate = jnp.dot(xs_ref[:, :D_MODEL], wg_ref[...], preferred_element_type=f32)
        up = jnp.dot(xs_ref[:, :D_MODEL], wu_ref[...], preferred_element_type=f32)
        hdn = (gate * jax.nn.sigmoid(gate)) * up
        y = jnp.dot(hdn.astype(bf16), wd_ref[...], preferred_element_type=f32)
        xs_ref[:, :D_MODEL] = (y * w_row).astype(bf16)
        store_tile(i).start()

    @pl.when(i == pl.num_programs(0) - 1)
    def _():
        for back in range(1, XS_RING + 1):
            @pl.when(n_used - back >= 0)
            def _():
                store_tile(n_used - back).wait()


def _experts(tile_exp, first, next_exp, slot, n_used, xs, wg, wu, wd, rows_per_tile):
    r = rows_per_tile
    n_tiles = xs.shape[0] // r
    grid_spec = pltpu.PrefetchScalarGridSpec(
        num_scalar_prefetch=5,
        grid=(n_tiles,),
        in_specs=[pl.BlockSpec(memory_space=pl.ANY),
                  pl.BlockSpec(memory_space=pl.ANY),
                  pl.BlockSpec(memory_space=pl.ANY),
                  pl.BlockSpec(memory_space=pl.ANY)],
        out_specs=pl.BlockSpec(memory_space=pl.ANY),
        scratch_shapes=[pltpu.VMEM((XS_RING, r, XS_COLS), bf16),
                        pltpu.VMEM((2, D_MODEL, D_FF_E), f32),
                        pltpu.VMEM((2, D_MODEL, D_FF_E), f32),
                        pltpu.VMEM((2, D_FF_E, D_MODEL), f32),
                        pltpu.VMEM((D_MODEL, D_FF_E), bf16),
                        pltpu.VMEM((D_MODEL, D_FF_E), bf16),
                        pltpu.VMEM((D_FF_E, D_MODEL), bf16),
                        pltpu.SemaphoreType.DMA((XS_RING,)),
                        pltpu.SemaphoreType.DMA((XS_RING,)),
                        pltpu.SemaphoreType.DMA((2,))])
    return pl.pallas_call(
        _experts_body,
        grid_spec=grid_spec,
        out_shape=jax.ShapeDtypeStruct(xs.shape, xs.dtype),
        input_output_aliases={5: 0},
        compiler_params=pltpu.CompilerParams(
            dimension_semantics=("arbitrary",), vmem_limit_bytes=VMEM_LIMIT),
        name="experts",
    )(tile_exp, first, next_exp, slot, n_used, xs, wg, wu, wd)


def _combine_out_body(runs_cur_ref, runs_nxt_ref, h_ref, slot_ref, p_ref, ys_ref,
                      wpg_ref, bpg_ref, wpp_ref, gple_ref, gfin_ref,
                      o_ref, ybuf_ref, moe_ref, sem_ref):
    tk = h_ref.shape[0]
    lb = ybuf_ref.shape[1]
    step = pl.program_id(0)
    n = pl.num_programs(0) - 1
    cur = jnp.minimum(step, n - 1) % 2

    def piece(s, local_row, global_row, rows):
        return pltpu.make_async_copy(ys_ref.at[pl.ds(global_row, rows), pl.ds(0, D_MODEL)],
                                     ybuf_ref.at[s, pl.ds(local_row, rows), :], sem_ref.at[s])

    @pl.when(step == 0)
    def _():
        ybuf_ref[...] = jnp.zeros(ybuf_ref.shape, bf16)
        moe_ref[...] = jnp.zeros(moe_ref.shape, f32)
        _for_each_run_piece(runs_cur_ref, lambda l, g, r: piece(0, l, g, r).start())

    @pl.when(step + 1 < n)
    def _():
        _for_each_run_piece(runs_nxt_ref, lambda l, g, r: piece(1 - cur, l, g, r).start())

    run_rows = _tile_run_rows(runs_cur_ref)

    @pl.when(step < n)
    def _():
        _wait_rows(run_rows, lambda r: piece(cur, 0, 0, r))

    def work(rows):
        h2 = h_ref[...] + moe_ref[...]
        gate = jax.nn.sigmoid(jnp.dot(h2.astype(bf16), wpg_ref[...], preferred_element_type=f32)
                              + bpg_ref[...])
        pp = jnp.dot(p_ref[...].astype(bf16), wpp_ref[...], preferred_element_type=f32)
        h3 = h2 + gate * _rms(pp, gple_ref[...])
        o_ref[...] = _rms(h3, gfin_ref[...])
        col = lax.broadcasted_iota(i32, (tk, rows), 1)
        unperm = jnp.where((col == slot_ref[:, 0:1]) | (col == slot_ref[:, 1:2]), 1.0, 0.0)
        moe_ref[...] = jnp.dot(unperm.astype(bf16), ybuf_ref[cur, :rows, :],
                               preferred_element_type=f32)

    short = lb - 256
    pl.when(run_rows <= short)(functools.partial(work, short))
    pl.when(run_rows > short)(functools.partial(work, lb))


def _combine_out(run_tiles, h2d, slot_t, p2d, ys, wpg, bpg, wpp, gple, gfin):
    t = h2d.shape[0]
    tk = TL_SORT
    nt = t // tk
    const = lambda shape: pl.BlockSpec(shape, lambda i: (0,) * len(shape))
    runs = lambda imap: pl.BlockSpec((1, 1, 3 * N_EXPERTS), imap, memory_space=pltpu.SMEM)
    uns = lambda s: jnp.minimum(s, nt - 1)
    fin = lambda s: jnp.maximum(s - 1, 0)
    return pl.pallas_call(
        _combine_out_body,
        grid=(nt + 1,),
        out_shape=jax.ShapeDtypeStruct((t, D_MODEL), f32),
        in_specs=[runs(lambda s: (uns(s), 0, 0)),
                  runs(lambda s: (uns(s + 1), 0, 0)),
                  pl.BlockSpec((tk, D_MODEL), lambda s: (fin(s), 0)),
                  pl.BlockSpec((tk, 8), lambda s: (uns(s), 0)),
                  pl.BlockSpec((tk, PLE_DIM), lambda s: (fin(s), 0)),
                  pl.BlockSpec(memory_space=pl.ANY),
                  const((D_MODEL, D_MODEL)), const((1, D_MODEL)),
                  const((PLE_DIM, D_MODEL)), const((1, D_MODEL)), const((1, D_MODEL))],
        out_specs=pl.BlockSpec((tk, D_MODEL), lambda s: (fin(s), 0)),
        scratch_shapes=[pltpu.VMEM((2, LB_SORT, D_MODEL), bf16),
                        pltpu.VMEM((tk, D_MODEL), f32),
                        pltpu.SemaphoreType.DMA((2,))],
        compiler_params=pltpu.CompilerParams(
            dimension_semantics=("arbitrary",), vmem_limit_bytes=VMEM_LIMIT),
        name="combine_out",
    )(run_tiles, run_tiles, h2d, slot_t, p2d, ys, wpg, bpg, wpp, gple, gfin)


def _head_major_to_tile_major(a, axis):
    axis = axis % a.ndim
    shape = a.shape
    a = a.reshape(shape[:axis] + (N_KV, GQA, HEAD_DIM) + shape[axis + 1:])
    return jnp.swapaxes(a, axis, axis + 1).reshape(shape)


def kernel(x, p, rel_bias, g_mix, w_in, ln_v_g, ln_v_b, w_spatial, b_spatial, sink, g_out_grp, w_out,
           g_ffn, w_router_group, b_router_group, w_router_expert, b_router_expert, w_gate_e, w_up_e,
           w_down_e, w_ple_proj, g_ple, w_ple_gate, b_ple_gate, g_final):
    b, s, d = x.shape
    t = b * s
    depth = g_mix.shape[0]
    assert depth == 1 and d == D_MODEL
    c1, c2 = 2 * D_A, 2 * D_A + D_B
    bias = _bias_table(rel_bias * LOG2E)
    hcur = x.astype(f32)
    for li in range(depth):
        col_scale = np.ones((1, D_IN), np.float32)
        col_scale[:, c1:c2] = LOG2E * HEAD_DIM ** -0.5
        win = (w_in[li] * col_scale).astype(bf16)
        bs = jnp.broadcast_to(b_spatial[li][:, :, None], (A_HEADS, BLK, BLK)).astype(f32)
        gout = g_out_grp[li]
        gout_b = _head_major_to_tile_major(gout[D_A:], 0)[None]
        wo = w_out[li]
        wout = jnp.concatenate([wo[:D_A], _head_major_to_tile_major(wo[D_A:], 0)],
                               axis=0).astype(bf16)
        wr = jnp.concatenate([
            w_router_group[li], jnp.zeros((D_MODEL, 8 - N_GROUPS), f32), w_router_expert[li],
            jnp.zeros((D_MODEL, BLK - ROUTE_ROWS), f32)], axis=1).astype(bf16)
        br = jnp.concatenate([b_router_group[li], jnp.full((8 - N_GROUPS,), NEG, f32),
                              b_router_expert[li]])
        br = jnp.broadcast_to(br[:, None], (ROUTE_ROWS, TQ_ATT))

        ya, q, k, v = _mix_in(hcur.reshape(t, d), g_mix[li][None], win, ln_v_g[li][None],
                              ln_v_b[li][None], w_spatial[li].astype(bf16), bs, gout[None, :D_A])
        h, m_rows, route_e, route_w = _attn_out(
            sink[li].astype(f32) * LOG2E, hcur, ya.reshape(b, s, D_A), q.reshape(b, s, D_B),
            k.reshape(b, s, BLK), v.reshape(b, s, BLK), bias, gout_b, wout,
            g_ffn[li][None], wr, br)

        r = R_EXP
        n_sort = t // TL_SORT
        max_rows = 2 * t + n_sort * N_EXPERTS * (RUN_ALIGN - 1)
        n_tiles = -(-max_rows // r) + N_EXPERTS
        assert n_tiles * r < 2 ** 16, "route_pos prefix sums hold row offsets in two bf16 bytes"
        slot, runs, offs = _route_pos(route_e, r)
        starts = offs[0, :, 0]
        padded = offs[1, :, 0]
        ends = starts + padded
        n_used = ends[-1] // r
        tile_idx = jnp.minimum(jnp.arange(n_tiles, dtype=i32), n_used - 1)
        tile_exp = jnp.minimum(
            jnp.sum((ends[None, :] <= (tile_idx * r)[:, None]).astype(i32), axis=1), N_EXPERTS - 1)
        tile_exp = tile_exp.astype(i32)
        first = jnp.concatenate([jnp.ones((1,), i32),
                                 (tile_exp[1:] != tile_exp[:-1]).astype(i32)])
        next_tile = ends[tile_exp] // r
        next_exp = jnp.where(next_tile < n_used, tile_exp[jnp.minimum(next_tile, n_tiles - 1)], -1)
        fetch_slot = (jnp.cumsum(first) - 1) % 2
        run_tiles = runs[:, :, :, 0].reshape(n_sort, 1, 3 * N_EXPERTS)
        xs, slot_t = _sort_rows(ends, padded, offs[2, :, 0], run_tiles, m_rows, slot, route_w,
                                n_tiles * r, r)
        ys = _experts(tile_exp, first, next_exp.astype(i32), fetch_slot.astype(i32),
                      n_used.reshape(1), xs, w_gate_e[li], w_up_e[li], w_down_e[li], r)

        out = _combine_out(run_tiles, h.reshape(t, d), slot_t,
                           p[li].reshape(t, PLE_DIM), ys, w_ple_gate[li].astype(bf16),
                           b_ple_gate[li][None], w_ple_proj[li].astype(bf16), g_ple[li][None],
                           g_final[None])
        hcur = out.reshape(b, s, d)
    return hcur
```

```python
import functools
import math

import jax
import jax.numpy as jnp
import numpy as np
from jax import lax
from jax.experimental import pallas as pl
from jax.experimental.pallas import tpu as pltpu

D_MODEL = 1024
D_A = 512
D_B = 512
BLK = 128
A_HEADS = 4
HEAD_DIM = 64
N_HEADS = 8
N_KV = 2
GQA = 4
WINDOW = 128
NUM_BUCKETS = 32
MAX_DIST = 128
D_IN = 2 * D_A + D_B + 2 * N_KV * HEAD_DIM
N_GROUPS = 4
E_PER_GROUP = 8
N_EXPERTS = 32
D_FF_E = 256
PLE_DIM = 256
EPS = 1e-6
NEG = -1e30
LOG2E = math.log2(math.e)

TM_MIX = 512
TQ_ATT = 512
TL_SORT = 512
RUN_ALIGN = 16
SMALL_BITS = 2
BIG_PIECE = RUN_ALIGN << SMALL_BITS
LB_SORT = 2 * TL_SORT + N_EXPERTS * RUN_ALIGN
R_EXP = 512
XS_COLS = D_MODEL + BLK
XS_RING = 4
XS_AHEAD = 2
ROUTE_ROWS = 8 + N_EXPERTS
VMEM_LIMIT = 48 * 1024 * 1024

f32 = jnp.float32
bf16 = jnp.bfloat16
i32 = jnp.int32


def _rms(x, g):
    return x * lax.rsqrt(jnp.mean(x * x, axis=-1, keepdims=True) + EPS) * g


def _gelu_tanh(x):
    c = math.sqrt(2.0 / math.pi)
    return x * (0.5 * (1.0 + jnp.tanh(c * (x + 0.044715 * (x * x * x)))))


def _bucket_table():
    n = NUM_BUCKETS // 2
    max_exact = n // 2
    i = np.arange(BLK)[:, None]
    j = np.arange(3 * BLK)[None, :]
    rel = j - BLK - i
    ret = np.where(rel > 0, n, 0)
    a = np.abs(rel)
    large = max_exact + (np.log(np.maximum(a, 1).astype(np.float64) / max_exact)
                         / math.log(MAX_DIST / max_exact) * (n - max_exact)).astype(np.int32)
    large = np.minimum(large, n - 1)
    bucket = ret + np.where(a < max_exact, a, large)
    return np.where(a <= WINDOW, bucket, -1).astype(np.int32)


def _bias_body(rb_ref, bucket_ref, o_ref):
    rows = 16
    for r0 in range(0, BLK, rows):
        bucket = bucket_ref[r0:r0 + rows, :]
        acc = [jnp.full(bucket.shape, NEG, f32) for _ in range(N_HEADS)]
        for b in range(NUM_BUCKETS):
            hit = bucket == b
            acc = [jnp.where(hit, rb_ref[b, h], acc[h]) for h in range(N_HEADS)]
        for h in range(N_HEADS):
            o_ref[h, r0:r0 + rows, :] = acc[h]


def _bias_table(rel_bias):
    bucket = jnp.asarray(_bucket_table())
    return pl.pallas_call(
        _bias_body,
        out_shape=jax.ShapeDtypeStruct((N_HEADS, BLK, 3 * BLK), f32),
        in_specs=[pl.BlockSpec(memory_space=pltpu.SMEM),
                  pl.BlockSpec((BLK, 3 * BLK), lambda: (0, 0))],
        out_specs=pl.BlockSpec((N_HEADS, BLK, 3 * BLK), lambda: (0, 0, 0)),
        name="bias_table",
    )(rel_bias.astype(f32), bucket)


def _mix_in_body(x_ref, gmix_ref, win_ref, lng_ref, lnb_ref, ws_ref, bs_ref, gout_ref,
                 ya_ref, q_ref, k_ref, v_ref):
    tm = x_ref.shape[0]
    nc = tm // BLK
    a = _rms(x_ref[...], gmix_ref[...])
    z = jnp.dot(a.astype(bf16), win_ref[...], preferred_element_type=f32)
    uv = _gelu_tanh(z[:, :2 * D_A])
    u = uv[:, :D_A]
    v = uv[:, D_A:]
    mu = jnp.mean(v, axis=-1, keepdims=True)
    vc = v - mu
    var = jnp.mean(vc * vc, axis=-1, keepdims=True)
    vn = (vc * lax.rsqrt(var + EPS) * lng_ref[...] + lnb_ref[...]).astype(bf16)
    cols = []
    for h in range(A_HEADS):
        rhs = jnp.concatenate(
            [vn[c * BLK:(c + 1) * BLK, h * BLK:(h + 1) * BLK] for c in range(nc)], axis=1)
        r = jnp.dot(ws_ref[h], rhs, preferred_element_type=f32)
        cols.append(jnp.concatenate(
            [r[:, c * BLK:(c + 1) * BLK] + bs_ref[h] for c in range(nc)], axis=0))
    sv = jnp.concatenate(cols, axis=1)
    ya_ref[...] = _rms(u * sv, gout_ref[...]).astype(bf16)
    qt = [z[:, 2 * D_A + j * BLK:2 * D_A + (j + 1) * BLK] for j in range(D_B // BLK)]
    low_half = lax.broadcasted_iota(i32, (tm, BLK), 1) < HEAD_DIM
    for g in range(GQA):
        a0, a1 = qt[g // 2], qt[GQA // 2 + g // 2]
        tile = (jnp.where(low_half, a0, pltpu.roll(a1, HEAD_DIM, 1)) if g % 2 == 0
                else jnp.where(low_half, pltpu.roll(a0, HEAD_DIM, 1), a1))
        q_ref[:, g * BLK:(g + 1) * BLK] = tile.astype(bf16)
    k_ref[...] = z[:, 2 * D_A + D_B:2 * D_A + D_B + BLK].astype(bf16)
    v_ref[...] = z[:, 2 * D_A + D_B + BLK:].astype(bf16)


def _mix_in(x2, gmix, win, lng, lnb, ws, bs, gout_a):
    t = x2.shape[0]
    tm = TM_MIX
    const = lambda shape: pl.BlockSpec(shape, lambda i: (0,) * len(shape))
    return pl.pallas_call(
        _mix_in_body,
        grid=(t // tm,),
        out_shape=(jax.ShapeDtypeStruct((t, D_A), bf16),
                   jax.ShapeDtypeStruct((t, D_B), bf16),
                   jax.ShapeDtypeStruct((t, BLK), bf16),
                   jax.ShapeDtypeStruct((t, BLK), bf16)),
        in_specs=[pl.BlockSpec((tm, D_MODEL), lambda i: (i, 0)),
                  const((1, D_MODEL)), const((D_MODEL, D_IN)),
                  const((1, D_A)), const((1, D_A)),
                  const((A_HEADS, BLK, BLK)), const((A_HEADS, BLK, BLK)),
                  const((1, D_A))],
        out_specs=(pl.BlockSpec((tm, D_A), lambda i: (i, 0)),
                   pl.BlockSpec((tm, D_B), lambda i: (i, 0)),
                   pl.BlockSpec((tm, BLK), lambda i: (i, 0)),
                   pl.BlockSpec((tm, BLK), lambda i: (i, 0))),
        compiler_params=pltpu.CompilerParams(
            dimension_semantics=("parallel",), vmem_limit_bytes=VMEM_LIMIT),
        name="mix_in",
    )(x2, gmix, win, lng, lnb, ws, bs, gout_a)


def _attn_out_body(sink_ref, x_ref, ya_ref, q_ref, kp_ref, km_ref, kn_ref, vp_ref, vm_ref, vn_ref,
                   bias_ref, goutb_ref, wout_ref, gffn_ref, wr_ref, br_ref,
                   h_ref, m_ref, re_ref, rw_ref,
                   kf_ref, vf_ref, e_ref, yb_ref, *, seq_len):
    tq = x_ref.shape[1]
    nb = tq // BLK
    step = pl.program_id(0)
    n_tok_tiles = pl.num_programs(0) - 1
    ti = jnp.minimum(step, n_tok_tiles - 1) % (seq_len // tq)

    @pl.when(step == 0)
    def _():
        yb_ref[1] = jnp.zeros(yb_ref.shape[1:], f32)

    ybn = _rms(yb_ref[1], goutb_ref[...]).astype(bf16)
    y = jnp.concatenate([ya_ref[0], ybn], axis=1)
    h = x_ref[0] + jnp.dot(y, wout_ref[...], preferred_element_type=f32)
    h_ref[0] = h
    m = _rms(h, gffn_ref[...])
    m_ref[...] = m.astype(bf16)

    logit_t = jnp.dot(m.astype(bf16), wr_ref[...], preferred_element_type=f32)
    logit = jnp.transpose(logit_t)[:ROUTE_ROWS] + br_ref[...]
    sub = lax.broadcasted_iota(i32, (8, tq), 0)
    lg = logit[0:8]
    mg = jnp.max(lg, axis=0, keepdims=True)
    pg_top = 1.0 / jnp.sum(jnp.exp(lg - mg), axis=0, keepdims=True)
    g_idx = jnp.min(jnp.where(lg == mg, sub, 8), axis=0, keepdims=True)
    sel = logit[8:16]
    for g in range(1, N_GROUPS):
        sel = jnp.where(g_idx == g, logit[8 + 8 * g:16 + 8 * g], sel)
    m1 = jnp.max(sel, axis=0, keepdims=True)
    i1 = jnp.min(jnp.where(sel == m1, sub, 8), axis=0, keepdims=True)
    sel2 = jnp.where(sub == i1, -jnp.inf, sel)
    m2 = jnp.max(sel2, axis=0, keepdims=True)
    i2 = jnp.min(jnp.where(sel2 == m2, sub, 8), axis=0, keepdims=True)
    r = jnp.exp(m2 - m1)
    w1 = pg_top / (1.0 + r)
    w2 = pg_top * r / (1.0 + r)
    e1 = g_idx * E_PER_GROUP + i1
    e2 = g_idx * E_PER_GROUP + i2
    re_ref[...] = jnp.where(sub == 0, e1, jnp.where(sub == 1, e2, 0))
    rw_ref[...] = jnp.where(sub == 0, w1, jnp.where(sub == 1, w2, 0.0))

    kf_ref[0:BLK] = kp_ref[0]
    kf_ref[BLK:BLK + tq] = km_ref[0]
    kf_ref[BLK + tq:] = kn_ref[0]
    vlane = lax.broadcasted_iota(i32, (BLK, BLK), 1)
    for src, r0, r1 in ((vp_ref, 0, BLK), (vn_ref, BLK + tq, tq + 2 * BLK)) + tuple(
            (vm_ref.at[:, pl.ds(c * BLK, BLK), :], BLK + c * BLK, BLK + (c + 1) * BLK)
            for c in range(nb)):
        vals = src[0].astype(f32)
        vf_ref[0, r0:r1] = jnp.where(vlane < HEAD_DIM, vals,
                                     jnp.where(vlane == HEAD_DIM, 1.0, 0.0)).astype(bf16)
        vf_ref[1, r0:r1] = jnp.where(vlane >= HEAD_DIM, vals,
                                     jnp.where(vlane == 0, 1.0, 0.0)).astype(bf16)

    lane = lax.broadcasted_iota(i32, (BLK, BLK), 1)
    low_half = lane < HEAD_DIM

    for n in range(nb):
        r0 = n * BLK
        qb = q_ref[0, pl.ds(r0, BLK), :]
        kb = kf_ref[pl.ds(r0, 3 * BLK), :]
        zero = jnp.zeros((BLK, BLK), bf16)
        lhs = []
        for kh in range(N_KV):
            for g in range(GQA):
                tile = qb[:, g * BLK:(g + 1) * BLK]
                lhs.append(jnp.where(low_half, tile, zero) if kh == 0
                           else jnp.where(low_half, zero, tile))
        lhs = jnp.concatenate(lhs, axis=0)
        s_all = lax.dot_general(lhs, kb, (((1,), (1,)), ((), ())),
                                preferred_element_type=f32)
        blk = ti * nb + n
        pen = [jnp.where(blk == 0, NEG, 0.0) if n == 0 else None, None,
               jnp.where((blk + 1) * BLK >= seq_len, NEG, 0.0) if n == nb - 1 else None]
        sink_e = []
        for h in range(N_HEADS):
            bias_h = bias_ref[h]
            bias_h = jnp.concatenate(
                [bias_h[:, j * BLK:(j + 1) * BLK] if pen[j] is None
                 else bias_h[:, j * BLK:(j + 1) * BLK] + pen[j] for j in range(3)], axis=1)
            s = s_all[h * BLK:(h + 1) * BLK] + bias_h
            sk = sink_ref[h]
            mrow = jnp.maximum(jnp.max(s, axis=-1, keepdims=True), sk)
            e_ref[n, h * BLK:(h + 1) * BLK, :] = jnp.exp2(s - mrow).astype(bf16)
            sink_e.append(jnp.exp2(sk - mrow))
        half = GQA * BLK
        pv = [jnp.dot(e_ref[n, kh * half:(kh + 1) * half, :], vf_ref[kh, pl.ds(r0, 3 * BLK), :],
                      preferred_element_type=f32) for kh in range(N_KV)]

        def head_out(h):
            kh, g = divmod(h, GQA)
            rows = pv[kh][g * BLK:(g + 1) * BLK]
            ones_col = HEAD_DIM if kh == 0 else 0
            return rows * (1.0 / (rows[:, ones_col:ones_col + 1] + sink_e[h]))

        for g in range(GQA):
            yb_ref[0, pl.ds(r0, BLK), g * BLK:(g + 1) * BLK] = jnp.where(
                low_half, head_out(g), head_out(GQA + g))

    yb_ref[1] = yb_ref[0]


def _attn_out(sink, x, ya, q, k, v, bias, goutb, wout, gffn, wr, br):
    b, s, _ = x.shape
    tq = TQ_ATT
    nb = tq // BLK
    nblk = s // BLK
    t = b * s
    nq = s // tq
    n_tok_tiles = b * nq
    att = lambda s: jnp.minimum(s, n_tok_tiles - 1)
    epi = lambda s: jnp.maximum(s - 1, 0)
    const = lambda shape: pl.BlockSpec(shape, lambda s, *_: (0,) * len(shape))
    tok_att = lambda w: pl.BlockSpec((1, tq, w), lambda s, *_: (att(s) // nq, att(s) % nq, 0))
    tok_epi = lambda w: pl.BlockSpec((1, tq, w), lambda s, *_: (epi(s) // nq, epi(s) % nq, 0))
    prev = pl.BlockSpec((1, BLK, BLK), lambda s, *_: (
        att(s) // nq, jnp.maximum((att(s) % nq) * nb - 1, 0), 0))
    nxt = pl.BlockSpec((1, BLK, BLK), lambda s, *_: (
        att(s) // nq, jnp.minimum((att(s) % nq) * nb + nb, nblk - 1), 0))
    rows = pl.BlockSpec((tq, D_MODEL), lambda s, *_: (epi(s), 0))
    lanes = lambda rows: pl.BlockSpec((rows, tq), lambda s, *_: (0, epi(s)))
    grid_spec = pltpu.PrefetchScalarGridSpec(
        num_scalar_prefetch=1,
        grid=(n_tok_tiles + 1,),
        in_specs=[tok_epi(D_MODEL), tok_epi(D_A), tok_att(D_B),
                  prev, tok_att(BLK), nxt, prev, tok_att(BLK), nxt,
                  const((N_HEADS, BLK, 3 * BLK)), const((1, D_B)), const((D_MODEL, D_MODEL)),
                  const((1, D_MODEL)), const((D_MODEL, BLK)), const((ROUTE_ROWS, tq))],
        out_specs=(tok_epi(D_MODEL), rows, lanes(8), lanes(8)),
        scratch_shapes=[pltpu.VMEM((tq + 2 * BLK, BLK), bf16),
                        pltpu.VMEM((N_KV, tq + 2 * BLK, BLK), bf16),
                        pltpu.VMEM((nb, N_HEADS * BLK, 3 * BLK), bf16),
                        pltpu.VMEM((2, tq, D_B), f32)])
    return pl.pallas_call(
        functools.partial(_attn_out_body, seq_len=s),
        grid_spec=grid_spec,
        out_shape=(jax.ShapeDtypeStruct((b, s, D_MODEL), f32),
                   jax.ShapeDtypeStruct((t, D_MODEL), bf16),
                   jax.ShapeDtypeStruct((8, t), i32),
                   jax.ShapeDtypeStruct((8, t), f32)),
        compiler_params=pltpu.CompilerParams(
            dimension_semantics=("arbitrary",), vmem_limit_bytes=VMEM_LIMIT),
        name="attn_out",
    )(sink, x, ya, q, k, k, k, v, v, v, bias, goutb, wout, gffn, wr, br)


def _exclusive_prefix(vals):
    a = lax.broadcasted_iota(i32, (N_EXPERTS, N_EXPERTS), 0)
    c = lax.broadcasted_iota(i32, (N_EXPERTS, N_EXPERTS), 1)
    low = jnp.where(c < a, 1.0, 0.0).astype(bf16)
    hi = (vals >> 8).astype(f32).astype(bf16)
    lo = (vals & 255).astype(f32).astype(bf16)
    return (jnp.dot(low, hi, preferred_element_type=f32) * 256.0
            + jnp.dot(low, lo, preferred_element_type=f32)).astype(i32)


def _round_up_pow2(vals, mult):
    log_m = mult.bit_length() - 1
    return ((vals + (mult - 1)) >> log_m) << log_m


def _route_pos_body(re_ref, slot_ref, runs_ref, offs_ref, tri_ref, *, rows_per_tile, tl):
    n_tok_tiles = re_ref.shape[1] // tl
    a = lax.broadcasted_iota(i32, (tl, tl), 0)
    c = lax.broadcasted_iota(i32, (tl, tl), 1)
    tri_ref[...] = jnp.where(a < c, 1.0, 0.0).astype(bf16)
    eid = lax.broadcasted_iota(i32, (N_EXPERTS, tl), 0)

    def tile_hits(j):
        cols = pl.ds(pl.multiple_of(j * tl, tl), tl)
        hit1 = eid == re_ref[0:1, cols]
        hit2 = eid == re_ref[1:2, cols]
        onehot = jnp.where(hit1 | hit2, 1.0, 0.0)
        cnt = jnp.sum(onehot, axis=1, keepdims=True).astype(i32)
        run_len = jnp.broadcast_to(_round_up_pow2(cnt, RUN_ALIGN), (N_EXPERTS, BLK))
        return cols, hit1, hit2, onehot, run_len

    total = lax.fori_loop(0, n_tok_tiles, lambda j, acc: acc + tile_hits(j)[4],
                          jnp.zeros((N_EXPERTS, BLK), i32))
    padded = _round_up_pow2(total, rows_per_tile)
    starts = _exclusive_prefix(padded)
    offs_ref[0] = starts
    offs_ref[1] = padded
    offs_ref[2] = total

    def place(j, global_off):
        cols, hit1, hit2, onehot, run_len = tile_hits(j)
        local_off = _exclusive_prefix(run_len)
        before = jnp.dot(onehot.astype(bf16), tri_ref[...], preferred_element_type=f32)
        slot = before + local_off[:, 0:1].astype(f32)
        s1 = jnp.sum(jnp.where(hit1, slot, 0.0), axis=0, keepdims=True)
        s2 = jnp.sum(jnp.where(hit2, slot, 0.0), axis=0, keepdims=True)
        sub = lax.broadcasted_iota(i32, (8, tl), 0)
        slot_ref[:, cols] = jnp.where(sub == 0, s1.astype(i32),
                                      jnp.where(sub == 1, s2.astype(i32), 0))
        runs_ref[j, 0] = local_off
        runs_ref[j, 1] = run_len
        runs_ref[j, 2] = global_off
        return global_off + run_len

    lax.fori_loop(0, n_tok_tiles, place, starts)


def _route_pos(route_e, rows_per_tile):
    t = route_e.shape[1]
    tl = TL_SORT
    return pl.pallas_call(
        functools.partial(_route_pos_body, rows_per_tile=rows_per_tile, tl=tl),
        out_shape=(jax.ShapeDtypeStruct((8, t), i32),
                   jax.ShapeDtypeStruct((t // tl, 3, N_EXPERTS, BLK), i32),
                   jax.ShapeDtypeStruct((3, N_EXPERTS, BLK), i32)),
        scratch_shapes=[pltpu.VMEM((tl, tl), bf16)],
        compiler_params=pltpu.CompilerParams(vmem_limit_bytes=VMEM_LIMIT),
        name="route_pos",
    )(route_e)


def _for_each_run_piece(runs_ref, fn):
    def per_expert(e, carry):
        local_off = runs_ref[0, 0, e]
        length = runs_ref[0, 0, N_EXPERTS + e]
        global_off = runs_ref[0, 0, 2 * N_EXPERTS + e]
        units = length >> (RUN_ALIGN.bit_length() - 1)

        def big_piece(k, inner):
            done = k * BIG_PIECE
            fn(pl.multiple_of(local_off + done, RUN_ALIGN),
               pl.multiple_of(global_off + done, RUN_ALIGN), BIG_PIECE)
            return inner

        lax.fori_loop(0, units >> SMALL_BITS, big_piece, 0)
        for b in range(SMALL_BITS):
            @pl.when(((units >> b) & 1) == 1)
            def _():
                done = ((units >> (b + 1)) << (b + 1)) * RUN_ALIGN
                fn(pl.multiple_of(local_off + done, RUN_ALIGN),
                   pl.multiple_of(global_off + done, RUN_ALIGN), RUN_ALIGN << b)
        return carry

    lax.fori_loop(0, N_EXPERTS, per_expert, 0)


def _tile_run_rows(runs_ref):
    return lax.fori_loop(0, N_EXPERTS, lambda e, acc: acc + runs_ref[0, 0, N_EXPERTS + e], 0)


def _wait_rows(total_rows, make_copy):
    units = total_rows >> (RUN_ALIGN.bit_length() - 1)
    for b in range((LB_SORT // RUN_ALIGN).bit_length()):
        @pl.when(((units >> b) & 1) == 1)
        def _():
            make_copy(RUN_ALIGN << b).wait()


def _sort_rows_body(ends_ref, padded_ref, total_ref, runs_ref, m_ref, slot_ref, rw_ref,
                    xs_ref, slot_t_ref,
                    local_ref, zero_ref, pending_ref, sem_ref, zsem_ref, *, rows_per_tile):
    tl = m_ref.shape[0]
    lb = local_ref.shape[1]
    step = pl.program_id(0)
    cur = step % 2
    n_tiles = xs_ref.shape[0] // rows_per_tile
    n_used = ends_ref[N_EXPERTS - 1] // rows_per_tile

    def zero_rows(start, rows, wait):
        cp = pltpu.make_async_copy(zero_ref.at[pl.ds(0, rows), :],
                                   xs_ref.at[pl.ds(pl.multiple_of(start, RUN_ALIGN), rows), :],
                                   zsem_ref)
        cp.wait() if wait else cp.start()

    def zero_fill(wait):
        for e in range(N_EXPERTS):
            pad_units = (padded_ref[e] - total_ref[e]) >> (RUN_ALIGN.bit_length() - 1)
            for bit in range((rows_per_tile // RUN_ALIGN).bit_length() - 1):
                @pl.when(((pad_units >> bit) & 1) == 1)
                def _():
                    done = ((pad_units >> (bit + 1)) << (bit + 1)) * RUN_ALIGN
                    zero_rows(ends_ref[e] - (padded_ref[e] - total_ref[e]) + done,
                              RUN_ALIGN << bit, wait)

        def spare(j, carry):
            zero_rows((n_used + j) * rows_per_tile, rows_per_tile, wait)
            return carry

        lax.fori_loop(0, n_tiles - n_used, spare, 0)

    @pl.when(step == 0)
    def _():
        zero_ref[...] = jnp.zeros(zero_ref.shape, bf16)
        zero_fill(wait=False)

    s1 = slot_ref[0:1, :]
    s2 = slot_ref[1:2, :]
    w1 = rw_ref[0:1, :]
    w2 = rw_ref[1:2, :]
    run_rows = _tile_run_rows(runs_ref)

    def sort_rows_upto(n):
        srow = lax.broadcasted_iota(i32, (n, tl), 0)
        p1 = srow == s1
        p2 = srow == s2
        perm = jnp.where(p1 | p2, 1.0, 0.0).astype(bf16)
        rows = jnp.dot(perm, m_ref[...], preferred_element_type=f32)
        w = jnp.sum(jnp.where(p1, w1, 0.0) + jnp.where(p2, w2, 0.0), axis=1, keepdims=True)
        w_hi = w.astype(bf16).astype(f32)
        w_lo = w - w_hi
        lane = lax.broadcasted_iota(i32, (n, BLK), 1)
        local_ref[cur, :n, :D_MODEL] = rows.astype(bf16)
        local_ref[cur, :n, D_MODEL:] = jnp.where(
            lane == 0, w_hi, jnp.where(lane == 1, w_lo, 0.0)).astype(bf16)
        sub_t = lax.broadcasted_iota(i32, (BLK, tl), 0)
        slots = jnp.where(sub_t == 0, s1.astype(f32), jnp.where(sub_t == 1, s2.astype(f32), 0.0))
        slot_t_ref[...] = jnp.transpose(slots)[:, :8].astype(i32)

    def piece(s, local_row, global_row, rows):
        return pltpu.make_async_copy(local_ref.at[s, pl.ds(local_row, rows), :],
                                     xs_ref.at[pl.ds(global_row, rows), :], sem_ref.at[s])

    short = lb - 256
    pl.when(run_rows <= short)(functools.partial(sort_rows_upto, short))
    pl.when(run_rows > short)(functools.partial(sort_rows_upto, lb))

    _for_each_run_piece(runs_ref, lambda l, g, n: piece(cur, l, g, n).start())

    @pl.when(step > 0)
    def _():
        _wait_rows(pending_ref[0], lambda n: piece(1 - cur, 0, 0, n))

    pending_ref[0] = run_rows

    @pl.when(step == pl.num_programs(0) - 1)
    def _():
        _wait_rows(pending_ref[0], lambda n: piece(cur, 0, 0, n))
        zero_fill(wait=True)


def _sort_rows(ends, padded, totals, run_tiles, m, slot, route_w, total_rows, rows_per_tile):
    t = m.shape[0]
    tl = TL_SORT
    grid_spec = pltpu.PrefetchScalarGridSpec(
        num_scalar_prefetch=3,
        grid=(t // tl,),
        in_specs=[pl.BlockSpec((1, 1, 3 * N_EXPERTS), lambda i, *_: (i, 0, 0),
                               memory_space=pltpu.SMEM),
                  pl.BlockSpec((tl, D_MODEL), lambda i, *_: (i, 0)),
                  pl.BlockSpec((8, tl), lambda i, *_: (0, i)),
                  pl.BlockSpec((8, tl), lambda i, *_: (0, i))],
        out_specs=(pl.BlockSpec(memory_space=pl.ANY),
                   pl.BlockSpec((tl, 8), lambda i, *_: (i, 0))),
        scratch_shapes=[pltpu.VMEM((2, LB_SORT, XS_COLS), bf16),
                        pltpu.VMEM((rows_per_tile, XS_COLS), bf16),
                        pltpu.SMEM((1,), i32),
                        pltpu.SemaphoreType.DMA((2,)),
                        pltpu.SemaphoreType.DMA(())])
    return pl.pallas_call(
        functools.partial(_sort_rows_body, rows_per_tile=rows_per_tile),
        grid_spec=grid_spec,
        out_shape=(jax.ShapeDtypeStruct((total_rows, XS_COLS), bf16),
                   jax.ShapeDtypeStruct((t, 8), i32)),
        compiler_params=pltpu.CompilerParams(
            dimension_semantics=("arbitrary",), vmem_limit_bytes=VMEM_LIMIT),
        name="sort_rows",
    )(ends, padded, totals, run_tiles, m, slot, route_w)


def _experts_body(texp_ref, first_ref, next_ref, slot_ref, nused_ref,
                  xs_hbm, wg_hbm, wu_hbm, wd_hbm, ys_hbm,
                  xbuf_ref, sg_ref, su_ref, sd_ref, wg_ref, wu_ref, wd_ref,
                  xsem_ref, osem_ref, sem_ref):
    i = pl.program_id(0)
    n_used = nused_ref[0]
    r = xbuf_ref.shape[1]

    def tile_rows(ref, tile):
        return ref.at[pl.ds(pl.multiple_of(tile * r, r), r), :]

    def fetch_tile(tile):
        s = tile % XS_RING
        return pltpu.make_async_copy(tile_rows(xs_hbm, tile), xbuf_ref.at[s], xsem_ref.at[s])

    def store_tile(tile):
        s = tile % XS_RING
        return pltpu.make_async_copy(xbuf_ref.at[s], tile_rows(ys_hbm, tile), osem_ref.at[s])

    @pl.when(i == 0)
    def _():
        for ahead in range(XS_AHEAD):
            @pl.when(ahead < n_used)
            def _():
                fetch_tile(ahead).start()

    @pl.when(i + XS_AHEAD < n_used)
    def _():
        @pl.when(i + XS_AHEAD >= XS_RING)
        def _():
            store_tile(i + XS_AHEAD - XS_RING).wait()

        fetch_tile(i + XS_AHEAD).start()

    def fetches(expert, s):
        return (pltpu.make_async_copy(wg_hbm.at[expert], sg_ref.at[s], sem_ref.at[s]),
                pltpu.make_async_copy(wu_hbm.at[expert], su_ref.at[s], sem_ref.at[s]),
                pltpu.make_async_copy(wd_hbm.at[expert], sd_ref.at[s], sem_ref.at[s]))

    @pl.when(i < n_used)
    def _():
        s = slot_ref[i]
        fetch_tile(i).wait()
        xs_ref = xbuf_ref.at[i % XS_RING]

        @pl.when(first_ref[i] == 1)
        def _():
            @pl.when(i == 0)
            def _():
                for cp in fetches(texp_ref[0], 0):
                    cp.start()

            for cp in fetches(texp_ref[i], s):
                cp.wait()

            @pl.when(next_ref[i] >= 0)
            def _():
                for cp in fetches(next_ref[i], 1 - s):
                    cp.start()

            wg_ref[...] = sg_ref[s].astype(bf16)
            wu_ref[...] = su_ref[s].astype(bf16)
            wd_ref[...] = sd_ref[s].astype(bf16)

        w_row = (xs_ref[:, D_MODEL:D_MODEL + 1].astype(f32)
                 + xs_ref[:, D_MODEL + 1:D_MODEL + 2].astype(f32))
        gate = jnp.dot(xs_ref[:, :D_MODEL], wg_ref[...], preferred_element_type=f32)
        up = jnp.dot(xs_ref[:, :D_MODEL], wu_ref[...], preferred_element_type=f32)
        hdn = (gate * jax.nn.sigmoid(gate)) * up
        y = jnp.dot(hdn.astype(bf16), wd_ref[...], preferred_element_type=f32)
        xs_ref[:, :D_MODEL] = (y * w_row).astype(bf16)
        store_tile(i).start()

    @pl.when(i == pl.num_programs(0) - 1)
    def _():
        for back in range(1, XS_RING + 1):
            @pl.when(n_used - back >= 0)
            def _():
                store_tile(n_used - back).wait()


def _experts(tile_exp, first, next_exp, slot, n_used, xs, wg, wu, wd, rows_per_tile):
    r = rows_per_tile
    n_tiles = xs.shape[0] // r
    grid_spec = pltpu.PrefetchScalarGridSpec(
        num_scalar_prefetch=5,
        grid=(n_tiles,),
        in_specs=[pl.BlockSpec(memory_space=pl.ANY),
                  pl.BlockSpec(memory_space=pl.ANY),
                  pl.BlockSpec(memory_space=pl.ANY),
                  pl.BlockSpec(memory_space=pl.ANY)],
        out_specs=pl.BlockSpec(memory_space=pl.ANY),
        scratch_shapes=[pltpu.VMEM((XS_RING, r, XS_COLS), bf16),
                        pltpu.VMEM((2, D_MODEL, D_FF_E), f32),
                        pltpu.VMEM((2, D_MODEL, D_FF_E), f32),
                        pltpu.VMEM((2, D_FF_E, D_MODEL), f32),
                        pltpu.VMEM((D_MODEL, D_FF_E), bf16),
                        pltpu.VMEM((D_MODEL, D_FF_E), bf16),
                        pltpu.VMEM((D_FF_E, D_MODEL), bf16),
                        pltpu.SemaphoreType.DMA((XS_RING,)),
                        pltpu.SemaphoreType.DMA((XS_RING,)),
                        pltpu.SemaphoreType.DMA((2,))])
    return pl.pallas_call(
        _experts_body,
        grid_spec=grid_spec,
        out_shape=jax.ShapeDtypeStruct(xs.shape, xs.dtype),
        input_output_aliases={5: 0},
        compiler_params=pltpu.CompilerParams(
            dimension_semantics=("arbitrary",), vmem_limit_bytes=VMEM_LIMIT),
        name="experts",
    )(tile_exp, first, next_exp, slot, n_used, xs, wg, wu, wd)


def _combine_out_body(runs_cur_ref, runs_nxt_ref, h_ref, slot_ref, p_ref, ys_ref,
                      wpg_ref, bpg_ref, wpp_ref, gple_ref, gfin_ref,
                      o_ref, ybuf_ref, moe_ref, sem_ref):
    tk = h_ref.shape[0]
    lb = ybuf_ref.shape[1]
    step = pl.program_id(0)
    n = pl.num_programs(0) - 1
    cur = jnp.minimum(step, n - 1) % 2

    def piece(s, local_row, global_row, rows):
        return pltpu.make_async_copy(ys_ref.at[pl.ds(global_row, rows), pl.ds(0, D_MODEL)],
                                     ybuf_ref.at[s, pl.ds(local_row, rows), :], sem_ref.at[s])

    @pl.when(step == 0)
    def _():
        ybuf_ref[...] = jnp.zeros(ybuf_ref.shape, bf16)
        moe_ref[...] = jnp.zeros(moe_ref.shape, f32)
        _for_each_run_piece(runs_cur_ref, lambda l, g, r: piece(0, l, g, r).start())

    @pl.when(step + 1 < n)
    def _():
        _for_each_run_piece(runs_nxt_ref, lambda l, g, r: piece(1 - cur, l, g, r).start())

    run_rows = _tile_run_rows(runs_cur_ref)

    @pl.when(step < n)
    def _():
        _wait_rows(run_rows, lambda r: piece(cur, 0, 0, r))

    def work(rows):
        h2 = h_ref[...] + moe_ref[...]
        gate = jax.nn.sigmoid(jnp.dot(h2.astype(bf16), wpg_ref[...], preferred_element_type=f32)
                              + bpg_ref[...])
        pp = jnp.dot(p_ref[...].astype(bf16), wpp_ref[...], preferred_element_type=f32)
        h3 = h2 + gate * _rms(pp, gple_ref[...])
        o_ref[...] = _rms(h3, gfin_ref[...])
        col = lax.broadcasted_iota(i32, (tk, rows), 1)
        unperm = jnp.where((col == slot_ref[:, 0:1]) | (col == slot_ref[:, 1:2]), 1.0, 0.0)
        moe_ref[...] = jnp.dot(unperm.astype(bf16), ybuf_ref[cur, :rows, :],
                               preferred_element_type=f32)

    short = lb - 256
    pl.when(run_rows <= short)(functools.partial(work, short))
    pl.when(run_rows > short)(functools.partial(work, lb))


def _combine_out(run_tiles, h2d, slot_t, p2d, ys, wpg, bpg, wpp, gple, gfin):
    t = h2d.shape[0]
    tk = TL_SORT
    nt = t // tk
    const = lambda shape: pl.BlockSpec(shape, lambda i: (0,) * len(shape))
    runs = lambda imap: pl.BlockSpec((1, 1, 3 * N_EXPERTS), imap, memory_space=pltpu.SMEM)
    uns = lambda s: jnp.minimum(s, nt - 1)
    fin = lambda s: jnp.maximum(s - 1, 0)
    return pl.pallas_call(
        _combine_out_body,
        grid=(nt + 1,),
        out_shape=jax.ShapeDtypeStruct((t, D_MODEL), f32),
        in_specs=[runs(lambda s: (uns(s), 0, 0)),
                  runs(lambda s: (uns(s + 1), 0, 0)),
                  pl.BlockSpec((tk, D_MODEL), lambda s: (fin(s), 0)),
                  pl.BlockSpec((tk, 8), lambda s: (uns(s), 0)),
                  pl.BlockSpec((tk, PLE_DIM), lambda s: (fin(s), 0)),
                  pl.BlockSpec(memory_space=pl.ANY),
                  const((D_MODEL, D_MODEL)), const((1, D_MODEL)),
                  const((PLE_DIM, D_MODEL)), const((1, D_MODEL)), const((1, D_MODEL))],
        out_specs=pl.BlockSpec((tk, D_MODEL), lambda s: (fin(s), 0)),
        scratch_shapes=[pltpu.VMEM((2, LB_SORT, D_MODEL), bf16),
                        pltpu.VMEM((tk, D_MODEL), f32),
                        pltpu.SemaphoreType.DMA((2,))],
        compiler_params=pltpu.CompilerParams(
            dimension_semantics=("arbitrary",), vmem_limit_bytes=VMEM_LIMIT),
        name="combine_out",
    )(run_tiles, run_tiles, h2d, slot_t, p2d, ys, wpg, bpg, wpp, gple, gfin)


def _head_major_to_tile_major(a, axis):
    axis = axis % a.ndim
    shape = a.shape
    a = a.reshape(shape[:axis] + (N_KV, GQA, HEAD_DIM) + shape[axis + 1:])
    return jnp.swapaxes(a, axis, axis + 1).reshape(shape)


def kernel(x, p, rel_bias, g_mix, w_in, ln_v_g, ln_v_b, w_spatial, b_spatial, sink, g_out_grp, w_out,
           g_ffn, w_router_group, b_router_group, w_router_expert, b_router_expert, w_gate_e, w_up_e,
           w_down_e, w_ple_proj, g_ple, w_ple_gate, b_ple_gate, g_final):
    b, s, d = x.shape
    t = b * s
    depth = g_mix.shape[0]
    assert depth == 1 and d == D_MODEL
    c1, c2 = 2 * D_A, 2 * D_A + D_B
    bias = _bias_table(rel_bias * LOG2E)
    hcur = x.astype(f32)
    for li in range(depth):
        col_scale = np.ones((1, D_IN), np.float32)
        col_scale[:, c1:c2] = LOG2E * HEAD_DIM ** -0.5
        win = (w_in[li] * col_scale).astype(bf16)
        bs = jnp.broadcast_to(b_spatial[li][:, :, None], (A_HEADS, BLK, BLK)).astype(f32)
        gout = g_out_grp[li]
        gout_b = _head_major_to_tile_major(gout[D_A:], 0)[None]
        wo = w_out[li]
        wout = jnp.concatenate([wo[:D_A], _head_major_to_tile_major(wo[D_A:], 0)],
                               axis=0).astype(bf16)
        wr = jnp.concatenate([
            w_router_group[li], jnp.zeros((D_MODEL, 8 - N_GROUPS), f32), w_router_expert[li],
            jnp.zeros((D_MODEL, BLK - ROUTE_ROWS), f32)], axis=1).astype(bf16)
        br = jnp.concatenate([b_router_group[li], jnp.full((8 - N_GROUPS,), NEG, f32),
                              b_router_expert[li]])
        br = jnp.broadcast_to(br[:, None], (ROUTE_ROWS, TQ_ATT))

        ya, q, k, v = _mix_in(hcur.reshape(t, d), g_mix[li][None], win, ln_v_g[li][None],
                              ln_v_b[li][None], w_spatial[li].astype(bf16), bs, gout[None, :D_A])
        h, m_rows, route_e, route_w = _attn_out(
            sink[li].astype(f32) * LOG2E, hcur, ya.reshape(b, s, D_A), q.reshape(b, s, D_B),
            k.reshape(b, s, BLK), v.reshape(b, s, BLK), bias, gout_b, wout,
            g_ffn[li][None], wr, br)

        r = R_EXP
        n_sort = t // TL_SORT
        max_rows = 2 * t + n_sort * N_EXPERTS * (RUN_ALIGN - 1)
        n_tiles = -(-max_rows // r) + N_EXPERTS
        assert n_tiles * r < 2 ** 16, "route_pos prefix sums hold row offsets in two bf16 bytes"
        slot, runs, offs = _route_pos(route_e, r)
        starts = offs[0, :, 0]
        padded = offs[1, :, 0]
        ends = starts + padded
        n_used = ends[-1] // r
        tile_idx = jnp.minimum(jnp.arange(n_tiles, dtype=i32), n_used - 1)
        tile_exp = jnp.minimum(
            jnp.sum((ends[None, :] <= (tile_idx * r)[:, None]).astype(i32), axis=1), N_EXPERTS - 1)
        tile_exp = tile_exp.astype(i32)
        first = jnp.concatenate([jnp.ones((1,), i32),
                                 (tile_exp[1:] != tile_exp[:-1]).astype(i32)])
        next_tile = ends[tile_exp] // r
        next_exp = jnp.where(next_tile < n_used, tile_exp[jnp.minimum(next_tile, n_tiles - 1)], -1)
        fetch_slot = (jnp.cumsum(first) - 1) % 2
        run_tiles = runs[:, :, :, 0].reshape(n_sort, 1, 3 * N_EXPERTS)
        xs, slot_t = _sort_rows(ends, padded, offs[2, :, 0], run_tiles, m_rows, slot, route_w,
                                n_tiles * r, r)
        ys = _experts(tile_exp, first, next_exp.astype(i32), fetch_slot.astype(i32),
                      n_used.reshape(1), xs, w_gate_e[li], w_up_e[li], w_down_e[li], r)

        out = _combine_out(run_tiles, h.reshape(t, d), slot_t,
                           p[li].reshape(t, PLE_DIM), ys, w_ple_gate[li].astype(bf16),
                           b_ple_gate[li][None], w_ple_proj[li].astype(bf16), g_ple[li][None],
                           g_final[None])
        hcur = out.reshape(b, s, d)
    return hcur
```

```python
import functools
import math

import jax
import jax.numpy as jnp
import numpy as np
from jax import lax
from jax.experimental import pallas as pl
from jax.experimental.pallas import tpu as pltpu

D_MODEL = 1024
D_A = 512
D_B = 512
BLK = 128
A_HEADS = 4
HEAD_DIM = 64
N_HEADS = 8
N_KV = 2
GQA = 4
WINDOW = 128
NUM_BUCKETS = 32
MAX_DIST = 128
D_IN = 2 * D_A + D_B + 2 * N_KV * HEAD_DIM
N_GROUPS = 4
E_PER_GROUP = 8
N_EXPERTS = 32
D_FF_E = 256
PLE_DIM = 256
EPS = 1e-6
NEG = -1e30
LOG2E = math.log2(math.e)

TM_MIX = 512
TQ_ATT = 512
TL_SORT = 512
RUN_ALIGN = 16
SMALL_BITS = 2
BIG_PIECE = RUN_ALIGN << SMALL_BITS
LB_SORT = 2 * TL_SORT + N_EXPERTS * RUN_ALIGN
R_EXP = 512
XS_COLS = D_MODEL + BLK
XS_RING = 4
XS_AHEAD = 2
ROUTE_ROWS = 8 + N_EXPERTS
VMEM_LIMIT = 48 * 1024 * 1024

f32 = jnp.float32
bf16 = jnp.bfloat16
i32 = jnp.int32


def _rms(x, g):
    return x * lax.rsqrt(jnp.mean(x * x, axis=-1, keepdims=True) + EPS) * g


def _gelu_tanh(x):
    c = math.sqrt(2.0 / math.pi)
    return x * (0.5 * (1.0 + jnp.tanh(c * (x + 0.044715 * (x * x * x)))))


def _bucket_table():
    n = NUM_BUCKETS // 2
    max_exact = n // 2
    i = np.arange(BLK)[:, None]
    j = np.arange(3 * BLK)[None, :]
    rel = j - BLK - i
    ret = np.where(rel > 0, n, 0)
    a = np.abs(rel)
    large = max_exact + (np.log(np.maximum(a, 1).astype(np.float64) / max_exact)
                         / math.log(MAX_DIST / max_exact) * (n - max_exact)).astype(np.int32)
    large = np.minimum(large, n - 1)
    bucket = ret + np.where(a < max_exact, a, large)
    return np.where(a <= WINDOW, bucket, -1).astype(np.int32)


def _bias_body(rb_ref, bucket_ref, o_ref):
    rows = 16
    for r0 in range(0, BLK, rows):
        bucket = bucket_ref[r0:r0 + rows, :]
        acc = [jnp.full(bucket.shape, NEG, f32) for _ in range(N_HEADS)]
        for b in range(NUM_BUCKETS):
            hit = bucket == b
            acc = [jnp.where(hit, rb_ref[b, h], acc[h]) for h in range(N_HEADS)]
        for h in range(N_HEADS):
            o_ref[h, r0:r0 + rows, :] = acc[h]


def _bias_table(rel_bias):
    bucket = jnp.asarray(_bucket_table())
    return pl.pallas_call(
        _bias_body,
        out_shape=jax.ShapeDtypeStruct((N_HEADS, BLK, 3 * BLK), f32),
        in_specs=[pl.BlockSpec(memory_space=pltpu.SMEM),
                  pl.BlockSpec((BLK, 3 * BLK), lambda: (0, 0))],
        out_specs=pl.BlockSpec((N_HEADS, BLK, 3 * BLK), lambda: (0, 0, 0)),
        name="bias_table",
    )(rel_bias.astype(f32), bucket)


def _mix_in_body(x_ref, gmix_ref, win_ref, lng_ref, lnb_ref, ws_ref, bs_ref, gout_ref,
                 ya_ref, q_ref, k_ref, v_ref):
    tm = x_ref.shape[0]
    nc = tm // BLK
    a = _rms(x_ref[...], gmix_ref[...])
    z = jnp.dot(a.astype(bf16), win_ref[...], preferred_element_type=f32)
    uv = _gelu_tanh(z[:, :2 * D_A])
    u = uv[:, :D_A]
    v = uv[:, D_A:]
    mu = jnp.mean(v, axis=-1, keepdims=True)
    vc = v - mu
    var = jnp.mean(vc * vc, axis=-1, keepdims=True)
    vn = (vc * lax.rsqrt(var + EPS) * lng_ref[...] + lnb_ref[...]).astype(bf16)
    cols = []
    for h in range(A_HEADS):
        rhs = jnp.concatenate(
            [vn[c * BLK:(c + 1) * BLK, h * BLK:(h + 1) * BLK] for c in range(nc)], axis=1)
        r = jnp.dot(ws_ref[h], rhs, preferred_element_type=f32)
        cols.append(jnp.concatenate(
            [r[:, c * BLK:(c + 1) * BLK] + bs_ref[h] for c in range(nc)], axis=0))
    sv = jnp.concatenate(cols, axis=1)
    ya_ref[...] = _rms(u * sv, gout_ref[...]).astype(bf16)
    qt = [z[:, 2 * D_A + j * BLK:2 * D_A + (j + 1) * BLK] for j in range(D_B // BLK)]
    low_half = lax.broadcasted_iota(i32, (tm, BLK), 1) < HEAD_DIM
    for g in range(GQA):
        a0, a1 = qt[g // 2], qt[GQA // 2 + g // 2]
        tile = (jnp.where(low_half, a0, pltpu.roll(a1, HEAD_DIM, 1)) if g % 2 == 0
                else jnp.where(low_half, pltpu.roll(a0, HEAD_DIM, 1), a1))
        q_ref[:, g * BLK:(g + 1) * BLK] = tile.astype(bf16)
    k_ref[...] = z[:, 2 * D_A + D_B:2 * D_A + D_B + BLK].astype(bf16)
    v_ref[...] = z[:, 2 * D_A + D_B + BLK:].astype(bf16)


def _mix_in(x2, gmix, win, lng, lnb, ws, bs, gout_a):
    t = x2.shape[0]
    tm = TM_MIX
    const = lambda shape: pl.BlockSpec(shape, lambda i: (0,) * len(shape))
    return pl.pallas_call(
        _mix_in_body,
        grid=(t // tm,),
        out_shape=(jax.ShapeDtypeStruct((t, D_A), bf16),
                   jax.ShapeDtypeStruct((t, D_B), bf16),
                   jax.ShapeDtypeStruct((t, BLK), bf16),
                   jax.ShapeDtypeStruct((t, BLK), bf16)),
        in_specs=[pl.BlockSpec((tm, D_MODEL), lambda i: (i, 0)),
                  const((1, D_MODEL)), const((D_MODEL, D_IN)),
                  const((1, D_A)), const((1, D_A)),
                  const((A_HEADS, BLK, BLK)), const((A_HEADS, BLK, BLK)),
                  const((1, D_A))],
        out_specs=(pl.BlockSpec((tm, D_A), lambda i: (i, 0)),
                   pl.BlockSpec((tm, D_B), lambda i: (i, 0)),
                   pl.BlockSpec((tm, BLK), lambda i: (i, 0)),
                   pl.BlockSpec((tm, BLK), lambda i: (i, 0))),
        compiler_params=pltpu.CompilerParams(
            dimension_semantics=("parallel",), vmem_limit_bytes=VMEM_LIMIT),
        name="mix_in",
    )(x2, gmix, win, lng, lnb, ws, bs, gout_a)


def _attn_epilogue(sink_ref, x_ref, ya_ref, q_ref, kp_ref, km_ref, kn_ref, vp_ref, vm_ref, vn_ref,
                   bias_ref, goutb_ref, wout_ref, gffn_ref, wr_ref, br_ref,
                   h_ref, m_ref, re_ref, rw_ref,
                   kf_ref, vf_ref, e_ref, yb_ref, *, seq_len):
    tq = x_ref.shape[1]
    ybn = _rms(yb_ref[1], goutb_ref[...]).astype(bf16)
    y = jnp.concatenate([ya_ref[0], ybn], axis=1)
    h = x_ref[0] + jnp.dot(y, wout_ref[...], preferred_element_type=f32)
    h_ref[0] = h
    m = _rms(h, gffn_ref[...])
    m_ref[...] = m.astype(bf16)

    logit_t = jnp.dot(m.astype(bf16), wr_ref[...], preferred_element_type=f32)
    logit = jnp.transpose(logit_t)[:ROUTE_ROWS] + br_ref[...]
    sub = lax.broadcasted_iota(i32, (8, tq), 0)
    lg = logit[0:8]
    mg = jnp.max(lg, axis=0, keepdims=True)
    pg_top = 1.0 / jnp.sum(jnp.exp(lg - mg), axis=0, keepdims=True)
    g_idx = jnp.min(jnp.where(lg == mg, sub, 8), axis=0, keepdims=True)
    sel = logit[8:16]
    for g in range(1, N_GROUPS):
        sel = jnp.where(g_idx == g, logit[8 + 8 * g:16 + 8 * g], sel)
    m1 = jnp.max(sel, axis=0, keepdims=True)
    i1 = jnp.min(jnp.where(sel == m1, sub, 8), axis=0, keepdims=True)
    sel2 = jnp.where(sub == i1, -jnp.inf, sel)
    m2 = jnp.max(sel2, axis=0, keepdims=True)
    i2 = jnp.min(jnp.where(sel2 == m2, sub, 8), axis=0, keepdims=True)
    r = jnp.exp(m2 - m1)
    w1 = pg_top / (1.0 + r)
    w2 = pg_top * r / (1.0 + r)
    e1 = g_idx * E_PER_GROUP + i1
    e2 = g_idx * E_PER_GROUP + i2
    re_ref[...] = jnp.where(sub == 0, e1, jnp.where(sub == 1, e2, 0))
    rw_ref[...] = jnp.where(sub == 0, w1, jnp.where(sub == 1, w2, 0.0))


def _attn_attention(sink_ref, x_ref, ya_ref, q_ref, kp_ref, km_ref, kn_ref, vp_ref, vm_ref, vn_ref,
                    bias_ref, goutb_ref, wout_ref, gffn_ref, wr_ref, br_ref,
                    h_ref, m_ref, re_ref, rw_ref,
                    kf_ref, vf_ref, e_ref, yb_ref, *, seq_len):
    tq = x_ref.shape[1]
    nb = tq // BLK
    ti = pl.program_id(0) % (seq_len // tq)
    kf_ref[0:BLK] = kp_ref[0]
    kf_ref[BLK:BLK + tq] = km_ref[0]
    kf_ref[BLK + tq:] = kn_ref[0]
    vlane = lax.broadcasted_iota(i32, (BLK, BLK), 1)
    for src, r0, r1 in ((vp_ref, 0, BLK), (vn_ref, BLK + tq, tq + 2 * BLK)) + tuple(
            (vm_ref.at[:, pl.ds(c * BLK, BLK), :], BLK + c * BLK, BLK + (c + 1) * BLK)
            for c in range(nb)):
        vals = src[0].astype(f32)
        vf_ref[0, r0:r1] = jnp.where(vlane < HEAD_DIM, vals,
                                     jnp.where(vlane == HEAD_DIM, 1.0, 0.0)).astype(bf16)
        vf_ref[1, r0:r1] = jnp.where(vlane >= HEAD_DIM, vals,
                                     jnp.where(vlane == 0, 1.0, 0.0)).astype(bf16)

    lane = lax.broadcasted_iota(i32, (BLK, BLK), 1)
    low_half = lane < HEAD_DIM

    for n in range(nb):
        r0 = n * BLK
        qb = q_ref[0, pl.ds(r0, BLK), :]
        kb = kf_ref[pl.ds(r0, 3 * BLK), :]
        zero = jnp.zeros((BLK, BLK), bf16)
        lhs = []
        for kh in range(N_KV):
            for g in range(GQA):
                tile = qb[:, g * BLK:(g + 1) * BLK]
                lhs.append(jnp.where(low_half, tile, zero) if kh == 0
                           else jnp.where(low_half, zero, tile))
        lhs = jnp.concatenate(lhs, axis=0)
        s_all = lax.dot_general(lhs, kb, (((1,), (1,)), ((), ())),
                                preferred_element_type=f32)
        blk = ti * nb + n
        pen = [jnp.where(blk == 0, NEG, 0.0) if n == 0 else None, None,
               jnp.where((blk + 1) * BLK >= seq_len, NEG, 0.0) if n == nb - 1 else None]
        sink_e = []
        for h in range(N_HEADS):
            bias_h = bias_ref[h]
            bias_h = jnp.concatenate(
                [bias_h[:, j * BLK:(j + 1) * BLK] if pen[j] is None
                 else bias_h[:, j * BLK:(j + 1) * BLK] + pen[j] for j in range(3)], axis=1)
            s = s_all[h * BLK:(h + 1) * BLK] + bias_h
            sk = sink_ref[h]
            mrow = jnp.maximum(jnp.max(s, axis=-1, keepdims=True), sk)
            e_ref[n, h * BLK:(h + 1) * BLK, :] = jnp.exp2(s - mrow).astype(bf16)
            sink_e.append(jnp.exp2(sk - mrow))
        half = GQA * BLK
        pv = [jnp.dot(e_ref[n, kh * half:(kh + 1) * half, :], vf_ref[kh, pl.ds(r0, 3 * BLK), :],
                      preferred_element_type=f32) for kh in range(N_KV)]

        def head_out(h):
            kh, g = divmod(h, GQA)
            rows = pv[kh][g * BLK:(g + 1) * BLK]
            ones_col = HEAD_DIM if kh == 0 else 0
            return rows * (1.0 / (rows[:, ones_col:ones_col + 1] + sink_e[h]))

        for g in range(GQA):
            yb_ref[0, pl.ds(r0, BLK), g * BLK:(g + 1) * BLK] = jnp.where(
                low_half, head_out(g), head_out(GQA + g))

    yb_ref[1] = yb_ref[0]


def _attn_out_body(*refs, seq_len):
    step = pl.program_id(0)
    n_tok_tiles = pl.num_programs(0) - 1

    @pl.when(step == 0)
    def _():
        _attn_attention(*refs, seq_len=seq_len)

    @pl.when((step > 0) & (step < n_tok_tiles))
    def _():
        _attn_epilogue(*refs, seq_len=seq_len)
        _attn_attention(*refs, seq_len=seq_len)

    @pl.when(step == n_tok_tiles)
    def _():
        _attn_epilogue(*refs, seq_len=seq_len)


def _attn_out(sink, x, ya, q, k, v, bias, goutb, wout, gffn, wr, br):
    b, s, _ = x.shape
    tq = TQ_ATT
    nb = tq // BLK
    nblk = s // BLK
    t = b * s
    nq = s // tq
    n_tok_tiles = b * nq
    att = lambda s: jnp.minimum(s, n_tok_tiles - 1)
    epi = lambda s: jnp.maximum(s - 1, 0)
    const = lambda shape: pl.BlockSpec(shape, lambda s, *_: (0,) * len(shape))
    tok_att = lambda w: pl.BlockSpec((1, tq, w), lambda s, *_: (att(s) // nq, att(s) % nq, 0))
    tok_epi = lambda w: pl.BlockSpec((1, tq, w), lambda s, *_: (epi(s) // nq, epi(s) % nq, 0))
    prev = pl.BlockSpec((1, BLK, BLK), lambda s, *_: (
        att(s) // nq, jnp.maximum((att(s) % nq) * nb - 1, 0), 0))
    nxt = pl.BlockSpec((1, BLK, BLK), lambda s, *_: (
        att(s) // nq, jnp.minimum((att(s) % nq) * nb + nb, nblk - 1), 0))
    rows = pl.BlockSpec((tq, D_MODEL), lambda s, *_: (epi(s), 0))
    lanes = lambda rows: pl.BlockSpec((rows, tq), lambda s, *_: (0, epi(s)))
    grid_spec = pltpu.PrefetchScalarGridSpec(
        num_scalar_prefetch=1,
        grid=(n_tok_tiles + 1,),
        in_specs=[tok_epi(D_MODEL), tok_epi(D_A), tok_att(D_B),
                  prev, tok_att(BLK), nxt, prev, tok_att(BLK), nxt,
                  const((N_HEADS, BLK, 3 * BLK)), const((1, D_B)), const((D_MODEL, D_MODEL)),
                  const((1, D_MODEL)), const((D_MODEL, BLK)), const((ROUTE_ROWS, tq))],
        out_specs=(tok_epi(D_MODEL), rows, lanes(8), lanes(8)),
        scratch_shapes=[pltpu.VMEM((tq + 2 * BLK, BLK), bf16),
                        pltpu.VMEM((N_KV, tq + 2 * BLK, BLK), bf16),
                        pltpu.VMEM((nb, N_HEADS * BLK, 3 * BLK), bf16),
                        pltpu.VMEM((2, tq, D_B), f32)])
    return pl.pallas_call(
        functools.partial(_attn_out_body, seq_len=s),
        grid_spec=grid_spec,
        out_shape=(jax.ShapeDtypeStruct((b, s, D_MODEL), f32),
                   jax.ShapeDtypeStruct((t, D_MODEL), bf16),
                   jax.ShapeDtypeStruct((8, t), i32),
                   jax.ShapeDtypeStruct((8, t), f32)),
        compiler_params=pltpu.CompilerParams(
            dimension_semantics=("arbitrary",), vmem_limit_bytes=VMEM_LIMIT),
        name="attn_out",
    )(sink, x, ya, q, k, k, k, v, v, v, bias, goutb, wout, gffn, wr, br)


def _exclusive_prefix(vals):
    a = lax.broadcasted_iota(i32, (N_EXPERTS, N_EXPERTS), 0)
    c = lax.broadcasted_iota(i32, (N_EXPERTS, N_EXPERTS), 1)
    low = jnp.where(c < a, 1.0, 0.0).astype(bf16)
    hi = (vals >> 8).astype(f32).astype(bf16)
    lo = (vals & 255).astype(f32).astype(bf16)
    return (jnp.dot(low, hi, preferred_element_type=f32) * 256.0
            + jnp.dot(low, lo, preferred_element_type=f32)).astype(i32)


def _round_up_pow2(vals, mult):
    log_m = mult.bit_length() - 1
    return ((vals + (mult - 1)) >> log_m) << log_m


def _route_pos_body(re_ref, slot_ref, runs_ref, offs_ref, tri_ref, *, rows_per_tile, tl):
    n_tok_tiles = re_ref.shape[1] // tl
    a = lax.broadcasted_iota(i32, (tl, tl), 0)
    c = lax.broadcasted_iota(i32, (tl, tl), 1)
    tri_ref[...] = jnp.where(a < c, 1.0, 0.0).astype(bf16)
    eid = lax.broadcasted_iota(i32, (N_EXPERTS, tl), 0)

    def tile_hits(j):
        cols = pl.ds(pl.multiple_of(j * tl, tl), tl)
        hit1 = eid == re_ref[0:1, cols]
        hit2 = eid == re_ref[1:2, cols]
        onehot = jnp.where(hit1 | hit2, 1.0, 0.0)
        cnt = jnp.sum(onehot, axis=1, keepdims=True).astype(i32)
        run_len = jnp.broadcast_to(_round_up_pow2(cnt, RUN_ALIGN), (N_EXPERTS, BLK))
        return cols, hit1, hit2, onehot, run_len

    total = lax.fori_loop(0, n_tok_tiles, lambda j, acc: acc + tile_hits(j)[4],
                          jnp.zeros((N_EXPERTS, BLK), i32))
    padded = _round_up_pow2(total, rows_per_tile)
    starts = _exclusive_prefix(padded)
    offs_ref[0] = starts
    offs_ref[1] = padded
    offs_ref[2] = total

    def place(j, global_off):
        cols, hit1, hit2, onehot, run_len = tile_hits(j)
        local_off = _exclusive_prefix(run_len)
        before = jnp.dot(onehot.astype(bf16), tri_ref[...], preferred_element_type=f32)
        slot = before + local_off[:, 0:1].astype(f32)
        s1 = jnp.sum(jnp.where(hit1, slot, 0.0), axis=0, keepdims=True)
        s2 = jnp.sum(jnp.where(hit2, slot, 0.0), axis=0, keepdims=True)
        sub = lax.broadcasted_iota(i32, (8, tl), 0)
        slot_ref[:, cols] = jnp.where(sub == 0, s1.astype(i32),
                                      jnp.where(sub == 1, s2.astype(i32), 0))
        runs_ref[j, 0] = local_off
        runs_ref[j, 1] = run_len
        runs_ref[j, 2] = global_off
        return global_off + run_len

    lax.fori_loop(0, n_tok_tiles, place, starts)


def _route_pos(route_e, rows_per_tile):
    t = route_e.shape[1]
    tl = TL_SORT
    return pl.pallas_call(
        functools.partial(_route_pos_body, rows_per_tile=rows_per_tile, tl=tl),
        out_shape=(jax.ShapeDtypeStruct((8, t), i32),
                   jax.ShapeDtypeStruct((t // tl, 3, N_EXPERTS, BLK), i32),
                   jax.ShapeDtypeStruct((3, N_EXPERTS, BLK), i32)),
        scratch_shapes=[pltpu.VMEM((tl, tl), bf16)],
        compiler_params=pltpu.CompilerParams(vmem_limit_bytes=VMEM_LIMIT),
        name="route_pos",
    )(route_e)


def _for_each_run_piece(runs_ref, fn):
    def per_expert(e, carry):
        local_off = runs_ref[0, 0, e]
        length = runs_ref[0, 0, N_EXPERTS + e]
        global_off = runs_ref[0, 0, 2 * N_EXPERTS + e]
        units = length >> (RUN_ALIGN.bit_length() - 1)

        def big_piece(k, inner):
            done = k * BIG_PIECE
            fn(pl.multiple_of(local_off + done, RUN_ALIGN),
               pl.multiple_of(global_off + done, RUN_ALIGN), BIG_PIECE)
            return inner

        lax.fori_loop(0, units >> SMALL_BITS, big_piece, 0)
        for b in range(SMALL_BITS):
            @pl.when(((units >> b) & 1) == 1)
            def _():
                done = ((units >> (b + 1)) << (b + 1)) * RUN_ALIGN
                fn(pl.multiple_of(local_off + done, RUN_ALIGN),
                   pl.multiple_of(global_off + done, RUN_ALIGN), RUN_ALIGN << b)
        return carry

    lax.fori_loop(0, N_EXPERTS, per_expert, 0)


def _tile_run_rows(runs_ref):
    return lax.fori_loop(0, N_EXPERTS, lambda e, acc: acc + runs_ref[0, 0, N_EXPERTS + e], 0)


def _wait_rows(total_rows, make_copy):
    units = total_rows >> (RUN_ALIGN.bit_length() - 1)
    for b in range((LB_SORT // RUN_ALIGN).bit_length()):
        @pl.when(((units >> b) & 1) == 1)
        def _():
            make_copy(RUN_ALIGN << b).wait()


def _sort_rows_body(ends_ref, padded_ref, total_ref, runs_ref, m_ref, slot_ref, rw_ref,
                    xs_ref, slot_t_ref,
                    local_ref, zero_ref, pending_ref, sem_ref, zsem_ref, *, rows_per_tile):
    tl = m_ref.shape[0]
    lb = local_ref.shape[1]
    step = pl.program_id(0)
    cur = step % 2
    n_tiles = xs_ref.shape[0] // rows_per_tile
    n_used = ends_ref[N_EXPERTS - 1] // rows_per_tile

    def zero_rows(start, rows, wait):
        cp = pltpu.make_async_copy(zero_ref.at[pl.ds(0, rows), :],
                                   xs_ref.at[pl.ds(pl.multiple_of(start, RUN_ALIGN), rows), :],
                                   zsem_ref)
        cp.wait() if wait else cp.start()

    def zero_fill(wait):
        for e in range(N_EXPERTS):
            pad_units = (padded_ref[e] - total_ref[e]) >> (RUN_ALIGN.bit_length() - 1)
            for bit in range((rows_per_tile // RUN_ALIGN).bit_length() - 1):
                @pl.when(((pad_units >> bit) & 1) == 1)
                def _():
                    done = ((pad_units >> (bit + 1)) << (bit + 1)) * RUN_ALIGN
                    zero_rows(ends_ref[e] - (padded_ref[e] - total_ref[e]) + done,
                              RUN_ALIGN << bit, wait)

        def spare(j, carry):
            zero_rows((n_used + j) * rows_per_tile, rows_per_tile, wait)
            return carry

        lax.fori_loop(0, n_tiles - n_used, spare, 0)

    @pl.when(step == 0)
    def _():
        zero_ref[...] = jnp.zeros(zero_ref.shape, bf16)
        zero_fill(wait=False)

    s1 = slot_ref[0:1, :]
    s2 = slot_ref[1:2, :]
    w1 = rw_ref[0:1, :]
    w2 = rw_ref[1:2, :]
    run_rows = _tile_run_rows(runs_ref)

    def sort_rows_upto(n):
        srow = lax.broadcasted_iota(i32, (n, tl), 0)
        p1 = srow == s1
        p2 = srow == s2
        perm = jnp.where(p1 | p2, 1.0, 0.0).astype(bf16)
        rows = jnp.dot(perm, m_ref[...], preferred_element_type=f32)
        w = jnp.sum(jnp.where(p1, w1, 0.0) + jnp.where(p2, w2, 0.0), axis=1, keepdims=True)
        w_hi = w.astype(bf16).astype(f32)
        w_lo = w - w_hi
        lane = lax.broadcasted_iota(i32, (n, BLK), 1)
        local_ref[cur, :n, :D_MODEL] = rows.astype(bf16)
        local_ref[cur, :n, D_MODEL:] = jnp.where(
            lane == 0, w_hi, jnp.where(lane == 1, w_lo, 0.0)).astype(bf16)
        sub_t = lax.broadcasted_iota(i32, (BLK, tl), 0)
        slots = jnp.where(sub_t == 0, s1.astype(f32), jnp.where(sub_t == 1, s2.astype(f32), 0.0))
        slot_t_ref[...] = jnp.transpose(slots)[:, :8].astype(i32)

    def piece(s, local_row, global_row, rows):
        return pltpu.make_async_copy(local_ref.at[s, pl.ds(local_row, rows), :],
                                     xs_ref.at[pl.ds(global_row, rows), :], sem_ref.at[s])

    short = lb - 256
    pl.when(run_rows <= short)(functools.partial(sort_rows_upto, short))
    pl.when(run_rows > short)(functools.partial(sort_rows_upto, lb))

    _for_each_run_piece(runs_ref, lambda l, g, n: piece(cur, l, g, n).start())

    @pl.when(step > 0)
    def _():
        _wait_rows(pending_ref[0], lambda n: piece(1 - cur, 0, 0, n))

    pending_ref[0] = run_rows

    @pl.when(step == pl.num_programs(0) - 1)
    def _():
        _wait_rows(pending_ref[0], lambda n: piece(cur, 0, 0, n))
        zero_fill(wait=True)


def _sort_rows(ends, padded, totals, run_tiles, m, slot, route_w, total_rows, rows_per_tile):
    t = m.shape[0]
    tl = TL_SORT
    grid_spec = pltpu.PrefetchScalarGridSpec(
        num_scalar_prefetch=3,
        grid=(t // tl,),
        in_specs=[pl.BlockSpec((1, 1, 3 * N_EXPERTS), lambda i, *_: (i, 0, 0),
                               memory_space=pltpu.SMEM),
                  pl.BlockSpec((tl, D_MODEL), lambda i, *_: (i, 0)),
                  pl.BlockSpec((8, tl), lambda i, *_: (0, i)),
                  pl.BlockSpec((8, tl), lambda i, *_: (0, i))],
        out_specs=(pl.BlockSpec(memory_space=pl.ANY),
                   pl.BlockSpec((tl, 8), lambda i, *_: (i, 0))),
        scratch_shapes=[pltpu.VMEM((2, LB_SORT, XS_COLS), bf16),
                        pltpu.VMEM((rows_per_tile, XS_COLS), bf16),
                        pltpu.SMEM((1,), i32),
                        pltpu.SemaphoreType.DMA((2,)),
                        pltpu.SemaphoreType.DMA(())])
    return pl.pallas_call(
        functools.partial(_sort_rows_body, rows_per_tile=rows_per_tile),
        grid_spec=grid_spec,
        out_shape=(jax.ShapeDtypeStruct((total_rows, XS_COLS), bf16),
                   jax.ShapeDtypeStruct((t, 8), i32)),
        compiler_params=pltpu.CompilerParams(
            dimension_semantics=("arbitrary",), vmem_limit_bytes=VMEM_LIMIT),
        name="sort_rows",
    )(ends, padded, totals, run_tiles, m, slot, route_w)


def _experts_body(texp_ref, first_ref, next_ref, slot_ref, nused_ref,
                  xs_hbm, wg_hbm, wu_hbm, wd_hbm, ys_hbm,
                  xbuf_ref, sg_ref, su_ref, sd_ref, wg_ref, wu_ref, wd_ref,
                  xsem_ref, osem_ref, sem_ref):
    i = pl.program_id(0)
    n_used = nused_ref[0]
    r = xbuf_ref.shape[1]

    def tile_rows(ref, tile):
        return ref.at[pl.ds(pl.multiple_of(tile * r, r), r), :]

    def fetch_tile(tile):
        s = tile % XS_RING
        return pltpu.make_async_copy(tile_rows(xs_hbm, tile), xbuf_ref.at[s], xsem_ref.at[s])

    def store_tile(tile):
        s = tile % XS_RING
        return pltpu.make_async_copy(xbuf_ref.at[s], tile_rows(ys_hbm, tile), osem_ref.at[s])

    @pl.when(i == 0)
    def _():
        for ahead in range(XS_AHEAD):
            @pl.when(ahead < n_used)
            def _():
                fetch_tile(ahead).start()

    @pl.when(i + XS_AHEAD < n_used)
    def _():
        @pl.when(i + XS_AHEAD >= XS_RING)
        def _():
            store_tile(i + XS_AHEAD - XS_RING).wait()

        fetch_tile(i + XS_AHEAD).start()

    def fetches(expert, s):
        return (pltpu.make_async_copy(wg_hbm.at[expert], sg_ref.at[s], sem_ref.at[s]),
                pltpu.make_async_copy(wu_hbm.at[expert], su_ref.at[s], sem_ref.at[s]),
                pltpu.make_async_copy(wd_hbm.at[expert], sd_ref.at[s], sem_ref.at[s]))

    @pl.when(i < n_used)
    def _():
        s = slot_ref[i]
        fetch_tile(i).wait()
        xs_ref = xbuf_ref.at[i % XS_RING]

        @pl.when(first_ref[i] == 1)
        def _():
            @pl.when(i == 0)
            def _():
                for cp in fetches(texp_ref[0], 0):
                    cp.start()

            for cp in fetches(texp_ref[i], s):
                cp.wait()

            @pl.when(next_ref[i] >= 0)
            def _():
                for cp in fetches(next_ref[i], 1 - s):
                    cp.start()

            wg_ref[...] = sg_ref[s].astype(bf16)
            wu_ref[...] = su_ref[s].astype(bf16)
            wd_ref[...] = sd_ref[s].astype(bf16)

        w_row = (xs_ref[:, D_MODEL:D_MODEL + 1].astype(f32)
                 + xs_ref[:, D_MODEL + 1:D_MODEL + 2].astype(f32))
        gate = jnp.dot(xs_ref[:, :D_MODEL], wg_ref[...], preferred_element_type=f32)
        up = jnp.dot(xs_ref[:, :D_MODEL], wu_ref[...], preferred_element_type=f32)
        hdn = (gate * jax.nn.sigmoid(gate)) * up
        y = jnp.dot(hdn.astype(bf16), wd_ref[...], preferred_element_type=f32)
        xs_ref[:, :D_MODEL] = (y * w_row).astype(bf16)
        store_tile(i).start()

    @pl.when(i == pl.num_programs(0) - 1)
    def _():
        for back in range(1, XS_RING + 1):
            @pl.when(n_used - back >= 0)
            def _():
                store_tile(n_used - back).wait()


def _experts(tile_exp, first, next_exp, slot, n_used, xs, wg, wu, wd, rows_per_tile):
    r = rows_per_tile
    n_tiles = xs.shape[0] // r
    grid_spec = pltpu.PrefetchScalarGridSpec(
        num_scalar_prefetch=5,
        grid=(n_tiles,),
        in_specs=[pl.BlockSpec(memory_space=pl.ANY),
                  pl.BlockSpec(memory_space=pl.ANY),
                  pl.BlockSpec(memory_space=pl.ANY),
                  pl.BlockSpec(memory_space=pl.ANY)],
        out_specs=pl.BlockSpec(memory_space=pl.ANY),
        scratch_shapes=[pltpu.VMEM((XS_RING, r, XS_COLS), bf16),
                        pltpu.VMEM((2, D_MODEL, D_FF_E), f32),
                        pltpu.VMEM((2, D_MODEL, D_FF_E), f32),
                        pltpu.VMEM((2, D_FF_E, D_MODEL), f32),
                        pltpu.VMEM((D_MODEL, D_FF_E), bf16),
                        pltpu.VMEM((D_MODEL, D_FF_E), bf16),
                        pltpu.VMEM((D_FF_E, D_MODEL), bf16),
                        pltpu.SemaphoreType.DMA((XS_RING,)),
                        pltpu.SemaphoreType.DMA((XS_RING,)),
                        pltpu.SemaphoreType.DMA((2,))])
    return pl.pallas_call(
        _experts_body,
        grid_spec=grid_spec,
        out_shape=jax.ShapeDtypeStruct(xs.shape, xs.dtype),
        input_output_aliases={5: 0},
        compiler_params=pltpu.CompilerParams(
            dimension_semantics=("arbitrary",), vmem_limit_bytes=VMEM_LIMIT),
        name="experts",
    )(tile_exp, first, next_exp, slot, n_used, xs, wg, wu, wd)


def _combine_out_body(runs_cur_ref, runs_nxt_ref, h_ref, slot_ref, p_ref, ys_ref,
                      wpg_ref, bpg_ref, wpp_ref, gple_ref, gfin_ref,
                      o_ref, ybuf_ref, moe_ref, sem_ref):
    tk = h_ref.shape[0]
    lb = ybuf_ref.shape[1]
    step = pl.program_id(0)
    n = pl.num_programs(0) - 1
    cur = jnp.minimum(step, n - 1) % 2

    def piece(s, local_row, global_row, rows):
        return pltpu.make_async_copy(ys_ref.at[pl.ds(global_row, rows), pl.ds(0, D_MODEL)],
                                     ybuf_ref.at[s, pl.ds(local_row, rows), :], sem_ref.at[s])

    @pl.when(step == 0)
    def _():
        ybuf_ref[...] = jnp.zeros(ybuf_ref.shape, bf16)
        moe_ref[...] = jnp.zeros(moe_ref.shape, f32)
        _for_each_run_piece(runs_cur_ref, lambda l, g, r: piece(0, l, g, r).start())

    @pl.when(step + 1 < n)
    def _():
        _for_each_run_piece(runs_nxt_ref, lambda l, g, r: piece(1 - cur, l, g, r).start())

    run_rows = _tile_run_rows(runs_cur_ref)

    @pl.when(step < n)
    def _():
        _wait_rows(run_rows, lambda r: piece(cur, 0, 0, r))

    def work(rows):
        h2 = h_ref[...] + moe_ref[...]
        gate = jax.nn.sigmoid(jnp.dot(h2.astype(bf16), wpg_ref[...], preferred_element_type=f32)
                              + bpg_ref[...])
        pp = jnp.dot(p_ref[...].astype(bf16), wpp_ref[...], preferred_element_type=f32)
        h3 = h2 + gate * _rms(pp, gple_ref[...])
        o_ref[...] = _rms(h3, gfin_ref[...])
        col = lax.broadcasted_iota(i32, (tk, rows), 1)
        unperm = jnp.where((col == slot_ref[:, 0:1]) | (col == slot_ref[:, 1:2]), 1.0, 0.0)
        moe_ref[...] = jnp.dot(unperm.astype(bf16), ybuf_ref[cur, :rows, :],
                               preferred_element_type=f32)

    short = lb - 256
    pl.when(run_rows <= short)(functools.partial(work, short))
    pl.when(run_rows > short)(functools.partial(work, lb))


def _combine_out(run_tiles, h2d, slot_t, p2d, ys, wpg, bpg, wpp, gple, gfin):
    t = h2d.shape[0]
    tk = TL_SORT
    nt = t // tk
    const = lambda shape: pl.BlockSpec(shape, lambda i: (0,) * len(shape))
    runs = lambda imap: pl.BlockSpec((1, 1, 3 * N_EXPERTS), imap, memory_space=pltpu.SMEM)
    uns = lambda s: jnp.minimum(s, nt - 1)
    fin = lambda s: jnp.maximum(s - 1, 0)
    return pl.pallas_call(
        _combine_out_body,
        grid=(nt + 1,),
        out_shape=jax.ShapeDtypeStruct((t, D_MODEL), f32),
        in_specs=[runs(lambda s: (uns(s), 0, 0)),
                  runs(lambda s: (uns(s + 1), 0, 0)),
                  pl.BlockSpec((tk, D_MODEL), lambda s: (fin(s), 0)),
                  pl.BlockSpec((tk, 8), lambda s: (uns(s), 0)),
                  pl.BlockSpec((tk, PLE_DIM), lambda s: (fin(s), 0)),
                  pl.BlockSpec(memory_space=pl.ANY),
                  const((D_MODEL, D_MODEL)), const((1, D_MODEL)),
                  const((PLE_DIM, D_MODEL)), const((1, D_MODEL)), const((1, D_MODEL))],
        out_specs=pl.BlockSpec((tk, D_MODEL), lambda s: (fin(s), 0)),
        scratch_shapes=[pltpu.VMEM((2, LB_SORT, D_MODEL), bf16),
                        pltpu.VMEM((tk, D_MODEL), f32),
                        pltpu.SemaphoreType.DMA((2,))],
        compiler_params=pltpu.CompilerParams(
            dimension_semantics=("arbitrary",), vmem_limit_bytes=VMEM_LIMIT),
        name="combine_out",
    )(run_tiles, run_tiles, h2d, slot_t, p2d, ys, wpg, bpg, wpp, gple, gfin)


def _head_major_to_tile_major(a, axis):
    axis = axis % a.ndim
    shape = a.shape
    a = a.reshape(shape[:axis] + (N_KV, GQA, HEAD_DIM) + shape[axis + 1:])
    return jnp.swapaxes(a, axis, axis + 1).reshape(shape)


def kernel(x, p, rel_bias, g_mix, w_in, ln_v_g, ln_v_b, w_spatial, b_spatial, sink, g_out_grp, w_out,
           g_ffn, w_router_group, b_router_group, w_router_expert, b_router_expert, w_gate_e, w_up_e,
           w_down_e, w_ple_proj, g_ple, w_ple_gate, b_ple_gate, g_final):
    b, s, d = x.shape
    t = b * s
    depth = g_mix.shape[0]
    assert depth == 1 and d == D_MODEL
    c1, c2 = 2 * D_A, 2 * D_A + D_B
    bias = _bias_table(rel_bias * LOG2E)
    hcur = x.astype(f32)
    for li in range(depth):
        col_scale = np.ones((1, D_IN), np.float32)
        col_scale[:, c1:c2] = LOG2E * HEAD_DIM ** -0.5
        win = (w_in[li] * col_scale).astype(bf16)
        bs = jnp.broadcast_to(b_spatial[li][:, :, None], (A_HEADS, BLK, BLK)).astype(f32)
        gout = g_out_grp[li]
        gout_b = _head_major_to_tile_major(gout[D_A:], 0)[None]
        wo = w_out[li]
        wout = jnp.concatenate([wo[:D_A], _head_major_to_tile_major(wo[D_A:], 0)],
                               axis=0).astype(bf16)
        wr = jnp.concatenate([
            w_router_group[li], jnp.zeros((D_MODEL, 8 - N_GROUPS), f32), w_router_expert[li],
            jnp.zeros((D_MODEL, BLK - ROUTE_ROWS), f32)], axis=1).astype(bf16)
        br = jnp.concatenate([b_router_group[li], jnp.full((8 - N_GROUPS,), NEG, f32),
                              b_router_expert[li]])
        br = jnp.broadcast_to(br[:, None], (ROUTE_ROWS, TQ_ATT))

        ya, q, k, v = _mix_in(hcur.reshape(t, d), g_mix[li][None], win, ln_v_g[li][None],
                              ln_v_b[li][None], w_spatial[li].astype(bf16), bs, gout[None, :D_A])
        h, m_rows, route_e, route_w = _attn_out(
            sink[li].astype(f32) * LOG2E, hcur, ya.reshape(b, s, D_A), q.reshape(b, s, D_B),
            k.reshape(b, s, BLK), v.reshape(b, s, BLK), bias, gout_b, wout,
            g_ffn[li][None], wr, br)

        r = R_EXP
        n_sort = t // TL_SORT
        max_rows = 2 * t + n_sort * N_EXPERTS * (RUN_ALIGN - 1)
        n_tiles = -(-max_rows // r) + N_EXPERTS
        assert n_tiles * r < 2 ** 16, "route_pos prefix sums hold row offsets in two bf16 bytes"
        slot, runs, offs = _route_pos(route_e, r)
        starts = offs[0, :, 0]
        padded = offs[1, :, 0]
        ends = starts + padded
        n_used = ends[-1] // r
        tile_idx = jnp.minimum(jnp.arange(n_tiles, dtype=i32), n_used - 1)
        tile_exp = jnp.minimum(
            jnp.sum((ends[None, :] <= (tile_idx * r)[:, None]).astype(i32), axis=1), N_EXPERTS - 1)
        tile_exp = tile_exp.astype(i32)
        first = jnp.concatenate([jnp.ones((1,), i32),
                                 (tile_exp[1:] != tile_exp[:-1]).astype(i32)])
        next_tile = ends[tile_exp] // r
        next_exp = jnp.where(next_tile < n_used, tile_exp[jnp.minimum(next_tile, n_tiles - 1)], -1)
        fetch_slot = (jnp.cumsum(first) - 1) % 2
        run_tiles = runs[:, :, :, 0].reshape(n_sort, 1, 3 * N_EXPERTS)
        xs, slot_t = _sort_rows(ends, padded, offs[2, :, 0], run_tiles, m_rows, slot, route_w,
                                n_tiles * r, r)
        ys = _experts(tile_exp, first, next_exp.astype(i32), fetch_slot.astype(i32),
                      n_used.reshape(1), xs, w_gate_e[li], w_up_e[li], w_down_e[li], r)

        out = _combine_out(run_tiles, h.reshape(t, d), slot_t,
                           p[li].reshape(t, PLE_DIM), ys, w_ple_gate[li].astype(bf16),
                           b_ple_gate[li][None], w_ple_proj[li].astype(bf16), g_ple[li][None],
                           g_final[None])
        hcur = out.reshape(b, s, d)
    return hcur
```

```python
import functools
import math

import jax
import jax.numpy as jnp
import numpy as np
from jax import lax
from jax.experimental import pallas as pl
from jax.experimental.pallas import tpu as pltpu

D_MODEL = 1024
D_A = 512
D_B = 512
BLK = 128
A_HEADS = 4
HEAD_DIM = 64
N_HEADS = 8
N_KV = 2
GQA = 4
WINDOW = 128
NUM_BUCKETS = 32
MAX_DIST = 128
D_IN = 2 * D_A + D_B + 2 * N_KV * HEAD_DIM
N_GROUPS = 4
E_PER_GROUP = 8
N_EXPERTS = 32
D_FF_E = 256
PLE_DIM = 256
EPS = 1e-6
NEG = -1e30
LOG2E = math.log2(math.e)

TM_MIX = 512
TQ_ATT = 512
TL_SORT = 512
RUN_ALIGN = 16
SMALL_BITS = 2
BIG_PIECE = RUN_ALIGN << SMALL_BITS
LB_SORT = 2 * TL_SORT + N_EXPERTS * RUN_ALIGN
R_EXP = 512
XS_COLS = D_MODEL + BLK
XS_RING = 4
XS_AHEAD = 2
ROUTE_ROWS = 8 + N_EXPERTS
VMEM_LIMIT = 48 * 1024 * 1024

f32 = jnp.float32
bf16 = jnp.bfloat16
i32 = jnp.int32


def _rms(x, g):
    return x * lax.rsqrt(jnp.mean(x * x, axis=-1, keepdims=True) + EPS) * g


def _gelu_tanh(x):
    c = math.sqrt(2.0 / math.pi)
    return x * (0.5 * (1.0 + jnp.tanh(c * (x + 0.044715 * (x * x * x)))))


def _bucket_table():
    n = NUM_BUCKETS // 2
    max_exact = n // 2
    i = np.arange(BLK)[:, None]
    j = np.arange(3 * BLK)[None, :]
    rel = j - BLK - i
    ret = np.where(rel > 0, n, 0)
    a = np.abs(rel)
    large = max_exact + (np.log(np.maximum(a, 1).astype(np.float64) / max_exact)
                         / math.log(MAX_DIST / max_exact) * (n - max_exact)).astype(np.int32)
    large = np.minimum(large, n - 1)
    bucket = ret + np.where(a < max_exact, a, large)
    return np.where(a <= WINDOW, bucket, -1).astype(np.int32)


def _bias_body(rb_ref, bucket_ref, o_ref):
    rows = 16
    for r0 in range(0, BLK, rows):
        bucket = bucket_ref[r0:r0 + rows, :]
        acc = [jnp.full(bucket.shape, NEG, f32) for _ in range(N_HEADS)]
        for b in range(NUM_BUCKETS):
            hit = bucket == b
            acc = [jnp.where(hit, rb_ref[b, h], acc[h]) for h in range(N_HEADS)]
        for h in range(N_HEADS):
            o_ref[h, r0:r0 + rows, :] = acc[h]


def _bias_table(rel_bias):
    bucket = jnp.asarray(_bucket_table())
    return pl.pallas_call(
        _bias_body,
        out_shape=jax.ShapeDtypeStruct((N_HEADS, BLK, 3 * BLK), f32),
        in_specs=[pl.BlockSpec(memory_space=pltpu.SMEM),
                  pl.BlockSpec((BLK, 3 * BLK), lambda: (0, 0))],
        out_specs=pl.BlockSpec((N_HEADS, BLK, 3 * BLK), lambda: (0, 0, 0)),
        name="bias_table",
    )(rel_bias.astype(f32), bucket)


def _mix_in_body(x_ref, gmix_ref, win_ref, lng_ref, lnb_ref, ws_ref, bs_ref, gout_ref,
                 ya_ref, q_ref, k_ref, v_ref):
    tm = x_ref.shape[0]
    nc = tm // BLK
    a = _rms(x_ref[...], gmix_ref[...])
    z = jnp.dot(a.astype(bf16), win_ref[...], preferred_element_type=f32)
    uv = _gelu_tanh(z[:, :2 * D_A])
    u = uv[:, :D_A]
    v = uv[:, D_A:]
    mu = jnp.mean(v, axis=-1, keepdims=True)
    vc = v - mu
    var = jnp.mean(vc * vc, axis=-1, keepdims=True)
    vn = (vc * lax.rsqrt(var + EPS) * lng_ref[...] + lnb_ref[...]).astype(bf16)
    cols = []
    for h in range(A_HEADS):
        rhs = jnp.concatenate(
            [vn[c * BLK:(c + 1) * BLK, h * BLK:(h + 1) * BLK] for c in range(nc)], axis=1)
        r = jnp.dot(ws_ref[h], rhs, preferred_element_type=f32)
        cols.append(jnp.concatenate(
            [r[:, c * BLK:(c + 1) * BLK] + bs_ref[h] for c in range(nc)], axis=0))
    sv = jnp.concatenate(cols, axis=1)
    ya_ref[...] = _rms(u * sv, gout_ref[...]).astype(bf16)
    qt = [z[:, 2 * D_A + j * BLK:2 * D_A + (j + 1) * BLK] for j in range(D_B // BLK)]
    low_half = lax.broadcasted_iota(i32, (tm, BLK), 1) < HEAD_DIM
    for g in range(GQA):
        a0, a1 = qt[g // 2], qt[GQA // 2 + g // 2]
        tile = (jnp.where(low_half, a0, pltpu.roll(a1, HEAD_DIM, 1)) if g % 2 == 0
                else jnp.where(low_half, pltpu.roll(a0, HEAD_DIM, 1), a1))
        q_ref[:, g * BLK:(g + 1) * BLK] = tile.astype(bf16)
    k_ref[...] = z[:, 2 * D_A + D_B:2 * D_A + D_B + BLK].astype(bf16)
    v_ref[...] = z[:, 2 * D_A + D_B + BLK:].astype(bf16)


def _mix_in(x2, gmix, win, lng, lnb, ws, bs, gout_a):
    t = x2.shape[0]
    tm = TM_MIX
    const = lambda shape: pl.BlockSpec(shape, lambda i: (0,) * len(shape))
    return pl.pallas_call(
        _mix_in_body,
        grid=(t // tm,),
        out_shape=(jax.ShapeDtypeStruct((t, D_A), bf16),
                   jax.ShapeDtypeStruct((t, D_B), bf16),
                   jax.ShapeDtypeStruct((t, BLK), bf16),
                   jax.ShapeDtypeStruct((t, BLK), bf16)),
        in_specs=[pl.BlockSpec((tm, D_MODEL), lambda i: (i, 0)),
                  const((1, D_MODEL)), const((D_MODEL, D_IN)),
                  const((1, D_A)), const((1, D_A)),
                  const((A_HEADS, BLK, BLK)), const((A_HEADS, BLK, BLK)),
                  const((1, D_A))],
        out_specs=(pl.BlockSpec((tm, D_A), lambda i: (i, 0)),
                   pl.BlockSpec((tm, D_B), lambda i: (i, 0)),
                   pl.BlockSpec((tm, BLK), lambda i: (i, 0)),
                   pl.BlockSpec((tm, BLK), lambda i: (i, 0))),
        compiler_params=pltpu.CompilerParams(
            dimension_semantics=("parallel",), vmem_limit_bytes=VMEM_LIMIT),
        name="mix_in",
    )(x2, gmix, win, lng, lnb, ws, bs, gout_a)


def _attn_epilogue(sink_ref, x_ref, ya_ref, q_ref, kp_ref, km_ref, kn_ref, vp_ref, vm_ref, vn_ref,
                   bias_ref, goutb_ref, wout_ref, gffn_ref, wr_ref, br_ref,
                   h_ref, m_ref, re_ref, rw_ref,
                   kf_ref, vf_ref, e_ref, yb_ref, *, seq_len):
    tq = x_ref.shape[1]
    ybn = _rms(yb_ref[1], goutb_ref[...]).astype(bf16)
    y = jnp.concatenate([ya_ref[0], ybn], axis=1)
    h = x_ref[0] + jnp.dot(y, wout_ref[...], preferred_element_type=f32)
    h_ref[0] = h
    m = _rms(h, gffn_ref[...])
    m_ref[...] = m.astype(bf16)

    logit_t = jnp.dot(m.astype(bf16), wr_ref[...], preferred_element_type=f32)
    logit = jnp.transpose(logit_t)[:ROUTE_ROWS] + br_ref[...]
    sub = lax.broadcasted_iota(i32, (8, tq), 0)
    lg = logit[0:8]
    mg = jnp.max(lg, axis=0, keepdims=True)
    pg_top = 1.0 / jnp.sum(jnp.exp(lg - mg), axis=0, keepdims=True)
    g_idx = jnp.min(jnp.where(lg == mg, sub, 8), axis=0, keepdims=True)
    sel = logit[8:16]
    for g in range(1, N_GROUPS):
        sel = jnp.where(g_idx == g, logit[8 + 8 * g:16 + 8 * g], sel)
    m1 = jnp.max(sel, axis=0, keepdims=True)
    i1 = jnp.min(jnp.where(sel == m1, sub, 8), axis=0, keepdims=True)
    sel2 = jnp.where(sub == i1, -jnp.inf, sel)
    m2 = jnp.max(sel2, axis=0, keepdims=True)
    i2 = jnp.min(jnp.where(sel2 == m2, sub, 8), axis=0, keepdims=True)
    r = jnp.exp(m2 - m1)
    w1 = pg_top / (1.0 + r)
    w2 = pg_top * r / (1.0 + r)
    e1 = g_idx * E_PER_GROUP + i1
    e2 = g_idx * E_PER_GROUP + i2
    re_ref[...] = jnp.where(sub == 0, e1, jnp.where(sub == 1, e2, 0))
    rw_ref[...] = jnp.where(sub == 0, w1, jnp.where(sub == 1, w2, 0.0))


def _attn_attention(sink_ref, x_ref, ya_ref, q_ref, kp_ref, km_ref, kn_ref, vp_ref, vm_ref, vn_ref,
                    bias_ref, goutb_ref, wout_ref, gffn_ref, wr_ref, br_ref,
                    h_ref, m_ref, re_ref, rw_ref,
                    kf_ref, vf_ref, e_ref, yb_ref, *, seq_len):
    tq = x_ref.shape[1]
    nb = tq // BLK
    ti = pl.program_id(0) % (seq_len // tq)
    kf_ref[0:BLK] = kp_ref[0]
    kf_ref[BLK:BLK + tq] = km_ref[0]
    kf_ref[BLK + tq:] = kn_ref[0]
    vlane = lax.broadcasted_iota(i32, (BLK, BLK), 1)
    for src, r0, r1 in ((vp_ref, 0, BLK), (vn_ref, BLK + tq, tq + 2 * BLK)) + tuple(
            (vm_ref.at[:, pl.ds(c * BLK, BLK), :], BLK + c * BLK, BLK + (c + 1) * BLK)
            for c in range(nb)):
        vals = src[0].astype(f32)
        vf_ref[0, r0:r1] = jnp.where(vlane < HEAD_DIM, vals,
                                     jnp.where(vlane == HEAD_DIM, 1.0, 0.0)).astype(bf16)
        vf_ref[1, r0:r1] = jnp.where(vlane >= HEAD_DIM, vals,
                                     jnp.where(vlane == 0, 1.0, 0.0)).astype(bf16)

    lane = lax.broadcasted_iota(i32, (BLK, BLK), 1)
    low_half = lane < HEAD_DIM

    for n in range(nb):
        r0 = n * BLK
        qb = q_ref[0, pl.ds(r0, BLK), :]
        kb = kf_ref[pl.ds(r0, 3 * BLK), :]
        zero = jnp.zeros((BLK, BLK), bf16)
        lhs = []
        for kh in range(N_KV):
            for g in range(GQA):
                tile = qb[:, g * BLK:(g + 1) * BLK]
                lhs.append(jnp.where(low_half, tile, zero) if kh == 0
                           else jnp.where(low_half, zero, tile))
        lhs = jnp.concatenate(lhs, axis=0)
        s_all = lax.dot_general(lhs, kb, (((1,), (1,)), ((), ())),
                                preferred_element_type=f32)
        blk = ti * nb + n
        pen = [jnp.where(blk == 0, NEG, 0.0) if n == 0 else None, None,
               jnp.where((blk + 1) * BLK >= seq_len, NEG, 0.0) if n == nb - 1 else None]
        sink_e = []
        for h in range(N_HEADS):
            bias_h = bias_ref[h]
            bias_h = jnp.concatenate(
                [bias_h[:, j * BLK:(j + 1) * BLK] if pen[j] is None
                 else bias_h[:, j * BLK:(j + 1) * BLK] + pen[j] for j in range(3)], axis=1)
            s = s_all[h * BLK:(h + 1) * BLK] + bias_h
            sk = sink_ref[h]
            mrow = jnp.maximum(jnp.max(s, axis=-1, keepdims=True), sk)
            e_ref[n, h * BLK:(h + 1) * BLK, :] = jnp.exp2(s - mrow).astype(bf16)
            sink_e.append(jnp.exp2(sk - mrow))
        half = GQA * BLK
        pv = [jnp.dot(e_ref[n, kh * half:(kh + 1) * half, :], vf_ref[kh, pl.ds(r0, 3 * BLK), :],
                      preferred_element_type=f32) for kh in range(N_KV)]

        def head_out(h):
            kh, g = divmod(h, GQA)
            rows = pv[kh][g * BLK:(g + 1) * BLK]
            ones_col = HEAD_DIM if kh == 0 else 0
            return rows * (1.0 / (rows[:, ones_col:ones_col + 1] + sink_e[h]))

        for g in range(GQA):
            yb_ref[0, pl.ds(r0, BLK), g * BLK:(g + 1) * BLK] = jnp.where(
                low_half, head_out(g), head_out(GQA + g))

    yb_ref[1] = yb_ref[0]


def _attn_out_body(*refs, seq_len):
    step = pl.program_id(0)
    n_tok_tiles = pl.num_programs(0) - 1

    @pl.when(step == 0)
    def _():
        _attn_attention(*refs, seq_len=seq_len)

    @pl.when((step > 0) & (step < n_tok_tiles))
    def _():
        _attn_epilogue(*refs, seq_len=seq_len)
        _attn_attention(*refs, seq_len=seq_len)

    @pl.when(step == n_tok_tiles)
    def _():
        _attn_epilogue(*refs, seq_len=seq_len)


def _attn_out(sink, x, ya, q, k, v, bias, goutb, wout, gffn, wr, br):
    b, s, _ = x.shape
    tq = TQ_ATT
    nb = tq // BLK
    nblk = s // BLK
    t = b * s
    nq = s // tq
    n_tok_tiles = b * nq
    att = lambda s: jnp.minimum(s, n_tok_tiles - 1)
    epi = lambda s: jnp.maximum(s - 1, 0)
    const = lambda shape: pl.BlockSpec(shape, lambda s, *_: (0,) * len(shape))
    tok_att = lambda w: pl.BlockSpec((1, tq, w), lambda s, *_: (att(s) // nq, att(s) % nq, 0))
    tok_epi = lambda w: pl.BlockSpec((1, tq, w), lambda s, *_: (epi(s) // nq, epi(s) % nq, 0))
    prev = pl.BlockSpec((1, BLK, BLK), lambda s, *_: (
        att(s) // nq, jnp.maximum((att(s) % nq) * nb - 1, 0), 0))
    nxt = pl.BlockSpec((1, BLK, BLK), lambda s, *_: (
        att(s) // nq, jnp.minimum((att(s) % nq) * nb + nb, nblk - 1), 0))
    rows = pl.BlockSpec((tq, D_MODEL), lambda s, *_: (epi(s), 0))
    lanes = lambda rows: pl.BlockSpec((rows, tq), lambda s, *_: (0, epi(s)))
    grid_spec = pltpu.PrefetchScalarGridSpec(
        num_scalar_prefetch=1,
        grid=(n_tok_tiles + 1,),
        in_specs=[tok_epi(D_MODEL), tok_epi(D_A), tok_att(D_B),
                  prev, tok_att(BLK), nxt, prev, tok_att(BLK), nxt,
                  const((N_HEADS, BLK, 3 * BLK)), const((1, D_B)), const((D_MODEL, D_MODEL)),
                  const((1, D_MODEL)), const((D_MODEL, BLK)), const((ROUTE_ROWS, tq))],
        out_specs=(tok_epi(D_MODEL), rows, lanes(8), lanes(8)),
        scratch_shapes=[pltpu.VMEM((tq + 2 * BLK, BLK), bf16),
                        pltpu.VMEM((N_KV, tq + 2 * BLK, BLK), bf16),
                        pltpu.VMEM((nb, N_HEADS * BLK, 3 * BLK), bf16),
                        pltpu.VMEM((2, tq, D_B), f32)])
    return pl.pallas_call(
        functools.partial(_attn_out_body, seq_len=s),
        grid_spec=grid_spec,
        out_shape=(jax.ShapeDtypeStruct((b, s, D_MODEL), f32),
                   jax.ShapeDtypeStruct((t, D_MODEL), bf16),
                   jax.ShapeDtypeStruct((8, t), i32),
                   jax.ShapeDtypeStruct((8, t), f32)),
        compiler_params=pltpu.CompilerParams(
            dimension_semantics=("arbitrary",), vmem_limit_bytes=VMEM_LIMIT),
        name="attn_out",
    )(sink, x, ya, q, k, k, k, v, v, v, bias, goutb, wout, gffn, wr, br)


def _exclusive_prefix(vals):
    a = lax.broadcasted_iota(i32, (N_EXPERTS, N_EXPERTS), 0)
    c = lax.broadcasted_iota(i32, (N_EXPERTS, N_EXPERTS), 1)
    low = jnp.where(c < a, 1.0, 0.0).astype(bf16)
    hi = (vals >> 8).astype(f32).astype(bf16)
    lo = (vals & 255).astype(f32).astype(bf16)
    return (jnp.dot(low, hi, preferred_element_type=f32) * 256.0
            + jnp.dot(low, lo, preferred_element_type=f32)).astype(i32)


def _round_up_pow2(vals, mult):
    log_m = mult.bit_length() - 1
    return ((vals + (mult - 1)) >> log_m) << log_m


def _route_pos_body(re_ref, slot_ref, runs_ref, offs_ref, tri_ref, *, rows_per_tile, tl):
    n_tok_tiles = re_ref.shape[1] // tl
    a = lax.broadcasted_iota(i32, (tl, tl), 0)
    c = lax.broadcasted_iota(i32, (tl, tl), 1)
    tri_ref[...] = jnp.where(a < c, 1.0, 0.0).astype(bf16)
    eid = lax.broadcasted_iota(i32, (N_EXPERTS, tl), 0)

    def tile_hits(j):
        cols = pl.ds(pl.multiple_of(j * tl, tl), tl)
        hit1 = eid == re_ref[0:1, cols]
        hit2 = eid == re_ref[1:2, cols]
        onehot = jnp.where(hit1 | hit2, 1.0, 0.0)
        cnt = jnp.sum(onehot, axis=1, keepdims=True).astype(i32)
        run_len = jnp.broadcast_to(_round_up_pow2(cnt, RUN_ALIGN), (N_EXPERTS, BLK))
        return cols, hit1, hit2, onehot, run_len

    total = lax.fori_loop(0, n_tok_tiles, lambda j, acc: acc + tile_hits(j)[4],
                          jnp.zeros((N_EXPERTS, BLK), i32))
    padded = _round_up_pow2(total, rows_per_tile)
    starts = _exclusive_prefix(padded)
    offs_ref[0] = starts
    offs_ref[1] = padded
    offs_ref[2] = total

    def place(j, global_off):
        cols, hit1, hit2, onehot, run_len = tile_hits(j)
        local_off = _exclusive_prefix(run_len)
        before = jnp.dot(onehot.astype(bf16), tri_ref[...], preferred_element_type=f32)
        slot = before + local_off[:, 0:1].astype(f32)
        s1 = jnp.sum(jnp.where(hit1, slot, 0.0), axis=0, keepdims=True)
        s2 = jnp.sum(jnp.where(hit2, slot, 0.0), axis=0, keepdims=True)
        sub = lax.broadcasted_iota(i32, (8, tl), 0)
        slot_ref[:, cols] = jnp.where(sub == 0, s1.astype(i32),
                                      jnp.where(sub == 1, s2.astype(i32), 0))
        runs_ref[j, 0] = local_off
        runs_ref[j, 1] = run_len
        runs_ref[j, 2] = global_off
        return global_off + run_len

    lax.fori_loop(0, n_tok_tiles, place, starts)


def _route_pos(route_e, rows_per_tile):
    t = route_e.shape[1]
    tl = TL_SORT
    return pl.pallas_call(
        functools.partial(_route_pos_body, rows_per_tile=rows_per_tile, tl=tl),
        out_shape=(jax.ShapeDtypeStruct((8, t), i32),
                   jax.ShapeDtypeStruct((t // tl, 3, N_EXPERTS, BLK), i32),
                   jax.ShapeDtypeStruct((3, N_EXPERTS, BLK), i32)),
        scratch_shapes=[pltpu.VMEM((tl, tl), bf16)],
        compiler_params=pltpu.CompilerParams(vmem_limit_bytes=VMEM_LIMIT),
        name="route_pos",
    )(route_e)


def _for_each_run_piece(runs_ref, fn):
    def per_expert(e, carry):
        local_off = runs_ref[0, 0, e]
        length = runs_ref[0, 0, N_EXPERTS + e]
        global_off = runs_ref[0, 0, 2 * N_EXPERTS + e]
        units = length >> (RUN_ALIGN.bit_length() - 1)

        def big_piece(k, inner):
            done = k * BIG_PIECE
            fn(pl.multiple_of(local_off + done, RUN_ALIGN),
               pl.multiple_of(global_off + done, RUN_ALIGN), BIG_PIECE)
            return inner

        lax.fori_loop(0, units >> SMALL_BITS, big_piece, 0)
        for b in range(SMALL_BITS):
            @pl.when(((units >> b) & 1) == 1)
            def _():
                done = ((units >> (b + 1)) << (b + 1)) * RUN_ALIGN
                fn(pl.multiple_of(local_off + done, RUN_ALIGN),
                   pl.multiple_of(global_off + done, RUN_ALIGN), RUN_ALIGN << b)
        return carry

    lax.fori_loop(0, N_EXPERTS, per_expert, 0)


def _tile_run_rows(runs_ref):
    return lax.fori_loop(0, N_EXPERTS, lambda e, acc: acc + runs_ref[0, 0, N_EXPERTS + e], 0)


def _wait_rows(total_rows, make_copy):
    units = total_rows >> (RUN_ALIGN.bit_length() - 1)
    for b in range((LB_SORT // RUN_ALIGN).bit_length()):
        @pl.when(((units >> b) & 1) == 1)
        def _():
            make_copy(RUN_ALIGN << b).wait()


def _sort_rows_body(ends_ref, padded_ref, total_ref, runs_ref, m_ref, slot_ref, rw_ref,
                    xs_ref, slot_t_ref,
                    local_ref, zero_ref, pending_ref, sem_ref, zsem_ref, *, rows_per_tile):
    tl = m_ref.shape[0]
    lb = local_ref.shape[1]
    step = pl.program_id(0)
    cur = step % 2
    n_tiles = xs_ref.shape[0] // rows_per_tile
    n_used = ends_ref[N_EXPERTS - 1] // rows_per_tile

    def zero_rows(start, rows, wait):
        cp = pltpu.make_async_copy(zero_ref.at[pl.ds(0, rows), :],
                                   xs_ref.at[pl.ds(pl.multiple_of(start, RUN_ALIGN), rows), :],
                                   zsem_ref)
        cp.wait() if wait else cp.start()

    def zero_fill(wait):
        for e in range(N_EXPERTS):
            pad_units = (padded_ref[e] - total_ref[e]) >> (RUN_ALIGN.bit_length() - 1)
            for bit in range((rows_per_tile // RUN_ALIGN).bit_length() - 1):
                @pl.when(((pad_units >> bit) & 1) == 1)
                def _():
                    done = ((pad_units >> (bit + 1)) << (bit + 1)) * RUN_ALIGN
                    zero_rows(ends_ref[e] - (padded_ref[e] - total_ref[e]) + done,
                              RUN_ALIGN << bit, wait)

        def spare(j, carry):
            zero_rows((n_used + j) * rows_per_tile, rows_per_tile, wait)
            return carry

        lax.fori_loop(0, n_tiles - n_used, spare, 0)

    @pl.when(step == 0)
    def _():
        zero_ref[...] = jnp.zeros(zero_ref.shape, bf16)
        zero_fill(wait=False)

    s1 = slot_ref[0:1, :]
    s2 = slot_ref[1:2, :]
    w1 = rw_ref[0:1, :]
    w2 = rw_ref[1:2, :]
    run_rows = _tile_run_rows(runs_ref)

    def sort_rows_upto(n):
        srow = lax.broadcasted_iota(i32, (n, tl), 0)
        p1 = srow == s1
        p2 = srow == s2
        perm = jnp.where(p1 | p2, 1.0, 0.0).astype(bf16)
        rows = jnp.dot(perm, m_ref[...], preferred_element_type=f32)
        w = jnp.sum(jnp.where(p1, w1, 0.0) + jnp.where(p2, w2, 0.0), axis=1, keepdims=True)
        w_hi = w.astype(bf16).astype(f32)
        w_lo = w - w_hi
        lane = lax.broadcasted_iota(i32, (n, BLK), 1)
        local_ref[cur, :n, :D_MODEL] = rows.astype(bf16)
        local_ref[cur, :n, D_MODEL:] = jnp.where(
            lane == 0, w_hi, jnp.where(lane == 1, w_lo, 0.0)).astype(bf16)
        sub_t = lax.broadcasted_iota(i32, (BLK, tl), 0)
        slots = jnp.where(sub_t == 0, s1.astype(f32), jnp.where(sub_t == 1, s2.astype(f32), 0.0))
        slot_t_ref[...] = jnp.transpose(slots)[:, :8].astype(i32)

    def piece(s, local_row, global_row, rows):
        return pltpu.make_async_copy(local_ref.at[s, pl.ds(local_row, rows), :],
                                     xs_ref.at[pl.ds(global_row, rows), :], sem_ref.at[s])

    short = lb - 256
    pl.when(run_rows <= short)(functools.partial(sort_rows_upto, short))
    pl.when(run_rows > short)(functools.partial(sort_rows_upto, lb))

    _for_each_run_piece(runs_ref, lambda l, g, n: piece(cur, l, g, n).start())

    @pl.when(step > 0)
    def _():
        _wait_rows(pending_ref[0], lambda n: piece(1 - cur, 0, 0, n))

    pending_ref[0] = run_rows

    @pl.when(step == pl.num_programs(0) - 1)
    def _():
        _wait_rows(pending_ref[0], lambda n: piece(cur, 0, 0, n))
        zero_fill(wait=True)


def _sort_rows(ends, padded, totals, run_tiles, m, slot, route_w, total_rows, rows_per_tile):
    t = m.shape[0]
    tl = TL_SORT
    grid_spec = pltpu.PrefetchScalarGridSpec(
        num_scalar_prefetch=3,
        grid=(t // tl,),
        in_specs=[pl.BlockSpec((1, 1, 3 * N_EXPERTS), lambda i, *_: (i, 0, 0),
                               memory_space=pltpu.SMEM),
                  pl.BlockSpec((tl, D_MODEL), lambda i, *_: (i, 0)),
                  pl.BlockSpec((8, tl), lambda i, *_: (0, i)),
                  pl.BlockSpec((8, tl), lambda i, *_: (0, i))],
        out_specs=(pl.BlockSpec(memory_space=pl.ANY),
                   pl.BlockSpec((tl, 8), lambda i, *_: (i, 0))),
        scratch_shapes=[pltpu.VMEM((2, LB_SORT, XS_COLS), bf16),
                        pltpu.VMEM((rows_per_tile, XS_COLS), bf16),
                        pltpu.SMEM((1,), i32),
                        pltpu.SemaphoreType.DMA((2,)),
                        pltpu.SemaphoreType.DMA(())])
    return pl.pallas_call(
        functools.partial(_sort_rows_body, rows_per_tile=rows_per_tile),
        grid_spec=grid_spec,
        out_shape=(jax.ShapeDtypeStruct((total_rows, XS_COLS), bf16),
                   jax.ShapeDtypeStruct((t, 8), i32)),
        compiler_params=pltpu.CompilerParams(
            dimension_semantics=("arbitrary",), vmem_limit_bytes=VMEM_LIMIT),
        name="sort_rows",
    )(ends, padded, totals, run_tiles, m, slot, route_w)


def _experts_body(texp_ref, first_ref, next_ref, slot_ref, nused_ref,
                  xs_hbm, wg_hbm, wu_hbm, wd_hbm, ys_hbm,
                  xbuf_ref, sg_ref, su_ref, sd_ref, wg_ref, wu_ref, wd_ref,
                  xsem_ref, osem_ref, sem_ref):
    i = pl.program_id(0)
    n_used = nused_ref[0]
    r = xbuf_ref.shape[1]

    def tile_rows(ref, tile):
        return ref.at[pl.ds(pl.multiple_of(tile * r, r), r), :]

    def fetch_tile(tile):
        s = tile % XS_RING
        return pltpu.make_async_copy(tile_rows(xs_hbm, tile), xbuf_ref.at[s], xsem_ref.at[s])

    def store_tile(tile):
        s = tile % XS_RING
        return pltpu.make_async_copy(xbuf_ref.at[s], tile_rows(ys_hbm, tile), osem_ref.at[s])

    @pl.when(i == 0)
    def _():
        for ahead in range(XS_AHEAD):
            @pl.when(ahead < n_used)
            def _():
                fetch_tile(ahead).start()

    @pl.when(i + XS_AHEAD < n_used)
    def _():
        @pl.when(i + XS_AHEAD >= XS_RING)
        def _():
            store_tile(i + XS_AHEAD - XS_RING).wait()

        fetch_tile(i + XS_AHEAD).start()

    def fetches(expert, s):
        return (pltpu.make_async_copy(wg_hbm.at[expert], sg_ref.at[s], sem_ref.at[s]),
                pltpu.make_async_copy(wu_hbm.at[expert], su_ref.at[s], sem_ref.at[s]),
                pltpu.make_async_copy(wd_hbm.at[expert], sd_ref.at[s], sem_ref.at[s]))

    @pl.when(i < n_used)
    def _():
        s = slot_ref[i]
        fetch_tile(i).wait()
        xs_ref = xbuf_ref.at[i % XS_RING]

        @pl.when(first_ref[i] == 1)
        def _():
            @pl.when(i == 0)
            def _():
                for cp in fetches(texp_ref[0], 0):
                    cp.start()

            for cp in fetches(texp_ref[i], s):
                cp.wait()

            @pl.when(next_ref[i] >= 0)
            def _():
                for cp in fetches(next_ref[i], 1 - s):
                    cp.start()

            wg_ref[...] = sg_ref[s].astype(bf16)
            wu_ref[...] = su_ref[s].astype(bf16)
            wd_ref[...] = sd_ref[s].astype(bf16)

        w_row = (xs_ref[:, D_MODEL:D_MODEL + 1].astype(f32)
                 + xs_ref[:, D_MODEL + 1:D_MODEL + 2].astype(f32))
        gate = jnp.dot(xs_ref[:, :D_MODEL], wg_ref[...], preferred_element_type=f32)
        up = jnp.dot(xs_ref[:, :D_MODEL], wu_ref[...], preferred_element_type=f32)
        hdn = (gate * jax.nn.sigmoid(gate)) * up
        y = jnp.dot(hdn.astype(bf16), wd_ref[...], preferred_element_type=f32)
        xs_ref[:, :D_MODEL] = (y * w_row).astype(bf16)
        store_tile(i).start()

    @pl.when(i == pl.num_programs(0) - 1)
    def _():
        for back in range(1, XS_RING + 1):
            @pl.when(n_used - back >= 0)
            def _():
                store_tile(n_used - back).wait()


def _experts(tile_exp, first, next_exp, slot, n_used, xs, wg, wu, wd, rows_per_tile):
    r = rows_per_tile
    n_tiles = xs.shape[0] // r
    grid_spec = pltpu.PrefetchScalarGridSpec(
        num_scalar_prefetch=5,
        grid=(n_tiles,),
        in_specs=[pl.BlockSpec(memory_space=pl.ANY),
                  pl.BlockSpec(memory_space=pl.ANY),
                  pl.BlockSpec(memory_space=pl.ANY),
                  pl.BlockSpec(memory_space=pl.ANY)],
        out_specs=pl.BlockSpec(memory_space=pl.ANY),
        scratch_shapes=[pltpu.VMEM((XS_RING, r, XS_COLS), bf16),
                        pltpu.VMEM((2, D_MODEL, D_FF_E), f32),
                        pltpu.VMEM((2, D_MODEL, D_FF_E), f32),
                        pltpu.VMEM((2, D_FF_E, D_MODEL), f32),
                        pltpu.VMEM((D_MODEL, D_FF_E), bf16),
                        pltpu.VMEM((D_MODEL, D_FF_E), bf16),
                        pltpu.VMEM((D_FF_E, D_MODEL), bf16),
                        pltpu.SemaphoreType.DMA((XS_RING,)),
                        pltpu.SemaphoreType.DMA((XS_RING,)),
                        pltpu.SemaphoreType.DMA((2,))])
    return pl.pallas_call(
        _experts_body,
        grid_spec=grid_spec,
        out_shape=jax.ShapeDtypeStruct(xs.shape, xs.dtype),
        input_output_aliases={5: 0},
        compiler_params=pltpu.CompilerParams(
            dimension_semantics=("arbitrary",), vmem_limit_bytes=VMEM_LIMIT),
        name="experts",
    )(tile_exp, first, next_exp, slot, n_used, xs, wg, wu, wd)


def _combine_out_body(runs_cur_ref, runs_nxt_ref, h_ref, slot_ref, p_ref, ys_ref,
                      wpg_ref, bpg_ref, wpp_ref, gple_ref, gfin_ref,
                      o_ref, ybuf_ref, moe_ref, sem_ref):
    tk = h_ref.shape[0]
    lb = ybuf_ref.shape[1]
    step = pl.program_id(0)
    n = pl.num_programs(0) - 1
    cur = jnp.minimum(step, n - 1) % 2

    def piece(s, local_row, global_row, rows):
        return pltpu.make_async_copy(ys_ref.at[pl.ds(global_row, rows), pl.ds(0, D_MODEL)],
                                     ybuf_ref.at[s, pl.ds(local_row, rows), :], sem_ref.at[s])

    @pl.when(step == 0)
    def _():
        ybuf_ref[...] = jnp.zeros(ybuf_ref.shape, bf16)
        _for_each_run_piece(runs_cur_ref, lambda l, g, r: piece(0, l, g, r).start())

    @pl.when(step + 1 < n)
    def _():
        _for_each_run_piece(runs_nxt_ref, lambda l, g, r: piece(1 - cur, l, g, r).start())

    run_rows = _tile_run_rows(runs_cur_ref)

    @pl.when(step < n)
    def _():
        _wait_rows(run_rows, lambda r: piece(cur, 0, 0, r))

    def work(rows, finish, unsort):
        if finish:
            h2 = h_ref[...] + moe_ref[...]
            gate = jax.nn.sigmoid(jnp.dot(h2.astype(bf16), wpg_ref[...], preferred_element_type=f32)
                                  + bpg_ref[...])
            pp = jnp.dot(p_ref[...].astype(bf16), wpp_ref[...], preferred_element_type=f32)
            h3 = h2 + gate * _rms(pp, gple_ref[...])
            o_ref[...] = _rms(h3, gfin_ref[...])
        if unsort:
            col = lax.broadcasted_iota(i32, (tk, rows), 1)
            unperm = jnp.where((col == slot_ref[:, 0:1]) | (col == slot_ref[:, 1:2]), 1.0, 0.0)
            moe_ref[...] = jnp.dot(unperm.astype(bf16), ybuf_ref[cur, :rows, :],
                                   preferred_element_type=f32)

    short = lb - 256
    is_short = run_rows <= short
    first = step == 0
    mid = (step > 0) & (step < n)
    pl.when(first & is_short)(functools.partial(work, short, False, True))
    pl.when(first & jnp.logical_not(is_short))(functools.partial(work, lb, False, True))
    pl.when(mid & is_short)(functools.partial(work, short, True, True))
    pl.when(mid & jnp.logical_not(is_short))(functools.partial(work, lb, True, True))
    pl.when(step == n)(functools.partial(work, lb, True, False))


def _combine_out(run_tiles, h2d, slot_t, p2d, ys, wpg, bpg, wpp, gple, gfin):
    t = h2d.shape[0]
    tk = TL_SORT
    nt = t // tk
    const = lambda shape: pl.BlockSpec(shape, lambda i: (0,) * len(shape))
    runs = lambda imap: pl.BlockSpec((1, 1, 3 * N_EXPERTS), imap, memory_space=pltpu.SMEM)
    uns = lambda s: jnp.minimum(s, nt - 1)
    fin = lambda s: jnp.maximum(s - 1, 0)
    return pl.pallas_call(
        _combine_out_body,
        grid=(nt + 1,),
        out_shape=jax.ShapeDtypeStruct((t, D_MODEL), f32),
        in_specs=[runs(lambda s: (uns(s), 0, 0)),
                  runs(lambda s: (uns(s + 1), 0, 0)),
                  pl.BlockSpec((tk, D_MODEL), lambda s: (fin(s), 0)),
                  pl.BlockSpec((tk, 8), lambda s: (uns(s), 0)),
                  pl.BlockSpec((tk, PLE_DIM), lambda s: (fin(s), 0)),
                  pl.BlockSpec(memory_space=pl.ANY),
                  const((D_MODEL, D_MODEL)), const((1, D_MODEL)),
                  const((PLE_DIM, D_MODEL)), const((1, D_MODEL)), const((1, D_MODEL))],
        out_specs=pl.BlockSpec((tk, D_MODEL), lambda s: (fin(s), 0)),
        scratch_shapes=[pltpu.VMEM((2, LB_SORT, D_MODEL), bf16),
                        pltpu.VMEM((tk, D_MODEL), f32),
                        pltpu.SemaphoreType.DMA((2,))],
        compiler_params=pltpu.CompilerParams(
            dimension_semantics=("arbitrary",), vmem_limit_bytes=VMEM_LIMIT),
        name="combine_out",
    )(run_tiles, run_tiles, h2d, slot_t, p2d, ys, wpg, bpg, wpp, gple, gfin)


def _head_major_to_tile_major(a, axis):
    axis = axis % a.ndim
    shape = a.shape
    a = a.reshape(shape[:axis] + (N_KV, GQA, HEAD_DIM) + shape[axis + 1:])
    return jnp.swapaxes(a, axis, axis + 1).reshape(shape)


def kernel(x, p, rel_bias, g_mix, w_in, ln_v_g, ln_v_b, w_spatial, b_spatial, sink, g_out_grp, w_out,
           g_ffn, w_router_group, b_router_group, w_router_expert, b_router_expert, w_gate_e, w_up_e,
           w_down_e, w_ple_proj, g_ple, w_ple_gate, b_ple_gate, g_final):
    b, s, d = x.shape
    t = b * s
    depth = g_mix.shape[0]
    assert depth == 1 and d == D_MODEL
    c1, c2 = 2 * D_A, 2 * D_A + D_B
    bias = _bias_table(rel_bias * LOG2E)
    hcur = x.astype(f32)
    for li in range(depth):
        col_scale = np.ones((1, D_IN), np.float32)
        col_scale[:, c1:c2] = LOG2E * HEAD_DIM ** -0.5
        win = (w_in[li] * col_scale).astype(bf16)
        bs = jnp.broadcast_to(b_spatial[li][:, :, None], (A_HEADS, BLK, BLK)).astype(f32)
        gout = g_out_grp[li]
        gout_b = _head_major_to_tile_major(gout[D_A:], 0)[None]
        wo = w_out[li]
        wout = jnp.concatenate([wo[:D_A], _head_major_to_tile_major(wo[D_A:], 0)],
                               axis=0).astype(bf16)
        wr = jnp.concatenate([
            w_router_group[li], jnp.zeros((D_MODEL, 8 - N_GROUPS), f32), w_router_expert[li],
            jnp.zeros((D_MODEL, BLK - ROUTE_ROWS), f32)], axis=1).astype(bf16)
        br = jnp.concatenate([b_router_group[li], jnp.full((8 - N_GROUPS,), NEG, f32),
                              b_router_expert[li]])
        br = jnp.broadcast_to(br[:, None], (ROUTE_ROWS, TQ_ATT))

        ya, q, k, v = _mix_in(hcur.reshape(t, d), g_mix[li][None], win, ln_v_g[li][None],
                              ln_v_b[li][None], w_spatial[li].astype(bf16), bs, gout[None, :D_A])
        h, m_rows, route_e, route_w = _attn_out(
            sink[li].astype(f32) * LOG2E, hcur, ya.reshape(b, s, D_A), q.reshape(b, s, D_B),
            k.reshape(b, s, BLK), v.reshape(b, s, BLK), bias, gout_b, wout,
            g_ffn[li][None], wr, br)

        r = R_EXP
        n_sort = t // TL_SORT
        max_rows = 2 * t + n_sort * N_EXPERTS * (RUN_ALIGN - 1)
        n_tiles = -(-max_rows // r) + N_EXPERTS
        assert n_tiles * r < 2 ** 16, "route_pos prefix sums hold row offsets in two bf16 bytes"
        slot, runs, offs = _route_pos(route_e, r)
        starts = offs[0, :, 0]
        padded = offs[1, :, 0]
        ends = starts + padded
        n_used = ends[-1] // r
        tile_idx = jnp.minimum(jnp.arange(n_tiles, dtype=i32), n_used - 1)
        tile_exp = jnp.minimum(
            jnp.sum((ends[None, :] <= (tile_idx * r)[:, None]).astype(i32), axis=1), N_EXPERTS - 1)
        tile_exp = tile_exp.astype(i32)
        first = jnp.concatenate([jnp.ones((1,), i32),
                                 (tile_exp[1:] != tile_exp[:-1]).astype(i32)])
        next_tile = ends[tile_exp] // r
        next_exp = jnp.where(next_tile < n_used, tile_exp[jnp.minimum(next_tile, n_tiles - 1)], -1)
        fetch_slot = (jnp.cumsum(first) - 1) % 2
        run_tiles = runs[:, :, :, 0].reshape(n_sort, 1, 3 * N_EXPERTS)
        xs, slot_t = _sort_rows(ends, padded, offs[2, :, 0], run_tiles, m_rows, slot, route_w,
                                n_tiles * r, r)
        ys = _experts(tile_exp, first, next_exp.astype(i32), fetch_slot.astype(i32),
                      n_used.reshape(1), xs, w_gate_e[li], w_up_e[li], w_down_e[li], r)

        out = _combine_out(run_tiles, h.reshape(t, d), slot_t,
                           p[li].reshape(t, PLE_DIM), ys, w_ple_gate[li].astype(bf16),
                           b_ple_gate[li][None], w_ple_proj[li].astype(bf16), g_ple[li][None],
                           g_final[None])
        hcur = out.reshape(b, s, d)
    return hcur
```

```python
import functools
import math

import jax
import jax.numpy as jnp
import numpy as np
from jax import lax
from jax.experimental import pallas as pl
from jax.experimental.pallas import tpu as pltpu

D_MODEL = 1024
D_A = 512
D_B = 512
BLK = 128
A_HEADS = 4
HEAD_DIM = 64
N_HEADS = 8
N_KV = 2
GQA = 4
WINDOW = 128
NUM_BUCKETS = 32
MAX_DIST = 128
D_IN = 2 * D_A + D_B + 2 * N_KV * HEAD_DIM
N_GROUPS = 4
E_PER_GROUP = 8
N_EXPERTS = 32
D_FF_E = 256
PLE_DIM = 256
EPS = 1e-6
NEG = -1e30
LOG2E = math.log2(math.e)

TM_MIX = 512
TQ_ATT = 512
TL_SORT = 512
RUN_ALIGN = 16
SMALL_BITS = 2
BIG_PIECE = RUN_ALIGN << SMALL_BITS
LB_SORT = 2 * TL_SORT + N_EXPERTS * RUN_ALIGN
R_EXP = 512
XS_COLS = D_MODEL + BLK
XS_RING = 4
XS_AHEAD = 2
ROUTE_ROWS = 8 + N_EXPERTS
VMEM_LIMIT = 48 * 1024 * 1024

f32 = jnp.float32
bf16 = jnp.bfloat16
i32 = jnp.int32


def _rms(x, g):
    return x * lax.rsqrt(jnp.mean(x * x, axis=-1, keepdims=True) + EPS) * g


def _gelu_tanh(x):
    c = math.sqrt(2.0 / math.pi)
    return x * (0.5 * (1.0 + jnp.tanh(c * (x + 0.044715 * (x * x * x)))))


def _bucket_row():
    n = NUM_BUCKETS // 2
    max_exact = n // 2
    rel = np.broadcast_to(np.arange(4 * BLK)[None, :] - BLK, (8, 4 * BLK))
    ret = np.where(rel > 0, n, 0)
    a = np.abs(rel)
    large = max_exact + (np.log(np.maximum(a, 1).astype(np.float64) / max_exact)
                         / math.log(MAX_DIST / max_exact) * (n - max_exact)).astype(np.int32)
    large = np.minimum(large, n - 1)
    bucket = ret + np.where(a < max_exact, a, large)
    return np.where(a <= WINDOW, bucket, -1).astype(np.int32)


def _bias_body(rb_ref, bucket_ref, o_ref):
    bucket = bucket_ref[...]
    row = lax.broadcasted_iota(i32, (BLK, BLK), 0)
    lane = lax.broadcasted_iota(i32, (BLK, BLK), 1)
    for h in range(N_HEADS):
        base = jnp.full(bucket.shape, NEG, f32)
        for b in range(NUM_BUCKETS):
            base = jnp.where(bucket == b, rb_ref[b, h], base)
        prev = jnp.full((BLK, BLK), NEG, f32)
        for c in range(3):
            tile = jnp.broadcast_to(base[0:1, c * BLK:(c + 1) * BLK], (BLK, BLK))
            rolled = pltpu.roll(tile, 0, 1, stride=1, stride_axis=0)
            o_ref[h, :, c * BLK:(c + 1) * BLK] = jnp.where(lane >= row, rolled, prev)
            prev = rolled


def _bias_table(rel_bias):
    bucket = jnp.asarray(_bucket_row())
    return pl.pallas_call(
        _bias_body,
        out_shape=jax.ShapeDtypeStruct((N_HEADS, BLK, 3 * BLK), f32),
        in_specs=[pl.BlockSpec(memory_space=pltpu.SMEM),
                  pl.BlockSpec((8, 4 * BLK), lambda: (0, 0))],
        out_specs=pl.BlockSpec((N_HEADS, BLK, 3 * BLK), lambda: (0, 0, 0)),
        name="bias_table",
    )(rel_bias.astype(f32), bucket)


def _mix_in_body(x_ref, gmix_ref, win_ref, lng_ref, lnb_ref, ws_ref, bs_ref, gout_ref,
                 ya_ref, q_ref, k_ref, v_ref):
    tm = x_ref.shape[0]
    nc = tm // BLK
    a = _rms(x_ref[...], gmix_ref[...])
    z = jnp.dot(a.astype(bf16), win_ref[...], preferred_element_type=f32)
    uv = _gelu_tanh(z[:, :2 * D_A])
    u = uv[:, :D_A]
    v = uv[:, D_A:]
    mu = jnp.mean(v, axis=-1, keepdims=True)
    vc = v - mu
    var = jnp.mean(vc * vc, axis=-1, keepdims=True)
    vn = (vc * lax.rsqrt(var + EPS) * lng_ref[...] + lnb_ref[...]).astype(bf16)
    cols = []
    for h in range(A_HEADS):
        rhs = jnp.concatenate(
            [vn[c * BLK:(c + 1) * BLK, h * BLK:(h + 1) * BLK] for c in range(nc)], axis=1)
        r = jnp.dot(ws_ref[h], rhs, preferred_element_type=f32)
        cols.append(jnp.concatenate(
            [r[:, c * BLK:(c + 1) * BLK] + bs_ref[h] for c in range(nc)], axis=0))
    sv = jnp.concatenate(cols, axis=1)
    ya_ref[...] = _rms(u * sv, gout_ref[...]).astype(bf16)
    qt = [z[:, 2 * D_A + j * BLK:2 * D_A + (j + 1) * BLK] for j in range(D_B // BLK)]
    low_half = lax.broadcasted_iota(i32, (tm, BLK), 1) < HEAD_DIM
    for g in range(GQA):
        a0, a1 = qt[g // 2], qt[GQA // 2 + g // 2]
        tile = (jnp.where(low_half, a0, pltpu.roll(a1, HEAD_DIM, 1)) if g % 2 == 0
                else jnp.where(low_half, pltpu.roll(a0, HEAD_DIM, 1), a1))
        q_ref[:, g * BLK:(g + 1) * BLK] = tile.astype(bf16)
    k_ref[...] = z[:, 2 * D_A + D_B:2 * D_A + D_B + BLK].astype(bf16)
    v_ref[...] = z[:, 2 * D_A + D_B + BLK:].astype(bf16)


def _mix_in(x2, gmix, win, lng, lnb, ws, bs, gout_a):
    t = x2.shape[0]
    tm = TM_MIX
    const = lambda shape: pl.BlockSpec(shape, lambda i: (0,) * len(shape))
    return pl.pallas_call(
        _mix_in_body,
        grid=(t // tm,),
        out_shape=(jax.ShapeDtypeStruct((t, D_A), bf16),
                   jax.ShapeDtypeStruct((t, D_B), bf16),
                   jax.ShapeDtypeStruct((t, BLK), bf16),
                   jax.ShapeDtypeStruct((t, BLK), bf16)),
        in_specs=[pl.BlockSpec((tm, D_MODEL), lambda i: (i, 0)),
                  const((1, D_MODEL)), const((D_MODEL, D_IN)),
                  const((1, D_A)), const((1, D_A)),
                  const((A_HEADS, BLK, BLK)), const((A_HEADS, BLK, BLK)),
                  const((1, D_A))],
        out_specs=(pl.BlockSpec((tm, D_A), lambda i: (i, 0)),
                   pl.BlockSpec((tm, D_B), lambda i: (i, 0)),
                   pl.BlockSpec((tm, BLK), lambda i: (i, 0)),
                   pl.BlockSpec((tm, BLK), lambda i: (i, 0))),
        compiler_params=pltpu.CompilerParams(
            dimension_semantics=("parallel",), vmem_limit_bytes=VMEM_LIMIT),
        name="mix_in",
    )(x2, gmix, win, lng, lnb, ws, bs, gout_a)


def _attn_epilogue(sink_ref, x_ref, ya_ref, q_ref, kp_ref, km_ref, kn_ref, vp_ref, vm_ref, vn_ref,
                   bias_ref, goutb_ref, wout_ref, gffn_ref, wr_ref, br_ref,
                   h_ref, m_ref, re_ref, rw_ref,
                   kf_ref, vf_ref, e_ref, yb_ref, *, seq_len):
    tq = x_ref.shape[1]
    ybn = _rms(yb_ref[1], goutb_ref[...]).astype(bf16)
    y = jnp.concatenate([ya_ref[0], ybn], axis=1)
    h = x_ref[0] + jnp.dot(y, wout_ref[...], preferred_element_type=f32)
    h_ref[0] = h
    m = _rms(h, gffn_ref[...])
    m_ref[...] = m.astype(bf16)

    logit_t = jnp.dot(m.astype(bf16), wr_ref[...], preferred_element_type=f32)
    logit = jnp.transpose(logit_t)[:ROUTE_ROWS] + br_ref[...]
    sub = lax.broadcasted_iota(i32, (8, tq), 0)
    lg = logit[0:8]
    mg = jnp.max(lg, axis=0, keepdims=True)
    pg_top = 1.0 / jnp.sum(jnp.exp(lg - mg), axis=0, keepdims=True)
    g_idx = jnp.min(jnp.where(lg == mg, sub, 8), axis=0, keepdims=True)
    sel = logit[8:16]
    for g in range(1, N_GROUPS):
        sel = jnp.where(g_idx == g, logit[8 + 8 * g:16 + 8 * g], sel)
    m1 = jnp.max(sel, axis=0, keepdims=True)
    i1 = jnp.min(jnp.where(sel == m1, sub, 8), axis=0, keepdims=True)
    sel2 = jnp.where(sub == i1, -jnp.inf, sel)
    m2 = jnp.max(sel2, axis=0, keepdims=True)
    i2 = jnp.min(jnp.where(sel2 == m2, sub, 8), axis=0, keepdims=True)
    r = jnp.exp(m2 - m1)
    w1 = pg_top / (1.0 + r)
    w2 = pg_top * r / (1.0 + r)
    e1 = g_idx * E_PER_GROUP + i1
    e2 = g_idx * E_PER_GROUP + i2
    re_ref[...] = jnp.where(sub == 0, e1, jnp.where(sub == 1, e2, 0))
    rw_ref[...] = jnp.where(sub == 0, w1, jnp.where(sub == 1, w2, 0.0))


def _attn_attention(sink_ref, x_ref, ya_ref, q_ref, kp_ref, km_ref, kn_ref, vp_ref, vm_ref, vn_ref,
                    bias_ref, goutb_ref, wout_ref, gffn_ref, wr_ref, br_ref,
                    h_ref, m_ref, re_ref, rw_ref,
                    kf_ref, vf_ref, e_ref, yb_ref, *, seq_len):
    tq = x_ref.shape[1]
    nb = tq // BLK
    ti = pl.program_id(0) % (seq_len // tq)
    kf_ref[0:BLK] = kp_ref[0]
    kf_ref[BLK:BLK + tq] = km_ref[0]
    kf_ref[BLK + tq:] = kn_ref[0]
    vlane = lax.broadcasted_iota(i32, (BLK, BLK), 1)
    for src, r0, r1 in ((vp_ref, 0, BLK), (vn_ref, BLK + tq, tq + 2 * BLK)) + tuple(
            (vm_ref.at[:, pl.ds(c * BLK, BLK), :], BLK + c * BLK, BLK + (c + 1) * BLK)
            for c in range(nb)):
        vals = src[0].astype(f32)
        vf_ref[0, r0:r1] = jnp.where(vlane < HEAD_DIM, vals,
                                     jnp.where(vlane == HEAD_DIM, 1.0, 0.0)).astype(bf16)
        vf_ref[1, r0:r1] = jnp.where(vlane >= HEAD_DIM, vals,
                                     jnp.where(vlane == 0, 1.0, 0.0)).astype(bf16)

    lane = lax.broadcasted_iota(i32, (BLK, BLK), 1)
    low_half = lane < HEAD_DIM

    for n in range(nb):
        r0 = n * BLK
        qb = q_ref[0, pl.ds(r0, BLK), :]
        kb = kf_ref[pl.ds(r0, 3 * BLK), :]
        zero = jnp.zeros((BLK, BLK), bf16)
        lhs = []
        for kh in range(N_KV):
            for g in range(GQA):
                tile = qb[:, g * BLK:(g + 1) * BLK]
                lhs.append(jnp.where(low_half, tile, zero) if kh == 0
                           else jnp.where(low_half, zero, tile))
        lhs = jnp.concatenate(lhs, axis=0)
        s_all = lax.dot_general(lhs, kb, (((1,), (1,)), ((), ())),
                                preferred_element_type=f32)
        blk = ti * nb + n
        pen = [jnp.where(blk == 0, NEG, 0.0) if n == 0 else None, None,
               jnp.where((blk + 1) * BLK >= seq_len, NEG, 0.0) if n == nb - 1 else None]
        sink_e = []
        for h in range(N_HEADS):
            bias_h = bias_ref[h]
            bias_h = jnp.concatenate(
                [bias_h[:, j * BLK:(j + 1) * BLK] if pen[j] is None
                 else bias_h[:, j * BLK:(j + 1) * BLK] + pen[j] for j in range(3)], axis=1)
            s = s_all[h * BLK:(h + 1) * BLK] + bias_h
            sk = sink_ref[h]
            mrow = jnp.maximum(jnp.max(s, axis=-1, keepdims=True), sk)
            e_ref[n, h * BLK:(h + 1) * BLK, :] = jnp.exp2(s - mrow).astype(bf16)
            sink_e.append(jnp.exp2(sk - mrow))
        half = GQA * BLK
        pv = [jnp.dot(e_ref[n, kh * half:(kh + 1) * half, :], vf_ref[kh, pl.ds(r0, 3 * BLK), :],
                      preferred_element_type=f32) for kh in range(N_KV)]

        def head_out(h):
            kh, g = divmod(h, GQA)
            rows = pv[kh][g * BLK:(g + 1) * BLK]
            ones_col = HEAD_DIM if kh == 0 else 0
            return rows * (1.0 / (rows[:, ones_col:ones_col + 1] + sink_e[h]))

        for g in range(GQA):
            yb_ref[0, pl.ds(r0, BLK), g * BLK:(g + 1) * BLK] = jnp.where(
                low_half, head_out(g), head_out(GQA + g))

    yb_ref[1] = yb_ref[0]


def _attn_out_body(*refs, seq_len):
    step = pl.program_id(0)
    n_tok_tiles = pl.num_programs(0) - 1

    @pl.when(step == 0)
    def _():
        _attn_attention(*refs, seq_len=seq_len)

    @pl.when((step > 0) & (step < n_tok_tiles))
    def _():
        _attn_epilogue(*refs, seq_len=seq_len)
        _attn_attention(*refs, seq_len=seq_len)

    @pl.when(step == n_tok_tiles)
    def _():
        _attn_epilogue(*refs, seq_len=seq_len)


def _attn_out(sink, x, ya, q, k, v, bias, goutb, wout, gffn, wr, br):
    b, s, _ = x.shape
    tq = TQ_ATT
    nb = tq // BLK
    nblk = s // BLK
    t = b * s
    nq = s // tq
    n_tok_tiles = b * nq
    att = lambda s: jnp.minimum(s, n_tok_tiles - 1)
    epi = lambda s: jnp.maximum(s - 1, 0)
    const = lambda shape: pl.BlockSpec(shape, lambda s, *_: (0,) * len(shape))
    tok_att = lambda w: pl.BlockSpec((1, tq, w), lambda s, *_: (att(s) // nq, att(s) % nq, 0))
    tok_epi = lambda w: pl.BlockSpec((1, tq, w), lambda s, *_: (epi(s) // nq, epi(s) % nq, 0))
    prev = pl.BlockSpec((1, BLK, BLK), lambda s, *_: (
        att(s) // nq, jnp.maximum((att(s) % nq) * nb - 1, 0), 0))
    nxt = pl.BlockSpec((1, BLK, BLK), lambda s, *_: (
        att(s) // nq, jnp.minimum((att(s) % nq) * nb + nb, nblk - 1), 0))
    rows = pl.BlockSpec((tq, D_MODEL), lambda s, *_: (epi(s), 0))
    lanes = lambda rows: pl.BlockSpec((rows, tq), lambda s, *_: (0, epi(s)))
    grid_spec = pltpu.PrefetchScalarGridSpec(
        num_scalar_prefetch=1,
        grid=(n_tok_tiles + 1,),
        in_specs=[tok_epi(D_MODEL), tok_epi(D_A), tok_att(D_B),
                  prev, tok_att(BLK), nxt, prev, tok_att(BLK), nxt,
                  const((N_HEADS, BLK, 3 * BLK)), const((1, D_B)), const((D_MODEL, D_MODEL)),
                  const((1, D_MODEL)), const((D_MODEL, BLK)), const((ROUTE_ROWS, tq))],
        out_specs=(tok_epi(D_MODEL), rows, lanes(8), lanes(8)),
        scratch_shapes=[pltpu.VMEM((tq + 2 * BLK, BLK), bf16),
                        pltpu.VMEM((N_KV, tq + 2 * BLK, BLK), bf16),
                        pltpu.VMEM((nb, N_HEADS * BLK, 3 * BLK), bf16),
                        pltpu.VMEM((2, tq, D_B), f32)])
    return pl.pallas_call(
        functools.partial(_attn_out_body, seq_len=s),
        grid_spec=grid_spec,
        out_shape=(jax.ShapeDtypeStruct((b, s, D_MODEL), f32),
                   jax.ShapeDtypeStruct((t, D_MODEL), bf16),
                   jax.ShapeDtypeStruct((8, t), i32),
                   jax.ShapeDtypeStruct((8, t), f32)),
        compiler_params=pltpu.CompilerParams(
            dimension_semantics=("arbitrary",), vmem_limit_bytes=VMEM_LIMIT),
        name="attn_out",
    )(sink, x, ya, q, k, k, k, v, v, v, bias, goutb, wout, gffn, wr, br)


def _exclusive_prefix(vals):
    a = lax.broadcasted_iota(i32, (N_EXPERTS, N_EXPERTS), 0)
    c = lax.broadcasted_iota(i32, (N_EXPERTS, N_EXPERTS), 1)
    low = jnp.where(c < a, 1.0, 0.0).astype(bf16)
    hi = (vals >> 8).astype(f32).astype(bf16)
    lo = (vals & 255).astype(f32).astype(bf16)
    return (jnp.dot(low, hi, preferred_element_type=f32) * 256.0
            + jnp.dot(low, lo, preferred_element_type=f32)).astype(i32)


def _round_up_pow2(vals, mult):
    log_m = mult.bit_length() - 1
    return ((vals + (mult - 1)) >> log_m) << log_m


def _route_pos_body(re_ref, slot_ref, runs_ref, offs_ref, tri_ref, *, rows_per_tile, tl):
    n_tok_tiles = re_ref.shape[1] // tl
    a = lax.broadcasted_iota(i32, (tl, tl), 0)
    c = lax.broadcasted_iota(i32, (tl, tl), 1)
    tri_ref[...] = jnp.where(a < c, 1.0, 0.0).astype(bf16)
    eid = lax.broadcasted_iota(i32, (N_EXPERTS, tl), 0)

    def tile_hits(j):
        cols = pl.ds(pl.multiple_of(j * tl, tl), tl)
        hit1 = eid == re_ref[0:1, cols]
        hit2 = eid == re_ref[1:2, cols]
        onehot = jnp.where(hit1 | hit2, 1.0, 0.0)
        cnt = jnp.sum(onehot, axis=1, keepdims=True).astype(i32)
        run_len = jnp.broadcast_to(_round_up_pow2(cnt, RUN_ALIGN), (N_EXPERTS, BLK))
        return cols, hit1, hit2, onehot, run_len

    total = lax.fori_loop(0, n_tok_tiles, lambda j, acc: acc + tile_hits(j)[4],
                          jnp.zeros((N_EXPERTS, BLK), i32))
    padded = _round_up_pow2(total, rows_per_tile)
    starts = _exclusive_prefix(padded)
    offs_ref[0] = starts
    offs_ref[1] = padded
    offs_ref[2] = total

    def place(j, global_off):
        cols, hit1, hit2, onehot, run_len = tile_hits(j)
        local_off = _exclusive_prefix(run_len)
        before = jnp.dot(onehot.astype(bf16), tri_ref[...], preferred_element_type=f32)
        slot = before + local_off[:, 0:1].astype(f32)
        s1 = jnp.sum(jnp.where(hit1, slot, 0.0), axis=0, keepdims=True)
        s2 = jnp.sum(jnp.where(hit2, slot, 0.0), axis=0, keepdims=True)
        sub = lax.broadcasted_iota(i32, (8, tl), 0)
        slot_ref[:, cols] = jnp.where(sub == 0, s1.astype(i32),
                                      jnp.where(sub == 1, s2.astype(i32), 0))
        runs_ref[j, 0] = local_off
        runs_ref[j, 1] = run_len
        runs_ref[j, 2] = global_off
        return global_off + run_len

    lax.fori_loop(0, n_tok_tiles, place, starts)


def _route_pos(route_e, rows_per_tile):
    t = route_e.shape[1]
    tl = TL_SORT
    return pl.pallas_call(
        functools.partial(_route_pos_body, rows_per_tile=rows_per_tile, tl=tl),
        out_shape=(jax.ShapeDtypeStruct((8, t), i32),
                   jax.ShapeDtypeStruct((t // tl, 3, N_EXPERTS, BLK), i32),
                   jax.ShapeDtypeStruct((3, N_EXPERTS, BLK), i32)),
        scratch_shapes=[pltpu.VMEM((tl, tl), bf16)],
        compiler_params=pltpu.CompilerParams(vmem_limit_bytes=VMEM_LIMIT),
        name="route_pos",
    )(route_e)


def _for_each_run_piece(runs_ref, fn):
    def per_expert(e, carry):
        local_off = runs_ref[0, 0, e]
        length = runs_ref[0, 0, N_EXPERTS + e]
        global_off = runs_ref[0, 0, 2 * N_EXPERTS + e]
        units = length >> (RUN_ALIGN.bit_length() - 1)

        def big_piece(k, inner):
            done = k * BIG_PIECE
            fn(pl.multiple_of(local_off + done, RUN_ALIGN),
               pl.multiple_of(global_off + done, RUN_ALIGN), BIG_PIECE)
            return inner

        lax.fori_loop(0, units >> SMALL_BITS, big_piece, 0)
        for b in range(SMALL_BITS):
            @pl.when(((units >> b) & 1) == 1)
            def _():
                done = ((units >> (b + 1)) << (b + 1)) * RUN_ALIGN
                fn(pl.multiple_of(local_off + done, RUN_ALIGN),
                   pl.multiple_of(global_off + done, RUN_ALIGN), RUN_ALIGN << b)
        return carry

    lax.fori_loop(0, N_EXPERTS, per_expert, 0)


def _tile_run_rows(runs_ref):
    return lax.fori_loop(0, N_EXPERTS, lambda e, acc: acc + runs_ref[0, 0, N_EXPERTS + e], 0)


def _wait_rows(total_rows, make_copy):
    units = total_rows >> (RUN_ALIGN.bit_length() - 1)
    for b in range((LB_SORT // RUN_ALIGN).bit_length()):
        @pl.when(((units >> b) & 1) == 1)
        def _():
            make_copy(RUN_ALIGN << b).wait()


def _sort_rows_body(ends_ref, padded_ref, total_ref, runs_ref, m_ref, slot_ref, rw_ref,
                    xs_ref, slot_t_ref,
                    local_ref, zero_ref, pending_ref, sem_ref, zsem_ref, *, rows_per_tile):
    tl = m_ref.shape[0]
    lb = local_ref.shape[1]
    step = pl.program_id(0)
    cur = step % 2
    n_tiles = xs_ref.shape[0] // rows_per_tile
    n_used = ends_ref[N_EXPERTS - 1] // rows_per_tile

    def zero_rows(start, rows, wait):
        cp = pltpu.make_async_copy(zero_ref.at[pl.ds(0, rows), :],
                                   xs_ref.at[pl.ds(pl.multiple_of(start, RUN_ALIGN), rows), :],
                                   zsem_ref)
        cp.wait() if wait else cp.start()

    def zero_fill(wait):
        for e in range(N_EXPERTS):
            pad_units = (padded_ref[e] - total_ref[e]) >> (RUN_ALIGN.bit_length() - 1)
            for bit in range((rows_per_tile // RUN_ALIGN).bit_length() - 1):
                @pl.when(((pad_units >> bit) & 1) == 1)
                def _():
                    done = ((pad_units >> (bit + 1)) << (bit + 1)) * RUN_ALIGN
                    zero_rows(ends_ref[e] - (padded_ref[e] - total_ref[e]) + done,
                              RUN_ALIGN << bit, wait)

        def spare(j, carry):
            zero_rows((n_used + j) * rows_per_tile, rows_per_tile, wait)
            return carry

        lax.fori_loop(0, n_tiles - n_used, spare, 0)

    @pl.when(step == 0)
    def _():
        zero_ref[...] = jnp.zeros(zero_ref.shape, bf16)
        zero_fill(wait=False)

    s1 = slot_ref[0:1, :]
    s2 = slot_ref[1:2, :]
    w1 = rw_ref[0:1, :]
    w2 = rw_ref[1:2, :]
    run_rows = _tile_run_rows(runs_ref)

    def sort_rows_upto(n):
        srow = lax.broadcasted_iota(i32, (n, tl), 0)
        p1 = srow == s1
        p2 = srow == s2
        perm = jnp.where(p1 | p2, 1.0, 0.0).astype(bf16)
        rows = jnp.dot(perm, m_ref[...], preferred_element_type=f32)
        w = jnp.sum(jnp.where(p1, w1, 0.0) + jnp.where(p2, w2, 0.0), axis=1, keepdims=True)
        w_hi = w.astype(bf16).astype(f32)
        w_lo = w - w_hi
        lane = lax.broadcasted_iota(i32, (n, BLK), 1)
        local_ref[cur, :n, :D_MODEL] = rows.astype(bf16)
        local_ref[cur, :n, D_MODEL:] = jnp.where(
            lane == 0, w_hi, jnp.where(lane == 1, w_lo, 0.0)).astype(bf16)
        sub_t = lax.broadcasted_iota(i32, (BLK, tl), 0)
        slots = jnp.where(sub_t == 0, s1.astype(f32), jnp.where(sub_t == 1, s2.astype(f32), 0.0))
        slot_t_ref[...] = jnp.transpose(slots)[:, :8].astype(i32)

    def piece(s, local_row, global_row, rows):
        return pltpu.make_async_copy(local_ref.at[s, pl.ds(local_row, rows), :],
                                     xs_ref.at[pl.ds(global_row, rows), :], sem_ref.at[s])

    short = lb - 256
    pl.when(run_rows <= short)(functools.partial(sort_rows_upto, short))
    pl.when(run_rows > short)(functools.partial(sort_rows_upto, lb))

    _for_each_run_piece(runs_ref, lambda l, g, n: piece(cur, l, g, n).start())

    @pl.when(step > 0)
    def _():
        _wait_rows(pending_ref[0], lambda n: piece(1 - cur, 0, 0, n))

    pending_ref[0] = run_rows

    @pl.when(step == pl.num_programs(0) - 1)
    def _():
        _wait_rows(pending_ref[0], lambda n: piece(cur, 0, 0, n))
        zero_fill(wait=True)


def _sort_rows(ends, padded, totals, run_tiles, m, slot, route_w, total_rows, rows_per_tile):
    t = m.shape[0]
    tl = TL_SORT
    grid_spec = pltpu.PrefetchScalarGridSpec(
        num_scalar_prefetch=3,
        grid=(t // tl,),
        in_specs=[pl.BlockSpec((1, 1, 3 * N_EXPERTS), lambda i, *_: (i, 0, 0),
                               memory_space=pltpu.SMEM),
                  pl.BlockSpec((tl, D_MODEL), lambda i, *_: (i, 0)),
                  pl.BlockSpec((8, tl), lambda i, *_: (0, i)),
                  pl.BlockSpec((8, tl), lambda i, *_: (0, i))],
        out_specs=(pl.BlockSpec(memory_space=pl.ANY),
                   pl.BlockSpec((tl, 8), lambda i, *_: (i, 0))),
        scratch_shapes=[pltpu.VMEM((2, LB_SORT, XS_COLS), bf16),
                        pltpu.VMEM((rows_per_tile, XS_COLS), bf16),
                        pltpu.SMEM((1,), i32),
                        pltpu.SemaphoreType.DMA((2,)),
                        pltpu.SemaphoreType.DMA(())])
    return pl.pallas_call(
        functools.partial(_sort_rows_body, rows_per_tile=rows_per_tile),
        grid_spec=grid_spec,
        out_shape=(jax.ShapeDtypeStruct((total_rows, XS_COLS), bf16),
                   jax.ShapeDtypeStruct((t, 8), i32)),
        compiler_params=pltpu.CompilerParams(
            dimension_semantics=("arbitrary",), vmem_limit_bytes=VMEM_LIMIT),
        name="sort_rows",
    )(ends, padded, totals, run_tiles, m, slot, route_w)


def _experts_body(texp_ref, first_ref, next_ref, slot_ref, nused_ref,
                  xs_hbm, wg_hbm, wu_hbm, wd_hbm, ys_hbm,
                  xbuf_ref, sg_ref, su_ref, sd_ref, wg_ref, wu_ref, wd_ref,
                  xsem_ref, osem_ref, sem_ref):
    i = pl.program_id(0)
    n_used = nused_ref[0]
    r = xbuf_ref.shape[1]

    def tile_rows(ref, tile):
        return ref.at[pl.ds(pl.multiple_of(tile * r, r), r), :]

    def fetch_tile(tile):
        s = tile % XS_RING
        return pltpu.make_async_copy(tile_rows(xs_hbm, tile), xbuf_ref.at[s], xsem_ref.at[s])

    def store_tile(tile):
        s = tile % XS_RING
        return pltpu.make_async_copy(xbuf_ref.at[s], tile_rows(ys_hbm, tile), osem_ref.at[s])

    @pl.when(i == 0)
    def _():
        for ahead in range(XS_AHEAD):
            @pl.when(ahead < n_used)
            def _():
                fetch_tile(ahead).start()

    @pl.when(i + XS_AHEAD < n_used)
    def _():
        @pl.when(i + XS_AHEAD >= XS_RING)
        def _():
            store_tile(i + XS_AHEAD - XS_RING).wait()

        fetch_tile(i + XS_AHEAD).start()

    def fetches(expert, s):
        return (pltpu.make_async_copy(wg_hbm.at[expert], sg_ref.at[s], sem_ref.at[s]),
                pltpu.make_async_copy(wu_hbm.at[expert], su_ref.at[s], sem_ref.at[s]),
                pltpu.make_async_copy(wd_hbm.at[expert], sd_ref.at[s], sem_ref.at[s]))

    @pl.when(i < n_used)
    def _():
        s = slot_ref[i]
        fetch_tile(i).wait()
        xs_ref = xbuf_ref.at[i % XS_RING]

        @pl.when(first_ref[i] == 1)
        def _():
            @pl.when(i == 0)
            def _():
                for cp in fetches(texp_ref[0], 0):
                    cp.start()

            for cp in fetches(texp_ref[i], s):
                cp.wait()

            @pl.when(next_ref[i] >= 0)
            def _():
                for cp in fetches(next_ref[i], 1 - s):
                    cp.start()

            wg_ref[...] = sg_ref[s].astype(bf16)
            wu_ref[...] = su_ref[s].astype(bf16)
            wd_ref[...] = sd_ref[s].astype(bf16)

        w_row = (xs_ref[:, D_MODEL:D_MODEL + 1].astype(f32)
                 + xs_ref[:, D_MODEL + 1:D_MODEL + 2].astype(f32))
        gate = jnp.dot(xs_ref[:, :D_MODEL], wg_ref[...], preferred_element_type=f32)
        up = jnp.dot(xs_ref[:, :D_MODEL], wu_ref[...], preferred_element_type=f32)
        hdn = (gate * jax.nn.sigmoid(gate)) * up
        y = jnp.dot(hdn.astype(bf16), wd_ref[...], preferred_element_type=f32)
        xs_ref[:, :D_MODEL] = (y * w_row).astype(bf16)
        store_tile(i).start()

    @pl.when(i == pl.num_programs(0) - 1)
    def _():
        for back in range(1, XS_RING + 1):
            @pl.when(n_used - back >= 0)
            def _():
                store_tile(n_used - back).wait()


def _experts(tile_exp, first, next_exp, slot, n_used, xs, wg, wu, wd, rows_per_tile):
    r = rows_per_tile
    n_tiles = xs.shape[0] // r
    grid_spec = pltpu.PrefetchScalarGridSpec(
        num_scalar_prefetch=5,
        grid=(n_tiles,),
        in_specs=[pl.BlockSpec(memory_space=pl.ANY),
                  pl.BlockSpec(memory_space=pl.ANY),
                  pl.BlockSpec(memory_space=pl.ANY),
                  pl.BlockSpec(memory_space=pl.ANY)],
        out_specs=pl.BlockSpec(memory_space=pl.ANY),
        scratch_shapes=[pltpu.VMEM((XS_RING, r, XS_COLS), bf16),
                        pltpu.VMEM((2, D_MODEL, D_FF_E), f32),
                        pltpu.VMEM((2, D_MODEL, D_FF_E), f32),
                        pltpu.VMEM((2, D_FF_E, D_MODEL), f32),
                        pltpu.VMEM((D_MODEL, D_FF_E), bf16),
                        pltpu.VMEM((D_MODEL, D_FF_E), bf16),
                        pltpu.VMEM((D_FF_E, D_MODEL), bf16),
                        pltpu.SemaphoreType.DMA((XS_RING,)),
                        pltpu.SemaphoreType.DMA((XS_RING,)),
                        pltpu.SemaphoreType.DMA((2,))])
    return pl.pallas_call(
        _experts_body,
        grid_spec=grid_spec,
        out_shape=jax.ShapeDtypeStruct(xs.shape, xs.dtype),
        input_output_aliases={5: 0},
        compiler_params=pltpu.CompilerParams(
            dimension_semantics=("arbitrary",), vmem_limit_bytes=VMEM_LIMIT),
        name="experts",
    )(tile_exp, first, next_exp, slot, n_used, xs, wg, wu, wd)


def _combine_out_body(runs_cur_ref, runs_nxt_ref, h_ref, slot_ref, p_ref, ys_ref,
                      wpg_ref, bpg_ref, wpp_ref, gple_ref, gfin_ref,
                      o_ref, ybuf_ref, moe_ref, sem_ref):
    tk = h_ref.shape[0]
    lb = ybuf_ref.shape[1]
    step = pl.program_id(0)
    n = pl.num_programs(0) - 1
    cur = jnp.minimum(step, n - 1) % 2

    def piece(s, local_row, global_row, rows):
        return pltpu.make_async_copy(ys_ref.at[pl.ds(global_row, rows), pl.ds(0, D_MODEL)],
                                     ybuf_ref.at[s, pl.ds(local_row, rows), :], sem_ref.at[s])

    @pl.when(step == 0)
    def _():
        ybuf_ref[...] = jnp.zeros(ybuf_ref.shape, bf16)
        _for_each_run_piece(runs_cur_ref, lambda l, g, r: piece(0, l, g, r).start())

    @pl.when(step + 1 < n)
    def _():
        _for_each_run_piece(runs_nxt_ref, lambda l, g, r: piece(1 - cur, l, g, r).start())

    run_rows = _tile_run_rows(runs_cur_ref)

    @pl.when(step < n)
    def _():
        _wait_rows(run_rows, lambda r: piece(cur, 0, 0, r))

    def work(rows, finish, unsort):
        if finish:
            h2 = h_ref[...] + moe_ref[...]
            gate = jax.nn.sigmoid(jnp.dot(h2.astype(bf16), wpg_ref[...], preferred_element_type=f32)
                                  + bpg_ref[...])
            pp = jnp.dot(p_ref[...].astype(bf16), wpp_ref[...], preferred_element_type=f32)
            h3 = h2 + gate * _rms(pp, gple_ref[...])
            o_ref[...] = _rms(h3, gfin_ref[...])
        if unsort:
            col = lax.broadcasted_iota(i32, (tk, rows), 1)
            unperm = jnp.where((col == slot_ref[:, 0:1]) | (col == slot_ref[:, 1:2]), 1.0, 0.0)
            moe_ref[...] = jnp.dot(unperm.astype(bf16), ybuf_ref[cur, :rows, :],
                                   preferred_element_type=f32)

    short = lb - 256
    is_short = run_rows <= short
    first = step == 0
    mid = (step > 0) & (step < n)
    pl.when(first & is_short)(functools.partial(work, short, False, True))
    pl.when(first & jnp.logical_not(is_short))(functools.partial(work, lb, False, True))
    pl.when(mid & is_short)(functools.partial(work, short, True, True))
    pl.when(mid & jnp.logical_not(is_short))(functools.partial(work, lb, True, True))
    pl.when(step == n)(functools.partial(work, lb, True, False))


def _combine_out(run_tiles, h2d, slot_t, p2d, ys, wpg, bpg, wpp, gple, gfin):
    t = h2d.shape[0]
    tk = TL_SORT
    nt = t // tk
    const = lambda shape: pl.BlockSpec(shape, lambda i: (0,) * len(shape))
    runs = lambda imap: pl.BlockSpec((1, 1, 3 * N_EXPERTS), imap, memory_space=pltpu.SMEM)
    uns = lambda s: jnp.minimum(s, nt - 1)
    fin = lambda s: jnp.maximum(s - 1, 0)
    return pl.pallas_call(
        _combine_out_body,
        grid=(nt + 1,),
        out_shape=jax.ShapeDtypeStruct((t, D_MODEL), f32),
        in_specs=[runs(lambda s: (uns(s), 0, 0)),
                  runs(lambda s: (uns(s + 1), 0, 0)),
                  pl.BlockSpec((tk, D_MODEL), lambda s: (fin(s), 0)),
                  pl.BlockSpec((tk, 8), lambda s: (uns(s), 0)),
                  pl.BlockSpec((tk, PLE_DIM), lambda s: (fin(s), 0)),
                  pl.BlockSpec(memory_space=pl.ANY),
                  const((D_MODEL, D_MODEL)), const((1, D_MODEL)),
                  const((PLE_DIM, D_MODEL)), const((1, D_MODEL)), const((1, D_MODEL))],
        out_specs=pl.BlockSpec((tk, D_MODEL), lambda s: (fin(s), 0)),
        scratch_shapes=[pltpu.VMEM((2, LB_SORT, D_MODEL), bf16),
                        pltpu.VMEM((tk, D_MODEL), f32),
                        pltpu.SemaphoreType.DMA((2,))],
        compiler_params=pltpu.CompilerParams(
            dimension_semantics=("arbitrary",), vmem_limit_bytes=VMEM_LIMIT),
        name="combine_out",
    )(run_tiles, run_tiles, h2d, slot_t, p2d, ys, wpg, bpg, wpp, gple, gfin)


def _head_major_to_tile_major(a, axis):
    axis = axis % a.ndim
    shape = a.shape
    a = a.reshape(shape[:axis] + (N_KV, GQA, HEAD_DIM) + shape[axis + 1:])
    return jnp.swapaxes(a, axis, axis + 1).reshape(shape)


def kernel(x, p, rel_bias, g_mix, w_in, ln_v_g, ln_v_b, w_spatial, b_spatial, sink, g_out_grp, w_out,
           g_ffn, w_router_group, b_router_group, w_router_expert, b_router_expert, w_gate_e, w_up_e,
           w_down_e, w_ple_proj, g_ple, w_ple_gate, b_ple_gate, g_final):
    b, s, d = x.shape
    t = b * s
    depth = g_mix.shape[0]
    assert depth == 1 and d == D_MODEL
    c1, c2 = 2 * D_A, 2 * D_A + D_B
    bias = _bias_table(rel_bias * LOG2E)
    hcur = x.astype(f32)
    for li in range(depth):
        col_scale = np.ones((1, D_IN), np.float32)
        col_scale[:, c1:c2] = LOG2E * HEAD_DIM ** -0.5
        win = (w_in[li] * col_scale).astype(bf16)
        bs = jnp.broadcast_to(b_spatial[li][:, :, None], (A_HEADS, BLK, BLK)).astype(f32)
        gout = g_out_grp[li]
        gout_b = _head_major_to_tile_major(gout[D_A:], 0)[None]
        wo = w_out[li]
        wout = jnp.concatenate([wo[:D_A], _head_major_to_tile_major(wo[D_A:], 0)],
                               axis=0).astype(bf16)
        wr = jnp.concatenate([
            w_router_group[li], jnp.zeros((D_MODEL, 8 - N_GROUPS), f32), w_router_expert[li],
            jnp.zeros((D_MODEL, BLK - ROUTE_ROWS), f32)], axis=1).astype(bf16)
        br = jnp.concatenate([b_router_group[li], jnp.full((8 - N_GROUPS,), NEG, f32),
                              b_router_expert[li]])
        br = jnp.broadcast_to(br[:, None], (ROUTE_ROWS, TQ_ATT))

        ya, q, k, v = _mix_in(hcur.reshape(t, d), g_mix[li][None], win, ln_v_g[li][None],
                              ln_v_b[li][None], w_spatial[li].astype(bf16), bs, gout[None, :D_A])
        h, m_rows, route_e, route_w = _attn_out(
            sink[li].astype(f32) * LOG2E, hcur, ya.reshape(b, s, D_A), q.reshape(b, s, D_B),
            k.reshape(b, s, BLK), v.reshape(b, s, BLK), bias, gout_b, wout,
            g_ffn[li][None], wr, br)

        r = R_EXP
        n_sort = t // TL_SORT
        max_rows = 2 * t + n_sort * N_EXPERTS * (RUN_ALIGN - 1)
        n_tiles = -(-max_rows // r) + N_EXPERTS
        assert n_tiles * r < 2 ** 16, "route_pos prefix sums hold row offsets in two bf16 bytes"
        slot, runs, offs = _route_pos(route_e, r)
        starts = offs[0, :, 0]
        padded = offs[1, :, 0]
        ends = starts + padded
        n_used = ends[-1] // r
        tile_idx = jnp.minimum(jnp.arange(n_tiles, dtype=i32), n_used - 1)
        tile_exp = jnp.minimum(
            jnp.sum((ends[None, :] <= (tile_idx * r)[:, None]).astype(i32), axis=1), N_EXPERTS - 1)
        tile_exp = tile_exp.astype(i32)
        first = jnp.concatenate([jnp.ones((1,), i32),
                                 (tile_exp[1:] != tile_exp[:-1]).astype(i32)])
        next_tile = ends[tile_exp] // r
        next_exp = jnp.where(next_tile < n_used, tile_exp[jnp.minimum(next_tile, n_tiles - 1)], -1)
        fetch_slot = (jnp.cumsum(first) - 1) % 2
        run_tiles = runs[:, :, :, 0].reshape(n_sort, 1, 3 * N_EXPERTS)
        xs, slot_t = _sort_rows(ends, padded, offs[2, :, 0], run_tiles, m_rows, slot, route_w,
                                n_tiles * r, r)
        ys = _experts(tile_exp, first, next_exp.astype(i32), fetch_slot.astype(i32),
                      n_used.reshape(1), xs, w_gate_e[li], w_up_e[li], w_down_e[li], r)

        out = _combine_out(run_tiles, h.reshape(t, d), slot_t,
                           p[li].reshape(t, PLE_DIM), ys, w_ple_gate[li].astype(bf16),
                           b_ple_gate[li][None], w_ple_proj[li].astype(bf16), g_ple[li][None],
                           g_final[None])
        hcur = out.reshape(b, s, d)
    return hcur
```

```python
import functools
import math

import jax
import jax.numpy as jnp
import numpy as np
from jax import lax
from jax.experimental import pallas as pl
from jax.experimental.pallas import tpu as pltpu

D_MODEL = 1024
D_A = 512
D_B = 512
BLK = 128
A_HEADS = 4
HEAD_DIM = 64
N_HEADS = 8
N_KV = 2
GQA = 4
WINDOW = 128
NUM_BUCKETS = 32
MAX_DIST = 128
D_IN = 2 * D_A + D_B + 2 * N_KV * HEAD_DIM
N_GROUPS = 4
E_PER_GROUP = 8
N_EXPERTS = 32
D_FF_E = 256
PLE_DIM = 256
EPS = 1e-6
NEG = -1e30
LOG2E = math.log2(math.e)

TM_MIX = 512
TQ_ATT = 512
TL_SORT = 512
RUN_ALIGN = 16
SMALL_BITS = 2
BIG_PIECE = RUN_ALIGN << SMALL_BITS
LB_SORT = 2 * TL_SORT + N_EXPERTS * RUN_ALIGN
R_EXP = 512
XS_COLS = D_MODEL + BLK
XS_RING = 4
XS_AHEAD = 2
ROUTE_ROWS = 8 + N_EXPERTS
VMEM_LIMIT = 48 * 1024 * 1024

f32 = jnp.float32
bf16 = jnp.bfloat16
i32 = jnp.int32


def _rms(x, g):
    return x * lax.rsqrt(jnp.mean(x * x, axis=-1, keepdims=True) + EPS) * g


def _gelu_tanh(x):
    c = math.sqrt(2.0 / math.pi)
    return x * (0.5 * (1.0 + jnp.tanh(c * (x + 0.044715 * (x * x * x)))))


def _bucket_row():
    n = NUM_BUCKETS // 2
    max_exact = n // 2
    rel = np.broadcast_to(np.arange(4 * BLK)[None, :] - BLK, (8, 4 * BLK))
    ret = np.where(rel > 0, n, 0)
    a = np.abs(rel)
    large = max_exact + (np.log(np.maximum(a, 1).astype(np.float64) / max_exact)
                         / math.log(MAX_DIST / max_exact) * (n - max_exact)).astype(np.int32)
    large = np.minimum(large, n - 1)
    bucket = ret + np.where(a < max_exact, a, large)
    return np.where(a <= WINDOW, bucket, -1).astype(np.int32)


def _bias_body(rb_ref, bucket_ref, o_ref):
    bucket = bucket_ref[...]
    row = lax.broadcasted_iota(i32, (BLK, BLK), 0)
    lane = lax.broadcasted_iota(i32, (BLK, BLK), 1)
    for h in range(N_HEADS):
        base = jnp.full(bucket.shape, NEG, f32)
        for b in range(NUM_BUCKETS):
            base = jnp.where(bucket == b, rb_ref[b, h], base)
        prev = jnp.full((BLK, BLK), NEG, f32)
        for c in range(3):
            tile = jnp.broadcast_to(base[0:1, c * BLK:(c + 1) * BLK], (BLK, BLK))
            rolled = pltpu.roll(tile, 0, 1, stride=1, stride_axis=0)
            o_ref[h, :, c * BLK:(c + 1) * BLK] = jnp.where(lane >= row, rolled, prev)
            prev = rolled


def _mix_in_body(x_ref, gmix_ref, win_ref, lng_ref, lnb_ref, ws_ref, bs_ref, gout_ref,
                 ya_ref, q_ref, k_ref, v_ref):
    tm = x_ref.shape[0]
    nc = tm // BLK
    a = _rms(x_ref[...], gmix_ref[...])
    z = jnp.dot(a.astype(bf16), win_ref[...], preferred_element_type=f32)
    uv = _gelu_tanh(z[:, :2 * D_A])
    u = uv[:, :D_A]
    v = uv[:, D_A:]
    mu = jnp.mean(v, axis=-1, keepdims=True)
    vc = v - mu
    var = jnp.mean(vc * vc, axis=-1, keepdims=True)
    vn = (vc * lax.rsqrt(var + EPS) * lng_ref[...] + lnb_ref[...]).astype(bf16)
    cols = []
    for h in range(A_HEADS):
        rhs = jnp.concatenate(
            [vn[c * BLK:(c + 1) * BLK, h * BLK:(h + 1) * BLK] for c in range(nc)], axis=1)
        r = jnp.dot(ws_ref[h], rhs, preferred_element_type=f32)
        cols.append(jnp.concatenate(
            [r[:, c * BLK:(c + 1) * BLK] + bs_ref[h] for c in range(nc)], axis=0))
    sv = jnp.concatenate(cols, axis=1)
    ya_ref[...] = _rms(u * sv, gout_ref[...]).astype(bf16)
    qt = [z[:, 2 * D_A + j * BLK:2 * D_A + (j + 1) * BLK] for j in range(D_B // BLK)]
    low_half = lax.broadcasted_iota(i32, (tm, BLK), 1) < HEAD_DIM
    for g in range(GQA):
        a0, a1 = qt[g // 2], qt[GQA // 2 + g // 2]
        tile = (jnp.where(low_half, a0, pltpu.roll(a1, HEAD_DIM, 1)) if g % 2 == 0
                else jnp.where(low_half, pltpu.roll(a0, HEAD_DIM, 1), a1))
        q_ref[:, g * BLK:(g + 1) * BLK] = tile.astype(bf16)
    k_ref[...] = z[:, 2 * D_A + D_B:2 * D_A + D_B + BLK].astype(bf16)
    v_ref[...] = z[:, 2 * D_A + D_B + BLK:].astype(bf16)


def _mix_in(x2, gmix, win, lng, lnb, ws, bs, gout_a):
    t = x2.shape[0]
    tm = TM_MIX
    const = lambda shape: pl.BlockSpec(shape, lambda i: (0,) * len(shape))
    return pl.pallas_call(
        _mix_in_body,
        grid=(t // tm,),
        out_shape=(jax.ShapeDtypeStruct((t, D_A), bf16),
                   jax.ShapeDtypeStruct((t, D_B), bf16),
                   jax.ShapeDtypeStruct((t, BLK), bf16),
                   jax.ShapeDtypeStruct((t, BLK), bf16)),
        in_specs=[pl.BlockSpec((tm, D_MODEL), lambda i: (i, 0)),
                  const((1, D_MODEL)), const((D_MODEL, D_IN)),
                  const((1, D_A)), const((1, D_A)),
                  const((A_HEADS, BLK, BLK)), const((A_HEADS, BLK, BLK)),
                  const((1, D_A))],
        out_specs=(pl.BlockSpec((tm, D_A), lambda i: (i, 0)),
                   pl.BlockSpec((tm, D_B), lambda i: (i, 0)),
                   pl.BlockSpec((tm, BLK), lambda i: (i, 0)),
                   pl.BlockSpec((tm, BLK), lambda i: (i, 0))),
        compiler_params=pltpu.CompilerParams(
            dimension_semantics=("parallel",), vmem_limit_bytes=VMEM_LIMIT),
        name="mix_in",
    )(x2, gmix, win, lng, lnb, ws, bs, gout_a)


def _attn_epilogue(sink_ref, x_ref, ya_ref, q_ref, kp_ref, km_ref, kn_ref, vp_ref, vm_ref, vn_ref,
                   rb_ref, bucket_ref, goutb_ref, wout_ref, gffn_ref, wr_ref, br_ref,
                   h_ref, m_ref, re_ref, rw_ref,
                   kf_ref, vf_ref, e_ref, yb_ref, bias_ref, *, seq_len):
    tq = x_ref.shape[1]
    ybn = _rms(yb_ref[1], goutb_ref[...]).astype(bf16)
    y = jnp.concatenate([ya_ref[0], ybn], axis=1)
    h = x_ref[0] + jnp.dot(y, wout_ref[...], preferred_element_type=f32)
    h_ref[0] = h
    m = _rms(h, gffn_ref[...])
    m_ref[...] = m.astype(bf16)

    logit_t = jnp.dot(m.astype(bf16), wr_ref[...], preferred_element_type=f32)
    logit = jnp.transpose(logit_t)[:ROUTE_ROWS] + br_ref[...]
    sub = lax.broadcasted_iota(i32, (8, tq), 0)
    lg = logit[0:8]
    mg = jnp.max(lg, axis=0, keepdims=True)
    pg_top = 1.0 / jnp.sum(jnp.exp(lg - mg), axis=0, keepdims=True)
    g_idx = jnp.min(jnp.where(lg == mg, sub, 8), axis=0, keepdims=True)
    sel = logit[8:16]
    for g in range(1, N_GROUPS):
        sel = jnp.where(g_idx == g, logit[8 + 8 * g:16 + 8 * g], sel)
    m1 = jnp.max(sel, axis=0, keepdims=True)
    i1 = jnp.min(jnp.where(sel == m1, sub, 8), axis=0, keepdims=True)
    sel2 = jnp.where(sub == i1, -jnp.inf, sel)
    m2 = jnp.max(sel2, axis=0, keepdims=True)
    i2 = jnp.min(jnp.where(sel2 == m2, sub, 8), axis=0, keepdims=True)
    r = jnp.exp(m2 - m1)
    w1 = pg_top / (1.0 + r)
    w2 = pg_top * r / (1.0 + r)
    e1 = g_idx * E_PER_GROUP + i1
    e2 = g_idx * E_PER_GROUP + i2
    re_ref[...] = jnp.where(sub == 0, e1, jnp.where(sub == 1, e2, 0))
    rw_ref[...] = jnp.where(sub == 0, w1, jnp.where(sub == 1, w2, 0.0))


def _attn_attention(sink_ref, x_ref, ya_ref, q_ref, kp_ref, km_ref, kn_ref, vp_ref, vm_ref, vn_ref,
                    rb_ref, bucket_ref, goutb_ref, wout_ref, gffn_ref, wr_ref, br_ref,
                    h_ref, m_ref, re_ref, rw_ref,
                    kf_ref, vf_ref, e_ref, yb_ref, bias_ref, *, seq_len):
    tq = x_ref.shape[1]
    nb = tq // BLK
    ti = pl.program_id(0) % (seq_len // tq)
    kf_ref[0:BLK] = kp_ref[0]
    kf_ref[BLK:BLK + tq] = km_ref[0]
    kf_ref[BLK + tq:] = kn_ref[0]
    vlane = lax.broadcasted_iota(i32, (BLK, BLK), 1)
    for src, r0, r1 in ((vp_ref, 0, BLK), (vn_ref, BLK + tq, tq + 2 * BLK)) + tuple(
            (vm_ref.at[:, pl.ds(c * BLK, BLK), :], BLK + c * BLK, BLK + (c + 1) * BLK)
            for c in range(nb)):
        vals = src[0].astype(f32)
        vf_ref[0, r0:r1] = jnp.where(vlane < HEAD_DIM, vals,
                                     jnp.where(vlane == HEAD_DIM, 1.0, 0.0)).astype(bf16)
        vf_ref[1, r0:r1] = jnp.where(vlane >= HEAD_DIM, vals,
                                     jnp.where(vlane == 0, 1.0, 0.0)).astype(bf16)

    lane = lax.broadcasted_iota(i32, (BLK, BLK), 1)
    low_half = lane < HEAD_DIM

    for n in range(nb):
        r0 = n * BLK
        qb = q_ref[0, pl.ds(r0, BLK), :]
        kb = kf_ref[pl.ds(r0, 3 * BLK), :]
        zero = jnp.zeros((BLK, BLK), bf16)
        lhs = []
        for kh in range(N_KV):
            for g in range(GQA):
                tile = qb[:, g * BLK:(g + 1) * BLK]
                lhs.append(jnp.where(low_half, tile, zero) if kh == 0
                           else jnp.where(low_half, zero, tile))
        lhs = jnp.concatenate(lhs, axis=0)
        s_all = lax.dot_general(lhs, kb, (((1,), (1,)), ((), ())),
                                preferred_element_type=f32)
        blk = ti * nb + n
        pen = [jnp.where(blk == 0, NEG, 0.0) if n == 0 else None, None,
               jnp.where((blk + 1) * BLK >= seq_len, NEG, 0.0) if n == nb - 1 else None]
        sink_e = []
        for h in range(N_HEADS):
            bias_h = bias_ref[h]
            bias_h = jnp.concatenate(
                [bias_h[:, j * BLK:(j + 1) * BLK] if pen[j] is None
                 else bias_h[:, j * BLK:(j + 1) * BLK] + pen[j] for j in range(3)], axis=1)
            s = s_all[h * BLK:(h + 1) * BLK] + bias_h
            sk = sink_ref[h]
            mrow = jnp.maximum(jnp.max(s, axis=-1, keepdims=True), sk)
            e_ref[n, h * BLK:(h + 1) * BLK, :] = jnp.exp2(s - mrow).astype(bf16)
            sink_e.append(jnp.exp2(sk - mrow))
        half = GQA * BLK
        pv = [jnp.dot(e_ref[n, kh * half:(kh + 1) * half, :], vf_ref[kh, pl.ds(r0, 3 * BLK), :],
                      preferred_element_type=f32) for kh in range(N_KV)]

        def head_out(h):
            kh, g = divmod(h, GQA)
            rows = pv[kh][g * BLK:(g + 1) * BLK]
            ones_col = HEAD_DIM if kh == 0 else 0
            return rows * (1.0 / (rows[:, ones_col:ones_col + 1] + sink_e[h]))

        for g in range(GQA):
            yb_ref[0, pl.ds(r0, BLK), g * BLK:(g + 1) * BLK] = jnp.where(
                low_half, head_out(g), head_out(GQA + g))

    yb_ref[1] = yb_ref[0]


def _attn_out_body(*refs, seq_len):
    step = pl.program_id(0)
    n_tok_tiles = pl.num_programs(0) - 1

    @pl.when(step == 0)
    def _():
        _bias_body(refs[10], refs[11], refs[-1])
        _attn_attention(*refs, seq_len=seq_len)

    @pl.when((step > 0) & (step < n_tok_tiles))
    def _():
        _attn_epilogue(*refs, seq_len=seq_len)
        _attn_attention(*refs, seq_len=seq_len)

    @pl.when(step == n_tok_tiles)
    def _():
        _attn_epilogue(*refs, seq_len=seq_len)


def _attn_out(sink, x, ya, q, k, v, rel_bias, goutb, wout, gffn, wr, br):
    b, s, _ = x.shape
    tq = TQ_ATT
    nb = tq // BLK
    nblk = s // BLK
    t = b * s
    nq = s // tq
    n_tok_tiles = b * nq
    att = lambda s: jnp.minimum(s, n_tok_tiles - 1)
    epi = lambda s: jnp.maximum(s - 1, 0)
    const = lambda shape: pl.BlockSpec(shape, lambda s, *_: (0,) * len(shape))
    tok_att = lambda w: pl.BlockSpec((1, tq, w), lambda s, *_: (att(s) // nq, att(s) % nq, 0))
    tok_epi = lambda w: pl.BlockSpec((1, tq, w), lambda s, *_: (epi(s) // nq, epi(s) % nq, 0))
    prev = pl.BlockSpec((1, BLK, BLK), lambda s, *_: (
        att(s) // nq, jnp.maximum((att(s) % nq) * nb - 1, 0), 0))
    nxt = pl.BlockSpec((1, BLK, BLK), lambda s, *_: (
        att(s) // nq, jnp.minimum((att(s) % nq) * nb + nb, nblk - 1), 0))
    rows = pl.BlockSpec((tq, D_MODEL), lambda s, *_: (epi(s), 0))
    lanes = lambda rows: pl.BlockSpec((rows, tq), lambda s, *_: (0, epi(s)))
    grid_spec = pltpu.PrefetchScalarGridSpec(
        num_scalar_prefetch=1,
        grid=(n_tok_tiles + 1,),
        in_specs=[tok_epi(D_MODEL), tok_epi(D_A), tok_att(D_B),
                  prev, tok_att(BLK), nxt, prev, tok_att(BLK), nxt,
                  pl.BlockSpec(memory_space=pltpu.SMEM), const((8, 4 * BLK)),
                  const((1, D_B)), const((D_MODEL, D_MODEL)),
                  const((1, D_MODEL)), const((D_MODEL, BLK)), const((ROUTE_ROWS, tq))],
        out_specs=(tok_epi(D_MODEL), rows, lanes(8), lanes(8)),
        scratch_shapes=[pltpu.VMEM((tq + 2 * BLK, BLK), bf16),
                        pltpu.VMEM((N_KV, tq + 2 * BLK, BLK), bf16),
                        pltpu.VMEM((nb, N_HEADS * BLK, 3 * BLK), bf16),
                        pltpu.VMEM((2, tq, D_B), f32),
                        pltpu.VMEM((N_HEADS, BLK, 3 * BLK), f32)])
    return pl.pallas_call(
        functools.partial(_attn_out_body, seq_len=s),
        grid_spec=grid_spec,
        out_shape=(jax.ShapeDtypeStruct((b, s, D_MODEL), f32),
                   jax.ShapeDtypeStruct((t, D_MODEL), bf16),
                   jax.ShapeDtypeStruct((8, t), i32),
                   jax.ShapeDtypeStruct((8, t), f32)),
        compiler_params=pltpu.CompilerParams(
            dimension_semantics=("arbitrary",), vmem_limit_bytes=VMEM_LIMIT),
        name="attn_out",
    )(sink, x, ya, q, k, k, k, v, v, v, rel_bias, jnp.asarray(_bucket_row()), goutb, wout, gffn, wr, br)


def _exclusive_prefix(vals):
    a = lax.broadcasted_iota(i32, (N_EXPERTS, N_EXPERTS), 0)
    c = lax.broadcasted_iota(i32, (N_EXPERTS, N_EXPERTS), 1)
    low = jnp.where(c < a, 1.0, 0.0).astype(bf16)
    hi = (vals >> 8).astype(f32).astype(bf16)
    lo = (vals & 255).astype(f32).astype(bf16)
    return (jnp.dot(low, hi, preferred_element_type=f32) * 256.0
            + jnp.dot(low, lo, preferred_element_type=f32)).astype(i32)


def _round_up_pow2(vals, mult):
    log_m = mult.bit_length() - 1
    return ((vals + (mult - 1)) >> log_m) << log_m


def _route_pos_body(re_ref, slot_ref, runs_ref, offs_ref, tri_ref, *, rows_per_tile, tl):
    n_tok_tiles = re_ref.shape[1] // tl
    a = lax.broadcasted_iota(i32, (tl, tl), 0)
    c = lax.broadcasted_iota(i32, (tl, tl), 1)
    tri_ref[...] = jnp.where(a < c, 1.0, 0.0).astype(bf16)
    eid = lax.broadcasted_iota(i32, (N_EXPERTS, tl), 0)

    def tile_hits(j):
        cols = pl.ds(pl.multiple_of(j * tl, tl), tl)
        hit1 = eid == re_ref[0:1, cols]
        hit2 = eid == re_ref[1:2, cols]
        onehot = jnp.where(hit1 | hit2, 1.0, 0.0)
        cnt = jnp.sum(onehot, axis=1, keepdims=True).astype(i32)
        run_len = jnp.broadcast_to(_round_up_pow2(cnt, RUN_ALIGN), (N_EXPERTS, BLK))
        return cols, hit1, hit2, onehot, run_len

    total = lax.fori_loop(0, n_tok_tiles, lambda j, acc: acc + tile_hits(j)[4],
                          jnp.zeros((N_EXPERTS, BLK), i32))
    padded = _round_up_pow2(total, rows_per_tile)
    starts = _exclusive_prefix(padded)
    offs_ref[0] = starts
    offs_ref[1] = padded
    offs_ref[2] = total

    def place(j, global_off):
        cols, hit1, hit2, onehot, run_len = tile_hits(j)
        local_off = _exclusive_prefix(run_len)
        before = jnp.dot(onehot.astype(bf16), tri_ref[...], preferred_element_type=f32)
        slot = before + local_off[:, 0:1].astype(f32)
        s1 = jnp.sum(jnp.where(hit1, slot, 0.0), axis=0, keepdims=True)
        s2 = jnp.sum(jnp.where(hit2, slot, 0.0), axis=0, keepdims=True)
        sub = lax.broadcasted_iota(i32, (8, tl), 0)
        slot_ref[:, cols] = jnp.where(sub == 0, s1.astype(i32),
                                      jnp.where(sub == 1, s2.astype(i32), 0))
        runs_ref[j, 0] = local_off
        runs_ref[j, 1] = run_len
        runs_ref[j, 2] = global_off
        return global_off + run_len

    lax.fori_loop(0, n_tok_tiles, place, starts)


def _route_pos(route_e, rows_per_tile):
    t = route_e.shape[1]
    tl = TL_SORT
    return pl.pallas_call(
        functools.partial(_route_pos_body, rows_per_tile=rows_per_tile, tl=tl),
        out_shape=(jax.ShapeDtypeStruct((8, t), i32),
                   jax.ShapeDtypeStruct((t // tl, 3, N_EXPERTS, BLK), i32),
                   jax.ShapeDtypeStruct((3, N_EXPERTS, BLK), i32)),
        scratch_shapes=[pltpu.VMEM((tl, tl), bf16)],
        compiler_params=pltpu.CompilerParams(vmem_limit_bytes=VMEM_LIMIT),
        name="route_pos",
    )(route_e)


def _for_each_run_piece(runs_ref, fn):
    def per_expert(e, carry):
        local_off = runs_ref[0, 0, e]
        length = runs_ref[0, 0, N_EXPERTS + e]
        global_off = runs_ref[0, 0, 2 * N_EXPERTS + e]
        units = length >> (RUN_ALIGN.bit_length() - 1)

        def big_piece(k, inner):
            done = k * BIG_PIECE
            fn(pl.multiple_of(local_off + done, RUN_ALIGN),
               pl.multiple_of(global_off + done, RUN_ALIGN), BIG_PIECE)
            return inner

        lax.fori_loop(0, units >> SMALL_BITS, big_piece, 0)
        for b in range(SMALL_BITS):
            @pl.when(((units >> b) & 1) == 1)
            def _():
                done = ((units >> (b + 1)) << (b + 1)) * RUN_ALIGN
                fn(pl.multiple_of(local_off + done, RUN_ALIGN),
                   pl.multiple_of(global_off + done, RUN_ALIGN), RUN_ALIGN << b)
        return carry

    lax.fori_loop(0, N_EXPERTS, per_expert, 0)


def _tile_run_rows(runs_ref):
    return lax.fori_loop(0, N_EXPERTS, lambda e, acc: acc + runs_ref[0, 0, N_EXPERTS + e], 0)


def _wait_rows(total_rows, make_copy):
    units = total_rows >> (RUN_ALIGN.bit_length() - 1)
    for b in range((LB_SORT // RUN_ALIGN).bit_length()):
        @pl.when(((units >> b) & 1) == 1)
        def _():
            make_copy(RUN_ALIGN << b).wait()


def _sort_rows_body(ends_ref, padded_ref, total_ref, runs_ref, m_ref, slot_ref, rw_ref,
                    xs_ref, slot_t_ref,
                    local_ref, zero_ref, pending_ref, sem_ref, zsem_ref, *, rows_per_tile):
    tl = m_ref.shape[0]
    lb = local_ref.shape[1]
    step = pl.program_id(0)
    cur = step % 2
    n_tiles = xs_ref.shape[0] // rows_per_tile
    n_used = ends_ref[N_EXPERTS - 1] // rows_per_tile

    def zero_rows(start, rows, wait):
        cp = pltpu.make_async_copy(zero_ref.at[pl.ds(0, rows), :],
                                   xs_ref.at[pl.ds(pl.multiple_of(start, RUN_ALIGN), rows), :],
                                   zsem_ref)
        cp.wait() if wait else cp.start()

    def zero_fill(wait):
        for e in range(N_EXPERTS):
            pad_units = (padded_ref[e] - total_ref[e]) >> (RUN_ALIGN.bit_length() - 1)
            for bit in range((rows_per_tile // RUN_ALIGN).bit_length() - 1):
                @pl.when(((pad_units >> bit) & 1) == 1)
                def _():
                    done = ((pad_units >> (bit + 1)) << (bit + 1)) * RUN_ALIGN
                    zero_rows(ends_ref[e] - (padded_ref[e] - total_ref[e]) + done,
                              RUN_ALIGN << bit, wait)

        def spare(j, carry):
            zero_rows((n_used + j) * rows_per_tile, rows_per_tile, wait)
            return carry

        lax.fori_loop(0, n_tiles - n_used, spare, 0)

    @pl.when(step == 0)
    def _():
        zero_ref[...] = jnp.zeros(zero_ref.shape, bf16)
        zero_fill(wait=False)

    s1 = slot_ref[0:1, :]
    s2 = slot_ref[1:2, :]
    w1 = rw_ref[0:1, :]
    w2 = rw_ref[1:2, :]
    run_rows = _tile_run_rows(runs_ref)

    def sort_rows_upto(n):
        srow = lax.broadcasted_iota(i32, (n, tl), 0)
        p1 = srow == s1
        p2 = srow == s2
        perm = jnp.where(p1 | p2, 1.0, 0.0).astype(bf16)
        rows = jnp.dot(perm, m_ref[...], preferred_element_type=f32)
        w = jnp.sum(jnp.where(p1, w1, 0.0) + jnp.where(p2, w2, 0.0), axis=1, keepdims=True)
        w_hi = w.astype(bf16).astype(f32)
        w_lo = w - w_hi
        lane = lax.broadcasted_iota(i32, (n, BLK), 1)
        local_ref[cur, :n, :D_MODEL] = rows.astype(bf16)
        local_ref[cur, :n, D_MODEL:] = jnp.where(
            lane == 0, w_hi, jnp.where(lane == 1, w_lo, 0.0)).astype(bf16)
        sub_t = lax.broadcasted_iota(i32, (BLK, tl), 0)
        slots = jnp.where(sub_t == 0, s1.astype(f32), jnp.where(sub_t == 1, s2.astype(f32), 0.0))
        slot_t_ref[...] = jnp.transpose(slots)[:, :8].astype(i32)

    def piece(s, local_row, global_row, rows):
        return pltpu.make_async_copy(local_ref.at[s, pl.ds(local_row, rows), :],
                                     xs_ref.at[pl.ds(global_row, rows), :], sem_ref.at[s])

    short = lb - 256
    pl.when(run_rows <= short)(functools.partial(sort_rows_upto, short))
    pl.when(run_rows > short)(functools.partial(sort_rows_upto, lb))

    _for_each_run_piece(runs_ref, lambda l, g, n: piece(cur, l, g, n).start())

    @pl.when(step > 0)
    def _():
        _wait_rows(pending_ref[0], lambda n: piece(1 - cur, 0, 0, n))

    pending_ref[0] = run_rows

    @pl.when(step == pl.num_programs(0) - 1)
    def _():
        _wait_rows(pending_ref[0], lambda n: piece(cur, 0, 0, n))
        zero_fill(wait=True)


def _sort_rows(ends, padded, totals, run_tiles, m, slot, route_w, total_rows, rows_per_tile):
    t = m.shape[0]
    tl = TL_SORT
    grid_spec = pltpu.PrefetchScalarGridSpec(
        num_scalar_prefetch=3,
        grid=(t // tl,),
        in_specs=[pl.BlockSpec((1, 1, 3 * N_EXPERTS), lambda i, *_: (i, 0, 0),
                               memory_space=pltpu.SMEM),
                  pl.BlockSpec((tl, D_MODEL), lambda i, *_: (i, 0)),
                  pl.BlockSpec((8, tl), lambda i, *_: (0, i)),
                  pl.BlockSpec((8, tl), lambda i, *_: (0, i))],
        out_specs=(pl.BlockSpec(memory_space=pl.ANY),
                   pl.BlockSpec((tl, 8), lambda i, *_: (i, 0))),
        scratch_shapes=[pltpu.VMEM((2, LB_SORT, XS_COLS), bf16),
                        pltpu.VMEM((rows_per_tile, XS_COLS), bf16),
                        pltpu.SMEM((1,), i32),
                        pltpu.SemaphoreType.DMA((2,)),
                        pltpu.SemaphoreType.DMA(())])
    return pl.pallas_call(
        functools.partial(_sort_rows_body, rows_per_tile=rows_per_tile),
        grid_spec=grid_spec,
        out_shape=(jax.ShapeDtypeStruct((total_rows, XS_COLS), bf16),
                   jax.ShapeDtypeStruct((t, 8), i32)),
        compiler_params=pltpu.CompilerParams(
            dimension_semantics=("arbitrary",), vmem_limit_bytes=VMEM_LIMIT),
        name="sort_rows",
    )(ends, padded, totals, run_tiles, m, slot, route_w)


def _experts_body(texp_ref, first_ref, next_ref, slot_ref, nused_ref,
                  xs_hbm, wg_hbm, wu_hbm, wd_hbm, ys_hbm,
                  xbuf_ref, sg_ref, su_ref, sd_ref, wg_ref, wu_ref, wd_ref,
                  xsem_ref, osem_ref, sem_ref):
    i = pl.program_id(0)
    n_used = nused_ref[0]
    r = xbuf_ref.shape[1]

    def tile_rows(ref, tile):
        return ref.at[pl.ds(pl.multiple_of(tile * r, r), r), :]

    def fetch_tile(tile):
        s = tile % XS_RING
        return pltpu.make_async_copy(tile_rows(xs_hbm, tile), xbuf_ref.at[s], xsem_ref.at[s])

    def store_tile(tile):
        s = tile % XS_RING
        return pltpu.make_async_copy(xbuf_ref.at[s], tile_rows(ys_hbm, tile), osem_ref.at[s])

    @pl.when(i == 0)
    def _():
        for ahead in range(XS_AHEAD):
            @pl.when(ahead < n_used)
            def _():
                fetch_tile(ahead).start()

    @pl.when(i + XS_AHEAD < n_used)
    def _():
        @pl.when(i + XS_AHEAD >= XS_RING)
        def _():
            store_tile(i + XS_AHEAD - XS_RING).wait()

        fetch_tile(i + XS_AHEAD).start()

    def fetches(expert, s):
        return (pltpu.make_async_copy(wg_hbm.at[expert], sg_ref.at[s], sem_ref.at[s]),
                pltpu.make_async_copy(wu_hbm.at[expert], su_ref.at[s], sem_ref.at[s]),
                pltpu.make_async_copy(wd_hbm.at[expert], sd_ref.at[s], sem_ref.at[s]))

    @pl.when(i < n_used)
    def _():
        s = slot_ref[i]
        fetch_tile(i).wait()
        xs_ref = xbuf_ref.at[i % XS_RING]

        @pl.when(first_ref[i] == 1)
        def _():
            @pl.when(i == 0)
            def _():
                for cp in fetches(texp_ref[0], 0):
                    cp.start()

            for cp in fetches(texp_ref[i], s):
                cp.wait()

            @pl.when(next_ref[i] >= 0)
            def _():
                for cp in fetches(next_ref[i], 1 - s):
                    cp.start()

            wg_ref[...] = sg_ref[s].astype(bf16)
            wu_ref[...] = su_ref[s].astype(bf16)
            wd_ref[...] = sd_ref[s].astype(bf16)

        w_row = (xs_ref[:, D_MODEL:D_MODEL + 1].astype(f32)
                 + xs_ref[:, D_MODEL + 1:D_MODEL + 2].astype(f32))
        gate = jnp.dot(xs_ref[:, :D_MODEL], wg_ref[...], preferred_element_type=f32)
        up = jnp.dot(xs_ref[:, :D_MODEL], wu_ref[...], preferred_element_type=f32)
        hdn = (gate * jax.nn.sigmoid(gate)) * up
        y = jnp.dot(hdn.astype(bf16), wd_ref[...], preferred_element_type=f32)
        xs_ref[:, :D_MODEL] = (y * w_row).astype(bf16)
        store_tile(i).start()

    @pl.when(i == pl.num_programs(0) - 1)
    def _():
        for back in range(1, XS_RING + 1):
            @pl.when(n_used - back >= 0)
            def _():
                store_tile(n_used - back).wait()


def _experts(tile_exp, first, next_exp, slot, n_used, xs, wg, wu, wd, rows_per_tile):
    r = rows_per_tile
    n_tiles = xs.shape[0] // r
    grid_spec = pltpu.PrefetchScalarGridSpec(
        num_scalar_prefetch=5,
        grid=(n_tiles,),
        in_specs=[pl.BlockSpec(memory_space=pl.ANY),
                  pl.BlockSpec(memory_space=pl.ANY),
                  pl.BlockSpec(memory_space=pl.ANY),
                  pl.BlockSpec(memory_space=pl.ANY)],
        out_specs=pl.BlockSpec(memory_space=pl.ANY),
        scratch_shapes=[pltpu.VMEM((XS_RING, r, XS_COLS), bf16),
                        pltpu.VMEM((2, D_MODEL, D_FF_E), f32),
                        pltpu.VMEM((2, D_MODEL, D_FF_E), f32),
                        pltpu.VMEM((2, D_FF_E, D_MODEL), f32),
                        pltpu.VMEM((D_MODEL, D_FF_E), bf16),
                        pltpu.VMEM((D_MODEL, D_FF_E), bf16),
                        pltpu.VMEM((D_FF_E, D_MODEL), bf16),
                        pltpu.SemaphoreType.DMA((XS_RING,)),
                        pltpu.SemaphoreType.DMA((XS_RING,)),
                        pltpu.SemaphoreType.DMA((2,))])
    return pl.pallas_call(
        _experts_body,
        grid_spec=grid_spec,
        out_shape=jax.ShapeDtypeStruct(xs.shape, xs.dtype),
        input_output_aliases={5: 0},
        compiler_params=pltpu.CompilerParams(
            dimension_semantics=("arbitrary",), vmem_limit_bytes=VMEM_LIMIT),
        name="experts",
    )(tile_exp, first, next_exp, slot, n_used, xs, wg, wu, wd)


def _combine_out_body(runs_cur_ref, runs_nxt_ref, h_ref, slot_ref, p_ref, ys_ref,
                      wpg_ref, bpg_ref, wpp_ref, gple_ref, gfin_ref,
                      o_ref, ybuf_ref, moe_ref, sem_ref):
    tk = h_ref.shape[0]
    lb = ybuf_ref.shape[1]
    step = pl.program_id(0)
    n = pl.num_programs(0) - 1
    cur = jnp.minimum(step, n - 1) % 2

    def piece(s, local_row, global_row, rows):
        return pltpu.make_async_copy(ys_ref.at[pl.ds(global_row, rows), pl.ds(0, D_MODEL)],
                                     ybuf_ref.at[s, pl.ds(local_row, rows), :], sem_ref.at[s])

    @pl.when(step == 0)
    def _():
        ybuf_ref[...] = jnp.zeros(ybuf_ref.shape, bf16)
        _for_each_run_piece(runs_cur_ref, lambda l, g, r: piece(0, l, g, r).start())

    @pl.when(step + 1 < n)
    def _():
        _for_each_run_piece(runs_nxt_ref, lambda l, g, r: piece(1 - cur, l, g, r).start())

    run_rows = _tile_run_rows(runs_cur_ref)

    @pl.when(step < n)
    def _():
        _wait_rows(run_rows, lambda r: piece(cur, 0, 0, r))

    def work(rows, finish, unsort):
        if finish:
            h2 = h_ref[...] + moe_ref[...]
            gate = jax.nn.sigmoid(jnp.dot(h2.astype(bf16), wpg_ref[...], preferred_element_type=f32)
                                  + bpg_ref[...])
            pp = jnp.dot(p_ref[...].astype(bf16), wpp_ref[...], preferred_element_type=f32)
            h3 = h2 + gate * _rms(pp, gple_ref[...])
            o_ref[...] = _rms(h3, gfin_ref[...])
        if unsort:
            col = lax.broadcasted_iota(i32, (tk, rows), 1)
            unperm = jnp.where((col == slot_ref[:, 0:1]) | (col == slot_ref[:, 1:2]), 1.0, 0.0)
            moe_ref[...] = jnp.dot(unperm.astype(bf16), ybuf_ref[cur, :rows, :],
                                   preferred_element_type=f32)

    short = lb - 256
    is_short = run_rows <= short
    first = step == 0
    mid = (step > 0) & (step < n)
    pl.when(first & is_short)(functools.partial(work, short, False, True))
    pl.when(first & jnp.logical_not(is_short))(functools.partial(work, lb, False, True))
    pl.when(mid & is_short)(functools.partial(work, short, True, True))
    pl.when(mid & jnp.logical_not(is_short))(functools.partial(work, lb, True, True))
    pl.when(step == n)(functools.partial(work, lb, True, False))


def _combine_out(run_tiles, h2d, slot_t, p2d, ys, wpg, bpg, wpp, gple, gfin):
    t = h2d.shape[0]
    tk = TL_SORT
    nt = t // tk
    const = lambda shape: pl.BlockSpec(shape, lambda i: (0,) * len(shape))
    runs = lambda imap: pl.BlockSpec((1, 1, 3 * N_EXPERTS), imap, memory_space=pltpu.SMEM)
    uns = lambda s: jnp.minimum(s, nt - 1)
    fin = lambda s: jnp.maximum(s - 1, 0)
    return pl.pallas_call(
        _combine_out_body,
        grid=(nt + 1,),
        out_shape=jax.ShapeDtypeStruct((t, D_MODEL), f32),
        in_specs=[runs(lambda s: (uns(s), 0, 0)),
                  runs(lambda s: (uns(s + 1), 0, 0)),
                  pl.BlockSpec((tk, D_MODEL), lambda s: (fin(s), 0)),
                  pl.BlockSpec((tk, 8), lambda s: (uns(s), 0)),
                  pl.BlockSpec((tk, PLE_DIM), lambda s: (fin(s), 0)),
                  pl.BlockSpec(memory_space=pl.ANY),
                  const((D_MODEL, D_MODEL)), const((1, D_MODEL)),
                  const((PLE_DIM, D_MODEL)), const((1, D_MODEL)), const((1, D_MODEL))],
        out_specs=pl.BlockSpec((tk, D_MODEL), lambda s: (fin(s), 0)),
        scratch_shapes=[pltpu.VMEM((2, LB_SORT, D_MODEL), bf16),
                        pltpu.VMEM((tk, D_MODEL), f32),
                        pltpu.SemaphoreType.DMA((2,))],
        compiler_params=pltpu.CompilerParams(
            dimension_semantics=("arbitrary",), vmem_limit_bytes=VMEM_LIMIT),
        name="combine_out",
    )(run_tiles, run_tiles, h2d, slot_t, p2d, ys, wpg, bpg, wpp, gple, gfin)


def _head_major_to_tile_major(a, axis):
    axis = axis % a.ndim
    shape = a.shape
    a = a.reshape(shape[:axis] + (N_KV, GQA, HEAD_DIM) + shape[axis + 1:])
    return jnp.swapaxes(a, axis, axis + 1).reshape(shape)


def kernel(x, p, rel_bias, g_mix, w_in, ln_v_g, ln_v_b, w_spatial, b_spatial, sink, g_out_grp, w_out,
           g_ffn, w_router_group, b_router_group, w_router_expert, b_router_expert, w_gate_e, w_up_e,
           w_down_e, w_ple_proj, g_ple, w_ple_gate, b_ple_gate, g_final):
    b, s, d = x.shape
    t = b * s
    depth = g_mix.shape[0]
    assert depth == 1 and d == D_MODEL
    c1, c2 = 2 * D_A, 2 * D_A + D_B
    bias = (rel_bias * LOG2E).astype(f32)
    hcur = x.astype(f32)
    for li in range(depth):
        col_scale = np.ones((1, D_IN), np.float32)
        col_scale[:, c1:c2] = LOG2E * HEAD_DIM ** -0.5
        win = (w_in[li] * col_scale).astype(bf16)
        bs = jnp.broadcast_to(b_spatial[li][:, :, None], (A_HEADS, BLK, BLK)).astype(f32)
        gout = g_out_grp[li]
        gout_b = _head_major_to_tile_major(gout[D_A:], 0)[None]
        wo = w_out[li]
        wout = jnp.concatenate([wo[:D_A], _head_major_to_tile_major(wo[D_A:], 0)],
                               axis=0).astype(bf16)
        wr = jnp.concatenate([
            w_router_group[li], jnp.zeros((D_MODEL, 8 - N_GROUPS), f32), w_router_expert[li],
            jnp.zeros((D_MODEL, BLK - ROUTE_ROWS), f32)], axis=1).astype(bf16)
        br = jnp.concatenate([b_router_group[li], jnp.full((8 - N_GROUPS,), NEG, f32),
                              b_router_expert[li]])
        br = jnp.broadcast_to(br[:, None], (ROUTE_ROWS, TQ_ATT))

        ya, q, k, v = _mix_in(hcur.reshape(t, d), g_mix[li][None], win, ln_v_g[li][None],
                              ln_v_b[li][None], w_spatial[li].astype(bf16), bs, gout[None, :D_A])
        h, m_rows, route_e, route_w = _attn_out(
            sink[li].astype(f32) * LOG2E, hcur, ya.reshape(b, s, D_A), q.reshape(b, s, D_B),
            k.reshape(b, s, BLK), v.reshape(b, s, BLK), bias, gout_b, wout,
            g_ffn[li][None], wr, br)

        r = R_EXP
        n_sort = t // TL_SORT
        max_rows = 2 * t + n_sort * N_EXPERTS * (RUN_ALIGN - 1)
        n_tiles = -(-max_rows // r) + N_EXPERTS
        assert n_tiles * r < 2 ** 16, "route_pos prefix sums hold row offsets in two bf16 bytes"
        slot, runs, offs = _route_pos(route_e, r)
        starts = offs[0, :, 0]
        padded = offs[1, :, 0]
        ends = starts + padded
        n_used = ends[-1] // r
        tile_idx = jnp.minimum(jnp.arange(n_tiles, dtype=i32), n_used - 1)
        tile_exp = jnp.minimum(
            jnp.sum((ends[None, :] <= (tile_idx * r)[:, None]).astype(i32), axis=1), N_EXPERTS - 1)
        tile_exp = tile_exp.astype(i32)
        first = jnp.concatenate([jnp.ones((1,), i32),
                                 (tile_exp[1:] != tile_exp[:-1]).astype(i32)])
        next_tile = ends[tile_exp] // r
        next_exp = jnp.where(next_tile < n_used, tile_exp[jnp.minimum(next_tile, n_tiles - 1)], -1)
        fetch_slot = (jnp.cumsum(first) - 1) % 2
        run_tiles = runs[:, :, :, 0].reshape(n_sort, 1, 3 * N_EXPERTS)
        xs, slot_t = _sort_rows(ends, padded, offs[2, :, 0], run_tiles, m_rows, slot, route_w,
                                n_tiles * r, r)
        ys = _experts(tile_exp, first, next_exp.astype(i32), fetch_slot.astype(i32),
                      n_used.reshape(1), xs, w_gate_e[li], w_up_e[li], w_down_e[li], r)

        out = _combine_out(run_tiles, h.reshape(t, d), slot_t,
                           p[li].reshape(t, PLE_DIM), ys, w_ple_gate[li].astype(bf16),
                           b_ple_gate[li][None], w_ple_proj[li].astype(bf16), g_ple[li][None],
                           g_final[None])
        hcur = out.reshape(b, s, d)
    return hcur
```

```python
import functools
import math

import jax
import jax.numpy as jnp
import numpy as np
from jax import lax
from jax.experimental import pallas as pl
from jax.experimental.pallas import tpu as pltpu

D_MODEL = 1024
D_A = 512
D_B = 512
BLK = 128
A_HEADS = 4
HEAD_DIM = 64
N_HEADS = 8
N_KV = 2
GQA = 4
WINDOW = 128
NUM_BUCKETS = 32
MAX_DIST = 128
D_IN = 2 * D_A + D_B + 2 * N_KV * HEAD_DIM
N_GROUPS = 4
E_PER_GROUP = 8
N_EXPERTS = 32
D_FF_E = 256
PLE_DIM = 256
EPS = 1e-6
NEG = -1e30
LOG2E = math.log2(math.e)

TM_MIX = 512
TQ_ATT = 512
TL_SORT = 512
RUN_ALIGN = 16
SMALL_BITS = 2
BIG_PIECE = RUN_ALIGN << SMALL_BITS
LB_SORT = 2 * TL_SORT + N_EXPERTS * RUN_ALIGN
R_EXP = 512
XS_COLS = D_MODEL + BLK
XS_RING = 4
XS_AHEAD = 2
ROUTE_ROWS = 8 + N_EXPERTS
VMEM_LIMIT = 48 * 1024 * 1024

f32 = jnp.float32
bf16 = jnp.bfloat16
i32 = jnp.int32


def _rms(x, g):
    return x * lax.rsqrt(jnp.mean(x * x, axis=-1, keepdims=True) + EPS) * g


def _gelu_tanh(x):
    c = math.sqrt(2.0 / math.pi)
    return x * (0.5 * (1.0 + jnp.tanh(c * (x + 0.044715 * (x * x * x)))))


def _bucket_row():
    n = NUM_BUCKETS // 2
    max_exact = n // 2
    rel = np.broadcast_to(np.arange(4 * BLK)[None, :] - BLK, (8, 4 * BLK))
    ret = np.where(rel > 0, n, 0)
    a = np.abs(rel)
    large = max_exact + (np.log(np.maximum(a, 1).astype(np.float64) / max_exact)
                         / math.log(MAX_DIST / max_exact) * (n - max_exact)).astype(np.int32)
    large = np.minimum(large, n - 1)
    bucket = ret + np.where(a < max_exact, a, large)
    return np.where(a <= WINDOW, bucket, -1).astype(np.int32)


def _bias_body(rb_ref, bucket_ref, o_ref):
    bucket = bucket_ref[...]
    row = lax.broadcasted_iota(i32, (BLK, BLK), 0)
    lane = lax.broadcasted_iota(i32, (BLK, BLK), 1)
    for h in range(N_HEADS):
        base = jnp.full(bucket.shape, NEG, f32)
        for b in range(NUM_BUCKETS):
            base = jnp.where(bucket == b, rb_ref[b, h], base)
        prev = jnp.full((BLK, BLK), NEG, f32)
        for c in range(3):
            tile = jnp.broadcast_to(base[0:1, c * BLK:(c + 1) * BLK], (BLK, BLK))
            rolled = pltpu.roll(tile, 0, 1, stride=1, stride_axis=0)
            o_ref[h, :, c * BLK:(c + 1) * BLK] = jnp.where(lane >= row, rolled, prev)
            prev = rolled


def _mix_in_body(x_ref, gmix_ref, win_ref, lng_ref, lnb_ref, ws_ref, bs_ref, gout_ref,
                 ya_ref, q_ref, k_ref, v_ref):
    tm = x_ref.shape[0]
    nc = tm // BLK
    a = _rms(x_ref[...], gmix_ref[...])
    z = jnp.dot(a.astype(bf16), win_ref[...], preferred_element_type=f32)
    uv = _gelu_tanh(z[:, :2 * D_A])
    u = uv[:, :D_A]
    v = uv[:, D_A:]
    mu = jnp.mean(v, axis=-1, keepdims=True)
    vc = v - mu
    var = jnp.mean(vc * vc, axis=-1, keepdims=True)
    vn = (vc * lax.rsqrt(var + EPS) * lng_ref[...] + lnb_ref[...]).astype(bf16)
    cols = []
    for h in range(A_HEADS):
        rhs = jnp.concatenate(
            [vn[c * BLK:(c + 1) * BLK, h * BLK:(h + 1) * BLK] for c in range(nc)], axis=1)
        r = jnp.dot(ws_ref[h], rhs, preferred_element_type=f32)
        cols.append(jnp.concatenate(
            [r[:, c * BLK:(c + 1) * BLK] + bs_ref[h] for c in range(nc)], axis=0))
    sv = jnp.concatenate(cols, axis=1)
    ya_ref[...] = _rms(u * sv, gout_ref[...]).astype(bf16)
    qt = [z[:, 2 * D_A + j * BLK:2 * D_A + (j + 1) * BLK] for j in range(D_B // BLK)]
    low_half = lax.broadcasted_iota(i32, (tm, BLK), 1) < HEAD_DIM
    for g in range(GQA):
        a0, a1 = qt[g // 2], qt[GQA // 2 + g // 2]
        tile = (jnp.where(low_half, a0, pltpu.roll(a1, HEAD_DIM, 1)) if g % 2 == 0
                else jnp.where(low_half, pltpu.roll(a0, HEAD_DIM, 1), a1))
        q_ref[:, g * BLK:(g + 1) * BLK] = tile.astype(bf16)
    k_ref[...] = z[:, 2 * D_A + D_B:2 * D_A + D_B + BLK].astype(bf16)
    v_ref[...] = z[:, 2 * D_A + D_B + BLK:].astype(bf16)


def _mix_in(x2, gmix, win, lng, lnb, ws, bs, gout_a):
    t = x2.shape[0]
    tm = TM_MIX
    const = lambda shape: pl.BlockSpec(shape, lambda i: (0,) * len(shape))
    return pl.pallas_call(
        _mix_in_body,
        grid=(t // tm,),
        out_shape=(jax.ShapeDtypeStruct((t, D_A), bf16),
                   jax.ShapeDtypeStruct((t, D_B), bf16),
                   jax.ShapeDtypeStruct((t, BLK), bf16),
                   jax.ShapeDtypeStruct((t, BLK), bf16)),
        in_specs=[pl.BlockSpec((tm, D_MODEL), lambda i: (i, 0)),
                  const((1, D_MODEL)), const((D_MODEL, D_IN)),
                  const((1, D_A)), const((1, D_A)),
                  const((A_HEADS, BLK, BLK)), const((A_HEADS, BLK, BLK)),
                  const((1, D_A))],
        out_specs=(pl.BlockSpec((tm, D_A), lambda i: (i, 0)),
                   pl.BlockSpec((tm, D_B), lambda i: (i, 0)),
                   pl.BlockSpec((tm, BLK), lambda i: (i, 0)),
                   pl.BlockSpec((tm, BLK), lambda i: (i, 0))),
        compiler_params=pltpu.CompilerParams(
            dimension_semantics=("parallel",), vmem_limit_bytes=VMEM_LIMIT),
        name="mix_in",
    )(x2, gmix, win, lng, lnb, ws, bs, gout_a)


def _attn_epilogue(sink_ref, x_ref, ya_ref, q_ref, kp_ref, km_ref, kn_ref, vp_ref, vm_ref, vn_ref,
                   rb_ref, bucket_ref, goutb_ref, wout_ref, gffn_ref, wr_ref, br_ref,
                   h_ref, m_ref, re_ref, rw_ref,
                   kf_ref, vf_ref, e_ref, yb_ref, bias_ref, *, seq_len):
    tq = x_ref.shape[1]
    ybn = _rms(yb_ref[1], goutb_ref[...]).astype(bf16)
    y = jnp.concatenate([ya_ref[0], ybn], axis=1)
    h = x_ref[0] + jnp.dot(y, wout_ref[...], preferred_element_type=f32)
    h_ref[0] = h
    m = _rms(h, gffn_ref[...])
    m_ref[...] = m.astype(bf16)

    logit_t = jnp.dot(m.astype(bf16), wr_ref[...], preferred_element_type=f32)
    logit = jnp.transpose(logit_t)[:ROUTE_ROWS] + br_ref[...]
    sub = lax.broadcasted_iota(i32, (8, tq), 0)
    lg = logit[0:8]
    mg = jnp.max(lg, axis=0, keepdims=True)
    pg_top = 1.0 / jnp.sum(jnp.exp(lg - mg), axis=0, keepdims=True)
    g_idx = jnp.min(jnp.where(lg == mg, sub, 8), axis=0, keepdims=True)
    sel = logit[8:16]
    for g in range(1, N_GROUPS):
        sel = jnp.where(g_idx == g, logit[8 + 8 * g:16 + 8 * g], sel)
    m1 = jnp.max(sel, axis=0, keepdims=True)
    i1 = jnp.min(jnp.where(sel == m1, sub, 8), axis=0, keepdims=True)
    sel2 = jnp.where(sub == i1, -jnp.inf, sel)
    m2 = jnp.max(sel2, axis=0, keepdims=True)
    i2 = jnp.min(jnp.where(sel2 == m2, sub, 8), axis=0, keepdims=True)
    r = jnp.exp(m2 - m1)
    w1 = pg_top / (1.0 + r)
    w2 = pg_top * r / (1.0 + r)
    e1 = g_idx * E_PER_GROUP + i1
    e2 = g_idx * E_PER_GROUP + i2
    re_ref[...] = jnp.where(sub == 0, e1, jnp.where(sub == 1, e2, 0))
    rw_ref[...] = jnp.where(sub == 0, w1, jnp.where(sub == 1, w2, 0.0))


def _attn_attention(sink_ref, x_ref, ya_ref, q_ref, kp_ref, km_ref, kn_ref, vp_ref, vm_ref, vn_ref,
                    rb_ref, bucket_ref, goutb_ref, wout_ref, gffn_ref, wr_ref, br_ref,
                    h_ref, m_ref, re_ref, rw_ref,
                    kf_ref, vf_ref, e_ref, yb_ref, bias_ref, *, seq_len):
    tq = x_ref.shape[1]
    nb = tq // BLK
    ti = pl.program_id(0) % (seq_len // tq)
    kf_ref[0:BLK] = kp_ref[0]
    kf_ref[BLK:BLK + tq] = km_ref[0]
    kf_ref[BLK + tq:] = kn_ref[0]
    vlane = lax.broadcasted_iota(i32, (BLK, BLK), 1)
    for src, r0, r1 in ((vp_ref, 0, BLK), (vn_ref, BLK + tq, tq + 2 * BLK)) + tuple(
            (vm_ref.at[:, pl.ds(c * BLK, BLK), :], BLK + c * BLK, BLK + (c + 1) * BLK)
            for c in range(nb)):
        vals = src[0].astype(f32)
        vf_ref[0, r0:r1] = jnp.where(vlane < HEAD_DIM, vals,
                                     jnp.where(vlane == HEAD_DIM, 1.0, 0.0)).astype(bf16)
        vf_ref[1, r0:r1] = jnp.where(vlane >= HEAD_DIM, vals,
                                     jnp.where(vlane == 0, 1.0, 0.0)).astype(bf16)

    lane = lax.broadcasted_iota(i32, (BLK, BLK), 1)
    low_half = lane < HEAD_DIM

    for n in range(nb):
        r0 = n * BLK
        qb = q_ref[0, pl.ds(r0, BLK), :]
        kb = kf_ref[pl.ds(r0, 3 * BLK), :]
        zero = jnp.zeros((BLK, BLK), bf16)
        lhs = []
        for kh in range(N_KV):
            for g in range(GQA):
                tile = qb[:, g * BLK:(g + 1) * BLK]
                lhs.append(jnp.where(low_half, tile, zero) if kh == 0
                           else jnp.where(low_half, zero, tile))
        lhs = jnp.concatenate(lhs, axis=0)
        s_all = lax.dot_general(lhs, kb, (((1,), (1,)), ((), ())),
                                preferred_element_type=f32)
        blk = ti * nb + n
        pen = [jnp.where(blk == 0, NEG, 0.0) if n == 0 else None, None,
               jnp.where((blk + 1) * BLK >= seq_len, NEG, 0.0) if n == nb - 1 else None]
        sink_e = []
        for h in range(N_HEADS):
            bias_h = bias_ref[h]
            bias_h = jnp.concatenate(
                [bias_h[:, j * BLK:(j + 1) * BLK] if pen[j] is None
                 else bias_h[:, j * BLK:(j + 1) * BLK] + pen[j] for j in range(3)], axis=1)
            s = s_all[h * BLK:(h + 1) * BLK] + bias_h
            sk = sink_ref[h]
            mrow = jnp.maximum(jnp.max(s, axis=-1, keepdims=True), sk)
            e_ref[n, h * BLK:(h + 1) * BLK, :] = jnp.exp2(s - mrow).astype(bf16)
            sink_e.append(jnp.exp2(sk - mrow))
        half = GQA * BLK
        pv = [jnp.dot(e_ref[n, kh * half:(kh + 1) * half, :], vf_ref[kh, pl.ds(r0, 3 * BLK), :],
                      preferred_element_type=f32) for kh in range(N_KV)]

        def head_out(h):
            kh, g = divmod(h, GQA)
            rows = pv[kh][g * BLK:(g + 1) * BLK]
            ones_col = HEAD_DIM if kh == 0 else 0
            return rows * (1.0 / (rows[:, ones_col:ones_col + 1] + sink_e[h]))

        for g in range(GQA):
            yb_ref[0, pl.ds(r0, BLK), g * BLK:(g + 1) * BLK] = jnp.where(
                low_half, head_out(g), head_out(GQA + g))

    yb_ref[1] = yb_ref[0]


def _attn_out_body(*refs, seq_len):
    step = pl.program_id(0)
    n_tok_tiles = pl.num_programs(0) - 1

    @pl.when(step == 0)
    def _():
        _bias_body(refs[10], refs[11], refs[-1])
        _attn_attention(*refs, seq_len=seq_len)

    @pl.when((step > 0) & (step < n_tok_tiles))
    def _():
        _attn_epilogue(*refs, seq_len=seq_len)
        _attn_attention(*refs, seq_len=seq_len)

    @pl.when(step == n_tok_tiles)
    def _():
        _attn_epilogue(*refs, seq_len=seq_len)


def _attn_out(sink, x, ya, q, k, v, rel_bias, goutb, wout, gffn, wr, br):
    b, s, _ = x.shape
    tq = TQ_ATT
    nb = tq // BLK
    nblk = s // BLK
    t = b * s
    nq = s // tq
    n_tok_tiles = b * nq
    att = lambda s: jnp.minimum(s, n_tok_tiles - 1)
    epi = lambda s: jnp.maximum(s - 1, 0)
    const = lambda shape: pl.BlockSpec(shape, lambda s, *_: (0,) * len(shape))
    tok_att = lambda w: pl.BlockSpec((1, tq, w), lambda s, *_: (att(s) // nq, att(s) % nq, 0))
    tok_epi = lambda w: pl.BlockSpec((1, tq, w), lambda s, *_: (epi(s) // nq, epi(s) % nq, 0))
    prev = pl.BlockSpec((1, BLK, BLK), lambda s, *_: (
        att(s) // nq, jnp.maximum((att(s) % nq) * nb - 1, 0), 0))
    nxt = pl.BlockSpec((1, BLK, BLK), lambda s, *_: (
        att(s) // nq, jnp.minimum((att(s) % nq) * nb + nb, nblk - 1), 0))
    rows = pl.BlockSpec((tq, D_MODEL), lambda s, *_: (epi(s), 0))
    lanes = lambda rows: pl.BlockSpec((rows, tq), lambda s, *_: (0, epi(s)))
    grid_spec = pltpu.PrefetchScalarGridSpec(
        num_scalar_prefetch=1,
        grid=(n_tok_tiles + 1,),
        in_specs=[tok_epi(D_MODEL), tok_epi(D_A), tok_att(D_B),
                  prev, tok_att(BLK), nxt, prev, tok_att(BLK), nxt,
                  pl.BlockSpec(memory_space=pltpu.SMEM), const((8, 4 * BLK)),
                  const((1, D_B)), const((D_MODEL, D_MODEL)),
                  const((1, D_MODEL)), const((D_MODEL, BLK)), const((ROUTE_ROWS, tq))],
        out_specs=(tok_epi(D_MODEL), rows, lanes(8), lanes(8)),
        scratch_shapes=[pltpu.VMEM((tq + 2 * BLK, BLK), bf16),
                        pltpu.VMEM((N_KV, tq + 2 * BLK, BLK), bf16),
                        pltpu.VMEM((nb, N_HEADS * BLK, 3 * BLK), bf16),
                        pltpu.VMEM((2, tq, D_B), f32),
                        pltpu.VMEM((N_HEADS, BLK, 3 * BLK), f32)])
    return pl.pallas_call(
        functools.partial(_attn_out_body, seq_len=s),
        grid_spec=grid_spec,
        out_shape=(jax.ShapeDtypeStruct((b, s, D_MODEL), f32),
                   jax.ShapeDtypeStruct((t, D_MODEL), bf16),
                   jax.ShapeDtypeStruct((8, t), i32),
                   jax.ShapeDtypeStruct((8, t), f32)),
        compiler_params=pltpu.CompilerParams(
            dimension_semantics=("arbitrary",), vmem_limit_bytes=VMEM_LIMIT),
        name="attn_out",
    )(sink, x, ya, q, k, k, k, v, v, v, rel_bias, jnp.asarray(_bucket_row()), goutb, wout, gffn, wr, br)


def _exclusive_prefix(vals):
    a = lax.broadcasted_iota(i32, (N_EXPERTS, N_EXPERTS), 0)
    c = lax.broadcasted_iota(i32, (N_EXPERTS, N_EXPERTS), 1)
    low = jnp.where(c < a, 1.0, 0.0).astype(bf16)
    hi = (vals >> 8).astype(f32).astype(bf16)
    lo = (vals & 255).astype(f32).astype(bf16)
    return (jnp.dot(low, hi, preferred_element_type=f32) * 256.0
            + jnp.dot(low, lo, preferred_element_type=f32)).astype(i32)


def _round_up_pow2(vals, mult):
    log_m = mult.bit_length() - 1
    return ((vals + (mult - 1)) >> log_m) << log_m


PLAN_TILE_EXP, PLAN_FIRST, PLAN_NEXT_EXP, PLAN_SLOT, PLAN_N_USED, PLAN_ENDS, PLAN_PADDED, PLAN_TOTAL = (
    range(8))


def _lane_row(vals, lane_off=0):
    sub = lax.broadcasted_iota(i32, vals.shape, 0)
    lane = lax.broadcasted_iota(i32, vals.shape, 1)
    return jnp.sum(jnp.where(lane == sub + lane_off, vals, 0), axis=0, keepdims=True)


def _route_pos_body(re_ref, slot_ref, runs_ref, plan_ref, tri_ref, *, rows_per_tile, tl):
    n_tok_tiles = re_ref.shape[1] // tl
    a = lax.broadcasted_iota(i32, (tl, tl), 0)
    c = lax.broadcasted_iota(i32, (tl, tl), 1)
    tri_ref[...] = jnp.where(a < c, 1.0, 0.0).astype(bf16)
    eid = lax.broadcasted_iota(i32, (N_EXPERTS, tl), 0)

    def tile_hits(j):
        cols = pl.ds(pl.multiple_of(j * tl, tl), tl)
        hit1 = eid == re_ref[0:1, cols]
        hit2 = eid == re_ref[1:2, cols]
        onehot = jnp.where(hit1 | hit2, 1.0, 0.0)
        cnt = jnp.sum(onehot, axis=1, keepdims=True).astype(i32)
        run_len = jnp.broadcast_to(_round_up_pow2(cnt, RUN_ALIGN), (N_EXPERTS, BLK))
        return cols, hit1, hit2, onehot, run_len

    total = lax.fori_loop(0, n_tok_tiles, lambda j, acc: acc + tile_hits(j)[4],
                          jnp.zeros((N_EXPERTS, BLK), i32))
    padded = _round_up_pow2(total, rows_per_tile)
    starts = _exclusive_prefix(padded)
    ends = starts + padded

    log_r = rows_per_tile.bit_length() - 1
    sub = lax.broadcasted_iota(i32, (N_EXPERTS, BLK), 0)
    lane = lax.broadcasted_iota(i32, (1, BLK), 1)
    n_used = ends[N_EXPERTS - 1:N_EXPERTS, :] >> log_r
    tile_idx = jnp.minimum(lane, n_used - 1)
    tile_exp = jnp.minimum(
        jnp.sum((ends <= (tile_idx << log_r)).astype(i32), axis=0, keepdims=True), N_EXPERTS - 1)
    nonempty = padded > 0
    regions_before = jnp.sum(((sub < tile_exp) & nonempty).astype(i32), axis=0, keepdims=True)
    nxt = jnp.min(jnp.where((sub > tile_exp) & nonempty, sub, N_EXPERTS), axis=0, keepdims=True)
    prev_exp = pltpu.roll(jnp.broadcast_to(tile_exp, (8, BLK)), 1, 1)[0:1]
    plan_ref[PLAN_TILE_EXP:PLAN_TILE_EXP + 1] = tile_exp
    plan_ref[PLAN_FIRST:PLAN_FIRST + 1] = ((lane == 0) | (tile_exp != prev_exp)).astype(i32)
    plan_ref[PLAN_NEXT_EXP:PLAN_NEXT_EXP + 1] = jnp.where(nxt < N_EXPERTS, nxt, -1)
    plan_ref[PLAN_SLOT:PLAN_SLOT + 1] = regions_before & 1
    plan_ref[PLAN_N_USED:PLAN_N_USED + 1] = n_used
    plan_ref[PLAN_ENDS:PLAN_ENDS + 1] = _lane_row(ends)
    plan_ref[PLAN_PADDED:PLAN_PADDED + 1] = _lane_row(padded)
    plan_ref[PLAN_TOTAL:PLAN_TOTAL + 1] = _lane_row(total)

    def place(j, global_off):
        cols, hit1, hit2, onehot, run_len = tile_hits(j)
        local_off = _exclusive_prefix(run_len)
        before = jnp.dot(onehot.astype(bf16), tri_ref[...], preferred_element_type=f32)
        slot = before + local_off[:, 0:1].astype(f32)
        s1 = jnp.sum(jnp.where(hit1, slot, 0.0), axis=0, keepdims=True)
        s2 = jnp.sum(jnp.where(hit2, slot, 0.0), axis=0, keepdims=True)
        sub = lax.broadcasted_iota(i32, (8, tl), 0)
        slot_ref[:, cols] = jnp.where(sub == 0, s1.astype(i32),
                                      jnp.where(sub == 1, s2.astype(i32), 0))
        runs_ref[j] = (_lane_row(local_off) + _lane_row(run_len, N_EXPERTS)
                       + _lane_row(global_off, 2 * N_EXPERTS))
        return global_off + run_len

    lax.fori_loop(0, n_tok_tiles, place, starts)


def _route_pos(route_e, rows_per_tile):
    t = route_e.shape[1]
    tl = TL_SORT
    return pl.pallas_call(
        functools.partial(_route_pos_body, rows_per_tile=rows_per_tile, tl=tl),
        out_shape=(jax.ShapeDtypeStruct((8, t), i32),
                   jax.ShapeDtypeStruct((t // tl, 1, BLK), i32),
                   jax.ShapeDtypeStruct((8, BLK), i32)),
        scratch_shapes=[pltpu.VMEM((tl, tl), bf16)],
        compiler_params=pltpu.CompilerParams(vmem_limit_bytes=VMEM_LIMIT),
        name="route_pos",
    )(route_e)


def _for_each_run_piece(runs_ref, fn):
    def per_expert(e, carry):
        local_off = runs_ref[0, 0, e]
        length = runs_ref[0, 0, N_EXPERTS + e]
        global_off = runs_ref[0, 0, 2 * N_EXPERTS + e]
        units = length >> (RUN_ALIGN.bit_length() - 1)

        def big_piece(k, inner):
            done = k * BIG_PIECE
            fn(pl.multiple_of(local_off + done, RUN_ALIGN),
               pl.multiple_of(global_off + done, RUN_ALIGN), BIG_PIECE)
            return inner

        lax.fori_loop(0, units >> SMALL_BITS, big_piece, 0)
        for b in range(SMALL_BITS):
            @pl.when(((units >> b) & 1) == 1)
            def _():
                done = ((units >> (b + 1)) << (b + 1)) * RUN_ALIGN
                fn(pl.multiple_of(local_off + done, RUN_ALIGN),
                   pl.multiple_of(global_off + done, RUN_ALIGN), RUN_ALIGN << b)
        return carry

    lax.fori_loop(0, N_EXPERTS, per_expert, 0)


def _tile_run_rows(runs_ref):
    return lax.fori_loop(0, N_EXPERTS, lambda e, acc: acc + runs_ref[0, 0, N_EXPERTS + e], 0)


def _wait_rows(total_rows, make_copy):
    units = total_rows >> (RUN_ALIGN.bit_length() - 1)
    for b in range((LB_SORT // RUN_ALIGN).bit_length()):
        @pl.when(((units >> b) & 1) == 1)
        def _():
            make_copy(RUN_ALIGN << b).wait()


def _sort_rows_body(plan_ref, runs_ref, m_ref, slot_ref, rw_ref,
                    xs_ref, slot_t_ref,
                    local_ref, zero_ref, pending_ref, sem_ref, zsem_ref, *, rows_per_tile):
    tl = m_ref.shape[0]
    lb = local_ref.shape[1]
    step = pl.program_id(0)
    cur = step % 2
    n_tiles = xs_ref.shape[0] // rows_per_tile
    n_used = plan_ref[PLAN_N_USED, 0]

    def zero_rows(start, rows, wait):
        cp = pltpu.make_async_copy(zero_ref.at[pl.ds(0, rows), :],
                                   xs_ref.at[pl.ds(pl.multiple_of(start, RUN_ALIGN), rows), :],
                                   zsem_ref)
        cp.wait() if wait else cp.start()

    def zero_fill(wait):
        for e in range(N_EXPERTS):
            pad_rows = plan_ref[PLAN_PADDED, e] - plan_ref[PLAN_TOTAL, e]
            pad_units = pad_rows >> (RUN_ALIGN.bit_length() - 1)
            for bit in range((rows_per_tile // RUN_ALIGN).bit_length() - 1):
                @pl.when(((pad_units >> bit) & 1) == 1)
                def _():
                    done = ((pad_units >> (bit + 1)) << (bit + 1)) * RUN_ALIGN
                    zero_rows(plan_ref[PLAN_ENDS, e] - pad_rows + done,
                              RUN_ALIGN << bit, wait)

        def spare(j, carry):
            zero_rows((n_used + j) * rows_per_tile, rows_per_tile, wait)
            return carry

        lax.fori_loop(0, n_tiles - n_used, spare, 0)

    @pl.when(step == 0)
    def _():
        zero_ref[...] = jnp.zeros(zero_ref.shape, bf16)
        zero_fill(wait=False)

    s1 = slot_ref[0:1, :]
    s2 = slot_ref[1:2, :]
    w1 = rw_ref[0:1, :]
    w2 = rw_ref[1:2, :]
    run_rows = _tile_run_rows(runs_ref)

    def sort_rows_upto(n):
        srow = lax.broadcasted_iota(i32, (n, tl), 0)
        p1 = srow == s1
        p2 = srow == s2
        perm = jnp.where(p1 | p2, 1.0, 0.0).astype(bf16)
        rows = jnp.dot(perm, m_ref[...], preferred_element_type=f32)
        w = jnp.sum(jnp.where(p1, w1, 0.0) + jnp.where(p2, w2, 0.0), axis=1, keepdims=True)
        w_hi = w.astype(bf16).astype(f32)
        w_lo = w - w_hi
        lane = lax.broadcasted_iota(i32, (n, BLK), 1)
        local_ref[cur, :n, :D_MODEL] = rows.astype(bf16)
        local_ref[cur, :n, D_MODEL:] = jnp.where(
            lane == 0, w_hi, jnp.where(lane == 1, w_lo, 0.0)).astype(bf16)
        sub_t = lax.broadcasted_iota(i32, (BLK, tl), 0)
        slots = jnp.where(sub_t == 0, s1.astype(f32), jnp.where(sub_t == 1, s2.astype(f32), 0.0))
        slot_t_ref[...] = jnp.transpose(slots)[:, :8].astype(i32)

    def piece(s, local_row, global_row, rows):
        return pltpu.make_async_copy(local_ref.at[s, pl.ds(local_row, rows), :],
                                     xs_ref.at[pl.ds(global_row, rows), :], sem_ref.at[s])

    short = lb - 256
    pl.when(run_rows <= short)(functools.partial(sort_rows_upto, short))
    pl.when(run_rows > short)(functools.partial(sort_rows_upto, lb))

    _for_each_run_piece(runs_ref, lambda l, g, n: piece(cur, l, g, n).start())

    @pl.when(step > 0)
    def _():
        _wait_rows(pending_ref[0], lambda n: piece(1 - cur, 0, 0, n))

    pending_ref[0] = run_rows

    @pl.when(step == pl.num_programs(0) - 1)
    def _():
        _wait_rows(pending_ref[0], lambda n: piece(cur, 0, 0, n))
        zero_fill(wait=True)


def _sort_rows(plan, run_tiles, m, slot, route_w, total_rows, rows_per_tile):
    t = m.shape[0]
    tl = TL_SORT
    grid_spec = pltpu.PrefetchScalarGridSpec(
        num_scalar_prefetch=1,
        grid=(t // tl,),
        in_specs=[pl.BlockSpec((1, 1, BLK), lambda i, *_: (i, 0, 0),
                               memory_space=pltpu.SMEM),
                  pl.BlockSpec((tl, D_MODEL), lambda i, *_: (i, 0)),
                  pl.BlockSpec((8, tl), lambda i, *_: (0, i)),
                  pl.BlockSpec((8, tl), lambda i, *_: (0, i))],
        out_specs=(pl.BlockSpec(memory_space=pl.ANY),
                   pl.BlockSpec((tl, 8), lambda i, *_: (i, 0))),
        scratch_shapes=[pltpu.VMEM((2, LB_SORT, XS_COLS), bf16),
                        pltpu.VMEM((rows_per_tile, XS_COLS), bf16),
                        pltpu.SMEM((1,), i32),
                        pltpu.SemaphoreType.DMA((2,)),
                        pltpu.SemaphoreType.DMA(())])
    return pl.pallas_call(
        functools.partial(_sort_rows_body, rows_per_tile=rows_per_tile),
        grid_spec=grid_spec,
        out_shape=(jax.ShapeDtypeStruct((total_rows, XS_COLS), bf16),
                   jax.ShapeDtypeStruct((t, 8), i32)),
        compiler_params=pltpu.CompilerParams(
            dimension_semantics=("arbitrary",), vmem_limit_bytes=VMEM_LIMIT),
        name="sort_rows",
    )(plan, run_tiles, m, slot, route_w)


def _experts_body(plan_ref,
                  xs_hbm, wg_hbm, wu_hbm, wd_hbm, ys_hbm,
                  xbuf_ref, sg_ref, su_ref, sd_ref, wg_ref, wu_ref, wd_ref,
                  xsem_ref, osem_ref, sem_ref):
    i = pl.program_id(0)
    n_used = plan_ref[PLAN_N_USED, 0]
    r = xbuf_ref.shape[1]

    def tile_rows(ref, tile):
        return ref.at[pl.ds(pl.multiple_of(tile * r, r), r), :]

    def fetch_tile(tile):
        s = tile % XS_RING
        return pltpu.make_async_copy(tile_rows(xs_hbm, tile), xbuf_ref.at[s], xsem_ref.at[s])

    def store_tile(tile):
        s = tile % XS_RING
        return pltpu.make_async_copy(xbuf_ref.at[s], tile_rows(ys_hbm, tile), osem_ref.at[s])

    @pl.when(i == 0)
    def _():
        for ahead in range(XS_AHEAD):
            @pl.when(ahead < n_used)
            def _():
                fetch_tile(ahead).start()

    @pl.when(i + XS_AHEAD < n_used)
    def _():
        @pl.when(i + XS_AHEAD >= XS_RING)
        def _():
            store_tile(i + XS_AHEAD - XS_RING).wait()

        fetch_tile(i + XS_AHEAD).start()

    def fetches(expert, s):
        return (pltpu.make_async_copy(wg_hbm.at[expert], sg_ref.at[s], sem_ref.at[s]),
                pltpu.make_async_copy(wu_hbm.at[expert], su_ref.at[s], sem_ref.at[s]),
                pltpu.make_async_copy(wd_hbm.at[expert], sd_ref.at[s], sem_ref.at[s]))

    @pl.when(i < n_used)
    def _():
        s = plan_ref[PLAN_SLOT, i]
        fetch_tile(i).wait()
        xs_ref = xbuf_ref.at[i % XS_RING]

        @pl.when(plan_ref[PLAN_FIRST, i] == 1)
        def _():
            @pl.when(i == 0)
            def _():
                for cp in fetches(plan_ref[PLAN_TILE_EXP, 0], 0):
                    cp.start()

            for cp in fetches(plan_ref[PLAN_TILE_EXP, i], s):
                cp.wait()

            @pl.when(plan_ref[PLAN_NEXT_EXP, i] >= 0)
            def _():
                for cp in fetches(plan_ref[PLAN_NEXT_EXP, i], 1 - s):
                    cp.start()

            wg_ref[...] = sg_ref[s].astype(bf16)
            wu_ref[...] = su_ref[s].astype(bf16)
            wd_ref[...] = sd_ref[s].astype(bf16)

        w_row = (xs_ref[:, D_MODEL:D_MODEL + 1].astype(f32)
                 + xs_ref[:, D_MODEL + 1:D_MODEL + 2].astype(f32))
        gate = jnp.dot(xs_ref[:, :D_MODEL], wg_ref[...], preferred_element_type=f32)
        up = jnp.dot(xs_ref[:, :D_MODEL], wu_ref[...], preferred_element_type=f32)
        hdn = (gate * jax.nn.sigmoid(gate)) * up
        y = jnp.dot(hdn.astype(bf16), wd_ref[...], preferred_element_type=f32)
        xs_ref[:, :D_MODEL] = (y * w_row).astype(bf16)
        store_tile(i).start()

    @pl.when(i == pl.num_programs(0) - 1)
    def _():
        for back in range(1, XS_RING + 1):
            @pl.when(n_used - back >= 0)
            def _():
                store_tile(n_used - back).wait()


def _experts(plan, xs, wg, wu, wd, rows_per_tile):
    r = rows_per_tile
    n_tiles = xs.shape[0] // r
    assert n_tiles <= BLK, "the plan holds one row tile per lane"
    grid_spec = pltpu.PrefetchScalarGridSpec(
        num_scalar_prefetch=1,
        grid=(n_tiles,),
        in_specs=[pl.BlockSpec(memory_space=pl.ANY),
                  pl.BlockSpec(memory_space=pl.ANY),
                  pl.BlockSpec(memory_space=pl.ANY),
                  pl.BlockSpec(memory_space=pl.ANY)],
        out_specs=pl.BlockSpec(memory_space=pl.ANY),
        scratch_shapes=[pltpu.VMEM((XS_RING, r, XS_COLS), bf16),
                        pltpu.VMEM((2, D_MODEL, D_FF_E), f32),
                        pltpu.VMEM((2, D_MODEL, D_FF_E), f32),
                        pltpu.VMEM((2, D_FF_E, D_MODEL), f32),
                        pltpu.VMEM((D_MODEL, D_FF_E), bf16),
                        pltpu.VMEM((D_MODEL, D_FF_E), bf16),
                        pltpu.VMEM((D_FF_E, D_MODEL), bf16),
                        pltpu.SemaphoreType.DMA((XS_RING,)),
                        pltpu.SemaphoreType.DMA((XS_RING,)),
                        pltpu.SemaphoreType.DMA((2,))])
    return pl.pallas_call(
        _experts_body,
        grid_spec=grid_spec,
        out_shape=jax.ShapeDtypeStruct(xs.shape, xs.dtype),
        input_output_aliases={1: 0},
        compiler_params=pltpu.CompilerParams(
            dimension_semantics=("arbitrary",), vmem_limit_bytes=VMEM_LIMIT),
        name="experts",
    )(plan, xs, wg, wu, wd)


def _combine_out_body(runs_cur_ref, runs_nxt_ref, h_ref, slot_ref, p_ref, ys_ref,
                      wpg_ref, bpg_ref, wpp_ref, gple_ref, gfin_ref,
                      o_ref, ybuf_ref, moe_ref, sem_ref):
    tk = h_ref.shape[0]
    lb = ybuf_ref.shape[1]
    step = pl.program_id(0)
    n = pl.num_programs(0) - 1
    cur = jnp.minimum(step, n - 1) % 2

    def piece(s, local_row, global_row, rows):
        return pltpu.make_async_copy(ys_ref.at[pl.ds(global_row, rows), pl.ds(0, D_MODEL)],
                                     ybuf_ref.at[s, pl.ds(local_row, rows), :], sem_ref.at[s])

    @pl.when(step == 0)
    def _():
        ybuf_ref[...] = jnp.zeros(ybuf_ref.shape, bf16)
        _for_each_run_piece(runs_cur_ref, lambda l, g, r: piece(0, l, g, r).start())

    @pl.when(step + 1 < n)
    def _():
        _for_each_run_piece(runs_nxt_ref, lambda l, g, r: piece(1 - cur, l, g, r).start())

    run_rows = _tile_run_rows(runs_cur_ref)

    @pl.when(step < n)
    def _():
        _wait_rows(run_rows, lambda r: piece(cur, 0, 0, r))

    def work(rows, finish, unsort):
        if finish:
            h2 = h_ref[...] + moe_ref[...]
            gate = jax.nn.sigmoid(jnp.dot(h2.astype(bf16), wpg_ref[...], preferred_element_type=f32)
                                  + bpg_ref[...])
            pp = jnp.dot(p_ref[...].astype(bf16), wpp_ref[...], preferred_element_type=f32)
            h3 = h2 + gate * _rms(pp, gple_ref[...])
            o_ref[...] = _rms(h3, gfin_ref[...])
        if unsort:
            col = lax.broadcasted_iota(i32, (tk, rows), 1)
            unperm = jnp.where((col == slot_ref[:, 0:1]) | (col == slot_ref[:, 1:2]), 1.0, 0.0)
            moe_ref[...] = jnp.dot(unperm.astype(bf16), ybuf_ref[cur, :rows, :],
                                   preferred_element_type=f32)

    short = lb - 256
    is_short = run_rows <= short
    first = step == 0
    mid = (step > 0) & (step < n)
    pl.when(first & is_short)(functools.partial(work, short, False, True))
    pl.when(first & jnp.logical_not(is_short))(functools.partial(work, lb, False, True))
    pl.when(mid & is_short)(functools.partial(work, short, True, True))
    pl.when(mid & jnp.logical_not(is_short))(functools.partial(work, lb, True, True))
    pl.when(step == n)(functools.partial(work, lb, True, False))


def _combine_out(run_tiles, h2d, slot_t, p2d, ys, wpg, bpg, wpp, gple, gfin):
    t = h2d.shape[0]
    tk = TL_SORT
    nt = t // tk
    const = lambda shape: pl.BlockSpec(shape, lambda i: (0,) * len(shape))
    runs = lambda imap: pl.BlockSpec((1, 1, BLK), imap, memory_space=pltpu.SMEM)
    uns = lambda s: jnp.minimum(s, nt - 1)
    fin = lambda s: jnp.maximum(s - 1, 0)
    return pl.pallas_call(
        _combine_out_body,
        grid=(nt + 1,),
        out_shape=jax.ShapeDtypeStruct((t, D_MODEL), f32),
        in_specs=[runs(lambda s: (uns(s), 0, 0)),
                  runs(lambda s: (uns(s + 1), 0, 0)),
                  pl.BlockSpec((tk, D_MODEL), lambda s: (fin(s), 0)),
                  pl.BlockSpec((tk, 8), lambda s: (uns(s), 0)),
                  pl.BlockSpec((tk, PLE_DIM), lambda s: (fin(s), 0)),
                  pl.BlockSpec(memory_space=pl.ANY),
                  const((D_MODEL, D_MODEL)), const((1, D_MODEL)),
                  const((PLE_DIM, D_MODEL)), const((1, D_MODEL)), const((1, D_MODEL))],
        out_specs=pl.BlockSpec((tk, D_MODEL), lambda s: (fin(s), 0)),
        scratch_shapes=[pltpu.VMEM((2, LB_SORT, D_MODEL), bf16),
                        pltpu.VMEM((tk, D_MODEL), f32),
                        pltpu.SemaphoreType.DMA((2,))],
        compiler_params=pltpu.CompilerParams(
            dimension_semantics=("arbitrary",), vmem_limit_bytes=VMEM_LIMIT),
        name="combine_out",
    )(run_tiles, run_tiles, h2d, slot_t, p2d, ys, wpg, bpg, wpp, gple, gfin)


def _head_major_to_tile_major(a, axis):
    axis = axis % a.ndim
    shape = a.shape
    a = a.reshape(shape[:axis] + (N_KV, GQA, HEAD_DIM) + shape[axis + 1:])
    return jnp.swapaxes(a, axis, axis + 1).reshape(shape)


def kernel(x, p, rel_bias, g_mix, w_in, ln_v_g, ln_v_b, w_spatial, b_spatial, sink, g_out_grp, w_out,
           g_ffn, w_router_group, b_router_group, w_router_expert, b_router_expert, w_gate_e, w_up_e,
           w_down_e, w_ple_proj, g_ple, w_ple_gate, b_ple_gate, g_final):
    b, s, d = x.shape
    t = b * s
    depth = g_mix.shape[0]
    assert depth == 1 and d == D_MODEL
    c1, c2 = 2 * D_A, 2 * D_A + D_B
    bias = (rel_bias * LOG2E).astype(f32)
    hcur = x.astype(f32)
    for li in range(depth):
        col_scale = np.ones((1, D_IN), np.float32)
        col_scale[:, c1:c2] = LOG2E * HEAD_DIM ** -0.5
        win = (w_in[li] * col_scale).astype(bf16)
        bs = jnp.broadcast_to(b_spatial[li][:, :, None], (A_HEADS, BLK, BLK)).astype(f32)
        gout = g_out_grp[li]
        gout_b = _head_major_to_tile_major(gout[D_A:], 0)[None]
        wo = w_out[li]
        wout = jnp.concatenate([wo[:D_A], _head_major_to_tile_major(wo[D_A:], 0)],
                               axis=0).astype(bf16)
        wr = jnp.concatenate([
            w_router_group[li], jnp.zeros((D_MODEL, 8 - N_GROUPS), f32), w_router_expert[li],
            jnp.zeros((D_MODEL, BLK - ROUTE_ROWS), f32)], axis=1).astype(bf16)
        br = jnp.concatenate([b_router_group[li], jnp.full((8 - N_GROUPS,), NEG, f32),
                              b_router_expert[li]])
        br = jnp.broadcast_to(br[:, None], (ROUTE_ROWS, TQ_ATT))

        ya, q, k, v = _mix_in(hcur.reshape(t, d), g_mix[li][None], win, ln_v_g[li][None],
                              ln_v_b[li][None], w_spatial[li].astype(bf16), bs, gout[None, :D_A])
        h, m_rows, route_e, route_w = _attn_out(
            sink[li].astype(f32) * LOG2E, hcur, ya.reshape(b, s, D_A), q.reshape(b, s, D_B),
            k.reshape(b, s, BLK), v.reshape(b, s, BLK), bias, gout_b, wout,
            g_ffn[li][None], wr, br)

        r = R_EXP
        n_sort = t // TL_SORT
        max_rows = 2 * t + n_sort * N_EXPERTS * (RUN_ALIGN - 1)
        n_tiles = -(-max_rows // r) + N_EXPERTS
        assert n_tiles * r < 2 ** 16, "route_pos prefix sums hold row offsets in two bf16 bytes"
        slot, run_tiles, plan = _route_pos(route_e, r)
        xs, slot_t = _sort_rows(plan, run_tiles, m_rows, slot, route_w, n_tiles * r, r)
        ys = _experts(plan, xs, w_gate_e[li], w_up_e[li], w_down_e[li], r)

        out = _combine_out(run_tiles, h.reshape(t, d), slot_t,
                           p[li].reshape(t, PLE_DIM), ys, w_ple_gate[li].astype(bf16),
                           b_ple_gate[li][None], w_ple_proj[li].astype(bf16), g_ple[li][None],
                           g_final[None])
        hcur = out.reshape(b, s, d)
    return hcur
```

```python
import functools
import math

import jax
import jax.numpy as jnp
import numpy as np
from jax import lax
from jax.experimental import pallas as pl
from jax.experimental.pallas import tpu as pltpu

D_MODEL = 1024
D_A = 512
D_B = 512
BLK = 128
A_HEADS = 4
HEAD_DIM = 64
N_HEADS = 8
N_KV = 2
GQA = 4
WINDOW = 128
NUM_BUCKETS = 32
MAX_DIST = 128
D_IN = 2 * D_A + D_B + 2 * N_KV * HEAD_DIM
N_GROUPS = 4
E_PER_GROUP = 8
N_EXPERTS = 32
D_FF_E = 256
PLE_DIM = 256
EPS = 1e-6
NEG = -1e30
LOG2E = math.log2(math.e)

TM_MIX = 512
TQ_ATT = 512
TL_SORT = 512
RUN_ALIGN = 16
SMALL_BITS = 2
BIG_PIECE = RUN_ALIGN << SMALL_BITS
LB_SORT = 2 * TL_SORT + N_EXPERTS * RUN_ALIGN
R_EXP = 512
XS_COLS = D_MODEL + BLK
XS_RING = 4
XS_AHEAD = 2
ROUTE_ROWS = 8 + N_EXPERTS
ROUTE_UNROLL = 8
VMEM_LIMIT = 48 * 1024 * 1024

f32 = jnp.float32
bf16 = jnp.bfloat16
i32 = jnp.int32


def _rms(x, g):
    return x * lax.rsqrt(jnp.mean(x * x, axis=-1, keepdims=True) + EPS) * g


def _gelu_tanh(x):
    c = math.sqrt(2.0 / math.pi)
    return x * (0.5 * (1.0 + jnp.tanh(c * (x + 0.044715 * (x * x * x)))))


def _bucket_row():
    n = NUM_BUCKETS // 2
    max_exact = n // 2
    rel = np.broadcast_to(np.arange(4 * BLK)[None, :] - BLK, (8, 4 * BLK))
    ret = np.where(rel > 0, n, 0)
    a = np.abs(rel)
    large = max_exact + (np.log(np.maximum(a, 1).astype(np.float64) / max_exact)
                         / math.log(MAX_DIST / max_exact) * (n - max_exact)).astype(np.int32)
    large = np.minimum(large, n - 1)
    bucket = ret + np.where(a < max_exact, a, large)
    return np.where(a <= WINDOW, bucket, -1).astype(np.int32)


def _bias_body(rb_ref, bucket_ref, o_ref):
    bucket = bucket_ref[...]
    row = lax.broadcasted_iota(i32, (BLK, BLK), 0)
    lane = lax.broadcasted_iota(i32, (BLK, BLK), 1)
    for h in range(N_HEADS):
        base = jnp.full(bucket.shape, NEG, f32)
        for b in range(NUM_BUCKETS):
            base = jnp.where(bucket == b, rb_ref[b, h], base)
        prev = jnp.full((BLK, BLK), NEG, f32)
        for c in range(3):
            tile = jnp.broadcast_to(base[0:1, c * BLK:(c + 1) * BLK], (BLK, BLK))
            rolled = pltpu.roll(tile, 0, 1, stride=1, stride_axis=0)
            o_ref[h, :, c * BLK:(c + 1) * BLK] = jnp.where(lane >= row, rolled, prev)
            prev = rolled


def _mix_in_body(x_ref, gmix_ref, win_ref, lng_ref, lnb_ref, ws_ref, bs_ref, gout_ref,
                 ya_ref, q_ref, k_ref, v_ref):
    tm = x_ref.shape[0]
    nc = tm // BLK
    a = _rms(x_ref[...], gmix_ref[...])
    z = jnp.dot(a.astype(bf16), win_ref[...], preferred_element_type=f32)
    uv = _gelu_tanh(z[:, :2 * D_A])
    u = uv[:, :D_A]
    v = uv[:, D_A:]
    mu = jnp.mean(v, axis=-1, keepdims=True)
    vc = v - mu
    var = jnp.mean(vc * vc, axis=-1, keepdims=True)
    vn = (vc * lax.rsqrt(var + EPS) * lng_ref[...] + lnb_ref[...]).astype(bf16)
    cols = []
    for h in range(A_HEADS):
        rhs = jnp.concatenate(
            [vn[c * BLK:(c + 1) * BLK, h * BLK:(h + 1) * BLK] for c in range(nc)], axis=1)
        r = jnp.dot(ws_ref[h], rhs, preferred_element_type=f32)
        cols.append(jnp.concatenate(
            [r[:, c * BLK:(c + 1) * BLK] + bs_ref[h] for c in range(nc)], axis=0))
    sv = jnp.concatenate(cols, axis=1)
    ya_ref[...] = _rms(u * sv, gout_ref[...]).astype(bf16)
    qt = [z[:, 2 * D_A + j * BLK:2 * D_A + (j + 1) * BLK] for j in range(D_B // BLK)]
    low_half = lax.broadcasted_iota(i32, (tm, BLK), 1) < HEAD_DIM
    for g in range(GQA):
        a0, a1 = qt[g // 2], qt[GQA // 2 + g // 2]
        tile = (jnp.where(low_half, a0, pltpu.roll(a1, HEAD_DIM, 1)) if g % 2 == 0
                else jnp.where(low_half, pltpu.roll(a0, HEAD_DIM, 1), a1))
        q_ref[:, g * BLK:(g + 1) * BLK] = tile.astype(bf16)
    k_ref[...] = z[:, 2 * D_A + D_B:2 * D_A + D_B + BLK].astype(bf16)
    v_ref[...] = z[:, 2 * D_A + D_B + BLK:].astype(bf16)


def _mix_in(x2, gmix, win, lng, lnb, ws, bs, gout_a):
    t = x2.shape[0]
    tm = TM_MIX
    const = lambda shape: pl.BlockSpec(shape, lambda i: (0,) * len(shape))
    return pl.pallas_call(
        _mix_in_body,
        grid=(t // tm,),
        out_shape=(jax.ShapeDtypeStruct((t, D_A), bf16),
                   jax.ShapeDtypeStruct((t, D_B), bf16),
                   jax.ShapeDtypeStruct((t, BLK), bf16),
                   jax.ShapeDtypeStruct((t, BLK), bf16)),
        in_specs=[pl.BlockSpec((tm, D_MODEL), lambda i: (i, 0)),
                  const((1, D_MODEL)), const((D_MODEL, D_IN)),
                  const((1, D_A)), const((1, D_A)),
                  const((A_HEADS, BLK, BLK)), const((A_HEADS, BLK, BLK)),
                  const((1, D_A))],
        out_specs=(pl.BlockSpec((tm, D_A), lambda i: (i, 0)),
                   pl.BlockSpec((tm, D_B), lambda i: (i, 0)),
                   pl.BlockSpec((tm, BLK), lambda i: (i, 0)),
                   pl.BlockSpec((tm, BLK), lambda i: (i, 0))),
        compiler_params=pltpu.CompilerParams(
            dimension_semantics=("parallel",), vmem_limit_bytes=VMEM_LIMIT),
        name="mix_in",
    )(x2, gmix, win, lng, lnb, ws, bs, gout_a)


def _attn_epilogue(sink_ref, x_ref, ya_ref, q_ref, kp_ref, km_ref, kn_ref, vp_ref, vm_ref, vn_ref,
                   rb_ref, bucket_ref, goutb_ref, wout_ref, gffn_ref, wr_ref, br_ref,
                   h_ref, m_ref, re_ref, rw_ref,
                   kf_ref, vf_ref, e_ref, yb_ref, bias_ref, *, seq_len):
    tq = x_ref.shape[1]
    ybn = _rms(yb_ref[1], goutb_ref[...]).astype(bf16)
    y = jnp.concatenate([ya_ref[0], ybn], axis=1)
    h = x_ref[0] + jnp.dot(y, wout_ref[...], preferred_element_type=f32)
    h_ref[0] = h
    m = _rms(h, gffn_ref[...])
    m_ref[...] = m.astype(bf16)

    logit_t = jnp.dot(m.astype(bf16), wr_ref[...], preferred_element_type=f32)
    logit = jnp.transpose(logit_t)[:ROUTE_ROWS] + br_ref[...]
    sub = lax.broadcasted_iota(i32, (8, tq), 0)
    lg = logit[0:8]
    mg = jnp.max(lg, axis=0, keepdims=True)
    pg_top = 1.0 / jnp.sum(jnp.exp(lg - mg), axis=0, keepdims=True)
    g_idx = jnp.min(jnp.where(lg == mg, sub, 8), axis=0, keepdims=True)
    sel = logit[8:16]
    for g in range(1, N_GROUPS):
        sel = jnp.where(g_idx == g, logit[8 + 8 * g:16 + 8 * g], sel)
    m1 = jnp.max(sel, axis=0, keepdims=True)
    i1 = jnp.min(jnp.where(sel == m1, sub, 8), axis=0, keepdims=True)
    sel2 = jnp.where(sub == i1, -jnp.inf, sel)
    m2 = jnp.max(sel2, axis=0, keepdims=True)
    i2 = jnp.min(jnp.where(sel2 == m2, sub, 8), axis=0, keepdims=True)
    r = jnp.exp(m2 - m1)
    w1 = pg_top / (1.0 + r)
    w2 = pg_top * r / (1.0 + r)
    e1 = g_idx * E_PER_GROUP + i1
    e2 = g_idx * E_PER_GROUP + i2
    re_ref[...] = jnp.where(sub == 0, e1, jnp.where(sub == 1, e2, 0))
    rw_ref[...] = jnp.where(sub == 0, w1, jnp.where(sub == 1, w2, 0.0))


def _attn_attention(sink_ref, x_ref, ya_ref, q_ref, kp_ref, km_ref, kn_ref, vp_ref, vm_ref, vn_ref,
                    rb_ref, bucket_ref, goutb_ref, wout_ref, gffn_ref, wr_ref, br_ref,
                    h_ref, m_ref, re_ref, rw_ref,
                    kf_ref, vf_ref, e_ref, yb_ref, bias_ref, *, seq_len):
    tq = x_ref.shape[1]
    nb = tq // BLK
    ti = pl.program_id(0) % (seq_len // tq)
    kf_ref[0:BLK] = kp_ref[0]
    kf_ref[BLK:BLK + tq] = km_ref[0]
    kf_ref[BLK + tq:] = kn_ref[0]
    vlane = lax.broadcasted_iota(i32, (BLK, BLK), 1)
    for src, r0, r1 in ((vp_ref, 0, BLK), (vn_ref, BLK + tq, tq + 2 * BLK)) + tuple(
            (vm_ref.at[:, pl.ds(c * BLK, BLK), :], BLK + c * BLK, BLK + (c + 1) * BLK)
            for c in range(nb)):
        vals = src[0].astype(f32)
        vf_ref[0, r0:r1] = jnp.where(vlane < HEAD_DIM, vals,
                                     jnp.where(vlane == HEAD_DIM, 1.0, 0.0)).astype(bf16)
        vf_ref[1, r0:r1] = jnp.where(vlane >= HEAD_DIM, vals,
                                     jnp.where(vlane == 0, 1.0, 0.0)).astype(bf16)

    lane = lax.broadcasted_iota(i32, (BLK, BLK), 1)
    low_half = lane < HEAD_DIM

    for n in range(nb):
        r0 = n * BLK
        qb = q_ref[0, pl.ds(r0, BLK), :]
        kb = kf_ref[pl.ds(r0, 3 * BLK), :]
        zero = jnp.zeros((BLK, BLK), bf16)
        lhs = []
        for kh in range(N_KV):
            for g in range(GQA):
                tile = qb[:, g * BLK:(g + 1) * BLK]
                lhs.append(jnp.where(low_half, tile, zero) if kh == 0
                           else jnp.where(low_half, zero, tile))
        lhs = jnp.concatenate(lhs, axis=0)
        s_all = lax.dot_general(lhs, kb, (((1,), (1,)), ((), ())),
                                preferred_element_type=f32)
        blk = ti * nb + n
        pen = [jnp.where(blk == 0, NEG, 0.0) if n == 0 else None, None,
               jnp.where((blk + 1) * BLK >= seq_len, NEG, 0.0) if n == nb - 1 else None]
        sink_e = []
        for h in range(N_HEADS):
            bias_h = bias_ref[h]
            bias_h = jnp.concatenate(
                [bias_h[:, j * BLK:(j + 1) * BLK] if pen[j] is None
                 else bias_h[:, j * BLK:(j + 1) * BLK] + pen[j] for j in range(3)], axis=1)
            s = s_all[h * BLK:(h + 1) * BLK] + bias_h
            sk = sink_ref[h]
            mrow = jnp.maximum(jnp.max(s, axis=-1, keepdims=True), sk)
            e_ref[n, h * BLK:(h + 1) * BLK, :] = jnp.exp2(s - mrow).astype(bf16)
            sink_e.append(jnp.exp2(sk - mrow))
        half = GQA * BLK
        pv = [jnp.dot(e_ref[n, kh * half:(kh + 1) * half, :], vf_ref[kh, pl.ds(r0, 3 * BLK), :],
                      preferred_element_type=f32) for kh in range(N_KV)]

        def head_out(h):
            kh, g = divmod(h, GQA)
            rows = pv[kh][g * BLK:(g + 1) * BLK]
            ones_col = HEAD_DIM if kh == 0 else 0
            return rows * (1.0 / (rows[:, ones_col:ones_col + 1] + sink_e[h]))

        for g in range(GQA):
            yb_ref[0, pl.ds(r0, BLK), g * BLK:(g + 1) * BLK] = jnp.where(
                low_half, head_out(g), head_out(GQA + g))

    yb_ref[1] = yb_ref[0]


def _attn_out_body(*refs, seq_len):
    step = pl.program_id(0)
    n_tok_tiles = pl.num_programs(0) - 1

    @pl.when(step == 0)
    def _():
        _bias_body(refs[10], refs[11], refs[-1])
        _attn_attention(*refs, seq_len=seq_len)

    @pl.when((step > 0) & (step < n_tok_tiles))
    def _():
        _attn_epilogue(*refs, seq_len=seq_len)
        _attn_attention(*refs, seq_len=seq_len)

    @pl.when(step == n_tok_tiles)
    def _():
        _attn_epilogue(*refs, seq_len=seq_len)


def _attn_out(sink, x, ya, q, k, v, rel_bias, goutb, wout, gffn, wr, br):
    b, s, _ = x.shape
    tq = TQ_ATT
    nb = tq // BLK
    nblk = s // BLK
    t = b * s
    nq = s // tq
    n_tok_tiles = b * nq
    att = lambda s: jnp.minimum(s, n_tok_tiles - 1)
    epi = lambda s: jnp.maximum(s - 1, 0)
    const = lambda shape: pl.BlockSpec(shape, lambda s, *_: (0,) * len(shape))
    tok_att = lambda w: pl.BlockSpec((1, tq, w), lambda s, *_: (att(s) // nq, att(s) % nq, 0))
    tok_epi = lambda w: pl.BlockSpec((1, tq, w), lambda s, *_: (epi(s) // nq, epi(s) % nq, 0))
    prev = pl.BlockSpec((1, BLK, BLK), lambda s, *_: (
        att(s) // nq, jnp.maximum((att(s) % nq) * nb - 1, 0), 0))
    nxt = pl.BlockSpec((1, BLK, BLK), lambda s, *_: (
        att(s) // nq, jnp.minimum((att(s) % nq) * nb + nb, nblk - 1), 0))
    rows = pl.BlockSpec((tq, D_MODEL), lambda s, *_: (epi(s), 0))
    lanes = lambda rows: pl.BlockSpec((rows, tq), lambda s, *_: (0, epi(s)))
    grid_spec = pltpu.PrefetchScalarGridSpec(
        num_scalar_prefetch=1,
        grid=(n_tok_tiles + 1,),
        in_specs=[tok_epi(D_MODEL), tok_epi(D_A), tok_att(D_B),
                  prev, tok_att(BLK), nxt, prev, tok_att(BLK), nxt,
                  pl.BlockSpec(memory_space=pltpu.SMEM), const((8, 4 * BLK)),
                  const((1, D_B)), const((D_MODEL, D_MODEL)),
                  const((1, D_MODEL)), const((D_MODEL, BLK)), const((ROUTE_ROWS, tq))],
        out_specs=(tok_epi(D_MODEL), rows, lanes(8), lanes(8)),
        scratch_shapes=[pltpu.VMEM((tq + 2 * BLK, BLK), bf16),
                        pltpu.VMEM((N_KV, tq + 2 * BLK, BLK), bf16),
                        pltpu.VMEM((nb, N_HEADS * BLK, 3 * BLK), bf16),
                        pltpu.VMEM((2, tq, D_B), f32),
                        pltpu.VMEM((N_HEADS, BLK, 3 * BLK), f32)])
    return pl.pallas_call(
        functools.partial(_attn_out_body, seq_len=s),
        grid_spec=grid_spec,
        out_shape=(jax.ShapeDtypeStruct((b, s, D_MODEL), f32),
                   jax.ShapeDtypeStruct((t, D_MODEL), bf16),
                   jax.ShapeDtypeStruct((8, t), i32),
                   jax.ShapeDtypeStruct((8, t), f32)),
        compiler_params=pltpu.CompilerParams(
            dimension_semantics=("arbitrary",), vmem_limit_bytes=VMEM_LIMIT),
        name="attn_out",
    )(sink, x, ya, q, k, k, k, v, v, v, rel_bias, jnp.asarray(_bucket_row()), goutb, wout, gffn, wr, br)


def _exclusive_prefix(vals):
    a = lax.broadcasted_iota(i32, (N_EXPERTS, N_EXPERTS), 0)
    c = lax.broadcasted_iota(i32, (N_EXPERTS, N_EXPERTS), 1)
    low = jnp.where(c < a, 1.0, 0.0).astype(bf16)
    hi = (vals >> 8).astype(f32).astype(bf16)
    lo = (vals & 255).astype(f32).astype(bf16)
    return (jnp.dot(low, hi, preferred_element_type=f32) * 256.0
            + jnp.dot(low, lo, preferred_element_type=f32)).astype(i32)


def _round_up_pow2(vals, mult):
    log_m = mult.bit_length() - 1
    return ((vals + (mult - 1)) >> log_m) << log_m


PLAN_TILE_EXP, PLAN_FIRST, PLAN_NEXT_EXP, PLAN_SLOT, PLAN_N_USED, PLAN_ENDS, PLAN_PADDED, PLAN_TOTAL = (
    range(8))


def _lane_row(vals, lane_off=0):
    sub = lax.broadcasted_iota(i32, vals.shape, 0)
    lane = lax.broadcasted_iota(i32, vals.shape, 1)
    return jnp.sum(jnp.where(lane == sub + lane_off, vals, 0), axis=0, keepdims=True)


def _route_pos_body(re_ref, slot_ref, runs_ref, plan_ref, tri_ref, *, rows_per_tile, tl):
    n_tok_tiles = re_ref.shape[1] // tl
    a = lax.broadcasted_iota(i32, (tl, tl), 0)
    c = lax.broadcasted_iota(i32, (tl, tl), 1)
    tri_ref[...] = jnp.where(a < c, 1.0, 0.0).astype(bf16)
    eid = lax.broadcasted_iota(i32, (N_EXPERTS, tl), 0)

    def tile_hits(j):
        cols = pl.ds(pl.multiple_of(j * tl, tl), tl)
        hit1 = eid == re_ref[0:1, cols]
        hit2 = eid == re_ref[1:2, cols]
        onehot = jnp.where(hit1 | hit2, 1.0, 0.0)
        cnt = jnp.sum(onehot, axis=1, keepdims=True).astype(i32)
        run_len = jnp.broadcast_to(_round_up_pow2(cnt, RUN_ALIGN), (N_EXPERTS, BLK))
        return cols, hit1, hit2, onehot, run_len

    total = lax.fori_loop(0, n_tok_tiles, lambda j, acc: acc + tile_hits(j)[4],
                          jnp.zeros((N_EXPERTS, BLK), i32), unroll=ROUTE_UNROLL)
    padded = _round_up_pow2(total, rows_per_tile)
    starts = _exclusive_prefix(padded)
    ends = starts + padded

    log_r = rows_per_tile.bit_length() - 1
    sub = lax.broadcasted_iota(i32, (N_EXPERTS, BLK), 0)
    lane = lax.broadcasted_iota(i32, (1, BLK), 1)
    n_used = ends[N_EXPERTS - 1:N_EXPERTS, :] >> log_r
    tile_idx = jnp.minimum(lane, n_used - 1)
    tile_exp = jnp.minimum(
        jnp.sum((ends <= (tile_idx << log_r)).astype(i32), axis=0, keepdims=True), N_EXPERTS - 1)
    nonempty = padded > 0
    regions_before = jnp.sum(((sub < tile_exp) & nonempty).astype(i32), axis=0, keepdims=True)
    nxt = jnp.min(jnp.where((sub > tile_exp) & nonempty, sub, N_EXPERTS), axis=0, keepdims=True)
    prev_exp = pltpu.roll(jnp.broadcast_to(tile_exp, (8, BLK)), 1, 1)[0:1]
    plan_ref[PLAN_TILE_EXP:PLAN_TILE_EXP + 1] = tile_exp
    plan_ref[PLAN_FIRST:PLAN_FIRST + 1] = ((lane == 0) | (tile_exp != prev_exp)).astype(i32)
    plan_ref[PLAN_NEXT_EXP:PLAN_NEXT_EXP + 1] = jnp.where(nxt < N_EXPERTS, nxt, -1)
    plan_ref[PLAN_SLOT:PLAN_SLOT + 1] = regions_before & 1
    plan_ref[PLAN_N_USED:PLAN_N_USED + 1] = n_used
    plan_ref[PLAN_ENDS:PLAN_ENDS + 1] = _lane_row(ends)
    plan_ref[PLAN_PADDED:PLAN_PADDED + 1] = _lane_row(padded)
    plan_ref[PLAN_TOTAL:PLAN_TOTAL + 1] = _lane_row(total)

    def place(j, global_off):
        cols, hit1, hit2, onehot, run_len = tile_hits(j)
        local_off = _exclusive_prefix(run_len)
        before = jnp.dot(onehot.astype(bf16), tri_ref[...], preferred_element_type=f32)
        slot = before + local_off[:, 0:1].astype(f32)
        s1 = jnp.sum(jnp.where(hit1, slot, 0.0), axis=0, keepdims=True)
        s2 = jnp.sum(jnp.where(hit2, slot, 0.0), axis=0, keepdims=True)
        sub = lax.broadcasted_iota(i32, (8, tl), 0)
        slot_ref[:, cols] = jnp.where(sub == 0, s1.astype(i32),
                                      jnp.where(sub == 1, s2.astype(i32), 0))
        runs_ref[j] = (_lane_row(local_off) + _lane_row(run_len, N_EXPERTS)
                       + _lane_row(global_off, 2 * N_EXPERTS))
        return global_off + run_len

    lax.fori_loop(0, n_tok_tiles, place, starts, unroll=ROUTE_UNROLL)


def _route_pos(route_e, rows_per_tile):
    t = route_e.shape[1]
    tl = TL_SORT
    return pl.pallas_call(
        functools.partial(_route_pos_body, rows_per_tile=rows_per_tile, tl=tl),
        out_shape=(jax.ShapeDtypeStruct((8, t), i32),
                   jax.ShapeDtypeStruct((t // tl, 1, BLK), i32),
                   jax.ShapeDtypeStruct((8, BLK), i32)),
        scratch_shapes=[pltpu.VMEM((tl, tl), bf16)],
        compiler_params=pltpu.CompilerParams(vmem_limit_bytes=VMEM_LIMIT),
        name="route_pos",
    )(route_e)


def _for_each_run_piece(runs_ref, fn):
    def per_expert(e, carry):
        local_off = runs_ref[0, 0, e]
        length = runs_ref[0, 0, N_EXPERTS + e]
        global_off = runs_ref[0, 0, 2 * N_EXPERTS + e]
        units = length >> (RUN_ALIGN.bit_length() - 1)

        def big_piece(k, inner):
            done = k * BIG_PIECE
            fn(pl.multiple_of(local_off + done, RUN_ALIGN),
               pl.multiple_of(global_off + done, RUN_ALIGN), BIG_PIECE)
            return inner

        lax.fori_loop(0, units >> SMALL_BITS, big_piece, 0)
        for b in range(SMALL_BITS):
            @pl.when(((units >> b) & 1) == 1)
            def _():
                done = ((units >> (b + 1)) << (b + 1)) * RUN_ALIGN
                fn(pl.multiple_of(local_off + done, RUN_ALIGN),
                   pl.multiple_of(global_off + done, RUN_ALIGN), RUN_ALIGN << b)
        return carry

    lax.fori_loop(0, N_EXPERTS, per_expert, 0)


def _tile_run_rows(runs_ref):
    return lax.fori_loop(0, N_EXPERTS, lambda e, acc: acc + runs_ref[0, 0, N_EXPERTS + e], 0)


def _wait_rows(total_rows, make_copy):
    units = total_rows >> (RUN_ALIGN.bit_length() - 1)
    for b in range((LB_SORT // RUN_ALIGN).bit_length()):
        @pl.when(((units >> b) & 1) == 1)
        def _():
            make_copy(RUN_ALIGN << b).wait()


def _sort_rows_body(plan_ref, runs_ref, m_ref, slot_ref, rw_ref,
                    xs_ref, slot_t_ref,
                    local_ref, zero_ref, pending_ref, sem_ref, zsem_ref, *, rows_per_tile):
    tl = m_ref.shape[0]
    lb = local_ref.shape[1]
    step = pl.program_id(0)
    cur = step % 2
    n_tiles = xs_ref.shape[0] // rows_per_tile
    n_used = plan_ref[PLAN_N_USED, 0]

    def zero_rows(start, rows, wait):
        cp = pltpu.make_async_copy(zero_ref.at[pl.ds(0, rows), :],
                                   xs_ref.at[pl.ds(pl.multiple_of(start, RUN_ALIGN), rows), :],
                                   zsem_ref)
        cp.wait() if wait else cp.start()

    def zero_fill(wait):
        for e in range(N_EXPERTS):
            pad_rows = plan_ref[PLAN_PADDED, e] - plan_ref[PLAN_TOTAL, e]
            pad_units = pad_rows >> (RUN_ALIGN.bit_length() - 1)
            for bit in range((rows_per_tile // RUN_ALIGN).bit_length() - 1):
                @pl.when(((pad_units >> bit) & 1) == 1)
                def _():
                    done = ((pad_units >> (bit + 1)) << (bit + 1)) * RUN_ALIGN
                    zero_rows(plan_ref[PLAN_ENDS, e] - pad_rows + done,
                              RUN_ALIGN << bit, wait)

        def spare(j, carry):
            zero_rows((n_used + j) * rows_per_tile, rows_per_tile, wait)
            return carry

        lax.fori_loop(0, n_tiles - n_used, spare, 0)

    @pl.when(step == 0)
    def _():
        zero_ref[...] = jnp.zeros(zero_ref.shape, bf16)
        zero_fill(wait=False)

    s1 = slot_ref[0:1, :]
    s2 = slot_ref[1:2, :]
    w1 = rw_ref[0:1, :]
    w2 = rw_ref[1:2, :]
    run_rows = _tile_run_rows(runs_ref)

    def sort_rows_upto(n):
        srow = lax.broadcasted_iota(i32, (n, tl), 0)
        p1 = srow == s1
        p2 = srow == s2
        perm = jnp.where(p1 | p2, 1.0, 0.0).astype(bf16)
        rows = jnp.dot(perm, m_ref[...], preferred_element_type=f32)
        w = jnp.sum(jnp.where(p1, w1, 0.0) + jnp.where(p2, w2, 0.0), axis=1, keepdims=True)
        w_hi = w.astype(bf16).astype(f32)
        w_lo = w - w_hi
        lane = lax.broadcasted_iota(i32, (n, BLK), 1)
        local_ref[cur, :n, :D_MODEL] = rows.astype(bf16)
        local_ref[cur, :n, D_MODEL:] = jnp.where(
            lane == 0, w_hi, jnp.where(lane == 1, w_lo, 0.0)).astype(bf16)
        sub_t = lax.broadcasted_iota(i32, (BLK, tl), 0)
        slots = jnp.where(sub_t == 0, s1.astype(f32), jnp.where(sub_t == 1, s2.astype(f32), 0.0))
        slot_t_ref[...] = jnp.transpose(slots)[:, :8].astype(i32)

    def piece(s, local_row, global_row, rows):
        return pltpu.make_async_copy(local_ref.at[s, pl.ds(local_row, rows), :],
                                     xs_ref.at[pl.ds(global_row, rows), :], sem_ref.at[s])

    short = lb - 256
    pl.when(run_rows <= short)(functools.partial(sort_rows_upto, short))
    pl.when(run_rows > short)(functools.partial(sort_rows_upto, lb))

    _for_each_run_piece(runs_ref, lambda l, g, n: piece(cur, l, g, n).start())

    @pl.when(step > 0)
    def _():
        _wait_rows(pending_ref[0], lambda n: piece(1 - cur, 0, 0, n))

    pending_ref[0] = run_rows

    @pl.when(step == pl.num_programs(0) - 1)
    def _():
        _wait_rows(pending_ref[0], lambda n: piece(cur, 0, 0, n))
        zero_fill(wait=True)


def _sort_rows(plan, run_tiles, m, slot, route_w, total_rows, rows_per_tile):
    t = m.shape[0]
    tl = TL_SORT
    grid_spec = pltpu.PrefetchScalarGridSpec(
        num_scalar_prefetch=1,
        grid=(t // tl,),
        in_specs=[pl.BlockSpec((1, 1, BLK), lambda i, *_: (i, 0, 0),
                               memory_space=pltpu.SMEM),
                  pl.BlockSpec((tl, D_MODEL), lambda i, *_: (i, 0)),
                  pl.BlockSpec((8, tl), lambda i, *_: (0, i)),
                  pl.BlockSpec((8, tl), lambda i, *_: (0, i))],
        out_specs=(pl.BlockSpec(memory_space=pl.ANY),
                   pl.BlockSpec((tl, 8), lambda i, *_: (i, 0))),
        scratch_shapes=[pltpu.VMEM((2, LB_SORT, XS_COLS), bf16),
                        pltpu.VMEM((rows_per_tile, XS_COLS), bf16),
                        pltpu.SMEM((1,), i32),
                        pltpu.SemaphoreType.DMA((2,)),
                        pltpu.SemaphoreType.DMA(())])
    return pl.pallas_call(
        functools.partial(_sort_rows_body, rows_per_tile=rows_per_tile),
        grid_spec=grid_spec,
        out_shape=(jax.ShapeDtypeStruct((total_rows, XS_COLS), bf16),
                   jax.ShapeDtypeStruct((t, 8), i32)),
        compiler_params=pltpu.CompilerParams(
            dimension_semantics=("arbitrary",), vmem_limit_bytes=VMEM_LIMIT),
        name="sort_rows",
    )(plan, run_tiles, m, slot, route_w)


def _experts_body(plan_ref,
                  xs_hbm, wg_hbm, wu_hbm, wd_hbm, ys_hbm,
                  xbuf_ref, sg_ref, su_ref, sd_ref, wg_ref, wu_ref, wd_ref,
                  xsem_ref, osem_ref, sem_ref):
    i = pl.program_id(0)
    n_used = plan_ref[PLAN_N_USED, 0]
    r = xbuf_ref.shape[1]

    def tile_rows(ref, tile):
        return ref.at[pl.ds(pl.multiple_of(tile * r, r), r), :]

    def fetch_tile(tile):
        s = tile % XS_RING
        return pltpu.make_async_copy(tile_rows(xs_hbm, tile), xbuf_ref.at[s], xsem_ref.at[s])

    def store_tile(tile):
        s = tile % XS_RING
        return pltpu.make_async_copy(xbuf_ref.at[s], tile_rows(ys_hbm, tile), osem_ref.at[s])

    @pl.when(i == 0)
    def _():
        for ahead in range(XS_AHEAD):
            @pl.when(ahead < n_used)
            def _():
                fetch_tile(ahead).start()

    @pl.when(i + XS_AHEAD < n_used)
    def _():
        @pl.when(i + XS_AHEAD >= XS_RING)
        def _():
            store_tile(i + XS_AHEAD - XS_RING).wait()

        fetch_tile(i + XS_AHEAD).start()

    def fetches(expert, s):
        return (pltpu.make_async_copy(wg_hbm.at[expert], sg_ref.at[s], sem_ref.at[s]),
                pltpu.make_async_copy(wu_hbm.at[expert], su_ref.at[s], sem_ref.at[s]),
                pltpu.make_async_copy(wd_hbm.at[expert], sd_ref.at[s], sem_ref.at[s]))

    @pl.when(i < n_used)
    def _():
        s = plan_ref[PLAN_SLOT, i]
        fetch_tile(i).wait()
        xs_ref = xbuf_ref.at[i % XS_RING]

        @pl.when(plan_ref[PLAN_FIRST, i] == 1)
        def _():
            @pl.when(i == 0)
            def _():
                for cp in fetches(plan_ref[PLAN_TILE_EXP, 0], 0):
                    cp.start()

            for cp in fetches(plan_ref[PLAN_TILE_EXP, i], s):
                cp.wait()

            @pl.when(plan_ref[PLAN_NEXT_EXP, i] >= 0)
            def _():
                for cp in fetches(plan_ref[PLAN_NEXT_EXP, i], 1 - s):
                    cp.start()

            wg_ref[...] = sg_ref[s].astype(bf16)
            wu_ref[...] = su_ref[s].astype(bf16)
            wd_ref[...] = sd_ref[s].astype(bf16)

        w_row = (xs_ref[:, D_MODEL:D_MODEL + 1].astype(f32)
                 + xs_ref[:, D_MODEL + 1:D_MODEL + 2].astype(f32))
        gate = jnp.dot(xs_ref[:, :D_MODEL], wg_ref[...], preferred_element_type=f32)
        up = jnp.dot(xs_ref[:, :D_MODEL], wu_ref[...], preferred_element_type=f32)
        hdn = (gate * jax.nn.sigmoid(gate)) * up
        y = jnp.dot(hdn.astype(bf16), wd_ref[...], preferred_element_type=f32)
        xs_ref[:, :D_MODEL] = (y * w_row).astype(bf16)
        store_tile(i).start()

    @pl.when(i == pl.num_programs(0) - 1)
    def _():
        for back in range(1, XS_RING + 1):
            @pl.when(n_used - back >= 0)
            def _():
                store_tile(n_used - back).wait()


def _experts(plan, xs, wg, wu, wd, rows_per_tile):
    r = rows_per_tile
    n_tiles = xs.shape[0] // r
    assert n_tiles <= BLK, "the plan holds one row tile per lane"
    grid_spec = pltpu.PrefetchScalarGridSpec(
        num_scalar_prefetch=1,
        grid=(n_tiles,),
        in_specs=[pl.BlockSpec(memory_space=pl.ANY),
                  pl.BlockSpec(memory_space=pl.ANY),
                  pl.BlockSpec(memory_space=pl.ANY),
                  pl.BlockSpec(memory_space=pl.ANY)],
        out_specs=pl.BlockSpec(memory_space=pl.ANY),
        scratch_shapes=[pltpu.VMEM((XS_RING, r, XS_COLS), bf16),
                        pltpu.VMEM((2, D_MODEL, D_FF_E), f32),
                        pltpu.VMEM((2, D_MODEL, D_FF_E), f32),
                        pltpu.VMEM((2, D_FF_E, D_MODEL), f32),
                        pltpu.VMEM((D_MODEL, D_FF_E), bf16),
                        pltpu.VMEM((D_MODEL, D_FF_E), bf16),
                        pltpu.VMEM((D_FF_E, D_MODEL), bf16),
                        pltpu.SemaphoreType.DMA((XS_RING,)),
                        pltpu.SemaphoreType.DMA((XS_RING,)),
                        pltpu.SemaphoreType.DMA((2,))])
    return pl.pallas_call(
        _experts_body,
        grid_spec=grid_spec,
        out_shape=jax.ShapeDtypeStruct(xs.shape, xs.dtype),
        input_output_aliases={1: 0},
        compiler_params=pltpu.CompilerParams(
            dimension_semantics=("arbitrary",), vmem_limit_bytes=VMEM_LIMIT),
        name="experts",
    )(plan, xs, wg, wu, wd)


def _combine_out_body(runs_cur_ref, runs_nxt_ref, h_ref, slot_ref, p_ref, ys_ref,
                      wpg_ref, bpg_ref, wpp_ref, gple_ref, gfin_ref,
                      o_ref, ybuf_ref, moe_ref, sem_ref):
    tk = h_ref.shape[0]
    lb = ybuf_ref.shape[1]
    step = pl.program_id(0)
    n = pl.num_programs(0) - 1
    cur = jnp.minimum(step, n - 1) % 2

    def piece(s, local_row, global_row, rows):
        return pltpu.make_async_copy(ys_ref.at[pl.ds(global_row, rows), pl.ds(0, D_MODEL)],
                                     ybuf_ref.at[s, pl.ds(local_row, rows), :], sem_ref.at[s])

    @pl.when(step == 0)
    def _():
        ybuf_ref[...] = jnp.zeros(ybuf_ref.shape, bf16)
        _for_each_run_piece(runs_cur_ref, lambda l, g, r: piece(0, l, g, r).start())

    @pl.when(step + 1 < n)
    def _():
        _for_each_run_piece(runs_nxt_ref, lambda l, g, r: piece(1 - cur, l, g, r).start())

    run_rows = _tile_run_rows(runs_cur_ref)

    @pl.when(step < n)
    def _():
        _wait_rows(run_rows, lambda r: piece(cur, 0, 0, r))

    def work(rows, finish, unsort):
        if finish:
            h2 = h_ref[...] + moe_ref[...]
            gate = jax.nn.sigmoid(jnp.dot(h2.astype(bf16), wpg_ref[...], preferred_element_type=f32)
                                  + bpg_ref[...])
            pp = jnp.dot(p_ref[...].astype(bf16), wpp_ref[...], preferred_element_type=f32)
            h3 = h2 + gate * _rms(pp, gple_ref[...])
            o_ref[...] = _rms(h3, gfin_ref[...])
        if unsort:
            col = lax.broadcasted_iota(i32, (tk, rows), 1)
            unperm = jnp.where((col == slot_ref[:, 0:1]) | (col == slot_ref[:, 1:2]), 1.0, 0.0)
            moe_ref[...] = jnp.dot(unperm.astype(bf16), ybuf_ref[cur, :rows, :],
                                   preferred_element_type=f32)

    short = lb - 256
    is_short = run_rows <= short
    first = step == 0
    mid = (step > 0) & (step < n)
    pl.when(first & is_short)(functools.partial(work, short, False, True))
    pl.when(first & jnp.logical_not(is_short))(functools.partial(work, lb, False, True))
    pl.when(mid & is_short)(functools.partial(work, short, True, True))
    pl.when(mid & jnp.logical_not(is_short))(functools.partial(work, lb, True, True))
    pl.when(step == n)(functools.partial(work, lb, True, False))


def _combine_out(run_tiles, h2d, slot_t, p2d, ys, wpg, bpg, wpp, gple, gfin):
    t = h2d.shape[0]
    tk = TL_SORT
    nt = t // tk
    const = lambda shape: pl.BlockSpec(shape, lambda i: (0,) * len(shape))
    runs = lambda imap: pl.BlockSpec((1, 1, BLK), imap, memory_space=pltpu.SMEM)
    uns = lambda s: jnp.minimum(s, nt - 1)
    fin = lambda s: jnp.maximum(s - 1, 0)
    return pl.pallas_call(
        _combine_out_body,
        grid=(nt + 1,),
        out_shape=jax.ShapeDtypeStruct((t, D_MODEL), f32),
        in_specs=[runs(lambda s: (uns(s), 0, 0)),
                  runs(lambda s: (uns(s + 1), 0, 0)),
                  pl.BlockSpec((tk, D_MODEL), lambda s: (fin(s), 0)),
                  pl.BlockSpec((tk, 8), lambda s: (uns(s), 0)),
                  pl.BlockSpec((tk, PLE_DIM), lambda s: (fin(s), 0)),
                  pl.BlockSpec(memory_space=pl.ANY),
                  const((D_MODEL, D_MODEL)), const((1, D_MODEL)),
                  const((PLE_DIM, D_MODEL)), const((1, D_MODEL)), const((1, D_MODEL))],
        out_specs=pl.BlockSpec((tk, D_MODEL), lambda s: (fin(s), 0)),
        scratch_shapes=[pltpu.VMEM((2, LB_SORT, D_MODEL), bf16),
                        pltpu.VMEM((tk, D_MODEL), f32),
                        pltpu.SemaphoreType.DMA((2,))],
        compiler_params=pltpu.CompilerParams(
            dimension_semantics=("arbitrary",), vmem_limit_bytes=VMEM_LIMIT),
        name="combine_out",
    )(run_tiles, run_tiles, h2d, slot_t, p2d, ys, wpg, bpg, wpp, gple, gfin)


def _head_major_to_tile_major(a, axis):
    axis = axis % a.ndim
    shape = a.shape
    a = a.reshape(shape[:axis] + (N_KV, GQA, HEAD_DIM) + shape[axis + 1:])
    return jnp.swapaxes(a, axis, axis + 1).reshape(shape)


def kernel(x, p, rel_bias, g_mix, w_in, ln_v_g, ln_v_b, w_spatial, b_spatial, sink, g_out_grp, w_out,
           g_ffn, w_router_group, b_router_group, w_router_expert, b_router_expert, w_gate_e, w_up_e,
           w_down_e, w_ple_proj, g_ple, w_ple_gate, b_ple_gate, g_final):
    b, s, d = x.shape
    t = b * s
    depth = g_mix.shape[0]
    assert depth == 1 and d == D_MODEL
    c1, c2 = 2 * D_A, 2 * D_A + D_B
    bias = (rel_bias * LOG2E).astype(f32)
    hcur = x.astype(f32)
    for li in range(depth):
        col_scale = np.ones((1, D_IN), np.float32)
        col_scale[:, c1:c2] = LOG2E * HEAD_DIM ** -0.5
        win = (w_in[li] * col_scale).astype(bf16)
        bs = jnp.broadcast_to(b_spatial[li][:, :, None], (A_HEADS, BLK, BLK)).astype(f32)
        gout = g_out_grp[li]
        gout_b = _head_major_to_tile_major(gout[D_A:], 0)[None]
        wo = w_out[li]
        wout = jnp.concatenate([wo[:D_A], _head_major_to_tile_major(wo[D_A:], 0)],
                               axis=0).astype(bf16)
        wr = jnp.concatenate([
            w_router_group[li], jnp.zeros((D_MODEL, 8 - N_GROUPS), f32), w_router_expert[li],
            jnp.zeros((D_MODEL, BLK - ROUTE_ROWS), f32)], axis=1).astype(bf16)
        br = jnp.concatenate([b_router_group[li], jnp.full((8 - N_GROUPS,), NEG, f32),
                              b_router_expert[li]])
        br = jnp.broadcast_to(br[:, None], (ROUTE_ROWS, TQ_ATT))

        ya, q, k, v = _mix_in(hcur.reshape(t, d), g_mix[li][None], win, ln_v_g[li][None],
                              ln_v_b[li][None], w_spatial[li].astype(bf16), bs, gout[None, :D_A])
        h, m_rows, route_e, route_w = _attn_out(
            sink[li].astype(f32) * LOG2E, hcur, ya.reshape(b, s, D_A), q.reshape(b, s, D_B),
            k.reshape(b, s, BLK), v.reshape(b, s, BLK), bias, gout_b, wout,
            g_ffn[li][None], wr, br)

        r = R_EXP
        n_sort = t // TL_SORT
        max_rows = 2 * t + n_sort * N_EXPERTS * (RUN_ALIGN - 1)
        n_tiles = -(-max_rows // r) + N_EXPERTS
        assert n_tiles * r < 2 ** 16, "route_pos prefix sums hold row offsets in two bf16 bytes"
        slot, run_tiles, plan = _route_pos(route_e, r)
        xs, slot_t = _sort_rows(plan, run_tiles, m_rows, slot, route_w, n_tiles * r, r)
        ys = _experts(plan, xs, w_gate_e[li], w_up_e[li], w_down_e[li], r)

        out = _combine_out(run_tiles, h.reshape(t, d), slot_t,
                           p[li].reshape(t, PLE_DIM), ys, w_ple_gate[li].astype(bf16),
                           b_ple_gate[li][None], w_ple_proj[li].astype(bf16), g_ple[li][None],
                           g_final[None])
        hcur = out.reshape(b, s, d)
    return hcur
```

```python
import functools
import math

import jax
import jax.numpy as jnp
import numpy as np
from jax import lax
from jax.experimental import pallas as pl
from jax.experimental.pallas import tpu as pltpu

D_MODEL = 1024
D_A = 512
D_B = 512
BLK = 128
A_HEADS = 4
HEAD_DIM = 64
N_HEADS = 8
N_KV = 2
GQA = 4
WINDOW = 128
NUM_BUCKETS = 32
MAX_DIST = 128
D_IN = 2 * D_A + D_B + 2 * N_KV * HEAD_DIM
N_GROUPS = 4
E_PER_GROUP = 8
N_EXPERTS = 32
D_FF_E = 256
PLE_DIM = 256
EPS = 1e-6
NEG = -1e30
LOG2E = math.log2(math.e)

TM_MIX = 512
TQ_ATT = 512
TL_SORT = 512
RUN_ALIGN = 16
SMALL_BITS = 2
BIG_PIECE = RUN_ALIGN << SMALL_BITS
LB_SORT = 2 * TL_SORT + N_EXPERTS * RUN_ALIGN
R_EXP = 512
XS_COLS = D_MODEL + BLK
XS_RING = 4
XS_AHEAD = 2
ROUTE_ROWS = 8 + N_EXPERTS
ROUTE_UNROLL = 8
VMEM_LIMIT = 48 * 1024 * 1024

f32 = jnp.float32
bf16 = jnp.bfloat16
i32 = jnp.int32


def _rms(x, g):
    return x * lax.rsqrt(jnp.mean(x * x, axis=-1, keepdims=True) + EPS) * g


def _gelu_tanh(x):
    c = math.sqrt(2.0 / math.pi)
    return x * (0.5 * (1.0 + jnp.tanh(c * (x + 0.044715 * (x * x * x)))))


def _bucket_row():
    n = NUM_BUCKETS // 2
    max_exact = n // 2
    rel = np.broadcast_to(np.arange(4 * BLK)[None, :] - BLK, (8, 4 * BLK))
    ret = np.where(rel > 0, n, 0)
    a = np.abs(rel)
    large = max_exact + (np.log(np.maximum(a, 1).astype(np.float64) / max_exact)
                         / math.log(MAX_DIST / max_exact) * (n - max_exact)).astype(np.int32)
    large = np.minimum(large, n - 1)
    bucket = ret + np.where(a < max_exact, a, large)
    return np.where(a <= WINDOW, bucket, -1).astype(np.int32)


def _bias_body(rb_ref, bucket_ref, o_ref):
    bucket = bucket_ref[...]
    row = lax.broadcasted_iota(i32, (BLK, BLK), 0)
    lane = lax.broadcasted_iota(i32, (BLK, BLK), 1)
    for h in range(N_HEADS):
        base = jnp.full(bucket.shape, NEG, f32)
        for b in range(NUM_BUCKETS):
            base = jnp.where(bucket == b, rb_ref[b, h], base)
        prev = jnp.full((BLK, BLK), NEG, f32)
        for c in range(3):
            tile = jnp.broadcast_to(base[0:1, c * BLK:(c + 1) * BLK], (BLK, BLK))
            rolled = pltpu.roll(tile, 0, 1, stride=1, stride_axis=0)
            o_ref[h, :, c * BLK:(c + 1) * BLK] = jnp.where(lane >= row, rolled, prev)
            prev = rolled


def _mix_in_body(x_ref, gmix_ref, win_ref, lng_ref, lnb_ref, ws_ref, bs_ref, gout_ref,
                 ya_ref, q_ref, k_ref, v_ref):
    tm = x_ref.shape[0]
    nc = tm // BLK
    a = _rms(x_ref[...], gmix_ref[...])
    z = jnp.dot(a.astype(bf16), win_ref[...], preferred_element_type=f32)
    uv = _gelu_tanh(z[:, :2 * D_A])
    u = uv[:, :D_A]
    v = uv[:, D_A:]
    mu = jnp.mean(v, axis=-1, keepdims=True)
    vc = v - mu
    var = jnp.mean(vc * vc, axis=-1, keepdims=True)
    vn = (vc * lax.rsqrt(var + EPS) * lng_ref[...] + lnb_ref[...]).astype(bf16)
    cols = []
    for h in range(A_HEADS):
        rhs = jnp.concatenate(
            [vn[c * BLK:(c + 1) * BLK, h * BLK:(h + 1) * BLK] for c in range(nc)], axis=1)
        r = jnp.dot(ws_ref[h], rhs, preferred_element_type=f32)
        cols.append(jnp.concatenate(
            [r[:, c * BLK:(c + 1) * BLK] + bs_ref[h] for c in range(nc)], axis=0))
    sv = jnp.concatenate(cols, axis=1)
    ya_ref[...] = _rms(u * sv, gout_ref[...]).astype(bf16)
    qt = [z[:, 2 * D_A + j * BLK:2 * D_A + (j + 1) * BLK] for j in range(D_B // BLK)]
    low_half = lax.broadcasted_iota(i32, (tm, BLK), 1) < HEAD_DIM
    for g in range(GQA):
        a0, a1 = qt[g // 2], qt[GQA // 2 + g // 2]
        tile = (jnp.where(low_half, a0, pltpu.roll(a1, HEAD_DIM, 1)) if g % 2 == 0
                else jnp.where(low_half, pltpu.roll(a0, HEAD_DIM, 1), a1))
        q_ref[:, g * BLK:(g + 1) * BLK] = tile.astype(bf16)
    k_ref[...] = z[:, 2 * D_A + D_B:2 * D_A + D_B + BLK].astype(bf16)
    v_ref[...] = z[:, 2 * D_A + D_B + BLK:].astype(bf16)


def _mix_in(x2, gmix, win, lng, lnb, ws, bs, gout_a):
    t = x2.shape[0]
    tm = TM_MIX
    const = lambda shape: pl.BlockSpec(shape, lambda i: (0,) * len(shape))
    return pl.pallas_call(
        _mix_in_body,
        grid=(t // tm,),
        out_shape=(jax.ShapeDtypeStruct((t, D_A), bf16),
                   jax.ShapeDtypeStruct((t, D_B), bf16),
                   jax.ShapeDtypeStruct((t, BLK), bf16),
                   jax.ShapeDtypeStruct((t, BLK), bf16)),
        in_specs=[pl.BlockSpec((tm, D_MODEL), lambda i: (i, 0)),
                  const((1, D_MODEL)), const((D_MODEL, D_IN)),
                  const((1, D_A)), const((1, D_A)),
                  const((A_HEADS, BLK, BLK)), const((A_HEADS, BLK, BLK)),
                  const((1, D_A))],
        out_specs=(pl.BlockSpec((tm, D_A), lambda i: (i, 0)),
                   pl.BlockSpec((tm, D_B), lambda i: (i, 0)),
                   pl.BlockSpec((tm, BLK), lambda i: (i, 0)),
                   pl.BlockSpec((tm, BLK), lambda i: (i, 0))),
        compiler_params=pltpu.CompilerParams(
            dimension_semantics=("parallel",), vmem_limit_bytes=VMEM_LIMIT),
        name="mix_in",
    )(x2, gmix, win, lng, lnb, ws, bs, gout_a)


def _attn_epilogue(sink_ref, x_ref, ya_ref, q_ref, kp_ref, km_ref, kn_ref, vp_ref, vm_ref, vn_ref,
                   rb_ref, bucket_ref, goutb_ref, wout_hbm, gffn_ref, wr_ref, br_ref,
                   h_ref, m_ref, re_ref, rw_ref,
                   kf_ref, vf_ref, e_ref, yb_ref, bias_ref, wstage_ref, wbf_ref, wsem_ref, *, seq_len):
    tq = x_ref.shape[1]
    ybn = _rms(yb_ref[1], goutb_ref[...]).astype(bf16)
    y = jnp.concatenate([ya_ref[0], ybn], axis=1)
    h = x_ref[0] + jnp.dot(y, wbf_ref[...], preferred_element_type=f32)
    h_ref[0] = h
    m = _rms(h, gffn_ref[...])
    m_ref[...] = m.astype(bf16)

    logit_t = jnp.dot(m.astype(bf16), wr_ref[...], preferred_element_type=f32)
    logit = jnp.transpose(logit_t)[:ROUTE_ROWS] + br_ref[...]
    sub = lax.broadcasted_iota(i32, (8, tq), 0)
    lg = logit[0:8]
    mg = jnp.max(lg, axis=0, keepdims=True)
    pg_top = 1.0 / jnp.sum(jnp.exp(lg - mg), axis=0, keepdims=True)
    g_idx = jnp.min(jnp.where(lg == mg, sub, 8), axis=0, keepdims=True)
    sel = logit[8:16]
    for g in range(1, N_GROUPS):
        sel = jnp.where(g_idx == g, logit[8 + 8 * g:16 + 8 * g], sel)
    m1 = jnp.max(sel, axis=0, keepdims=True)
    i1 = jnp.min(jnp.where(sel == m1, sub, 8), axis=0, keepdims=True)
    sel2 = jnp.where(sub == i1, -jnp.inf, sel)
    m2 = jnp.max(sel2, axis=0, keepdims=True)
    i2 = jnp.min(jnp.where(sel2 == m2, sub, 8), axis=0, keepdims=True)
    r = jnp.exp(m2 - m1)
    w1 = pg_top / (1.0 + r)
    w2 = pg_top * r / (1.0 + r)
    e1 = g_idx * E_PER_GROUP + i1
    e2 = g_idx * E_PER_GROUP + i2
    re_ref[...] = jnp.where(sub == 0, e1, jnp.where(sub == 1, e2, 0))
    rw_ref[...] = jnp.where(sub == 0, w1, jnp.where(sub == 1, w2, 0.0))


def _attn_attention(sink_ref, x_ref, ya_ref, q_ref, kp_ref, km_ref, kn_ref, vp_ref, vm_ref, vn_ref,
                    rb_ref, bucket_ref, goutb_ref, wout_hbm, gffn_ref, wr_ref, br_ref,
                    h_ref, m_ref, re_ref, rw_ref,
                    kf_ref, vf_ref, e_ref, yb_ref, bias_ref, wstage_ref, wbf_ref, wsem_ref, *, seq_len):
    tq = x_ref.shape[1]
    nb = tq // BLK
    ti = pl.program_id(0) % (seq_len // tq)
    kf_ref[0:BLK] = kp_ref[0]
    kf_ref[BLK:BLK + tq] = km_ref[0]
    kf_ref[BLK + tq:] = kn_ref[0]
    vlane = lax.broadcasted_iota(i32, (BLK, BLK), 1)
    for src, r0, r1 in ((vp_ref, 0, BLK), (vn_ref, BLK + tq, tq + 2 * BLK)) + tuple(
            (vm_ref.at[:, pl.ds(c * BLK, BLK), :], BLK + c * BLK, BLK + (c + 1) * BLK)
            for c in range(nb)):
        vals = src[0].astype(f32)
        vf_ref[0, r0:r1] = jnp.where(vlane < HEAD_DIM, vals,
                                     jnp.where(vlane == HEAD_DIM, 1.0, 0.0)).astype(bf16)
        vf_ref[1, r0:r1] = jnp.where(vlane >= HEAD_DIM, vals,
                                     jnp.where(vlane == 0, 1.0, 0.0)).astype(bf16)

    lane = lax.broadcasted_iota(i32, (BLK, BLK), 1)
    low_half = lane < HEAD_DIM

    for n in range(nb):
        r0 = n * BLK
        qb = q_ref[0, pl.ds(r0, BLK), :]
        kb = kf_ref[pl.ds(r0, 3 * BLK), :]
        zero = jnp.zeros((BLK, BLK), bf16)
        lhs = []
        for kh in range(N_KV):
            for g in range(GQA):
                tile = qb[:, g * BLK:(g + 1) * BLK]
                lhs.append(jnp.where(low_half, tile, zero) if kh == 0
                           else jnp.where(low_half, zero, tile))
        lhs = jnp.concatenate(lhs, axis=0)
        s_all = lax.dot_general(lhs, kb, (((1,), (1,)), ((), ())),
                                preferred_element_type=f32)
        blk = ti * nb + n
        pen = [jnp.where(blk == 0, NEG, 0.0) if n == 0 else None, None,
               jnp.where((blk + 1) * BLK >= seq_len, NEG, 0.0) if n == nb - 1 else None]
        sink_e = []
        for h in range(N_HEADS):
            bias_h = bias_ref[h]
            bias_h = jnp.concatenate(
                [bias_h[:, j * BLK:(j + 1) * BLK] if pen[j] is None
                 else bias_h[:, j * BLK:(j + 1) * BLK] + pen[j] for j in range(3)], axis=1)
            s = s_all[h * BLK:(h + 1) * BLK] + bias_h
            sk = sink_ref[h]
            mrow = jnp.maximum(jnp.max(s, axis=-1, keepdims=True), sk)
            e_ref[n, h * BLK:(h + 1) * BLK, :] = jnp.exp2(s - mrow).astype(bf16)
            sink_e.append(jnp.exp2(sk - mrow))
        half = GQA * BLK
        pv = [jnp.dot(e_ref[n, kh * half:(kh + 1) * half, :], vf_ref[kh, pl.ds(r0, 3 * BLK), :],
                      preferred_element_type=f32) for kh in range(N_KV)]

        def head_out(h):
            kh, g = divmod(h, GQA)
            rows = pv[kh][g * BLK:(g + 1) * BLK]
            ones_col = HEAD_DIM if kh == 0 else 0
            return rows * (1.0 / (rows[:, ones_col:ones_col + 1] + sink_e[h]))

        for g in range(GQA):
            yb_ref[0, pl.ds(r0, BLK), g * BLK:(g + 1) * BLK] = jnp.where(
                low_half, head_out(g), head_out(GQA + g))

    yb_ref[1] = yb_ref[0]


def _wout_copies(wout_hbm, wstage_ref, wsem_ref):
    copies = [pltpu.make_async_copy(wout_hbm.at[pl.ds(0, D_A), :], wstage_ref.at[pl.ds(0, D_A), :],
                                    wsem_ref)]
    for kv in range(N_KV):
        for g in range(GQA):
            src = D_A + (kv * GQA + g) * HEAD_DIM
            dst = D_A + (g * N_KV + kv) * HEAD_DIM
            copies.append(pltpu.make_async_copy(wout_hbm.at[pl.ds(src, HEAD_DIM), :],
                                                wstage_ref.at[pl.ds(dst, HEAD_DIM), :], wsem_ref))
    return copies


def _attn_out_body(*refs, seq_len):
    step = pl.program_id(0)
    n_tok_tiles = pl.num_programs(0) - 1

    @pl.when(step == 0)
    def _():
        wout_hbm, (bias_ref, wstage_ref, wbf_ref, wsem_ref) = refs[13], refs[-4:]
        copies = _wout_copies(wout_hbm, wstage_ref, wsem_ref)
        for cp in copies:
            cp.start()
        _bias_body(refs[10], refs[11], bias_ref)
        _attn_attention(*refs, seq_len=seq_len)
        for cp in copies:
            cp.wait()
        for r0 in range(0, D_MODEL, BLK):
            wbf_ref[r0:r0 + BLK, :] = wstage_ref[r0:r0 + BLK, :].astype(bf16)

    @pl.when((step > 0) & (step < n_tok_tiles))
    def _():
        _attn_epilogue(*refs, seq_len=seq_len)
        _attn_attention(*refs, seq_len=seq_len)

    @pl.when(step == n_tok_tiles)
    def _():
        _attn_epilogue(*refs, seq_len=seq_len)


def _attn_out(sink, x, ya, q, k, v, rel_bias, goutb, wout, gffn, wr, br):
    b, s, _ = x.shape
    tq = TQ_ATT
    nb = tq // BLK
    nblk = s // BLK
    t = b * s
    nq = s // tq
    n_tok_tiles = b * nq
    att = lambda s: jnp.minimum(s, n_tok_tiles - 1)
    epi = lambda s: jnp.maximum(s - 1, 0)
    const = lambda shape: pl.BlockSpec(shape, lambda s, *_: (0,) * len(shape))
    tok_att = lambda w: pl.BlockSpec((1, tq, w), lambda s, *_: (att(s) // nq, att(s) % nq, 0))
    tok_epi = lambda w: pl.BlockSpec((1, tq, w), lambda s, *_: (epi(s) // nq, epi(s) % nq, 0))
    prev = pl.BlockSpec((1, BLK, BLK), lambda s, *_: (
        att(s) // nq, jnp.maximum((att(s) % nq) * nb - 1, 0), 0))
    nxt = pl.BlockSpec((1, BLK, BLK), lambda s, *_: (
        att(s) // nq, jnp.minimum((att(s) % nq) * nb + nb, nblk - 1), 0))
    rows = pl.BlockSpec((tq, D_MODEL), lambda s, *_: (epi(s), 0))
    lanes = lambda rows: pl.BlockSpec((rows, tq), lambda s, *_: (0, epi(s)))
    grid_spec = pltpu.PrefetchScalarGridSpec(
        num_scalar_prefetch=1,
        grid=(n_tok_tiles + 1,),
        in_specs=[tok_epi(D_MODEL), tok_epi(D_A), tok_att(D_B),
                  prev, tok_att(BLK), nxt, prev, tok_att(BLK), nxt,
                  pl.BlockSpec(memory_space=pltpu.SMEM), const((8, 4 * BLK)),
                  const((1, D_B)), pl.BlockSpec(memory_space=pl.ANY),
                  const((1, D_MODEL)), const((D_MODEL, BLK)), const((ROUTE_ROWS, tq))],
        out_specs=(tok_epi(D_MODEL), rows, lanes(8), lanes(8)),
        scratch_shapes=[pltpu.VMEM((tq + 2 * BLK, BLK), bf16),
                        pltpu.VMEM((N_KV, tq + 2 * BLK, BLK), bf16),
                        pltpu.VMEM((nb, N_HEADS * BLK, 3 * BLK), bf16),
                        pltpu.VMEM((2, tq, D_B), f32),
                        pltpu.VMEM((N_HEADS, BLK, 3 * BLK), f32),
                        pltpu.VMEM((D_MODEL, D_MODEL), f32),
                        pltpu.VMEM((D_MODEL, D_MODEL), bf16),
                        pltpu.SemaphoreType.DMA(())])
    return pl.pallas_call(
        functools.partial(_attn_out_body, seq_len=s),
        grid_spec=grid_spec,
        out_shape=(jax.ShapeDtypeStruct((b, s, D_MODEL), f32),
                   jax.ShapeDtypeStruct((t, D_MODEL), bf16),
                   jax.ShapeDtypeStruct((8, t), i32),
                   jax.ShapeDtypeStruct((8, t), f32)),
        compiler_params=pltpu.CompilerParams(
            dimension_semantics=("arbitrary",), vmem_limit_bytes=VMEM_LIMIT),
        name="attn_out",
    )(sink, x, ya, q, k, k, k, v, v, v, rel_bias, jnp.asarray(_bucket_row()), goutb, wout, gffn, wr, br)


def _exclusive_prefix(vals):
    a = lax.broadcasted_iota(i32, (N_EXPERTS, N_EXPERTS), 0)
    c = lax.broadcasted_iota(i32, (N_EXPERTS, N_EXPERTS), 1)
    low = jnp.where(c < a, 1.0, 0.0).astype(bf16)
    hi = (vals >> 8).astype(f32).astype(bf16)
    lo = (vals & 255).astype(f32).astype(bf16)
    return (jnp.dot(low, hi, preferred_element_type=f32) * 256.0
            + jnp.dot(low, lo, preferred_element_type=f32)).astype(i32)


def _round_up_pow2(vals, mult):
    log_m = mult.bit_length() - 1
    return ((vals + (mult - 1)) >> log_m) << log_m


PLAN_TILE_EXP, PLAN_FIRST, PLAN_NEXT_EXP, PLAN_SLOT, PLAN_N_USED, PLAN_ENDS, PLAN_PADDED, PLAN_TOTAL = (
    range(8))


def _lane_row(vals, lane_off=0):
    sub = lax.broadcasted_iota(i32, vals.shape, 0)
    lane = lax.broadcasted_iota(i32, vals.shape, 1)
    return jnp.sum(jnp.where(lane == sub + lane_off, vals, 0), axis=0, keepdims=True)


def _route_pos_body(re_ref, slot_ref, runs_ref, plan_ref, tri_ref, *, rows_per_tile, tl):
    n_tok_tiles = re_ref.shape[1] // tl
    a = lax.broadcasted_iota(i32, (tl, tl), 0)
    c = lax.broadcasted_iota(i32, (tl, tl), 1)
    tri_ref[...] = jnp.where(a < c, 1.0, 0.0).astype(bf16)
    eid = lax.broadcasted_iota(i32, (N_EXPERTS, tl), 0)

    def tile_hits(j):
        cols = pl.ds(pl.multiple_of(j * tl, tl), tl)
        hit1 = eid == re_ref[0:1, cols]
        hit2 = eid == re_ref[1:2, cols]
        onehot = jnp.where(hit1 | hit2, 1.0, 0.0)
        cnt = jnp.sum(onehot, axis=1, keepdims=True).astype(i32)
        run_len = jnp.broadcast_to(_round_up_pow2(cnt, RUN_ALIGN), (N_EXPERTS, BLK))
        return cols, hit1, hit2, onehot, run_len

    total = lax.fori_loop(0, n_tok_tiles, lambda j, acc: acc + tile_hits(j)[4],
                          jnp.zeros((N_EXPERTS, BLK), i32), unroll=ROUTE_UNROLL)
    padded = _round_up_pow2(total, rows_per_tile)
    starts = _exclusive_prefix(padded)
    ends = starts + padded

    log_r = rows_per_tile.bit_length() - 1
    sub = lax.broadcasted_iota(i32, (N_EXPERTS, BLK), 0)
    lane = lax.broadcasted_iota(i32, (1, BLK), 1)
    n_used = ends[N_EXPERTS - 1:N_EXPERTS, :] >> log_r
    tile_idx = jnp.minimum(lane, n_used - 1)
    tile_exp = jnp.minimum(
        jnp.sum((ends <= (tile_idx << log_r)).astype(i32), axis=0, keepdims=True), N_EXPERTS - 1)
    nonempty = padded > 0
    regions_before = jnp.sum(((sub < tile_exp) & nonempty).astype(i32), axis=0, keepdims=True)
    nxt = jnp.min(jnp.where((sub > tile_exp) & nonempty, sub, N_EXPERTS), axis=0, keepdims=True)
    prev_exp = pltpu.roll(jnp.broadcast_to(tile_exp, (8, BLK)), 1, 1)[0:1]
    plan_ref[PLAN_TILE_EXP:PLAN_TILE_EXP + 1] = tile_exp
    plan_ref[PLAN_FIRST:PLAN_FIRST + 1] = ((lane == 0) | (tile_exp != prev_exp)).astype(i32)
    plan_ref[PLAN_NEXT_EXP:PLAN_NEXT_EXP + 1] = jnp.where(nxt < N_EXPERTS, nxt, -1)
    plan_ref[PLAN_SLOT:PLAN_SLOT + 1] = regions_before & 1
    plan_ref[PLAN_N_USED:PLAN_N_USED + 1] = n_used
    plan_ref[PLAN_ENDS:PLAN_ENDS + 1] = _lane_row(ends)
    plan_ref[PLAN_PADDED:PLAN_PADDED + 1] = _lane_row(padded)
    plan_ref[PLAN_TOTAL:PLAN_TOTAL + 1] = _lane_row(total)

    def place(j, global_off):
        cols, hit1, hit2, onehot, run_len = tile_hits(j)
        local_off = _exclusive_prefix(run_len)
        before = jnp.dot(onehot.astype(bf16), tri_ref[...], preferred_element_type=f32)
        slot = before + local_off[:, 0:1].astype(f32)
        s1 = jnp.sum(jnp.where(hit1, slot, 0.0), axis=0, keepdims=True)
        s2 = jnp.sum(jnp.where(hit2, slot, 0.0), axis=0, keepdims=True)
        sub = lax.broadcasted_iota(i32, (8, tl), 0)
        slot_ref[:, cols] = jnp.where(sub == 0, s1.astype(i32),
                                      jnp.where(sub == 1, s2.astype(i32), 0))
        runs_ref[j] = (_lane_row(local_off) + _lane_row(run_len, N_EXPERTS)
                       + _lane_row(global_off, 2 * N_EXPERTS))
        return global_off + run_len

    lax.fori_loop(0, n_tok_tiles, place, starts, unroll=ROUTE_UNROLL)


def _route_pos(route_e, rows_per_tile):
    t = route_e.shape[1]
    tl = TL_SORT
    return pl.pallas_call(
        functools.partial(_route_pos_body, rows_per_tile=rows_per_tile, tl=tl),
        out_shape=(jax.ShapeDtypeStruct((8, t), i32),
                   jax.ShapeDtypeStruct((t // tl, 1, BLK), i32),
                   jax.ShapeDtypeStruct((8, BLK), i32)),
        scratch_shapes=[pltpu.VMEM((tl, tl), bf16)],
        compiler_params=pltpu.CompilerParams(vmem_limit_bytes=VMEM_LIMIT),
        name="route_pos",
    )(route_e)


def _for_each_run_piece(runs_ref, fn):
    def per_expert(e, carry):
        local_off = runs_ref[0, 0, e]
        length = runs_ref[0, 0, N_EXPERTS + e]
        global_off = runs_ref[0, 0, 2 * N_EXPERTS + e]
        units = length >> (RUN_ALIGN.bit_length() - 1)

        def big_piece(k, inner):
            done = k * BIG_PIECE
            fn(pl.multiple_of(local_off + done, RUN_ALIGN),
               pl.multiple_of(global_off + done, RUN_ALIGN), BIG_PIECE)
            return inner

        lax.fori_loop(0, units >> SMALL_BITS, big_piece, 0)
        for b in range(SMALL_BITS):
            @pl.when(((units >> b) & 1) == 1)
            def _():
                done = ((units >> (b + 1)) << (b + 1)) * RUN_ALIGN
                fn(pl.multiple_of(local_off + done, RUN_ALIGN),
                   pl.multiple_of(global_off + done, RUN_ALIGN), RUN_ALIGN << b)
        return carry

    lax.fori_loop(0, N_EXPERTS, per_expert, 0)


def _tile_run_rows(runs_ref):
    return lax.fori_loop(0, N_EXPERTS, lambda e, acc: acc + runs_ref[0, 0, N_EXPERTS + e], 0)


def _wait_rows(total_rows, make_copy):
    units = total_rows >> (RUN_ALIGN.bit_length() - 1)
    for b in range((LB_SORT // RUN_ALIGN).bit_length()):
        @pl.when(((units >> b) & 1) == 1)
        def _():
            make_copy(RUN_ALIGN << b).wait()


def _sort_rows_body(plan_ref, runs_ref, m_ref, slot_ref, rw_ref,
                    xs_ref, slot_t_ref,
                    local_ref, zero_ref, pending_ref, sem_ref, zsem_ref, *, rows_per_tile):
    tl = m_ref.shape[0]
    lb = local_ref.shape[1]
    step = pl.program_id(0)
    cur = step % 2
    n_tiles = xs_ref.shape[0] // rows_per_tile
    n_used = plan_ref[PLAN_N_USED, 0]

    def zero_rows(start, rows, wait):
        cp = pltpu.make_async_copy(zero_ref.at[pl.ds(0, rows), :],
                                   xs_ref.at[pl.ds(pl.multiple_of(start, RUN_ALIGN), rows), :],
                                   zsem_ref)
        cp.wait() if wait else cp.start()

    def zero_fill(wait):
        for e in range(N_EXPERTS):
            pad_rows = plan_ref[PLAN_PADDED, e] - plan_ref[PLAN_TOTAL, e]
            pad_units = pad_rows >> (RUN_ALIGN.bit_length() - 1)
            for bit in range((rows_per_tile // RUN_ALIGN).bit_length() - 1):
                @pl.when(((pad_units >> bit) & 1) == 1)
                def _():
                    done = ((pad_units >> (bit + 1)) << (bit + 1)) * RUN_ALIGN
                    zero_rows(plan_ref[PLAN_ENDS, e] - pad_rows + done,
                              RUN_ALIGN << bit, wait)

        def spare(j, carry):
            zero_rows((n_used + j) * rows_per_tile, rows_per_tile, wait)
            return carry

        lax.fori_loop(0, n_tiles - n_used, spare, 0)

    @pl.when(step == 0)
    def _():
        zero_ref[...] = jnp.zeros(zero_ref.shape, bf16)
        zero_fill(wait=False)

    s1 = slot_ref[0:1, :]
    s2 = slot_ref[1:2, :]
    w1 = rw_ref[0:1, :]
    w2 = rw_ref[1:2, :]
    run_rows = _tile_run_rows(runs_ref)

    def sort_rows_upto(n):
        srow = lax.broadcasted_iota(i32, (n, tl), 0)
        p1 = srow == s1
        p2 = srow == s2
        perm = jnp.where(p1 | p2, 1.0, 0.0).astype(bf16)
        rows = jnp.dot(perm, m_ref[...], preferred_element_type=f32)
        w = jnp.sum(jnp.where(p1, w1, 0.0) + jnp.where(p2, w2, 0.0), axis=1, keepdims=True)
        w_hi = w.astype(bf16).astype(f32)
        w_lo = w - w_hi
        lane = lax.broadcasted_iota(i32, (n, BLK), 1)
        local_ref[cur, :n, :D_MODEL] = rows.astype(bf16)
        local_ref[cur, :n, D_MODEL:] = jnp.where(
            lane == 0, w_hi, jnp.where(lane == 1, w_lo, 0.0)).astype(bf16)
        sub_t = lax.broadcasted_iota(i32, (BLK, tl), 0)
        slots = jnp.where(sub_t == 0, s1.astype(f32), jnp.where(sub_t == 1, s2.astype(f32), 0.0))
        slot_t_ref[...] = jnp.transpose(slots)[:, :8].astype(i32)

    def piece(s, local_row, global_row, rows):
        return pltpu.make_async_copy(local_ref.at[s, pl.ds(local_row, rows), :],
                                     xs_ref.at[pl.ds(global_row, rows), :], sem_ref.at[s])

    short = lb - 256
    pl.when(run_rows <= short)(functools.partial(sort_rows_upto, short))
    pl.when(run_rows > short)(functools.partial(sort_rows_upto, lb))

    _for_each_run_piece(runs_ref, lambda l, g, n: piece(cur, l, g, n).start())

    @pl.when(step > 0)
    def _():
        _wait_rows(pending_ref[0], lambda n: piece(1 - cur, 0, 0, n))

    pending_ref[0] = run_rows

    @pl.when(step == pl.num_programs(0) - 1)
    def _():
        _wait_rows(pending_ref[0], lambda n: piece(cur, 0, 0, n))
        zero_fill(wait=True)


def _sort_rows(plan, run_tiles, m, slot, route_w, total_rows, rows_per_tile):
    t = m.shape[0]
    tl = TL_SORT
    grid_spec = pltpu.PrefetchScalarGridSpec(
        num_scalar_prefetch=1,
        grid=(t // tl,),
        in_specs=[pl.BlockSpec((1, 1, BLK), lambda i, *_: (i, 0, 0),
                               memory_space=pltpu.SMEM),
                  pl.BlockSpec((tl, D_MODEL), lambda i, *_: (i, 0)),
                  pl.BlockSpec((8, tl), lambda i, *_: (0, i)),
                  pl.BlockSpec((8, tl), lambda i, *_: (0, i))],
        out_specs=(pl.BlockSpec(memory_space=pl.ANY),
                   pl.BlockSpec((tl, 8), lambda i, *_: (i, 0))),
        scratch_shapes=[pltpu.VMEM((2, LB_SORT, XS_COLS), bf16),
                        pltpu.VMEM((rows_per_tile, XS_COLS), bf16),
                        pltpu.SMEM((1,), i32),
                        pltpu.SemaphoreType.DMA((2,)),
                        pltpu.SemaphoreType.DMA(())])
    return pl.pallas_call(
        functools.partial(_sort_rows_body, rows_per_tile=rows_per_tile),
        grid_spec=grid_spec,
        out_shape=(jax.ShapeDtypeStruct((total_rows, XS_COLS), bf16),
                   jax.ShapeDtypeStruct((t, 8), i32)),
        compiler_params=pltpu.CompilerParams(
            dimension_semantics=("arbitrary",), vmem_limit_bytes=VMEM_LIMIT),
        name="sort_rows",
    )(plan, run_tiles, m, slot, route_w)


def _experts_body(plan_ref,
                  xs_hbm, wg_hbm, wu_hbm, wd_hbm, ys_hbm,
                  xbuf_ref, sg_ref, su_ref, sd_ref, wg_ref, wu_ref, wd_ref,
                  xsem_ref, osem_ref, sem_ref):
    i = pl.program_id(0)
    n_used = plan_ref[PLAN_N_USED, 0]
    r = xbuf_ref.shape[1]

    def tile_rows(ref, tile):
        return ref.at[pl.ds(pl.multiple_of(tile * r, r), r), :]

    def fetch_tile(tile):
        s = tile % XS_RING
        return pltpu.make_async_copy(tile_rows(xs_hbm, tile), xbuf_ref.at[s], xsem_ref.at[s])

    def store_tile(tile):
        s = tile % XS_RING
        return pltpu.make_async_copy(xbuf_ref.at[s], tile_rows(ys_hbm, tile), osem_ref.at[s])

    @pl.when(i == 0)
    def _():
        for ahead in range(XS_AHEAD):
            @pl.when(ahead < n_used)
            def _():
                fetch_tile(ahead).start()

    @pl.when(i + XS_AHEAD < n_used)
    def _():
        @pl.when(i + XS_AHEAD >= XS_RING)
        def _():
            store_tile(i + XS_AHEAD - XS_RING).wait()

        fetch_tile(i + XS_AHEAD).start()

    def fetches(expert, s):
        return (pltpu.make_async_copy(wg_hbm.at[expert], sg_ref.at[s], sem_ref.at[s]),
                pltpu.make_async_copy(wu_hbm.at[expert], su_ref.at[s], sem_ref.at[s]),
                pltpu.make_async_copy(wd_hbm.at[expert], sd_ref.at[s], sem_ref.at[s]))

    @pl.when(i < n_used)
    def _():
        s = plan_ref[PLAN_SLOT, i]
        fetch_tile(i).wait()
        xs_ref = xbuf_ref.at[i % XS_RING]

        @pl.when(plan_ref[PLAN_FIRST, i] == 1)
        def _():
            @pl.when(i == 0)
            def _():
                for cp in fetches(plan_ref[PLAN_TILE_EXP, 0], 0):
                    cp.start()

            for cp in fetches(plan_ref[PLAN_TILE_EXP, i], s):
                cp.wait()

            @pl.when(plan_ref[PLAN_NEXT_EXP, i] >= 0)
            def _():
                for cp in fetches(plan_ref[PLAN_NEXT_EXP, i], 1 - s):
                    cp.start()

            wg_ref[...] = sg_ref[s].astype(bf16)
            wu_ref[...] = su_ref[s].astype(bf16)
            wd_ref[...] = sd_ref[s].astype(bf16)

        w_row = (xs_ref[:, D_MODEL:D_MODEL + 1].astype(f32)
                 + xs_ref[:, D_MODEL + 1:D_MODEL + 2].astype(f32))
        gate = jnp.dot(xs_ref[:, :D_MODEL], wg_ref[...], preferred_element_type=f32)
        up = jnp.dot(xs_ref[:, :D_MODEL], wu_ref[...], preferred_element_type=f32)
        hdn = (gate * jax.nn.sigmoid(gate)) * up
        y = jnp.dot(hdn.astype(bf16), wd_ref[...], preferred_element_type=f32)
        xs_ref[:, :D_MODEL] = (y * w_row).astype(bf16)
        store_tile(i).start()

    @pl.when(i == pl.num_programs(0) - 1)
    def _():
        for back in range(1, XS_RING + 1):
            @pl.when(n_used - back >= 0)
            def _():
                store_tile(n_used - back).wait()


def _experts(plan, xs, wg, wu, wd, rows_per_tile):
    r = rows_per_tile
    n_tiles = xs.shape[0] // r
    assert n_tiles <= BLK, "the plan holds one row tile per lane"
    grid_spec = pltpu.PrefetchScalarGridSpec(
        num_scalar_prefetch=1,
        grid=(n_tiles,),
        in_specs=[pl.BlockSpec(memory_space=pl.ANY),
                  pl.BlockSpec(memory_space=pl.ANY),
                  pl.BlockSpec(memory_space=pl.ANY),
                  pl.BlockSpec(memory_space=pl.ANY)],
        out_specs=pl.BlockSpec(memory_space=pl.ANY),
        scratch_shapes=[pltpu.VMEM((XS_RING, r, XS_COLS), bf16),
                        pltpu.VMEM((2, D_MODEL, D_FF_E), f32),
                        pltpu.VMEM((2, D_MODEL, D_FF_E), f32),
                        pltpu.VMEM((2, D_FF_E, D_MODEL), f32),
                        pltpu.VMEM((D_MODEL, D_FF_E), bf16),
                        pltpu.VMEM((D_MODEL, D_FF_E), bf16),
                        pltpu.VMEM((D_FF_E, D_MODEL), bf16),
                        pltpu.SemaphoreType.DMA((XS_RING,)),
                        pltpu.SemaphoreType.DMA((XS_RING,)),
                        pltpu.SemaphoreType.DMA((2,))])
    return pl.pallas_call(
        _experts_body,
        grid_spec=grid_spec,
        out_shape=jax.ShapeDtypeStruct(xs.shape, xs.dtype),
        input_output_aliases={1: 0},
        compiler_params=pltpu.CompilerParams(
            dimension_semantics=("arbitrary",), vmem_limit_bytes=VMEM_LIMIT),
        name="experts",
    )(plan, xs, wg, wu, wd)


def _combine_out_body(runs_cur_ref, runs_nxt_ref, h_ref, slot_ref, p_ref, ys_ref,
                      wpg_ref, bpg_ref, wpp_ref, gple_ref, gfin_ref,
                      o_ref, ybuf_ref, moe_ref, sem_ref):
    tk = h_ref.shape[0]
    lb = ybuf_ref.shape[1]
    step = pl.program_id(0)
    n = pl.num_programs(0) - 1
    cur = jnp.minimum(step, n - 1) % 2

    def piece(s, local_row, global_row, rows):
        return pltpu.make_async_copy(ys_ref.at[pl.ds(global_row, rows), pl.ds(0, D_MODEL)],
                                     ybuf_ref.at[s, pl.ds(local_row, rows), :], sem_ref.at[s])

    @pl.when(step == 0)
    def _():
        ybuf_ref[...] = jnp.zeros(ybuf_ref.shape, bf16)
        _for_each_run_piece(runs_cur_ref, lambda l, g, r: piece(0, l, g, r).start())

    @pl.when(step + 1 < n)
    def _():
        _for_each_run_piece(runs_nxt_ref, lambda l, g, r: piece(1 - cur, l, g, r).start())

    run_rows = _tile_run_rows(runs_cur_ref)

    @pl.when(step < n)
    def _():
        _wait_rows(run_rows, lambda r: piece(cur, 0, 0, r))

    def work(rows, finish, unsort):
        if finish:
            h2 = h_ref[...] + moe_ref[...]
            gate = jax.nn.sigmoid(jnp.dot(h2.astype(bf16), wpg_ref[...], preferred_element_type=f32)
                                  + bpg_ref[...])
            pp = jnp.dot(p_ref[...].astype(bf16), wpp_ref[...], preferred_element_type=f32)
            h3 = h2 + gate * _rms(pp, gple_ref[...])
            o_ref[...] = _rms(h3, gfin_ref[...])
        if unsort:
            col = lax.broadcasted_iota(i32, (tk, rows), 1)
            unperm = jnp.where((col == slot_ref[:, 0:1]) | (col == slot_ref[:, 1:2]), 1.0, 0.0)
            moe_ref[...] = jnp.dot(unperm.astype(bf16), ybuf_ref[cur, :rows, :],
                                   preferred_element_type=f32)

    short = lb - 256
    is_short = run_rows <= short
    first = step == 0
    mid = (step > 0) & (step < n)
    pl.when(first & is_short)(functools.partial(work, short, False, True))
    pl.when(first & jnp.logical_not(is_short))(functools.partial(work, lb, False, True))
    pl.when(mid & is_short)(functools.partial(work, short, True, True))
    pl.when(mid & jnp.logical_not(is_short))(functools.partial(work, lb, True, True))
    pl.when(step == n)(functools.partial(work, lb, True, False))


def _combine_out(run_tiles, h2d, slot_t, p2d, ys, wpg, bpg, wpp, gple, gfin):
    t = h2d.shape[0]
    tk = TL_SORT
    nt = t // tk
    const = lambda shape: pl.BlockSpec(shape, lambda i: (0,) * len(shape))
    runs = lambda imap: pl.BlockSpec((1, 1, BLK), imap, memory_space=pltpu.SMEM)
    uns = lambda s: jnp.minimum(s, nt - 1)
    fin = lambda s: jnp.maximum(s - 1, 0)
    return pl.pallas_call(
        _combine_out_body,
        grid=(nt + 1,),
        out_shape=jax.ShapeDtypeStruct((t, D_MODEL), f32),
        in_specs=[runs(lambda s: (uns(s), 0, 0)),
                  runs(lambda s: (uns(s + 1), 0, 0)),
                  pl.BlockSpec((tk, D_MODEL), lambda s: (fin(s), 0)),
                  pl.BlockSpec((tk, 8), lambda s: (uns(s), 0)),
                  pl.BlockSpec((tk, PLE_DIM), lambda s: (fin(s), 0)),
                  pl.BlockSpec(memory_space=pl.ANY),
                  const((D_MODEL, D_MODEL)), const((1, D_MODEL)),
                  const((PLE_DIM, D_MODEL)), const((1, D_MODEL)), const((1, D_MODEL))],
        out_specs=pl.BlockSpec((tk, D_MODEL), lambda s: (fin(s), 0)),
        scratch_shapes=[pltpu.VMEM((2, LB_SORT, D_MODEL), bf16),
                        pltpu.VMEM((tk, D_MODEL), f32),
                        pltpu.SemaphoreType.DMA((2,))],
        compiler_params=pltpu.CompilerParams(
            dimension_semantics=("arbitrary",), vmem_limit_bytes=VMEM_LIMIT),
        name="combine_out",
    )(run_tiles, run_tiles, h2d, slot_t, p2d, ys, wpg, bpg, wpp, gple, gfin)


def _head_major_to_tile_major(a, axis):
    axis = axis % a.ndim
    shape = a.shape
    a = a.reshape(shape[:axis] + (N_KV, GQA, HEAD_DIM) + shape[axis + 1:])
    return jnp.swapaxes(a, axis, axis + 1).reshape(shape)


def kernel(x, p, rel_bias, g_mix, w_in, ln_v_g, ln_v_b, w_spatial, b_spatial, sink, g_out_grp, w_out,
           g_ffn, w_router_group, b_router_group, w_router_expert, b_router_expert, w_gate_e, w_up_e,
           w_down_e, w_ple_proj, g_ple, w_ple_gate, b_ple_gate, g_final):
    b, s, d = x.shape
    t = b * s
    depth = g_mix.shape[0]
    assert depth == 1 and d == D_MODEL
    c1, c2 = 2 * D_A, 2 * D_A + D_B
    bias = (rel_bias * LOG2E).astype(f32)
    hcur = x.astype(f32)
    for li in range(depth):
        col_scale = np.ones((1, D_IN), np.float32)
        col_scale[:, c1:c2] = LOG2E * HEAD_DIM ** -0.5
        win = (w_in[li] * col_scale).astype(bf16)
        bs = jnp.broadcast_to(b_spatial[li][:, :, None], (A_HEADS, BLK, BLK)).astype(f32)
        gout = g_out_grp[li]
        gout_b = _head_major_to_tile_major(gout[D_A:], 0)[None]
        wout = w_out.reshape(D_MODEL, D_MODEL)
        wr = jnp.concatenate([
            w_router_group[li], jnp.zeros((D_MODEL, 8 - N_GROUPS), f32), w_router_expert[li],
            jnp.zeros((D_MODEL, BLK - ROUTE_ROWS), f32)], axis=1).astype(bf16)
        br = jnp.concatenate([b_router_group[li], jnp.full((8 - N_GROUPS,), NEG, f32),
                              b_router_expert[li]])
        br = jnp.broadcast_to(br[:, None], (ROUTE_ROWS, TQ_ATT))

        ya, q, k, v = _mix_in(hcur.reshape(t, d), g_mix[li][None], win, ln_v_g[li][None],
                              ln_v_b[li][None], w_spatial[li].astype(bf16), bs, gout[None, :D_A])
        h, m_rows, route_e, route_w = _attn_out(
            sink[li].astype(f32) * LOG2E, hcur, ya.reshape(b, s, D_A), q.reshape(b, s, D_B),
            k.reshape(b, s, BLK), v.reshape(b, s, BLK), bias, gout_b, wout,
            g_ffn[li][None], wr, br)

        r = R_EXP
        n_sort = t // TL_SORT
        max_rows = 2 * t + n_sort * N_EXPERTS * (RUN_ALIGN - 1)
        n_tiles = -(-max_rows // r) + N_EXPERTS
        assert n_tiles * r < 2 ** 16, "route_pos prefix sums hold row offsets in two bf16 bytes"
        slot, run_tiles, plan = _route_pos(route_e, r)
        xs, slot_t = _sort_rows(plan, run_tiles, m_rows, slot, route_w, n_tiles * r, r)
        ys = _experts(plan, xs, w_gate_e[li], w_up_e[li], w_down_e[li], r)

        out = _combine_out(run_tiles, h.reshape(t, d), slot_t,
                           p[li].reshape(t, PLE_DIM), ys, w_ple_gate[li].astype(bf16),
                           b_ple_gate[li][None], w_ple_proj[li].astype(bf16), g_ple[li][None],
                           g_final[None])
        hcur = out.reshape(b, s, d)
    return hcur
```

```python
import functools
import math

import jax
import jax.numpy as jnp
import numpy as np
from jax import lax
from jax.experimental import pallas as pl
from jax.experimental.pallas import tpu as pltpu

D_MODEL = 1024
D_A = 512
D_B = 512
BLK = 128
A_HEADS = 4
HEAD_DIM = 64
N_HEADS = 8
N_KV = 2
GQA = 4
WINDOW = 128
NUM_BUCKETS = 32
MAX_DIST = 128
D_IN = 2 * D_A + D_B + 2 * N_KV * HEAD_DIM
N_GROUPS = 4
E_PER_GROUP = 8
N_EXPERTS = 32
D_FF_E = 256
PLE_DIM = 256
EPS = 1e-6
NEG = -1e30
LOG2E = math.log2(math.e)

TM_MIX = 512
TQ_ATT = 512
TL_SORT = 512
RUN_ALIGN = 16
SMALL_BITS = 2
BIG_PIECE = RUN_ALIGN << SMALL_BITS
LB_SORT = 2 * TL_SORT + N_EXPERTS * RUN_ALIGN
R_EXP = 512
XS_COLS = D_MODEL + BLK
XS_RING = 4
XS_AHEAD = 2
ROUTE_ROWS = 8 + N_EXPERTS
ROUTE_UNROLL = 8
VMEM_LIMIT = 48 * 1024 * 1024

f32 = jnp.float32
bf16 = jnp.bfloat16
i32 = jnp.int32


def _rms(x, g):
    return x * lax.rsqrt(jnp.mean(x * x, axis=-1, keepdims=True) + EPS) * g


def _gelu_tanh(x):
    c = math.sqrt(2.0 / math.pi)
    return x * (0.5 * (1.0 + jnp.tanh(c * (x + 0.044715 * (x * x * x)))))


def _bucket_row():
    n = NUM_BUCKETS // 2
    max_exact = n // 2
    rel = np.broadcast_to(np.arange(4 * BLK)[None, :] - BLK, (8, 4 * BLK))
    ret = np.where(rel > 0, n, 0)
    a = np.abs(rel)
    large = max_exact + (np.log(np.maximum(a, 1).astype(np.float64) / max_exact)
                         / math.log(MAX_DIST / max_exact) * (n - max_exact)).astype(np.int32)
    large = np.minimum(large, n - 1)
    bucket = ret + np.where(a < max_exact, a, large)
    return np.where(a <= WINDOW, bucket, -1).astype(np.int32)


def _bias_body(rb_ref, bucket_ref, o_ref):
    bucket = bucket_ref[...]
    row = lax.broadcasted_iota(i32, (BLK, BLK), 0)
    lane = lax.broadcasted_iota(i32, (BLK, BLK), 1)
    for h in range(N_HEADS):
        base = jnp.full(bucket.shape, NEG, f32)
        for b in range(NUM_BUCKETS):
            base = jnp.where(bucket == b, rb_ref[b, h], base)
        prev = jnp.full((BLK, BLK), NEG, f32)
        for c in range(3):
            tile = jnp.broadcast_to(base[0:1, c * BLK:(c + 1) * BLK], (BLK, BLK))
            rolled = pltpu.roll(tile, 0, 1, stride=1, stride_axis=0)
            o_ref[h, :, c * BLK:(c + 1) * BLK] = jnp.where(lane >= row, rolled, prev)
            prev = rolled


def _mix_in_body(x_ref, gmix_ref, win_ref, lng_ref, lnb_ref, ws_ref, bs_ref, gout_ref,
                 ya_ref, q_ref, k_ref, v_ref):
    tm = x_ref.shape[0]
    nc = tm // BLK
    a = _rms(x_ref[...], gmix_ref[...])
    z = jnp.dot(a.astype(bf16), win_ref[...], preferred_element_type=f32)
    uv = _gelu_tanh(z[:, :2 * D_A])
    u = uv[:, :D_A]
    v = uv[:, D_A:]
    mu = jnp.mean(v, axis=-1, keepdims=True)
    vc = v - mu
    var = jnp.mean(vc * vc, axis=-1, keepdims=True)
    vn = (vc * lax.rsqrt(var + EPS) * lng_ref[...] + lnb_ref[...]).astype(bf16)
    cols = []
    for h in range(A_HEADS):
        rhs = jnp.concatenate(
            [vn[c * BLK:(c + 1) * BLK, h * BLK:(h + 1) * BLK] for c in range(nc)], axis=1)
        r = jnp.dot(ws_ref[h], rhs, preferred_element_type=f32)
        cols.append(jnp.concatenate(
            [r[:, c * BLK:(c + 1) * BLK] + bs_ref[h] for c in range(nc)], axis=0))
    sv = jnp.concatenate(cols, axis=1)
    ya_ref[...] = _rms(u * sv, gout_ref[...]).astype(bf16)
    qt = [z[:, 2 * D_A + j * BLK:2 * D_A + (j + 1) * BLK] for j in range(D_B // BLK)]
    low_half = lax.broadcasted_iota(i32, (tm, BLK), 1) < HEAD_DIM
    for g in range(GQA):
        a0, a1 = qt[g // 2], qt[GQA // 2 + g // 2]
        tile = (jnp.where(low_half, a0, pltpu.roll(a1, HEAD_DIM, 1)) if g % 2 == 0
                else jnp.where(low_half, pltpu.roll(a0, HEAD_DIM, 1), a1))
        q_ref[:, g * BLK:(g + 1) * BLK] = tile.astype(bf16)
    k_ref[...] = z[:, 2 * D_A + D_B:2 * D_A + D_B + BLK].astype(bf16)
    v_ref[...] = z[:, 2 * D_A + D_B + BLK:].astype(bf16)


def _mix_in(x2, gmix, win, lng, lnb, ws, bs, gout_a):
    t = x2.shape[0]
    tm = TM_MIX
    const = lambda shape: pl.BlockSpec(shape, lambda i: (0,) * len(shape))
    return pl.pallas_call(
        _mix_in_body,
        grid=(t // tm,),
        out_shape=(jax.ShapeDtypeStruct((t, D_A), bf16),
                   jax.ShapeDtypeStruct((t, D_B), bf16),
                   jax.ShapeDtypeStruct((t, BLK), bf16),
                   jax.ShapeDtypeStruct((t, BLK), bf16)),
        in_specs=[pl.BlockSpec((tm, D_MODEL), lambda i: (i, 0)),
                  const((1, D_MODEL)), const((D_MODEL, D_IN)),
                  const((1, D_A)), const((1, D_A)),
                  const((A_HEADS, BLK, BLK)), const((A_HEADS, BLK, BLK)),
                  const((1, D_A))],
        out_specs=(pl.BlockSpec((tm, D_A), lambda i: (i, 0)),
                   pl.BlockSpec((tm, D_B), lambda i: (i, 0)),
                   pl.BlockSpec((tm, BLK), lambda i: (i, 0)),
                   pl.BlockSpec((tm, BLK), lambda i: (i, 0))),
        compiler_params=pltpu.CompilerParams(
            dimension_semantics=("parallel",), vmem_limit_bytes=VMEM_LIMIT),
        name="mix_in",
    )(x2, gmix, win, lng, lnb, ws, bs, gout_a)


def _attn_epilogue(sink_ref, x_ref, ya_ref, q_ref, kp_ref, km_ref, kn_ref, vp_ref, vm_ref, vn_ref,
                   rb_ref, bucket_ref, goutb_ref, wout_ref, gffn_ref, wr_ref, br_ref,
                   h_ref, m_ref, re_ref, rw_ref,
                   kf_ref, vf_ref, e_ref, yb_ref, bias_ref, *, seq_len):
    tq = x_ref.shape[1]
    ybn = _rms(yb_ref[1], goutb_ref[...]).astype(bf16)
    y = jnp.concatenate([ya_ref[0], ybn], axis=1)
    h = x_ref[0] + jnp.dot(y, wout_ref[...], preferred_element_type=f32)
    h_ref[0] = h
    m = _rms(h, gffn_ref[...])
    m_ref[...] = m.astype(bf16)

    logit_t = jnp.dot(m.astype(bf16), wr_ref[...], preferred_element_type=f32)
    logit = jnp.transpose(logit_t)[:ROUTE_ROWS] + br_ref[...]
    sub = lax.broadcasted_iota(i32, (8, tq), 0)
    lg = logit[0:8]
    mg = jnp.max(lg, axis=0, keepdims=True)
    pg_top = 1.0 / jnp.sum(jnp.exp(lg - mg), axis=0, keepdims=True)
    g_idx = jnp.min(jnp.where(lg == mg, sub, 8), axis=0, keepdims=True)
    sel = logit[8:16]
    for g in range(1, N_GROUPS):
        sel = jnp.where(g_idx == g, logit[8 + 8 * g:16 + 8 * g], sel)
    m1 = jnp.max(sel, axis=0, keepdims=True)
    i1 = jnp.min(jnp.where(sel == m1, sub, 8), axis=0, keepdims=True)
    sel2 = jnp.where(sub == i1, -jnp.inf, sel)
    m2 = jnp.max(sel2, axis=0, keepdims=True)
    i2 = jnp.min(jnp.where(sel2 == m2, sub, 8), axis=0, keepdims=True)
    r = jnp.exp(m2 - m1)
    w1 = pg_top / (1.0 + r)
    w2 = pg_top * r / (1.0 + r)
    e1 = g_idx * E_PER_GROUP + i1
    e2 = g_idx * E_PER_GROUP + i2
    re_ref[...] = jnp.where(sub == 0, e1, jnp.where(sub == 1, e2, 0))
    rw_ref[...] = jnp.where(sub == 0, w1, jnp.where(sub == 1, w2, 0.0))


def _attn_attention(sink_ref, x_ref, ya_ref, q_ref, kp_ref, km_ref, kn_ref, vp_ref, vm_ref, vn_ref,
                    rb_ref, bucket_ref, goutb_ref, wout_ref, gffn_ref, wr_ref, br_ref,
                    h_ref, m_ref, re_ref, rw_ref,
                    kf_ref, vf_ref, e_ref, yb_ref, bias_ref, *, seq_len):
    tq = x_ref.shape[1]
    nb = tq // BLK
    ti = pl.program_id(0) % (seq_len // tq)
    kf_ref[0:BLK] = kp_ref[0]
    kf_ref[BLK:BLK + tq] = km_ref[0]
    kf_ref[BLK + tq:] = kn_ref[0]
    vlane = lax.broadcasted_iota(i32, (BLK, BLK), 1)
    for src, r0, r1 in ((vp_ref, 0, BLK), (vn_ref, BLK + tq, tq + 2 * BLK)) + tuple(
            (vm_ref.at[:, pl.ds(c * BLK, BLK), :], BLK + c * BLK, BLK + (c + 1) * BLK)
            for c in range(nb)):
        vals = src[0].astype(f32)
        vf_ref[0, r0:r1] = jnp.where(vlane < HEAD_DIM, vals,
                                     jnp.where(vlane == HEAD_DIM, 1.0, 0.0)).astype(bf16)
        vf_ref[1, r0:r1] = jnp.where(vlane >= HEAD_DIM, vals,
                                     jnp.where(vlane == 0, 1.0, 0.0)).astype(bf16)

    lane = lax.broadcasted_iota(i32, (BLK, BLK), 1)
    low_half = lane < HEAD_DIM

    for n in range(nb):
        r0 = n * BLK
        qb = q_ref[0, pl.ds(r0, BLK), :]
        kb = kf_ref[pl.ds(r0, 3 * BLK), :]
        zero = jnp.zeros((BLK, BLK), bf16)
        lhs = []
        for kh in range(N_KV):
            for g in range(GQA):
                tile = qb[:, g * BLK:(g + 1) * BLK]
                lhs.append(jnp.where(low_half, tile, zero) if kh == 0
                           else jnp.where(low_half, zero, tile))
        lhs = jnp.concatenate(lhs, axis=0)
        s_all = lax.dot_general(lhs, kb, (((1,), (1,)), ((), ())),
                                preferred_element_type=f32)
        blk = ti * nb + n
        pen = [jnp.where(blk == 0, NEG, 0.0) if n == 0 else None, None,
               jnp.where((blk + 1) * BLK >= seq_len, NEG, 0.0) if n == nb - 1 else None]
        sink_e = []
        for h in range(N_HEADS):
            bias_h = bias_ref[h]
            bias_h = jnp.concatenate(
                [bias_h[:, j * BLK:(j + 1) * BLK] if pen[j] is None
                 else bias_h[:, j * BLK:(j + 1) * BLK] + pen[j] for j in range(3)], axis=1)
            s = s_all[h * BLK:(h + 1) * BLK] + bias_h
            sk = sink_ref[h]
            mrow = jnp.maximum(jnp.max(s, axis=-1, keepdims=True), sk)
            e_ref[n, h * BLK:(h + 1) * BLK, :] = jnp.exp2(s - mrow).astype(bf16)
            sink_e.append(jnp.exp2(sk - mrow))
        half = GQA * BLK
        pv = [jnp.dot(e_ref[n, kh * half:(kh + 1) * half, :], vf_ref[kh, pl.ds(r0, 3 * BLK), :],
                      preferred_element_type=f32) for kh in range(N_KV)]

        def head_out(h):
            kh, g = divmod(h, GQA)
            rows = pv[kh][g * BLK:(g + 1) * BLK]
            ones_col = HEAD_DIM if kh == 0 else 0
            return rows * (1.0 / (rows[:, ones_col:ones_col + 1] + sink_e[h]))

        for g in range(GQA):
            yb_ref[0, pl.ds(r0, BLK), g * BLK:(g + 1) * BLK] = jnp.where(
                low_half, head_out(g), head_out(GQA + g))

    yb_ref[1] = yb_ref[0]


def _attn_out_body(*refs, seq_len):
    step = pl.program_id(0)
    n_tok_tiles = pl.num_programs(0) - 1

    @pl.when(step == 0)
    def _():
        _bias_body(refs[10], refs[11], refs[-1])
        _attn_attention(*refs, seq_len=seq_len)

    @pl.when((step > 0) & (step < n_tok_tiles))
    def _():
        _attn_epilogue(*refs, seq_len=seq_len)
        _attn_attention(*refs, seq_len=seq_len)

    @pl.when(step == n_tok_tiles)
    def _():
        _attn_epilogue(*refs, seq_len=seq_len)


def _attn_out(sink, x, ya, q, k, v, rel_bias, goutb, wout, gffn, wr, br):
    b, s, _ = x.shape
    tq = TQ_ATT
    nb = tq // BLK
    nblk = s // BLK
    t = b * s
    nq = s // tq
    n_tok_tiles = b * nq
    att = lambda s: jnp.minimum(s, n_tok_tiles - 1)
    epi = lambda s: jnp.maximum(s - 1, 0)
    const = lambda shape: pl.BlockSpec(shape, lambda s, *_: (0,) * len(shape))
    tok_att = lambda w: pl.BlockSpec((1, tq, w), lambda s, *_: (att(s) // nq, att(s) % nq, 0))
    tok_epi = lambda w: pl.BlockSpec((1, tq, w), lambda s, *_: (epi(s) // nq, epi(s) % nq, 0))
    prev = pl.BlockSpec((1, BLK, BLK), lambda s, *_: (
        att(s) // nq, jnp.maximum((att(s) % nq) * nb - 1, 0), 0))
    nxt = pl.BlockSpec((1, BLK, BLK), lambda s, *_: (
        att(s) // nq, jnp.minimum((att(s) % nq) * nb + nb, nblk - 1), 0))
    rows = pl.BlockSpec((tq, D_MODEL), lambda s, *_: (epi(s), 0))
    lanes = lambda rows: pl.BlockSpec((rows, tq), lambda s, *_: (0, epi(s)))
    grid_spec = pltpu.PrefetchScalarGridSpec(
        num_scalar_prefetch=1,
        grid=(n_tok_tiles + 1,),
        in_specs=[tok_epi(D_MODEL), tok_epi(D_A), tok_att(D_B),
                  prev, tok_att(BLK), nxt, prev, tok_att(BLK), nxt,
                  pl.BlockSpec(memory_space=pltpu.SMEM), const((8, 4 * BLK)),
                  const((1, D_B)), const((D_MODEL, D_MODEL)),
                  const((1, D_MODEL)), const((D_MODEL, BLK)), const((ROUTE_ROWS, tq))],
        out_specs=(tok_epi(D_MODEL), rows, lanes(8), lanes(8)),
        scratch_shapes=[pltpu.VMEM((tq + 2 * BLK, BLK), bf16),
                        pltpu.VMEM((N_KV, tq + 2 * BLK, BLK), bf16),
                        pltpu.VMEM((nb, N_HEADS * BLK, 3 * BLK), bf16),
                        pltpu.VMEM((2, tq, D_B), f32),
                        pltpu.VMEM((N_HEADS, BLK, 3 * BLK), f32)])
    return pl.pallas_call(
        functools.partial(_attn_out_body, seq_len=s),
        grid_spec=grid_spec,
        out_shape=(jax.ShapeDtypeStruct((b, s, D_MODEL), f32),
                   jax.ShapeDtypeStruct((t, D_MODEL), bf16),
                   jax.ShapeDtypeStruct((8, t), i32),
                   jax.ShapeDtypeStruct((8, t), f32)),
        compiler_params=pltpu.CompilerParams(
            dimension_semantics=("arbitrary",), vmem_limit_bytes=VMEM_LIMIT),
        name="attn_out",
    )(sink, x, ya, q, k, k, k, v, v, v, rel_bias, jnp.asarray(_bucket_row()), goutb, wout, gffn, wr, br)


def _exclusive_prefix(vals):
    a = lax.broadcasted_iota(i32, (N_EXPERTS, N_EXPERTS), 0)
    c = lax.broadcasted_iota(i32, (N_EXPERTS, N_EXPERTS), 1)
    low = jnp.where(c < a, 1.0, 0.0).astype(bf16)
    hi = (vals >> 8).astype(f32).astype(bf16)
    lo = (vals & 255).astype(f32).astype(bf16)
    return (jnp.dot(low, hi, preferred_element_type=f32) * 256.0
            + jnp.dot(low, lo, preferred_element_type=f32)).astype(i32)


def _round_up_pow2(vals, mult):
    log_m = mult.bit_length() - 1
    return ((vals + (mult - 1)) >> log_m) << log_m


PLAN_TILE_EXP, PLAN_FIRST, PLAN_NEXT_EXP, PLAN_SLOT, PLAN_N_USED, PLAN_ENDS, PLAN_PADDED, PLAN_TOTAL = (
    range(8))


def _lane_row(vals, lane_off=0):
    sub = lax.broadcasted_iota(i32, vals.shape, 0)
    lane = lax.broadcasted_iota(i32, vals.shape, 1)
    return jnp.sum(jnp.where(lane == sub + lane_off, vals, 0), axis=0, keepdims=True)


def _route_pos_body(re_ref, slot_ref, runs_ref, plan_ref, tri_ref, *, rows_per_tile, tl):
    n_tok_tiles = re_ref.shape[1] // tl
    a = lax.broadcasted_iota(i32, (tl, tl), 0)
    c = lax.broadcasted_iota(i32, (tl, tl), 1)
    tri_ref[...] = jnp.where(a < c, 1.0, 0.0).astype(bf16)
    eid = lax.broadcasted_iota(i32, (N_EXPERTS, tl), 0)

    def tile_hits(j):
        cols = pl.ds(pl.multiple_of(j * tl, tl), tl)
        hit1 = eid == re_ref[0:1, cols]
        hit2 = eid == re_ref[1:2, cols]
        onehot = jnp.where(hit1 | hit2, 1.0, 0.0)
        cnt = jnp.sum(onehot, axis=1, keepdims=True).astype(i32)
        run_len = jnp.broadcast_to(_round_up_pow2(cnt, RUN_ALIGN), (N_EXPERTS, BLK))
        return cols, hit1, hit2, onehot, run_len

    total = lax.fori_loop(0, n_tok_tiles, lambda j, acc: acc + tile_hits(j)[4],
                          jnp.zeros((N_EXPERTS, BLK), i32), unroll=ROUTE_UNROLL)
    padded = _round_up_pow2(total, rows_per_tile)
    starts = _exclusive_prefix(padded)
    ends = starts + padded

    log_r = rows_per_tile.bit_length() - 1
    sub = lax.broadcasted_iota(i32, (N_EXPERTS, BLK), 0)
    lane = lax.broadcasted_iota(i32, (1, BLK), 1)
    n_used = ends[N_EXPERTS - 1:N_EXPERTS, :] >> log_r
    tile_idx = jnp.minimum(lane, n_used - 1)
    tile_exp = jnp.minimum(
        jnp.sum((ends <= (tile_idx << log_r)).astype(i32), axis=0, keepdims=True), N_EXPERTS - 1)
    nonempty = padded > 0
    regions_before = jnp.sum(((sub < tile_exp) & nonempty).astype(i32), axis=0, keepdims=True)
    nxt = jnp.min(jnp.where((sub > tile_exp) & nonempty, sub, N_EXPERTS), axis=0, keepdims=True)
    prev_exp = pltpu.roll(jnp.broadcast_to(tile_exp, (8, BLK)), 1, 1)[0:1]
    plan_ref[PLAN_TILE_EXP:PLAN_TILE_EXP + 1] = tile_exp
    plan_ref[PLAN_FIRST:PLAN_FIRST + 1] = ((lane == 0) | (tile_exp != prev_exp)).astype(i32)
    plan_ref[PLAN_NEXT_EXP:PLAN_NEXT_EXP + 1] = jnp.where(nxt < N_EXPERTS, nxt, -1)
    plan_ref[PLAN_SLOT:PLAN_SLOT + 1] = regions_before & 1
    plan_ref[PLAN_N_USED:PLAN_N_USED + 1] = n_used
    plan_ref[PLAN_ENDS:PLAN_ENDS + 1] = _lane_row(ends)
    plan_ref[PLAN_PADDED:PLAN_PADDED + 1] = _lane_row(padded)
    plan_ref[PLAN_TOTAL:PLAN_TOTAL + 1] = _lane_row(total)

    def place(j, global_off):
        cols, hit1, hit2, onehot, run_len = tile_hits(j)
        local_off = _exclusive_prefix(run_len)
        before = jnp.dot(onehot.astype(bf16), tri_ref[...], preferred_element_type=f32)
        slot = before + local_off[:, 0:1].astype(f32)
        s1 = jnp.sum(jnp.where(hit1, slot, 0.0), axis=0, keepdims=True)
        s2 = jnp.sum(jnp.where(hit2, slot, 0.0), axis=0, keepdims=True)
        sub = lax.broadcasted_iota(i32, (8, tl), 0)
        slot_ref[:, cols] = jnp.where(sub == 0, s1.astype(i32),
                                      jnp.where(sub == 1, s2.astype(i32), 0))
        tile_rows = jnp.sum(run_len, axis=0, keepdims=True)
        lane = lax.broadcasted_iota(i32, (1, BLK), 1)
        runs_ref[j] = (_lane_row(local_off) + _lane_row(run_len, N_EXPERTS)
                       + _lane_row(global_off, 2 * N_EXPERTS)
                       + jnp.where(lane == 3 * N_EXPERTS, tile_rows, 0))
        return global_off + run_len

    lax.fori_loop(0, n_tok_tiles, place, starts, unroll=ROUTE_UNROLL)


def _route_pos(route_e, rows_per_tile):
    t = route_e.shape[1]
    tl = TL_SORT
    return pl.pallas_call(
        functools.partial(_route_pos_body, rows_per_tile=rows_per_tile, tl=tl),
        out_shape=(jax.ShapeDtypeStruct((8, t), i32),
                   jax.ShapeDtypeStruct((t // tl, 1, BLK), i32),
                   jax.ShapeDtypeStruct((8, BLK), i32)),
        scratch_shapes=[pltpu.VMEM((tl, tl), bf16)],
        compiler_params=pltpu.CompilerParams(vmem_limit_bytes=VMEM_LIMIT),
        name="route_pos",
    )(route_e)


def _for_each_run_piece(runs_ref, fn):
    def per_expert(e, carry):
        local_off = runs_ref[0, 0, e]
        length = runs_ref[0, 0, N_EXPERTS + e]
        global_off = runs_ref[0, 0, 2 * N_EXPERTS + e]
        units = length >> (RUN_ALIGN.bit_length() - 1)

        def big_piece(k, inner):
            done = k * BIG_PIECE
            fn(pl.multiple_of(local_off + done, RUN_ALIGN),
               pl.multiple_of(global_off + done, RUN_ALIGN), BIG_PIECE)
            return inner

        lax.fori_loop(0, units >> SMALL_BITS, big_piece, 0)
        for b in range(SMALL_BITS):
            @pl.when(((units >> b) & 1) == 1)
            def _():
                done = ((units >> (b + 1)) << (b + 1)) * RUN_ALIGN
                fn(pl.multiple_of(local_off + done, RUN_ALIGN),
                   pl.multiple_of(global_off + done, RUN_ALIGN), RUN_ALIGN << b)
        return carry

    lax.fori_loop(0, N_EXPERTS, per_expert, 0)


def _tile_run_rows(runs_ref):
    return runs_ref[0, 0, 3 * N_EXPERTS]


def _wait_rows(total_rows, make_copy):
    units = total_rows >> (RUN_ALIGN.bit_length() - 1)
    for b in range((LB_SORT // RUN_ALIGN).bit_length()):
        @pl.when(((units >> b) & 1) == 1)
        def _():
            make_copy(RUN_ALIGN << b).wait()


def _sort_rows_body(plan_ref, runs_ref, m_ref, slot_ref, rw_ref,
                    xs_ref, slot_t_ref,
                    local_ref, zero_ref, pending_ref, sem_ref, zsem_ref, *, rows_per_tile):
    tl = m_ref.shape[0]
    lb = local_ref.shape[1]
    step = pl.program_id(0)
    cur = step % 2
    n_tiles = xs_ref.shape[0] // rows_per_tile
    n_used = plan_ref[PLAN_N_USED, 0]

    def zero_rows(start, rows, wait):
        cp = pltpu.make_async_copy(zero_ref.at[pl.ds(0, rows), :],
                                   xs_ref.at[pl.ds(pl.multiple_of(start, RUN_ALIGN), rows), :],
                                   zsem_ref)
        cp.wait() if wait else cp.start()

    def zero_fill(wait):
        for e in range(N_EXPERTS):
            pad_rows = plan_ref[PLAN_PADDED, e] - plan_ref[PLAN_TOTAL, e]
            pad_units = pad_rows >> (RUN_ALIGN.bit_length() - 1)
            for bit in range((rows_per_tile // RUN_ALIGN).bit_length() - 1):
                @pl.when(((pad_units >> bit) & 1) == 1)
                def _():
                    done = ((pad_units >> (bit + 1)) << (bit + 1)) * RUN_ALIGN
                    zero_rows(plan_ref[PLAN_ENDS, e] - pad_rows + done,
                              RUN_ALIGN << bit, wait)

        def spare(j, carry):
            zero_rows((n_used + j) * rows_per_tile, rows_per_tile, wait)
            return carry

        lax.fori_loop(0, n_tiles - n_used, spare, 0)

    @pl.when(step == 0)
    def _():
        zero_ref[...] = jnp.zeros(zero_ref.shape, bf16)
        zero_fill(wait=False)

    s1 = slot_ref[0:1, :]
    s2 = slot_ref[1:2, :]
    w1 = rw_ref[0:1, :]
    w2 = rw_ref[1:2, :]
    run_rows = _tile_run_rows(runs_ref)

    def sort_rows_upto(n):
        srow = lax.broadcasted_iota(i32, (n, tl), 0)
        p1 = srow == s1
        p2 = srow == s2
        perm = jnp.where(p1 | p2, 1.0, 0.0).astype(bf16)
        rows = jnp.dot(perm, m_ref[...], preferred_element_type=f32)
        w = jnp.sum(jnp.where(p1, w1, 0.0) + jnp.where(p2, w2, 0.0), axis=1, keepdims=True)
        w_hi = w.astype(bf16).astype(f32)
        w_lo = w - w_hi
        lane = lax.broadcasted_iota(i32, (n, BLK), 1)
        local_ref[cur, :n, :D_MODEL] = rows.astype(bf16)
        local_ref[cur, :n, D_MODEL:] = jnp.where(
            lane == 0, w_hi, jnp.where(lane == 1, w_lo, 0.0)).astype(bf16)
        sub_t = lax.broadcasted_iota(i32, (BLK, tl), 0)
        slots = jnp.where(sub_t == 0, s1.astype(f32), jnp.where(sub_t == 1, s2.astype(f32), 0.0))
        slot_t_ref[...] = jnp.transpose(slots)[:, :8].astype(i32)

    def piece(s, local_row, global_row, rows):
        return pltpu.make_async_copy(local_ref.at[s, pl.ds(local_row, rows), :],
                                     xs_ref.at[pl.ds(global_row, rows), :], sem_ref.at[s])

    short = lb - 256
    pl.when(run_rows <= short)(functools.partial(sort_rows_upto, short))
    pl.when(run_rows > short)(functools.partial(sort_rows_upto, lb))

    _for_each_run_piece(runs_ref, lambda l, g, n: piece(cur, l, g, n).start())

    @pl.when(step > 0)
    def _():
        _wait_rows(pending_ref[0], lambda n: piece(1 - cur, 0, 0, n))

    pending_ref[0] = run_rows

    @pl.when(step == pl.num_programs(0) - 1)
    def _():
        _wait_rows(pending_ref[0], lambda n: piece(cur, 0, 0, n))
        zero_fill(wait=True)


def _sort_rows(plan, run_tiles, m, slot, route_w, total_rows, rows_per_tile):
    t = m.shape[0]
    tl = TL_SORT
    grid_spec = pltpu.PrefetchScalarGridSpec(
        num_scalar_prefetch=1,
        grid=(t // tl,),
        in_specs=[pl.BlockSpec((1, 1, BLK), lambda i, *_: (i, 0, 0),
                               memory_space=pltpu.SMEM),
                  pl.BlockSpec((tl, D_MODEL), lambda i, *_: (i, 0)),
                  pl.BlockSpec((8, tl), lambda i, *_: (0, i)),
                  pl.BlockSpec((8, tl), lambda i, *_: (0, i))],
        out_specs=(pl.BlockSpec(memory_space=pl.ANY),
                   pl.BlockSpec((tl, 8), lambda i, *_: (i, 0))),
        scratch_shapes=[pltpu.VMEM((2, LB_SORT, XS_COLS), bf16),
                        pltpu.VMEM((rows_per_tile, XS_COLS), bf16),
                        pltpu.SMEM((1,), i32),
                        pltpu.SemaphoreType.DMA((2,)),
                        pltpu.SemaphoreType.DMA(())])
    return pl.pallas_call(
        functools.partial(_sort_rows_body, rows_per_tile=rows_per_tile),
        grid_spec=grid_spec,
        out_shape=(jax.ShapeDtypeStruct((total_rows, XS_COLS), bf16),
                   jax.ShapeDtypeStruct((t, 8), i32)),
        compiler_params=pltpu.CompilerParams(
            dimension_semantics=("arbitrary",), vmem_limit_bytes=VMEM_LIMIT),
        name="sort_rows",
    )(plan, run_tiles, m, slot, route_w)


def _experts_body(plan_ref,
                  xs_hbm, wg_hbm, wu_hbm, wd_hbm, ys_hbm,
                  xbuf_ref, sg_ref, su_ref, sd_ref, wg_ref, wu_ref, wd_ref,
                  xsem_ref, osem_ref, sem_ref):
    i = pl.program_id(0)
    n_used = plan_ref[PLAN_N_USED, 0]
    r = xbuf_ref.shape[1]

    def tile_rows(ref, tile):
        return ref.at[pl.ds(pl.multiple_of(tile * r, r), r), :]

    def fetch_tile(tile):
        s = tile % XS_RING
        return pltpu.make_async_copy(tile_rows(xs_hbm, tile), xbuf_ref.at[s], xsem_ref.at[s])

    def store_tile(tile):
        s = tile % XS_RING
        return pltpu.make_async_copy(xbuf_ref.at[s], tile_rows(ys_hbm, tile), osem_ref.at[s])

    @pl.when(i == 0)
    def _():
        for ahead in range(XS_AHEAD):
            @pl.when(ahead < n_used)
            def _():
                fetch_tile(ahead).start()

    @pl.when(i + XS_AHEAD < n_used)
    def _():
        @pl.when(i + XS_AHEAD >= XS_RING)
        def _():
            store_tile(i + XS_AHEAD - XS_RING).wait()

        fetch_tile(i + XS_AHEAD).start()

    def fetches(expert, s):
        return (pltpu.make_async_copy(wg_hbm.at[expert], sg_ref.at[s], sem_ref.at[s]),
                pltpu.make_async_copy(wu_hbm.at[expert], su_ref.at[s], sem_ref.at[s]),
                pltpu.make_async_copy(wd_hbm.at[expert], sd_ref.at[s], sem_ref.at[s]))

    @pl.when(i < n_used)
    def _():
        s = plan_ref[PLAN_SLOT, i]
        fetch_tile(i).wait()
        xs_ref = xbuf_ref.at[i % XS_RING]

        @pl.when(plan_ref[PLAN_FIRST, i] == 1)
        def _():
            @pl.when(i == 0)
            def _():
                for cp in fetches(plan_ref[PLAN_TILE_EXP, 0], 0):
                    cp.start()

            for cp in fetches(plan_ref[PLAN_TILE_EXP, i], s):
                cp.wait()

            @pl.when(plan_ref[PLAN_NEXT_EXP, i] >= 0)
            def _():
                for cp in fetches(plan_ref[PLAN_NEXT_EXP, i], 1 - s):
                    cp.start()

            wg_ref[...] = sg_ref[s].astype(bf16)
            wu_ref[...] = su_ref[s].astype(bf16)
            wd_ref[...] = sd_ref[s].astype(bf16)

        w_row = (xs_ref[:, D_MODEL:D_MODEL + 1].astype(f32)
                 + xs_ref[:, D_MODEL + 1:D_MODEL + 2].astype(f32))
        gate = jnp.dot(xs_ref[:, :D_MODEL], wg_ref[...], preferred_element_type=f32)
        up = jnp.dot(xs_ref[:, :D_MODEL], wu_ref[...], preferred_element_type=f32)
        hdn = (gate * jax.nn.sigmoid(gate)) * up
        y = jnp.dot(hdn.astype(bf16), wd_ref[...], preferred_element_type=f32)
        xs_ref[:, :D_MODEL] = (y * w_row).astype(bf16)
        store_tile(i).start()

    @pl.when(i == pl.num_programs(0) - 1)
    def _():
        for back in range(1, XS_RING + 1):
            @pl.when(n_used - back >= 0)
            def _():
                store_tile(n_used - back).wait()


def _experts(plan, xs, wg, wu, wd, rows_per_tile):
    r = rows_per_tile
    n_tiles = xs.shape[0] // r
    assert n_tiles <= BLK, "the plan holds one row tile per lane"
    grid_spec = pltpu.PrefetchScalarGridSpec(
        num_scalar_prefetch=1,
        grid=(n_tiles,),
        in_specs=[pl.BlockSpec(memory_space=pl.ANY),
                  pl.BlockSpec(memory_space=pl.ANY),
                  pl.BlockSpec(memory_space=pl.ANY),
                  pl.BlockSpec(memory_space=pl.ANY)],
        out_specs=pl.BlockSpec(memory_space=pl.ANY),
        scratch_shapes=[pltpu.VMEM((XS_RING, r, XS_COLS), bf16),
                        pltpu.VMEM((2, D_MODEL, D_FF_E), f32),
                        pltpu.VMEM((2, D_MODEL, D_FF_E), f32),
                        pltpu.VMEM((2, D_FF_E, D_MODEL), f32),
                        pltpu.VMEM((D_MODEL, D_FF_E), bf16),
                        pltpu.VMEM((D_MODEL, D_FF_E), bf16),
                        pltpu.VMEM((D_FF_E, D_MODEL), bf16),
                        pltpu.SemaphoreType.DMA((XS_RING,)),
                        pltpu.SemaphoreType.DMA((XS_RING,)),
                        pltpu.SemaphoreType.DMA((2,))])
    return pl.pallas_call(
        _experts_body,
        grid_spec=grid_spec,
        out_shape=jax.ShapeDtypeStruct(xs.shape, xs.dtype),
        input_output_aliases={1: 0},
        compiler_params=pltpu.CompilerParams(
            dimension_semantics=("arbitrary",), vmem_limit_bytes=VMEM_LIMIT),
        name="experts",
    )(plan, xs, wg, wu, wd)


def _combine_out_body(runs_cur_ref, runs_nxt_ref, h_ref, slot_ref, p_ref, ys_ref,
                      wpg_ref, bpg_ref, wpp_ref, gple_ref, gfin_ref,
                      o_ref, ybuf_ref, moe_ref, sem_ref):
    tk = h_ref.shape[0]
    lb = ybuf_ref.shape[1]
    step = pl.program_id(0)
    n = pl.num_programs(0) - 1
    cur = jnp.minimum(step, n - 1) % 2

    def piece(s, local_row, global_row, rows):
        return pltpu.make_async_copy(ys_ref.at[pl.ds(global_row, rows), pl.ds(0, D_MODEL)],
                                     ybuf_ref.at[s, pl.ds(local_row, rows), :], sem_ref.at[s])

    @pl.when(step == 0)
    def _():
        ybuf_ref[...] = jnp.zeros(ybuf_ref.shape, bf16)
        _for_each_run_piece(runs_cur_ref, lambda l, g, r: piece(0, l, g, r).start())

    @pl.when(step + 1 < n)
    def _():
        _for_each_run_piece(runs_nxt_ref, lambda l, g, r: piece(1 - cur, l, g, r).start())

    run_rows = _tile_run_rows(runs_cur_ref)

    @pl.when(step < n)
    def _():
        _wait_rows(run_rows, lambda r: piece(cur, 0, 0, r))

    def work(rows, finish, unsort):
        if finish:
            h2 = h_ref[...] + moe_ref[...]
            gate = jax.nn.sigmoid(jnp.dot(h2.astype(bf16), wpg_ref[...], preferred_element_type=f32)
                                  + bpg_ref[...])
            pp = jnp.dot(p_ref[...].astype(bf16), wpp_ref[...], preferred_element_type=f32)
            h3 = h2 + gate * _rms(pp, gple_ref[...])
            o_ref[...] = _rms(h3, gfin_ref[...])
        if unsort:
            col = lax.broadcasted_iota(i32, (tk, rows), 1)
            unperm = jnp.where((col == slot_ref[:, 0:1]) | (col == slot_ref[:, 1:2]), 1.0, 0.0)
            moe_ref[...] = jnp.dot(unperm.astype(bf16), ybuf_ref[cur, :rows, :],
                                   preferred_element_type=f32)

    short = lb - 256
    is_short = run_rows <= short
    first = step == 0
    mid = (step > 0) & (step < n)
    pl.when(first & is_short)(functools.partial(work, short, False, True))
    pl.when(first & jnp.logical_not(is_short))(functools.partial(work, lb, False, True))
    pl.when(mid & is_short)(functools.partial(work, short, True, True))
    pl.when(mid & jnp.logical_not(is_short))(functools.partial(work, lb, True, True))
    pl.when(step == n)(functools.partial(work, lb, True, False))


def _combine_out(run_tiles, h2d, slot_t, p2d, ys, wpg, bpg, wpp, gple, gfin):
    t = h2d.shape[0]
    tk = TL_SORT
    nt = t // tk
    const = lambda shape: pl.BlockSpec(shape, lambda i: (0,) * len(shape))
    runs = lambda imap: pl.BlockSpec((1, 1, BLK), imap, memory_space=pltpu.SMEM)
    uns = lambda s: jnp.minimum(s, nt - 1)
    fin = lambda s: jnp.maximum(s - 1, 0)
    return pl.pallas_call(
        _combine_out_body,
        grid=(nt + 1,),
        out_shape=jax.ShapeDtypeStruct((t, D_MODEL), f32),
        in_specs=[runs(lambda s: (uns(s), 0, 0)),
                  runs(lambda s: (uns(s + 1), 0, 0)),
                  pl.BlockSpec((tk, D_MODEL), lambda s: (fin(s), 0)),
                  pl.BlockSpec((tk, 8), lambda s: (uns(s), 0)),
                  pl.BlockSpec((tk, PLE_DIM), lambda s: (fin(s), 0)),
                  pl.BlockSpec(memory_space=pl.ANY),
                  const((D_MODEL, D_MODEL)), const((1, D_MODEL)),
                  const((PLE_DIM, D_MODEL)), const((1, D_MODEL)), const((1, D_MODEL))],
        out_specs=pl.BlockSpec((tk, D_MODEL), lambda s: (fin(s), 0)),
        scratch_shapes=[pltpu.VMEM((2, LB_SORT, D_MODEL), bf16),
                        pltpu.VMEM((tk, D_MODEL), f32),
                        pltpu.SemaphoreType.DMA((2,))],
        compiler_params=pltpu.CompilerParams(
            dimension_semantics=("arbitrary",), vmem_limit_bytes=VMEM_LIMIT),
        name="combine_out",
    )(run_tiles, run_tiles, h2d, slot_t, p2d, ys, wpg, bpg, wpp, gple, gfin)


def _head_major_to_tile_major(a, axis):
    axis = axis % a.ndim
    shape = a.shape
    a = a.reshape(shape[:axis] + (N_KV, GQA, HEAD_DIM) + shape[axis + 1:])
    return jnp.swapaxes(a, axis, axis + 1).reshape(shape)


def kernel(x, p, rel_bias, g_mix, w_in, ln_v_g, ln_v_b, w_spatial, b_spatial, sink, g_out_grp, w_out,
           g_ffn, w_router_group, b_router_group, w_router_expert, b_router_expert, w_gate_e, w_up_e,
           w_down_e, w_ple_proj, g_ple, w_ple_gate, b_ple_gate, g_final):
    b, s, d = x.shape
    t = b * s
    depth = g_mix.shape[0]
    assert depth == 1 and d == D_MODEL
    c1, c2 = 2 * D_A, 2 * D_A + D_B
    bias = (rel_bias * LOG2E).astype(f32)
    hcur = x.astype(f32)
    for li in range(depth):
        col_scale = np.ones((1, D_IN), np.float32)
        col_scale[:, c1:c2] = LOG2E * HEAD_DIM ** -0.5
        win = (w_in[li] * col_scale).astype(bf16)
        bs = jnp.broadcast_to(b_spatial[li][:, :, None], (A_HEADS, BLK, BLK)).astype(f32)
        gout = g_out_grp[li]
        gout_b = _head_major_to_tile_major(gout[D_A:], 0)[None]
        wo = w_out[li]
        wout = jnp.concatenate([wo[:D_A], _head_major_to_tile_major(wo[D_A:], 0)],
                               axis=0).astype(bf16)
        wr = jnp.concatenate([
            w_router_group[li], jnp.zeros((D_MODEL, 8 - N_GROUPS), f32), w_router_expert[li],
            jnp.zeros((D_MODEL, BLK - ROUTE_ROWS), f32)], axis=1).astype(bf16)
        br = jnp.concatenate([b_router_group[li], jnp.full((8 - N_GROUPS,), NEG, f32),
                              b_router_expert[li]])
        br = jnp.broadcast_to(br[:, None], (ROUTE_ROWS, TQ_ATT))

        ya, q, k, v = _mix_in(hcur.reshape(t, d), g_mix[li][None], win, ln_v_g[li][None],
                              ln_v_b[li][None], w_spatial[li].astype(bf16), bs, gout[None, :D_A])
        h, m_rows, route_e, route_w = _attn_out(
            sink[li].astype(f32) * LOG2E, hcur, ya.reshape(b, s, D_A), q.reshape(b, s, D_B),
            k.reshape(b, s, BLK), v.reshape(b, s, BLK), bias, gout_b, wout,
            g_ffn[li][None], wr, br)

        r = R_EXP
        n_sort = t // TL_SORT
        max_rows = 2 * t + n_sort * N_EXPERTS * (RUN_ALIGN - 1)
        n_tiles = -(-max_rows // r) + N_EXPERTS
        assert n_tiles * r < 2 ** 16, "route_pos prefix sums hold row offsets in two bf16 bytes"
        slot, run_tiles, plan = _route_pos(route_e, r)
        xs, slot_t = _sort_rows(plan, run_tiles, m_rows, slot, route_w, n_tiles * r, r)
        ys = _experts(plan, xs, w_gate_e[li], w_up_e[li], w_down_e[li], r)

        out = _combine_out(run_tiles, h.reshape(t, d), slot_t,
                           p[li].reshape(t, PLE_DIM), ys, w_ple_gate[li].astype(bf16),
                           b_ple_gate[li][None], w_ple_proj[li].astype(bf16), g_ple[li][None],
                           g_final[None])
        hcur = out.reshape(b, s, d)
    return hcur
```

```python
import functools
import math

import jax
import jax.numpy as jnp
import numpy as np
from jax import lax
from jax.experimental import pallas as pl
from jax.experimental.pallas import tpu as pltpu

D_MODEL = 1024
D_A = 512
D_B = 512
BLK = 128
A_HEADS = 4
HEAD_DIM = 64
N_HEADS = 8
N_KV = 2
GQA = 4
WINDOW = 128
NUM_BUCKETS = 32
MAX_DIST = 128
D_IN = 2 * D_A + D_B + 2 * N_KV * HEAD_DIM
N_GROUPS = 4
E_PER_GROUP = 8
N_EXPERTS = 32
D_FF_E = 256
PLE_DIM = 256
EPS = 1e-6
NEG = -1e30
LOG2E = math.log2(math.e)

TM_MIX = 512
TQ_ATT = 512
TL_SORT = 512
RUN_ALIGN = 16
SMALL_BITS = 2
BIG_PIECE = RUN_ALIGN << SMALL_BITS
LB_SORT = 2 * TL_SORT + N_EXPERTS * RUN_ALIGN
R_EXP = 512
XS_COLS = D_MODEL + BLK
XS_RING = 4
XS_AHEAD = 2
ROUTE_ROWS = 8 + N_EXPERTS
ROUTE_UNROLL = 8
VMEM_LIMIT = 48 * 1024 * 1024

f32 = jnp.float32
bf16 = jnp.bfloat16
i32 = jnp.int32


def _rms(x, g):
    return x * lax.rsqrt(jnp.mean(x * x, axis=-1, keepdims=True) + EPS) * g


def _gelu_tanh(x):
    c = math.sqrt(2.0 / math.pi)
    return x * (0.5 * (1.0 + jnp.tanh(c * (x + 0.044715 * (x * x * x)))))


def _bucket_row():
    n = NUM_BUCKETS // 2
    max_exact = n // 2
    rel = np.broadcast_to(np.arange(4 * BLK)[None, :] - BLK, (8, 4 * BLK))
    ret = np.where(rel > 0, n, 0)
    a = np.abs(rel)
    large = max_exact + (np.log(np.maximum(a, 1).astype(np.float64) / max_exact)
                         / math.log(MAX_DIST / max_exact) * (n - max_exact)).astype(np.int32)
    large = np.minimum(large, n - 1)
    bucket = ret + np.where(a < max_exact, a, large)
    return np.where(a <= WINDOW, bucket, -1).astype(np.int32)


def _bias_body(rb_ref, bucket_ref, o_ref):
    bucket = bucket_ref[...]
    row = lax.broadcasted_iota(i32, (BLK, BLK), 0)
    lane = lax.broadcasted_iota(i32, (BLK, BLK), 1)
    for h in range(N_HEADS):
        base = jnp.full(bucket.shape, NEG, f32)
        for b in range(NUM_BUCKETS):
            base = jnp.where(bucket == b, rb_ref[b, h], base)
        prev = jnp.full((BLK, BLK), NEG, f32)
        for c in range(3):
            tile = jnp.broadcast_to(base[0:1, c * BLK:(c + 1) * BLK], (BLK, BLK))
            rolled = pltpu.roll(tile, 0, 1, stride=1, stride_axis=0)
            o_ref[h, :, c * BLK:(c + 1) * BLK] = jnp.where(lane >= row, rolled, prev)
            prev = rolled


def _mix_in_body(x_ref, gmix_ref, win_ref, lng_ref, lnb_ref, ws_ref, bs_ref, gout_ref,
                 ya_ref, q_ref, k_ref, v_ref):
    tm = x_ref.shape[0]
    nc = tm // BLK
    a = _rms(x_ref[...], gmix_ref[...])
    z = jnp.dot(a.astype(bf16), win_ref[...], preferred_element_type=f32)
    uv = _gelu_tanh(z[:, :2 * D_A])
    u = uv[:, :D_A]
    v = uv[:, D_A:]
    mu = jnp.mean(v, axis=-1, keepdims=True)
    vc = v - mu
    var = jnp.mean(vc * vc, axis=-1, keepdims=True)
    vn = (vc * lax.rsqrt(var + EPS) * lng_ref[...] + lnb_ref[...]).astype(bf16)
    cols = []
    for h in range(A_HEADS):
        rhs = jnp.concatenate(
            [vn[c * BLK:(c + 1) * BLK, h * BLK:(h + 1) * BLK] for c in range(nc)], axis=1)
        r = jnp.dot(ws_ref[h], rhs, preferred_element_type=f32)
        cols.append(jnp.concatenate(
            [r[:, c * BLK:(c + 1) * BLK] + bs_ref[h] for c in range(nc)], axis=0))
    sv = jnp.concatenate(cols, axis=1)
    ya_ref[...] = _rms(u * sv, gout_ref[...]).astype(bf16)
    qt = [z[:, 2 * D_A + j * BLK:2 * D_A + (j + 1) * BLK] for j in range(D_B // BLK)]
    low_half = lax.broadcasted_iota(i32, (tm, BLK), 1) < HEAD_DIM
    for g in range(GQA):
        a0, a1 = qt[g // 2], qt[GQA // 2 + g // 2]
        tile = (jnp.where(low_half, a0, pltpu.roll(a1, HEAD_DIM, 1)) if g % 2 == 0
                else jnp.where(low_half, pltpu.roll(a0, HEAD_DIM, 1), a1))
        q_ref[:, g * BLK:(g + 1) * BLK] = tile.astype(bf16)
    k_ref[...] = z[:, 2 * D_A + D_B:2 * D_A + D_B + BLK].astype(bf16)
    v_ref[...] = z[:, 2 * D_A + D_B + BLK:].astype(bf16)


def _mix_in(x2, gmix, win, lng, lnb, ws, bs, gout_a):
    t = x2.shape[0]
    tm = TM_MIX
    const = lambda shape: pl.BlockSpec(shape, lambda i: (0,) * len(shape))
    return pl.pallas_call(
        _mix_in_body,
        grid=(t // tm,),
        out_shape=(jax.ShapeDtypeStruct((t, D_A), bf16),
                   jax.ShapeDtypeStruct((t, D_B), bf16),
                   jax.ShapeDtypeStruct((t, BLK), bf16),
                   jax.ShapeDtypeStruct((t, BLK), bf16)),
        in_specs=[pl.BlockSpec((tm, D_MODEL), lambda i: (i, 0)),
                  const((1, D_MODEL)), const((D_MODEL, D_IN)),
                  const((1, D_A)), const((1, D_A)),
                  const((A_HEADS, BLK, BLK)), const((A_HEADS, BLK, BLK)),
                  const((1, D_A))],
        out_specs=(pl.BlockSpec((tm, D_A), lambda i: (i, 0)),
                   pl.BlockSpec((tm, D_B), lambda i: (i, 0)),
                   pl.BlockSpec((tm, BLK), lambda i: (i, 0)),
                   pl.BlockSpec((tm, BLK), lambda i: (i, 0))),
        compiler_params=pltpu.CompilerParams(
            dimension_semantics=("parallel",), vmem_limit_bytes=VMEM_LIMIT),
        name="mix_in",
    )(x2, gmix, win, lng, lnb, ws, bs, gout_a)


def _attn_epilogue(sink_ref, x_ref, ya_ref, q_ref, kp_ref, km_ref, kn_ref, vp_ref, vm_ref, vn_ref,
                   rb_ref, bucket_ref, goutb_ref, wout_ref, gffn_ref, wr_ref, br_ref,
                   h_ref, m_ref, re_ref, rw_ref,
                   kf_ref, vf_ref, e_ref, yb_ref, bias_ref, *, seq_len):
    tq = x_ref.shape[1]
    ybn = _rms(yb_ref[1], goutb_ref[...]).astype(bf16)
    y = jnp.concatenate([ya_ref[0], ybn], axis=1)
    h = x_ref[0] + jnp.dot(y, wout_ref[...], preferred_element_type=f32)
    h_ref[0] = h
    m = _rms(h, gffn_ref[...])
    m_ref[...] = m.astype(bf16)

    logit_t = jnp.dot(m.astype(bf16), wr_ref[...], preferred_element_type=f32)
    logit = jnp.transpose(logit_t)[:ROUTE_ROWS] + br_ref[...]
    sub = lax.broadcasted_iota(i32, (8, tq), 0)
    lg = logit[0:8]
    mg = jnp.max(lg, axis=0, keepdims=True)
    pg_top = 1.0 / jnp.sum(jnp.exp(lg - mg), axis=0, keepdims=True)
    g_idx = jnp.min(jnp.where(lg == mg, sub, 8), axis=0, keepdims=True)
    sel = logit[8:16]
    for g in range(1, N_GROUPS):
        sel = jnp.where(g_idx == g, logit[8 + 8 * g:16 + 8 * g], sel)
    m1 = jnp.max(sel, axis=0, keepdims=True)
    i1 = jnp.min(jnp.where(sel == m1, sub, 8), axis=0, keepdims=True)
    sel2 = jnp.where(sub == i1, -jnp.inf, sel)
    m2 = jnp.max(sel2, axis=0, keepdims=True)
    i2 = jnp.min(jnp.where(sel2 == m2, sub, 8), axis=0, keepdims=True)
    r = jnp.exp(m2 - m1)
    w1 = pg_top / (1.0 + r)
    w2 = pg_top * r / (1.0 + r)
    e1 = g_idx * E_PER_GROUP + i1
    e2 = g_idx * E_PER_GROUP + i2
    re_ref[...] = jnp.where(sub == 0, e1, jnp.where(sub == 1, e2, 0))
    rw_ref[...] = jnp.where(sub == 0, w1, jnp.where(sub == 1, w2, 0.0))


def _attn_attention(sink_ref, x_ref, ya_ref, q_ref, kp_ref, km_ref, kn_ref, vp_ref, vm_ref, vn_ref,
                    rb_ref, bucket_ref, goutb_ref, wout_ref, gffn_ref, wr_ref, br_ref,
                    h_ref, m_ref, re_ref, rw_ref,
                    kf_ref, vf_ref, e_ref, yb_ref, bias_ref, *, seq_len):
    tq = x_ref.shape[1]
    nb = tq // BLK
    ti = pl.program_id(0) % (seq_len // tq)
    kf_ref[0:BLK] = kp_ref[0]
    kf_ref[BLK:BLK + tq] = km_ref[0]
    kf_ref[BLK + tq:] = kn_ref[0]
    vlane = lax.broadcasted_iota(i32, (BLK, BLK), 1)
    for src, r0, r1 in ((vp_ref, 0, BLK), (vn_ref, BLK + tq, tq + 2 * BLK)) + tuple(
            (vm_ref.at[:, pl.ds(c * BLK, BLK), :], BLK + c * BLK, BLK + (c + 1) * BLK)
            for c in range(nb)):
        vals = src[0].astype(f32)
        vf_ref[0, r0:r1] = jnp.where(vlane < HEAD_DIM, vals,
                                     jnp.where(vlane == HEAD_DIM, 1.0, 0.0)).astype(bf16)
        vf_ref[1, r0:r1] = jnp.where(vlane >= HEAD_DIM, vals,
                                     jnp.where(vlane == 0, 1.0, 0.0)).astype(bf16)

    lane = lax.broadcasted_iota(i32, (BLK, BLK), 1)
    low_half = lane < HEAD_DIM

    for n in range(nb):
        r0 = n * BLK
        qb = q_ref[0, pl.ds(r0, BLK), :]
        kb = kf_ref[pl.ds(r0, 3 * BLK), :]
        zero = jnp.zeros((BLK, BLK), bf16)
        lhs = []
        for kh in range(N_KV):
            for g in range(GQA):
                tile = qb[:, g * BLK:(g + 1) * BLK]
                lhs.append(jnp.where(low_half, tile, zero) if kh == 0
                           else jnp.where(low_half, zero, tile))
        lhs = jnp.concatenate(lhs, axis=0)
        s_all = lax.dot_general(lhs, kb, (((1,), (1,)), ((), ())),
                                preferred_element_type=f32)
        blk = ti * nb + n
        pen = [jnp.where(blk == 0, NEG, 0.0) if n == 0 else None, None,
               jnp.where((blk + 1) * BLK >= seq_len, NEG, 0.0) if n == nb - 1 else None]
        sink_e = []
        for h in range(N_HEADS):
            bias_h = bias_ref[h]
            bias_h = jnp.concatenate(
                [bias_h[:, j * BLK:(j + 1) * BLK] if pen[j] is None
                 else bias_h[:, j * BLK:(j + 1) * BLK] + pen[j] for j in range(3)], axis=1)
            s = s_all[h * BLK:(h + 1) * BLK] + bias_h
            sk = sink_ref[h]
            mrow = jnp.maximum(jnp.max(s, axis=-1, keepdims=True), sk)
            e_ref[n, h * BLK:(h + 1) * BLK, :] = jnp.exp2(s - mrow).astype(bf16)
            sink_e.append(jnp.exp2(sk - mrow))
        half = GQA * BLK
        pv = [jnp.dot(e_ref[n, kh * half:(kh + 1) * half, :], vf_ref[kh, pl.ds(r0, 3 * BLK), :],
                      preferred_element_type=f32) for kh in range(N_KV)]

        def head_out(h):
            kh, g = divmod(h, GQA)
            rows = pv[kh][g * BLK:(g + 1) * BLK]
            ones_col = HEAD_DIM if kh == 0 else 0
            return rows * (1.0 / (rows[:, ones_col:ones_col + 1] + sink_e[h]))

        for g in range(GQA):
            yb_ref[0, pl.ds(r0, BLK), g * BLK:(g + 1) * BLK] = jnp.where(
                low_half, head_out(g), head_out(GQA + g))

    yb_ref[1] = yb_ref[0]


def _attn_out_body(*refs, seq_len):
    step = pl.program_id(0)
    n_tok_tiles = pl.num_programs(0) - 1

    @pl.when(step == 0)
    def _():
        _bias_body(refs[10], refs[11], refs[-1])
        _attn_attention(*refs, seq_len=seq_len)

    @pl.when((step > 0) & (step < n_tok_tiles))
    def _():
        _attn_epilogue(*refs, seq_len=seq_len)
        _attn_attention(*refs, seq_len=seq_len)

    @pl.when(step == n_tok_tiles)
    def _():
        _attn_epilogue(*refs, seq_len=seq_len)


def _attn_out(sink, x, ya, q, k, v, rel_bias, goutb, wout, gffn, wr, br):
    b, s, _ = x.shape
    tq = TQ_ATT
    nb = tq // BLK
    nblk = s // BLK
    t = b * s
    nq = s // tq
    n_tok_tiles = b * nq
    att = lambda s: jnp.minimum(s, n_tok_tiles - 1)
    epi = lambda s: jnp.maximum(s - 1, 0)
    const = lambda shape: pl.BlockSpec(shape, lambda s, *_: (0,) * len(shape))
    tok_att = lambda w: pl.BlockSpec((1, tq, w), lambda s, *_: (att(s) // nq, att(s) % nq, 0))
    tok_epi = lambda w: pl.BlockSpec((1, tq, w), lambda s, *_: (epi(s) // nq, epi(s) % nq, 0))
    prev = pl.BlockSpec((1, BLK, BLK), lambda s, *_: (
        att(s) // nq, jnp.maximum((att(s) % nq) * nb - 1, 0), 0))
    nxt = pl.BlockSpec((1, BLK, BLK), lambda s, *_: (
        att(s) // nq, jnp.minimum((att(s) % nq) * nb + nb, nblk - 1), 0))
    rows = pl.BlockSpec((tq, D_MODEL), lambda s, *_: (epi(s), 0))
    lanes = lambda rows: pl.BlockSpec((rows, tq), lambda s, *_: (0, epi(s)))
    grid_spec = pltpu.PrefetchScalarGridSpec(
        num_scalar_prefetch=1,
        grid=(n_tok_tiles + 1,),
        in_specs=[tok_epi(D_MODEL), tok_epi(D_A), tok_att(D_B),
                  prev, tok_att(BLK), nxt, prev, tok_att(BLK), nxt,
                  pl.BlockSpec(memory_space=pltpu.SMEM), const((8, 4 * BLK)),
                  const((1, D_B)), const((D_MODEL, D_MODEL)),
                  const((1, D_MODEL)), const((D_MODEL, BLK)), const((ROUTE_ROWS, tq))],
        out_specs=(tok_epi(D_MODEL), rows, lanes(8), lanes(8)),
        scratch_shapes=[pltpu.VMEM((tq + 2 * BLK, BLK), bf16),
                        pltpu.VMEM((N_KV, tq + 2 * BLK, BLK), bf16),
                        pltpu.VMEM((nb, N_HEADS * BLK, 3 * BLK), bf16),
                        pltpu.VMEM((2, tq, D_B), f32),
                        pltpu.VMEM((N_HEADS, BLK, 3 * BLK), f32)])
    return pl.pallas_call(
        functools.partial(_attn_out_body, seq_len=s),
        grid_spec=grid_spec,
        out_shape=(jax.ShapeDtypeStruct((b, s, D_MODEL), f32),
                   jax.ShapeDtypeStruct((t, D_MODEL), bf16),
                   jax.ShapeDtypeStruct((8, t), i32),
                   jax.ShapeDtypeStruct((8, t), f32)),
        compiler_params=pltpu.CompilerParams(
            dimension_semantics=("arbitrary",), vmem_limit_bytes=VMEM_LIMIT),
        name="attn_out",
    )(sink, x, ya, q, k, k, k, v, v, v, rel_bias, jnp.asarray(_bucket_row()), goutb, wout, gffn, wr, br)


def _exclusive_prefix(vals):
    a = lax.broadcasted_iota(i32, (N_EXPERTS, N_EXPERTS), 0)
    c = lax.broadcasted_iota(i32, (N_EXPERTS, N_EXPERTS), 1)
    low = jnp.where(c < a, 1.0, 0.0).astype(bf16)
    hi = (vals >> 8).astype(f32).astype(bf16)
    lo = (vals & 255).astype(f32).astype(bf16)
    return (jnp.dot(low, hi, preferred_element_type=f32) * 256.0
            + jnp.dot(low, lo, preferred_element_type=f32)).astype(i32)


def _round_up_pow2(vals, mult):
    log_m = mult.bit_length() - 1
    return ((vals + (mult - 1)) >> log_m) << log_m


PLAN_TILE_EXP, PLAN_FIRST, PLAN_NEXT_EXP, PLAN_SLOT, PLAN_N_USED, PLAN_ENDS, PLAN_PADDED, PLAN_TOTAL = (
    range(8))


def _lane_row(vals, lane_off=0):
    sub = lax.broadcasted_iota(i32, vals.shape, 0)
    lane = lax.broadcasted_iota(i32, vals.shape, 1)
    return jnp.sum(jnp.where(lane == sub + lane_off, vals, 0), axis=0, keepdims=True)


def _route_pos_body(re_ref, slot_ref, runs_ref, plan_ref, tri_ref, *, rows_per_tile, tl):
    n_tok_tiles = re_ref.shape[1] // tl
    a = lax.broadcasted_iota(i32, (tl, tl), 0)
    c = lax.broadcasted_iota(i32, (tl, tl), 1)
    tri_ref[...] = jnp.where(a < c, 1.0, 0.0).astype(bf16)
    eid = lax.broadcasted_iota(i32, (N_EXPERTS, tl), 0)

    def tile_hits(j):
        cols = pl.ds(pl.multiple_of(j * tl, tl), tl)
        hit1 = eid == re_ref[0:1, cols]
        hit2 = eid == re_ref[1:2, cols]
        onehot = jnp.where(hit1 | hit2, 1.0, 0.0)
        cnt = jnp.sum(onehot, axis=1, keepdims=True).astype(i32)
        run_len = jnp.broadcast_to(_round_up_pow2(cnt, RUN_ALIGN), (N_EXPERTS, BLK))
        return cols, hit1, hit2, onehot, run_len

    total = lax.fori_loop(0, n_tok_tiles, lambda j, acc: acc + tile_hits(j)[4],
                          jnp.zeros((N_EXPERTS, BLK), i32), unroll=ROUTE_UNROLL)
    padded = _round_up_pow2(total, rows_per_tile)
    starts = _exclusive_prefix(padded)
    ends = starts + padded

    log_r = rows_per_tile.bit_length() - 1
    sub = lax.broadcasted_iota(i32, (N_EXPERTS, BLK), 0)
    lane = lax.broadcasted_iota(i32, (1, BLK), 1)
    n_used = ends[N_EXPERTS - 1:N_EXPERTS, :] >> log_r
    tile_idx = jnp.minimum(lane, n_used - 1)
    tile_exp = jnp.minimum(
        jnp.sum((ends <= (tile_idx << log_r)).astype(i32), axis=0, keepdims=True), N_EXPERTS - 1)
    nonempty = padded > 0
    regions_before = jnp.sum(((sub < tile_exp) & nonempty).astype(i32), axis=0, keepdims=True)
    nxt = jnp.min(jnp.where((sub > tile_exp) & nonempty, sub, N_EXPERTS), axis=0, keepdims=True)
    prev_exp = pltpu.roll(jnp.broadcast_to(tile_exp, (8, BLK)), 1, 1)[0:1]
    plan_ref[PLAN_TILE_EXP:PLAN_TILE_EXP + 1] = tile_exp
    plan_ref[PLAN_FIRST:PLAN_FIRST + 1] = ((lane == 0) | (tile_exp != prev_exp)).astype(i32)
    plan_ref[PLAN_NEXT_EXP:PLAN_NEXT_EXP + 1] = jnp.where(nxt < N_EXPERTS, nxt, -1)
    plan_ref[PLAN_SLOT:PLAN_SLOT + 1] = regions_before & 1
    plan_ref[PLAN_N_USED:PLAN_N_USED + 1] = n_used
    plan_ref[PLAN_ENDS:PLAN_ENDS + 1] = _lane_row(ends)
    plan_ref[PLAN_PADDED:PLAN_PADDED + 1] = _lane_row(padded)
    plan_ref[PLAN_TOTAL:PLAN_TOTAL + 1] = _lane_row(total)

    def place(j, global_off):
        cols, hit1, hit2, onehot, run_len = tile_hits(j)
        local_off = _exclusive_prefix(run_len)
        before = jnp.dot(onehot.astype(bf16), tri_ref[...], preferred_element_type=f32)
        slot = before + local_off[:, 0:1].astype(f32)
        s1 = jnp.sum(jnp.where(hit1, slot, 0.0), axis=0, keepdims=True)
        s2 = jnp.sum(jnp.where(hit2, slot, 0.0), axis=0, keepdims=True)
        sub = lax.broadcasted_iota(i32, (8, tl), 0)
        slot_ref[:, cols] = jnp.where(sub == 0, s1.astype(i32),
                                      jnp.where(sub == 1, s2.astype(i32), 0))
        tile_rows = jnp.sum(run_len, axis=0, keepdims=True)
        lane = lax.broadcasted_iota(i32, (1, BLK), 1)
        runs_ref[j] = (_lane_row(local_off) + _lane_row(run_len, N_EXPERTS)
                       + _lane_row(global_off, 2 * N_EXPERTS)
                       + jnp.where(lane == 3 * N_EXPERTS, tile_rows, 0))
        return global_off + run_len

    lax.fori_loop(0, n_tok_tiles, place, starts, unroll=ROUTE_UNROLL)


def _route_pos(route_e, rows_per_tile):
    t = route_e.shape[1]
    tl = TL_SORT
    return pl.pallas_call(
        functools.partial(_route_pos_body, rows_per_tile=rows_per_tile, tl=tl),
        out_shape=(jax.ShapeDtypeStruct((8, t), i32),
                   jax.ShapeDtypeStruct((t // tl, 1, BLK), i32),
                   jax.ShapeDtypeStruct((8, BLK), i32)),
        scratch_shapes=[pltpu.VMEM((tl, tl), bf16)],
        compiler_params=pltpu.CompilerParams(vmem_limit_bytes=VMEM_LIMIT),
        name="route_pos",
    )(route_e)


def _for_each_run_piece(runs_ref, fn):
    def run(e):
        return (runs_ref[0, 0, e], runs_ref[0, 0, N_EXPERTS + e], runs_ref[0, 0, 2 * N_EXPERTS + e])

    def per_expert(e, carry):
        local_off, length, global_off = carry
        carry = run(e + 1)
        units = length >> (RUN_ALIGN.bit_length() - 1)

        def big_piece(k, inner):
            done = k * BIG_PIECE
            fn(pl.multiple_of(local_off + done, RUN_ALIGN),
               pl.multiple_of(global_off + done, RUN_ALIGN), BIG_PIECE)
            return inner

        lax.fori_loop(0, units >> SMALL_BITS, big_piece, 0)
        for b in range(SMALL_BITS):
            @pl.when(((units >> b) & 1) == 1)
            def _():
                done = ((units >> (b + 1)) << (b + 1)) * RUN_ALIGN
                fn(pl.multiple_of(local_off + done, RUN_ALIGN),
                   pl.multiple_of(global_off + done, RUN_ALIGN), RUN_ALIGN << b)
        return carry

    lax.fori_loop(0, N_EXPERTS, per_expert, run(0))


def _tile_run_rows(runs_ref):
    return runs_ref[0, 0, 3 * N_EXPERTS]


def _wait_rows(total_rows, make_copy):
    units = total_rows >> (RUN_ALIGN.bit_length() - 1)
    for b in range((LB_SORT // RUN_ALIGN).bit_length()):
        @pl.when(((units >> b) & 1) == 1)
        def _():
            make_copy(RUN_ALIGN << b).wait()


def _sort_rows_body(plan_ref, runs_ref, m_ref, slot_ref, rw_ref,
                    xs_ref, slot_t_ref,
                    local_ref, zero_ref, pending_ref, sem_ref, zsem_ref, *, rows_per_tile):
    tl = m_ref.shape[0]
    lb = local_ref.shape[1]
    step = pl.program_id(0)
    cur = step % 2
    n_tiles = xs_ref.shape[0] // rows_per_tile
    n_used = plan_ref[PLAN_N_USED, 0]

    def zero_rows(start, rows, wait):
        cp = pltpu.make_async_copy(zero_ref.at[pl.ds(0, rows), :],
                                   xs_ref.at[pl.ds(pl.multiple_of(start, RUN_ALIGN), rows), :],
                                   zsem_ref)
        cp.wait() if wait else cp.start()

    def zero_fill(wait):
        for e in range(N_EXPERTS):
            pad_rows = plan_ref[PLAN_PADDED, e] - plan_ref[PLAN_TOTAL, e]
            pad_units = pad_rows >> (RUN_ALIGN.bit_length() - 1)
            for bit in range((rows_per_tile // RUN_ALIGN).bit_length() - 1):
                @pl.when(((pad_units >> bit) & 1) == 1)
                def _():
                    done = ((pad_units >> (bit + 1)) << (bit + 1)) * RUN_ALIGN
                    zero_rows(plan_ref[PLAN_ENDS, e] - pad_rows + done,
                              RUN_ALIGN << bit, wait)

        def spare(j, carry):
            zero_rows((n_used + j) * rows_per_tile, rows_per_tile, wait)
            return carry

        lax.fori_loop(0, n_tiles - n_used, spare, 0)

    @pl.when(step == 0)
    def _():
        zero_ref[...] = jnp.zeros(zero_ref.shape, bf16)
        zero_fill(wait=False)

    s1 = slot_ref[0:1, :]
    s2 = slot_ref[1:2, :]
    w1 = rw_ref[0:1, :]
    w2 = rw_ref[1:2, :]
    run_rows = _tile_run_rows(runs_ref)

    def sort_rows_upto(n):
        srow = lax.broadcasted_iota(i32, (n, tl), 0)
        p1 = srow == s1
        p2 = srow == s2
        perm = jnp.where(p1 | p2, 1.0, 0.0).astype(bf16)
        rows = jnp.dot(perm, m_ref[...], preferred_element_type=f32)
        w = jnp.sum(jnp.where(p1, w1, 0.0) + jnp.where(p2, w2, 0.0), axis=1, keepdims=True)
        w_hi = w.astype(bf16).astype(f32)
        w_lo = w - w_hi
        lane = lax.broadcasted_iota(i32, (n, BLK), 1)
        local_ref[cur, :n, :D_MODEL] = rows.astype(bf16)
        local_ref[cur, :n, D_MODEL:] = jnp.where(
            lane == 0, w_hi, jnp.where(lane == 1, w_lo, 0.0)).astype(bf16)
        sub_t = lax.broadcasted_iota(i32, (BLK, tl), 0)
        slots = jnp.where(sub_t == 0, s1.astype(f32), jnp.where(sub_t == 1, s2.astype(f32), 0.0))
        slot_t_ref[...] = jnp.transpose(slots)[:, :8].astype(i32)

    def piece(s, local_row, global_row, rows):
        return pltpu.make_async_copy(local_ref.at[s, pl.ds(local_row, rows), :],
                                     xs_ref.at[pl.ds(global_row, rows), :], sem_ref.at[s])

    short = lb - 256
    pl.when(run_rows <= short)(functools.partial(sort_rows_upto, short))
    pl.when(run_rows > short)(functools.partial(sort_rows_upto, lb))

    _for_each_run_piece(runs_ref, lambda l, g, n: piece(cur, l, g, n).start())

    @pl.when(step > 0)
    def _():
        _wait_rows(pending_ref[0], lambda n: piece(1 - cur, 0, 0, n))

    pending_ref[0] = run_rows

    @pl.when(step == pl.num_programs(0) - 1)
    def _():
        _wait_rows(pending_ref[0], lambda n: piece(cur, 0, 0, n))
        zero_fill(wait=True)


def _sort_rows(plan, run_tiles, m, slot, route_w, total_rows, rows_per_tile):
    t = m.shape[0]
    tl = TL_SORT
    grid_spec = pltpu.PrefetchScalarGridSpec(
        num_scalar_prefetch=1,
        grid=(t // tl,),
        in_specs=[pl.BlockSpec((1, 1, BLK), lambda i, *_: (i, 0, 0),
                               memory_space=pltpu.SMEM),
                  pl.BlockSpec((tl, D_MODEL), lambda i, *_: (i, 0)),
                  pl.BlockSpec((8, tl), lambda i, *_: (0, i)),
                  pl.BlockSpec((8, tl), lambda i, *_: (0, i))],
        out_specs=(pl.BlockSpec(memory_space=pl.ANY),
                   pl.BlockSpec((tl, 8), lambda i, *_: (i, 0))),
        scratch_shapes=[pltpu.VMEM((2, LB_SORT, XS_COLS), bf16),
                        pltpu.VMEM((rows_per_tile, XS_COLS), bf16),
                        pltpu.SMEM((1,), i32),
                        pltpu.SemaphoreType.DMA((2,)),
                        pltpu.SemaphoreType.DMA(())])
    return pl.pallas_call(
        functools.partial(_sort_rows_body, rows_per_tile=rows_per_tile),
        grid_spec=grid_spec,
        out_shape=(jax.ShapeDtypeStruct((total_rows, XS_COLS), bf16),
                   jax.ShapeDtypeStruct((t, 8), i32)),
        compiler_params=pltpu.CompilerParams(
            dimension_semantics=("arbitrary",), vmem_limit_bytes=VMEM_LIMIT),
        name="sort_rows",
    )(plan, run_tiles, m, slot, route_w)


def _experts_body(plan_ref,
                  xs_hbm, wg_hbm, wu_hbm, wd_hbm, ys_hbm,
                  xbuf_ref, sg_ref, su_ref, sd_ref, wg_ref, wu_ref, wd_ref,
                  xsem_ref, osem_ref, sem_ref):
    i = pl.program_id(0)
    n_used = plan_ref[PLAN_N_USED, 0]
    r = xbuf_ref.shape[1]

    def tile_rows(ref, tile):
        return ref.at[pl.ds(pl.multiple_of(tile * r, r), r), :]

    def fetch_tile(tile):
        s = tile % XS_RING
        return pltpu.make_async_copy(tile_rows(xs_hbm, tile), xbuf_ref.at[s], xsem_ref.at[s])

    def store_tile(tile):
        s = tile % XS_RING
        return pltpu.make_async_copy(xbuf_ref.at[s], tile_rows(ys_hbm, tile), osem_ref.at[s])

    @pl.when(i == 0)
    def _():
        for ahead in range(XS_AHEAD):
            @pl.when(ahead < n_used)
            def _():
                fetch_tile(ahead).start()

    @pl.when(i + XS_AHEAD < n_used)
    def _():
        @pl.when(i + XS_AHEAD >= XS_RING)
        def _():
            store_tile(i + XS_AHEAD - XS_RING).wait()

        fetch_tile(i + XS_AHEAD).start()

    def fetches(expert, s):
        return (pltpu.make_async_copy(wg_hbm.at[expert], sg_ref.at[s], sem_ref.at[s]),
                pltpu.make_async_copy(wu_hbm.at[expert], su_ref.at[s], sem_ref.at[s]),
                pltpu.make_async_copy(wd_hbm.at[expert], sd_ref.at[s], sem_ref.at[s]))

    @pl.when(i < n_used)
    def _():
        s = plan_ref[PLAN_SLOT, i]
        fetch_tile(i).wait()
        xs_ref = xbuf_ref.at[i % XS_RING]

        @pl.when(plan_ref[PLAN_FIRST, i] == 1)
        def _():
            @pl.when(i == 0)
            def _():
                for cp in fetches(plan_ref[PLAN_TILE_EXP, 0], 0):
                    cp.start()

            for cp in fetches(plan_ref[PLAN_TILE_EXP, i], s):
                cp.wait()

            @pl.when(plan_ref[PLAN_NEXT_EXP, i] >= 0)
            def _():
                for cp in fetches(plan_ref[PLAN_NEXT_EXP, i], 1 - s):
                    cp.start()

            wg_ref[...] = sg_ref[s].astype(bf16)
            wu_ref[...] = su_ref[s].astype(bf16)
            wd_ref[...] = sd_ref[s].astype(bf16)

        w_row = (xs_ref[:, D_MODEL:D_MODEL + 1].astype(f32)
                 + xs_ref[:, D_MODEL + 1:D_MODEL + 2].astype(f32))
        gate = jnp.dot(xs_ref[:, :D_MODEL], wg_ref[...], preferred_element_type=f32)
        up = jnp.dot(xs_ref[:, :D_MODEL], wu_ref[...], preferred_element_type=f32)
        hdn = (gate * jax.nn.sigmoid(gate)) * up
        y = jnp.dot(hdn.astype(bf16), wd_ref[...], preferred_element_type=f32)
        xs_ref[:, :D_MODEL] = (y * w_row).astype(bf16)
        store_tile(i).start()

    @pl.when(i == pl.num_programs(0) - 1)
    def _():
        for back in range(1, XS_RING + 1):
            @pl.when(n_used - back >= 0)
            def _():
                store_tile(n_used - back).wait()


def _experts(plan, xs, wg, wu, wd, rows_per_tile):
    r = rows_per_tile
    n_tiles = xs.shape[0] // r
    assert n_tiles <= BLK, "the plan holds one row tile per lane"
    grid_spec = pltpu.PrefetchScalarGridSpec(
        num_scalar_prefetch=1,
        grid=(n_tiles,),
        in_specs=[pl.BlockSpec(memory_space=pl.ANY),
                  pl.BlockSpec(memory_space=pl.ANY),
                  pl.BlockSpec(memory_space=pl.ANY),
                  pl.BlockSpec(memory_space=pl.ANY)],
        out_specs=pl.BlockSpec(memory_space=pl.ANY),
        scratch_shapes=[pltpu.VMEM((XS_RING, r, XS_COLS), bf16),
                        pltpu.VMEM((2, D_MODEL, D_FF_E), f32),
                        pltpu.VMEM((2, D_MODEL, D_FF_E), f32),
                        pltpu.VMEM((2, D_FF_E, D_MODEL), f32),
                        pltpu.VMEM((D_MODEL, D_FF_E), bf16),
                        pltpu.VMEM((D_MODEL, D_FF_E), bf16),
                        pltpu.VMEM((D_FF_E, D_MODEL), bf16),
                        pltpu.SemaphoreType.DMA((XS_RING,)),
                        pltpu.SemaphoreType.DMA((XS_RING,)),
                        pltpu.SemaphoreType.DMA((2,))])
    return pl.pallas_call(
        _experts_body,
        grid_spec=grid_spec,
        out_shape=jax.ShapeDtypeStruct(xs.shape, xs.dtype),
        input_output_aliases={1: 0},
        compiler_params=pltpu.CompilerParams(
            dimension_semantics=("arbitrary",), vmem_limit_bytes=VMEM_LIMIT),
        name="experts",
    )(plan, xs, wg, wu, wd)


def _combine_out_body(runs_cur_ref, runs_nxt_ref, h_ref, slot_ref, p_ref, ys_ref,
                      wpg_ref, bpg_ref, wpp_ref, gple_ref, gfin_ref,
                      o_ref, ybuf_ref, moe_ref, sem_ref):
    tk = h_ref.shape[0]
    lb = ybuf_ref.shape[1]
    step = pl.program_id(0)
    n = pl.num_programs(0) - 1
    cur = jnp.minimum(step, n - 1) % 2

    def piece(s, local_row, global_row, rows):
        return pltpu.make_async_copy(ys_ref.at[pl.ds(global_row, rows), pl.ds(0, D_MODEL)],
                                     ybuf_ref.at[s, pl.ds(local_row, rows), :], sem_ref.at[s])

    @pl.when(step == 0)
    def _():
        ybuf_ref[...] = jnp.zeros(ybuf_ref.shape, bf16)
        _for_each_run_piece(runs_cur_ref, lambda l, g, r: piece(0, l, g, r).start())

    @pl.when(step + 1 < n)
    def _():
        _for_each_run_piece(runs_nxt_ref, lambda l, g, r: piece(1 - cur, l, g, r).start())

    run_rows = _tile_run_rows(runs_cur_ref)

    @pl.when(step < n)
    def _():
        _wait_rows(run_rows, lambda r: piece(cur, 0, 0, r))

    def work(rows, finish, unsort):
        if finish:
            h2 = h_ref[...] + moe_ref[...]
            gate = jax.nn.sigmoid(jnp.dot(h2.astype(bf16), wpg_ref[...], preferred_element_type=f32)
                                  + bpg_ref[...])
            pp = jnp.dot(p_ref[...].astype(bf16), wpp_ref[...], preferred_element_type=f32)
            h3 = h2 + gate * _rms(pp, gple_ref[...])
            o_ref[...] = _rms(h3, gfin_ref[...])
        if unsort:
            col = lax.broadcasted_iota(i32, (tk, rows), 1)
            unperm = jnp.where((col == slot_ref[:, 0:1]) | (col == slot_ref[:, 1:2]), 1.0, 0.0)
            moe_ref[...] = jnp.dot(unperm.astype(bf16), ybuf_ref[cur, :rows, :],
                                   preferred_element_type=f32)

    short = lb - 256
    is_short = run_rows <= short
    first = step == 0
    mid = (step > 0) & (step < n)
    pl.when(first & is_short)(functools.partial(work, short, False, True))
    pl.when(first & jnp.logical_not(is_short))(functools.partial(work, lb, False, True))
    pl.when(mid & is_short)(functools.partial(work, short, True, True))
    pl.when(mid & jnp.logical_not(is_short))(functools.partial(work, lb, True, True))
    pl.when(step == n)(functools.partial(work, lb, True, False))


def _combine_out(run_tiles, h2d, slot_t, p2d, ys, wpg, bpg, wpp, gple, gfin):
    t = h2d.shape[0]
    tk = TL_SORT
    nt = t // tk
    const = lambda shape: pl.BlockSpec(shape, lambda i: (0,) * len(shape))
    runs = lambda imap: pl.BlockSpec((1, 1, BLK), imap, memory_space=pltpu.SMEM)
    uns = lambda s: jnp.minimum(s, nt - 1)
    fin = lambda s: jnp.maximum(s - 1, 0)
    return pl.pallas_call(
        _combine_out_body,
        grid=(nt + 1,),
        out_shape=jax.ShapeDtypeStruct((t, D_MODEL), f32),
        in_specs=[runs(lambda s: (uns(s), 0, 0)),
                  runs(lambda s: (uns(s + 1), 0, 0)),
                  pl.BlockSpec((tk, D_MODEL), lambda s: (fin(s), 0)),
                  pl.BlockSpec((tk, 8), lambda s: (uns(s), 0)),
                  pl.BlockSpec((tk, PLE_DIM), lambda s: (fin(s), 0)),
                  pl.BlockSpec(memory_space=pl.ANY),
                  const((D_MODEL, D_MODEL)), const((1, D_MODEL)),
                  const((PLE_DIM, D_MODEL)), const((1, D_MODEL)), const((1, D_MODEL))],
        out_specs=pl.BlockSpec((tk, D_MODEL), lambda s: (fin(s), 0)),
        scratch_shapes=[pltpu.VMEM((2, LB_SORT, D_MODEL), bf16),
                        pltpu.VMEM((tk, D_MODEL), f32),
                        pltpu.SemaphoreType.DMA((2,))],
        compiler_params=pltpu.CompilerParams(
            dimension_semantics=("arbitrary",), vmem_limit_bytes=VMEM_LIMIT),
        name="combine_out",
    )(run_tiles, run_tiles, h2d, slot_t, p2d, ys, wpg, bpg, wpp, gple, gfin)


def _head_major_to_tile_major(a, axis):
    axis = axis % a.ndim
    shape = a.shape
    a = a.reshape(shape[:axis] + (N_KV, GQA, HEAD_DIM) + shape[axis + 1:])
    return jnp.swapaxes(a, axis, axis + 1).reshape(shape)


def kernel(x, p, rel_bias, g_mix, w_in, ln_v_g, ln_v_b, w_spatial, b_spatial, sink, g_out_grp, w_out,
           g_ffn, w_router_group, b_router_group, w_router_expert, b_router_expert, w_gate_e, w_up_e,
           w_down_e, w_ple_proj, g_ple, w_ple_gate, b_ple_gate, g_final):
    b, s, d = x.shape
    t = b * s
    depth = g_mix.shape[0]
    assert depth == 1 and d == D_MODEL
    c1, c2 = 2 * D_A, 2 * D_A + D_B
    bias = (rel_bias * LOG2E).astype(f32)
    hcur = x.astype(f32)
    for li in range(depth):
        col_scale = np.ones((1, D_IN), np.float32)
        col_scale[:, c1:c2] = LOG2E * HEAD_DIM ** -0.5
        win = (w_in[li] * col_scale).astype(bf16)
        bs = jnp.broadcast_to(b_spatial[li][:, :, None], (A_HEADS, BLK, BLK)).astype(f32)
        gout = g_out_grp[li]
        gout_b = _head_major_to_tile_major(gout[D_A:], 0)[None]
        wo = w_out[li]
        wout = jnp.concatenate([wo[:D_A], _head_major_to_tile_major(wo[D_A:], 0)],
                               axis=0).astype(bf16)
        wr = jnp.concatenate([
            w_router_group[li], jnp.zeros((D_MODEL, 8 - N_GROUPS), f32), w_router_expert[li],
            jnp.zeros((D_MODEL, BLK - ROUTE_ROWS), f32)], axis=1).astype(bf16)
        br = jnp.concatenate([b_router_group[li], jnp.full((8 - N_GROUPS,), NEG, f32),
                              b_router_expert[li]])
        br = jnp.broadcast_to(br[:, None], (ROUTE_ROWS, TQ_ATT))

        ya, q, k, v = _mix_in(hcur.reshape(t, d), g_mix[li][None], win, ln_v_g[li][None],
                              ln_v_b[li][None], w_spatial[li].astype(bf16), bs, gout[None, :D_A])
        h, m_rows, route_e, route_w = _attn_out(
            sink[li].astype(f32) * LOG2E, hcur, ya.reshape(b, s, D_A), q.reshape(b, s, D_B),
            k.reshape(b, s, BLK), v.reshape(b, s, BLK), bias, gout_b, wout,
            g_ffn[li][None], wr, br)

        r = R_EXP
        n_sort = t // TL_SORT
        max_rows = 2 * t + n_sort * N_EXPERTS * (RUN_ALIGN - 1)
        n_tiles = -(-max_rows // r) + N_EXPERTS
        assert n_tiles * r < 2 ** 16, "route_pos prefix sums hold row offsets in two bf16 bytes"
        slot, run_tiles, plan = _route_pos(route_e, r)
        xs, slot_t = _sort_rows(plan, run_tiles, m_rows, slot, route_w, n_tiles * r, r)
        ys = _experts(plan, xs, w_gate_e[li], w_up_e[li], w_down_e[li], r)

        out = _combine_out(run_tiles, h.reshape(t, d), slot_t,
                           p[li].reshape(t, PLE_DIM), ys, w_ple_gate[li].astype(bf16),
                           b_ple_gate[li][None], w_ple_proj[li].astype(bf16), g_ple[li][None],
                           g_final[None])
        hcur = out.reshape(b, s, d)
    return hcur
```

```python
import functools
import math

import jax
import jax.numpy as jnp
import numpy as np
from jax import lax
from jax.experimental import pallas as pl
from jax.experimental.pallas import tpu as pltpu

D_MODEL = 1024
D_A = 512
D_B = 512
BLK = 128
A_HEADS = 4
HEAD_DIM = 64
N_HEADS = 8
N_KV = 2
GQA = 4
WINDOW = 128
NUM_BUCKETS = 32
MAX_DIST = 128
D_IN = 2 * D_A + D_B + 2 * N_KV * HEAD_DIM
N_GROUPS = 4
E_PER_GROUP = 8
N_EXPERTS = 32
D_FF_E = 256
PLE_DIM = 256
EPS = 1e-6
NEG = -1e30
LOG2E = math.log2(math.e)

TM_MIX = 512
TQ_ATT = 512
TL_SORT = 512
RUN_ALIGN = 16
SMALL_BITS = 2
BIG_PIECE = RUN_ALIGN << SMALL_BITS
LB_SORT = 2 * TL_SORT + N_EXPERTS * RUN_ALIGN
R_EXP = 512
XS_COLS = D_MODEL + BLK
XS_RING = 4
XS_AHEAD = 2
ROUTE_ROWS = 8 + N_EXPERTS
ROUTE_UNROLL = 8
VMEM_LIMIT = 48 * 1024 * 1024

f32 = jnp.float32
bf16 = jnp.bfloat16
i32 = jnp.int32


def _rms(x, g):
    return x * lax.rsqrt(jnp.mean(x * x, axis=-1, keepdims=True) + EPS) * g


def _gelu_tanh(x):
    c = math.sqrt(2.0 / math.pi)
    return x * (0.5 * (1.0 + jnp.tanh(c * (x + 0.044715 * (x * x * x)))))


def _bucket_row():
    n = NUM_BUCKETS // 2
    max_exact = n // 2
    rel = np.broadcast_to(np.arange(4 * BLK)[None, :] - BLK, (8, 4 * BLK))
    ret = np.where(rel > 0, n, 0)
    a = np.abs(rel)
    large = max_exact + (np.log(np.maximum(a, 1).astype(np.float64) / max_exact)
                         / math.log(MAX_DIST / max_exact) * (n - max_exact)).astype(np.int32)
    large = np.minimum(large, n - 1)
    bucket = ret + np.where(a < max_exact, a, large)
    return np.where(a <= WINDOW, bucket, -1).astype(np.int32)


def _bias_body(rb_ref, bucket_ref, o_ref):
    bucket = bucket_ref[...]
    row = lax.broadcasted_iota(i32, (BLK, BLK), 0)
    lane = lax.broadcasted_iota(i32, (BLK, BLK), 1)
    for h in range(N_HEADS):
        base = jnp.full(bucket.shape, NEG, f32)
        for b in range(NUM_BUCKETS):
            base = jnp.where(bucket == b, rb_ref[b, h], base)
        prev = jnp.full((BLK, BLK), NEG, f32)
        for c in range(3):
            tile = jnp.broadcast_to(base[0:1, c * BLK:(c + 1) * BLK], (BLK, BLK))
            rolled = pltpu.roll(tile, 0, 1, stride=1, stride_axis=0)
            o_ref[h, :, c * BLK:(c + 1) * BLK] = jnp.where(lane >= row, rolled, prev)
            prev = rolled


def _mix_in_body(x_ref, gmix_ref, win_ref, lng_ref, lnb_ref, ws_ref, bs_ref, gout_ref,
                 ya_ref, q_ref, k_ref, v_ref):
    tm = x_ref.shape[0]
    nc = tm // BLK
    a = _rms(x_ref[...], gmix_ref[...])
    z = jnp.dot(a.astype(bf16), win_ref[...], preferred_element_type=f32)
    uv = _gelu_tanh(z[:, :2 * D_A])
    u = uv[:, :D_A]
    v = uv[:, D_A:]
    mu = jnp.mean(v, axis=-1, keepdims=True)
    vc = v - mu
    var = jnp.mean(vc * vc, axis=-1, keepdims=True)
    vn = (vc * lax.rsqrt(var + EPS) * lng_ref[...] + lnb_ref[...]).astype(bf16)
    cols = []
    for h in range(A_HEADS):
        rhs = jnp.concatenate(
            [vn[c * BLK:(c + 1) * BLK, h * BLK:(h + 1) * BLK] for c in range(nc)], axis=1)
        r = jnp.dot(ws_ref[h], rhs, preferred_element_type=f32)
        cols.append(jnp.concatenate(
            [r[:, c * BLK:(c + 1) * BLK] + bs_ref[h] for c in range(nc)], axis=0))
    sv = jnp.concatenate(cols, axis=1)
    ya_ref[...] = _rms(u * sv, gout_ref[...]).astype(bf16)
    qt = [z[:, 2 * D_A + j * BLK:2 * D_A + (j + 1) * BLK] for j in range(D_B // BLK)]
    low_half = lax.broadcasted_iota(i32, (tm, BLK), 1) < HEAD_DIM
    for g in range(GQA):
        a0, a1 = qt[g // 2], qt[GQA // 2 + g // 2]
        tile = (jnp.where(low_half, a0, pltpu.roll(a1, HEAD_DIM, 1)) if g % 2 == 0
                else jnp.where(low_half, pltpu.roll(a0, HEAD_DIM, 1), a1))
        q_ref[:, g * BLK:(g + 1) * BLK] = tile.astype(bf16)
    k_ref[...] = z[:, 2 * D_A + D_B:2 * D_A + D_B + BLK].astype(bf16)
    v_ref[...] = z[:, 2 * D_A + D_B + BLK:].astype(bf16)


def _mix_in(x2, gmix, win, lng, lnb, ws, bs, gout_a):
    t = x2.shape[0]
    tm = TM_MIX
    const = lambda shape: pl.BlockSpec(shape, lambda i: (0,) * len(shape))
    return pl.pallas_call(
        _mix_in_body,
        grid=(t // tm,),
        out_shape=(jax.ShapeDtypeStruct((t, D_A), bf16),
                   jax.ShapeDtypeStruct((t, D_B), bf16),
                   jax.ShapeDtypeStruct((t, BLK), bf16),
                   jax.ShapeDtypeStruct((t, BLK), bf16)),
        in_specs=[pl.BlockSpec((tm, D_MODEL), lambda i: (i, 0)),
                  const((1, D_MODEL)), const((D_MODEL, D_IN)),
                  const((1, D_A)), const((1, D_A)),
                  const((A_HEADS, BLK, BLK)), const((A_HEADS, BLK, BLK)),
                  const((1, D_A))],
        out_specs=(pl.BlockSpec((tm, D_A), lambda i: (i, 0)),
                   pl.BlockSpec((tm, D_B), lambda i: (i, 0)),
                   pl.BlockSpec((tm, BLK), lambda i: (i, 0)),
                   pl.BlockSpec((tm, BLK), lambda i: (i, 0))),
        compiler_params=pltpu.CompilerParams(
            dimension_semantics=("parallel",), vmem_limit_bytes=VMEM_LIMIT),
        name="mix_in",
    )(x2, gmix, win, lng, lnb, ws, bs, gout_a)


def _attn_epilogue(sink_ref, x_ref, ya_ref, q_ref, kp_ref, km_ref, kn_ref, vp_ref, vm_ref, vn_ref,
                   rb_ref, bucket_ref, goutb_ref, wout_ref, gffn_ref, wr_ref, br_ref,
                   h_ref, m_ref, re_ref, rw_ref,
                   kf_ref, vf_ref, e_ref, yb_ref, bias_ref, xring_ref, xsem_ref, *, seq_len):
    tq = ya_ref.shape[1]
    slot = (pl.program_id(0) - 1) % 3
    pltpu.make_async_copy(x_ref.at[pl.ds(0, tq), :], xring_ref.at[slot], xsem_ref.at[slot]).wait()
    ybn = _rms(yb_ref[1], goutb_ref[...]).astype(bf16)
    y = jnp.concatenate([ya_ref[0], ybn], axis=1)
    h = xring_ref[slot] + jnp.dot(y, wout_ref[...], preferred_element_type=f32)
    h_ref[0] = h
    m = _rms(h, gffn_ref[...])
    m_ref[...] = m.astype(bf16)

    logit_t = jnp.dot(m.astype(bf16), wr_ref[...], preferred_element_type=f32)
    logit = jnp.transpose(logit_t)[:ROUTE_ROWS] + br_ref[...]
    sub = lax.broadcasted_iota(i32, (8, tq), 0)
    lg = logit[0:8]
    mg = jnp.max(lg, axis=0, keepdims=True)
    pg_top = 1.0 / jnp.sum(jnp.exp(lg - mg), axis=0, keepdims=True)
    g_idx = jnp.min(jnp.where(lg == mg, sub, 8), axis=0, keepdims=True)
    sel = logit[8:16]
    for g in range(1, N_GROUPS):
        sel = jnp.where(g_idx == g, logit[8 + 8 * g:16 + 8 * g], sel)
    m1 = jnp.max(sel, axis=0, keepdims=True)
    i1 = jnp.min(jnp.where(sel == m1, sub, 8), axis=0, keepdims=True)
    sel2 = jnp.where(sub == i1, -jnp.inf, sel)
    m2 = jnp.max(sel2, axis=0, keepdims=True)
    i2 = jnp.min(jnp.where(sel2 == m2, sub, 8), axis=0, keepdims=True)
    r = jnp.exp(m2 - m1)
    w1 = pg_top / (1.0 + r)
    w2 = pg_top * r / (1.0 + r)
    e1 = g_idx * E_PER_GROUP + i1
    e2 = g_idx * E_PER_GROUP + i2
    re_ref[...] = jnp.where(sub == 0, e1, jnp.where(sub == 1, e2, 0))
    rw_ref[...] = jnp.where(sub == 0, w1, jnp.where(sub == 1, w2, 0.0))


def _attn_attention(sink_ref, x_ref, ya_ref, q_ref, kp_ref, km_ref, kn_ref, vp_ref, vm_ref, vn_ref,
                    rb_ref, bucket_ref, goutb_ref, wout_ref, gffn_ref, wr_ref, br_ref,
                    h_ref, m_ref, re_ref, rw_ref,
                    kf_ref, vf_ref, e_ref, yb_ref, bias_ref, xring_ref, xsem_ref, *, seq_len):
    tq = q_ref.shape[1]
    nb = tq // BLK
    ti = pl.program_id(0) % (seq_len // tq)
    kf_ref[0:BLK] = kp_ref[0]
    kf_ref[BLK:BLK + tq] = km_ref[0]
    kf_ref[BLK + tq:] = kn_ref[0]
    vlane = lax.broadcasted_iota(i32, (BLK, BLK), 1)
    for src, r0, r1 in ((vp_ref, 0, BLK), (vn_ref, BLK + tq, tq + 2 * BLK)) + tuple(
            (vm_ref.at[:, pl.ds(c * BLK, BLK), :], BLK + c * BLK, BLK + (c + 1) * BLK)
            for c in range(nb)):
        vals = src[0].astype(f32)
        vf_ref[0, r0:r1] = jnp.where(vlane < HEAD_DIM, vals,
                                     jnp.where(vlane == HEAD_DIM, 1.0, 0.0)).astype(bf16)
        vf_ref[1, r0:r1] = jnp.where(vlane >= HEAD_DIM, vals,
                                     jnp.where(vlane == 0, 1.0, 0.0)).astype(bf16)

    lane = lax.broadcasted_iota(i32, (BLK, BLK), 1)
    low_half = lane < HEAD_DIM

    for n in range(nb):
        r0 = n * BLK
        qb = q_ref[0, pl.ds(r0, BLK), :]
        kb = kf_ref[pl.ds(r0, 3 * BLK), :]
        zero = jnp.zeros((BLK, BLK), bf16)
        lhs = []
        for kh in range(N_KV):
            for g in range(GQA):
                tile = qb[:, g * BLK:(g + 1) * BLK]
                lhs.append(jnp.where(low_half, tile, zero) if kh == 0
                           else jnp.where(low_half, zero, tile))
        lhs = jnp.concatenate(lhs, axis=0)
        s_all = lax.dot_general(lhs, kb, (((1,), (1,)), ((), ())),
                                preferred_element_type=f32)
        blk = ti * nb + n
        pen = [jnp.where(blk == 0, NEG, 0.0) if n == 0 else None, None,
               jnp.where((blk + 1) * BLK >= seq_len, NEG, 0.0) if n == nb - 1 else None]
        sink_e = []
        for h in range(N_HEADS):
            bias_h = bias_ref[h]
            bias_h = jnp.concatenate(
                [bias_h[:, j * BLK:(j + 1) * BLK] if pen[j] is None
                 else bias_h[:, j * BLK:(j + 1) * BLK] + pen[j] for j in range(3)], axis=1)
            s = s_all[h * BLK:(h + 1) * BLK] + bias_h
            sk = sink_ref[h]
            mrow = jnp.maximum(jnp.max(s, axis=-1, keepdims=True), sk)
            e_ref[n, h * BLK:(h + 1) * BLK, :] = jnp.exp2(s - mrow).astype(bf16)
            sink_e.append(jnp.exp2(sk - mrow))
        half = GQA * BLK
        pv = [jnp.dot(e_ref[n, kh * half:(kh + 1) * half, :], vf_ref[kh, pl.ds(r0, 3 * BLK), :],
                      preferred_element_type=f32) for kh in range(N_KV)]

        def head_out(h):
            kh, g = divmod(h, GQA)
            rows = pv[kh][g * BLK:(g + 1) * BLK]
            ones_col = HEAD_DIM if kh == 0 else 0
            return rows * (1.0 / (rows[:, ones_col:ones_col + 1] + sink_e[h]))

        for g in range(GQA):
            yb_ref[0, pl.ds(r0, BLK), g * BLK:(g + 1) * BLK] = jnp.where(
                low_half, head_out(g), head_out(GQA + g))

    yb_ref[1] = yb_ref[0]


def _attn_out_body(*refs, seq_len):
    step = pl.program_id(0)
    n_tok_tiles = pl.num_programs(0) - 1
    x_hbm, xring_ref, xsem_ref = refs[1], refs[-2], refs[-1]
    tq = xring_ref.shape[1]

    def x_copy(tile):
        return pltpu.make_async_copy(x_hbm.at[pl.ds(pl.multiple_of(tile * tq, tq), tq), :],
                                     xring_ref.at[tile % 3], xsem_ref.at[tile % 3])

    @pl.when(step == 0)
    def _():
        x_copy(0).start()

    @pl.when(step + 1 < n_tok_tiles)
    def _():
        x_copy(step + 1).start()

    @pl.when(step == 0)
    def _():
        _bias_body(refs[10], refs[11], refs[-3])
        _attn_attention(*refs, seq_len=seq_len)

    @pl.when((step > 0) & (step < n_tok_tiles))
    def _():
        _attn_epilogue(*refs, seq_len=seq_len)
        _attn_attention(*refs, seq_len=seq_len)

    @pl.when(step == n_tok_tiles)
    def _():
        _attn_epilogue(*refs, seq_len=seq_len)


def _attn_out(sink, x, ya, q, k, v, rel_bias, goutb, wout, gffn, wr, br):
    b, s, _ = x.shape
    tq = TQ_ATT
    nb = tq // BLK
    nblk = s // BLK
    t = b * s
    nq = s // tq
    n_tok_tiles = b * nq
    att = lambda s: jnp.minimum(s, n_tok_tiles - 1)
    epi = lambda s: jnp.maximum(s - 1, 0)
    const = lambda shape: pl.BlockSpec(shape, lambda s, *_: (0,) * len(shape))
    tok_att = lambda w: pl.BlockSpec((1, tq, w), lambda s, *_: (att(s) // nq, att(s) % nq, 0))
    tok_epi = lambda w: pl.BlockSpec((1, tq, w), lambda s, *_: (epi(s) // nq, epi(s) % nq, 0))
    prev = pl.BlockSpec((1, BLK, BLK), lambda s, *_: (
        att(s) // nq, jnp.maximum((att(s) % nq) * nb - 1, 0), 0))
    nxt = pl.BlockSpec((1, BLK, BLK), lambda s, *_: (
        att(s) // nq, jnp.minimum((att(s) % nq) * nb + nb, nblk - 1), 0))
    rows = pl.BlockSpec((tq, D_MODEL), lambda s, *_: (epi(s), 0))
    lanes = lambda rows: pl.BlockSpec((rows, tq), lambda s, *_: (0, epi(s)))
    grid_spec = pltpu.PrefetchScalarGridSpec(
        num_scalar_prefetch=1,
        grid=(n_tok_tiles + 1,),
        in_specs=[pl.BlockSpec(memory_space=pl.ANY), tok_epi(D_A), tok_att(D_B),
                  prev, tok_att(BLK), nxt, prev, tok_att(BLK), nxt,
                  pl.BlockSpec(memory_space=pltpu.SMEM), const((8, 4 * BLK)),
                  const((1, D_B)), const((D_MODEL, D_MODEL)),
                  const((1, D_MODEL)), const((D_MODEL, BLK)), const((ROUTE_ROWS, tq))],
        out_specs=(tok_epi(D_MODEL), rows, lanes(8), lanes(8)),
        scratch_shapes=[pltpu.VMEM((tq + 2 * BLK, BLK), bf16),
                        pltpu.VMEM((N_KV, tq + 2 * BLK, BLK), bf16),
                        pltpu.VMEM((nb, N_HEADS * BLK, 3 * BLK), bf16),
                        pltpu.VMEM((2, tq, D_B), f32),
                        pltpu.VMEM((N_HEADS, BLK, 3 * BLK), f32),
                        pltpu.VMEM((3, tq, D_MODEL), f32),
                        pltpu.SemaphoreType.DMA((3,))])
    return pl.pallas_call(
        functools.partial(_attn_out_body, seq_len=s),
        grid_spec=grid_spec,
        out_shape=(jax.ShapeDtypeStruct((b, s, D_MODEL), f32),
                   jax.ShapeDtypeStruct((t, D_MODEL), bf16),
                   jax.ShapeDtypeStruct((8, t), i32),
                   jax.ShapeDtypeStruct((8, t), f32)),
        compiler_params=pltpu.CompilerParams(
            dimension_semantics=("arbitrary",), vmem_limit_bytes=VMEM_LIMIT),
        name="attn_out",
    )(sink, x.reshape(t, D_MODEL), ya, q, k, k, k, v, v, v, rel_bias, jnp.asarray(_bucket_row()), goutb, wout, gffn, wr, br)


def _exclusive_prefix(vals):
    a = lax.broadcasted_iota(i32, (N_EXPERTS, N_EXPERTS), 0)
    c = lax.broadcasted_iota(i32, (N_EXPERTS, N_EXPERTS), 1)
    low = jnp.where(c < a, 1.0, 0.0).astype(bf16)
    hi = (vals >> 8).astype(f32).astype(bf16)
    lo = (vals & 255).astype(f32).astype(bf16)
    return (jnp.dot(low, hi, preferred_element_type=f32) * 256.0
            + jnp.dot(low, lo, preferred_element_type=f32)).astype(i32)


def _round_up_pow2(vals, mult):
    log_m = mult.bit_length() - 1
    return ((vals + (mult - 1)) >> log_m) << log_m


PLAN_TILE_EXP, PLAN_FIRST, PLAN_NEXT_EXP, PLAN_SLOT, PLAN_N_USED, PLAN_ENDS, PLAN_PADDED, PLAN_TOTAL = (
    range(8))


def _lane_row(vals, lane_off=0):
    sub = lax.broadcasted_iota(i32, vals.shape, 0)
    lane = lax.broadcasted_iota(i32, vals.shape, 1)
    return jnp.sum(jnp.where(lane == sub + lane_off, vals, 0), axis=0, keepdims=True)


def _route_pos_body(re_ref, slot_ref, runs_ref, plan_ref, tri_ref, *, rows_per_tile, tl):
    n_tok_tiles = re_ref.shape[1] // tl
    a = lax.broadcasted_iota(i32, (tl, tl), 0)
    c = lax.broadcasted_iota(i32, (tl, tl), 1)
    tri_ref[...] = jnp.where(a < c, 1.0, 0.0).astype(bf16)
    eid = lax.broadcasted_iota(i32, (N_EXPERTS, tl), 0)

    def tile_hits(j):
        cols = pl.ds(pl.multiple_of(j * tl, tl), tl)
        hit1 = eid == re_ref[0:1, cols]
        hit2 = eid == re_ref[1:2, cols]
        onehot = jnp.where(hit1 | hit2, 1.0, 0.0)
        cnt = jnp.sum(onehot, axis=1, keepdims=True).astype(i32)
        run_len = jnp.broadcast_to(_round_up_pow2(cnt, RUN_ALIGN), (N_EXPERTS, BLK))
        return cols, hit1, hit2, onehot, run_len

    total = lax.fori_loop(0, n_tok_tiles, lambda j, acc: acc + tile_hits(j)[4],
                          jnp.zeros((N_EXPERTS, BLK), i32), unroll=ROUTE_UNROLL)
    padded = _round_up_pow2(total, rows_per_tile)
    starts = _exclusive_prefix(padded)
    ends = starts + padded

    log_r = rows_per_tile.bit_length() - 1
    sub = lax.broadcasted_iota(i32, (N_EXPERTS, BLK), 0)
    lane = lax.broadcasted_iota(i32, (1, BLK), 1)
    n_used = ends[N_EXPERTS - 1:N_EXPERTS, :] >> log_r
    tile_idx = jnp.minimum(lane, n_used - 1)
    tile_exp = jnp.minimum(
        jnp.sum((ends <= (tile_idx << log_r)).astype(i32), axis=0, keepdims=True), N_EXPERTS - 1)
    nonempty = padded > 0
    regions_before = jnp.sum(((sub < tile_exp) & nonempty).astype(i32), axis=0, keepdims=True)
    nxt = jnp.min(jnp.where((sub > tile_exp) & nonempty, sub, N_EXPERTS), axis=0, keepdims=True)
    prev_exp = pltpu.roll(jnp.broadcast_to(tile_exp, (8, BLK)), 1, 1)[0:1]
    plan_ref[PLAN_TILE_EXP:PLAN_TILE_EXP + 1] = tile_exp
    plan_ref[PLAN_FIRST:PLAN_FIRST + 1] = ((lane == 0) | (tile_exp != prev_exp)).astype(i32)
    plan_ref[PLAN_NEXT_EXP:PLAN_NEXT_EXP + 1] = jnp.where(nxt < N_EXPERTS, nxt, -1)
    plan_ref[PLAN_SLOT:PLAN_SLOT + 1] = regions_before & 1
    plan_ref[PLAN_N_USED:PLAN_N_USED + 1] = n_used
    plan_ref[PLAN_ENDS:PLAN_ENDS + 1] = _lane_row(ends)
    plan_ref[PLAN_PADDED:PLAN_PADDED + 1] = _lane_row(padded)
    plan_ref[PLAN_TOTAL:PLAN_TOTAL + 1] = _lane_row(total)

    def place(j, global_off):
        cols, hit1, hit2, onehot, run_len = tile_hits(j)
        local_off = _exclusive_prefix(run_len)
        before = jnp.dot(onehot.astype(bf16), tri_ref[...], preferred_element_type=f32)
        slot = before + local_off[:, 0:1].astype(f32)
        s1 = jnp.sum(jnp.where(hit1, slot, 0.0), axis=0, keepdims=True)
        s2 = jnp.sum(jnp.where(hit2, slot, 0.0), axis=0, keepdims=True)
        sub = lax.broadcasted_iota(i32, (8, tl), 0)
        slot_ref[:, cols] = jnp.where(sub == 0, s1.astype(i32),
                                      jnp.where(sub == 1, s2.astype(i32), 0))
        tile_rows = jnp.sum(run_len, axis=0, keepdims=True)
        lane = lax.broadcasted_iota(i32, (1, BLK), 1)
        runs_ref[j] = (_lane_row(local_off) + _lane_row(run_len, N_EXPERTS)
                       + _lane_row(global_off, 2 * N_EXPERTS)
                       + jnp.where(lane == 3 * N_EXPERTS, tile_rows, 0))
        return global_off + run_len

    lax.fori_loop(0, n_tok_tiles, place, starts, unroll=ROUTE_UNROLL)


def _route_pos(route_e, rows_per_tile):
    t = route_e.shape[1]
    tl = TL_SORT
    return pl.pallas_call(
        functools.partial(_route_pos_body, rows_per_tile=rows_per_tile, tl=tl),
        out_shape=(jax.ShapeDtypeStruct((8, t), i32),
                   jax.ShapeDtypeStruct((t // tl, 1, BLK), i32),
                   jax.ShapeDtypeStruct((8, BLK), i32)),
        scratch_shapes=[pltpu.VMEM((tl, tl), bf16)],
        compiler_params=pltpu.CompilerParams(vmem_limit_bytes=VMEM_LIMIT),
        name="route_pos",
    )(route_e)


def _for_each_run_piece(runs_ref, fn):
    def run(e):
        return (runs_ref[0, 0, e], runs_ref[0, 0, N_EXPERTS + e], runs_ref[0, 0, 2 * N_EXPERTS + e])

    def per_expert(e, carry):
        local_off, length, global_off = carry
        carry = run(e + 1)
        units = length >> (RUN_ALIGN.bit_length() - 1)

        def big_piece(k, inner):
            done = k * BIG_PIECE
            fn(pl.multiple_of(local_off + done, RUN_ALIGN),
               pl.multiple_of(global_off + done, RUN_ALIGN), BIG_PIECE)
            return inner

        lax.fori_loop(0, units >> SMALL_BITS, big_piece, 0)
        for b in range(SMALL_BITS):
            @pl.when(((units >> b) & 1) == 1)
            def _():
                done = ((units >> (b + 1)) << (b + 1)) * RUN_ALIGN
                fn(pl.multiple_of(local_off + done, RUN_ALIGN),
                   pl.multiple_of(global_off + done, RUN_ALIGN), RUN_ALIGN << b)
        return carry

    lax.fori_loop(0, N_EXPERTS, per_expert, run(0))


def _tile_run_rows(runs_ref):
    return runs_ref[0, 0, 3 * N_EXPERTS]


def _wait_rows(total_rows, make_copy):
    units = total_rows >> (RUN_ALIGN.bit_length() - 1)
    for b in range((LB_SORT // RUN_ALIGN).bit_length()):
        @pl.when(((units >> b) & 1) == 1)
        def _():
            make_copy(RUN_ALIGN << b).wait()


def _sort_rows_body(plan_ref, runs_ref, m_ref, slot_ref, rw_ref,
                    xs_ref, slot_t_ref,
                    local_ref, zero_ref, pending_ref, sem_ref, zsem_ref, *, rows_per_tile):
    tl = m_ref.shape[0]
    lb = local_ref.shape[1]
    step = pl.program_id(0)
    cur = step % 2
    n_tiles = xs_ref.shape[0] // rows_per_tile
    n_used = plan_ref[PLAN_N_USED, 0]

    def zero_rows(start, rows, wait):
        cp = pltpu.make_async_copy(zero_ref.at[pl.ds(0, rows), :],
                                   xs_ref.at[pl.ds(pl.multiple_of(start, RUN_ALIGN), rows), :],
                                   zsem_ref)
        cp.wait() if wait else cp.start()

    def zero_fill(wait):
        for e in range(N_EXPERTS):
            pad_rows = plan_ref[PLAN_PADDED, e] - plan_ref[PLAN_TOTAL, e]
            pad_units = pad_rows >> (RUN_ALIGN.bit_length() - 1)
            for bit in range((rows_per_tile // RUN_ALIGN).bit_length() - 1):
                @pl.when(((pad_units >> bit) & 1) == 1)
                def _():
                    done = ((pad_units >> (bit + 1)) << (bit + 1)) * RUN_ALIGN
                    zero_rows(plan_ref[PLAN_ENDS, e] - pad_rows + done,
                              RUN_ALIGN << bit, wait)

        def spare(j, carry):
            zero_rows((n_used + j) * rows_per_tile, rows_per_tile, wait)
            return carry

        lax.fori_loop(0, n_tiles - n_used, spare, 0)

    @pl.when(step == 0)
    def _():
        zero_ref[...] = jnp.zeros(zero_ref.shape, bf16)
        zero_fill(wait=False)

    s1 = slot_ref[0:1, :]
    s2 = slot_ref[1:2, :]
    w1 = rw_ref[0:1, :]
    w2 = rw_ref[1:2, :]
    run_rows = _tile_run_rows(runs_ref)

    def sort_rows_upto(n):
        srow = lax.broadcasted_iota(i32, (n, tl), 0)
        p1 = srow == s1
        p2 = srow == s2
        perm = jnp.where(p1 | p2, 1.0, 0.0).astype(bf16)
        rows = jnp.dot(perm, m_ref[...], preferred_element_type=f32)
        w = jnp.sum(jnp.where(p1, w1, 0.0) + jnp.where(p2, w2, 0.0), axis=1, keepdims=True)
        w_hi = w.astype(bf16).astype(f32)
        w_lo = w - w_hi
        lane = lax.broadcasted_iota(i32, (n, BLK), 1)
        local_ref[cur, :n, :D_MODEL] = rows.astype(bf16)
        local_ref[cur, :n, D_MODEL:] = jnp.where(
            lane == 0, w_hi, jnp.where(lane == 1, w_lo, 0.0)).astype(bf16)
        sub_t = lax.broadcasted_iota(i32, (BLK, tl), 0)
        slots = jnp.where(sub_t == 0, s1.astype(f32), jnp.where(sub_t == 1, s2.astype(f32), 0.0))
        slot_t_ref[...] = jnp.transpose(slots)[:, :8].astype(i32)

    def piece(s, local_row, global_row, rows):
        return pltpu.make_async_copy(local_ref.at[s, pl.ds(local_row, rows), :],
                                     xs_ref.at[pl.ds(global_row, rows), :], sem_ref.at[s])

    short = lb - 256
    pl.when(run_rows <= short)(functools.partial(sort_rows_upto, short))
    pl.when(run_rows > short)(functools.partial(sort_rows_upto, lb))

    _for_each_run_piece(runs_ref, lambda l, g, n: piece(cur, l, g, n).start())

    @pl.when(step > 0)
    def _():
        _wait_rows(pending_ref[0], lambda n: piece(1 - cur, 0, 0, n))

    pending_ref[0] = run_rows

    @pl.when(step == pl.num_programs(0) - 1)
    def _():
        _wait_rows(pending_ref[0], lambda n: piece(cur, 0, 0, n))
        zero_fill(wait=True)


def _sort_rows(plan, run_tiles, m, slot, route_w, total_rows, rows_per_tile):
    t = m.shape[0]
    tl = TL_SORT
    grid_spec = pltpu.PrefetchScalarGridSpec(
        num_scalar_prefetch=1,
        grid=(t // tl,),
        in_specs=[pl.BlockSpec((1, 1, BLK), lambda i, *_: (i, 0, 0),
                               memory_space=pltpu.SMEM),
                  pl.BlockSpec((tl, D_MODEL), lambda i, *_: (i, 0)),
                  pl.BlockSpec((8, tl), lambda i, *_: (0, i)),
                  pl.BlockSpec((8, tl), lambda i, *_: (0, i))],
        out_specs=(pl.BlockSpec(memory_space=pl.ANY),
                   pl.BlockSpec((tl, 8), lambda i, *_: (i, 0))),
        scratch_shapes=[pltpu.VMEM((2, LB_SORT, XS_COLS), bf16),
                        pltpu.VMEM((rows_per_tile, XS_COLS), bf16),
                        pltpu.SMEM((1,), i32),
                        pltpu.SemaphoreType.DMA((2,)),
                        pltpu.SemaphoreType.DMA(())])
    return pl.pallas_call(
        functools.partial(_sort_rows_body, rows_per_tile=rows_per_tile),
        grid_spec=grid_spec,
        out_shape=(jax.ShapeDtypeStruct((total_rows, XS_COLS), bf16),
                   jax.ShapeDtypeStruct((t, 8), i32)),
        compiler_params=pltpu.CompilerParams(
            dimension_semantics=("arbitrary",), vmem_limit_bytes=VMEM_LIMIT),
        name="sort_rows",
    )(plan, run_tiles, m, slot, route_w)


def _experts_body(plan_ref,
                  xs_hbm, wg_hbm, wu_hbm, wd_hbm, ys_hbm,
                  xbuf_ref, sg_ref, su_ref, sd_ref, wg_ref, wu_ref, wd_ref,
                  xsem_ref, osem_ref, sem_ref):
    i = pl.program_id(0)
    n_used = plan_ref[PLAN_N_USED, 0]
    r = xbuf_ref.shape[1]

    def tile_rows(ref, tile):
        return ref.at[pl.ds(pl.multiple_of(tile * r, r), r), :]

    def fetch_tile(tile):
        s = tile % XS_RING
        return pltpu.make_async_copy(tile_rows(xs_hbm, tile), xbuf_ref.at[s], xsem_ref.at[s])

    def store_tile(tile):
        s = tile % XS_RING
        return pltpu.make_async_copy(xbuf_ref.at[s], tile_rows(ys_hbm, tile), osem_ref.at[s])

    @pl.when(i == 0)
    def _():
        for ahead in range(XS_AHEAD):
            @pl.when(ahead < n_used)
            def _():
                fetch_tile(ahead).start()

    @pl.when(i + XS_AHEAD < n_used)
    def _():
        @pl.when(i + XS_AHEAD >= XS_RING)
        def _():
            store_tile(i + XS_AHEAD - XS_RING).wait()

        fetch_tile(i + XS_AHEAD).start()

    def fetches(expert, s):
        return (pltpu.make_async_copy(wg_hbm.at[expert], sg_ref.at[s], sem_ref.at[s]),
                pltpu.make_async_copy(wu_hbm.at[expert], su_ref.at[s], sem_ref.at[s]),
                pltpu.make_async_copy(wd_hbm.at[expert], sd_ref.at[s], sem_ref.at[s]))

    @pl.when(i < n_used)
    def _():
        s = plan_ref[PLAN_SLOT, i]
        fetch_tile(i).wait()
        xs_ref = xbuf_ref.at[i % XS_RING]

        @pl.when(plan_ref[PLAN_FIRST, i] == 1)
        def _():
            @pl.when(i == 0)
            def _():
                for cp in fetches(plan_ref[PLAN_TILE_EXP, 0], 0):
                    cp.start()

            for cp in fetches(plan_ref[PLAN_TILE_EXP, i], s):
                cp.wait()

            @pl.when(plan_ref[PLAN_NEXT_EXP, i] >= 0)
            def _():
                for cp in fetches(plan_ref[PLAN_NEXT_EXP, i], 1 - s):
                    cp.start()

            wg_ref[...] = sg_ref[s].astype(bf16)
            wu_ref[...] = su_ref[s].astype(bf16)
            wd_ref[...] = sd_ref[s].astype(bf16)

        w_row = (xs_ref[:, D_MODEL:D_MODEL + 1].astype(f32)
                 + xs_ref[:, D_MODEL + 1:D_MODEL + 2].astype(f32))
        gate = jnp.dot(xs_ref[:, :D_MODEL], wg_ref[...], preferred_element_type=f32)
        up = jnp.dot(xs_ref[:, :D_MODEL], wu_ref[...], preferred_element_type=f32)
        hdn = (gate * jax.nn.sigmoid(gate)) * up
        y = jnp.dot(hdn.astype(bf16), wd_ref[...], preferred_element_type=f32)
        xs_ref[:, :D_MODEL] = (y * w_row).astype(bf16)
        store_tile(i).start()

    @pl.when(i == pl.num_programs(0) - 1)
    def _():
        for back in range(1, XS_RING + 1):
            @pl.when(n_used - back >= 0)
            def _():
                store_tile(n_used - back).wait()


def _experts(plan, xs, wg, wu, wd, rows_per_tile):
    r = rows_per_tile
    n_tiles = xs.shape[0] // r
    assert n_tiles <= BLK, "the plan holds one row tile per lane"
    grid_spec = pltpu.PrefetchScalarGridSpec(
        num_scalar_prefetch=1,
        grid=(n_tiles,),
        in_specs=[pl.BlockSpec(memory_space=pl.ANY),
                  pl.BlockSpec(memory_space=pl.ANY),
                  pl.BlockSpec(memory_space=pl.ANY),
                  pl.BlockSpec(memory_space=pl.ANY)],
        out_specs=pl.BlockSpec(memory_space=pl.ANY),
        scratch_shapes=[pltpu.VMEM((XS_RING, r, XS_COLS), bf16),
                        pltpu.VMEM((2, D_MODEL, D_FF_E), f32),
                        pltpu.VMEM((2, D_MODEL, D_FF_E), f32),
                        pltpu.VMEM((2, D_FF_E, D_MODEL), f32),
                        pltpu.VMEM((D_MODEL, D_FF_E), bf16),
                        pltpu.VMEM((D_MODEL, D_FF_E), bf16),
                        pltpu.VMEM((D_FF_E, D_MODEL), bf16),
                        pltpu.SemaphoreType.DMA((XS_RING,)),
                        pltpu.SemaphoreType.DMA((XS_RING,)),
                        pltpu.SemaphoreType.DMA((2,))])
    return pl.pallas_call(
        _experts_body,
        grid_spec=grid_spec,
        out_shape=jax.ShapeDtypeStruct(xs.shape, xs.dtype),
        input_output_aliases={1: 0},
        compiler_params=pltpu.CompilerParams(
            dimension_semantics=("arbitrary",), vmem_limit_bytes=VMEM_LIMIT),
        name="experts",
    )(plan, xs, wg, wu, wd)


def _combine_out_body(runs_cur_ref, runs_nxt_ref, h_ref, slot_ref, p_ref, ys_ref,
                      wpg_ref, bpg_ref, wpp_ref, gple_ref, gfin_ref,
                      o_ref, ybuf_ref, moe_ref, sem_ref):
    tk = h_ref.shape[0]
    lb = ybuf_ref.shape[1]
    step = pl.program_id(0)
    n = pl.num_programs(0) - 1
    cur = jnp.minimum(step, n - 1) % 2

    def piece(s, local_row, global_row, rows):
        return pltpu.make_async_copy(ys_ref.at[pl.ds(global_row, rows), pl.ds(0, D_MODEL)],
                                     ybuf_ref.at[s, pl.ds(local_row, rows), :], sem_ref.at[s])

    @pl.when(step == 0)
    def _():
        ybuf_ref[...] = jnp.zeros(ybuf_ref.shape, bf16)
        _for_each_run_piece(runs_cur_ref, lambda l, g, r: piece(0, l, g, r).start())

    @pl.when(step + 1 < n)
    def _():
        _for_each_run_piece(runs_nxt_ref, lambda l, g, r: piece(1 - cur, l, g, r).start())

    run_rows = _tile_run_rows(runs_cur_ref)

    @pl.when(step < n)
    def _():
        _wait_rows(run_rows, lambda r: piece(cur, 0, 0, r))

    def work(rows, finish, unsort):
        if finish:
            h2 = h_ref[...] + moe_ref[...]
            gate = jax.nn.sigmoid(jnp.dot(h2.astype(bf16), wpg_ref[...], preferred_element_type=f32)
                                  + bpg_ref[...])
            pp = jnp.dot(p_ref[...].astype(bf16), wpp_ref[...], preferred_element_type=f32)
            h3 = h2 + gate * _rms(pp, gple_ref[...])
            o_ref[...] = _rms(h3, gfin_ref[...])
        if unsort:
            col = lax.broadcasted_iota(i32, (tk, rows), 1)
            unperm = jnp.where((col == slot_ref[:, 0:1]) | (col == slot_ref[:, 1:2]), 1.0, 0.0)
            moe_ref[...] = jnp.dot(unperm.astype(bf16), ybuf_ref[cur, :rows, :],
                                   preferred_element_type=f32)

    short = lb - 256
    is_short = run_rows <= short
    first = step == 0
    mid = (step > 0) & (step < n)
    pl.when(first & is_short)(functools.partial(work, short, False, True))
    pl.when(first & jnp.logical_not(is_short))(functools.partial(work, lb, False, True))
    pl.when(mid & is_short)(functools.partial(work, short, True, True))
    pl.when(mid & jnp.logical_not(is_short))(functools.partial(work, lb, True, True))
    pl.when(step == n)(functools.partial(work, lb, True, False))


def _combine_out(run_tiles, h2d, slot_t, p2d, ys, wpg, bpg, wpp, gple, gfin):
    t = h2d.shape[0]
    tk = TL_SORT
    nt = t // tk
    const = lambda shape: pl.BlockSpec(shape, lambda i: (0,) * len(shape))
    runs = lambda imap: pl.BlockSpec((1, 1, BLK), imap, memory_space=pltpu.SMEM)
    uns = lambda s: jnp.minimum(s, nt - 1)
    fin = lambda s: jnp.maximum(s - 1, 0)
    return pl.pallas_call(
        _combine_out_body,
        grid=(nt + 1,),
        out_shape=jax.ShapeDtypeStruct((t, D_MODEL), f32),
        in_specs=[runs(lambda s: (uns(s), 0, 0)),
                  runs(lambda s: (uns(s + 1), 0, 0)),
                  pl.BlockSpec((tk, D_MODEL), lambda s: (fin(s), 0)),
                  pl.BlockSpec((tk, 8), lambda s: (uns(s), 0)),
                  pl.BlockSpec((tk, PLE_DIM), lambda s: (fin(s), 0)),
                  pl.BlockSpec(memory_space=pl.ANY),
                  const((D_MODEL, D_MODEL)), const((1, D_MODEL)),
                  const((PLE_DIM, D_MODEL)), const((1, D_MODEL)), const((1, D_MODEL))],
        out_specs=pl.BlockSpec((tk, D_MODEL), lambda s: (fin(s), 0)),
        scratch_shapes=[pltpu.VMEM((2, LB_SORT, D_MODEL), bf16),
                        pltpu.VMEM((tk, D_MODEL), f32),
                        pltpu.SemaphoreType.DMA((2,))],
        compiler_params=pltpu.CompilerParams(
            dimension_semantics=("arbitrary",), vmem_limit_bytes=VMEM_LIMIT),
        name="combine_out",
    )(run_tiles, run_tiles, h2d, slot_t, p2d, ys, wpg, bpg, wpp, gple, gfin)


def _head_major_to_tile_major(a, axis):
    axis = axis % a.ndim
    shape = a.shape
    a = a.reshape(shape[:axis] + (N_KV, GQA, HEAD_DIM) + shape[axis + 1:])
    return jnp.swapaxes(a, axis, axis + 1).reshape(shape)


def kernel(x, p, rel_bias, g_mix, w_in, ln_v_g, ln_v_b, w_spatial, b_spatial, sink, g_out_grp, w_out,
           g_ffn, w_router_group, b_router_group, w_router_expert, b_router_expert, w_gate_e, w_up_e,
           w_down_e, w_ple_proj, g_ple, w_ple_gate, b_ple_gate, g_final):
    b, s, d = x.shape
    t = b * s
    depth = g_mix.shape[0]
    assert depth == 1 and d == D_MODEL
    c1, c2 = 2 * D_A, 2 * D_A + D_B
    bias = (rel_bias * LOG2E).astype(f32)
    hcur = x.astype(f32)
    for li in range(depth):
        col_scale = np.ones((1, D_IN), np.float32)
        col_scale[:, c1:c2] = LOG2E * HEAD_DIM ** -0.5
        win = (w_in[li] * col_scale).astype(bf16)
        bs = jnp.broadcast_to(b_spatial[li][:, :, None], (A_HEADS, BLK, BLK)).astype(f32)
        gout = g_out_grp[li]
        gout_b = _head_major_to_tile_major(gout[D_A:], 0)[None]
        wo = w_out[li]
        wout = jnp.concatenate([wo[:D_A], _head_major_to_tile_major(wo[D_A:], 0)],
                               axis=0).astype(bf16)
        wr = jnp.concatenate([
            w_router_group[li], jnp.zeros((D_MODEL, 8 - N_GROUPS), f32), w_router_expert[li],
            jnp.zeros((D_MODEL, BLK - ROUTE_ROWS), f32)], axis=1).astype(bf16)
        br = jnp.concatenate([b_router_group[li], jnp.full((8 - N_GROUPS,), NEG, f32),
                              b_router_expert[li]])
        br = jnp.broadcast_to(br[:, None], (ROUTE_ROWS, TQ_ATT))

        ya, q, k, v = _mix_in(hcur.reshape(t, d), g_mix[li][None], win, ln_v_g[li][None],
                              ln_v_b[li][None], w_spatial[li].astype(bf16), bs, gout[None, :D_A])
        h, m_rows, route_e, route_w = _attn_out(
            sink[li].astype(f32) * LOG2E, hcur, ya.reshape(b, s, D_A), q.reshape(b, s, D_B),
            k.reshape(b, s, BLK), v.reshape(b, s, BLK), bias, gout_b, wout,
            g_ffn[li][None], wr, br)

        r = R_EXP
        n_sort = t // TL_SORT
        max_rows = 2 * t + n_sort * N_EXPERTS * (RUN_ALIGN - 1)
        n_tiles = -(-max_rows // r) + N_EXPERTS
        assert n_tiles * r < 2 ** 16, "route_pos prefix sums hold row offsets in two bf16 bytes"
        slot, run_tiles, plan = _route_pos(route_e, r)
        xs, slot_t = _sort_rows(plan, run_tiles, m_rows, slot, route_w, n_tiles * r, r)
        ys = _experts(plan, xs, w_gate_e[li], w_up_e[li], w_down_e[li], r)

        out = _combine_out(run_tiles, h.reshape(t, d), slot_t,
                           p[li].reshape(t, PLE_DIM), ys, w_ple_gate[li].astype(bf16),
                           b_ple_gate[li][None], w_ple_proj[li].astype(bf16), g_ple[li][None],
                           g_final[None])
        hcur = out.reshape(b, s, d)
    return hcur
```
